```python
import jax, jax.numpy as jnp
from jax import lax
import numpy as np

D_MODEL = 1024
BATCH = 8
SEQ = 8192
DEPTH = 4

HEAD_DIM = 64
N_Q_HEADS = D_MODEL // (2 * HEAD_DIM)
N_KV_HEADS = max(1, N_Q_HEADS // 4)
GQA_GROUP = N_Q_HEADS // N_KV_HEADS
WINDOW = 128
ATTN_BLOCK = 128
GM_HEADS = N_Q_HEADS
GM_HEAD_DIM = HEAD_DIM
CHUNK = 128
ATTN_W = N_Q_HEADS * HEAD_DIM
KV_W = N_KV_HEADS * HEAD_DIM
GM_W = GM_HEADS * GM_HEAD_DIM
D_MIX = ATTN_W + GM_W
D_IN = ATTN_W + 2 * KV_W + 2 * GM_W
D_FF = ((8 * D_MODEL + 3 * 256 - 1) // (3 * 256)) * 256
PLE_DIM = 256
NORM_EPS = 1e-6
NEG_BIG = -1e30

kernel_name = "hymba_swa_gmlp_hybrid"


def rmsnorm(x, g):
    xf = x.astype(jnp.float32)
    y = xf * lax.rsqrt(jnp.mean(xf * xf, axis=-1, keepdims=True) + NORM_EPS)
    return (y * g.astype(jnp.float32)).astype(x.dtype)


def layernorm(x, g, b):
    xf = x.astype(jnp.float32)
    mu = jnp.mean(xf, axis=-1, keepdims=True)
    xc = xf - mu
    y = xc * lax.rsqrt(jnp.mean(xc * xc, axis=-1, keepdims=True) + NORM_EPS)
    return (y * g.astype(jnp.float32) + b.astype(jnp.float32)).astype(x.dtype)


def alibi_slopes(n_heads):
    return jnp.exp2(-8.0 * (jnp.arange(n_heads, dtype=jnp.float32) + 1.0) / n_heads)


def sliding_window_attention(q, k, v, sinks):
    B, S = q.shape[0], q.shape[1]
    nb = S // ATTN_BLOCK
    qb = q.reshape(B, nb, ATTN_BLOCK, N_KV_HEADS, GQA_GROUP, HEAD_DIM)
    kb = k.reshape(B, nb, ATTN_BLOCK, N_KV_HEADS, HEAD_DIM)
    vb = v.reshape(B, nb, ATTN_BLOCK, N_KV_HEADS, HEAD_DIM)
    pad = ((0, 0), (1, 0), (0, 0), (0, 0), (0, 0))
    kk = jnp.concatenate([jnp.pad(kb[:, :-1], pad), kb], axis=2)
    vv = jnp.concatenate([jnp.pad(vb[:, :-1], pad), vb], axis=2)
    scale = HEAD_DIM ** -0.5
    s = jnp.einsum('bnqkgd,bnskd->bnkgqs', qb, kk,
                   preferred_element_type=jnp.float32) * scale
    qi = jnp.arange(ATTN_BLOCK)[:, None]
    kj = jnp.arange(2 * ATTN_BLOCK)[None, :]
    dist = qi + ATTN_BLOCK - kj
    band = (dist >= 0) & (dist < WINDOW)
    blk = jnp.arange(nb)[:, None, None]
    valid = band[None] & ((blk > 0) | (kj >= ATTN_BLOCK)[None])
    slopes = alibi_slopes(N_Q_HEADS).reshape(N_KV_HEADS, GQA_GROUP)
    bias = -slopes[:, :, None, None] * dist.astype(jnp.float32)
    s = jnp.where(valid[None, :, None, None], s + bias[None, None], NEG_BIG)
    sink = sinks.astype(jnp.float32).reshape(N_KV_HEADS, GQA_GROUP)[None, None, :, :, None, None]
    m = jnp.maximum(jnp.max(s, axis=-1, keepdims=True), sink)
    e = jnp.exp(s - m)
    pr = e / (jnp.sum(e, axis=-1, keepdims=True) + jnp.exp(sink - m))
    o = jnp.einsum('bnkgqs,bnskd->bnqkgd', pr.astype(vv.dtype), vv)
    return o.reshape(B, S, ATTN_W)


def chunked_spatial_gating(zu, zv, ln_g, ln_b, ws, bs):
    B, S = zu.shape[0], zu.shape[1]
    nc = S // CHUNK
    zv = layernorm(zv, ln_g, ln_b)
    vh = zv.reshape(B, nc, CHUNK, GM_HEADS, GM_HEAD_DIM)
    causal = jnp.tril(jnp.ones((CHUNK, CHUNK), dtype=bool))
    w = jnp.where(causal[None], ws, jnp.zeros_like(ws))
    mixed = jnp.einsum('hts,bnshc->bnthc', w, vh) + bs.T[None, None, :, :, None]
    return zu * mixed.reshape(B, S, GM_W)


def _fwd_setup_inputs(seed: int = 0) -> dict:
    key = jax.random.key(seed)
    ks = jax.random.split(key, 24)
    f32 = jnp.float32
    nrm = lambda k, shape, s: jax.random.normal(k, shape, f32) * s
    gain = lambda k, n: 1.0 + 0.05 * jax.random.normal(k, (DEPTH, n), f32)
    return {
        "x": nrm(ks[0], (BATCH, SEQ, D_MODEL), 1.0),
        "p": nrm(ks[1], (DEPTH, BATCH, SEQ, PLE_DIM), 1.0),
        "ln_mix_pre": gain(ks[2], D_MODEL),
        "w_in": nrm(ks[3], (DEPTH, D_MODEL, D_IN), D_MODEL ** -0.5),
        "attn_sinks": nrm(ks[4], (DEPTH, N_Q_HEADS), 1.0),
        "gm_ln_g": gain(ks[5], GM_W),
        "gm_ln_b": nrm(ks[6], (DEPTH, GM_W), 0.01),
        "gm_ws": nrm(ks[7], (DEPTH, GM_HEADS, CHUNK, CHUNK), CHUNK ** -0.5),
        "gm_bs": 1.0 + nrm(ks[8], (DEPTH, GM_HEADS, CHUNK), 0.01),
        "g_attn_out": gain(ks[9], ATTN_W),
        "g_gm_out": gain(ks[10], GM_W),
        "w_out": nrm(ks[11], (DEPTH, D_MIX, D_MODEL), D_MIX ** -0.5),
        "ln_mix_post": gain(ks[12], D_MODEL),
        "ln_ffn_pre": gain(ks[13], D_MODEL),
        "w_ffn_gate": nrm(ks[14], (DEPTH, D_MODEL, D_FF), D_MODEL ** -0.5),
        "w_ffn_up": nrm(ks[15], (DEPTH, D_MODEL, D_FF), D_MODEL ** -0.5),
        "w_ffn_down": nrm(ks[16], (DEPTH, D_FF, D_MODEL), D_FF ** -0.5),
        "ln_ffn_post": gain(ks[17], D_MODEL),
        "w_ple": nrm(ks[18], (DEPTH, PLE_DIM, D_MODEL), PLE_DIM ** -0.5),
        "ln_ple_gate": gain(ks[19], D_MODEL),
        "w_ple_gate": nrm(ks[20], (DEPTH, D_MODEL, D_MODEL), D_MODEL ** -0.5),
    }


def _fwd_reference(x, p, ln_mix_pre, w_in, attn_sinks, gm_ln_g, gm_ln_b, gm_ws, gm_bs,
              g_attn_out, g_gm_out, w_out, ln_mix_post, ln_ffn_pre, w_ffn_gate,
              w_ffn_up, w_ffn_down, ln_ffn_post, w_ple, ln_ple_gate, w_ple_gate):
    h = x
    splits = [ATTN_W, ATTN_W + KV_W, ATTN_W + 2 * KV_W, ATTN_W + 2 * KV_W + GM_W]
    for i in range(DEPTH):
        a = rmsnorm(h, ln_mix_pre[i])
        z = a @ w_in[i]
        q, k, v, zu, zv = jnp.split(z, splits, axis=-1)
        attn = sliding_window_attention(q, k, v, attn_sinks[i])
        gm = chunked_spatial_gating(jax.nn.gelu(zu), jax.nn.gelu(zv),
                                    gm_ln_g[i], gm_ln_b[i], gm_ws[i], gm_bs[i])
        heads = jnp.concatenate([rmsnorm(attn, g_attn_out[i]),
                                 rmsnorm(gm, g_gm_out[i])], axis=-1)
        h = h + rmsnorm(heads @ w_out[i], ln_mix_post[i])
        f = rmsnorm(h, ln_ffn_pre[i])
        f = (jax.nn.silu(f @ w_ffn_gate[i]) * (f @ w_ffn_up[i])) @ w_ffn_down[i]
        h = h + rmsnorm(f, ln_ffn_post[i])
        gate = jax.nn.sigmoid(rmsnorm(h, ln_ple_gate[i]) @ w_ple_gate[i])
        h = h + (p[i] @ w_ple[i]) * gate
    return h


import jax as _jax
import jax.numpy as _jnp

TWIN_FORMAT = 'train_step'
FWD_PARAMS = ['x', 'p', 'ln_mix_pre', 'w_in', 'attn_sinks', 'gm_ln_g', 'gm_ln_b', 'gm_ws', 'gm_bs', 'g_attn_out', 'g_gm_out', 'w_out', 'ln_mix_post', 'ln_ffn_pre', 'w_ffn_gate', 'w_ffn_up', 'w_ffn_down', 'ln_ffn_post', 'w_ple', 'ln_ple_gate', 'w_ple_gate']
TWIN_WEIGHTS = ['ln_mix_pre', 'w_in', 'attn_sinks', 'gm_ln_g', 'gm_ln_b', 'gm_ws', 'gm_bs', 'g_attn_out', 'g_gm_out', 'w_out', 'ln_mix_post', 'ln_ffn_pre', 'w_ffn_gate', 'w_ffn_up', 'w_ffn_down', 'ln_ffn_post', 'w_ple', 'ln_ple_gate', 'w_ple_gate']
TWIN_DIFF_INPUT = 'x'
TWIN_INPUTS = ['x', 'p', 'ln_mix_pre', 'w_in', 'attn_sinks', 'gm_ln_g', 'gm_ln_b', 'gm_ws', 'gm_bs', 'g_attn_out', 'g_gm_out', 'w_out', 'ln_mix_post', 'ln_ffn_pre', 'w_ffn_gate', 'w_ffn_up', 'w_ffn_down', 'ln_ffn_post', 'w_ple', 'ln_ple_gate', 'w_ple_gate', 'loss_target', 'm_ln_mix_pre', 'm_w_in', 'm_attn_sinks', 'm_gm_ln_g', 'm_gm_ln_b', 'm_gm_ws', 'm_gm_bs', 'm_g_attn_out', 'm_g_gm_out', 'm_w_out', 'm_ln_mix_post', 'm_ln_ffn_pre', 'm_w_ffn_gate', 'm_w_ffn_up', 'm_w_ffn_down', 'm_ln_ffn_post', 'm_w_ple', 'm_ln_ple_gate', 'm_w_ple_gate', 'v_ln_mix_pre', 'v_w_in', 'v_attn_sinks', 'v_gm_ln_g', 'v_gm_ln_b', 'v_gm_ws', 'v_gm_bs', 'v_g_attn_out', 'v_g_gm_out', 'v_w_out', 'v_ln_mix_post', 'v_ln_ffn_pre', 'v_w_ffn_gate', 'v_w_ffn_up', 'v_w_ffn_down', 'v_ln_ffn_post', 'v_w_ple', 'v_ln_ple_gate', 'v_w_ple_gate']
TWIN_OUTPUTS = ['loss', 'grad_x', 'grad_ln_mix_pre', 'grad_w_in', 'grad_attn_sinks', 'grad_gm_ln_g', 'grad_gm_ln_b', 'grad_gm_ws', 'grad_gm_bs', 'grad_g_attn_out', 'grad_g_gm_out', 'grad_w_out', 'grad_ln_mix_post', 'grad_ln_ffn_pre', 'grad_w_ffn_gate', 'grad_w_ffn_up', 'grad_w_ffn_down', 'grad_ln_ffn_post', 'grad_w_ple', 'grad_ln_ple_gate', 'grad_w_ple_gate', 'delta_ln_mix_pre', 'delta_w_in', 'delta_attn_sinks', 'delta_gm_ln_g', 'delta_gm_ln_b', 'delta_gm_ws', 'delta_gm_bs', 'delta_g_attn_out', 'delta_g_gm_out', 'delta_w_out', 'delta_ln_mix_post', 'delta_ln_ffn_pre', 'delta_w_ffn_gate', 'delta_w_ffn_up', 'delta_w_ffn_down', 'delta_ln_ffn_post', 'delta_w_ple', 'delta_ln_ple_gate', 'delta_w_ple_gate', 'new_m_ln_mix_pre', 'new_m_w_in', 'new_m_attn_sinks', 'new_m_gm_ln_g', 'new_m_gm_ln_b', 'new_m_gm_ws', 'new_m_gm_bs', 'new_m_g_attn_out', 'new_m_g_gm_out', 'new_m_w_out', 'new_m_ln_mix_post', 'new_m_ln_ffn_pre', 'new_m_w_ffn_gate', 'new_m_w_ffn_up', 'new_m_w_ffn_down', 'new_m_ln_ffn_post', 'new_m_w_ple', 'new_m_ln_ple_gate', 'new_m_w_ple_gate', 'new_v_ln_mix_pre', 'new_v_w_in', 'new_v_attn_sinks', 'new_v_gm_ln_g', 'new_v_gm_ln_b', 'new_v_gm_ws', 'new_v_gm_bs', 'new_v_g_attn_out', 'new_v_g_gm_out', 'new_v_w_out', 'new_v_ln_mix_post', 'new_v_ln_ffn_pre', 'new_v_w_ffn_gate', 'new_v_w_ffn_up', 'new_v_w_ffn_down', 'new_v_ln_ffn_post', 'new_v_w_ple', 'new_v_ln_ple_gate', 'new_v_w_ple_gate']
TWIN_LEAF_KINDS = {'loss': 'loss', 'grad_x': 'grad_x', 'grad_ln_mix_pre': 'grad_w', 'grad_w_in': 'grad_w', 'grad_attn_sinks': 'grad_w', 'grad_gm_ln_g': 'grad_w', 'grad_gm_ln_b': 'grad_w', 'grad_gm_ws': 'grad_w', 'grad_gm_bs': 'grad_w', 'grad_g_attn_out': 'grad_w', 'grad_g_gm_out': 'grad_w', 'grad_w_out': 'grad_w', 'grad_ln_mix_post': 'grad_w', 'grad_ln_ffn_pre': 'grad_w', 'grad_w_ffn_gate': 'grad_w', 'grad_w_ffn_up': 'grad_w', 'grad_w_ffn_down': 'grad_w', 'grad_ln_ffn_post': 'grad_w', 'grad_w_ple': 'grad_w', 'grad_ln_ple_gate': 'grad_w', 'grad_w_ple_gate': 'grad_w', 'delta_ln_mix_pre': 'delta_w', 'delta_w_in': 'delta_w', 'delta_attn_sinks': 'delta_w', 'delta_gm_ln_g': 'delta_w', 'delta_gm_ln_b': 'delta_w', 'delta_gm_ws': 'delta_w', 'delta_gm_bs': 'delta_w', 'delta_g_attn_out': 'delta_w', 'delta_g_gm_out': 'delta_w', 'delta_w_out': 'delta_w', 'delta_ln_mix_post': 'delta_w', 'delta_ln_ffn_pre': 'delta_w', 'delta_w_ffn_gate': 'delta_w', 'delta_w_ffn_up': 'delta_w', 'delta_w_ffn_down': 'delta_w', 'delta_ln_ffn_post': 'delta_w', 'delta_w_ple': 'delta_w', 'delta_ln_ple_gate': 'delta_w', 'delta_w_ple_gate': 'delta_w', 'new_m_ln_mix_pre': 'new_m', 'new_m_w_in': 'new_m', 'new_m_attn_sinks': 'new_m', 'new_m_gm_ln_g': 'new_m', 'new_m_gm_ln_b': 'new_m', 'new_m_gm_ws': 'new_m', 'new_m_gm_bs': 'new_m', 'new_m_g_attn_out': 'new_m', 'new_m_g_gm_out': 'new_m', 'new_m_w_out': 'new_m', 'new_m_ln_mix_post': 'new_m', 'new_m_ln_ffn_pre': 'new_m', 'new_m_w_ffn_gate': 'new_m', 'new_m_w_ffn_up': 'new_m', 'new_m_w_ffn_down': 'new_m', 'new_m_ln_ffn_post': 'new_m', 'new_m_w_ple': 'new_m', 'new_m_ln_ple_gate': 'new_m', 'new_m_w_ple_gate': 'new_m', 'new_v_ln_mix_pre': 'new_v', 'new_v_w_in': 'new_v', 'new_v_attn_sinks': 'new_v', 'new_v_gm_ln_g': 'new_v', 'new_v_gm_ln_b': 'new_v', 'new_v_gm_ws': 'new_v', 'new_v_gm_bs': 'new_v', 'new_v_g_attn_out': 'new_v', 'new_v_g_gm_out': 'new_v', 'new_v_w_out': 'new_v', 'new_v_ln_mix_post': 'new_v', 'new_v_ln_ffn_pre': 'new_v', 'new_v_w_ffn_gate': 'new_v', 'new_v_w_ffn_up': 'new_v', 'new_v_w_ffn_down': 'new_v', 'new_v_ln_ffn_post': 'new_v', 'new_v_w_ple': 'new_v', 'new_v_ln_ple_gate': 'new_v', 'new_v_w_ple_gate': 'new_v'}


def _forward(args):
    return _fwd_reference(*[args[k] for k in FWD_PARAMS])


def _output_shape():
    def fwd():
        inp = _fwd_setup_inputs(0)
        return _fwd_reference(*[inp[k] for k in FWD_PARAMS])
    out = _jax.eval_shape(fwd)
    return out.shape, out.dtype

N_MICROBATCH = 1
ADAM_LR = 0.001
ADAM_B1 = 0.9
ADAM_B2 = 0.999
ADAM_EPS = 1e-08
ADAM_WD = 0.01
ADAM_STEP = 10
PER_EXAMPLE_BATCH_AXIS = {'x': 0, 'p': 1, 'loss_target': 0}
SHARED_INPUTS = []
_WEIGHT_DTYPES = {'ln_mix_pre': _jnp.float32, 'w_in': _jnp.float32, 'attn_sinks': _jnp.float32, 'gm_ln_g': _jnp.float32, 'gm_ln_b': _jnp.float32, 'gm_ws': _jnp.float32, 'gm_bs': _jnp.float32, 'g_attn_out': _jnp.float32, 'g_gm_out': _jnp.float32, 'w_out': _jnp.float32, 'ln_mix_post': _jnp.float32, 'ln_ffn_pre': _jnp.float32, 'w_ffn_gate': _jnp.float32, 'w_ffn_up': _jnp.float32, 'w_ffn_down': _jnp.float32, 'ln_ffn_post': _jnp.float32, 'w_ple': _jnp.float32, 'ln_ple_gate': _jnp.float32, 'w_ple_gate': _jnp.float32}
MOMENT_SCALE = {'ln_mix_pre': 1.628246e+01, 'w_in': 1.198130e+01, 'attn_sinks': 1.452848e+01, 'gm_ln_g': 1.133564e+00, 'gm_ln_b': 1.391945e+00, 'gm_ws': 6.832383e-01, 'gm_bs': 1.220182e+00, 'g_attn_out': 2.520702e+01, 'g_gm_out': 2.801200e+01, 'w_out': 2.612240e+01, 'ln_mix_post': 7.044127e+01, 'ln_ffn_pre': 7.346773e+00, 'w_ffn_gate': 2.000904e+00, 'w_ffn_up': 3.739752e+00, 'w_ffn_down': 6.329390e+00, 'ln_ffn_post': 6.439445e+01, 'w_ple': 1.047671e+00, 'ln_ple_gate': 2.197045e+00, 'w_ple_gate': 1.135756e+00}


def _to_microbatches(a, axis):
    t = _jnp.moveaxis(a, axis, 0)
    t = t.reshape((N_MICROBATCH, t.shape[0] // N_MICROBATCH) + t.shape[1:])
    return _jnp.moveaxis(t, 1, axis + 1)


def setup_inputs(seed: int = 0) -> dict:
    inp = _fwd_setup_inputs(seed)
    key = _jax.random.fold_in(_jax.random.key(seed), 7919)
    shape, _ = _output_shape()
    out = dict(inp)
    out["loss_target"] = _jax.random.normal(_jax.random.fold_in(key, 0), shape, _jnp.float32)
    for i, name in enumerate(TWIN_WEIGHTS):
        w = inp[name].astype(_jnp.float32)
        if MOMENT_SCALE is None:
            s = _jnp.sqrt(_jnp.mean(_jnp.square(w)) + 1e-30)
        else:
            s = MOMENT_SCALE[name]
        km, kv = _jax.random.split(_jax.random.fold_in(key, i + 1))
        out[name] = w
        out["m_" + name] = s * _jax.random.normal(km, w.shape, _jnp.float32)
        out["v_" + name] = (s * s) * _jax.random.uniform(kv, w.shape, _jnp.float32, 0.5, 1.5)
    if N_MICROBATCH > 1:
        for name, axis in PER_EXAMPLE_BATCH_AXIS.items():
            out[name] = _to_microbatches(out[name], axis)
    return {'x': out['x'], 'p': out['p'], 'ln_mix_pre': out['ln_mix_pre'], 'w_in': out['w_in'], 'attn_sinks': out['attn_sinks'], 'gm_ln_g': out['gm_ln_g'], 'gm_ln_b': out['gm_ln_b'], 'gm_ws': out['gm_ws'], 'gm_bs': out['gm_bs'], 'g_attn_out': out['g_attn_out'], 'g_gm_out': out['g_gm_out'], 'w_out': out['w_out'], 'ln_mix_post': out['ln_mix_post'], 'ln_ffn_pre': out['ln_ffn_pre'], 'w_ffn_gate': out['w_ffn_gate'], 'w_ffn_up': out['w_ffn_up'], 'w_ffn_down': out['w_ffn_down'], 'ln_ffn_post': out['ln_ffn_post'], 'w_ple': out['w_ple'], 'ln_ple_gate': out['ln_ple_gate'], 'w_ple_gate': out['w_ple_gate'], 'loss_target': out['loss_target'], 'm_ln_mix_pre': out['m_ln_mix_pre'], 'm_w_in': out['m_w_in'], 'm_attn_sinks': out['m_attn_sinks'], 'm_gm_ln_g': out['m_gm_ln_g'], 'm_gm_ln_b': out['m_gm_ln_b'], 'm_gm_ws': out['m_gm_ws'], 'm_gm_bs': out['m_gm_bs'], 'm_g_attn_out': out['m_g_attn_out'], 'm_g_gm_out': out['m_g_gm_out'], 'm_w_out': out['m_w_out'], 'm_ln_mix_post': out['m_ln_mix_post'], 'm_ln_ffn_pre': out['m_ln_ffn_pre'], 'm_w_ffn_gate': out['m_w_ffn_gate'], 'm_w_ffn_up': out['m_w_ffn_up'], 'm_w_ffn_down': out['m_w_ffn_down'], 'm_ln_ffn_post': out['m_ln_ffn_post'], 'm_w_ple': out['m_w_ple'], 'm_ln_ple_gate': out['m_ln_ple_gate'], 'm_w_ple_gate': out['m_w_ple_gate'], 'v_ln_mix_pre': out['v_ln_mix_pre'], 'v_w_in': out['v_w_in'], 'v_attn_sinks': out['v_attn_sinks'], 'v_gm_ln_g': out['v_gm_ln_g'], 'v_gm_ln_b': out['v_gm_ln_b'], 'v_gm_ws': out['v_gm_ws'], 'v_gm_bs': out['v_gm_bs'], 'v_g_attn_out': out['v_g_attn_out'], 'v_g_gm_out': out['v_g_gm_out'], 'v_w_out': out['v_w_out'], 'v_ln_mix_post': out['v_ln_mix_post'], 'v_ln_ffn_pre': out['v_ln_ffn_pre'], 'v_w_ffn_gate': out['v_w_ffn_gate'], 'v_w_ffn_up': out['v_w_ffn_up'], 'v_w_ffn_down': out['v_w_ffn_down'], 'v_ln_ffn_post': out['v_ln_ffn_post'], 'v_w_ple': out['v_w_ple'], 'v_ln_ple_gate': out['v_ln_ple_gate'], 'v_w_ple_gate': out['v_w_ple_gate']}


def _loss(weights, diff, rest, loss_target):
    with _jax.named_scope("forward"):
        args = {**rest, TWIN_DIFF_INPUT: diff, **{k: w.astype(_WEIGHT_DTYPES[k]) for k, w in weights.items()}}
        y = _forward(args)
    with _jax.named_scope("loss_head"):
        err = _jnp.square(y.astype(_jnp.float32) - loss_target)
        return 0.5 * _jnp.sum(_jnp.mean(err, axis=-1)) if err.ndim else 0.5 * err


def _adamw(w, g, m, v):
    m = ADAM_B1 * m + (1.0 - ADAM_B1) * g
    v = ADAM_B2 * v + (1.0 - ADAM_B2) * _jnp.square(g)
    m_hat = m / (1.0 - ADAM_B1 ** ADAM_STEP)
    v_hat = v / (1.0 - ADAM_B2 ** ADAM_STEP)
    delta = -ADAM_LR * (m_hat / (_jnp.sqrt(v_hat) + ADAM_EPS) + ADAM_WD * w)
    return delta, m, v


def reference(x, p, ln_mix_pre, w_in, attn_sinks, gm_ln_g, gm_ln_b, gm_ws, gm_bs, g_attn_out, g_gm_out, w_out, ln_mix_post, ln_ffn_pre, w_ffn_gate, w_ffn_up, w_ffn_down, ln_ffn_post, w_ple, ln_ple_gate, w_ple_gate, loss_target, m_ln_mix_pre, m_w_in, m_attn_sinks, m_gm_ln_g, m_gm_ln_b, m_gm_ws, m_gm_bs, m_g_attn_out, m_g_gm_out, m_w_out, m_ln_mix_post, m_ln_ffn_pre, m_w_ffn_gate, m_w_ffn_up, m_w_ffn_down, m_ln_ffn_post, m_w_ple, m_ln_ple_gate, m_w_ple_gate, v_ln_mix_pre, v_w_in, v_attn_sinks, v_gm_ln_g, v_gm_ln_b, v_gm_ws, v_gm_bs, v_g_attn_out, v_g_gm_out, v_w_out, v_ln_mix_post, v_ln_ffn_pre, v_w_ffn_gate, v_w_ffn_up, v_w_ffn_down, v_ln_ffn_post, v_w_ple, v_ln_ple_gate, v_w_ple_gate):
    given = dict(x=x, p=p, ln_mix_pre=ln_mix_pre, w_in=w_in, attn_sinks=attn_sinks, gm_ln_g=gm_ln_g, gm_ln_b=gm_ln_b, gm_ws=gm_ws, gm_bs=gm_bs, g_attn_out=g_attn_out, g_gm_out=g_gm_out, w_out=w_out, ln_mix_post=ln_mix_post, ln_ffn_pre=ln_ffn_pre, w_ffn_gate=w_ffn_gate, w_ffn_up=w_ffn_up, w_ffn_down=w_ffn_down, ln_ffn_post=ln_ffn_post, w_ple=w_ple, ln_ple_gate=ln_ple_gate, w_ple_gate=w_ple_gate, loss_target=loss_target, m_ln_mix_pre=m_ln_mix_pre, m_w_in=m_w_in, m_attn_sinks=m_attn_sinks, m_gm_ln_g=m_gm_ln_g, m_gm_ln_b=m_gm_ln_b, m_gm_ws=m_gm_ws, m_gm_bs=m_gm_bs, m_g_attn_out=m_g_attn_out, m_g_gm_out=m_g_gm_out, m_w_out=m_w_out, m_ln_mix_post=m_ln_mix_post, m_ln_ffn_pre=m_ln_ffn_pre, m_w_ffn_gate=m_w_ffn_gate, m_w_ffn_up=m_w_ffn_up, m_w_ffn_down=m_w_ffn_down, m_ln_ffn_post=m_ln_ffn_post, m_w_ple=m_w_ple, m_ln_ple_gate=m_ln_ple_gate, m_w_ple_gate=m_w_ple_gate, v_ln_mix_pre=v_ln_mix_pre, v_w_in=v_w_in, v_attn_sinks=v_attn_sinks, v_gm_ln_g=v_gm_ln_g, v_gm_ln_b=v_gm_ln_b, v_gm_ws=v_gm_ws, v_gm_bs=v_gm_bs, v_g_attn_out=v_g_attn_out, v_g_gm_out=v_g_gm_out, v_w_out=v_w_out, v_ln_mix_post=v_ln_mix_post, v_ln_ffn_pre=v_ln_ffn_pre, v_w_ffn_gate=v_w_ffn_gate, v_w_ffn_up=v_w_ffn_up, v_w_ffn_down=v_w_ffn_down, v_ln_ffn_post=v_ln_ffn_post, v_w_ple=v_w_ple, v_ln_ple_gate=v_ln_ple_gate, v_w_ple_gate=v_w_ple_gate)
    weights = {n: given[n] for n in TWIN_WEIGHTS}
    shared = {n: given[n] for n in SHARED_INPUTS}
    per_example = {n: given[n] for n in ['x', 'p']}
    grad_fn = _jax.value_and_grad(_loss, argnums=(0, 1))

    def one_microbatch(ex, loss_target):
        ex = dict(ex)
        diff = ex.pop(TWIN_DIFF_INPUT)
        return grad_fn(weights, diff, {**shared, **ex}, loss_target)

    if N_MICROBATCH == 1:
        loss, (grad_w, grad_x) = one_microbatch(per_example, given["loss_target"])
    else:
        def body(carry, xs):
            loss_sum, grad_sum = carry
            l_k, (gw_k, gx_k) = one_microbatch(xs[0], xs[1])
            with _jax.named_scope("update"):
                return (loss_sum + l_k, _jax.tree.map(_jnp.add, grad_sum, gw_k)), gx_k

        init = (_jnp.zeros((), _jnp.float32), _jax.tree.map(_jnp.zeros_like, weights))
        (loss, grad_w), grad_x = _jax.lax.scan(body, init, (per_example, given["loss_target"]))
    with _jax.named_scope("update"):
        delta_w, new_m, new_v = {}, {}, {}
        for n in TWIN_WEIGHTS:
            delta_w[n], new_m[n], new_v[n] = _adamw(weights[n], grad_w[n], given["m_" + n], given["v_" + n])
    return (loss, grad_x, *[grad_w[n] for n in TWIN_WEIGHTS], *[delta_w[n] for n in TWIN_WEIGHTS],
            *[new_m[n] for n in TWIN_WEIGHTS], *[new_v[n] for n in TWIN_WEIGHTS])
```

```python
import math

import jax
import jax.numpy as jnp
from jax import lax
from jax.experimental import pallas as pl
from jax.experimental.pallas import tpu as pltpu

F32 = jnp.float32
BF16 = jnp.bfloat16
MESH = pl.DeviceIdType.MESH
AXES = ("x", "y", "c")

D_MODEL = 1024
DEPTH = 4
N_DEV = 8
HEAD_DIM = 64
ATTN_W = 512
KV_W = 128
GM_W = 512
D_IN = 1792
D_FF = 2816
PLE_DIM = 256
BLK = 128
NORM_EPS = 1e-6
NEG_BIG = -1e30
ATTN_SCALE = HEAD_DIM ** -0.5

ADAM_LR = 0.001
ADAM_B1 = 0.9
ADAM_B2 = 0.999
ADAM_EPS = 1e-08
ADAM_WD = 0.01
ADAM_STEP = 10

ROWS_A = D_FF // N_DEV
ROWS_B = D_MODEL // N_DEV
ROWS_C = D_IN // N_DEV
SMALL_ROWS = 144

VMEM_LIMIT = 56 * 2 ** 20


def _params(*sem):
    return pltpu.CompilerParams(dimension_semantics=sem, vmem_limit_bytes=VMEM_LIMIT)


def _dot(a, b):
    return jnp.dot(a, b, preferred_element_type=F32)


def _dot_nt(a, b):
    return lax.dot_general(a, b, (((1,), (1,)), ((), ())), preferred_element_type=F32)


def _dot_tn(a, b):
    return lax.dot_general(a, b, (((0,), (0,)), ((), ())), preferred_element_type=F32)


def _rms_fwd(x, g):
    r = lax.rsqrt(jnp.mean(x * x, axis=-1, keepdims=True) + NORM_EPS)
    return x * r * g


def _rms_bwd(x, g, dy):
    r = lax.rsqrt(jnp.mean(x * x, axis=-1, keepdims=True) + NORM_EPS)
    xh = x * r
    dg = jnp.sum(dy * xh, axis=0, keepdims=True)
    dxh = dy * g
    dx = r * (dxh - xh * jnp.mean(dxh * xh, axis=-1, keepdims=True))
    return dx, dg


_GELU_C = math.sqrt(2.0 / math.pi)


def _gelu(x):
    t = jnp.tanh(_GELU_C * (x + 0.044715 * (x * x * x)))
    return 0.5 * x * (1.0 + t)


def _gelu_grad(x):
    x2 = x * x
    t = jnp.tanh(_GELU_C * (x + 0.044715 * (x2 * x)))
    return 0.5 * (1.0 + t) + 0.5 * x * (1.0 - t * t) * (_GELU_C * (1.0 + 3.0 * 0.044715 * x2))


def _sigmoid(x):
    return 1.0 / (1.0 + jnp.exp(-x))


def _row_spec(tm, n):
    return pl.BlockSpec((tm, n), lambda i: (i, 0))


def _vec_spec(n):
    return pl.BlockSpec((1, n), lambda i: (0, 0))


def _seg_spec(rows, cols, seg):
    return pl.BlockSpec((N_DEV * rows, cols), lambda i: (seg, 0))


def _zero_at(first, *refs):
    @pl.when(first)
    def _():
        for r in refs:
            r[...] = jnp.zeros(r.shape, r.dtype)


def _tile(t, want):
    return min(t, want)


def _in_proj(h, g, wc, layer):
    t = h.shape[0]
    tm = _tile(t, 512)

    def body(h_ref, g_ref, w_ref, a_ref, q_ref, kv_ref, zu_ref, zv_ref):
        a = _rms_fwd(h_ref[...], g_ref[...]).astype(BF16)
        a_ref[...] = a
        q_ref[...] = _dot_nt(a, w_ref[0:512, :]).astype(BF16)
        kv_ref[...] = _dot_nt(a, w_ref[512:768, :]).astype(BF16)
        zu_ref[...] = _dot_nt(a, w_ref[768:1280, :])
        zv_ref[...] = _dot_nt(a, w_ref[1280:1792, :])

    return pl.pallas_call(
        body, name="in_proj", grid=(t // tm,),
        in_specs=[_row_spec(tm, D_MODEL), _vec_spec(D_MODEL), _seg_spec(ROWS_C, D_MODEL, layer)],
        out_specs=[_row_spec(tm, D_MODEL), _row_spec(tm, ATTN_W), _row_spec(tm, 2 * KV_W),
                   _row_spec(tm, GM_W), _row_spec(tm, GM_W)],
        out_shape=[jax.ShapeDtypeStruct((t, D_MODEL), BF16), jax.ShapeDtypeStruct((t, ATTN_W), BF16),
                   jax.ShapeDtypeStruct((t, 2 * KV_W), BF16), jax.ShapeDtypeStruct((t, GM_W), F32),
                   jax.ShapeDtypeStruct((t, GM_W), F32)],
        compiler_params=_params("parallel"),
    )(h, g, wc)


def _head_variants(x, low):
    xr = pltpu.roll(x, 64, axis=1)
    zero = jnp.zeros_like(x)
    return {
        (0, 0): jnp.where(low, x, zero).astype(BF16),
        (0, 1): jnp.where(low, zero, xr).astype(BF16),
        (1, 0): jnp.where(low, xr, zero).astype(BF16),
        (1, 1): jnp.where(low, zero, x).astype(BF16),
    }


def _attn_masks(i):
    row = lax.broadcasted_iota(jnp.int32, (BLK, BLK), 0)
    lane = lax.broadcasted_iota(jnp.int32, (BLK, BLK), 1)
    dcur = (row - lane).astype(F32)
    dprev = dcur + float(BLK)
    vcur = row >= lane
    vprev = jnp.logical_and(lane > row, i > 0)
    return lane < 64, dcur, dprev, vcur, vprev


def _attn_probs(qh, kc, kp, slope, sink, dcur, dprev, vcur, vprev):
    sc = _dot_nt(qh, kc) * ATTN_SCALE - slope * dcur
    sp = _dot_nt(qh, kp) * ATTN_SCALE - slope * dprev
    sc = jnp.where(vcur, sc, NEG_BIG)
    sp = jnp.where(vprev, sp, NEG_BIG)
    m = jnp.maximum(jnp.maximum(jnp.max(sc, axis=1, keepdims=True), jnp.max(sp, axis=1, keepdims=True)), sink)
    ec = jnp.exp(sc - m)
    ep = jnp.exp(sp - m)
    es = jnp.exp(sink - m)
    inv = 1.0 / (jnp.sum(ec, axis=1, keepdims=True) + jnp.sum(ep, axis=1, keepdims=True) + es)
    return ec * inv, ep * inv, es * inv


def _kv_specs():
    cur = pl.BlockSpec((BLK, 2 * KV_W), lambda i: (i, 0))
    prev = pl.BlockSpec((BLK, 2 * KV_W), lambda i: (jnp.maximum(i - 1, 0), 0))
    return cur, prev


def _attn_fwd(q, kv, sinks):
    t = q.shape[0]
    cur, prev = _kv_specs()

    def body(sink_ref, q_ref, kvc_ref, kvp_ref, o_ref):
        i = pl.program_id(0)
        low, dcur, dprev, vcur, vprev = _attn_masks(i)
        kc = _head_variants(kvc_ref[:, 0:128].astype(F32), low)
        vc = _head_variants(kvc_ref[:, 128:256].astype(F32), low)
        kp = _head_variants(kvp_ref[:, 0:128].astype(F32), low)
        vp = _head_variants(kvp_ref[:, 128:256].astype(F32), low)
        for col in range(4):
            acc = None
            for half in range(2):
                h = 2 * col + half
                key = (h // 4, half)
                pc, pp, _ = _attn_probs(q_ref[:, col * 128:(col + 1) * 128], kc[key], kp[key],
                                        2.0 ** -(h + 1), sink_ref[h], dcur, dprev, vcur, vprev)
                o = _dot(pc.astype(BF16), vc[key]) + _dot(pp.astype(BF16), vp[key])
                acc = o if acc is None else acc + o
            o_ref[:, col * 128:(col + 1) * 128] = acc

    return pl.pallas_call(
        body, name="attn_fwd", grid=(t // BLK,),
        in_specs=[pl.BlockSpec(memory_space=pltpu.SMEM), _row_spec(BLK, ATTN_W), cur, prev],
        out_specs=_row_spec(BLK, ATTN_W),
        out_shape=jax.ShapeDtypeStruct((t, ATTN_W), F32),
        compiler_params=_params("parallel"),
    )(sinks, q, kv, kv)


def _gm_forward_block(zu, zv, lng, lnb, w_ref, bsx, low):
    gu = _gelu(zu)
    gv = _gelu(zv)
    mu = jnp.mean(gv, axis=-1, keepdims=True)
    xc = gv - mu
    rstd = lax.rsqrt(jnp.mean(xc * xc, axis=-1, keepdims=True) + NORM_EPS)
    xn = xc * rstd
    ln = xn * lng + lnb
    mixed = []
    for col in range(4):
        lc = ln[:, col * 128:(col + 1) * 128]
        lo = jnp.where(low, lc, 0.0).astype(BF16)
        hi = jnp.where(low, 0.0, lc).astype(BF16)
        mixed.append(_dot(w_ref[2 * col], lo) + _dot(w_ref[2 * col + 1], hi) + bsx[:, col * 128:(col + 1) * 128])
    return gu, ln, xn, rstd, mixed


def _gm_fwd(zu, zv, lng, lnb, wtril, bsx):
    t = zu.shape[0]

    def body(zu_ref, zv_ref, g_ref, b_ref, w_ref, bs_ref, o_ref):
        low = lax.broadcasted_iota(jnp.int32, (BLK, BLK), 1) < 64
        gu, _, _, _, mixed = _gm_forward_block(zu_ref[...], zv_ref[...], g_ref[...], b_ref[...], w_ref, bs_ref[...], low)
        for col in range(4):
            o_ref[:, col * 128:(col + 1) * 128] = gu[:, col * 128:(col + 1) * 128] * mixed[col]

    return pl.pallas_call(
        body, name="gm_fwd", grid=(t // BLK,),
        in_specs=[_row_spec(BLK, GM_W), _row_spec(BLK, GM_W), _vec_spec(GM_W), _vec_spec(GM_W),
                  pl.BlockSpec((8, BLK, BLK), lambda i: (0, 0, 0)), pl.BlockSpec((BLK, GM_W), lambda i: (0, 0))],
        out_specs=_row_spec(BLK, GM_W),
        out_shape=jax.ShapeDtypeStruct((t, GM_W), F32),
        compiler_params=_params("parallel"),
    )(zu, zv, lng, lnb, wtril, bsx)


def _out_proj(attn, gm, h, ga, gg, gpost, wb, layer):
    t = h.shape[0]
    tm = _tile(t, 512)

    def body(a_ref, m_ref, h_ref, ga_ref, gg_ref, gp_ref, w_ref, heads_ref, mix_ref, h1_ref):
        ha = _rms_fwd(a_ref[...], ga_ref[...]).astype(BF16)
        hg = _rms_fwd(m_ref[...], gg_ref[...]).astype(BF16)
        heads_ref[:, 0:512] = ha
        heads_ref[:, 512:1024] = hg
        mix = _dot(ha, w_ref[0:512, :]) + _dot(hg, w_ref[512:1024, :])
        mix_ref[...] = mix
        h1_ref[...] = h_ref[...] + _rms_fwd(mix, gp_ref[...])

    return pl.pallas_call(
        body, name="out_proj", grid=(t // tm,),
        in_specs=[_row_spec(tm, ATTN_W), _row_spec(tm, GM_W), _row_spec(tm, D_MODEL), _vec_spec(ATTN_W),
                  _vec_spec(GM_W), _vec_spec(D_MODEL), _seg_spec(ROWS_B, D_MODEL, 2 * layer)],
        out_specs=[_row_spec(tm, D_MODEL), _row_spec(tm, D_MODEL), _row_spec(tm, D_MODEL)],
        out_shape=[jax.ShapeDtypeStruct((t, D_MODEL), BF16), jax.ShapeDtypeStruct((t, D_MODEL), F32),
                   jax.ShapeDtypeStruct((t, D_MODEL), F32)],
        compiler_params=_params("parallel"),
    )(attn, gm, h, ga, gg, gpost, wb)


def _ffn_up(h1, g, wa, layer):
    t = h1.shape[0]
    tm = _tile(t, 256)

    def body(h_ref, g_ref, wg_ref, wu_ref, f_ref, gp_ref, up_ref, act_ref):
        f = _rms_fwd(h_ref[...], g_ref[...]).astype(BF16)
        f_ref[...] = f
        gp = _dot_nt(f, wg_ref[...])
        up = _dot_nt(f, wu_ref[...])
        gp_ref[...] = gp
        up_ref[...] = up
        act_ref[...] = (gp * _sigmoid(gp) * up).astype(BF16)

    return pl.pallas_call(
        body, name="ffn_up", grid=(t // tm,),
        in_specs=[_row_spec(tm, D_MODEL), _vec_spec(D_MODEL), _seg_spec(ROWS_A, D_MODEL, 3 * layer),
                  _seg_spec(ROWS_A, D_MODEL, 3 * layer + 1)],
        out_specs=[_row_spec(tm, D_MODEL), _row_spec(tm, D_FF), _row_spec(tm, D_FF), _row_spec(tm, D_FF)],
        out_shape=[jax.ShapeDtypeStruct((t, D_MODEL), BF16), jax.ShapeDtypeStruct((t, D_FF), F32),
                   jax.ShapeDtypeStruct((t, D_FF), F32), jax.ShapeDtypeStruct((t, D_FF), BF16)],
        compiler_params=_params("parallel"),
    )(h1, g, wa, wa)


def _ffn_down(act, h1, g, wa, layer):
    t = h1.shape[0]
    tm = _tile(t, 512)

    def body(a_ref, h_ref, g_ref, w_ref, fo_ref, h2_ref):
        fo = _dot(a_ref[...], w_ref[...])
        fo_ref[...] = fo
        h2_ref[...] = h_ref[...] + _rms_fwd(fo, g_ref[...])

    return pl.pallas_call(
        body, name="ffn_down", grid=(t // tm,),
        in_specs=[_row_spec(tm, D_FF), _row_spec(tm, D_MODEL), _vec_spec(D_MODEL),
                  _seg_spec(ROWS_A, D_MODEL, 3 * layer + 2)],
        out_specs=[_row_spec(tm, D_MODEL), _row_spec(tm, D_MODEL)],
        out_shape=[jax.ShapeDtypeStruct((t, D_MODEL), F32), jax.ShapeDtypeStruct((t, D_MODEL), F32)],
        compiler_params=_params("parallel"),
    )(act, h1, g, wa)


def _ple(h2, p, g, wb, wp, layer):
    t = h2.shape[0]
    tm = _tile(t, 512)

    def body(h_ref, p_ref, g_ref, wg_ref, wp_ref, hn_ref, gate_ref, pe_ref, h3_ref):
        h = h_ref[...]
        hn = _rms_fwd(h, g_ref[...]).astype(BF16)
        hn_ref[...] = hn
        gate = _sigmoid(_dot(hn, wg_ref[...]))
        pe = _dot_nt(p_ref[...].astype(BF16), wp_ref[...])
        gate_ref[...] = gate
        pe_ref[...] = pe
        h3_ref[...] = h + pe * gate

    return pl.pallas_call(
        body, name="ple_fwd", grid=(t // tm,),
        in_specs=[_row_spec(tm, D_MODEL), _row_spec(tm, PLE_DIM), _vec_spec(D_MODEL),
                  _seg_spec(ROWS_B, D_MODEL, 2 * layer + 1), _seg_spec(ROWS_B, PLE_DIM, layer)],
        out_specs=[_row_spec(tm, D_MODEL)] * 4,
        out_shape=[jax.ShapeDtypeStruct((t, D_MODEL), BF16)] + [jax.ShapeDtypeStruct((t, D_MODEL), F32)] * 3,
        compiler_params=_params("parallel"),
    )(h2, p, g, wb, wp)


def _loss_head(y, target):
    t = y.shape[0]
    tm = _tile(t, 512)

    def body(y_ref, t_ref, dy_ref, l_ref):
        _zero_at(pl.program_id(0) == 0, l_ref)
        e = y_ref[...] - t_ref[...]
        dy_ref[...] = e * (1.0 / D_MODEL)
        s = jnp.sum(jnp.sum(e * e, axis=1, keepdims=True), axis=0, keepdims=True)
        l_ref[...] += jnp.broadcast_to(s, (1, 128))

    return pl.pallas_call(
        body, name="loss_head", grid=(t // tm,),
        in_specs=[_row_spec(tm, D_MODEL), _row_spec(tm, D_MODEL)],
        out_specs=[_row_spec(tm, D_MODEL), _vec_spec(128)],
        out_shape=[jax.ShapeDtypeStruct((t, D_MODEL), F32), jax.ShapeDtypeStruct((1, 128), F32)],
        compiler_params=_params("arbitrary"),
    )(y, target)


def _ple_bwd(dh3, h2, gate, pe, g, wb, layer):
    t = h2.shape[0]
    tm = _tile(t, 512)

    def body(d_ref, h_ref, gate_ref, pe_ref, g_ref, w_ref, dh2_ref, dgl_ref, dpe_ref, dg_ref):
        _zero_at(pl.program_id(0) == 0, dg_ref)
        d = d_ref[...]
        gate = gate_ref[...]
        dpe_ref[...] = (d * gate).astype(BF16)
        dgl = (d * pe_ref[...] * gate * (1.0 - gate)).astype(BF16)
        dgl_ref[...] = dgl
        dhn = _dot_nt(dgl, w_ref[...])
        dx, dg = _rms_bwd(h_ref[...], g_ref[...], dhn)
        dh2_ref[...] = d + dx
        dg_ref[...] += dg

    return pl.pallas_call(
        body, name="ple_bwd", grid=(t // tm,),
        in_specs=[_row_spec(tm, D_MODEL)] * 4 + [_vec_spec(D_MODEL), _seg_spec(ROWS_B, D_MODEL, 2 * layer + 1)],
        out_specs=[_row_spec(tm, D_MODEL)] * 3 + [_vec_spec(D_MODEL)],
        out_shape=[jax.ShapeDtypeStruct((t, D_MODEL), F32), jax.ShapeDtypeStruct((t, D_MODEL), BF16),
                   jax.ShapeDtypeStruct((t, D_MODEL), BF16), jax.ShapeDtypeStruct((1, D_MODEL), F32)],
        compiler_params=_params("arbitrary"),
    )(dh3, h2, gate, pe, g, wb)


def _ffn_down_bwd(dh2, fo, gp, up, g, wa, layer):
    t = dh2.shape[0]
    tm = _tile(t, 256)

    def body(d_ref, fo_ref, gp_ref, up_ref, g_ref, w_ref, dfo_ref, dgp_ref, dup_ref, dg_ref):
        _zero_at(pl.program_id(0) == 0, dg_ref)
        dfo, dg = _rms_bwd(fo_ref[...], g_ref[...], d_ref[...])
        dfo = dfo.astype(BF16)
        dfo_ref[...] = dfo
        dact = _dot_nt(dfo, w_ref[...])
        gp = gp_ref[...]
        sg = _sigmoid(gp)
        dgp_ref[...] = (dact * up_ref[...] * (sg * (1.0 + gp * (1.0 - sg)))).astype(BF16)
        dup_ref[...] = (dact * (gp * sg)).astype(BF16)
        dg_ref[...] += dg

    return pl.pallas_call(
        body, name="ffn_down_bwd", grid=(t // tm,),
        in_specs=[_row_spec(tm, D_MODEL), _row_spec(tm, D_MODEL), _row_spec(tm, D_FF), _row_spec(tm, D_FF),
                  _vec_spec(D_MODEL), _seg_spec(ROWS_A, D_MODEL, 3 * layer + 2)],
        out_specs=[_row_spec(tm, D_MODEL), _row_spec(tm, D_FF), _row_spec(tm, D_FF), _vec_spec(D_MODEL)],
        out_shape=[jax.ShapeDtypeStruct((t, D_MODEL), BF16), jax.ShapeDtypeStruct((t, D_FF), BF16),
                   jax.ShapeDtypeStruct((t, D_FF), BF16), jax.ShapeDtypeStruct((1, D_MODEL), F32)],
        compiler_params=_params("arbitrary"),
    )(dh2, fo, gp, up, g, wa)


def _ffn_up_bwd(dgp, dup, h1, dh2, g, wa, layer):
    t = h1.shape[0]
    tm = _tile(t, 256)

    def body(dgp_ref, dup_ref, h_ref, d_ref, g_ref, wg_ref, wu_ref, dh1_ref, dg_ref):
        _zero_at(pl.program_id(0) == 0, dg_ref)
        df = _dot(dgp_ref[...], wg_ref[...]) + _dot(dup_ref[...], wu_ref[...])
        dx, dg = _rms_bwd(h_ref[...], g_ref[...], df)
        dh1_ref[...] = d_ref[...] + dx
        dg_ref[...] += dg

    return pl.pallas_call(
        body, name="ffn_up_bwd", grid=(t // tm,),
        in_specs=[_row_spec(tm, D_FF), _row_spec(tm, D_FF), _row_spec(tm, D_MODEL), _row_spec(tm, D_MODEL),
                  _vec_spec(D_MODEL), _seg_spec(ROWS_A, D_MODEL, 3 * layer), _seg_spec(ROWS_A, D_MODEL, 3 * layer + 1)],
        out_specs=[_row_spec(tm, D_MODEL), _vec_spec(D_MODEL)],
        out_shape=[jax.ShapeDtypeStruct((t, D_MODEL), F32), jax.ShapeDtypeStruct((1, D_MODEL), F32)],
        compiler_params=_params("arbitrary"),
    )(dgp, dup, h1, dh2, g, wa, wa)


def _out_proj_bwd(dh1, mix, attn, gm, gpost, ga, gg, wb, layer):
    t = dh1.shape[0]
    tm = _tile(t, 512)

    def body(d_ref, mix_ref, a_ref, m_ref, gp_ref, ga_ref, gg_ref, w_ref,
             dmix_ref, da_ref, dm_ref, dgp_ref, dga_ref, dgg_ref):
        _zero_at(pl.program_id(0) == 0, dgp_ref, dga_ref, dgg_ref)
        dmix, dgp = _rms_bwd(mix_ref[...], gp_ref[...], d_ref[...])
        dmix = dmix.astype(BF16)
        dmix_ref[...] = dmix
        da, dga = _rms_bwd(a_ref[...], ga_ref[...], _dot_nt(dmix, w_ref[0:512, :]))
        dm, dgg = _rms_bwd(m_ref[...], gg_ref[...], _dot_nt(dmix, w_ref[512:1024, :]))
        da_ref[...] = da.astype(BF16)
        dm_ref[...] = dm
        dgp_ref[...] += dgp
        dga_ref[...] += dga
        dgg_ref[...] += dgg

    return pl.pallas_call(
        body, name="out_proj_bwd", grid=(t // tm,),
        in_specs=[_row_spec(tm, D_MODEL), _row_spec(tm, D_MODEL), _row_spec(tm, ATTN_W), _row_spec(tm, GM_W),
                  _vec_spec(D_MODEL), _vec_spec(ATTN_W), _vec_spec(GM_W), _seg_spec(ROWS_B, D_MODEL, 2 * layer)],
        out_specs=[_row_spec(tm, D_MODEL), _row_spec(tm, ATTN_W), _row_spec(tm, GM_W),
                   _vec_spec(D_MODEL), _vec_spec(ATTN_W), _vec_spec(GM_W)],
        out_shape=[jax.ShapeDtypeStruct((t, D_MODEL), BF16), jax.ShapeDtypeStruct((t, ATTN_W), BF16),
                   jax.ShapeDtypeStruct((t, GM_W), F32), jax.ShapeDtypeStruct((1, D_MODEL), F32),
                   jax.ShapeDtypeStruct((1, ATTN_W), F32), jax.ShapeDtypeStruct((1, GM_W), F32)],
        compiler_params=_params("arbitrary"),
    )(dh1, mix, attn, gm, gpost, ga, gg, wb)


def _split3(x):
    hi = x.astype(BF16)
    r1 = x - hi.astype(F32)
    mid = r1.astype(BF16)
    lo = (r1 - mid.astype(F32)).astype(BF16)
    return hi, mid, lo


def _gm_bwd(dgm, zu, zv, lng, lnb, wtril, bsx):
    t = zu.shape[0]
    nb = t // BLK

    def body(d_ref, zu_ref, zv_ref, g_ref, b_ref, w_ref, bs_ref,
             dzu_ref, dzv_ref, dw_ref, dbs_ref, dlg_ref, dlb_ref, dbsx_ref):
        i = pl.program_id(0)
        _zero_at(i == 0, dw_ref, dlg_ref, dlb_ref, dbsx_ref)
        row = lax.broadcasted_iota(jnp.int32, (BLK, BLK), 0)
        lane = lax.broadcasted_iota(jnp.int32, (BLK, BLK), 1)
        low = lane < 64
        tril = row >= lane
        zu = zu_ref[...]
        zv = zv_ref[...]
        lng = g_ref[...]
        gu, ln, xn, rstd, mixed = _gm_forward_block(zu, zv, lng, b_ref[...], w_ref, bs_ref[...], low)
        dgm = d_ref[...]
        dgu_cols, dmx_cols, dln_cols = [], [], []
        for col in range(4):
            sl = slice(col * 128, (col + 1) * 128)
            dg = dgm[:, sl]
            dgu_cols.append(dg * mixed[col])
            dmx = dg * gu[:, sl]
            dmx_cols.append(dmx)
            lc = ln[:, sl]
            halves = (jnp.where(low, lc, 0.0).astype(BF16), jnp.where(low, 0.0, lc).astype(BF16))
            dmx16 = dmx.astype(BF16)
            dmx_half = (jnp.where(low, dmx, 0.0).astype(BF16), jnp.where(low, 0.0, dmx).astype(BF16))
            dln = None
            for half in range(2):
                hd = 2 * col + half
                dw_ref[hd] += jnp.where(tril, _dot_nt(dmx16, halves[half]), 0.0)
                part = _dot_tn(w_ref[hd], dmx_half[half])
                dln = part if dln is None else dln + part
            dln_cols.append(dln)
        dgu = jnp.concatenate(dgu_cols, axis=1)
        dmx = jnp.concatenate(dmx_cols, axis=1)
        dln = jnp.concatenate(dln_cols, axis=1)
        dzu_ref[...] = (dgu * _gelu_grad(zu)).astype(BF16)
        dbsx_ref[...] += dmx
        dlg_ref[...] += jnp.sum(dln * xn, axis=0, keepdims=True)
        dlb_ref[...] += jnp.sum(dln, axis=0, keepdims=True)
        dxn = dln * lng
        dgv = rstd * (dxn - jnp.mean(dxn, axis=-1, keepdims=True) - xn * jnp.mean(dxn * xn, axis=-1, keepdims=True))
        dzv_ref[...] = (dgv * _gelu_grad(zv)).astype(BF16)

        @pl.when(i == nb - 1)
        def _():
            r = lax.broadcasted_iota(jnp.int32, (GM_W, BLK), 0)
            c = lax.broadcasted_iota(jnp.int32, (GM_W, BLK), 1)
            e = jnp.where(jnp.logical_and(r >= c * 64, r < c * 64 + 64), 1.0, 0.0).astype(BF16)
            hi, mid, lo = _split3(dbsx_ref[...])
            dbs_ref[...] = _dot(hi, e) + _dot(mid, e) + _dot(lo, e)

    vec = _vec_spec(GM_W)
    return pl.pallas_call(
        body, name="gm_bwd", grid=(nb,),
        in_specs=[_row_spec(BLK, GM_W)] * 3 + [vec, vec, pl.BlockSpec((8, BLK, BLK), lambda i: (0, 0, 0)),
                                               pl.BlockSpec((BLK, GM_W), lambda i: (0, 0))],
        out_specs=[_row_spec(BLK, GM_W), _row_spec(BLK, GM_W), pl.BlockSpec((8, BLK, BLK), lambda i: (0, 0, 0)),
                   pl.BlockSpec((BLK, BLK), lambda i: (0, 0)), vec, vec],
        out_shape=[jax.ShapeDtypeStruct((t, GM_W), BF16), jax.ShapeDtypeStruct((t, GM_W), BF16),
                   jax.ShapeDtypeStruct((8, BLK, BLK), F32), jax.ShapeDtypeStruct((BLK, BLK), F32),
                   jax.ShapeDtypeStruct((1, GM_W), F32), jax.ShapeDtypeStruct((1, GM_W), F32)],
        scratch_shapes=[pltpu.VMEM((BLK, GM_W), F32)],
        compiler_params=_params("arbitrary"),
    )(dgm, zu, zv, lng, lnb, wtril, bsx)


def _attn_bwd(q, kv, do, sinks, zeros_kv):
    t = q.shape[0]
    cur, prev = _kv_specs()

    def body(sink_ref, q_ref, kvc_ref, kvp_ref, do_ref, zero_ref, dq_ref, dkc_ref, dkp_ref, ds_ref):
        i = pl.program_id(0)
        _zero_at(i == 0, ds_ref)
        low, dcur, dprev, vcur, vprev = _attn_masks(i)
        kc = _head_variants(kvc_ref[:, 0:128].astype(F32), low)
        vc = _head_variants(kvc_ref[:, 128:256].astype(F32), low)
        kp = _head_variants(kvp_ref[:, 0:128].astype(F32), low)
        vp = _head_variants(kvp_ref[:, 128:256].astype(F32), low)
        acc = {}

        def add(name, key, val):
            acc[(name, key)] = val if (name, key) not in acc else acc[(name, key)] + val

        for col in range(4):
            dq = None
            for half in range(2):
                h = 2 * col + half
                key = (h // 4, half)
                qh = q_ref[:, col * 128:(col + 1) * 128]
                doh = do_ref[:, col * 128:(col + 1) * 128]
                pc, pp, ps = _attn_probs(qh, kc[key], kp[key], 2.0 ** -(h + 1), sink_ref[h], dcur, dprev, vcur, vprev)
                dpc = _dot_nt(doh, vc[key])
                dpp = _dot_nt(doh, vp[key])
                delta = jnp.sum(pc * dpc, axis=1, keepdims=True) + jnp.sum(pp * dpp, axis=1, keepdims=True)
                dsc = (pc * (dpc - delta) * ATTN_SCALE).astype(BF16)
                dsp = (pp * (dpp - delta) * ATTN_SCALE).astype(BF16)
                dsink = jnp.sum(-ps * delta, axis=0, keepdims=True)
                ds_ref[h:h + 1, :] += jnp.broadcast_to(dsink, (1, 128))
                part = _dot(dsc, kc[key]) + _dot(dsp, kp[key])
                dq = part if dq is None else dq + part
                add("kc", key, _dot_tn(dsc, qh))
                add("kp", key, _dot_tn(dsp, qh))
                add("vc", key, _dot_tn(pc.astype(BF16), doh))
                add("vp", key, _dot_tn(pp.astype(BF16), doh))
            dq_ref[:, col * 128:(col + 1) * 128] = dq.astype(BF16)

        def place(name):
            head0 = acc[(name, (0, 0))] + pltpu.roll(acc[(name, (0, 1))], 64, axis=1)
            head1 = pltpu.roll(acc[(name, (1, 0))], 64, axis=1) + acc[(name, (1, 1))]
            return jnp.where(low, head0, head1)

        dkc_ref[:, 0:128] = place("kc")
        dkc_ref[:, 128:256] = place("vc")
        dkp_ref[:, 0:128] = place("kp")
        dkp_ref[:, 128:256] = place("vp")

    row_q = _row_spec(BLK, ATTN_W)
    return pl.pallas_call(
        body, name="attn_bwd", grid=(t // BLK,),
        in_specs=[pl.BlockSpec(memory_space=pltpu.SMEM), row_q, cur, prev, row_q, pl.BlockSpec(memory_space=pl.ANY)],
        out_specs=[row_q, cur, prev, pl.BlockSpec((8, 128), lambda i: (0, 0))],
        out_shape=[jax.ShapeDtypeStruct((t, ATTN_W), BF16), jax.ShapeDtypeStruct((t, 2 * KV_W), F32),
                   jax.ShapeDtypeStruct((t, 2 * KV_W), F32), jax.ShapeDtypeStruct((8, 128), F32)],
        input_output_aliases={5: 2},
        compiler_params=_params("arbitrary"),
    )(sinks, q, kv, kv, do, zeros_kv)


def _in_proj_bwd(dq, dkc, dkp, dzu, dzv, h, dres, g, wc, layer):
    t = h.shape[0]
    tm = _tile(t, 512)

    def body(dq_ref, dkc_ref, dkp_ref, dzu_ref, dzv_ref, h_ref, d_ref, g_ref, w_ref, dz_ref, dh_ref, dg_ref):
        _zero_at(pl.program_id(0) == 0, dg_ref)
        dq = dq_ref[...]
        dkv = (dkc_ref[...] + dkp_ref[...]).astype(BF16)
        dzu = dzu_ref[...]
        dzv = dzv_ref[...]
        dz_ref[:, 0:512] = dq
        dz_ref[:, 512:768] = dkv
        dz_ref[:, 768:1280] = dzu
        dz_ref[:, 1280:1792] = dzv
        da = (_dot(dq, w_ref[0:512, :]) + _dot(dkv, w_ref[512:768, :]) + _dot(dzu, w_ref[768:1280, :])
              + _dot(dzv, w_ref[1280:1792, :]))
        dx, dg = _rms_bwd(h_ref[...], g_ref[...], da)
        dh_ref[...] = d_ref[...] + dx
        dg_ref[...] += dg

    return pl.pallas_call(
        body, name="in_proj_bwd", grid=(t // tm,),
        in_specs=[_row_spec(tm, ATTN_W), _row_spec(tm, 2 * KV_W), _row_spec(tm, 2 * KV_W), _row_spec(tm, GM_W),
                  _row_spec(tm, GM_W), _row_spec(tm, D_MODEL), _row_spec(tm, D_MODEL), _vec_spec(D_MODEL),
                  _seg_spec(ROWS_C, D_MODEL, layer)],
        out_specs=[_row_spec(tm, D_IN), _row_spec(tm, D_MODEL), _vec_spec(D_MODEL)],
        out_shape=[jax.ShapeDtypeStruct((t, D_IN), BF16), jax.ShapeDtypeStruct((t, D_MODEL), F32),
                   jax.ShapeDtypeStruct((1, D_MODEL), F32)],
        compiler_params=_params("arbitrary"),
    )(dq, dkc, dkp, dzu, dzv, h, dres, g, wc)


def _weight_grad(a, b, buf, seg):
    t, m = a.shape
    n = b.shape[1]
    assert buf.shape[0] % m == 0 and buf.shape[1] == n
    tm = _tile(t, 512)

    def body(a_ref, b_ref, buf_ref, o_ref):
        _zero_at(pl.program_id(0) == 0, o_ref)
        o_ref[...] += _dot_tn(a_ref[...], b_ref[...].astype(BF16))

    return pl.pallas_call(
        body, name="weight_grad", grid=(t // tm,),
        in_specs=[_row_spec(tm, m), _row_spec(tm, n), pl.BlockSpec(memory_space=pl.ANY)],
        out_specs=pl.BlockSpec((m, n), lambda i: (seg, 0)),
        out_shape=jax.ShapeDtypeStruct(buf.shape, buf.dtype),
        input_output_aliases={2: 0},
        compiler_params=_params("arbitrary"),
    )(a, b, buf)


def _small_pack(d):
    rows = [d["ln_mix_pre"], d["ln_mix_post"], d["ln_ffn_pre"], d["ln_ffn_post"], d["ln_ple_gate"],
            jnp.concatenate([d["gm_ln_g"], d["gm_ln_b"]], axis=1),
            jnp.concatenate([d["g_attn_out"], d["g_gm_out"]], axis=1),
            d["gm_bs"].reshape(1, 1024),
            jnp.pad(d["attn_sinks"].reshape(1, 8), ((0, 0), (0, 1016))),
            d["gm_ws"].reshape(128, 1024),
            jnp.zeros((SMALL_ROWS - 137, 1024), F32)]
    return jnp.concatenate(rows, axis=0)


def _small_unpack(s):
    return {
        "ln_mix_pre": s[:, 0], "ln_mix_post": s[:, 1], "ln_ffn_pre": s[:, 2], "ln_ffn_post": s[:, 3],
        "ln_ple_gate": s[:, 4], "gm_ln_g": s[:, 5, :512], "gm_ln_b": s[:, 5, 512:],
        "g_attn_out": s[:, 6, :512], "g_gm_out": s[:, 6, 512:], "gm_bs": s[:, 7].reshape(DEPTH, 8, 128),
        "attn_sinks": s[:, 8, :8], "gm_ws": s[:, 9:137].reshape(DEPTH, 8, 128, 128),
    }


def _local_step(x, p, target, sp, wa, wb, wc, wp):
    t = x.shape[0]
    row = lambda v: v.reshape(1, -1)
    tril = jnp.tril(jnp.ones((BLK, BLK), bool))
    saved = []
    h = x
    for l in range(DEPTH):
        wtril = jnp.where(tril[None], sp["gm_ws"][l], 0.0).astype(BF16)
        bsx = jnp.repeat(sp["gm_bs"][l].T, HEAD_DIM, axis=1)
        a, q, kv, zu, zv = _in_proj(h, row(sp["ln_mix_pre"][l]), wc, l)
        attn = _attn_fwd(q, kv, sp["attn_sinks"][l])
        gm = _gm_fwd(zu, zv, row(sp["gm_ln_g"][l]), row(sp["gm_ln_b"][l]), wtril, bsx)
        heads, mix, h1 = _out_proj(attn, gm, h, row(sp["g_attn_out"][l]), row(sp["g_gm_out"][l]),
                                   row(sp["ln_mix_post"][l]), wb, l)
        f, gpre, up, act = _ffn_up(h1, row(sp["ln_ffn_pre"][l]), wa, l)
        fo, h2 = _ffn_down(act, h1, row(sp["ln_ffn_post"][l]), wa, l)
        hn, gate, pe, h3 = _ple(h2, p[l], row(sp["ln_ple_gate"][l]), wb, wp, l)
        saved.append(dict(h=h, a=a, q=q, kv=kv, zu=zu, zv=zv, attn=attn, gm=gm, heads=heads, mix=mix, h1=h1, f=f,
                          gpre=gpre, up=up, act=act, fo=fo, h2=h2, hn=hn, gate=gate, pe=pe, wtril=wtril, bsx=bsx))
        h = h3

    dh, sq = _loss_head(h, target)
    ga = lax.empty(wa.shape, F32)
    gb = lax.empty(wb.shape, F32)
    gc = lax.empty(wc.shape, F32)
    gp = lax.empty(wp.shape, F32)
    zeros_kv = jnp.zeros((t, 2 * KV_W), F32)
    small = [None] * DEPTH
    for l in reversed(range(DEPTH)):
        s = saved[l]
        d = {}
        dh2, dgl, dpe, d["ln_ple_gate"] = _ple_bwd(dh, s["h2"], s["gate"], s["pe"], row(sp["ln_ple_gate"][l]), wb, l)
        gb = _weight_grad(s["hn"], dgl, gb, 2 * l + 1)
        gp = _weight_grad(dpe, p[l], gp, l)
        dfo, dgp, dup, d["ln_ffn_post"] = _ffn_down_bwd(dh2, s["fo"], s["gpre"], s["up"], row(sp["ln_ffn_post"][l]), wa, l)
        ga = _weight_grad(s["act"], dfo, ga, 3 * l + 2)
        ga = _weight_grad(dgp, s["f"], ga, 3 * l)
        ga = _weight_grad(dup, s["f"], ga, 3 * l + 1)
        dh1, d["ln_ffn_pre"] = _ffn_up_bwd(dgp, dup, s["h1"], dh2, row(sp["ln_ffn_pre"][l]), wa, l)
        dmix, dattn, dgm, d["ln_mix_post"], d["g_attn_out"], d["g_gm_out"] = _out_proj_bwd(
            dh1, s["mix"], s["attn"], s["gm"], row(sp["ln_mix_post"][l]), row(sp["g_attn_out"][l]),
            row(sp["g_gm_out"][l]), wb, l)
        gb = _weight_grad(s["heads"], dmix, gb, 2 * l)
        dzu, dzv, d["gm_ws"], dbs, d["gm_ln_g"], d["gm_ln_b"] = _gm_bwd(
            dgm, s["zu"], s["zv"], row(sp["gm_ln_g"][l]), row(sp["gm_ln_b"][l]), s["wtril"], s["bsx"])
        d["gm_bs"] = dbs[:, :8].T
        dq, dkc, dkp, dsink = _attn_bwd(s["q"], s["kv"], dattn, sp["attn_sinks"][l], zeros_kv)
        d["attn_sinks"] = dsink[:, 0]
        dz, dh, d["ln_mix_pre"] = _in_proj_bwd(dq, dkc, dkp, dzu, dzv, s["h"], dh1, row(sp["ln_mix_pre"][l]), wc, l)
        gc = _weight_grad(dz, s["a"], gc, l)
        small[l] = _small_pack(d)
    return sq[0, 0], dh, ga, gb, gc, gp, jnp.concatenate(small, axis=0)


ANY = pl.BlockSpec(memory_space=pl.ANY)


def _place():
    x, y, c = lax.axis_index("x"), lax.axis_index("y"), lax.axis_index("c")
    chips = [(1 - x, y), (x, 1 - y), (1 - x, 1 - y)]
    return x, y, c, chips


def _all_gather(shards):
    n = len(shards)

    def body(*refs):
        ins, outs = refs[:n], refs[n:2 * n]
        send_sems, recv_sems, local_sems = refs[2 * n:]
        x, y, c, chips = _place()
        me, sibling = (x, y, c), (x, y, 1 - c)

        def block(k, px, py, pc):
            return outs[k].at[:, pl.ds(4 * px + 2 * py + pc, 1)]

        def copy(k, j, who, to, src=None):
            return pltpu.make_async_remote_copy(
                src_ref=block(k, *who) if src is None else src, dst_ref=block(k, *who),
                send_sem=send_sems.at[7 * k + j], recv_sem=recv_sems.at[7 * k + j],
                device_id=to, device_id_type=MESH)

        mine = [pltpu.make_async_copy(ins[k], block(k, *me), local_sems.at[k]) for k in range(n)]
        for cp in mine:
            cp.start()
        first = []
        for k in range(n):
            first.append(copy(k, 0, me, sibling, src=ins[k]))
            first += [copy(k, 1 + j, me, (*chip, c), src=ins[k]) for j, chip in enumerate(chips)]
        for cp in first:
            cp.start()
        passed = []
        for j, chip in enumerate(chips):
            for k in range(n):
                copy(k, 1 + j, (*chip, c), me).wait_recv()
                cp = copy(k, 4 + j, (*chip, c), sibling)
                cp.start()
                passed.append(cp)
        for k in range(n):
            copy(k, 0, sibling, me).wait_recv()
            for j, chip in enumerate(chips):
                copy(k, 4 + j, (*chip, 1 - c), me).wait_recv()
        for cp in first + passed:
            cp.wait_send()
        for cp in mine:
            cp.wait()

    return pl.pallas_call(
        body, name="all_gather_weights",
        in_specs=[ANY] * n, out_specs=[ANY] * n,
        out_shape=[jax.ShapeDtypeStruct((s.shape[0], N_DEV) + s.shape[2:], s.dtype) for s in shards],
        scratch_shapes=[pltpu.SemaphoreType.DMA((7 * n,)), pltpu.SemaphoreType.DMA((7 * n,)),
                        pltpu.SemaphoreType.DMA((n,))],
        compiler_params=pltpu.CompilerParams(has_side_effects=True),
    )(*shards)


def _sibling_exchange(bufs, small):
    n = len(bufs)

    def body(*refs):
        ins, small_ref = refs[:n], refs[n]
        outs, both_ref = refs[n + 1:2 * n + 1], refs[2 * n + 1]
        send_sems, recv_sems, local_sem = refs[2 * n + 2:]
        x, y, c, _ = _place()
        sibling = (x, y, 1 - c)
        mine = pltpu.make_async_copy(small_ref, both_ref.at[c], local_sem)
        mine.start()
        copies = [pltpu.make_async_remote_copy(
            src_ref=ins[k].at[:, :, pl.ds(1 - c, 1)], dst_ref=outs[k], send_sem=send_sems.at[k],
            recv_sem=recv_sems.at[k], device_id=sibling, device_id_type=MESH) for k in range(n)]
        copies.append(pltpu.make_async_remote_copy(
            src_ref=small_ref, dst_ref=both_ref.at[c], send_sem=send_sems.at[n], recv_sem=recv_sems.at[n],
            device_id=sibling, device_id_type=MESH))
        for cp in copies:
            cp.start()
        for k in range(n):
            copies[k].wait_recv()
        pltpu.make_async_remote_copy(
            src_ref=small_ref, dst_ref=both_ref.at[1 - c], send_sem=send_sems.at[n], recv_sem=recv_sems.at[n],
            device_id=sibling, device_id_type=MESH).wait_recv()
        for cp in copies:
            cp.wait_send()
        mine.wait()

    return pl.pallas_call(
        body, name="sibling_exchange",
        in_specs=[ANY] * (n + 1), out_specs=[ANY] * (n + 1),
        out_shape=[jax.ShapeDtypeStruct(b.shape[:2] + (1,) + b.shape[3:], b.dtype) for b in bufs]
        + [jax.ShapeDtypeStruct((2,) + small.shape, small.dtype)],
        scratch_shapes=[pltpu.SemaphoreType.DMA((n + 1,)), pltpu.SemaphoreType.DMA((n + 1,)), pltpu.SemaphoreType.DMA],
        compiler_params=pltpu.CompilerParams(has_side_effects=True),
    )(*bufs, small)


def _chip_exchange(sends, small):
    n = len(sends)

    def body(*refs):
        ins, small_ref = refs[:n], refs[n]
        outs, all_ref = refs[n + 1:2 * n + 1], refs[2 * n + 1]
        send_sems, recv_sems, local_sem = refs[2 * n + 2:]
        x, y, c, chips = _place()
        mine = pltpu.make_async_copy(small_ref, all_ref.at[2 * x + y], local_sem)
        mine.start()
        copies = []
        for j, chip in enumerate(chips):
            for k in range(n):
                copies.append(pltpu.make_async_remote_copy(
                    src_ref=ins[k].at[j], dst_ref=outs[k].at[j], send_sem=send_sems.at[3 * k + j],
                    recv_sem=recv_sems.at[3 * k + j], device_id=(*chip, c), device_id_type=MESH))
            copies.append(pltpu.make_async_remote_copy(
                src_ref=small_ref, dst_ref=all_ref.at[2 * x + y], send_sem=send_sems.at[3 * n + j],
                recv_sem=recv_sems.at[3 * n + j], device_id=(*chip, c), device_id_type=MESH))
        for cp in copies:
            cp.start()
        for j, (px, py) in enumerate(chips):
            for k in range(n):
                copies[j * (n + 1) + k].wait_recv()
            pltpu.make_async_remote_copy(
                src_ref=small_ref, dst_ref=all_ref.at[2 * px + py], send_sem=send_sems.at[3 * n + j],
                recv_sem=recv_sems.at[3 * n + j], device_id=(px, py, c), device_id_type=MESH).wait_recv()
        for cp in copies:
            cp.wait_send()
        mine.wait()

    return pl.pallas_call(
        body, name="chip_exchange",
        in_specs=[ANY] * (n + 1), out_specs=[ANY] * (n + 1),
        out_shape=[jax.ShapeDtypeStruct(s.shape, s.dtype) for s in sends]
        + [jax.ShapeDtypeStruct((4,) + small.shape, small.dtype)],
        scratch_shapes=[pltpu.SemaphoreType.DMA((3 * n + 3,)), pltpu.SemaphoreType.DMA((3 * n + 3,)),
                        pltpu.SemaphoreType.DMA],
        compiler_params=pltpu.CompilerParams(has_side_effects=True),
    )(*sends, small)


def _pair_add(buf, got, chip_ids, dtype):
    nseg, _, _, rows, cols = buf.shape
    nr = chip_ids.shape[0] - 1

    def body(ids_ref, a_ref, b_ref, o_ref):
        o_ref[...] = (a_ref[...] + b_ref[...]).astype(dtype)

    return pl.pallas_call(
        body, name="pair_add",
        grid_spec=pltpu.PrefetchScalarGridSpec(
            num_scalar_prefetch=1, grid=(nr, nseg),
            in_specs=[pl.BlockSpec((None, None, None, rows, cols), lambda r, s, ids: (s, ids[r], ids[nr], 0, 0)),
                      pl.BlockSpec((None, None, None, rows, cols), lambda r, s, ids: (s, ids[r], 0, 0, 0))],
            out_specs=pl.BlockSpec((None, None, rows, cols), lambda r, s, ids: (r, s, 0, 0))),
        out_shape=jax.ShapeDtypeStruct((nr, nseg, rows, cols), dtype),
        compiler_params=_params("parallel", "parallel"),
    )(chip_ids, buf, got)


def _sum_slots(z, tr):
    n, rows, cols = z.shape

    def body(z_ref, o_ref):
        s = z_ref[0]
        for k in range(1, n):
            s = s + z_ref[k]
        o_ref[...] = s

    return pl.pallas_call(
        body, name="sum_slots", grid=(rows // tr,),
        in_specs=[pl.BlockSpec((n, tr, cols), lambda i: (0, i, 0))],
        out_specs=pl.BlockSpec((tr, cols), lambda i: (i, 0)),
        out_shape=jax.ShapeDtypeStruct((rows, cols), F32),
        compiler_params=_params("parallel"),
    )(z)


def _final_sum(own, got):
    _, nseg, rows, cols = own.shape

    def body(a_ref, b_ref, o_ref):
        s = a_ref[0]
        for k in range(3):
            s = s + b_ref[k].astype(F32)
        o_ref[...] = s

    return pl.pallas_call(
        body, name="final_sum", grid=(nseg,),
        in_specs=[pl.BlockSpec((1, None, rows, cols), lambda i: (0, i, 0, 0)),
                  pl.BlockSpec((3, None, rows, cols), lambda i: (0, i, 0, 0))],
        out_specs=pl.BlockSpec((None, rows, cols), lambda i: (i, 0, 0)),
        out_shape=jax.ShapeDtypeStruct((nseg, rows, cols), F32),
        compiler_params=_params("parallel"),
    )(own, got)


def _adamw(w, g, m, v):
    shape = w.shape
    cols = shape[-1]
    rows = w.size // cols
    tr = rows
    for cand in (512, 256, 128, 64, 32, 16, 8):
        if rows % cand == 0:
            tr = cand
            break
    c1 = 1.0 / (1.0 - ADAM_B1 ** ADAM_STEP)
    c2 = 1.0 / (1.0 - ADAM_B2 ** ADAM_STEP)

    def body(w_ref, g_ref, m_ref, v_ref, d_ref, nm_ref, nv_ref):
        g = g_ref[...]
        m = ADAM_B1 * m_ref[...] + (1.0 - ADAM_B1) * g
        v = ADAM_B2 * v_ref[...] + (1.0 - ADAM_B2) * (g * g)
        nm_ref[...] = m
        nv_ref[...] = v
        d_ref[...] = -ADAM_LR * ((m * c1) / (jnp.sqrt(v * c2) + ADAM_EPS) + ADAM_WD * w_ref[...])

    spec = pl.BlockSpec((tr, cols), lambda i: (i, 0))
    outs = pl.pallas_call(
        body, name="adamw", grid=(rows // tr,),
        in_specs=[spec] * 4, out_specs=[spec] * 3,
        out_shape=[jax.ShapeDtypeStruct((rows, cols), F32)] * 3,
        compiler_params=_params("parallel"),
    )(*[a.reshape(rows, cols) for a in (w, g, m, v)])
    return [o.reshape(shape) for o in outs]


SMALL = ("ln_mix_pre", "attn_sinks", "gm_ln_g", "gm_ln_b", "gm_ws", "gm_bs", "g_attn_out", "g_gm_out",
         "ln_mix_post", "ln_ffn_pre", "ln_ffn_post", "ln_ple_gate")
WEIGHTS = ("ln_mix_pre", "w_in", "attn_sinks", "gm_ln_g", "gm_ln_b", "gm_ws", "gm_bs", "g_attn_out", "g_gm_out",
           "w_out", "ln_mix_post", "ln_ffn_pre", "w_ffn_gate", "w_ffn_up", "w_ffn_down", "ln_ffn_post", "w_ple",
           "ln_ple_gate", "w_ple_gate")


def _pack_shards(w):
    tr = lambda a: jnp.swapaxes(a, 1, 2)
    sa = jnp.stack([tr(w["w_ffn_gate"]), tr(w["w_ffn_up"]), w["w_ffn_down"]], axis=1)
    sb = jnp.stack([w["w_out"], w["w_ple_gate"]], axis=1)
    return (sa.reshape(3 * DEPTH, 1, ROWS_A, D_MODEL).astype(BF16), sb.reshape(2 * DEPTH, 1, ROWS_B, D_MODEL).astype(BF16),
            tr(w["w_in"]).reshape(DEPTH, 1, ROWS_C, D_MODEL).astype(BF16),
            tr(w["w_ple"]).reshape(DEPTH, 1, ROWS_B, PLE_DIM).astype(BF16))


def _unpack_grads(ra, rb, rc, rp):
    tr = lambda a: jnp.swapaxes(a, 1, 2)
    ra = ra.reshape(DEPTH, 3, ROWS_A, D_MODEL)
    rb = rb.reshape(DEPTH, 2, ROWS_B, D_MODEL)
    return {
        "w_ffn_gate": tr(ra[:, 0]), "w_ffn_up": tr(ra[:, 1]), "w_ffn_down": ra[:, 2],
        "w_out": rb[:, 0], "w_ple_gate": rb[:, 1], "w_in": tr(rc), "w_ple": tr(rp),
    }


def _reduce_grads(bufs, small):
    x, y, c, _ = _place()
    q = 2 * x + y
    ids_own = jnp.stack([q, c]).astype(jnp.int32)
    ids_send = jnp.stack([q ^ 2, q ^ 1, q ^ 3, c]).astype(jnp.int32)
    rows = (ROWS_A, ROWS_B, ROWS_C, ROWS_B)
    views = [b.reshape(-1, 4, 2, r, b.shape[1]) for b, r in zip(bufs, rows)]
    *got, both = _sibling_exchange(views, small)
    own = [_pair_add(v, g, ids_own, F32) for v, g in zip(views, got)]
    send = [_pair_add(v, g, ids_send, BF16) for v, g in zip(views, got)]
    small2 = _sum_slots(both, SMALL_ROWS)
    *landed, every = _chip_exchange(send, small2)
    reduced = [_final_sum(o, g) for o, g in zip(own, landed)]
    return reduced, _sum_slots(every, SMALL_ROWS)


def kernel(x, p, ln_mix_pre, w_in, attn_sinks, gm_ln_g, gm_ln_b, gm_ws, gm_bs, g_attn_out, g_gm_out, w_out, ln_mix_post, ln_ffn_pre, w_ffn_gate, w_ffn_up, w_ffn_down, ln_ffn_post, w_ple, ln_ple_gate, w_ple_gate, loss_target, m_ln_mix_pre, m_w_in, m_attn_sinks, m_gm_ln_g, m_gm_ln_b, m_gm_ws, m_gm_bs, m_g_attn_out, m_g_gm_out, m_w_out, m_ln_mix_post, m_ln_ffn_pre, m_w_ffn_gate, m_w_ffn_up, m_w_ffn_down, m_ln_ffn_post, m_w_ple, m_ln_ple_gate, m_w_ple_gate, v_ln_mix_pre, v_w_in, v_attn_sinks, v_gm_ln_g, v_gm_ln_b, v_gm_ws, v_gm_bs, v_g_attn_out, v_g_gm_out, v_w_out, v_ln_mix_post, v_ln_ffn_pre, v_w_ffn_gate, v_w_ffn_up, v_w_ffn_down, v_ln_ffn_post, v_w_ple, v_ln_ple_gate, v_w_ple_gate):
    given = dict(locals())
    w = {n: given[n] for n in WEIGHTS}
    gathered = _all_gather(_pack_shards(w))
    wa, wb, wc, wp = [g.reshape(-1, g.shape[-1]) for g in gathered]
    sq, grad_x, ga, gb, gc, gp, small = _local_step(x[0], p[:, 0], loss_target[0], {n: w[n] for n in SMALL}, wa, wb, wc, wp)
    reduced, small_sum = _reduce_grads([ga, gb, gc, gp], small)
    grads = _unpack_grads(*reduced)
    grads.update(_small_unpack(small_sum.reshape(DEPTH, SMALL_ROWS, D_MODEL)))
    loss = lax.psum(sq * (0.5 / D_MODEL), AXES)
    delta, new_m, new_v = {}, {}, {}
    for n in WEIGHTS:
        delta[n], new_m[n], new_v[n] = _adamw(w[n], grads[n], given["m_" + n], given["v_" + n])
    return (loss, grad_x[None], *[grads[n] for n in WEIGHTS], *[delta[n] for n in WEIGHTS],
            *[new_m[n] for n in WEIGHTS], *[new_v[n] for n in WEIGHTS])
```

```python
import math

import jax
import jax.numpy as jnp
from jax import lax
from jax.experimental import pallas as pl
from jax.experimental.pallas import tpu as pltpu

F32 = jnp.float32
BF16 = jnp.bfloat16
MESH = pl.DeviceIdType.MESH
AXES = ("x", "y", "c")

D_MODEL = 1024
DEPTH = 4
N_DEV = 8
HEAD_DIM = 64
ATTN_W = 512
KV_W = 128
GM_W = 512
D_IN = 1792
D_FF = 2816
PLE_DIM = 256
BLK = 128
NORM_EPS = 1e-6
NEG_BIG = -1e30
ATTN_SCALE = HEAD_DIM ** -0.5

ADAM_LR = 0.001
ADAM_B1 = 0.9
ADAM_B2 = 0.999
ADAM_EPS = 1e-08
ADAM_WD = 0.01
ADAM_STEP = 10

ROWS_A = D_FF // N_DEV
ROWS_B = D_MODEL // N_DEV
ROWS_C = D_IN // N_DEV
SMALL_ROWS = 200

VMEM_LIMIT = 56 * 2 ** 20


def _params(*sem):
    return pltpu.CompilerParams(dimension_semantics=sem, vmem_limit_bytes=VMEM_LIMIT)


def _dot(a, b):
    return jnp.dot(a, b, preferred_element_type=F32)


def _dot_nt(a, b):
    return lax.dot_general(a, b, (((1,), (1,)), ((), ())), preferred_element_type=F32)


def _dot_tn(a, b):
    return lax.dot_general(a, b, (((0,), (0,)), ((), ())), preferred_element_type=F32)


def _rms_fwd(x, g):
    r = lax.rsqrt(jnp.mean(x * x, axis=-1, keepdims=True) + NORM_EPS)
    return x * r * g


def _rms_bwd(x, g, dy):
    r = lax.rsqrt(jnp.mean(x * x, axis=-1, keepdims=True) + NORM_EPS)
    xh = x * r
    dg = jnp.sum(dy * xh, axis=0, keepdims=True)
    dxh = dy * g
    dx = r * (dxh - xh * jnp.mean(dxh * xh, axis=-1, keepdims=True))
    return dx, dg


_GELU_C = math.sqrt(2.0 / math.pi)


def _gelu(x):
    t = jnp.tanh(_GELU_C * (x + 0.044715 * (x * x * x)))
    return 0.5 * x * (1.0 + t)


def _gelu_grad(x):
    x2 = x * x
    t = jnp.tanh(_GELU_C * (x + 0.044715 * (x2 * x)))
    return 0.5 * (1.0 + t) + 0.5 * x * (1.0 - t * t) * (_GELU_C * (1.0 + 3.0 * 0.044715 * x2))


def _sigmoid(x):
    return 1.0 / (1.0 + jnp.exp(-x))


def _row_spec(tm, n):
    return pl.BlockSpec((tm, n), lambda i: (i, 0))


def _vec_spec(n):
    return pl.BlockSpec((1, n), lambda i: (0, 0))


def _seg_spec(rows, cols, seg):
    return pl.BlockSpec((N_DEV * rows, cols), lambda i: (seg, 0))


def _zero_at(first, *refs):
    @pl.when(first)
    def _():
        for r in refs:
            r[...] = jnp.zeros(r.shape, r.dtype)


def _tile(t, want):
    return min(t, want)


def _in_proj(h, g, wc, layer):
    t = h.shape[0]
    tm = _tile(t, 512)

    def body(h_ref, g_ref, w_ref, a_ref, q_ref, kv_ref, zu_ref, zv_ref):
        a = _rms_fwd(h_ref[...], g_ref[...]).astype(BF16)
        a_ref[...] = a
        q_ref[...] = _dot_nt(a, w_ref[0:512, :]).astype(BF16)
        kv_ref[...] = _dot_nt(a, w_ref[512:768, :]).astype(BF16)
        zu_ref[...] = _dot_nt(a, w_ref[768:1280, :])
        zv_ref[...] = _dot_nt(a, w_ref[1280:1792, :])

    return pl.pallas_call(
        body, name="in_proj", grid=(t // tm,),
        in_specs=[_row_spec(tm, D_MODEL), _vec_spec(D_MODEL), _seg_spec(ROWS_C, D_MODEL, layer)],
        out_specs=[_row_spec(tm, D_MODEL), _row_spec(tm, ATTN_W), _row_spec(tm, 2 * KV_W),
                   _row_spec(tm, GM_W), _row_spec(tm, GM_W)],
        out_shape=[jax.ShapeDtypeStruct((t, D_MODEL), BF16), jax.ShapeDtypeStruct((t, ATTN_W), BF16),
                   jax.ShapeDtypeStruct((t, 2 * KV_W), BF16), jax.ShapeDtypeStruct((t, GM_W), F32),
                   jax.ShapeDtypeStruct((t, GM_W), F32)],
        compiler_params=_params("parallel"),
    )(h, g, wc)


def _head_variants(x, low):
    xr = pltpu.roll(x, 64, axis=1)
    zero = jnp.zeros_like(x)
    return {
        (0, 0): jnp.where(low, x, zero).astype(BF16),
        (0, 1): jnp.where(low, zero, xr).astype(BF16),
        (1, 0): jnp.where(low, xr, zero).astype(BF16),
        (1, 1): jnp.where(low, zero, x).astype(BF16),
    }


def _attn_masks(i):
    row = lax.broadcasted_iota(jnp.int32, (BLK, BLK), 0)
    lane = lax.broadcasted_iota(jnp.int32, (BLK, BLK), 1)
    vcur = row >= lane
    dist = jnp.where(vcur, row - lane, row - lane + BLK).astype(F32)
    valid = jnp.logical_or(vcur, i > 0)
    return lane < 64, vcur, dist, valid


def _head_key(h):
    return (h // 4, h % 2)


def _attn_scores(q_ref, kc, kp, vcur):
    out = []
    for h in range(8):
        qh = q_ref[:, (h // 2) * 128:(h // 2 + 1) * 128]
        out.append(jnp.where(vcur, _dot_nt(qh, kc[_head_key(h)]), _dot_nt(qh, kp[_head_key(h)])))
    return out


def _attn_probs(s, h, sink, dist, valid):
    s = jnp.where(valid, s * ATTN_SCALE - (2.0 ** -(h + 1)) * dist, NEG_BIG)
    m = jnp.maximum(jnp.max(s, axis=1, keepdims=True), sink)
    e = jnp.exp(s - m)
    es = jnp.exp(sink - m)
    inv = 1.0 / (jnp.sum(e, axis=1, keepdims=True) + es)
    return e * inv, es * inv


def _kv_specs():
    cur = pl.BlockSpec((BLK, 2 * KV_W), lambda i: (i, 0))
    prev = pl.BlockSpec((BLK, 2 * KV_W), lambda i: (jnp.maximum(i - 1, 0), 0))
    return cur, prev


def _attn_fwd(q, kv, sinks):
    t = q.shape[0]
    cur, prev = _kv_specs()

    def body(sink_ref, q_ref, kvc_ref, kvp_ref, o_ref):
        i = pl.program_id(0)
        low, vcur, dist, valid = _attn_masks(i)
        kc = _head_variants(kvc_ref[:, 0:128].astype(F32), low)
        vc = _head_variants(kvc_ref[:, 128:256].astype(F32), low)
        kp = _head_variants(kvp_ref[:, 0:128].astype(F32), low)
        vp = _head_variants(kvp_ref[:, 128:256].astype(F32), low)
        scores = _attn_scores(q_ref, kc, kp, vcur)
        probs = [_attn_probs(scores[h], h, sink_ref[h], dist, valid)[0] for h in range(8)]
        for col in range(4):
            acc = None
            for half in range(2):
                h = 2 * col + half
                p = probs[h]
                o = (_dot(jnp.where(vcur, p, 0.0).astype(BF16), vc[_head_key(h)])
                     + _dot(jnp.where(vcur, 0.0, p).astype(BF16), vp[_head_key(h)]))
                acc = o if acc is None else acc + o
            o_ref[:, col * 128:(col + 1) * 128] = acc

    return pl.pallas_call(
        body, name="attn_fwd", grid=(t // BLK,),
        in_specs=[pl.BlockSpec(memory_space=pltpu.SMEM), _row_spec(BLK, ATTN_W), cur, prev],
        out_specs=_row_spec(BLK, ATTN_W),
        out_shape=jax.ShapeDtypeStruct((t, ATTN_W), F32),
        compiler_params=_params("parallel"),
    )(sinks, q, kv, kv)


def _gm_forward_block(zu, zv, lng, lnb, w_ref, bsx, low):
    gu = _gelu(zu)
    gv = _gelu(zv)
    mu = jnp.mean(gv, axis=-1, keepdims=True)
    xc = gv - mu
    rstd = lax.rsqrt(jnp.mean(xc * xc, axis=-1, keepdims=True) + NORM_EPS)
    xn = xc * rstd
    ln = xn * lng + lnb
    mixed = []
    for col in range(4):
        lc = ln[:, col * 128:(col + 1) * 128]
        lo = jnp.where(low, lc, 0.0).astype(BF16)
        hi = jnp.where(low, 0.0, lc).astype(BF16)
        mixed.append(_dot(w_ref[2 * col], lo) + _dot(w_ref[2 * col + 1], hi) + bsx[:, col * 128:(col + 1) * 128])
    return gu, ln, xn, rstd, mixed


def _gm_fwd(zu, zv, lng, lnb, wtril, bsx):
    t = zu.shape[0]

    def body(zu_ref, zv_ref, g_ref, b_ref, w_ref, bs_ref, o_ref):
        low = lax.broadcasted_iota(jnp.int32, (BLK, BLK), 1) < 64
        gu, _, _, _, mixed = _gm_forward_block(zu_ref[...], zv_ref[...], g_ref[...], b_ref[...], w_ref, bs_ref[...], low)
        for col in range(4):
            o_ref[:, col * 128:(col + 1) * 128] = gu[:, col * 128:(col + 1) * 128] * mixed[col]

    return pl.pallas_call(
        body, name="gm_fwd", grid=(t // BLK,),
        in_specs=[_row_spec(BLK, GM_W), _row_spec(BLK, GM_W), _vec_spec(GM_W), _vec_spec(GM_W),
                  pl.BlockSpec((8, BLK, BLK), lambda i: (0, 0, 0)), pl.BlockSpec((BLK, GM_W), lambda i: (0, 0))],
        out_specs=_row_spec(BLK, GM_W),
        out_shape=jax.ShapeDtypeStruct((t, GM_W), F32),
        compiler_params=_params("parallel"),
    )(zu, zv, lng, lnb, wtril, bsx)


def _out_proj(attn, gm, h, ga, gg, gpost, wb, layer):
    t = h.shape[0]
    tm = _tile(t, 512)

    def body(a_ref, m_ref, h_ref, ga_ref, gg_ref, gp_ref, w_ref, heads_ref, mix_ref, h1_ref):
        ha = _rms_fwd(a_ref[...], ga_ref[...]).astype(BF16)
        hg = _rms_fwd(m_ref[...], gg_ref[...]).astype(BF16)
        heads_ref[:, 0:512] = ha
        heads_ref[:, 512:1024] = hg
        mix = _dot(ha, w_ref[0:512, :]) + _dot(hg, w_ref[512:1024, :])
        mix_ref[...] = mix
        h1_ref[...] = h_ref[...] + _rms_fwd(mix, gp_ref[...])

    return pl.pallas_call(
        body, name="out_proj", grid=(t // tm,),
        in_specs=[_row_spec(tm, ATTN_W), _row_spec(tm, GM_W), _row_spec(tm, D_MODEL), _vec_spec(ATTN_W),
                  _vec_spec(GM_W), _vec_spec(D_MODEL), _seg_spec(ROWS_B, D_MODEL, 2 * layer)],
        out_specs=[_row_spec(tm, D_MODEL), _row_spec(tm, D_MODEL), _row_spec(tm, D_MODEL)],
        out_shape=[jax.ShapeDtypeStruct((t, D_MODEL), BF16), jax.ShapeDtypeStruct((t, D_MODEL), F32),
                   jax.ShapeDtypeStruct((t, D_MODEL), F32)],
        compiler_params=_params("parallel"),
    )(attn, gm, h, ga, gg, gpost, wb)


def _ffn_up(h1, g, wa, layer):
    t = h1.shape[0]
    tm = _tile(t, 256)

    def body(h_ref, g_ref, wg_ref, wu_ref, f_ref, gp_ref, up_ref, act_ref):
        f = _rms_fwd(h_ref[...], g_ref[...]).astype(BF16)
        f_ref[...] = f
        gp = _dot_nt(f, wg_ref[...])
        up = _dot_nt(f, wu_ref[...])
        gp_ref[...] = gp
        up_ref[...] = up
        act_ref[...] = (gp * _sigmoid(gp) * up).astype(BF16)

    return pl.pallas_call(
        body, name="ffn_up", grid=(t // tm,),
        in_specs=[_row_spec(tm, D_MODEL), _vec_spec(D_MODEL), _seg_spec(ROWS_A, D_MODEL, 3 * layer),
                  _seg_spec(ROWS_A, D_MODEL, 3 * layer + 1)],
        out_specs=[_row_spec(tm, D_MODEL), _row_spec(tm, D_FF), _row_spec(tm, D_FF), _row_spec(tm, D_FF)],
        out_shape=[jax.ShapeDtypeStruct((t, D_MODEL), BF16), jax.ShapeDtypeStruct((t, D_FF), F32),
                   jax.ShapeDtypeStruct((t, D_FF), F32), jax.ShapeDtypeStruct((t, D_FF), BF16)],
        compiler_params=_params("parallel"),
    )(h1, g, wa, wa)


def _ffn_down(act, h1, g, wa, layer):
    t = h1.shape[0]
    tm = _tile(t, 512)

    def body(a_ref, h_ref, g_ref, w_ref, fo_ref, h2_ref):
        fo = _dot(a_ref[...], w_ref[...])
        fo_ref[...] = fo
        h2_ref[...] = h_ref[...] + _rms_fwd(fo, g_ref[...])

    return pl.pallas_call(
        body, name="ffn_down", grid=(t // tm,),
        in_specs=[_row_spec(tm, D_FF), _row_spec(tm, D_MODEL), _vec_spec(D_MODEL),
                  _seg_spec(ROWS_A, D_MODEL, 3 * layer + 2)],
        out_specs=[_row_spec(tm, D_MODEL), _row_spec(tm, D_MODEL)],
        out_shape=[jax.ShapeDtypeStruct((t, D_MODEL), F32), jax.ShapeDtypeStruct((t, D_MODEL), F32)],
        compiler_params=_params("parallel"),
    )(act, h1, g, wa)


def _ple(h2, p, g, wb, wp, layer):
    t = h2.shape[0]
    tm = _tile(t, 512)

    def body(h_ref, p_ref, g_ref, wg_ref, wp_ref, hn_ref, gate_ref, pe_ref, h3_ref):
        h = h_ref[...]
        hn = _rms_fwd(h, g_ref[...]).astype(BF16)
        hn_ref[...] = hn
        gate = _sigmoid(_dot(hn, wg_ref[...]))
        pe = _dot_nt(p_ref[...].astype(BF16), wp_ref[...])
        gate_ref[...] = gate
        pe_ref[...] = pe
        h3_ref[...] = h + pe * gate

    return pl.pallas_call(
        body, name="ple_fwd", grid=(t // tm,),
        in_specs=[_row_spec(tm, D_MODEL), _row_spec(tm, PLE_DIM), _vec_spec(D_MODEL),
                  _seg_spec(ROWS_B, D_MODEL, 2 * layer + 1), _seg_spec(ROWS_B, PLE_DIM, layer)],
        out_specs=[_row_spec(tm, D_MODEL)] * 4,
        out_shape=[jax.ShapeDtypeStruct((t, D_MODEL), BF16)] + [jax.ShapeDtypeStruct((t, D_MODEL), F32)] * 3,
        compiler_params=_params("parallel"),
    )(h2, p, g, wb, wp)


def _loss_head(y, target):
    t = y.shape[0]
    tm = _tile(t, 512)

    def body(y_ref, t_ref, dy_ref, l_ref):
        _zero_at(pl.program_id(0) == 0, l_ref)
        e = y_ref[...] - t_ref[...]
        dy_ref[...] = e * (1.0 / D_MODEL)
        s = jnp.sum(jnp.sum(e * e, axis=1, keepdims=True), axis=0, keepdims=True)
        l_ref[...] += jnp.broadcast_to(s, (1, 128))

    return pl.pallas_call(
        body, name="loss_head", grid=(t // tm,),
        in_specs=[_row_spec(tm, D_MODEL), _row_spec(tm, D_MODEL)],
        out_specs=[_row_spec(tm, D_MODEL), _vec_spec(128)],
        out_shape=[jax.ShapeDtypeStruct((t, D_MODEL), F32), jax.ShapeDtypeStruct((1, 128), F32)],
        compiler_params=_params("arbitrary"),
    )(y, target)


def _ple_bwd(dh3, h2, gate, pe, g, wb, layer):
    t = h2.shape[0]
    tm = _tile(t, 512)

    def body(d_ref, h_ref, gate_ref, pe_ref, g_ref, w_ref, dh2_ref, dgl_ref, dpe_ref, dg_ref):
        _zero_at(pl.program_id(0) == 0, dg_ref)
        d = d_ref[...]
        gate = gate_ref[...]
        dpe_ref[...] = (d * gate).astype(BF16)
        dgl = (d * pe_ref[...] * gate * (1.0 - gate)).astype(BF16)
        dgl_ref[...] = dgl
        dhn = _dot_nt(dgl, w_ref[...])
        dx, dg = _rms_bwd(h_ref[...], g_ref[...], dhn)
        dh2_ref[...] = d + dx
        dg_ref[...] += dg

    return pl.pallas_call(
        body, name="ple_bwd", grid=(t // tm,),
        in_specs=[_row_spec(tm, D_MODEL)] * 4 + [_vec_spec(D_MODEL), _seg_spec(ROWS_B, D_MODEL, 2 * layer + 1)],
        out_specs=[_row_spec(tm, D_MODEL)] * 3 + [_vec_spec(D_MODEL)],
        out_shape=[jax.ShapeDtypeStruct((t, D_MODEL), F32), jax.ShapeDtypeStruct((t, D_MODEL), BF16),
                   jax.ShapeDtypeStruct((t, D_MODEL), BF16), jax.ShapeDtypeStruct((1, D_MODEL), F32)],
        compiler_params=_params("arbitrary"),
    )(dh3, h2, gate, pe, g, wb)


def _ffn_down_bwd(dh2, fo, gp, up, g, wa, layer):
    t = dh2.shape[0]
    tm = _tile(t, 256)

    def body(d_ref, fo_ref, gp_ref, up_ref, g_ref, w_ref, dfo_ref, dgp_ref, dup_ref, dg_ref):
        _zero_at(pl.program_id(0) == 0, dg_ref)
        dfo, dg = _rms_bwd(fo_ref[...], g_ref[...], d_ref[...])
        dfo = dfo.astype(BF16)
        dfo_ref[...] = dfo
        dact = _dot_nt(dfo, w_ref[...])
        gp = gp_ref[...]
        sg = _sigmoid(gp)
        dgp_ref[...] = (dact * up_ref[...] * (sg * (1.0 + gp * (1.0 - sg)))).astype(BF16)
        dup_ref[...] = (dact * (gp * sg)).astype(BF16)
        dg_ref[...] += dg

    return pl.pallas_call(
        body, name="ffn_down_bwd", grid=(t // tm,),
        in_specs=[_row_spec(tm, D_MODEL), _row_spec(tm, D_MODEL), _row_spec(tm, D_FF), _row_spec(tm, D_FF),
                  _vec_spec(D_MODEL), _seg_spec(ROWS_A, D_MODEL, 3 * layer + 2)],
        out_specs=[_row_spec(tm, D_MODEL), _row_spec(tm, D_FF), _row_spec(tm, D_FF), _vec_spec(D_MODEL)],
        out_shape=[jax.ShapeDtypeStruct((t, D_MODEL), BF16), jax.ShapeDtypeStruct((t, D_FF), BF16),
                   jax.ShapeDtypeStruct((t, D_FF), BF16), jax.ShapeDtypeStruct((1, D_MODEL), F32)],
        compiler_params=_params("arbitrary"),
    )(dh2, fo, gp, up, g, wa)


def _ffn_up_bwd(dgp, dup, h1, dh2, g, wa, layer):
    t = h1.shape[0]
    tm = _tile(t, 256)

    def body(dgp_ref, dup_ref, h_ref, d_ref, g_ref, wg_ref, wu_ref, dh1_ref, dg_ref):
        _zero_at(pl.program_id(0) == 0, dg_ref)
        df = _dot(dgp_ref[...], wg_ref[...]) + _dot(dup_ref[...], wu_ref[...])
        dx, dg = _rms_bwd(h_ref[...], g_ref[...], df)
        dh1_ref[...] = d_ref[...] + dx
        dg_ref[...] += dg

    return pl.pallas_call(
        body, name="ffn_up_bwd", grid=(t // tm,),
        in_specs=[_row_spec(tm, D_FF), _row_spec(tm, D_FF), _row_spec(tm, D_MODEL), _row_spec(tm, D_MODEL),
                  _vec_spec(D_MODEL), _seg_spec(ROWS_A, D_MODEL, 3 * layer), _seg_spec(ROWS_A, D_MODEL, 3 * layer + 1)],
        out_specs=[_row_spec(tm, D_MODEL), _vec_spec(D_MODEL)],
        out_shape=[jax.ShapeDtypeStruct((t, D_MODEL), F32), jax.ShapeDtypeStruct((1, D_MODEL), F32)],
        compiler_params=_params("arbitrary"),
    )(dgp, dup, h1, dh2, g, wa, wa)


def _out_proj_bwd(dh1, mix, attn, gm, gpost, ga, gg, wb, layer):
    t = dh1.shape[0]
    tm = _tile(t, 512)

    def body(d_ref, mix_ref, a_ref, m_ref, gp_ref, ga_ref, gg_ref, w_ref,
             dmix_ref, da_ref, dm_ref, dgp_ref, dga_ref, dgg_ref):
        _zero_at(pl.program_id(0) == 0, dgp_ref, dga_ref, dgg_ref)
        dmix, dgp = _rms_bwd(mix_ref[...], gp_ref[...], d_ref[...])
        dmix = dmix.astype(BF16)
        dmix_ref[...] = dmix
        da, dga = _rms_bwd(a_ref[...], ga_ref[...], _dot_nt(dmix, w_ref[0:512, :]))
        dm, dgg = _rms_bwd(m_ref[...], gg_ref[...], _dot_nt(dmix, w_ref[512:1024, :]))
        da_ref[...] = da.astype(BF16)
        dm_ref[...] = dm
        dgp_ref[...] += dgp
        dga_ref[...] += dga
        dgg_ref[...] += dgg

    return pl.pallas_call(
        body, name="out_proj_bwd", grid=(t // tm,),
        in_specs=[_row_spec(tm, D_MODEL), _row_spec(tm, D_MODEL), _row_spec(tm, ATTN_W), _row_spec(tm, GM_W),
                  _vec_spec(D_MODEL), _vec_spec(ATTN_W), _vec_spec(GM_W), _seg_spec(ROWS_B, D_MODEL, 2 * layer)],
        out_specs=[_row_spec(tm, D_MODEL), _row_spec(tm, ATTN_W), _row_spec(tm, GM_W),
                   _vec_spec(D_MODEL), _vec_spec(ATTN_W), _vec_spec(GM_W)],
        out_shape=[jax.ShapeDtypeStruct((t, D_MODEL), BF16), jax.ShapeDtypeStruct((t, ATTN_W), BF16),
                   jax.ShapeDtypeStruct((t, GM_W), F32), jax.ShapeDtypeStruct((1, D_MODEL), F32),
                   jax.ShapeDtypeStruct((1, ATTN_W), F32), jax.ShapeDtypeStruct((1, GM_W), F32)],
        compiler_params=_params("arbitrary"),
    )(dh1, mix, attn, gm, gpost, ga, gg, wb)


def _split3(x):
    hi = x.astype(BF16)
    r1 = x - hi.astype(F32)
    mid = r1.astype(BF16)
    lo = (r1 - mid.astype(F32)).astype(BF16)
    return hi, mid, lo


def _gm_bwd(dgm, zu, zv, lng, lnb, wtril, bsx):
    t = zu.shape[0]
    nb = t // BLK

    def body(d_ref, zu_ref, zv_ref, g_ref, b_ref, w_ref, bs_ref,
             dzu_ref, dzv_ref, dw_ref, dbs_ref, dlg_ref, dlb_ref, dbsx_ref):
        i = pl.program_id(0)
        _zero_at(i == 0, dw_ref, dlg_ref, dlb_ref, dbsx_ref)
        row = lax.broadcasted_iota(jnp.int32, (BLK, BLK), 0)
        lane = lax.broadcasted_iota(jnp.int32, (BLK, BLK), 1)
        low = lane < 64
        tril = row >= lane
        zu = zu_ref[...]
        zv = zv_ref[...]
        lng = g_ref[...]
        gu, ln, xn, rstd, mixed = _gm_forward_block(zu, zv, lng, b_ref[...], w_ref, bs_ref[...], low)
        dgm = d_ref[...]
        dgu_cols, dmx_cols, dln_cols = [], [], []
        for col in range(4):
            sl = slice(col * 128, (col + 1) * 128)
            dg = dgm[:, sl]
            dgu_cols.append(dg * mixed[col])
            dmx = dg * gu[:, sl]
            dmx_cols.append(dmx)
            lc = ln[:, sl]
            halves = (jnp.where(low, lc, 0.0).astype(BF16), jnp.where(low, 0.0, lc).astype(BF16))
            dmx16 = dmx.astype(BF16)
            dmx_half = (jnp.where(low, dmx, 0.0).astype(BF16), jnp.where(low, 0.0, dmx).astype(BF16))
            dln = None
            for half in range(2):
                hd = 2 * col + half
                dw_ref[hd] += jnp.where(tril, _dot_nt(dmx16, halves[half]), 0.0)
                part = _dot_tn(w_ref[hd], dmx_half[half])
                dln = part if dln is None else dln + part
            dln_cols.append(dln)
        dgu = jnp.concatenate(dgu_cols, axis=1)
        dmx = jnp.concatenate(dmx_cols, axis=1)
        dln = jnp.concatenate(dln_cols, axis=1)
        dzu_ref[...] = (dgu * _gelu_grad(zu)).astype(BF16)
        dbsx_ref[...] += dmx
        dlg_ref[...] += jnp.sum(dln * xn, axis=0, keepdims=True)
        dlb_ref[...] += jnp.sum(dln, axis=0, keepdims=True)
        dxn = dln * lng
        dgv = rstd * (dxn - jnp.mean(dxn, axis=-1, keepdims=True) - xn * jnp.mean(dxn * xn, axis=-1, keepdims=True))
        dzv_ref[...] = (dgv * _gelu_grad(zv)).astype(BF16)

        @pl.when(i == nb - 1)
        def _():
            r = lax.broadcasted_iota(jnp.int32, (GM_W, BLK), 0)
            c = lax.broadcasted_iota(jnp.int32, (GM_W, BLK), 1)
            e = jnp.where(jnp.logical_and(r >= c * 64, r < c * 64 + 64), 1.0, 0.0).astype(BF16)
            hi, mid, lo = _split3(dbsx_ref[...])
            dbs_ref[...] = _dot(hi, e) + _dot(mid, e) + _dot(lo, e)

    vec = _vec_spec(GM_W)
    return pl.pallas_call(
        body, name="gm_bwd", grid=(nb,),
        in_specs=[_row_spec(BLK, GM_W)] * 3 + [vec, vec, pl.BlockSpec((8, BLK, BLK), lambda i: (0, 0, 0)),
                                               pl.BlockSpec((BLK, GM_W), lambda i: (0, 0))],
        out_specs=[_row_spec(BLK, GM_W), _row_spec(BLK, GM_W), pl.BlockSpec((8, BLK, BLK), lambda i: (0, 0, 0)),
                   pl.BlockSpec((BLK, BLK), lambda i: (0, 0)), vec, vec],
        out_shape=[jax.ShapeDtypeStruct((t, GM_W), BF16), jax.ShapeDtypeStruct((t, GM_W), BF16),
                   jax.ShapeDtypeStruct((8, BLK, BLK), F32), jax.ShapeDtypeStruct((BLK, BLK), F32),
                   jax.ShapeDtypeStruct((1, GM_W), F32), jax.ShapeDtypeStruct((1, GM_W), F32)],
        scratch_shapes=[pltpu.VMEM((BLK, GM_W), F32)],
        compiler_params=_params("arbitrary"),
    )(dgm, zu, zv, lng, lnb, wtril, bsx)


def _attn_bwd(q, kv, do, sinks, zeros_kv):
    t = q.shape[0]
    cur, prev = _kv_specs()

    def body(sink_ref, q_ref, kvc_ref, kvp_ref, do_ref, zero_ref, dq_ref, dkc_ref, dkp_ref, ds_ref):
        i = pl.program_id(0)
        _zero_at(i == 0, ds_ref)
        low, vcur, dist, valid = _attn_masks(i)
        kc = _head_variants(kvc_ref[:, 0:128].astype(F32), low)
        vc = _head_variants(kvc_ref[:, 128:256].astype(F32), low)
        kp = _head_variants(kvp_ref[:, 0:128].astype(F32), low)
        vp = _head_variants(kvp_ref[:, 128:256].astype(F32), low)
        scores = _attn_scores(q_ref, kc, kp, vcur)
        dprobs = _attn_scores(do_ref, vc, vp, vcur)
        head_row = lax.broadcasted_iota(jnp.int32, (8, 128), 0)
        dsink_tile = jnp.zeros((8, 128), F32)
        parts = []
        for h in range(8):
            p, ps = _attn_probs(scores[h], h, sink_ref[h], dist, valid)
            delta = jnp.sum(p * dprobs[h], axis=1, keepdims=True)
            ds = p * (dprobs[h] - delta) * ATTN_SCALE
            dsink_tile = jnp.where(head_row == h, jnp.sum(-ps * delta, axis=0, keepdims=True), dsink_tile)
            parts.append((jnp.where(vcur, ds, 0.0).astype(BF16), jnp.where(vcur, 0.0, ds).astype(BF16),
                          jnp.where(vcur, p, 0.0).astype(BF16), jnp.where(vcur, 0.0, p).astype(BF16)))
        ds_ref[...] += dsink_tile
        acc = {}

        def add(name, key, val):
            acc[(name, key)] = val if (name, key) not in acc else acc[(name, key)] + val

        for col in range(4):
            dq = None
            qh = q_ref[:, col * 128:(col + 1) * 128]
            doh = do_ref[:, col * 128:(col + 1) * 128]
            for half in range(2):
                h = 2 * col + half
                key = _head_key(h)
                dsc, dsp, pc, pp = parts[h]
                part = _dot(dsc, kc[key]) + _dot(dsp, kp[key])
                dq = part if dq is None else dq + part
                add("kc", key, _dot_tn(dsc, qh))
                add("kp", key, _dot_tn(dsp, qh))
                add("vc", key, _dot_tn(pc, doh))
                add("vp", key, _dot_tn(pp, doh))
            dq_ref[:, col * 128:(col + 1) * 128] = dq.astype(BF16)

        def place(name):
            head0 = acc[(name, (0, 0))] + pltpu.roll(acc[(name, (0, 1))], 64, axis=1)
            head1 = pltpu.roll(acc[(name, (1, 0))], 64, axis=1) + acc[(name, (1, 1))]
            return jnp.where(low, head0, head1)

        dkc_ref[:, 0:128] = place("kc")
        dkc_ref[:, 128:256] = place("vc")
        dkp_ref[:, 0:128] = place("kp")
        dkp_ref[:, 128:256] = place("vp")

    row_q = _row_spec(BLK, ATTN_W)
    return pl.pallas_call(
        body, name="attn_bwd", grid=(t // BLK,),
        in_specs=[pl.BlockSpec(memory_space=pltpu.SMEM), row_q, cur, prev, row_q, pl.BlockSpec(memory_space=pl.ANY)],
        out_specs=[row_q, cur, prev, pl.BlockSpec((8, 128), lambda i: (0, 0))],
        out_shape=[jax.ShapeDtypeStruct((t, ATTN_W), BF16), jax.ShapeDtypeStruct((t, 2 * KV_W), F32),
                   jax.ShapeDtypeStruct((t, 2 * KV_W), F32), jax.ShapeDtypeStruct((8, 128), F32)],
        input_output_aliases={5: 2},
        compiler_params=_params("arbitrary"),
    )(sinks, q, kv, kv, do, zeros_kv)


def _in_proj_bwd(dq, dkc, dkp, dzu, dzv, h, dres, g, wc, layer):
    t = h.shape[0]
    tm = _tile(t, 512)

    def body(dq_ref, dkc_ref, dkp_ref, dzu_ref, dzv_ref, h_ref, d_ref, g_ref, w_ref, dz_ref, dh_ref, dg_ref):
        _zero_at(pl.program_id(0) == 0, dg_ref)
        dq = dq_ref[...]
        dkv = (dkc_ref[...] + dkp_ref[...]).astype(BF16)
        dzu = dzu_ref[...]
        dzv = dzv_ref[...]
        dz_ref[:, 0:512] = dq
        dz_ref[:, 512:768] = dkv
        dz_ref[:, 768:1280] = dzu
        dz_ref[:, 1280:1792] = dzv
        da = (_dot(dq, w_ref[0:512, :]) + _dot(dkv, w_ref[512:768, :]) + _dot(dzu, w_ref[768:1280, :])
              + _dot(dzv, w_ref[1280:1792, :]))
        dx, dg = _rms_bwd(h_ref[...], g_ref[...], da)
        dh_ref[...] = d_ref[...] + dx
        dg_ref[...] += dg

    return pl.pallas_call(
        body, name="in_proj_bwd", grid=(t // tm,),
        in_specs=[_row_spec(tm, ATTN_W), _row_spec(tm, 2 * KV_W), _row_spec(tm, 2 * KV_W), _row_spec(tm, GM_W),
                  _row_spec(tm, GM_W), _row_spec(tm, D_MODEL), _row_spec(tm, D_MODEL), _vec_spec(D_MODEL),
                  _seg_spec(ROWS_C, D_MODEL, layer)],
        out_specs=[_row_spec(tm, D_IN), _row_spec(tm, D_MODEL), _vec_spec(D_MODEL)],
        out_shape=[jax.ShapeDtypeStruct((t, D_IN), BF16), jax.ShapeDtypeStruct((t, D_MODEL), F32),
                   jax.ShapeDtypeStruct((1, D_MODEL), F32)],
        compiler_params=_params("arbitrary"),
    )(dq, dkc, dkp, dzu, dzv, h, dres, g, wc)


def _weight_grad(a, b, buf, seg):
    t, m = a.shape
    n = b.shape[1]
    assert buf.shape[0] % m == 0 and buf.shape[1] == n
    tm = _tile(t, 512)

    def body(a_ref, b_ref, buf_ref, o_ref):
        _zero_at(pl.program_id(0) == 0, o_ref)
        o_ref[...] += _dot_tn(a_ref[...], b_ref[...].astype(BF16))

    return pl.pallas_call(
        body, name="weight_grad", grid=(t // tm,),
        in_specs=[_row_spec(tm, m), _row_spec(tm, n), pl.BlockSpec(memory_space=pl.ANY)],
        out_specs=pl.BlockSpec((m, n), lambda i: (seg, 0)),
        out_shape=jax.ShapeDtypeStruct(buf.shape, buf.dtype),
        input_output_aliases={2: 0},
        compiler_params=_params("arbitrary"),
    )(a, b, buf)


def _small_pack(d):
    rows = [d["ln_mix_pre"], d["ln_mix_post"], d["ln_ffn_pre"], d["ln_ffn_post"], d["ln_ple_gate"],
            jnp.concatenate([d["gm_ln_g"], d["gm_ln_b"]], axis=1),
            jnp.concatenate([d["g_attn_out"], d["g_gm_out"]], axis=1),
            d["gm_bs"].reshape(1, 1024),
            jnp.pad(d["attn_sinks"].reshape(1, 8), ((0, 0), (0, 1016)))]
    parts = [jnp.pad(r, ((0, 7), (0, 0))) for r in rows] + [d["gm_ws"].reshape(128, 1024)]
    return jnp.concatenate(parts, axis=0)


def _small_unpack(s):
    return {
        "ln_mix_pre": s[:, 0], "ln_mix_post": s[:, 8], "ln_ffn_pre": s[:, 16], "ln_ffn_post": s[:, 24],
        "ln_ple_gate": s[:, 32], "gm_ln_g": s[:, 40, :512], "gm_ln_b": s[:, 40, 512:],
        "g_attn_out": s[:, 48, :512], "g_gm_out": s[:, 48, 512:], "gm_bs": s[:, 56].reshape(DEPTH, 8, 128),
        "attn_sinks": s[:, 64, :8], "gm_ws": s[:, 72:200].reshape(DEPTH, 8, 128, 128),
    }


def _local_step(x, p, target, sp, wa, wb, wc, wp):
    t = x.shape[0]
    row = lambda v: v.reshape(1, -1)
    tril = jnp.tril(jnp.ones((BLK, BLK), bool))
    saved = []
    h = x
    for l in range(DEPTH):
        wtril = jnp.where(tril[None], sp["gm_ws"][l], 0.0).astype(BF16)
        bsx = jnp.repeat(sp["gm_bs"][l].T, HEAD_DIM, axis=1)
        a, q, kv, zu, zv = _in_proj(h, row(sp["ln_mix_pre"][l]), wc, l)
        attn = _attn_fwd(q, kv, sp["attn_sinks"][l])
        gm = _gm_fwd(zu, zv, row(sp["gm_ln_g"][l]), row(sp["gm_ln_b"][l]), wtril, bsx)
        heads, mix, h1 = _out_proj(attn, gm, h, row(sp["g_attn_out"][l]), row(sp["g_gm_out"][l]),
                                   row(sp["ln_mix_post"][l]), wb, l)
        f, gpre, up, act = _ffn_up(h1, row(sp["ln_ffn_pre"][l]), wa, l)
        fo, h2 = _ffn_down(act, h1, row(sp["ln_ffn_post"][l]), wa, l)
        hn, gate, pe, h3 = _ple(h2, p[l], row(sp["ln_ple_gate"][l]), wb, wp, l)
        saved.append(dict(h=h, a=a, q=q, kv=kv, zu=zu, zv=zv, attn=attn, gm=gm, heads=heads, mix=mix, h1=h1, f=f,
                          gpre=gpre, up=up, act=act, fo=fo, h2=h2, hn=hn, gate=gate, pe=pe, wtril=wtril, bsx=bsx))
        h = h3

    dh, sq = _loss_head(h, target)
    ga = lax.empty(wa.shape, F32)
    gb = lax.empty(wb.shape, F32)
    gc = lax.empty(wc.shape, F32)
    gp = lax.empty(wp.shape, F32)
    zeros_kv = jnp.zeros((t, 2 * KV_W), F32)
    small = [None] * DEPTH
    for l in reversed(range(DEPTH)):
        s = saved[l]
        d = {}
        dh2, dgl, dpe, d["ln_ple_gate"] = _ple_bwd(dh, s["h2"], s["gate"], s["pe"], row(sp["ln_ple_gate"][l]), wb, l)
        gb = _weight_grad(s["hn"], dgl, gb, 2 * l + 1)
        gp = _weight_grad(dpe, p[l], gp, l)
        dfo, dgp, dup, d["ln_ffn_post"] = _ffn_down_bwd(dh2, s["fo"], s["gpre"], s["up"], row(sp["ln_ffn_post"][l]), wa, l)
        ga = _weight_grad(s["act"], dfo, ga, 3 * l + 2)
        ga = _weight_grad(dgp, s["f"], ga, 3 * l)
        ga = _weight_grad(dup, s["f"], ga, 3 * l + 1)
        dh1, d["ln_ffn_pre"] = _ffn_up_bwd(dgp, dup, s["h1"], dh2, row(sp["ln_ffn_pre"][l]), wa, l)
        dmix, dattn, dgm, d["ln_mix_post"], d["g_attn_out"], d["g_gm_out"] = _out_proj_bwd(
            dh1, s["mix"], s["attn"], s["gm"], row(sp["ln_mix_post"][l]), row(sp["g_attn_out"][l]),
            row(sp["g_gm_out"][l]), wb, l)
        gb = _weight_grad(s["heads"], dmix, gb, 2 * l)
        dzu, dzv, d["gm_ws"], dbs, d["gm_ln_g"], d["gm_ln_b"] = _gm_bwd(
            dgm, s["zu"], s["zv"], row(sp["gm_ln_g"][l]), row(sp["gm_ln_b"][l]), s["wtril"], s["bsx"])
        d["gm_bs"] = dbs[:, :8].T
        dq, dkc, dkp, dsink = _attn_bwd(s["q"], s["kv"], dattn, sp["attn_sinks"][l], zeros_kv)
        d["attn_sinks"] = dsink[:, 0]
        dz, dh, d["ln_mix_pre"] = _in_proj_bwd(dq, dkc, dkp, dzu, dzv, s["h"], dh1, row(sp["ln_mix_pre"][l]), wc, l)
        gc = _weight_grad(dz, s["a"], gc, l)
        small[l] = _small_pack(d)
    return sq[0, 0], dh, ga, gb, gc, gp, jnp.concatenate(small, axis=0)


ANY = pl.BlockSpec(memory_space=pl.ANY)


def _place():
    x, y, c = lax.axis_index("x"), lax.axis_index("y"), lax.axis_index("c")
    chips = [(1 - x, y), (x, 1 - y), (1 - x, 1 - y)]
    return x, y, c, chips


def _all_gather(shards):
    n = len(shards)

    def body(*refs):
        ins, outs = refs[:n], refs[n:2 * n]
        send_sems, recv_sems, local_sems = refs[2 * n:]
        x, y, c, chips = _place()
        me, sibling = (x, y, c), (x, y, 1 - c)

        def block(k, px, py, pc):
            return outs[k].at[:, pl.ds(4 * px + 2 * py + pc, 1)]

        def copy(k, j, who, to, src=None):
            return pltpu.make_async_remote_copy(
                src_ref=block(k, *who) if src is None else src, dst_ref=block(k, *who),
                send_sem=send_sems.at[7 * k + j], recv_sem=recv_sems.at[7 * k + j],
                device_id=to, device_id_type=MESH)

        mine = [pltpu.make_async_copy(ins[k], block(k, *me), local_sems.at[k]) for k in range(n)]
        for cp in mine:
            cp.start()
        first = []
        for k in range(n):
            first.append(copy(k, 0, me, sibling, src=ins[k]))
            first += [copy(k, 1 + j, me, (*chip, c), src=ins[k]) for j, chip in enumerate(chips)]
        for cp in first:
            cp.start()
        passed = []
        for j, chip in enumerate(chips):
            for k in range(n):
                copy(k, 1 + j, (*chip, c), me).wait_recv()
                cp = copy(k, 4 + j, (*chip, c), sibling)
                cp.start()
                passed.append(cp)
        for k in range(n):
            copy(k, 0, sibling, me).wait_recv()
            for j, chip in enumerate(chips):
                copy(k, 4 + j, (*chip, 1 - c), me).wait_recv()
        for cp in first + passed:
            cp.wait_send()
        for cp in mine:
            cp.wait()

    return pl.pallas_call(
        body, name="all_gather_weights",
        in_specs=[ANY] * n, out_specs=[ANY] * n,
        out_shape=[jax.ShapeDtypeStruct((s.shape[0], N_DEV) + s.shape[2:], s.dtype) for s in shards],
        scratch_shapes=[pltpu.SemaphoreType.DMA((7 * n,)), pltpu.SemaphoreType.DMA((7 * n,)),
                        pltpu.SemaphoreType.DMA((n,))],
        compiler_params=pltpu.CompilerParams(has_side_effects=True),
    )(*shards)


def _sibling_exchange(bufs, small):
    n = len(bufs)

    def body(*refs):
        ins, small_ref = refs[:n], refs[n]
        outs, both_ref = refs[n + 1:2 * n + 1], refs[2 * n + 1]
        send_sems, recv_sems, local_sem = refs[2 * n + 2:]
        x, y, c, _ = _place()
        sibling = (x, y, 1 - c)
        mine = pltpu.make_async_copy(small_ref, both_ref.at[c], local_sem)
        mine.start()
        copies = [pltpu.make_async_remote_copy(
            src_ref=ins[k].at[:, :, pl.ds(1 - c, 1)], dst_ref=outs[k], send_sem=send_sems.at[k],
            recv_sem=recv_sems.at[k], device_id=sibling, device_id_type=MESH) for k in range(n)]
        copies.append(pltpu.make_async_remote_copy(
            src_ref=small_ref, dst_ref=both_ref.at[c], send_sem=send_sems.at[n], recv_sem=recv_sems.at[n],
            device_id=sibling, device_id_type=MESH))
        for cp in copies:
            cp.start()
        for k in range(n):
            copies[k].wait_recv()
        pltpu.make_async_remote_copy(
            src_ref=small_ref, dst_ref=both_ref.at[1 - c], send_sem=send_sems.at[n], recv_sem=recv_sems.at[n],
            device_id=sibling, device_id_type=MESH).wait_recv()
        for cp in copies:
            cp.wait_send()
        mine.wait()

    return pl.pallas_call(
        body, name="sibling_exchange",
        in_specs=[ANY] * (n + 1), out_specs=[ANY] * (n + 1),
        out_shape=[jax.ShapeDtypeStruct(b.shape[:2] + (1,) + b.shape[3:], b.dtype) for b in bufs]
        + [jax.ShapeDtypeStruct((2,) + small.shape, small.dtype)],
        scratch_shapes=[pltpu.SemaphoreType.DMA((n + 1,)), pltpu.SemaphoreType.DMA((n + 1,)), pltpu.SemaphoreType.DMA],
        compiler_params=pltpu.CompilerParams(has_side_effects=True),
    )(*bufs, small)


def _chip_exchange(sends, small):
    n = len(sends)

    def body(*refs):
        ins, small_ref = refs[:n], refs[n]
        outs, all_ref = refs[n + 1:2 * n + 1], refs[2 * n + 1]
        send_sems, recv_sems, local_sem = refs[2 * n + 2:]
        x, y, c, chips = _place()
        mine = pltpu.make_async_copy(small_ref, all_ref.at[2 * x + y], local_sem)
        mine.start()
        copies = []
        for j, chip in enumerate(chips):
            for k in range(n):
                copies.append(pltpu.make_async_remote_copy(
                    src_ref=ins[k].at[j], dst_ref=outs[k].at[j], send_sem=send_sems.at[3 * k + j],
                    recv_sem=recv_sems.at[3 * k + j], device_id=(*chip, c), device_id_type=MESH))
            copies.append(pltpu.make_async_remote_copy(
                src_ref=small_ref, dst_ref=all_ref.at[2 * x + y], send_sem=send_sems.at[3 * n + j],
                recv_sem=recv_sems.at[3 * n + j], device_id=(*chip, c), device_id_type=MESH))
        for cp in copies:
            cp.start()
        for j, (px, py) in enumerate(chips):
            for k in range(n):
                copies[j * (n + 1) + k].wait_recv()
            pltpu.make_async_remote_copy(
                src_ref=small_ref, dst_ref=all_ref.at[2 * px + py], send_sem=send_sems.at[3 * n + j],
                recv_sem=recv_sems.at[3 * n + j], device_id=(px, py, c), device_id_type=MESH).wait_recv()
        for cp in copies:
            cp.wait_send()
        mine.wait()

    return pl.pallas_call(
        body, name="chip_exchange",
        in_specs=[ANY] * (n + 1), out_specs=[ANY] * (n + 1),
        out_shape=[jax.ShapeDtypeStruct(s.shape, s.dtype) for s in sends]
        + [jax.ShapeDtypeStruct((4,) + small.shape, small.dtype)],
        scratch_shapes=[pltpu.SemaphoreType.DMA((3 * n + 3,)), pltpu.SemaphoreType.DMA((3 * n + 3,)),
                        pltpu.SemaphoreType.DMA],
        compiler_params=pltpu.CompilerParams(has_side_effects=True),
    )(*sends, small)


def _pair_add(buf, got, chip_ids, dtype):
    nseg, _, _, rows, cols = buf.shape
    nr = chip_ids.shape[0] - 1

    def body(ids_ref, a_ref, b_ref, o_ref):
        o_ref[...] = (a_ref[...] + b_ref[...]).astype(dtype)

    return pl.pallas_call(
        body, name="pair_add",
        grid_spec=pltpu.PrefetchScalarGridSpec(
            num_scalar_prefetch=1, grid=(nr, nseg),
            in_specs=[pl.BlockSpec((None, None, None, rows, cols), lambda r, s, ids: (s, ids[r], ids[nr], 0, 0)),
                      pl.BlockSpec((None, None, None, rows, cols), lambda r, s, ids: (s, ids[r], 0, 0, 0))],
            out_specs=pl.BlockSpec((None, None, rows, cols), lambda r, s, ids: (r, s, 0, 0))),
        out_shape=jax.ShapeDtypeStruct((nr, nseg, rows, cols), dtype),
        compiler_params=_params("parallel", "parallel"),
    )(chip_ids, buf, got)


def _sum_slots(z, tr):
    n, rows, cols = z.shape

    def body(z_ref, o_ref):
        s = z_ref[0]
        for k in range(1, n):
            s = s + z_ref[k]
        o_ref[...] = s

    return pl.pallas_call(
        body, name="sum_slots", grid=(rows // tr,),
        in_specs=[pl.BlockSpec((n, tr, cols), lambda i: (0, i, 0))],
        out_specs=pl.BlockSpec((tr, cols), lambda i: (i, 0)),
        out_shape=jax.ShapeDtypeStruct((rows, cols), F32),
        compiler_params=_params("parallel"),
    )(z)


def _final_sum(own, got):
    _, nseg, rows, cols = own.shape

    def body(a_ref, b_ref, o_ref):
        s = a_ref[0]
        for k in range(3):
            s = s + b_ref[k].astype(F32)
        o_ref[...] = s

    return pl.pallas_call(
        body, name="final_sum", grid=(nseg,),
        in_specs=[pl.BlockSpec((1, None, rows, cols), lambda i: (0, i, 0, 0)),
                  pl.BlockSpec((3, None, rows, cols), lambda i: (0, i, 0, 0))],
        out_specs=pl.BlockSpec((None, rows, cols), lambda i: (i, 0, 0)),
        out_shape=jax.ShapeDtypeStruct((nseg, rows, cols), F32),
        compiler_params=_params("parallel"),
    )(own, got)


def _adamw(w, g, m, v):
    shape = w.shape
    cols = shape[-1]
    rows = w.size // cols
    tr = rows
    for cand in (512, 256, 128, 64, 32, 16, 8):
        if rows % cand == 0:
            tr = cand
            break
    c1 = 1.0 / (1.0 - ADAM_B1 ** ADAM_STEP)
    c2 = 1.0 / (1.0 - ADAM_B2 ** ADAM_STEP)

    def body(w_ref, g_ref, m_ref, v_ref, d_ref, nm_ref, nv_ref):
        g = g_ref[...]
        m = ADAM_B1 * m_ref[...] + (1.0 - ADAM_B1) * g
        v = ADAM_B2 * v_ref[...] + (1.0 - ADAM_B2) * (g * g)
        nm_ref[...] = m
        nv_ref[...] = v
        d_ref[...] = -ADAM_LR * ((m * c1) / (jnp.sqrt(v * c2) + ADAM_EPS) + ADAM_WD * w_ref[...])

    spec = pl.BlockSpec((tr, cols), lambda i: (i, 0))
    outs = pl.pallas_call(
        body, name="adamw", grid=(rows // tr,),
        in_specs=[spec] * 4, out_specs=[spec] * 3,
        out_shape=[jax.ShapeDtypeStruct((rows, cols), F32)] * 3,
        compiler_params=_params("parallel"),
    )(*[a.reshape(rows, cols) for a in (w, g, m, v)])
    return [o.reshape(shape) for o in outs]


SMALL = ("ln_mix_pre", "attn_sinks", "gm_ln_g", "gm_ln_b", "gm_ws", "gm_bs", "g_attn_out", "g_gm_out",
         "ln_mix_post", "ln_ffn_pre", "ln_ffn_post", "ln_ple_gate")
WEIGHTS = ("ln_mix_pre", "w_in", "attn_sinks", "gm_ln_g", "gm_ln_b", "gm_ws", "gm_bs", "g_attn_out", "g_gm_out",
           "w_out", "ln_mix_post", "ln_ffn_pre", "w_ffn_gate", "w_ffn_up", "w_ffn_down", "ln_ffn_post", "w_ple",
           "ln_ple_gate", "w_ple_gate")


def _pack_shards(w):
    tr = lambda a: jnp.swapaxes(a, 1, 2)
    sa = jnp.stack([tr(w["w_ffn_gate"]), tr(w["w_ffn_up"]), w["w_ffn_down"]], axis=1)
    sb = jnp.stack([w["w_out"], w["w_ple_gate"]], axis=1)
    return (sa.reshape(3 * DEPTH, 1, ROWS_A, D_MODEL).astype(BF16), sb.reshape(2 * DEPTH, 1, ROWS_B, D_MODEL).astype(BF16),
            tr(w["w_in"]).reshape(DEPTH, 1, ROWS_C, D_MODEL).astype(BF16),
            tr(w["w_ple"]).reshape(DEPTH, 1, ROWS_B, PLE_DIM).astype(BF16))


def _unpack_grads(ra, rb, rc, rp):
    tr = lambda a: jnp.swapaxes(a, 1, 2)
    ra = ra.reshape(DEPTH, 3, ROWS_A, D_MODEL)
    rb = rb.reshape(DEPTH, 2, ROWS_B, D_MODEL)
    return {
        "w_ffn_gate": tr(ra[:, 0]), "w_ffn_up": tr(ra[:, 1]), "w_ffn_down": ra[:, 2],
        "w_out": rb[:, 0], "w_ple_gate": rb[:, 1], "w_in": tr(rc), "w_ple": tr(rp),
    }


def _reduce_grads(bufs, small):
    x, y, c, _ = _place()
    q = 2 * x + y
    ids_own = jnp.stack([q, c]).astype(jnp.int32)
    ids_send = jnp.stack([q ^ 2, q ^ 1, q ^ 3, c]).astype(jnp.int32)
    rows = (ROWS_A, ROWS_B, ROWS_C, ROWS_B)
    views = [b.reshape(-1, 4, 2, r, b.shape[1]) for b, r in zip(bufs, rows)]
    *got, both = _sibling_exchange(views, small)
    own = [_pair_add(v, g, ids_own, F32) for v, g in zip(views, got)]
    send = [_pair_add(v, g, ids_send, BF16) for v, g in zip(views, got)]
    small2 = _sum_slots(both, SMALL_ROWS)
    *landed, every = _chip_exchange(send, small2)
    reduced = [_final_sum(o, g) for o, g in zip(own, landed)]
    return reduced, _sum_slots(every, SMALL_ROWS)


def kernel(x, p, ln_mix_pre, w_in, attn_sinks, gm_ln_g, gm_ln_b, gm_ws, gm_bs, g_attn_out, g_gm_out, w_out, ln_mix_post, ln_ffn_pre, w_ffn_gate, w_ffn_up, w_ffn_down, ln_ffn_post, w_ple, ln_ple_gate, w_ple_gate, loss_target, m_ln_mix_pre, m_w_in, m_attn_sinks, m_gm_ln_g, m_gm_ln_b, m_gm_ws, m_gm_bs, m_g_attn_out, m_g_gm_out, m_w_out, m_ln_mix_post, m_ln_ffn_pre, m_w_ffn_gate, m_w_ffn_up, m_w_ffn_down, m_ln_ffn_post, m_w_ple, m_ln_ple_gate, m_w_ple_gate, v_ln_mix_pre, v_w_in, v_attn_sinks, v_gm_ln_g, v_gm_ln_b, v_gm_ws, v_gm_bs, v_g_attn_out, v_g_gm_out, v_w_out, v_ln_mix_post, v_ln_ffn_pre, v_w_ffn_gate, v_w_ffn_up, v_w_ffn_down, v_ln_ffn_post, v_w_ple, v_ln_ple_gate, v_w_ple_gate):
    given = dict(locals())
    w = {n: given[n] for n in WEIGHTS}
    gathered = _all_gather(_pack_shards(w))
    wa, wb, wc, wp = [g.reshape(-1, g.shape[-1]) for g in gathered]
    sq, grad_x, ga, gb, gc, gp, small = _local_step(x[0], p[:, 0], loss_target[0], {n: w[n] for n in SMALL}, wa, wb, wc, wp)
    reduced, small_sum = _reduce_grads([ga, gb, gc, gp], small)
    grads = _unpack_grads(*reduced)
    grads.update(_small_unpack(small_sum.reshape(DEPTH, SMALL_ROWS, D_MODEL)))
    loss = lax.psum(sq * (0.5 / D_MODEL), AXES)
    delta, new_m, new_v = {}, {}, {}
    for n in WEIGHTS:
        delta[n], new_m[n], new_v[n] = _adamw(w[n], grads[n], given["m_" + n], given["v_" + n])
    return (loss, grad_x[None], *[grads[n] for n in WEIGHTS], *[delta[n] for n in WEIGHTS],
            *[new_m[n] for n in WEIGHTS], *[new_v[n] for n in WEIGHTS])
```

```python
import math

import jax
import jax.numpy as jnp
from jax import lax
from jax.experimental import pallas as pl
from jax.experimental.pallas import tpu as pltpu

F32 = jnp.float32
BF16 = jnp.bfloat16
MESH = pl.DeviceIdType.MESH
AXES = ("x", "y", "c")

D_MODEL = 1024
DEPTH = 4
N_DEV = 8
HEAD_DIM = 64
ATTN_W = 512
KV_W = 128
GM_W = 512
D_IN = 1792
D_FF = 2816
PLE_DIM = 256
BLK = 128
NORM_EPS = 1e-6
NEG_BIG = -1e30
ATTN_SCALE = HEAD_DIM ** -0.5

ADAM_LR = 0.001
ADAM_B1 = 0.9
ADAM_B2 = 0.999
ADAM_EPS = 1e-08
ADAM_WD = 0.01
ADAM_STEP = 10

ROWS_A = D_FF // N_DEV
ROWS_B = D_MODEL // N_DEV
ROWS_C = D_IN // N_DEV
SMALL_ROWS = 200

VMEM_LIMIT = 56 * 2 ** 20


def _params(*sem):
    return pltpu.CompilerParams(dimension_semantics=sem, vmem_limit_bytes=VMEM_LIMIT)


def _dot(a, b):
    return jnp.dot(a, b, preferred_element_type=F32)


def _dot_nt(a, b):
    return lax.dot_general(a, b, (((1,), (1,)), ((), ())), preferred_element_type=F32)


def _dot_tn(a, b):
    return lax.dot_general(a, b, (((0,), (0,)), ((), ())), preferred_element_type=F32)


def _rms_fwd(x, g):
    r = lax.rsqrt(jnp.mean(x * x, axis=-1, keepdims=True) + NORM_EPS)
    return x * r * g


def _rms_bwd(x, g, dy):
    r = lax.rsqrt(jnp.mean(x * x, axis=-1, keepdims=True) + NORM_EPS)
    xh = x * r
    dg = jnp.sum(dy * xh, axis=0, keepdims=True)
    dxh = dy * g
    dx = r * (dxh - xh * jnp.mean(dxh * xh, axis=-1, keepdims=True))
    return dx, dg


_GELU_C = math.sqrt(2.0 / math.pi)


def _gelu(x):
    t = jnp.tanh(_GELU_C * (x + 0.044715 * (x * x * x)))
    return 0.5 * x * (1.0 + t)


def _gelu_grad(x):
    x2 = x * x
    t = jnp.tanh(_GELU_C * (x + 0.044715 * (x2 * x)))
    return 0.5 * (1.0 + t) + 0.5 * x * (1.0 - t * t) * (_GELU_C * (1.0 + 3.0 * 0.044715 * x2))


def _sigmoid(x):
    return 1.0 / (1.0 + jnp.exp(-x))


def _row_spec(tm, n):
    return pl.BlockSpec((tm, n), lambda i: (i, 0))


def _vec_spec(n):
    return pl.BlockSpec((1, n), lambda i: (0, 0))


def _seg_spec(rows, cols, seg):
    return pl.BlockSpec((N_DEV * rows, cols), lambda i: (seg, 0))


def _zero_at(first, *refs):
    @pl.when(first)
    def _():
        for r in refs:
            r[...] = jnp.zeros(r.shape, r.dtype)


def _tile(t, want):
    return min(t, want)


def _in_proj(h, g, wc, layer):
    t = h.shape[0]
    tm = _tile(t, 512)

    def body(h_ref, g_ref, w_ref, a_ref, q_ref, kv_ref, zu_ref, zv_ref):
        a = _rms_fwd(h_ref[...], g_ref[...]).astype(BF16)
        a_ref[...] = a
        q_ref[...] = _dot_nt(a, w_ref[0:512, :]).astype(BF16)
        kv_ref[...] = _dot_nt(a, w_ref[512:768, :]).astype(BF16)
        zu_ref[...] = _dot_nt(a, w_ref[768:1280, :])
        zv_ref[...] = _dot_nt(a, w_ref[1280:1792, :])

    return pl.pallas_call(
        body, name="in_proj", grid=(t // tm,),
        in_specs=[_row_spec(tm, D_MODEL), _vec_spec(D_MODEL), _seg_spec(ROWS_C, D_MODEL, layer)],
        out_specs=[_row_spec(tm, D_MODEL), _row_spec(tm, ATTN_W), _row_spec(tm, 2 * KV_W),
                   _row_spec(tm, GM_W), _row_spec(tm, GM_W)],
        out_shape=[jax.ShapeDtypeStruct((t, D_MODEL), BF16), jax.ShapeDtypeStruct((t, ATTN_W), BF16),
                   jax.ShapeDtypeStruct((t, 2 * KV_W), BF16), jax.ShapeDtypeStruct((t, GM_W), F32),
                   jax.ShapeDtypeStruct((t, GM_W), F32)],
        compiler_params=_params("parallel"),
    )(h, g, wc)


def _head_variants(x, low):
    xr = pltpu.roll(x, 64, axis=1)
    zero = jnp.zeros_like(x)
    return {
        (0, 0): jnp.where(low, x, zero).astype(BF16),
        (0, 1): jnp.where(low, zero, xr).astype(BF16),
        (1, 0): jnp.where(low, xr, zero).astype(BF16),
        (1, 1): jnp.where(low, zero, x).astype(BF16),
    }


def _attn_masks(i):
    row = lax.broadcasted_iota(jnp.int32, (BLK, BLK), 0)
    lane = lax.broadcasted_iota(jnp.int32, (BLK, BLK), 1)
    vcur = row >= lane
    dist = jnp.where(vcur, row - lane, row - lane + BLK).astype(F32)
    valid = jnp.logical_or(vcur, i > 0)
    return lane < 64, vcur, dist, valid


def _head_key(h):
    return (h // 4, h % 2)


def _attn_scores(q_ref, kc, kp, vcur):
    out = []
    for h in range(8):
        qh = q_ref[:, (h // 2) * 128:(h // 2 + 1) * 128]
        out.append(jnp.where(vcur, _dot_nt(qh, kc[_head_key(h)]), _dot_nt(qh, kp[_head_key(h)])))
    return out


def _attn_probs(s, h, sink, dist, valid):
    s = jnp.where(valid, s * ATTN_SCALE - (2.0 ** -(h + 1)) * dist, NEG_BIG)
    m = jnp.maximum(jnp.max(s, axis=1, keepdims=True), sink)
    e = jnp.exp(s - m)
    es = jnp.exp(sink - m)
    inv = 1.0 / (jnp.sum(e, axis=1, keepdims=True) + es)
    return e * inv, es * inv


def _kv_specs():
    cur = pl.BlockSpec((BLK, 2 * KV_W), lambda i: (i, 0))
    prev = pl.BlockSpec((BLK, 2 * KV_W), lambda i: (jnp.maximum(i - 1, 0), 0))
    return cur, prev


def _attn_fwd(q, kv, sinks):
    t = q.shape[0]
    cur, prev = _kv_specs()

    def body(sink_ref, q_ref, kvc_ref, kvp_ref, o_ref):
        i = pl.program_id(0)
        low, vcur, dist, valid = _attn_masks(i)
        kc = _head_variants(kvc_ref[:, 0:128].astype(F32), low)
        vc = _head_variants(kvc_ref[:, 128:256].astype(F32), low)
        kp = _head_variants(kvp_ref[:, 0:128].astype(F32), low)
        vp = _head_variants(kvp_ref[:, 128:256].astype(F32), low)
        scores = _attn_scores(q_ref, kc, kp, vcur)
        probs = [_attn_probs(scores[h], h, sink_ref[h], dist, valid)[0] for h in range(8)]
        for col in range(4):
            acc = None
            for half in range(2):
                h = 2 * col + half
                p = probs[h]
                o = (_dot(jnp.where(vcur, p, 0.0).astype(BF16), vc[_head_key(h)])
                     + _dot(jnp.where(vcur, 0.0, p).astype(BF16), vp[_head_key(h)]))
                acc = o if acc is None else acc + o
            o_ref[:, col * 128:(col + 1) * 128] = acc

    return pl.pallas_call(
        body, name="attn_fwd", grid=(t // BLK,),
        in_specs=[pl.BlockSpec(memory_space=pltpu.SMEM), _row_spec(BLK, ATTN_W), cur, prev],
        out_specs=_row_spec(BLK, ATTN_W),
        out_shape=jax.ShapeDtypeStruct((t, ATTN_W), F32),
        compiler_params=_params("parallel"),
    )(sinks, q, kv, kv)


def _gm_forward_block(zu, zv, lng, lnb, w_ref, bsx, low):
    gu = _gelu(zu)
    gv = _gelu(zv)
    mu = jnp.mean(gv, axis=-1, keepdims=True)
    xc = gv - mu
    rstd = lax.rsqrt(jnp.mean(xc * xc, axis=-1, keepdims=True) + NORM_EPS)
    xn = xc * rstd
    ln = xn * lng + lnb
    mixed = []
    for col in range(4):
        lc = ln[:, col * 128:(col + 1) * 128]
        lo = jnp.where(low, lc, 0.0).astype(BF16)
        hi = jnp.where(low, 0.0, lc).astype(BF16)
        mixed.append(_dot(w_ref[2 * col], lo) + _dot(w_ref[2 * col + 1], hi) + bsx[:, col * 128:(col + 1) * 128])
    return gu, ln, xn, rstd, mixed


def _gm_fwd(zu, zv, lng, lnb, wtril, bsx):
    t = zu.shape[0]

    def body(zu_ref, zv_ref, g_ref, b_ref, w_ref, bs_ref, o_ref):
        low = lax.broadcasted_iota(jnp.int32, (BLK, BLK), 1) < 64
        gu, _, _, _, mixed = _gm_forward_block(zu_ref[...], zv_ref[...], g_ref[...], b_ref[...], w_ref, bs_ref[...], low)
        for col in range(4):
            o_ref[:, col * 128:(col + 1) * 128] = gu[:, col * 128:(col + 1) * 128] * mixed[col]

    return pl.pallas_call(
        body, name="gm_fwd", grid=(t // BLK,),
        in_specs=[_row_spec(BLK, GM_W), _row_spec(BLK, GM_W), _vec_spec(GM_W), _vec_spec(GM_W),
                  pl.BlockSpec((8, BLK, BLK), lambda i: (0, 0, 0)), pl.BlockSpec((BLK, GM_W), lambda i: (0, 0))],
        out_specs=_row_spec(BLK, GM_W),
        out_shape=jax.ShapeDtypeStruct((t, GM_W), F32),
        compiler_params=_params("parallel"),
    )(zu, zv, lng, lnb, wtril, bsx)


def _out_proj(attn, gm, h, ga, gg, gpost, wb, layer):
    t = h.shape[0]
    tm = _tile(t, 512)

    def body(a_ref, m_ref, h_ref, ga_ref, gg_ref, gp_ref, w_ref, heads_ref, mix_ref, h1_ref):
        ha = _rms_fwd(a_ref[...], ga_ref[...]).astype(BF16)
        hg = _rms_fwd(m_ref[...], gg_ref[...]).astype(BF16)
        heads_ref[:, 0:512] = ha
        heads_ref[:, 512:1024] = hg
        mix = _dot(ha, w_ref[0:512, :]) + _dot(hg, w_ref[512:1024, :])
        mix_ref[...] = mix
        h1_ref[...] = h_ref[...] + _rms_fwd(mix, gp_ref[...])

    return pl.pallas_call(
        body, name="out_proj", grid=(t // tm,),
        in_specs=[_row_spec(tm, ATTN_W), _row_spec(tm, GM_W), _row_spec(tm, D_MODEL), _vec_spec(ATTN_W),
                  _vec_spec(GM_W), _vec_spec(D_MODEL), _seg_spec(ROWS_B, D_MODEL, 2 * layer)],
        out_specs=[_row_spec(tm, D_MODEL), _row_spec(tm, D_MODEL), _row_spec(tm, D_MODEL)],
        out_shape=[jax.ShapeDtypeStruct((t, D_MODEL), BF16), jax.ShapeDtypeStruct((t, D_MODEL), F32),
                   jax.ShapeDtypeStruct((t, D_MODEL), F32)],
        compiler_params=_params("parallel"),
    )(attn, gm, h, ga, gg, gpost, wb)


def _ffn_up(h1, g, wa, layer):
    t = h1.shape[0]
    tm = _tile(t, 256)

    def body(h_ref, g_ref, wg_ref, wu_ref, f_ref, gp_ref, up_ref, act_ref):
        f = _rms_fwd(h_ref[...], g_ref[...]).astype(BF16)
        f_ref[...] = f
        gp = _dot_nt(f, wg_ref[...])
        up = _dot_nt(f, wu_ref[...])
        gp_ref[...] = gp
        up_ref[...] = up
        act_ref[...] = (gp * _sigmoid(gp) * up).astype(BF16)

    return pl.pallas_call(
        body, name="ffn_up", grid=(t // tm,),
        in_specs=[_row_spec(tm, D_MODEL), _vec_spec(D_MODEL), _seg_spec(ROWS_A, D_MODEL, 3 * layer),
                  _seg_spec(ROWS_A, D_MODEL, 3 * layer + 1)],
        out_specs=[_row_spec(tm, D_MODEL), _row_spec(tm, D_FF), _row_spec(tm, D_FF), _row_spec(tm, D_FF)],
        out_shape=[jax.ShapeDtypeStruct((t, D_MODEL), BF16), jax.ShapeDtypeStruct((t, D_FF), F32),
                   jax.ShapeDtypeStruct((t, D_FF), F32), jax.ShapeDtypeStruct((t, D_FF), BF16)],
        compiler_params=_params("parallel"),
    )(h1, g, wa, wa)


def _ffn_down(act, h1, g, wa, layer):
    t = h1.shape[0]
    tm = _tile(t, 512)

    def body(a_ref, h_ref, g_ref, w_ref, fo_ref, h2_ref):
        fo = _dot(a_ref[...], w_ref[...])
        fo_ref[...] = fo
        h2_ref[...] = h_ref[...] + _rms_fwd(fo, g_ref[...])

    return pl.pallas_call(
        body, name="ffn_down", grid=(t // tm,),
        in_specs=[_row_spec(tm, D_FF), _row_spec(tm, D_MODEL), _vec_spec(D_MODEL),
                  _seg_spec(ROWS_A, D_MODEL, 3 * layer + 2)],
        out_specs=[_row_spec(tm, D_MODEL), _row_spec(tm, D_MODEL)],
        out_shape=[jax.ShapeDtypeStruct((t, D_MODEL), F32), jax.ShapeDtypeStruct((t, D_MODEL), F32)],
        compiler_params=_params("parallel"),
    )(act, h1, g, wa)


def _ple(h2, p, g, wb, wp, layer):
    t = h2.shape[0]
    tm = _tile(t, 512)

    def body(h_ref, p_ref, g_ref, wg_ref, wp_ref, hn_ref, gate_ref, pe_ref, h3_ref):
        h = h_ref[...]
        hn = _rms_fwd(h, g_ref[...]).astype(BF16)
        hn_ref[...] = hn
        gate = _sigmoid(_dot(hn, wg_ref[...]))
        pe = _dot_nt(p_ref[...].astype(BF16), wp_ref[...])
        gate_ref[...] = gate
        pe_ref[...] = pe
        h3_ref[...] = h + pe * gate

    return pl.pallas_call(
        body, name="ple_fwd", grid=(t // tm,),
        in_specs=[_row_spec(tm, D_MODEL), _row_spec(tm, PLE_DIM), _vec_spec(D_MODEL),
                  _seg_spec(ROWS_B, D_MODEL, 2 * layer + 1), _seg_spec(ROWS_B, PLE_DIM, layer)],
        out_specs=[_row_spec(tm, D_MODEL)] * 4,
        out_shape=[jax.ShapeDtypeStruct((t, D_MODEL), BF16)] + [jax.ShapeDtypeStruct((t, D_MODEL), F32)] * 3,
        compiler_params=_params("parallel"),
    )(h2, p, g, wb, wp)


def _loss_head(y, target):
    t = y.shape[0]
    tm = _tile(t, 512)

    def body(y_ref, t_ref, dy_ref, l_ref):
        _zero_at(pl.program_id(0) == 0, l_ref)
        e = y_ref[...] - t_ref[...]
        dy_ref[...] = e * (1.0 / D_MODEL)
        s = jnp.sum(jnp.sum(e * e, axis=1, keepdims=True), axis=0, keepdims=True)
        l_ref[...] += jnp.broadcast_to(s, (1, 128))

    return pl.pallas_call(
        body, name="loss_head", grid=(t // tm,),
        in_specs=[_row_spec(tm, D_MODEL), _row_spec(tm, D_MODEL)],
        out_specs=[_row_spec(tm, D_MODEL), _vec_spec(128)],
        out_shape=[jax.ShapeDtypeStruct((t, D_MODEL), F32), jax.ShapeDtypeStruct((1, 128), F32)],
        compiler_params=_params("arbitrary"),
    )(y, target)


def _ple_bwd(dh3, h2, gate, pe, g, wb, layer, after):
    t = h2.shape[0]
    tm = _tile(t, 512)

    def body(d_ref, h_ref, gate_ref, pe_ref, g_ref, w_ref, after_ref, dh2_ref, dgl_ref, dpe_ref, dg_ref):
        _zero_at(pl.program_id(0) == 0, dg_ref)
        d = d_ref[...]
        gate = gate_ref[...]
        dpe_ref[...] = (d * gate).astype(BF16)
        dgl = (d * pe_ref[...] * gate * (1.0 - gate)).astype(BF16)
        dgl_ref[...] = dgl
        dhn = _dot_nt(dgl, w_ref[...])
        dx, dg = _rms_bwd(h_ref[...], g_ref[...], dhn)
        dh2_ref[...] = d + dx
        dg_ref[...] += dg

    return pl.pallas_call(
        body, name="ple_bwd", grid=(t // tm,),
        in_specs=[_row_spec(tm, D_MODEL)] * 4 + [_vec_spec(D_MODEL), _seg_spec(ROWS_B, D_MODEL, 2 * layer + 1),
                                                 pl.BlockSpec(memory_space=pl.ANY)],
        out_specs=[_row_spec(tm, D_MODEL)] * 3 + [_vec_spec(D_MODEL)],
        out_shape=[jax.ShapeDtypeStruct((t, D_MODEL), F32), jax.ShapeDtypeStruct((t, D_MODEL), BF16),
                   jax.ShapeDtypeStruct((t, D_MODEL), BF16), jax.ShapeDtypeStruct((1, D_MODEL), F32)],
        compiler_params=_params("arbitrary"),
    )(dh3, h2, gate, pe, g, wb, after)


def _ffn_down_bwd(dh2, fo, gp, up, g, wa, layer):
    t = dh2.shape[0]
    tm = _tile(t, 256)

    def body(d_ref, fo_ref, gp_ref, up_ref, g_ref, w_ref, dfo_ref, dgp_ref, dup_ref, dg_ref):
        _zero_at(pl.program_id(0) == 0, dg_ref)
        dfo, dg = _rms_bwd(fo_ref[...], g_ref[...], d_ref[...])
        dfo = dfo.astype(BF16)
        dfo_ref[...] = dfo
        dact = _dot_nt(dfo, w_ref[...])
        gp = gp_ref[...]
        sg = _sigmoid(gp)
        dgp_ref[...] = (dact * up_ref[...] * (sg * (1.0 + gp * (1.0 - sg)))).astype(BF16)
        dup_ref[...] = (dact * (gp * sg)).astype(BF16)
        dg_ref[...] += dg

    return pl.pallas_call(
        body, name="ffn_down_bwd", grid=(t // tm,),
        in_specs=[_row_spec(tm, D_MODEL), _row_spec(tm, D_MODEL), _row_spec(tm, D_FF), _row_spec(tm, D_FF),
                  _vec_spec(D_MODEL), _seg_spec(ROWS_A, D_MODEL, 3 * layer + 2)],
        out_specs=[_row_spec(tm, D_MODEL), _row_spec(tm, D_FF), _row_spec(tm, D_FF), _vec_spec(D_MODEL)],
        out_shape=[jax.ShapeDtypeStruct((t, D_MODEL), BF16), jax.ShapeDtypeStruct((t, D_FF), BF16),
                   jax.ShapeDtypeStruct((t, D_FF), BF16), jax.ShapeDtypeStruct((1, D_MODEL), F32)],
        compiler_params=_params("arbitrary"),
    )(dh2, fo, gp, up, g, wa)


def _ffn_up_bwd(dgp, dup, h1, dh2, g, wa, layer, after):
    t = h1.shape[0]
    tm = _tile(t, 256)

    def body(dgp_ref, dup_ref, h_ref, d_ref, g_ref, wg_ref, wu_ref, after_ref, dh1_ref, dg_ref):
        _zero_at(pl.program_id(0) == 0, dg_ref)
        df = _dot(dgp_ref[...], wg_ref[...]) + _dot(dup_ref[...], wu_ref[...])
        dx, dg = _rms_bwd(h_ref[...], g_ref[...], df)
        dh1_ref[...] = d_ref[...] + dx
        dg_ref[...] += dg

    return pl.pallas_call(
        body, name="ffn_up_bwd", grid=(t // tm,),
        in_specs=[_row_spec(tm, D_FF), _row_spec(tm, D_FF), _row_spec(tm, D_MODEL), _row_spec(tm, D_MODEL),
                  _vec_spec(D_MODEL), _seg_spec(ROWS_A, D_MODEL, 3 * layer), _seg_spec(ROWS_A, D_MODEL, 3 * layer + 1),
                  pl.BlockSpec(memory_space=pl.ANY)],
        out_specs=[_row_spec(tm, D_MODEL), _vec_spec(D_MODEL)],
        out_shape=[jax.ShapeDtypeStruct((t, D_MODEL), F32), jax.ShapeDtypeStruct((1, D_MODEL), F32)],
        compiler_params=_params("arbitrary"),
    )(dgp, dup, h1, dh2, g, wa, wa, after)


def _out_proj_bwd(dh1, mix, attn, gm, gpost, ga, gg, wb, layer):
    t = dh1.shape[0]
    tm = _tile(t, 512)

    def body(d_ref, mix_ref, a_ref, m_ref, gp_ref, ga_ref, gg_ref, w_ref,
             dmix_ref, da_ref, dm_ref, dgp_ref, dga_ref, dgg_ref):
        _zero_at(pl.program_id(0) == 0, dgp_ref, dga_ref, dgg_ref)
        dmix, dgp = _rms_bwd(mix_ref[...], gp_ref[...], d_ref[...])
        dmix = dmix.astype(BF16)
        dmix_ref[...] = dmix
        da, dga = _rms_bwd(a_ref[...], ga_ref[...], _dot_nt(dmix, w_ref[0:512, :]))
        dm, dgg = _rms_bwd(m_ref[...], gg_ref[...], _dot_nt(dmix, w_ref[512:1024, :]))
        da_ref[...] = da.astype(BF16)
        dm_ref[...] = dm
        dgp_ref[...] += dgp
        dga_ref[...] += dga
        dgg_ref[...] += dgg

    return pl.pallas_call(
        body, name="out_proj_bwd", grid=(t // tm,),
        in_specs=[_row_spec(tm, D_MODEL), _row_spec(tm, D_MODEL), _row_spec(tm, ATTN_W), _row_spec(tm, GM_W),
                  _vec_spec(D_MODEL), _vec_spec(ATTN_W), _vec_spec(GM_W), _seg_spec(ROWS_B, D_MODEL, 2 * layer)],
        out_specs=[_row_spec(tm, D_MODEL), _row_spec(tm, ATTN_W), _row_spec(tm, GM_W),
                   _vec_spec(D_MODEL), _vec_spec(ATTN_W), _vec_spec(GM_W)],
        out_shape=[jax.ShapeDtypeStruct((t, D_MODEL), BF16), jax.ShapeDtypeStruct((t, ATTN_W), BF16),
                   jax.ShapeDtypeStruct((t, GM_W), F32), jax.ShapeDtypeStruct((1, D_MODEL), F32),
                   jax.ShapeDtypeStruct((1, ATTN_W), F32), jax.ShapeDtypeStruct((1, GM_W), F32)],
        compiler_params=_params("arbitrary"),
    )(dh1, mix, attn, gm, gpost, ga, gg, wb)


def _split3(x):
    hi = x.astype(BF16)
    r1 = x - hi.astype(F32)
    mid = r1.astype(BF16)
    lo = (r1 - mid.astype(F32)).astype(BF16)
    return hi, mid, lo


def _gm_bwd(dgm, zu, zv, lng, lnb, wtril, bsx):
    t = zu.shape[0]
    nb = t // BLK

    def body(d_ref, zu_ref, zv_ref, g_ref, b_ref, w_ref, bs_ref,
             dzu_ref, dzv_ref, dw_ref, dbs_ref, dlg_ref, dlb_ref, dbsx_ref):
        i = pl.program_id(0)
        _zero_at(i == 0, dw_ref, dlg_ref, dlb_ref, dbsx_ref)
        row = lax.broadcasted_iota(jnp.int32, (BLK, BLK), 0)
        lane = lax.broadcasted_iota(jnp.int32, (BLK, BLK), 1)
        low = lane < 64
        tril = row >= lane
        zu = zu_ref[...]
        zv = zv_ref[...]
        lng = g_ref[...]
        gu, ln, xn, rstd, mixed = _gm_forward_block(zu, zv, lng, b_ref[...], w_ref, bs_ref[...], low)
        dgm = d_ref[...]
        dgu_cols, dmx_cols, dln_cols = [], [], []
        for col in range(4):
            sl = slice(col * 128, (col + 1) * 128)
            dg = dgm[:, sl]
            dgu_cols.append(dg * mixed[col])
            dmx = dg * gu[:, sl]
            dmx_cols.append(dmx)
            lc = ln[:, sl]
            halves = (jnp.where(low, lc, 0.0).astype(BF16), jnp.where(low, 0.0, lc).astype(BF16))
            dmx16 = dmx.astype(BF16)
            dmx_half = (jnp.where(low, dmx, 0.0).astype(BF16), jnp.where(low, 0.0, dmx).astype(BF16))
            dln = None
            for half in range(2):
                hd = 2 * col + half
                dw_ref[hd] += jnp.where(tril, _dot_nt(dmx16, halves[half]), 0.0)
                part = _dot_tn(w_ref[hd], dmx_half[half])
                dln = part if dln is None else dln + part
            dln_cols.append(dln)
        dgu = jnp.concatenate(dgu_cols, axis=1)
        dmx = jnp.concatenate(dmx_cols, axis=1)
        dln = jnp.concatenate(dln_cols, axis=1)
        dzu_ref[...] = (dgu * _gelu_grad(zu)).astype(BF16)
        dbsx_ref[...] += dmx
        dlg_ref[...] += jnp.sum(dln * xn, axis=0, keepdims=True)
        dlb_ref[...] += jnp.sum(dln, axis=0, keepdims=True)
        dxn = dln * lng
        dgv = rstd * (dxn - jnp.mean(dxn, axis=-1, keepdims=True) - xn * jnp.mean(dxn * xn, axis=-1, keepdims=True))
        dzv_ref[...] = (dgv * _gelu_grad(zv)).astype(BF16)

        @pl.when(i == nb - 1)
        def _():
            r = lax.broadcasted_iota(jnp.int32, (GM_W, BLK), 0)
            c = lax.broadcasted_iota(jnp.int32, (GM_W, BLK), 1)
            e = jnp.where(jnp.logical_and(r >= c * 64, r < c * 64 + 64), 1.0, 0.0).astype(BF16)
            hi, mid, lo = _split3(dbsx_ref[...])
            dbs_ref[...] = _dot(hi, e) + _dot(mid, e) + _dot(lo, e)

    vec = _vec_spec(GM_W)
    return pl.pallas_call(
        body, name="gm_bwd", grid=(nb,),
        in_specs=[_row_spec(BLK, GM_W)] * 3 + [vec, vec, pl.BlockSpec((8, BLK, BLK), lambda i: (0, 0, 0)),
                                               pl.BlockSpec((BLK, GM_W), lambda i: (0, 0))],
        out_specs=[_row_spec(BLK, GM_W), _row_spec(BLK, GM_W), pl.BlockSpec((8, BLK, BLK), lambda i: (0, 0, 0)),
                   pl.BlockSpec((BLK, BLK), lambda i: (0, 0)), vec, vec],
        out_shape=[jax.ShapeDtypeStruct((t, GM_W), BF16), jax.ShapeDtypeStruct((t, GM_W), BF16),
                   jax.ShapeDtypeStruct((8, BLK, BLK), F32), jax.ShapeDtypeStruct((BLK, BLK), F32),
                   jax.ShapeDtypeStruct((1, GM_W), F32), jax.ShapeDtypeStruct((1, GM_W), F32)],
        scratch_shapes=[pltpu.VMEM((BLK, GM_W), F32)],
        compiler_params=_params("arbitrary"),
    )(dgm, zu, zv, lng, lnb, wtril, bsx)


def _attn_bwd(q, kv, do, sinks, zeros_kv):
    t = q.shape[0]
    cur, prev = _kv_specs()

    def body(sink_ref, q_ref, kvc_ref, kvp_ref, do_ref, zero_ref, dq_ref, dkc_ref, dkp_ref, ds_ref):
        i = pl.program_id(0)
        _zero_at(i == 0, ds_ref)
        low, vcur, dist, valid = _attn_masks(i)
        kc = _head_variants(kvc_ref[:, 0:128].astype(F32), low)
        vc = _head_variants(kvc_ref[:, 128:256].astype(F32), low)
        kp = _head_variants(kvp_ref[:, 0:128].astype(F32), low)
        vp = _head_variants(kvp_ref[:, 128:256].astype(F32), low)
        scores = _attn_scores(q_ref, kc, kp, vcur)
        dprobs = _attn_scores(do_ref, vc, vp, vcur)
        head_row = lax.broadcasted_iota(jnp.int32, (8, 128), 0)
        dsink_tile = jnp.zeros((8, 128), F32)
        parts = []
        for h in range(8):
            p, ps = _attn_probs(scores[h], h, sink_ref[h], dist, valid)
            delta = jnp.sum(p * dprobs[h], axis=1, keepdims=True)
            ds = p * (dprobs[h] - delta) * ATTN_SCALE
            dsink_tile = jnp.where(head_row == h, jnp.sum(-ps * delta, axis=0, keepdims=True), dsink_tile)
            parts.append((jnp.where(vcur, ds, 0.0).astype(BF16), jnp.where(vcur, 0.0, ds).astype(BF16),
                          jnp.where(vcur, p, 0.0).astype(BF16), jnp.where(vcur, 0.0, p).astype(BF16)))
        ds_ref[...] += dsink_tile
        acc = {}

        def add(name, key, val):
            acc[(name, key)] = val if (name, key) not in acc else acc[(name, key)] + val

        for col in range(4):
            dq = None
            qh = q_ref[:, col * 128:(col + 1) * 128]
            doh = do_ref[:, col * 128:(col + 1) * 128]
            for half in range(2):
                h = 2 * col + half
                key = _head_key(h)
                dsc, dsp, pc, pp = parts[h]
                part = _dot(dsc, kc[key]) + _dot(dsp, kp[key])
                dq = part if dq is None else dq + part
                add("kc", key, _dot_tn(dsc, qh))
                add("kp", key, _dot_tn(dsp, qh))
                add("vc", key, _dot_tn(pc, doh))
                add("vp", key, _dot_tn(pp, doh))
            dq_ref[:, col * 128:(col + 1) * 128] = dq.astype(BF16)

        def place(name):
            head0 = acc[(name, (0, 0))] + pltpu.roll(acc[(name, (0, 1))], 64, axis=1)
            head1 = pltpu.roll(acc[(name, (1, 0))], 64, axis=1) + acc[(name, (1, 1))]
            return jnp.where(low, head0, head1)

        dkc_ref[:, 0:128] = place("kc")
        dkc_ref[:, 128:256] = place("vc")
        dkp_ref[:, 0:128] = place("kp")
        dkp_ref[:, 128:256] = place("vp")

    row_q = _row_spec(BLK, ATTN_W)
    return pl.pallas_call(
        body, name="attn_bwd", grid=(t // BLK,),
        in_specs=[pl.BlockSpec(memory_space=pltpu.SMEM), row_q, cur, prev, row_q, pl.BlockSpec(memory_space=pl.ANY)],
        out_specs=[row_q, cur, prev, pl.BlockSpec((8, 128), lambda i: (0, 0))],
        out_shape=[jax.ShapeDtypeStruct((t, ATTN_W), BF16), jax.ShapeDtypeStruct((t, 2 * KV_W), F32),
                   jax.ShapeDtypeStruct((t, 2 * KV_W), F32), jax.ShapeDtypeStruct((8, 128), F32)],
        input_output_aliases={5: 2},
        compiler_params=_params("arbitrary"),
    )(sinks, q, kv, kv, do, zeros_kv)


def _in_proj_bwd(dq, dkc, dkp, dzu, dzv, h, dres, g, wc, layer):
    t = h.shape[0]
    tm = _tile(t, 512)

    def body(dq_ref, dkc_ref, dkp_ref, dzu_ref, dzv_ref, h_ref, d_ref, g_ref, w_ref, dz_ref, dh_ref, dg_ref):
        _zero_at(pl.program_id(0) == 0, dg_ref)
        dq = dq_ref[...]
        dkv = (dkc_ref[...] + dkp_ref[...]).astype(BF16)
        dzu = dzu_ref[...]
        dzv = dzv_ref[...]
        dz_ref[:, 0:512] = dq
        dz_ref[:, 512:768] = dkv
        dz_ref[:, 768:1280] = dzu
        dz_ref[:, 1280:1792] = dzv
        da = (_dot(dq, w_ref[0:512, :]) + _dot(dkv, w_ref[512:768, :]) + _dot(dzu, w_ref[768:1280, :])
              + _dot(dzv, w_ref[1280:1792, :]))
        dx, dg = _rms_bwd(h_ref[...], g_ref[...], da)
        dh_ref[...] = d_ref[...] + dx
        dg_ref[...] += dg

    return pl.pallas_call(
        body, name="in_proj_bwd", grid=(t // tm,),
        in_specs=[_row_spec(tm, ATTN_W), _row_spec(tm, 2 * KV_W), _row_spec(tm, 2 * KV_W), _row_spec(tm, GM_W),
                  _row_spec(tm, GM_W), _row_spec(tm, D_MODEL), _row_spec(tm, D_MODEL), _vec_spec(D_MODEL),
                  _seg_spec(ROWS_C, D_MODEL, layer)],
        out_specs=[_row_spec(tm, D_IN), _row_spec(tm, D_MODEL), _vec_spec(D_MODEL)],
        out_shape=[jax.ShapeDtypeStruct((t, D_IN), BF16), jax.ShapeDtypeStruct((t, D_MODEL), F32),
                   jax.ShapeDtypeStruct((1, D_MODEL), F32)],
        compiler_params=_params("arbitrary"),
    )(dq, dkc, dkp, dzu, dzv, h, dres, g, wc)


def _weight_grad(a, b, buf, seg):
    t, m = a.shape
    n = b.shape[1]
    assert buf.shape[0] % m == 0 and buf.shape[1] == n
    tm = _tile(t, 512)
    steps = t // tm

    def body(a_ref, b_ref, buf_ref, o_ref, acc_ref):
        i = pl.program_id(0)
        _zero_at(i == 0, acc_ref)
        acc_ref[...] += _dot_tn(a_ref[...], b_ref[...].astype(BF16))

        @pl.when(i == steps - 1)
        def _():
            o_ref[...] = acc_ref[...].astype(o_ref.dtype)

    return pl.pallas_call(
        body, name="weight_grad", grid=(steps,),
        in_specs=[_row_spec(tm, m), _row_spec(tm, n), pl.BlockSpec(memory_space=pl.ANY)],
        out_specs=pl.BlockSpec((m, n), lambda i: (seg, 0)),
        out_shape=jax.ShapeDtypeStruct(buf.shape, buf.dtype),
        scratch_shapes=[pltpu.VMEM((m, n), F32)],
        input_output_aliases={2: 0},
        compiler_params=_params("arbitrary"),
    )(a, b, buf)


def _small_pack(d):
    rows = [d["ln_mix_pre"], d["ln_mix_post"], d["ln_ffn_pre"], d["ln_ffn_post"], d["ln_ple_gate"],
            jnp.concatenate([d["gm_ln_g"], d["gm_ln_b"]], axis=1),
            jnp.concatenate([d["g_attn_out"], d["g_gm_out"]], axis=1),
            d["gm_bs"].reshape(1, 1024),
            jnp.pad(d["attn_sinks"].reshape(1, 8), ((0, 0), (0, 1016)))]
    parts = [jnp.pad(r, ((0, 7), (0, 0))) for r in rows] + [d["gm_ws"].reshape(128, 1024)]
    return jnp.concatenate(parts, axis=0)


def _small_unpack(s):
    return {
        "ln_mix_pre": s[:, 0], "ln_mix_post": s[:, 8], "ln_ffn_pre": s[:, 16], "ln_ffn_post": s[:, 24],
        "ln_ple_gate": s[:, 32], "gm_ln_g": s[:, 40, :512], "gm_ln_b": s[:, 40, 512:],
        "g_attn_out": s[:, 48, :512], "g_gm_out": s[:, 48, 512:], "gm_bs": s[:, 56].reshape(DEPTH, 8, 128),
        "attn_sinks": s[:, 64, :8], "gm_ws": s[:, 72:200].reshape(DEPTH, 8, 128, 128),
    }


def _row(v):
    return v.reshape(1, -1)


def _layer_fwd(h, p, sp, l, weights):
    tril = jnp.tril(jnp.ones((BLK, BLK), bool))
    wtril = jnp.where(tril[None], sp["gm_ws"][l], 0.0).astype(BF16)
    bsx = jnp.repeat(sp["gm_bs"][l].T, HEAD_DIM, axis=1)
    a, q, kv, zu, zv = _in_proj(h, _row(sp["ln_mix_pre"][l]), weights("c", h), 0)
    attn = _attn_fwd(q, kv, sp["attn_sinks"][l])
    gm = _gm_fwd(zu, zv, _row(sp["gm_ln_g"][l]), _row(sp["gm_ln_b"][l]), wtril, bsx)
    wb = weights("b", gm)
    heads, mix, h1 = _out_proj(attn, gm, h, _row(sp["g_attn_out"][l]), _row(sp["g_gm_out"][l]),
                               _row(sp["ln_mix_post"][l]), wb, 0)
    wa = weights("a", h1)
    f, gpre, up, act = _ffn_up(h1, _row(sp["ln_ffn_pre"][l]), wa, 0)
    fo, h2 = _ffn_down(act, h1, _row(sp["ln_ffn_post"][l]), wa, 0)
    hn, gate, pe, h3 = _ple(h2, p, _row(sp["ln_ple_gate"][l]), wb, weights("p", gm), 0)
    saved = dict(h=h, a=a, q=q, kv=kv, zu=zu, zv=zv, attn=attn, gm=gm, heads=heads, mix=mix, h1=h1, f=f,
                 gpre=gpre, up=up, act=act, fo=fo, h2=h2, hn=hn, gate=gate, pe=pe, wtril=wtril, bsx=bsx)
    return h3, saved


def _layer_bwd_upper(dh, s, p, sp, l, wa, wb, after):
    d = {}
    dh2, dgl, dpe, d["ln_ple_gate"] = _ple_bwd(dh, s["h2"], s["gate"], s["pe"], _row(sp["ln_ple_gate"][l]), wb, 0, after)
    gb = _weight_grad(s["hn"], dgl, lax.empty((2 * D_MODEL, D_MODEL), BF16), 1)
    gp = _weight_grad(dpe, p, lax.empty((D_MODEL, PLE_DIM), BF16), 0)
    dfo, dgp, dup, d["ln_ffn_post"] = _ffn_down_bwd(dh2, s["fo"], s["gpre"], s["up"], _row(sp["ln_ffn_post"][l]), wa, 0)
    ga = _weight_grad(s["act"], dfo, lax.empty((3 * D_FF, D_MODEL), BF16), 2)
    ga = _weight_grad(dgp, s["f"], ga, 0)
    ga = _weight_grad(dup, s["f"], ga, 1)
    return (dh2, dgp, dup, d), ga, gp, gb


def _layer_bwd_lower(carry, s, sp, l, wa, wb, wc, gb, after):
    dh2, dgp, dup, d = carry
    dh1, d["ln_ffn_pre"] = _ffn_up_bwd(dgp, dup, s["h1"], dh2, _row(sp["ln_ffn_pre"][l]), wa, 0, after)
    dmix, dattn, dgm, d["ln_mix_post"], d["g_attn_out"], d["g_gm_out"] = _out_proj_bwd(
        dh1, s["mix"], s["attn"], s["gm"], _row(sp["ln_mix_post"][l]), _row(sp["g_attn_out"][l]),
        _row(sp["g_gm_out"][l]), wb, 0)
    gb = _weight_grad(s["heads"], dmix, gb, 0)
    dzu, dzv, d["gm_ws"], dbs, d["gm_ln_g"], d["gm_ln_b"] = _gm_bwd(
        dgm, s["zu"], s["zv"], _row(sp["gm_ln_g"][l]), _row(sp["gm_ln_b"][l]), s["wtril"], s["bsx"])
    d["gm_bs"] = dbs[:, :8].T
    zeros_kv = jnp.zeros((dh1.shape[0], 2 * KV_W), F32)
    dq, dkc, dkp, dsink = _attn_bwd(s["q"], s["kv"], dattn, sp["attn_sinks"][l], zeros_kv)
    d["attn_sinks"] = dsink[:, 0]
    dz, dh, d["ln_mix_pre"] = _in_proj_bwd(dq, dkc, dkp, dzu, dzv, s["h"], dh1, _row(sp["ln_mix_pre"][l]), wc, 0)
    gc = _weight_grad(dz, s["a"], lax.empty((D_IN, D_MODEL), BF16), 0)
    return dh, gb, gc, _small_pack(d)


ANY = pl.BlockSpec(memory_space=pl.ANY)


def _place():
    x, y, c = lax.axis_index("x"), lax.axis_index("y"), lax.axis_index("c")
    chips = [(1 - x, y), (x, 1 - y), (1 - x, 1 - y)]
    return x, y, c, chips


def _all_gather(shards):
    n = len(shards)

    def body(*refs):
        ins, outs = refs[:n], refs[n:2 * n]
        send_sems, recv_sems, local_sems = refs[2 * n:]
        x, y, c, chips = _place()
        me, sibling = (x, y, c), (x, y, 1 - c)

        def block(k, px, py, pc):
            return outs[k].at[:, pl.ds(4 * px + 2 * py + pc, 1)]

        def copy(k, j, who, to, src=None):
            return pltpu.make_async_remote_copy(
                src_ref=block(k, *who) if src is None else src, dst_ref=block(k, *who),
                send_sem=send_sems.at[7 * k + j], recv_sem=recv_sems.at[7 * k + j],
                device_id=to, device_id_type=MESH)

        mine = [pltpu.make_async_copy(ins[k], block(k, *me), local_sems.at[k]) for k in range(n)]
        for cp in mine:
            cp.start()
        first = []
        for k in range(n):
            first.append(copy(k, 0, me, sibling, src=ins[k]))
            first += [copy(k, 1 + j, me, (*chip, c), src=ins[k]) for j, chip in enumerate(chips)]
        for cp in first:
            cp.start()
        passed = []
        for j, chip in enumerate(chips):
            for k in range(n):
                copy(k, 1 + j, (*chip, c), me).wait_recv()
                cp = copy(k, 4 + j, (*chip, c), sibling)
                cp.start()
                passed.append(cp)
        for k in range(n):
            copy(k, 0, sibling, me).wait_recv()
            for j, chip in enumerate(chips):
                copy(k, 4 + j, (*chip, 1 - c), me).wait_recv()
        for cp in first + passed:
            cp.wait_send()
        for cp in mine:
            cp.wait()

    return pl.pallas_call(
        body, name="all_gather_weights",
        in_specs=[ANY] * n, out_specs=[ANY] * n,
        out_shape=[jax.ShapeDtypeStruct((s.shape[0], N_DEV) + s.shape[2:], s.dtype) for s in shards],
        scratch_shapes=[pltpu.SemaphoreType.DMA((7 * n,)), pltpu.SemaphoreType.DMA((7 * n,)),
                        pltpu.SemaphoreType.DMA((n,))],
        compiler_params=pltpu.CompilerParams(has_side_effects=True),
    )(*shards)


def _sibling_exchange(bufs, small):
    n = len(bufs)

    def body(*refs):
        ins, small_ref = refs[:n], refs[n]
        outs, both_ref = refs[n + 1:2 * n + 1], refs[2 * n + 1]
        send_sems, recv_sems, local_sem = refs[2 * n + 2:]
        x, y, c, _ = _place()
        sibling = (x, y, 1 - c)
        mine = pltpu.make_async_copy(small_ref, both_ref.at[c], local_sem)
        mine.start()
        copies = [pltpu.make_async_remote_copy(
            src_ref=ins[k].at[:, :, pl.ds(1 - c, 1)], dst_ref=outs[k], send_sem=send_sems.at[k],
            recv_sem=recv_sems.at[k], device_id=sibling, device_id_type=MESH) for k in range(n)]
        copies.append(pltpu.make_async_remote_copy(
            src_ref=small_ref, dst_ref=both_ref.at[c], send_sem=send_sems.at[n], recv_sem=recv_sems.at[n],
            device_id=sibling, device_id_type=MESH))
        for cp in copies:
            cp.start()
        for k in range(n):
            copies[k].wait_recv()
        pltpu.make_async_remote_copy(
            src_ref=small_ref, dst_ref=both_ref.at[1 - c], send_sem=send_sems.at[n], recv_sem=recv_sems.at[n],
            device_id=sibling, device_id_type=MESH).wait_recv()
        for cp in copies:
            cp.wait_send()
        mine.wait()

    return pl.pallas_call(
        body, name="sibling_exchange",
        in_specs=[ANY] * (n + 1), out_specs=[ANY] * (n + 1),
        out_shape=[jax.ShapeDtypeStruct(b.shape[:2] + (1,) + b.shape[3:], b.dtype) for b in bufs]
        + [jax.ShapeDtypeStruct((2,) + small.shape, small.dtype)],
        scratch_shapes=[pltpu.SemaphoreType.DMA((n + 1,)), pltpu.SemaphoreType.DMA((n + 1,)), pltpu.SemaphoreType.DMA],
        compiler_params=pltpu.CompilerParams(has_side_effects=True),
    )(*bufs, small)


def _chip_exchange(sends, small):
    n = len(sends)

    def body(*refs):
        ins, small_ref = refs[:n], refs[n]
        outs, all_ref = refs[n + 1:2 * n + 1], refs[2 * n + 1]
        send_sems, recv_sems, local_sem = refs[2 * n + 2:]
        x, y, c, chips = _place()
        mine = pltpu.make_async_copy(small_ref, all_ref.at[2 * x + y], local_sem)
        mine.start()
        copies = []
        for j, chip in enumerate(chips):
            for k in range(n):
                copies.append(pltpu.make_async_remote_copy(
                    src_ref=ins[k].at[j], dst_ref=outs[k].at[j], send_sem=send_sems.at[3 * k + j],
                    recv_sem=recv_sems.at[3 * k + j], device_id=(*chip, c), device_id_type=MESH))
            copies.append(pltpu.make_async_remote_copy(
                src_ref=small_ref, dst_ref=all_ref.at[2 * x + y], send_sem=send_sems.at[3 * n + j],
                recv_sem=recv_sems.at[3 * n + j], device_id=(*chip, c), device_id_type=MESH))
        for cp in copies:
            cp.start()
        for j, (px, py) in enumerate(chips):
            for k in range(n):
                copies[j * (n + 1) + k].wait_recv()
            pltpu.make_async_remote_copy(
                src_ref=small_ref, dst_ref=all_ref.at[2 * px + py], send_sem=send_sems.at[3 * n + j],
                recv_sem=recv_sems.at[3 * n + j], device_id=(px, py, c), device_id_type=MESH).wait_recv()
        for cp in copies:
            cp.wait_send()
        mine.wait()

    return pl.pallas_call(
        body, name="chip_exchange",
        in_specs=[ANY] * (n + 1), out_specs=[ANY] * (n + 1),
        out_shape=[jax.ShapeDtypeStruct(s.shape, s.dtype) for s in sends]
        + [jax.ShapeDtypeStruct((4,) + small.shape, small.dtype)],
        scratch_shapes=[pltpu.SemaphoreType.DMA((3 * n + 3,)), pltpu.SemaphoreType.DMA((3 * n + 3,)),
                        pltpu.SemaphoreType.DMA],
        compiler_params=pltpu.CompilerParams(has_side_effects=True),
    )(*sends, small)


def _pair_add(buf, got, chip_ids, dtype):
    nseg, _, _, rows, cols = buf.shape
    nr = chip_ids.shape[0] - 1

    def body(ids_ref, a_ref, b_ref, o_ref):
        o_ref[...] = (a_ref[...] + b_ref[...]).astype(dtype)

    return pl.pallas_call(
        body, name="pair_add",
        grid_spec=pltpu.PrefetchScalarGridSpec(
            num_scalar_prefetch=1, grid=(nr, nseg),
            in_specs=[pl.BlockSpec((None, None, None, rows, cols), lambda r, s, ids: (s, ids[r], ids[nr], 0, 0)),
                      pl.BlockSpec((None, None, None, rows, cols), lambda r, s, ids: (s, ids[r], 0, 0, 0))],
            out_specs=pl.BlockSpec((None, None, rows, cols), lambda r, s, ids: (r, s, 0, 0))),
        out_shape=jax.ShapeDtypeStruct((nr, nseg, rows, cols), dtype),
        compiler_params=_params("parallel", "parallel"),
    )(chip_ids, buf, got)


def _sum_slots(z, tr):
    n, rows, cols = z.shape

    def body(z_ref, o_ref):
        s = z_ref[0]
        for k in range(1, n):
            s = s + z_ref[k]
        o_ref[...] = s

    return pl.pallas_call(
        body, name="sum_slots", grid=(rows // tr,),
        in_specs=[pl.BlockSpec((n, tr, cols), lambda i: (0, i, 0))],
        out_specs=pl.BlockSpec((tr, cols), lambda i: (i, 0)),
        out_shape=jax.ShapeDtypeStruct((rows, cols), F32),
        compiler_params=_params("parallel"),
    )(z)


def _final_sum(own, got):
    _, nseg, rows, cols = own.shape

    def body(a_ref, b_ref, o_ref):
        s = a_ref[0]
        for k in range(3):
            s = s + b_ref[k].astype(F32)
        o_ref[...] = s

    return pl.pallas_call(
        body, name="final_sum", grid=(nseg,),
        in_specs=[pl.BlockSpec((1, None, rows, cols), lambda i: (0, i, 0, 0)),
                  pl.BlockSpec((3, None, rows, cols), lambda i: (0, i, 0, 0))],
        out_specs=pl.BlockSpec((None, rows, cols), lambda i: (i, 0, 0)),
        out_shape=jax.ShapeDtypeStruct((nseg, rows, cols), F32),
        compiler_params=_params("parallel"),
    )(own, got)


HBM = pl.BlockSpec(memory_space=pltpu.HBM)
SEM = pl.BlockSpec(memory_space=pltpu.SEMAPHORE)
N_PEERS = N_DEV - 1


def _peers():
    x, y, c = lax.axis_index("x"), lax.axis_index("y"), lax.axis_index("c")
    peers = []
    for r in range(1, N_DEV):
        px = 1 - x if r & 4 else x
        py = 1 - y if r & 2 else y
        pc = 1 - c if r & 1 else c
        peers.append(((px, py, pc), 4 * px + 2 * py + pc))
    return 4 * x + 2 * y + c, peers


def _peer_copy(src, land, send_sems, recv_sems, r, me, peer, peer_slot, scatter):
    return pltpu.make_async_remote_copy(
        src_ref=src.at[:, pl.ds(peer_slot, 1)] if scatter else src, dst_ref=land.at[:, pl.ds(me, 1)],
        send_sem=send_sems.at[r - 1], recv_sem=recv_sems.at[r - 1], device_id=peer, device_id_type=MESH)


def _peer_arrival(src, land, send_sems, recv_sems, r, me, peer, peer_slot, scatter):
    return pltpu.make_async_remote_copy(
        src_ref=src.at[:, pl.ds(me, 1)] if scatter else src, dst_ref=land.at[:, pl.ds(peer_slot, 1)],
        send_sem=send_sems.at[r - 1], recv_sem=recv_sems.at[r - 1], device_id=peer, device_id_type=MESH)


def _fill_own(shards, lands):
    n = len(shards)

    def body(*refs):
        ins, outs, sems = refs[:n], refs[2 * n:3 * n], refs[3 * n]
        me, _ = _peers()
        copies = [pltpu.make_async_copy(ins[k], outs[k].at[:, pl.ds(me, 1)], sems.at[k]) for k in range(n)]
        for cp in copies:
            cp.start()
        for cp in copies:
            cp.wait()

    return pl.pallas_call(
        body, name="fill_own", in_specs=[ANY] * (2 * n), out_specs=[ANY] * n,
        out_shape=[jax.ShapeDtypeStruct(a.shape, a.dtype) for a in lands],
        scratch_shapes=[pltpu.SemaphoreType.DMA((n,))],
        input_output_aliases={n + k: k for k in range(n)},
    )(*shards, *lands)


def _send_start(name, srcs, lands, scatter):
    n = len(srcs)

    def body(*refs):
        src_refs, land_refs = refs[:n], refs[n:2 * n]
        outs = refs[2 * n:]
        send_sems, recv_sems, token = outs[2 * n:3 * n], outs[3 * n:4 * n], outs[4 * n]
        me, peers = _peers()
        for k in range(n):
            for r, (peer, slot) in enumerate(peers, 1):
                _peer_copy(src_refs[k], land_refs[k], send_sems[k], recv_sems[k], r, me, peer, slot, scatter[k]).start()
        token[...] = jnp.zeros_like(token)

    hbm = lambda a: pltpu.HBM(a.shape, a.dtype)
    sems = [pltpu.SemaphoreType.DMA((N_PEERS,))] * (2 * n)
    outs = pl.pallas_call(
        body, name=name, in_specs=[HBM] * (2 * n),
        out_specs=[HBM] * (2 * n) + [SEM] * (2 * n) + [pl.BlockSpec(memory_space=pltpu.VMEM)],
        out_shape=[hbm(a) for a in srcs] + [hbm(a) for a in lands] + sems + [jax.ShapeDtypeStruct((8, 128), F32)],
        input_output_aliases={k: k for k in range(2 * n)},
        compiler_params=pltpu.CompilerParams(has_side_effects=pltpu.SideEffectType.DATAFLOW_SIDE_EFFECTING),
    )(*[pltpu.with_memory_space_constraint(a, pltpu.HBM) for a in list(srcs) + list(lands)])
    return dict(srcs=outs[:n], lands=outs[n:2 * n], send=outs[2 * n:3 * n], recv=outs[3 * n:4 * n],
                scatter=list(scatter)), outs[4 * n]


def _send_wait(name, sent, ks, after):
    n = len(ks)
    srcs = [sent["srcs"][k] for k in ks]
    lands = [sent["lands"][k] for k in ks]
    scatter = [sent["scatter"][k] for k in ks]

    def body(*refs):
        src_refs, land_refs = refs[:n], refs[n:2 * n]
        send_sems, recv_sems = refs[2 * n:3 * n], refs[3 * n:4 * n]
        me, peers = _peers()
        for k in range(n):
            for r, (peer, slot) in enumerate(peers, 1):
                args = (src_refs[k], land_refs[k], send_sems[k], recv_sems[k], r, me, peer, slot, scatter[k])
                _peer_copy(*args).wait_send()
                _peer_arrival(*args).wait_recv()

    hbm = lambda a: pltpu.HBM(a.shape, a.dtype)
    outs = pl.pallas_call(
        body, name=name, in_specs=[HBM] * (2 * n) + [SEM] * (2 * n) + [ANY],
        out_specs=[HBM] * (2 * n), out_shape=[hbm(a) for a in srcs] + [hbm(a) for a in lands],
        input_output_aliases={k: k for k in range(2 * n)},
        compiler_params=pltpu.CompilerParams(has_side_effects=pltpu.SideEffectType.DATAFLOW_SIDE_EFFECTING),
    )(*srcs, *lands, *[sent["send"][k] for k in ks], *[sent["recv"][k] for k in ks], after)
    return outs[n:], outs[:n]


def _sum_blocks(own, land, ids):
    nseg, _, rows, cols = land.shape

    def body(ids_ref, own_ref, land_ref, o_ref):
        me = ids_ref[1]
        total = None
        for j in range(N_DEV):
            term = jnp.where(me == j, own_ref[...], land_ref[j]).astype(F32)
            total = term if total is None else total + term
        o_ref[...] = total

    return pl.pallas_call(
        body, name="sum_blocks",
        grid_spec=pltpu.PrefetchScalarGridSpec(
            num_scalar_prefetch=1, grid=(nseg,),
            in_specs=[pl.BlockSpec((None, None, rows, cols), lambda s, ids: (s, ids[0], 0, 0)),
                      pl.BlockSpec((None, N_DEV, rows, cols), lambda s, ids: (s, 0, 0, 0))],
            out_specs=pl.BlockSpec((None, rows, cols), lambda s, ids: (s, 0, 0))),
        out_shape=jax.ShapeDtypeStruct((nseg, rows, cols), F32),
        compiler_params=_params("parallel"),
    )(ids, own, land)


def _adamw(w, g, m, v):
    shape = w.shape
    cols = shape[-1]
    rows = w.size // cols
    tr = rows
    for cand in (512, 256, 128, 64, 32, 16, 8):
        if rows % cand == 0:
            tr = cand
            break
    c1 = 1.0 / (1.0 - ADAM_B1 ** ADAM_STEP)
    c2 = 1.0 / (1.0 - ADAM_B2 ** ADAM_STEP)

    def body(w_ref, g_ref, m_ref, v_ref, d_ref, nm_ref, nv_ref):
        g = g_ref[...]
        m = ADAM_B1 * m_ref[...] + (1.0 - ADAM_B1) * g
        v = ADAM_B2 * v_ref[...] + (1.0 - ADAM_B2) * (g * g)
        nm_ref[...] = m
        nv_ref[...] = v
        d_ref[...] = -ADAM_LR * ((m * c1) / (jnp.sqrt(v * c2) + ADAM_EPS) + ADAM_WD * w_ref[...])

    spec = pl.BlockSpec((tr, cols), lambda i: (i, 0))
    outs = pl.pallas_call(
        body, name="adamw", grid=(rows // tr,),
        in_specs=[spec] * 4, out_specs=[spec] * 3,
        out_shape=[jax.ShapeDtypeStruct((rows, cols), F32)] * 3,
        compiler_params=_params("parallel"),
    )(*[a.reshape(rows, cols) for a in (w, g, m, v)])
    return [o.reshape(shape) for o in outs]


SMALL = ("ln_mix_pre", "attn_sinks", "gm_ln_g", "gm_ln_b", "gm_ws", "gm_bs", "g_attn_out", "g_gm_out",
         "ln_mix_post", "ln_ffn_pre", "ln_ffn_post", "ln_ple_gate")
WEIGHTS = ("ln_mix_pre", "w_in", "attn_sinks", "gm_ln_g", "gm_ln_b", "gm_ws", "gm_bs", "g_attn_out", "g_gm_out",
           "w_out", "ln_mix_post", "ln_ffn_pre", "w_ffn_gate", "w_ffn_up", "w_ffn_down", "ln_ffn_post", "w_ple",
           "ln_ple_gate", "w_ple_gate")


def _pack_shards(w, l):
    sa = jnp.stack([w["w_ffn_gate"][l].T, w["w_ffn_up"][l].T, w["w_ffn_down"][l]])[:, None]
    sb = jnp.stack([w["w_out"][l], w["w_ple_gate"][l]])[:, None]
    return [w["w_in"][l].T[None, None].astype(BF16), sb.astype(BF16), w["w_ple"][l].T[None, None].astype(BF16),
            sa.astype(BF16)]


def _unpack_grads(rc, rb, rp, ra):
    return {"w_in": rc[0].T, "w_out": rb[0], "w_ple_gate": rb[1], "w_ple": rp[0].T,
            "w_ffn_gate": ra[0].T, "w_ffn_up": ra[1].T, "w_ffn_down": ra[2]}


def kernel(x, p, ln_mix_pre, w_in, attn_sinks, gm_ln_g, gm_ln_b, gm_ws, gm_bs, g_attn_out, g_gm_out, w_out, ln_mix_post, ln_ffn_pre, w_ffn_gate, w_ffn_up, w_ffn_down, ln_ffn_post, w_ple, ln_ple_gate, w_ple_gate, loss_target, m_ln_mix_pre, m_w_in, m_attn_sinks, m_gm_ln_g, m_gm_ln_b, m_gm_ws, m_gm_bs, m_g_attn_out, m_g_gm_out, m_w_out, m_ln_mix_post, m_ln_ffn_pre, m_w_ffn_gate, m_w_ffn_up, m_w_ffn_down, m_ln_ffn_post, m_w_ple, m_ln_ple_gate, m_w_ple_gate, v_ln_mix_pre, v_w_in, v_attn_sinks, v_gm_ln_g, v_gm_ln_b, v_gm_ws, v_gm_bs, v_g_attn_out, v_g_gm_out, v_w_out, v_ln_mix_post, v_ln_ffn_pre, v_w_ffn_gate, v_w_ffn_up, v_w_ffn_down, v_ln_ffn_post, v_w_ple, v_ln_ple_gate, v_w_ple_gate):
    given = dict(locals())
    w = {n: given[n] for n in WEIGHTS}
    sp = {n: w[n] for n in SMALL}
    kinds = ("c", "b", "p", "a")

    shards = [s for l in range(DEPTH) for s in _pack_shards(w, l)]
    lands = _fill_own(shards, [lax.empty((s.shape[0], N_DEV) + s.shape[2:], BF16) for s in shards])
    gather, token = _send_start("gather_start", shards, lands, [False] * len(shards))
    layer_weights = [{} for _ in range(DEPTH)]

    def weights_of(l):
        def get(kind, after):
            have = layer_weights[l]
            if kind not in have:
                if l == 0:
                    group = {"c": ("c",), "b": ("b", "p"), "p": ("b", "p"), "a": ("a",)}[kind]
                    after = token if kind == "c" else after
                else:
                    group = kinds
                got, _ = _send_wait(f"gather_wait_{l}{group[0]}", gather, [4 * l + kinds.index(k) for k in group], after)
                for k, g in zip(group, got):
                    have[k] = g.reshape(-1, g.shape[-1])
            return have[kind]
        return get

    h = x[0]
    saved = []
    for l in range(DEPTH):
        h, s = _layer_fwd(h, p[l, 0], sp, l, weights_of(l))
        saved.append(s)
    dh, sq = _loss_head(h, loss_target[0])

    reduces = []
    after = token
    view = lambda g, rows: g.reshape(-1, N_DEV, rows, g.shape[-1])
    zeros_like = lambda a: jnp.zeros((a.shape[0], N_DEV) + a.shape[2:], BF16)
    for l in reversed(range(DEPTH)):
        lw = layer_weights[l]
        carry, ga, gp, gb = _layer_bwd_upper(dh, saved[l], p[l, 0], sp, l, lw["a"], lw["b"], after)
        first = [view(ga, ROWS_A), view(gp, ROWS_B)]
        sent1, tok1 = _send_start(f"reduce_start_{l}a", first, [zeros_like(a) for a in first], [True, True])
        dh, gb, gc, small = _layer_bwd_lower(carry, saved[l], sp, l, lw["a"], lw["b"], lw["c"], gb, tok1)
        second = [view(gb, ROWS_B), view(gc, ROWS_C), small.astype(BF16)[None, None]]
        sent2, after = _send_start(f"reduce_start_{l}b", second, [zeros_like(a) for a in second], [True, True, False])
        reduces.append((l, sent1, sent2))

    me, _ = _peers()
    mine = jnp.stack([me, me]).astype(jnp.int32)
    whole = jnp.stack([jnp.zeros_like(me), me]).astype(jnp.int32)
    per_layer = [None] * DEPTH
    small_sums = [None] * DEPTH
    for l, sent1, sent2 in reduces:
        (la, lp), (ga, gp) = _send_wait(f"reduce_wait_{l}a", sent1, [0, 1], dh)
        (lb, lc, ls), (gb, gc, gs) = _send_wait(f"reduce_wait_{l}b", sent2, [0, 1, 2], dh)
        per_layer[l] = _unpack_grads(_sum_blocks(gc, lc, mine), _sum_blocks(gb, lb, mine), _sum_blocks(gp, lp, mine),
                                     _sum_blocks(ga, la, mine))
        small_sums[l] = _sum_blocks(gs, ls, whole)[0]
    grads = {n: jnp.stack([per_layer[l][n] for l in range(DEPTH)]) for n in per_layer[0]}
    grads.update(_small_unpack(jnp.stack(small_sums)))
    grad_x = dh
    loss = lax.psum(sq[0, 0] * (0.5 / D_MODEL), AXES)
    delta, new_m, new_v = {}, {}, {}
    for n in WEIGHTS:
        delta[n], new_m[n], new_v[n] = _adamw(w[n], grads[n], given["m_" + n], given["v_" + n])
    return (loss, grad_x[None], *[grads[n] for n in WEIGHTS], *[delta[n] for n in WEIGHTS],
            *[new_m[n] for n in WEIGHTS], *[new_v[n] for n in WEIGHTS])
```

```python
import math

import jax
import jax.numpy as jnp
from jax import lax
from jax.experimental import pallas as pl
from jax.experimental.pallas import tpu as pltpu

F32 = jnp.float32
BF16 = jnp.bfloat16
MESH = pl.DeviceIdType.MESH
AXES = ("x", "y", "c")

D_MODEL = 1024
DEPTH = 4
N_DEV = 8
HEAD_DIM = 64
ATTN_W = 512
KV_W = 128
GM_W = 512
D_IN = 1792
D_FF = 2816
PLE_DIM = 256
BLK = 128
NORM_EPS = 1e-6
NEG_BIG = -1e30
ATTN_SCALE = HEAD_DIM ** -0.5

ADAM_LR = 0.001
ADAM_B1 = 0.9
ADAM_B2 = 0.999
ADAM_EPS = 1e-08
ADAM_WD = 0.01
ADAM_STEP = 10

ROWS_A = D_FF // N_DEV
ROWS_B = D_MODEL // N_DEV
ROWS_C = D_IN // N_DEV
GATING_ROWS = 144
REST_ROWS = 56
SMALL_ROWS = 200

VMEM_LIMIT = 56 * 2 ** 20


def _params(*sem):
    return pltpu.CompilerParams(dimension_semantics=sem, vmem_limit_bytes=VMEM_LIMIT)


def _dot(a, b):
    return jnp.dot(a, b, preferred_element_type=F32)


def _dot_nt(a, b):
    return lax.dot_general(a, b, (((1,), (1,)), ((), ())), preferred_element_type=F32)


def _dot_tn(a, b):
    return lax.dot_general(a, b, (((0,), (0,)), ((), ())), preferred_element_type=F32)


def _rms_fwd(x, g):
    r = lax.rsqrt(jnp.mean(x * x, axis=-1, keepdims=True) + NORM_EPS)
    return x * r * g


def _rms_bwd(x, g, dy):
    r = lax.rsqrt(jnp.mean(x * x, axis=-1, keepdims=True) + NORM_EPS)
    xh = x * r
    dg = jnp.sum(dy * xh, axis=0, keepdims=True)
    dxh = dy * g
    dx = r * (dxh - xh * jnp.mean(dxh * xh, axis=-1, keepdims=True))
    return dx, dg


_GELU_C = math.sqrt(2.0 / math.pi)


def _gelu(x):
    t = jnp.tanh(_GELU_C * (x + 0.044715 * (x * x * x)))
    return 0.5 * x * (1.0 + t)


def _gelu_grad(x):
    x2 = x * x
    t = jnp.tanh(_GELU_C * (x + 0.044715 * (x2 * x)))
    return 0.5 * (1.0 + t) + 0.5 * x * (1.0 - t * t) * (_GELU_C * (1.0 + 3.0 * 0.044715 * x2))


def _sigmoid(x):
    return 1.0 / (1.0 + jnp.exp(-x))


def _row_spec(tm, n):
    return pl.BlockSpec((tm, n), lambda i: (i, 0))


def _vec_spec(n):
    return pl.BlockSpec((1, n), lambda i: (0, 0))


def _seg_spec(rows, cols, seg):
    return pl.BlockSpec((N_DEV * rows, cols), lambda i: (seg, 0))


def _zero_at(first, *refs):
    @pl.when(first)
    def _():
        for r in refs:
            r[...] = jnp.zeros(r.shape, r.dtype)


def _tile(t, want):
    return min(t, want)


def _in_proj(h, g, wc, layer):
    t = h.shape[0]
    tm = _tile(t, 512)

    def body(h_ref, g_ref, w_ref, a_ref, q_ref, kv_ref, zu_ref, zv_ref):
        a = _rms_fwd(h_ref[...], g_ref[...]).astype(BF16)
        a_ref[...] = a
        q_ref[...] = _dot_nt(a, w_ref[0:512, :]).astype(BF16)
        kv_ref[...] = _dot_nt(a, w_ref[512:768, :]).astype(BF16)
        zu_ref[...] = _dot_nt(a, w_ref[768:1280, :])
        zv_ref[...] = _dot_nt(a, w_ref[1280:1792, :])

    return pl.pallas_call(
        body, name="in_proj", grid=(t // tm,),
        in_specs=[_row_spec(tm, D_MODEL), _vec_spec(D_MODEL), _seg_spec(ROWS_C, D_MODEL, layer)],
        out_specs=[_row_spec(tm, D_MODEL), _row_spec(tm, ATTN_W), _row_spec(tm, 2 * KV_W),
                   _row_spec(tm, GM_W), _row_spec(tm, GM_W)],
        out_shape=[jax.ShapeDtypeStruct((t, D_MODEL), BF16), jax.ShapeDtypeStruct((t, ATTN_W), BF16),
                   jax.ShapeDtypeStruct((t, 2 * KV_W), BF16), jax.ShapeDtypeStruct((t, GM_W), F32),
                   jax.ShapeDtypeStruct((t, GM_W), F32)],
        compiler_params=_params("parallel"),
    )(h, g, wc)


def _head_variants(x, low):
    xr = pltpu.roll(x, 64, axis=1)
    zero = jnp.zeros_like(x)
    return {
        (0, 0): jnp.where(low, x, zero).astype(BF16),
        (0, 1): jnp.where(low, zero, xr).astype(BF16),
        (1, 0): jnp.where(low, xr, zero).astype(BF16),
        (1, 1): jnp.where(low, zero, x).astype(BF16),
    }


def _attn_masks(i):
    row = lax.broadcasted_iota(jnp.int32, (BLK, BLK), 0)
    lane = lax.broadcasted_iota(jnp.int32, (BLK, BLK), 1)
    vcur = row >= lane
    dist = jnp.where(vcur, row - lane, row - lane + BLK).astype(F32)
    valid = jnp.logical_or(vcur, i > 0)
    return lane < 64, vcur, dist, valid


def _head_key(h):
    return (h // 4, h % 2)


def _attn_scores(q_ref, rows, kc, kp, vcur):
    out = []
    for h in range(8):
        qh = q_ref[rows, (h // 2) * 128:(h // 2 + 1) * 128]
        out.append(jnp.where(vcur, _dot_nt(qh, kc[_head_key(h)]), _dot_nt(qh, kp[_head_key(h)])))
    return out


def _attn_probs(s, h, sink, dist, valid):
    s = s * ATTN_SCALE - (2.0 ** -(h + 1)) * dist
    if valid is not None:
        s = jnp.where(valid, s, NEG_BIG)
    m = jnp.maximum(jnp.max(s, axis=1, keepdims=True), sink)
    e = jnp.exp(s - m)
    es = jnp.exp(sink - m)
    inv = 1.0 / (jnp.sum(e, axis=1, keepdims=True) + es)
    return e * inv, es * inv


def _kv_prev_spec(blocks):
    return pl.BlockSpec((BLK, 2 * KV_W), lambda i: (jnp.maximum(i * blocks - 1, 0), 0))


def _kv_variants(kv_ref, rows, low):
    return (_head_variants(kv_ref[rows, 0:128].astype(F32), low), _head_variants(kv_ref[rows, 128:256].astype(F32), low))


def _attn_fwd(q, kv, sinks):
    t = q.shape[0]
    tq = _tile(t, 512)
    blocks = tq // BLK

    def body(sink_ref, q_ref, kvc_ref, kvp_ref, o_ref):
        low, vcur, dist, valid = _attn_masks(pl.program_id(0))
        kp, vp = _kv_variants(kvp_ref, slice(None), low)
        for b in range(blocks):
            rows = slice(b * BLK, (b + 1) * BLK)
            kc, vc = _kv_variants(kvc_ref, rows, low)
            scores = _attn_scores(q_ref, rows, kc, kp, vcur)
            probs = [_attn_probs(scores[h], h, sink_ref[h], dist, valid if b == 0 else None)[0] for h in range(8)]
            for col in range(4):
                acc = None
                for half in range(2):
                    h = 2 * col + half
                    p = probs[h]
                    o = (_dot(jnp.where(vcur, p, 0.0).astype(BF16), vc[_head_key(h)])
                         + _dot(jnp.where(vcur, 0.0, p).astype(BF16), vp[_head_key(h)]))
                    acc = o if acc is None else acc + o
                o_ref[rows, col * 128:(col + 1) * 128] = acc
            kp, vp = kc, vc

    return pl.pallas_call(
        body, name="attn_fwd", grid=(t // tq,),
        in_specs=[pl.BlockSpec(memory_space=pltpu.SMEM), _row_spec(tq, ATTN_W), _row_spec(tq, 2 * KV_W),
                  _kv_prev_spec(blocks)],
        out_specs=_row_spec(tq, ATTN_W),
        out_shape=jax.ShapeDtypeStruct((t, ATTN_W), F32),
        compiler_params=_params("parallel"),
    )(sinks, q, kv, kv)


def _gm_forward_block(zu, zv, lng, lnb, w_ref, bsx, low):
    gu = _gelu(zu)
    gv = _gelu(zv)
    mu = jnp.mean(gv, axis=-1, keepdims=True)
    xc = gv - mu
    rstd = lax.rsqrt(jnp.mean(xc * xc, axis=-1, keepdims=True) + NORM_EPS)
    xn = xc * rstd
    ln = xn * lng + lnb
    mixed = []
    for col in range(4):
        lc = ln[:, col * 128:(col + 1) * 128]
        lo = jnp.where(low, lc, 0.0).astype(BF16)
        hi = jnp.where(low, 0.0, lc).astype(BF16)
        mixed.append(_dot(w_ref[2 * col], lo) + _dot(w_ref[2 * col + 1], hi) + bsx[:, col * 128:(col + 1) * 128])
    return gu, ln, xn, rstd, mixed


def _gm_fwd(zu, zv, lng, lnb, wtril, bsx):
    t = zu.shape[0]
    tm = _tile(t, 512)

    def body(zu_ref, zv_ref, g_ref, b_ref, w_ref, bs_ref, o_ref):
        low = lax.broadcasted_iota(jnp.int32, (BLK, BLK), 1) < 64
        for b in range(tm // BLK):
            rows = slice(b * BLK, (b + 1) * BLK)
            gu, _, _, _, mixed = _gm_forward_block(zu_ref[rows, :], zv_ref[rows, :], g_ref[...], b_ref[...], w_ref,
                                                   bs_ref[...], low)
            for col in range(4):
                o_ref[rows, col * 128:(col + 1) * 128] = gu[:, col * 128:(col + 1) * 128] * mixed[col]

    return pl.pallas_call(
        body, name="gm_fwd", grid=(t // tm,),
        in_specs=[_row_spec(tm, GM_W), _row_spec(tm, GM_W), _vec_spec(GM_W), _vec_spec(GM_W),
                  pl.BlockSpec((8, BLK, BLK), lambda i: (0, 0, 0)), pl.BlockSpec((BLK, GM_W), lambda i: (0, 0))],
        out_specs=_row_spec(tm, GM_W),
        out_shape=jax.ShapeDtypeStruct((t, GM_W), F32),
        compiler_params=_params("parallel"),
    )(zu, zv, lng, lnb, wtril, bsx)


def _out_proj(attn, gm, h, ga, gg, gpost, wb, layer):
    t = h.shape[0]
    tm = _tile(t, 512)

    def body(a_ref, m_ref, h_ref, ga_ref, gg_ref, gp_ref, w_ref, heads_ref, mix_ref, h1_ref):
        ha = _rms_fwd(a_ref[...], ga_ref[...]).astype(BF16)
        hg = _rms_fwd(m_ref[...], gg_ref[...]).astype(BF16)
        heads_ref[:, 0:512] = ha
        heads_ref[:, 512:1024] = hg
        mix = _dot(ha, w_ref[0:512, :]) + _dot(hg, w_ref[512:1024, :])
        mix_ref[...] = mix
        h1_ref[...] = h_ref[...] + _rms_fwd(mix, gp_ref[...])

    return pl.pallas_call(
        body, name="out_proj", grid=(t // tm,),
        in_specs=[_row_spec(tm, ATTN_W), _row_spec(tm, GM_W), _row_spec(tm, D_MODEL), _vec_spec(ATTN_W),
                  _vec_spec(GM_W), _vec_spec(D_MODEL), _seg_spec(ROWS_B, D_MODEL, 2 * layer)],
        out_specs=[_row_spec(tm, D_MODEL), _row_spec(tm, D_MODEL), _row_spec(tm, D_MODEL)],
        out_shape=[jax.ShapeDtypeStruct((t, D_MODEL), BF16), jax.ShapeDtypeStruct((t, D_MODEL), F32),
                   jax.ShapeDtypeStruct((t, D_MODEL), F32)],
        compiler_params=_params("parallel"),
    )(attn, gm, h, ga, gg, gpost, wb)


def _ffn_up(h1, g, wa, layer):
    t = h1.shape[0]
    tm = _tile(t, 256)

    def body(h_ref, g_ref, wg_ref, wu_ref, f_ref, gp_ref, up_ref, act_ref):
        f = _rms_fwd(h_ref[...], g_ref[...]).astype(BF16)
        f_ref[...] = f
        gp = _dot_nt(f, wg_ref[...])
        up = _dot_nt(f, wu_ref[...])
        gp_ref[...] = gp.astype(BF16)
        up_ref[...] = up.astype(BF16)
        act_ref[...] = (gp * _sigmoid(gp) * up).astype(BF16)

    return pl.pallas_call(
        body, name="ffn_up", grid=(t // tm,),
        in_specs=[_row_spec(tm, D_MODEL), _vec_spec(D_MODEL), _seg_spec(ROWS_A, D_MODEL, 3 * layer),
                  _seg_spec(ROWS_A, D_MODEL, 3 * layer + 1)],
        out_specs=[_row_spec(tm, D_MODEL), _row_spec(tm, D_FF), _row_spec(tm, D_FF), _row_spec(tm, D_FF)],
        out_shape=[jax.ShapeDtypeStruct((t, D_MODEL), BF16), jax.ShapeDtypeStruct((t, D_FF), BF16),
                   jax.ShapeDtypeStruct((t, D_FF), BF16), jax.ShapeDtypeStruct((t, D_FF), BF16)],
        compiler_params=_params("parallel"),
    )(h1, g, wa, wa)


def _ffn_down(act, h1, g, wa, layer):
    t = h1.shape[0]
    tm = _tile(t, 512)

    def body(a_ref, h_ref, g_ref, w_ref, fo_ref, h2_ref):
        fo = _dot(a_ref[...], w_ref[...])
        fo_ref[...] = fo
        h2_ref[...] = h_ref[...] + _rms_fwd(fo, g_ref[...])

    return pl.pallas_call(
        body, name="ffn_down", grid=(t // tm,),
        in_specs=[_row_spec(tm, D_FF), _row_spec(tm, D_MODEL), _vec_spec(D_MODEL),
                  _seg_spec(ROWS_A, D_MODEL, 3 * layer + 2)],
        out_specs=[_row_spec(tm, D_MODEL), _row_spec(tm, D_MODEL)],
        out_shape=[jax.ShapeDtypeStruct((t, D_MODEL), F32), jax.ShapeDtypeStruct((t, D_MODEL), F32)],
        compiler_params=_params("parallel"),
    )(act, h1, g, wa)


def _ple(h2, p, g, wb, wp, layer):
    t = h2.shape[0]
    tm = _tile(t, 512)

    def body(h_ref, p_ref, g_ref, wg_ref, wp_ref, hn_ref, gate_ref, h3_ref):
        h = h_ref[...]
        hn = _rms_fwd(h, g_ref[...]).astype(BF16)
        hn_ref[...] = hn
        gate = _sigmoid(_dot(hn, wg_ref[...]))
        pe = _dot_nt(p_ref[...].astype(BF16), wp_ref[...])
        gate_ref[...] = gate.astype(BF16)
        h3_ref[...] = h + pe * gate

    return pl.pallas_call(
        body, name="ple_fwd", grid=(t // tm,),
        in_specs=[_row_spec(tm, D_MODEL), _row_spec(tm, PLE_DIM), _vec_spec(D_MODEL),
                  _seg_spec(ROWS_B, D_MODEL, 2 * layer + 1), _seg_spec(ROWS_B, PLE_DIM, layer)],
        out_specs=[_row_spec(tm, D_MODEL)] * 3,
        out_shape=[jax.ShapeDtypeStruct((t, D_MODEL), BF16), jax.ShapeDtypeStruct((t, D_MODEL), BF16),
                   jax.ShapeDtypeStruct((t, D_MODEL), F32)],
        compiler_params=_params("parallel"),
    )(h2, p, g, wb, wp)


def _loss_head(y, target):
    t = y.shape[0]
    tm = _tile(t, 512)

    def body(y_ref, t_ref, dy_ref, l_ref):
        _zero_at(pl.program_id(0) == 0, l_ref)
        e = y_ref[...] - t_ref[...]
        dy_ref[...] = e * (1.0 / D_MODEL)
        s = jnp.sum(jnp.sum(e * e, axis=1, keepdims=True), axis=0, keepdims=True)
        l_ref[...] += jnp.broadcast_to(s, (1, 128))

    return pl.pallas_call(
        body, name="loss_head", grid=(t // tm,),
        in_specs=[_row_spec(tm, D_MODEL), _row_spec(tm, D_MODEL)],
        out_specs=[_row_spec(tm, D_MODEL), _vec_spec(128)],
        out_shape=[jax.ShapeDtypeStruct((t, D_MODEL), F32), jax.ShapeDtypeStruct((1, 128), F32)],
        compiler_params=_params("arbitrary"),
    )(y, target)


def _ple_bwd(dh3, h2, gate, p, g, wb, wp, layer, after):
    t = h2.shape[0]
    tm = _tile(t, 512)

    def body(d_ref, h_ref, gate_ref, p_ref, g_ref, w_ref, wp_ref, after_ref, dh2_ref, dgl_ref, dpe_ref, dg_ref):
        _zero_at(pl.program_id(0) == 0, dg_ref)
        d = d_ref[...]
        gate = gate_ref[...].astype(F32)
        pe = _dot_nt(p_ref[...].astype(BF16), wp_ref[...])
        dpe_ref[...] = (d * gate).astype(BF16)
        dgl = (d * pe * gate * (1.0 - gate)).astype(BF16)
        dgl_ref[...] = dgl
        dhn = _dot_nt(dgl, w_ref[...])
        dx, dg = _rms_bwd(h_ref[...], g_ref[...], dhn)
        dh2_ref[...] = d + dx
        dg_ref[...] += dg

    return pl.pallas_call(
        body, name="ple_bwd", grid=(t // tm,),
        in_specs=[_row_spec(tm, D_MODEL)] * 3 + [_row_spec(tm, PLE_DIM), _vec_spec(D_MODEL),
                                                 _seg_spec(ROWS_B, D_MODEL, 2 * layer + 1),
                                                 _seg_spec(ROWS_B, PLE_DIM, layer), pl.BlockSpec(memory_space=pl.ANY)],
        out_specs=[_row_spec(tm, D_MODEL)] * 3 + [_vec_spec(D_MODEL)],
        out_shape=[jax.ShapeDtypeStruct((t, D_MODEL), F32), jax.ShapeDtypeStruct((t, D_MODEL), BF16),
                   jax.ShapeDtypeStruct((t, D_MODEL), BF16), jax.ShapeDtypeStruct((1, D_MODEL), F32)],
        compiler_params=_params("arbitrary"),
    )(dh3, h2, gate, p, g, wb, wp, after)


def _ffn_down_bwd(dh2, fo, gp, up, g, wa, layer):
    t = dh2.shape[0]
    tm = _tile(t, 256)

    def body(d_ref, fo_ref, gp_ref, up_ref, g_ref, w_ref, dfo_ref, dgp_ref, dup_ref, dg_ref):
        _zero_at(pl.program_id(0) == 0, dg_ref)
        dfo, dg = _rms_bwd(fo_ref[...], g_ref[...], d_ref[...])
        dfo = dfo.astype(BF16)
        dfo_ref[...] = dfo
        dact = _dot_nt(dfo, w_ref[...])
        gp = gp_ref[...].astype(F32)
        sg = _sigmoid(gp)
        dgp_ref[...] = (dact * up_ref[...].astype(F32) * (sg * (1.0 + gp * (1.0 - sg)))).astype(BF16)
        dup_ref[...] = (dact * (gp * sg)).astype(BF16)
        dg_ref[...] += dg

    return pl.pallas_call(
        body, name="ffn_down_bwd", grid=(t // tm,),
        in_specs=[_row_spec(tm, D_MODEL), _row_spec(tm, D_MODEL), _row_spec(tm, D_FF), _row_spec(tm, D_FF),
                  _vec_spec(D_MODEL), _seg_spec(ROWS_A, D_MODEL, 3 * layer + 2)],
        out_specs=[_row_spec(tm, D_MODEL), _row_spec(tm, D_FF), _row_spec(tm, D_FF), _vec_spec(D_MODEL)],
        out_shape=[jax.ShapeDtypeStruct((t, D_MODEL), BF16), jax.ShapeDtypeStruct((t, D_FF), BF16),
                   jax.ShapeDtypeStruct((t, D_FF), BF16), jax.ShapeDtypeStruct((1, D_MODEL), F32)],
        compiler_params=_params("arbitrary"),
    )(dh2, fo, gp, up, g, wa)


def _ffn_up_bwd(dgp, dup, h1, dh2, g, wa, layer, after):
    t = h1.shape[0]
    tm = _tile(t, 256)

    def body(dgp_ref, dup_ref, h_ref, d_ref, g_ref, wg_ref, wu_ref, after_ref, dh1_ref, dg_ref):
        _zero_at(pl.program_id(0) == 0, dg_ref)
        df = _dot(dgp_ref[...], wg_ref[...]) + _dot(dup_ref[...], wu_ref[...])
        dx, dg = _rms_bwd(h_ref[...], g_ref[...], df)
        dh1_ref[...] = d_ref[...] + dx
        dg_ref[...] += dg

    return pl.pallas_call(
        body, name="ffn_up_bwd", grid=(t // tm,),
        in_specs=[_row_spec(tm, D_FF), _row_spec(tm, D_FF), _row_spec(tm, D_MODEL), _row_spec(tm, D_MODEL),
                  _vec_spec(D_MODEL), _seg_spec(ROWS_A, D_MODEL, 3 * layer), _seg_spec(ROWS_A, D_MODEL, 3 * layer + 1),
                  pl.BlockSpec(memory_space=pl.ANY)],
        out_specs=[_row_spec(tm, D_MODEL), _vec_spec(D_MODEL)],
        out_shape=[jax.ShapeDtypeStruct((t, D_MODEL), F32), jax.ShapeDtypeStruct((1, D_MODEL), F32)],
        compiler_params=_params("arbitrary"),
    )(dgp, dup, h1, dh2, g, wa, wa, after)


def _out_proj_bwd(dh1, mix, attn, gm, gpost, ga, gg, wb, layer):
    t = dh1.shape[0]
    tm = _tile(t, 512)

    def body(d_ref, mix_ref, a_ref, m_ref, gp_ref, ga_ref, gg_ref, w_ref,
             dmix_ref, da_ref, dm_ref, dgp_ref, dga_ref, dgg_ref):
        _zero_at(pl.program_id(0) == 0, dgp_ref, dga_ref, dgg_ref)
        dmix, dgp = _rms_bwd(mix_ref[...], gp_ref[...], d_ref[...])
        dmix = dmix.astype(BF16)
        dmix_ref[...] = dmix
        da, dga = _rms_bwd(a_ref[...], ga_ref[...], _dot_nt(dmix, w_ref[0:512, :]))
        dm, dgg = _rms_bwd(m_ref[...], gg_ref[...], _dot_nt(dmix, w_ref[512:1024, :]))
        da_ref[...] = da.astype(BF16)
        dm_ref[...] = dm
        dgp_ref[...] += dgp
        dga_ref[...] += dga
        dgg_ref[...] += dgg

    return pl.pallas_call(
        body, name="out_proj_bwd", grid=(t // tm,),
        in_specs=[_row_spec(tm, D_MODEL), _row_spec(tm, D_MODEL), _row_spec(tm, ATTN_W), _row_spec(tm, GM_W),
                  _vec_spec(D_MODEL), _vec_spec(ATTN_W), _vec_spec(GM_W), _seg_spec(ROWS_B, D_MODEL, 2 * layer)],
        out_specs=[_row_spec(tm, D_MODEL), _row_spec(tm, ATTN_W), _row_spec(tm, GM_W),
                   _vec_spec(D_MODEL), _vec_spec(ATTN_W), _vec_spec(GM_W)],
        out_shape=[jax.ShapeDtypeStruct((t, D_MODEL), BF16), jax.ShapeDtypeStruct((t, ATTN_W), BF16),
                   jax.ShapeDtypeStruct((t, GM_W), F32), jax.ShapeDtypeStruct((1, D_MODEL), F32),
                   jax.ShapeDtypeStruct((1, ATTN_W), F32), jax.ShapeDtypeStruct((1, GM_W), F32)],
        compiler_params=_params("arbitrary"),
    )(dh1, mix, attn, gm, gpost, ga, gg, wb)


def _split3(x):
    hi = x.astype(BF16)
    r1 = x - hi.astype(F32)
    mid = r1.astype(BF16)
    lo = (r1 - mid.astype(F32)).astype(BF16)
    return hi, mid, lo


def _gm_bwd(dgm, zu, zv, lng, lnb, wtril, bsx):
    t = zu.shape[0]
    tm = _tile(t, 512)
    nb = t // tm

    def body(d_ref, zu_ref, zv_ref, g_ref, b_ref, w_ref, bs_ref,
             dzu_ref, dzv_ref, dw_ref, dbs_ref, dlg_ref, dlb_ref, dbsx_ref):
        i = pl.program_id(0)
        _zero_at(i == 0, dw_ref, dlg_ref, dlb_ref, dbsx_ref)
        row = lax.broadcasted_iota(jnp.int32, (BLK, BLK), 0)
        lane = lax.broadcasted_iota(jnp.int32, (BLK, BLK), 1)
        low = lane < 64
        tril = row >= lane
        lng = g_ref[...]
        for b in range(tm // BLK):
            rows = slice(b * BLK, (b + 1) * BLK)
            zu = zu_ref[rows, :]
            zv = zv_ref[rows, :]
            gu, ln, xn, rstd, mixed = _gm_forward_block(zu, zv, lng, b_ref[...], w_ref, bs_ref[...], low)
            dgm = d_ref[rows, :]
            dgu_cols, dmx_cols, dln_cols = [], [], []
            for col in range(4):
                sl = slice(col * 128, (col + 1) * 128)
                dg = dgm[:, sl]
                dgu_cols.append(dg * mixed[col])
                dmx = dg * gu[:, sl]
                dmx_cols.append(dmx)
                lc = ln[:, sl]
                halves = (jnp.where(low, lc, 0.0).astype(BF16), jnp.where(low, 0.0, lc).astype(BF16))
                dmx16 = dmx.astype(BF16)
                dmx_half = (jnp.where(low, dmx, 0.0).astype(BF16), jnp.where(low, 0.0, dmx).astype(BF16))
                dln = None
                for half in range(2):
                    hd = 2 * col + half
                    dw_ref[hd] += jnp.where(tril, _dot_nt(dmx16, halves[half]), 0.0)
                    part = _dot_tn(w_ref[hd], dmx_half[half])
                    dln = part if dln is None else dln + part
                dln_cols.append(dln)
            dgu = jnp.concatenate(dgu_cols, axis=1)
            dmx = jnp.concatenate(dmx_cols, axis=1)
            dln = jnp.concatenate(dln_cols, axis=1)
            dzu_ref[rows, :] = (dgu * _gelu_grad(zu)).astype(BF16)
            dbsx_ref[...] += dmx
            dlg_ref[...] += jnp.sum(dln * xn, axis=0, keepdims=True)
            dlb_ref[...] += jnp.sum(dln, axis=0, keepdims=True)
            dxn = dln * lng
            dgv = rstd * (dxn - jnp.mean(dxn, axis=-1, keepdims=True) - xn * jnp.mean(dxn * xn, axis=-1, keepdims=True))
            dzv_ref[rows, :] = (dgv * _gelu_grad(zv)).astype(BF16)

        @pl.when(i == nb - 1)
        def _():
            r = lax.broadcasted_iota(jnp.int32, (GM_W, BLK), 0)
            c = lax.broadcasted_iota(jnp.int32, (GM_W, BLK), 1)
            e = jnp.where(jnp.logical_and(r >= c * 64, r < c * 64 + 64), 1.0, 0.0).astype(BF16)
            hi, mid, lo = _split3(dbsx_ref[...])
            dbs_ref[...] = _dot(hi, e) + _dot(mid, e) + _dot(lo, e)

    vec = _vec_spec(GM_W)
    return pl.pallas_call(
        body, name="gm_bwd", grid=(nb,),
        in_specs=[_row_spec(tm, GM_W)] * 3 + [vec, vec, pl.BlockSpec((8, BLK, BLK), lambda i: (0, 0, 0)),
                                              pl.BlockSpec((BLK, GM_W), lambda i: (0, 0))],
        out_specs=[_row_spec(tm, GM_W), _row_spec(tm, GM_W), pl.BlockSpec((8, BLK, BLK), lambda i: (0, 0, 0)),
                   pl.BlockSpec((BLK, BLK), lambda i: (0, 0)), vec, vec],
        out_shape=[jax.ShapeDtypeStruct((t, GM_W), BF16), jax.ShapeDtypeStruct((t, GM_W), BF16),
                   jax.ShapeDtypeStruct((8, BLK, BLK), F32), jax.ShapeDtypeStruct((BLK, BLK), F32),
                   jax.ShapeDtypeStruct((1, GM_W), F32), jax.ShapeDtypeStruct((1, GM_W), F32)],
        scratch_shapes=[pltpu.VMEM((BLK, GM_W), F32)],
        compiler_params=_params("arbitrary"),
    )(dgm, zu, zv, lng, lnb, wtril, bsx)


def _attn_bwd(q, kv, do, sinks, after):
    t = q.shape[0]
    tq = _tile(t, 512)
    blocks = tq // BLK

    def body(sink_ref, q_ref, kvc_ref, kvp_ref, do_ref, after_ref, dq_ref, dkv_ref, dkf_ref, ds_ref):
        i = pl.program_id(0)
        _zero_at(i == 0, ds_ref)
        low, vcur, dist, valid = _attn_masks(i)
        head_row = lax.broadcasted_iota(jnp.int32, (8, 128), 0)
        dsink_tile = jnp.zeros((8, 128), F32)
        kp, vp = _kv_variants(kvp_ref, slice(None), low)
        own = None
        for b in range(blocks):
            rows = slice(b * BLK, (b + 1) * BLK)
            kc, vc = _kv_variants(kvc_ref, rows, low)
            scores = _attn_scores(q_ref, rows, kc, kp, vcur)
            dprobs = _attn_scores(do_ref, rows, vc, vp, vcur)
            parts = []
            for h in range(8):
                p, ps = _attn_probs(scores[h], h, sink_ref[h], dist, valid if b == 0 else None)
                delta = jnp.sum(p * dprobs[h], axis=1, keepdims=True)
                ds = p * (dprobs[h] - delta) * ATTN_SCALE
                dsink = jnp.sum(-ps * delta, axis=0, keepdims=True)
                dsink_tile = jnp.where(head_row == h, dsink_tile + dsink, dsink_tile)
                parts.append((jnp.where(vcur, ds, 0.0).astype(BF16), jnp.where(vcur, 0.0, ds).astype(BF16),
                              jnp.where(vcur, p, 0.0).astype(BF16), jnp.where(vcur, 0.0, p).astype(BF16)))
            acc = {}

            def add(name, key, val):
                acc[(name, key)] = val if (name, key) not in acc else acc[(name, key)] + val

            for col in range(4):
                dq = None
                qh = q_ref[rows, col * 128:(col + 1) * 128]
                doh = do_ref[rows, col * 128:(col + 1) * 128]
                for half in range(2):
                    h = 2 * col + half
                    key = _head_key(h)
                    dsc, dsp, pc, pp = parts[h]
                    part = _dot(dsc, kc[key]) + _dot(dsp, kp[key])
                    dq = part if dq is None else dq + part
                    add("kc", key, _dot_tn(dsc, qh))
                    add("kp", key, _dot_tn(dsp, qh))
                    add("vc", key, _dot_tn(pc, doh))
                    add("vp", key, _dot_tn(pp, doh))
                dq_ref[rows, col * 128:(col + 1) * 128] = dq.astype(BF16)

            def place(name):
                head0 = acc[(name, (0, 0))] + pltpu.roll(acc[(name, (0, 1))], 64, axis=1)
                head1 = pltpu.roll(acc[(name, (1, 0))], 64, axis=1) + acc[(name, (1, 1))]
                return jnp.where(low, head0, head1)

            before = (place("kp"), place("vp"))
            if b == 0:
                dkf_ref[:, 0:128], dkf_ref[:, 128:256] = before
            else:
                last = slice((b - 1) * BLK, b * BLK)
                dkv_ref[last, 0:128] = own[0] + before[0]
                dkv_ref[last, 128:256] = own[1] + before[1]
            own = (place("kc"), place("vc"))
            kp, vp = kc, vc
        final = slice((blocks - 1) * BLK, blocks * BLK)
        dkv_ref[final, 0:128], dkv_ref[final, 128:256] = own
        ds_ref[...] += dsink_tile

    row_q = _row_spec(tq, ATTN_W)
    row_kv = _row_spec(tq, 2 * KV_W)
    return pl.pallas_call(
        body, name="attn_bwd", grid=(t // tq,),
        in_specs=[pl.BlockSpec(memory_space=pltpu.SMEM), row_q, row_kv, _kv_prev_spec(blocks), row_q,
                  pl.BlockSpec(memory_space=pl.ANY)],
        out_specs=[row_q, row_kv, _row_spec(BLK, 2 * KV_W), pl.BlockSpec((8, 128), lambda i: (0, 0))],
        out_shape=[jax.ShapeDtypeStruct((t, ATTN_W), BF16), jax.ShapeDtypeStruct((t, 2 * KV_W), F32),
                   jax.ShapeDtypeStruct((t // tq * BLK, 2 * KV_W), F32), jax.ShapeDtypeStruct((8, 128), F32)],
        compiler_params=_params("arbitrary"),
    )(sinks, q, kv, kv, do, after)


def _in_proj_bwd(dq, dkv, dkf, dzu, dzv, h, dres, g, wc, layer):
    t = h.shape[0]
    tm = _tile(t, 512)
    steps = t // tm

    def body(dq_ref, dkv_ref, dkn_ref, dzu_ref, dzv_ref, h_ref, d_ref, g_ref, w_ref, dz_ref, dh_ref, dg_ref):
        i = pl.program_id(0)
        _zero_at(i == 0, dg_ref)
        dq = dq_ref[...]
        tail = dkv_ref[tm - BLK:tm, :] + jnp.where(i < steps - 1, dkn_ref[...], 0.0)
        dkv = tail if tm == BLK else jnp.concatenate([dkv_ref[0:tm - BLK, :], tail], axis=0)
        dkv = dkv.astype(BF16)
        dzu = dzu_ref[...]
        dzv = dzv_ref[...]
        dz_ref[:, 0:512] = dq
        dz_ref[:, 512:768] = dkv
        dz_ref[:, 768:1280] = dzu
        dz_ref[:, 1280:1792] = dzv
        da = (_dot(dq, w_ref[0:512, :]) + _dot(dkv, w_ref[512:768, :]) + _dot(dzu, w_ref[768:1280, :])
              + _dot(dzv, w_ref[1280:1792, :]))
        dx, dg = _rms_bwd(h_ref[...], g_ref[...], da)
        dh_ref[...] = d_ref[...] + dx
        dg_ref[...] += dg

    return pl.pallas_call(
        body, name="in_proj_bwd", grid=(t // tm,),
        in_specs=[_row_spec(tm, ATTN_W), _row_spec(tm, 2 * KV_W),
                  pl.BlockSpec((BLK, 2 * KV_W), lambda i: (jnp.minimum(i + 1, steps - 1), 0)), _row_spec(tm, GM_W),
                  _row_spec(tm, GM_W), _row_spec(tm, D_MODEL), _row_spec(tm, D_MODEL), _vec_spec(D_MODEL),
                  _seg_spec(ROWS_C, D_MODEL, layer)],
        out_specs=[_row_spec(tm, D_IN), _row_spec(tm, D_MODEL), _vec_spec(D_MODEL)],
        out_shape=[jax.ShapeDtypeStruct((t, D_IN), BF16), jax.ShapeDtypeStruct((t, D_MODEL), F32),
                   jax.ShapeDtypeStruct((1, D_MODEL), F32)],
        compiler_params=_params("arbitrary"),
    )(dq, dkv, dkf, dzu, dzv, h, dres, g, wc)


def _weight_grad(a, b, buf, seg):
    t, m = a.shape
    n = b.shape[1]
    assert buf.shape[0] % m == 0 and buf.shape[1] == n
    tm = _tile(t, 512)
    steps = t // tm

    def body(a_ref, b_ref, buf_ref, o_ref, acc_ref):
        i = pl.program_id(0)
        _zero_at(i == 0, acc_ref)
        acc_ref[...] += _dot_tn(a_ref[...], b_ref[...].astype(BF16))

        @pl.when(i == steps - 1)
        def _():
            o_ref[...] = acc_ref[...].astype(o_ref.dtype)

    return pl.pallas_call(
        body, name="weight_grad", grid=(steps,),
        in_specs=[_row_spec(tm, m), _row_spec(tm, n), pl.BlockSpec(memory_space=pl.ANY)],
        out_specs=pl.BlockSpec((m, n), lambda i: (seg, 0)),
        out_shape=jax.ShapeDtypeStruct(buf.shape, buf.dtype),
        scratch_shapes=[pltpu.VMEM((m, n), F32)],
        input_output_aliases={2: 0},
        compiler_params=_params("arbitrary"),
    )(a, b, buf)


def _rows8(rows):
    return [jnp.pad(r, ((0, 7), (0, 0))) for r in rows]


def _small_pack_gating(d):
    rows = [jnp.concatenate([d["gm_ln_g"], d["gm_ln_b"]], axis=1), d["gm_bs"].reshape(1, 1024)]
    return jnp.concatenate(_rows8(rows) + [d["gm_ws"].reshape(128, 1024)], axis=0)


def _small_pack_rest(d):
    rows = [d["ln_mix_pre"], d["ln_mix_post"], d["ln_ffn_pre"], d["ln_ffn_post"], d["ln_ple_gate"],
            jnp.concatenate([d["g_attn_out"], d["g_gm_out"]], axis=1),
            jnp.pad(d["attn_sinks"].reshape(1, 8), ((0, 0), (0, 1016)))]
    return jnp.concatenate(_rows8(rows), axis=0)


def _small_unpack(g, s):
    return {
        "gm_ln_g": g[:, 0, :512], "gm_ln_b": g[:, 0, 512:], "gm_bs": g[:, 8].reshape(DEPTH, 8, 128),
        "gm_ws": g[:, 16:144].reshape(DEPTH, 8, 128, 128),
        "ln_mix_pre": s[:, 0], "ln_mix_post": s[:, 8], "ln_ffn_pre": s[:, 16], "ln_ffn_post": s[:, 24],
        "ln_ple_gate": s[:, 32], "g_attn_out": s[:, 40, :512], "g_gm_out": s[:, 40, 512:], "attn_sinks": s[:, 48, :8],
    }


def _row(v):
    return v.reshape(1, -1)


def _layer_fwd(h, p, sp, l, weights):
    tril = jnp.tril(jnp.ones((BLK, BLK), bool))
    wtril = jnp.where(tril[None], sp["gm_ws"][l], 0.0).astype(BF16)
    bsx = jnp.repeat(sp["gm_bs"][l].T, HEAD_DIM, axis=1)
    a, q, kv, zu, zv = _in_proj(h, _row(sp["ln_mix_pre"][l]), weights("c", h), 0)
    attn = _attn_fwd(q, kv, sp["attn_sinks"][l])
    gm = _gm_fwd(zu, zv, _row(sp["gm_ln_g"][l]), _row(sp["gm_ln_b"][l]), wtril, bsx)
    wb = weights("b", gm)
    heads, mix, h1 = _out_proj(attn, gm, h, _row(sp["g_attn_out"][l]), _row(sp["g_gm_out"][l]),
                               _row(sp["ln_mix_post"][l]), wb, 0)
    wa = weights("a", h1)
    f, gpre, up, act = _ffn_up(h1, _row(sp["ln_ffn_pre"][l]), wa, 0)
    fo, h2 = _ffn_down(act, h1, _row(sp["ln_ffn_post"][l]), wa, 0)
    hn, gate, h3 = _ple(h2, p, _row(sp["ln_ple_gate"][l]), wb, weights("p", gm), 0)
    saved = dict(h=h, a=a, q=q, kv=kv, zu=zu, zv=zv, attn=attn, gm=gm, heads=heads, mix=mix, h1=h1, f=f,
                 gpre=gpre, up=up, act=act, fo=fo, h2=h2, hn=hn, gate=gate, wtril=wtril, bsx=bsx)
    return h3, saved


def _layer_bwd_upper(dh, s, p, sp, l, wa, wb, wp, after):
    d = {}
    dh2, dgl, dpe, d["ln_ple_gate"] = _ple_bwd(dh, s["h2"], s["gate"], p, _row(sp["ln_ple_gate"][l]), wb, wp, 0, after)
    gb = _weight_grad(s["hn"], dgl, lax.empty((2 * D_MODEL, D_MODEL), BF16), 1)
    gp = _weight_grad(dpe, p, lax.empty((D_MODEL, PLE_DIM), BF16), 0)
    dfo, dgp, dup, d["ln_ffn_post"] = _ffn_down_bwd(dh2, s["fo"], s["gpre"], s["up"], _row(sp["ln_ffn_post"][l]), wa, 0)
    ga = _weight_grad(s["act"], dfo, lax.empty((3 * D_FF, D_MODEL), BF16), 2)
    ga = _weight_grad(dgp, s["f"], ga, 0)
    ga = _weight_grad(dup, s["f"], ga, 1)
    return (dh2, dgp, dup, d), ga, gp, gb


def _layer_bwd_middle(carry, s, sp, l, wa, wb, gb, after):
    dh2, dgp, dup, d = carry
    dh1, d["ln_ffn_pre"] = _ffn_up_bwd(dgp, dup, s["h1"], dh2, _row(sp["ln_ffn_pre"][l]), wa, 0, after)
    dmix, dattn, dgm, d["ln_mix_post"], d["g_attn_out"], d["g_gm_out"] = _out_proj_bwd(
        dh1, s["mix"], s["attn"], s["gm"], _row(sp["ln_mix_post"][l]), _row(sp["g_attn_out"][l]),
        _row(sp["g_gm_out"][l]), wb, 0)
    gb = _weight_grad(s["heads"], dmix, gb, 0)
    dzu, dzv, d["gm_ws"], dbs, d["gm_ln_g"], d["gm_ln_b"] = _gm_bwd(
        dgm, s["zu"], s["zv"], _row(sp["gm_ln_g"][l]), _row(sp["gm_ln_b"][l]), s["wtril"], s["bsx"])
    d["gm_bs"] = dbs[:, :8].T
    return (dh1, dattn, dzu, dzv, d), gb, _small_pack_gating(d)


def _layer_bwd_lower(carry, s, sp, l, wc, after):
    dh1, dattn, dzu, dzv, d = carry
    dq, dkv, dkf, dsink = _attn_bwd(s["q"], s["kv"], dattn, sp["attn_sinks"][l], after)
    d["attn_sinks"] = dsink[:, 0]
    dz, dh, d["ln_mix_pre"] = _in_proj_bwd(dq, dkv, dkf, dzu, dzv, s["h"], dh1, _row(sp["ln_mix_pre"][l]), wc, 0)
    gc = _weight_grad(dz, s["a"], lax.empty((D_IN, D_MODEL), BF16), 0)
    return dh, gc, _small_pack_rest(d)


ANY = pl.BlockSpec(memory_space=pl.ANY)


def _place():
    x, y, c = lax.axis_index("x"), lax.axis_index("y"), lax.axis_index("c")
    chips = [(1 - x, y), (x, 1 - y), (1 - x, 1 - y)]
    return x, y, c, chips


def _all_gather(shards):
    n = len(shards)

    def body(*refs):
        ins, outs = refs[:n], refs[n:2 * n]
        send_sems, recv_sems, local_sems = refs[2 * n:]
        x, y, c, chips = _place()
        me, sibling = (x, y, c), (x, y, 1 - c)

        def block(k, px, py, pc):
            return outs[k].at[:, pl.ds(4 * px + 2 * py + pc, 1)]

        def copy(k, j, who, to, src=None):
            return pltpu.make_async_remote_copy(
                src_ref=block(k, *who) if src is None else src, dst_ref=block(k, *who),
                send_sem=send_sems.at[7 * k + j], recv_sem=recv_sems.at[7 * k + j],
                device_id=to, device_id_type=MESH)

        mine = [pltpu.make_async_copy(ins[k], block(k, *me), local_sems.at[k]) for k in range(n)]
        for cp in mine:
            cp.start()
        first = []
        for k in range(n):
            first.append(copy(k, 0, me, sibling, src=ins[k]))
            first += [copy(k, 1 + j, me, (*chip, c), src=ins[k]) for j, chip in enumerate(chips)]
        for cp in first:
            cp.start()
        passed = []
        for j, chip in enumerate(chips):
            for k in range(n):
                copy(k, 1 + j, (*chip, c), me).wait_recv()
                cp = copy(k, 4 + j, (*chip, c), sibling)
                cp.start()
                passed.append(cp)
        for k in range(n):
            copy(k, 0, sibling, me).wait_recv()
            for j, chip in enumerate(chips):
                copy(k, 4 + j, (*chip, 1 - c), me).wait_recv()
        for cp in first + passed:
            cp.wait_send()
        for cp in mine:
            cp.wait()

    return pl.pallas_call(
        body, name="all_gather_weights",
        in_specs=[ANY] * n, out_specs=[ANY] * n,
        out_shape=[jax.ShapeDtypeStruct((s.shape[0], N_DEV) + s.shape[2:], s.dtype) for s in shards],
        scratch_shapes=[pltpu.SemaphoreType.DMA((7 * n,)), pltpu.SemaphoreType.DMA((7 * n,)),
                        pltpu.SemaphoreType.DMA((n,))],
        compiler_params=pltpu.CompilerParams(has_side_effects=True),
    )(*shards)


def _sibling_exchange(bufs, small):
    n = len(bufs)

    def body(*refs):
        ins, small_ref = refs[:n], refs[n]
        outs, both_ref = refs[n + 1:2 * n + 1], refs[2 * n + 1]
        send_sems, recv_sems, local_sem = refs[2 * n + 2:]
        x, y, c, _ = _place()
        sibling = (x, y, 1 - c)
        mine = pltpu.make_async_copy(small_ref, both_ref.at[c], local_sem)
        mine.start()
        copies = [pltpu.make_async_remote_copy(
            src_ref=ins[k].at[:, :, pl.ds(1 - c, 1)], dst_ref=outs[k], send_sem=send_sems.at[k],
            recv_sem=recv_sems.at[k], device_id=sibling, device_id_type=MESH) for k in range(n)]
        copies.append(pltpu.make_async_remote_copy(
            src_ref=small_ref, dst_ref=both_ref.at[c], send_sem=send_sems.at[n], recv_sem=recv_sems.at[n],
            device_id=sibling, device_id_type=MESH))
        for cp in copies:
            cp.start()
        for k in range(n):
            copies[k].wait_recv()
        pltpu.make_async_remote_copy(
            src_ref=small_ref, dst_ref=both_ref.at[1 - c], send_sem=send_sems.at[n], recv_sem=recv_sems.at[n],
            device_id=sibling, device_id_type=MESH).wait_recv()
        for cp in copies:
            cp.wait_send()
        mine.wait()

    return pl.pallas_call(
        body, name="sibling_exchange",
        in_specs=[ANY] * (n + 1), out_specs=[ANY] * (n + 1),
        out_shape=[jax.ShapeDtypeStruct(b.shape[:2] + (1,) + b.shape[3:], b.dtype) for b in bufs]
        + [jax.ShapeDtypeStruct((2,) + small.shape, small.dtype)],
        scratch_shapes=[pltpu.SemaphoreType.DMA((n + 1,)), pltpu.SemaphoreType.DMA((n + 1,)), pltpu.SemaphoreType.DMA],
        compiler_params=pltpu.CompilerParams(has_side_effects=True),
    )(*bufs, small)


def _chip_exchange(sends, small):
    n = len(sends)

    def body(*refs):
        ins, small_ref = refs[:n], refs[n]
        outs, all_ref = refs[n + 1:2 * n + 1], refs[2 * n + 1]
        send_sems, recv_sems, local_sem = refs[2 * n + 2:]
        x, y, c, chips = _place()
        mine = pltpu.make_async_copy(small_ref, all_ref.at[2 * x + y], local_sem)
        mine.start()
        copies = []
        for j, chip in enumerate(chips):
            for k in range(n):
                copies.append(pltpu.make_async_remote_copy(
                    src_ref=ins[k].at[j], dst_ref=outs[k].at[j], send_sem=send_sems.at[3 * k + j],
                    recv_sem=recv_sems.at[3 * k + j], device_id=(*chip, c), device_id_type=MESH))
            copies.append(pltpu.make_async_remote_copy(
                src_ref=small_ref, dst_ref=all_ref.at[2 * x + y], send_sem=send_sems.at[3 * n + j],
                recv_sem=recv_sems.at[3 * n + j], device_id=(*chip, c), device_id_type=MESH))
        for cp in copies:
            cp.start()
        for j, (px, py) in enumerate(chips):
            for k in range(n):
                copies[j * (n + 1) + k].wait_recv()
            pltpu.make_async_remote_copy(
                src_ref=small_ref, dst_ref=all_ref.at[2 * px + py], send_sem=send_sems.at[3 * n + j],
                recv_sem=recv_sems.at[3 * n + j], device_id=(px, py, c), device_id_type=MESH).wait_recv()
        for cp in copies:
            cp.wait_send()
        mine.wait()

    return pl.pallas_call(
        body, name="chip_exchange",
        in_specs=[ANY] * (n + 1), out_specs=[ANY] * (n + 1),
        out_shape=[jax.ShapeDtypeStruct(s.shape, s.dtype) for s in sends]
        + [jax.ShapeDtypeStruct((4,) + small.shape, small.dtype)],
        scratch_shapes=[pltpu.SemaphoreType.DMA((3 * n + 3,)), pltpu.SemaphoreType.DMA((3 * n + 3,)),
                        pltpu.SemaphoreType.DMA],
        compiler_params=pltpu.CompilerParams(has_side_effects=True),
    )(*sends, small)


def _pair_add(buf, got, chip_ids, dtype):
    nseg, _, _, rows, cols = buf.shape
    nr = chip_ids.shape[0] - 1

    def body(ids_ref, a_ref, b_ref, o_ref):
        o_ref[...] = (a_ref[...] + b_ref[...]).astype(dtype)

    return pl.pallas_call(
        body, name="pair_add",
        grid_spec=pltpu.PrefetchScalarGridSpec(
            num_scalar_prefetch=1, grid=(nr, nseg),
            in_specs=[pl.BlockSpec((None, None, None, rows, cols), lambda r, s, ids: (s, ids[r], ids[nr], 0, 0)),
                      pl.BlockSpec((None, None, None, rows, cols), lambda r, s, ids: (s, ids[r], 0, 0, 0))],
            out_specs=pl.BlockSpec((None, None, rows, cols), lambda r, s, ids: (r, s, 0, 0))),
        out_shape=jax.ShapeDtypeStruct((nr, nseg, rows, cols), dtype),
        compiler_params=_params("parallel", "parallel"),
    )(chip_ids, buf, got)


def _sum_slots(z, tr):
    n, rows, cols = z.shape

    def body(z_ref, o_ref):
        s = z_ref[0]
        for k in range(1, n):
            s = s + z_ref[k]
        o_ref[...] = s

    return pl.pallas_call(
        body, name="sum_slots", grid=(rows // tr,),
        in_specs=[pl.BlockSpec((n, tr, cols), lambda i: (0, i, 0))],
        out_specs=pl.BlockSpec((tr, cols), lambda i: (i, 0)),
        out_shape=jax.ShapeDtypeStruct((rows, cols), F32),
        compiler_params=_params("parallel"),
    )(z)


def _final_sum(own, got):
    _, nseg, rows, cols = own.shape

    def body(a_ref, b_ref, o_ref):
        s = a_ref[0]
        for k in range(3):
            s = s + b_ref[k].astype(F32)
        o_ref[...] = s

    return pl.pallas_call(
        body, name="final_sum", grid=(nseg,),
        in_specs=[pl.BlockSpec((1, None, rows, cols), lambda i: (0, i, 0, 0)),
                  pl.BlockSpec((3, None, rows, cols), lambda i: (0, i, 0, 0))],
        out_specs=pl.BlockSpec((None, rows, cols), lambda i: (i, 0, 0)),
        out_shape=jax.ShapeDtypeStruct((nseg, rows, cols), F32),
        compiler_params=_params("parallel"),
    )(own, got)


HBM = pl.BlockSpec(memory_space=pltpu.HBM)
SEM = pl.BlockSpec(memory_space=pltpu.SEMAPHORE)
N_PEERS = N_DEV - 1


def _peers():
    x, y, c = lax.axis_index("x"), lax.axis_index("y"), lax.axis_index("c")
    peers = []
    for r in range(1, N_DEV):
        px = 1 - x if r & 4 else x
        py = 1 - y if r & 2 else y
        pc = 1 - c if r & 1 else c
        peers.append(((px, py, pc), 4 * px + 2 * py + pc))
    return 4 * x + 2 * y + c, peers


GATHER, SCATTER, SPREAD = "gather", "scatter", "spread"


def _peer_copy(src, land, send_sems, recv_sems, r, me, peer, peer_slot, mode):
    return pltpu.make_async_remote_copy(
        src_ref=src.at[:, pl.ds(peer_slot, 1)] if mode == SCATTER else src,
        dst_ref=land.at[:, pl.ds(me, 1)] if mode == GATHER else land.at[:, pl.ds(r - 1, 1)],
        send_sem=send_sems.at[r - 1], recv_sem=recv_sems.at[r - 1], device_id=peer, device_id_type=MESH)


def _peer_arrival(src, land, send_sems, recv_sems, r, me, peer, peer_slot, mode):
    return pltpu.make_async_remote_copy(
        src_ref=src.at[:, pl.ds(me, 1)] if mode == SCATTER else src,
        dst_ref=land.at[:, pl.ds(peer_slot, 1)] if mode == GATHER else land.at[:, pl.ds(r - 1, 1)],
        send_sem=send_sems.at[r - 1], recv_sem=recv_sems.at[r - 1], device_id=peer, device_id_type=MESH)


def _send_start(name, srcs, lands, modes):
    n = len(srcs)

    def body(*refs):
        src_refs, land_refs = refs[:n], refs[n:2 * n]
        outs = refs[2 * n:]
        send_sems, recv_sems, token = outs[2 * n:3 * n], outs[3 * n:4 * n], outs[4 * n]
        me, peers = _peers()
        for k in range(n):
            for r, (peer, slot) in enumerate(peers, 1):
                _peer_copy(src_refs[k], land_refs[k], send_sems[k], recv_sems[k], r, me, peer, slot, modes[k]).start()
        token[...] = jnp.zeros_like(token)

    hbm = lambda a: pltpu.HBM(a.shape, a.dtype)
    sems = [pltpu.SemaphoreType.DMA((N_PEERS,))] * (2 * n)
    outs = pl.pallas_call(
        body, name=name, in_specs=[HBM] * (2 * n),
        out_specs=[HBM] * (2 * n) + [SEM] * (2 * n) + [pl.BlockSpec(memory_space=pltpu.VMEM)],
        out_shape=[hbm(a) for a in srcs] + [hbm(a) for a in lands] + sems + [jax.ShapeDtypeStruct((8, 128), F32)],
        input_output_aliases={k: k for k in range(2 * n)},
        compiler_params=pltpu.CompilerParams(has_side_effects=pltpu.SideEffectType.DATAFLOW_SIDE_EFFECTING),
    )(*[pltpu.with_memory_space_constraint(a, pltpu.HBM) for a in list(srcs) + list(lands)])
    return dict(srcs=outs[:n], lands=outs[n:2 * n], send=outs[2 * n:3 * n], recv=outs[3 * n:4 * n],
                modes=list(modes)), outs[4 * n]


def _send_wait(name, sent, ks, after):
    n = len(ks)
    srcs = [sent["srcs"][k] for k in ks]
    lands = [sent["lands"][k] for k in ks]
    modes = [sent["modes"][k] for k in ks]

    def body(*refs):
        src_refs, land_refs = refs[:n], refs[n:2 * n]
        send_sems, recv_sems = refs[2 * n:3 * n], refs[3 * n:4 * n]
        me, peers = _peers()
        for k in range(n):
            for r, (peer, slot) in enumerate(peers, 1):
                args = (src_refs[k], land_refs[k], send_sems[k], recv_sems[k], r, me, peer, slot, modes[k])
                _peer_copy(*args).wait_send()
                _peer_arrival(*args).wait_recv()

    hbm = lambda a: pltpu.HBM(a.shape, a.dtype)
    outs = pl.pallas_call(
        body, name=name, in_specs=[HBM] * (2 * n) + [SEM] * (2 * n) + [ANY],
        out_specs=[HBM] * (2 * n), out_shape=[hbm(a) for a in srcs] + [hbm(a) for a in lands],
        input_output_aliases={k: k for k in range(2 * n)},
        compiler_params=pltpu.CompilerParams(has_side_effects=pltpu.SideEffectType.DATAFLOW_SIDE_EFFECTING),
    )(*srcs, *lands, *[sent["send"][k] for k in ks], *[sent["recv"][k] for k in ks], after)
    return outs[n:], outs[:n]


def _sum_blocks(own, land, ids):
    nseg, _, rows, cols = land.shape

    def body(ids_ref, own_ref, land_ref, o_ref):
        me = ids_ref[1]
        total = None
        for j in range(N_DEV):
            slot = jnp.maximum(jnp.bitwise_xor(me, j) - 1, 0)
            term = jnp.where(me == j, own_ref[...], land_ref[slot]).astype(F32)
            total = term if total is None else total + term
        o_ref[...] = total

    return pl.pallas_call(
        body, name="sum_blocks",
        grid_spec=pltpu.PrefetchScalarGridSpec(
            num_scalar_prefetch=1, grid=(nseg,),
            in_specs=[pl.BlockSpec((None, None, rows, cols), lambda s, ids: (s, ids[0], 0, 0)),
                      pl.BlockSpec((None, N_PEERS, rows, cols), lambda s, ids: (s, 0, 0, 0))],
            out_specs=pl.BlockSpec((None, rows, cols), lambda s, ids: (s, 0, 0))),
        out_shape=jax.ShapeDtypeStruct((nseg, rows, cols), F32),
        compiler_params=_params("parallel"),
    )(ids, own, land)


def _adamw(w, g, m, v):
    shape = w.shape
    cols = shape[-1]
    rows = w.size // cols
    tr = rows
    for cand in (512, 256, 128, 64, 32, 16, 8):
        if rows % cand == 0:
            tr = cand
            break
    c1 = 1.0 / (1.0 - ADAM_B1 ** ADAM_STEP)
    c2 = 1.0 / (1.0 - ADAM_B2 ** ADAM_STEP)

    def body(w_ref, g_ref, m_ref, v_ref, d_ref, nm_ref, nv_ref):
        g = g_ref[...]
        m = ADAM_B1 * m_ref[...] + (1.0 - ADAM_B1) * g
        v = ADAM_B2 * v_ref[...] + (1.0 - ADAM_B2) * (g * g)
        nm_ref[...] = m
        nv_ref[...] = v
        d_ref[...] = -ADAM_LR * ((m * c1) / (jnp.sqrt(v * c2) + ADAM_EPS) + ADAM_WD * w_ref[...])

    spec = pl.BlockSpec((tr, cols), lambda i: (i, 0))
    outs = pl.pallas_call(
        body, name="adamw", grid=(rows // tr,),
        in_specs=[spec] * 4, out_specs=[spec] * 3,
        out_shape=[jax.ShapeDtypeStruct((rows, cols), F32)] * 3,
        compiler_params=_params("parallel"),
    )(*[a.reshape(rows, cols) for a in (w, g, m, v)])
    return [o.reshape(shape) for o in outs]


SMALL = ("ln_mix_pre", "attn_sinks", "gm_ln_g", "gm_ln_b", "gm_ws", "gm_bs", "g_attn_out", "g_gm_out",
         "ln_mix_post", "ln_ffn_pre", "ln_ffn_post", "ln_ple_gate")
WEIGHTS = ("ln_mix_pre", "w_in", "attn_sinks", "gm_ln_g", "gm_ln_b", "gm_ws", "gm_bs", "g_attn_out", "g_gm_out",
           "w_out", "ln_mix_post", "ln_ffn_pre", "w_ffn_gate", "w_ffn_up", "w_ffn_down", "ln_ffn_post", "w_ple",
           "ln_ple_gate", "w_ple_gate")


def _pack_shards(w, l):
    sa = jnp.stack([w["w_ffn_gate"][l].T, w["w_ffn_up"][l].T, w["w_ffn_down"][l]])[:, None]
    sb = jnp.stack([w["w_out"][l], w["w_ple_gate"][l]])[:, None]
    return [w["w_in"][l].T[None, None].astype(BF16), sb.astype(BF16), w["w_ple"][l].T[None, None].astype(BF16),
            sa.astype(BF16)]


def _unpack_grads(rc, rb, rp, ra):
    return {"w_in": rc[0].T, "w_out": rb[0], "w_ple_gate": rb[1], "w_ple": rp[0].T,
            "w_ffn_gate": ra[0].T, "w_ffn_up": ra[1].T, "w_ffn_down": ra[2]}


def kernel(x, p, ln_mix_pre, w_in, attn_sinks, gm_ln_g, gm_ln_b, gm_ws, gm_bs, g_attn_out, g_gm_out, w_out, ln_mix_post, ln_ffn_pre, w_ffn_gate, w_ffn_up, w_ffn_down, ln_ffn_post, w_ple, ln_ple_gate, w_ple_gate, loss_target, m_ln_mix_pre, m_w_in, m_attn_sinks, m_gm_ln_g, m_gm_ln_b, m_gm_ws, m_gm_bs, m_g_attn_out, m_g_gm_out, m_w_out, m_ln_mix_post, m_ln_ffn_pre, m_w_ffn_gate, m_w_ffn_up, m_w_ffn_down, m_ln_ffn_post, m_w_ple, m_ln_ple_gate, m_w_ple_gate, v_ln_mix_pre, v_w_in, v_attn_sinks, v_gm_ln_g, v_gm_ln_b, v_gm_ws, v_gm_bs, v_g_attn_out, v_g_gm_out, v_w_out, v_ln_mix_post, v_ln_ffn_pre, v_w_ffn_gate, v_w_ffn_up, v_w_ffn_down, v_ln_ffn_post, v_w_ple, v_ln_ple_gate, v_w_ple_gate):
    given = dict(locals())
    w = {n: given[n] for n in WEIGHTS}
    sp = {n: w[n] for n in SMALL}
    kinds = ("c", "b", "p", "a")

    me, _ = _peers()
    shards = [s for l in range(DEPTH) for s in _pack_shards(w, l)]
    lands = [lax.dynamic_update_slice(lax.empty((s.shape[0], N_DEV) + s.shape[2:], BF16), s, (0, me, 0, 0))
             for s in shards]
    gather, token = _send_start("gather_start", shards, lands, [GATHER] * len(shards))
    layer_weights = [{} for _ in range(DEPTH)]

    def weights_of(l):
        def get(kind, after):
            have = layer_weights[l]
            if kind not in have:
                if l == 0:
                    group = {"c": ("c",), "b": ("b", "p"), "p": ("b", "p"), "a": ("a",)}[kind]
                    after = token if kind == "c" else after
                else:
                    group = kinds
                got, _ = _send_wait(f"gather_wait_{l}{group[0]}", gather, [4 * l + kinds.index(k) for k in group], after)
                for k, g in zip(group, got):
                    have[k] = g.reshape(-1, g.shape[-1])
            return have[kind]
        return get

    h = x[0]
    saved = []
    for l in range(DEPTH):
        h, s = _layer_fwd(h, p[l, 0], sp, l, weights_of(l))
        saved.append(s)
    dh, sq = _loss_head(h, loss_target[0])

    reduces = []
    after = token
    view = lambda g, rows: g.reshape(-1, N_DEV, rows, g.shape[-1])
    pack16 = lambda s: s.astype(BF16)[None, None]
    landing = lambda a: lax.empty((a.shape[0], N_PEERS) + a.shape[2:], BF16)

    def send(name, bufs, modes):
        return _send_start(name, bufs, [landing(a) for a in bufs], modes)

    for l in reversed(range(DEPTH)):
        lw = layer_weights[l]
        carry, ga, gp, gb = _layer_bwd_upper(dh, saved[l], p[l, 0], sp, l, lw["a"], lw["b"], lw["p"], after)
        sent1, after = send(f"reduce_start_{l}a", [view(ga, ROWS_A), view(gp, ROWS_B)], [SCATTER, SCATTER])
        carry, gb, gating = _layer_bwd_middle(carry, saved[l], sp, l, lw["a"], lw["b"], gb, after)
        sent2, after = send(f"reduce_start_{l}b", [view(gb, ROWS_B), pack16(gating)], [SCATTER, SPREAD])
        dh, gc, rest = _layer_bwd_lower(carry, saved[l], sp, l, lw["c"], after)
        sent3, after = send(f"reduce_start_{l}c", [view(gc, ROWS_C), pack16(rest)], [SCATTER, SPREAD])
        reduces.append((l, sent1, sent2, sent3))

    mine = jnp.stack([me, me]).astype(jnp.int32)
    whole = jnp.stack([jnp.zeros_like(me), me]).astype(jnp.int32)
    per_layer, gatings, rests = [None] * DEPTH, [None] * DEPTH, [None] * DEPTH
    for l, sent1, sent2, sent3 in reduces:
        (la, lp), (ga, gp) = _send_wait(f"reduce_wait_{l}a", sent1, [0, 1], dh)
        (lb, lg), (gb, gg) = _send_wait(f"reduce_wait_{l}b", sent2, [0, 1], dh)
        (lc, lr), (gc, gr) = _send_wait(f"reduce_wait_{l}c", sent3, [0, 1], dh)
        per_layer[l] = _unpack_grads(_sum_blocks(gc, lc, mine), _sum_blocks(gb, lb, mine), _sum_blocks(gp, lp, mine),
                                     _sum_blocks(ga, la, mine))
        gatings[l] = _sum_blocks(gg, lg, whole)[0]
        rests[l] = _sum_blocks(gr, lr, whole)[0]
    grads = {n: jnp.stack([per_layer[l][n] for l in range(DEPTH)]) for n in per_layer[0]}
    grads.update(_small_unpack(jnp.stack(gatings), jnp.stack(rests)))
    grad_x = dh
    loss = lax.psum(sq[0, 0] * (0.5 / D_MODEL), AXES)
    delta, new_m, new_v = {}, {}, {}
    for n in WEIGHTS:
        delta[n], new_m[n], new_v[n] = _adamw(w[n], grads[n], given["m_" + n], given["v_" + n])
    return (loss, grad_x[None], *[grads[n] for n in WEIGHTS], *[delta[n] for n in WEIGHTS],
            *[new_m[n] for n in WEIGHTS], *[new_v[n] for n in WEIGHTS])
```

```python
import math

import jax
import jax.numpy as jnp
from jax import lax
from jax.experimental import pallas as pl
from jax.experimental.pallas import tpu as pltpu

F32 = jnp.float32
BF16 = jnp.bfloat16
MESH = pl.DeviceIdType.MESH
AXES = ("x", "y", "c")

D_MODEL = 1024
DEPTH = 4
N_DEV = 8
HEAD_DIM = 64
ATTN_W = 512
KV_W = 128
GM_W = 512
D_IN = 1792
D_FF = 2816
PLE_DIM = 256
BLK = 128
FF_CHUNK = 256
WGRAD_TOKENS = 1024
NORM_EPS = 1e-6
NEG_BIG = -1e30
ATTN_SCALE = HEAD_DIM ** -0.5

ADAM_LR = 0.001
ADAM_B1 = 0.9
ADAM_B2 = 0.999
ADAM_EPS = 1e-08
ADAM_WD = 0.01
ADAM_STEP = 10

ROWS_A = D_FF // N_DEV
ROWS_B = D_MODEL // N_DEV
ROWS_C = D_IN // N_DEV
GATING_ROWS = 144
REST_ROWS = 56
SMALL_ROWS = 200

VMEM_LIMIT = 56 * 2 ** 20


def _params(*sem):
    return pltpu.CompilerParams(dimension_semantics=sem, vmem_limit_bytes=VMEM_LIMIT)


def _dot(a, b):
    return jnp.dot(a, b, preferred_element_type=F32)


def _dot_nt(a, b):
    return lax.dot_general(a, b, (((1,), (1,)), ((), ())), preferred_element_type=F32)


def _dot_tn(a, b):
    return lax.dot_general(a, b, (((0,), (0,)), ((), ())), preferred_element_type=F32)


def _rms_fwd(x, g):
    r = lax.rsqrt(jnp.mean(x * x, axis=-1, keepdims=True) + NORM_EPS)
    return x * r * g


def _rms_bwd(x, g, dy):
    r = lax.rsqrt(jnp.mean(x * x, axis=-1, keepdims=True) + NORM_EPS)
    xh = x * r
    dg = jnp.sum(dy * xh, axis=0, keepdims=True)
    dxh = dy * g
    dx = r * (dxh - xh * jnp.mean(dxh * xh, axis=-1, keepdims=True))
    return dx, dg


_GELU_C = math.sqrt(2.0 / math.pi)


def _gelu(x):
    t = jnp.tanh(_GELU_C * (x + 0.044715 * (x * x * x)))
    return 0.5 * x * (1.0 + t)


def _gelu_grad(x):
    x2 = x * x
    t = jnp.tanh(_GELU_C * (x + 0.044715 * (x2 * x)))
    return 0.5 * (1.0 + t) + 0.5 * x * (1.0 - t * t) * (_GELU_C * (1.0 + 3.0 * 0.044715 * x2))


def _sigmoid(x):
    return 1.0 / (1.0 + jnp.exp(-x))


def _row_spec(tm, n):
    return pl.BlockSpec((tm, n), lambda i: (i, 0))


def _vec_spec(n):
    return pl.BlockSpec((1, n), lambda i: (0, 0))


def _seg_spec(rows, cols, seg):
    return pl.BlockSpec((N_DEV * rows, cols), lambda i: (seg, 0), pipeline_mode=pl.Buffered(1))


def _zero_at(first, *refs):
    @pl.when(first)
    def _():
        for r in refs:
            r[...] = jnp.zeros(r.shape, r.dtype)


def _tile(t, want):
    return min(t, want)


def _in_proj(h, g, wc, layer):
    t = h.shape[0]
    tm = _tile(t, 512)

    def body(h_ref, g_ref, w_ref, a_ref, q_ref, kv_ref, zu_ref, zv_ref):
        a = _rms_fwd(h_ref[...], g_ref[...]).astype(BF16)
        a_ref[...] = a
        q_ref[...] = _dot_nt(a, w_ref[0:512, :]).astype(BF16)
        kv_ref[...] = _dot_nt(a, w_ref[512:768, :]).astype(BF16)
        zu_ref[...] = _dot_nt(a, w_ref[768:1280, :])
        zv_ref[...] = _dot_nt(a, w_ref[1280:1792, :])

    return pl.pallas_call(
        body, name="in_proj", grid=(t // tm,),
        in_specs=[_row_spec(tm, D_MODEL), _vec_spec(D_MODEL), _seg_spec(ROWS_C, D_MODEL, layer)],
        out_specs=[_row_spec(tm, D_MODEL), _row_spec(tm, ATTN_W), _row_spec(tm, 2 * KV_W),
                   _row_spec(tm, GM_W), _row_spec(tm, GM_W)],
        out_shape=[jax.ShapeDtypeStruct((t, D_MODEL), BF16), jax.ShapeDtypeStruct((t, ATTN_W), BF16),
                   jax.ShapeDtypeStruct((t, 2 * KV_W), BF16), jax.ShapeDtypeStruct((t, GM_W), F32),
                   jax.ShapeDtypeStruct((t, GM_W), F32)],
        compiler_params=_params("parallel"),
    )(h, g, wc)


def _head_variants(x, low):
    xr = pltpu.roll(x, 64, axis=1)
    zero = jnp.zeros_like(x)
    return {
        (0, 0): jnp.where(low, x, zero).astype(BF16),
        (0, 1): jnp.where(low, zero, xr).astype(BF16),
        (1, 0): jnp.where(low, xr, zero).astype(BF16),
        (1, 1): jnp.where(low, zero, x).astype(BF16),
    }


def _attn_masks(i):
    row = lax.broadcasted_iota(jnp.int32, (BLK, BLK), 0)
    lane = lax.broadcasted_iota(jnp.int32, (BLK, BLK), 1)
    vcur = row >= lane
    dist = jnp.where(vcur, row - lane, row - lane + BLK).astype(F32)
    valid = jnp.logical_or(vcur, i > 0)
    return lane < 64, vcur, dist, valid


def _head_key(h):
    return (h // 4, h % 2)


def _attn_scores(q_ref, rows, kc, kp, vcur):
    out = []
    for h in range(8):
        qh = q_ref[rows, (h // 2) * 128:(h // 2 + 1) * 128]
        out.append(jnp.where(vcur, _dot_nt(qh, kc[_head_key(h)]), _dot_nt(qh, kp[_head_key(h)])))
    return out


def _attn_probs(s, h, sink, dist, valid):
    s = s * ATTN_SCALE - (2.0 ** -(h + 1)) * dist
    if valid is not None:
        s = jnp.where(valid, s, NEG_BIG)
    m = jnp.maximum(jnp.max(s, axis=1, keepdims=True), sink)
    e = jnp.exp(s - m)
    es = jnp.exp(sink - m)
    inv = 1.0 / (jnp.sum(e, axis=1, keepdims=True) + es)
    return e * inv, es * inv


def _kv_prev_spec(blocks):
    return pl.BlockSpec((BLK, 2 * KV_W), lambda i: (jnp.maximum(i * blocks - 1, 0), 0))


def _kv_variants(kv_ref, rows, low):
    return (_head_variants(kv_ref[rows, 0:128].astype(F32), low), _head_variants(kv_ref[rows, 128:256].astype(F32), low))


def _attn_fwd(q, kv, sinks):
    t = q.shape[0]
    tq = _tile(t, 512)
    blocks = tq // BLK

    def body(sink_ref, q_ref, kvc_ref, kvp_ref, o_ref):
        low, vcur, dist, valid = _attn_masks(pl.program_id(0))
        kp, vp = _kv_variants(kvp_ref, slice(None), low)
        for b in range(blocks):
            rows = slice(b * BLK, (b + 1) * BLK)
            kc, vc = _kv_variants(kvc_ref, rows, low)
            scores = _attn_scores(q_ref, rows, kc, kp, vcur)
            probs = [_attn_probs(scores[h], h, sink_ref[h], dist, valid if b == 0 else None)[0] for h in range(8)]
            for col in range(4):
                acc = None
                for half in range(2):
                    h = 2 * col + half
                    p = probs[h]
                    o = (_dot(jnp.where(vcur, p, 0.0).astype(BF16), vc[_head_key(h)])
                         + _dot(jnp.where(vcur, 0.0, p).astype(BF16), vp[_head_key(h)]))
                    acc = o if acc is None else acc + o
                o_ref[rows, col * 128:(col + 1) * 128] = acc
            kp, vp = kc, vc

    return pl.pallas_call(
        body, name="attn_fwd", grid=(t // tq,),
        in_specs=[pl.BlockSpec(memory_space=pltpu.SMEM), _row_spec(tq, ATTN_W), _row_spec(tq, 2 * KV_W),
                  _kv_prev_spec(blocks)],
        out_specs=_row_spec(tq, ATTN_W),
        out_shape=jax.ShapeDtypeStruct((t, ATTN_W), F32),
        compiler_params=_params("parallel"),
    )(sinks, q, kv, kv)


def _gm_forward_block(zu, zv, lng, lnb, w_ref, bsx, low):
    gu = _gelu(zu)
    gv = _gelu(zv)
    mu = jnp.mean(gv, axis=-1, keepdims=True)
    xc = gv - mu
    rstd = lax.rsqrt(jnp.mean(xc * xc, axis=-1, keepdims=True) + NORM_EPS)
    xn = xc * rstd
    ln = xn * lng + lnb
    mixed = []
    for col in range(4):
        lc = ln[:, col * 128:(col + 1) * 128]
        lo = jnp.where(low, lc, 0.0).astype(BF16)
        hi = jnp.where(low, 0.0, lc).astype(BF16)
        mixed.append(_dot(w_ref[2 * col], lo) + _dot(w_ref[2 * col + 1], hi) + bsx[:, col * 128:(col + 1) * 128])
    return gu, ln, xn, rstd, mixed


def _gm_fwd(zu, zv, lng, lnb, wtril, bsx):
    t = zu.shape[0]
    tm = _tile(t, 512)

    def body(zu_ref, zv_ref, g_ref, b_ref, w_ref, bs_ref, o_ref):
        low = lax.broadcasted_iota(jnp.int32, (BLK, BLK), 1) < 64
        for b in range(tm // BLK):
            rows = slice(b * BLK, (b + 1) * BLK)
            gu, _, _, _, mixed = _gm_forward_block(zu_ref[rows, :], zv_ref[rows, :], g_ref[...], b_ref[...], w_ref,
                                                   bs_ref[...], low)
            for col in range(4):
                o_ref[rows, col * 128:(col + 1) * 128] = gu[:, col * 128:(col + 1) * 128] * mixed[col]

    return pl.pallas_call(
        body, name="gm_fwd", grid=(t // tm,),
        in_specs=[_row_spec(tm, GM_W), _row_spec(tm, GM_W), _vec_spec(GM_W), _vec_spec(GM_W),
                  pl.BlockSpec((8, BLK, BLK), lambda i: (0, 0, 0)), pl.BlockSpec((BLK, GM_W), lambda i: (0, 0))],
        out_specs=_row_spec(tm, GM_W),
        out_shape=jax.ShapeDtypeStruct((t, GM_W), F32),
        compiler_params=_params("parallel"),
    )(zu, zv, lng, lnb, wtril, bsx)


def _out_proj(attn, gm, h, ga, gg, gpost, wb, layer):
    t = h.shape[0]
    tm = _tile(t, 512)

    def body(a_ref, m_ref, h_ref, ga_ref, gg_ref, gp_ref, w_ref, heads_ref, mix_ref, h1_ref):
        ha = _rms_fwd(a_ref[...], ga_ref[...]).astype(BF16)
        hg = _rms_fwd(m_ref[...], gg_ref[...]).astype(BF16)
        heads_ref[:, 0:512] = ha
        heads_ref[:, 512:1024] = hg
        mix = _dot(ha, w_ref[0:512, :]) + _dot(hg, w_ref[512:1024, :])
        mix_ref[...] = mix
        h1_ref[...] = h_ref[...] + _rms_fwd(mix, gp_ref[...])

    return pl.pallas_call(
        body, name="out_proj", grid=(t // tm,),
        in_specs=[_row_spec(tm, ATTN_W), _row_spec(tm, GM_W), _row_spec(tm, D_MODEL), _vec_spec(ATTN_W),
                  _vec_spec(GM_W), _vec_spec(D_MODEL), _seg_spec(ROWS_B, D_MODEL, 2 * layer)],
        out_specs=[_row_spec(tm, D_MODEL), _row_spec(tm, D_MODEL), _row_spec(tm, D_MODEL)],
        out_shape=[jax.ShapeDtypeStruct((t, D_MODEL), BF16), jax.ShapeDtypeStruct((t, D_MODEL), F32),
                   jax.ShapeDtypeStruct((t, D_MODEL), F32)],
        compiler_params=_params("parallel"),
    )(attn, gm, h, ga, gg, gpost, wb)


def _ffn_fwd(h1, gpre, gpost, wa, layer):
    t = h1.shape[0]
    tm = _tile(t, 256)

    def body(h_ref, gpre_ref, gpost_ref, wg_ref, wu_ref, wd_ref, f_ref, gp_ref, up_ref, act_ref, fo_ref, h2_ref):
        h = h_ref[...]
        f = _rms_fwd(h, gpre_ref[...]).astype(BF16)
        f_ref[...] = f
        chunks = [slice(j * FF_CHUNK, (j + 1) * FF_CHUNK) for j in range(D_FF // FF_CHUNK)]
        fo = None
        gp, up = _dot_nt(f, wg_ref[chunks[0], :]), _dot_nt(f, wu_ref[chunks[0], :])
        for j, cols in enumerate(chunks):
            if j + 1 < len(chunks):
                gp_next, up_next = _dot_nt(f, wg_ref[chunks[j + 1], :]), _dot_nt(f, wu_ref[chunks[j + 1], :])
            act = (gp * _sigmoid(gp) * up).astype(BF16)
            gp_ref[:, cols] = gp.astype(BF16)
            up_ref[:, cols] = up.astype(BF16)
            act_ref[:, cols] = act
            part = _dot(act, wd_ref[cols, :])
            fo = part if fo is None else fo + part
            if j + 1 < len(chunks):
                gp, up = gp_next, up_next
        fo_ref[...] = fo
        h2_ref[...] = h + _rms_fwd(fo, gpost_ref[...])

    wide = _row_spec(tm, D_FF)
    return pl.pallas_call(
        body, name="ffn_fwd", grid=(t // tm,),
        in_specs=[_row_spec(tm, D_MODEL), _vec_spec(D_MODEL), _vec_spec(D_MODEL), _seg_spec(ROWS_A, D_MODEL, 3 * layer),
                  _seg_spec(ROWS_A, D_MODEL, 3 * layer + 1), _seg_spec(ROWS_A, D_MODEL, 3 * layer + 2)],
        out_specs=[_row_spec(tm, D_MODEL), wide, wide, wide, _row_spec(tm, D_MODEL), _row_spec(tm, D_MODEL)],
        out_shape=[jax.ShapeDtypeStruct((t, D_MODEL), BF16)] + [jax.ShapeDtypeStruct((t, D_FF), BF16)] * 3
        + [jax.ShapeDtypeStruct((t, D_MODEL), F32)] * 2,
        compiler_params=_params("parallel"),
    )(h1, gpre, gpost, wa, wa, wa)


def _ple(h2, p, g, wb, wp, layer):
    t = h2.shape[0]
    tm = _tile(t, 512)

    def body(h_ref, p_ref, g_ref, wg_ref, wp_ref, hn_ref, gate_ref, h3_ref):
        h = h_ref[...]
        hn = _rms_fwd(h, g_ref[...]).astype(BF16)
        hn_ref[...] = hn
        gate = _sigmoid(_dot(hn, wg_ref[...]))
        pe = _dot_nt(p_ref[...].astype(BF16), wp_ref[...])
        gate_ref[...] = gate.astype(BF16)
        h3_ref[...] = h + pe * gate

    return pl.pallas_call(
        body, name="ple_fwd", grid=(t // tm,),
        in_specs=[_row_spec(tm, D_MODEL), _row_spec(tm, PLE_DIM), _vec_spec(D_MODEL),
                  _seg_spec(ROWS_B, D_MODEL, 2 * layer + 1), _seg_spec(ROWS_B, PLE_DIM, layer)],
        out_specs=[_row_spec(tm, D_MODEL)] * 3,
        out_shape=[jax.ShapeDtypeStruct((t, D_MODEL), BF16), jax.ShapeDtypeStruct((t, D_MODEL), BF16),
                   jax.ShapeDtypeStruct((t, D_MODEL), F32)],
        compiler_params=_params("parallel"),
    )(h2, p, g, wb, wp)


def _loss_head(y, target):
    t = y.shape[0]
    tm = _tile(t, 512)

    def body(y_ref, t_ref, dy_ref, l_ref):
        _zero_at(pl.program_id(0) == 0, l_ref)
        e = y_ref[...] - t_ref[...]
        dy_ref[...] = e * (1.0 / D_MODEL)
        s = jnp.sum(jnp.sum(e * e, axis=1, keepdims=True), axis=0, keepdims=True)
        l_ref[...] += jnp.broadcast_to(s, (1, 128))

    return pl.pallas_call(
        body, name="loss_head", grid=(t // tm,),
        in_specs=[_row_spec(tm, D_MODEL), _row_spec(tm, D_MODEL)],
        out_specs=[_row_spec(tm, D_MODEL), _vec_spec(128)],
        out_shape=[jax.ShapeDtypeStruct((t, D_MODEL), F32), jax.ShapeDtypeStruct((1, 128), F32)],
        compiler_params=_params("arbitrary"),
    )(y, target)


def _ple_bwd(dh3, h2, gate, p, g, wb, wp, layer, after):
    t = h2.shape[0]
    tm = _tile(t, 512)

    def body(d_ref, h_ref, gate_ref, p_ref, g_ref, w_ref, wp_ref, after_ref, dh2_ref, dgl_ref, dpe_ref, dg_ref):
        _zero_at(pl.program_id(0) == 0, dg_ref)
        d = d_ref[...]
        gate = gate_ref[...].astype(F32)
        pe = _dot_nt(p_ref[...].astype(BF16), wp_ref[...])
        dpe_ref[...] = (d * gate).astype(BF16)
        dgl = (d * pe * gate * (1.0 - gate)).astype(BF16)
        dgl_ref[...] = dgl
        dhn = _dot_nt(dgl, w_ref[...])
        dx, dg = _rms_bwd(h_ref[...], g_ref[...], dhn)
        dh2_ref[...] = d + dx
        dg_ref[...] += dg

    return pl.pallas_call(
        body, name="ple_bwd", grid=(t // tm,),
        in_specs=[_row_spec(tm, D_MODEL)] * 3 + [_row_spec(tm, PLE_DIM), _vec_spec(D_MODEL),
                                                 _seg_spec(ROWS_B, D_MODEL, 2 * layer + 1),
                                                 _seg_spec(ROWS_B, PLE_DIM, layer), pl.BlockSpec(memory_space=pl.ANY)],
        out_specs=[_row_spec(tm, D_MODEL)] * 3 + [_vec_spec(D_MODEL)],
        out_shape=[jax.ShapeDtypeStruct((t, D_MODEL), F32), jax.ShapeDtypeStruct((t, D_MODEL), BF16),
                   jax.ShapeDtypeStruct((t, D_MODEL), BF16), jax.ShapeDtypeStruct((1, D_MODEL), F32)],
        compiler_params=_params("arbitrary"),
    )(dh3, h2, gate, p, g, wb, wp, after)


def _ffn_bwd(dh2, fo, gp, up, h1, gpost, gpre, wa, layer):
    t = dh2.shape[0]
    tm = _tile(t, 256)

    def body(d_ref, fo_ref, gp_ref, up_ref, h_ref, gpost_ref, gpre_ref, wg_ref, wu_ref, wd_ref,
             dfo_ref, dgp_ref, dup_ref, dh1_ref, dgpost_ref, dgpre_ref):
        _zero_at(pl.program_id(0) == 0, dgpost_ref, dgpre_ref)
        d = d_ref[...]
        dfo, dgpost = _rms_bwd(fo_ref[...], gpost_ref[...], d)
        dfo = dfo.astype(BF16)
        dfo_ref[...] = dfo
        dgpost_ref[...] += dgpost
        chunks = [slice(j * FF_CHUNK, (j + 1) * FF_CHUNK) for j in range(D_FF // FF_CHUNK)]
        df = None
        dact = _dot_nt(dfo, wd_ref[chunks[0], :])
        for j, cols in enumerate(chunks):
            if j + 1 < len(chunks):
                dact_next = _dot_nt(dfo, wd_ref[chunks[j + 1], :])
            gp = gp_ref[:, cols].astype(F32)
            sg = _sigmoid(gp)
            dgp = (dact * up_ref[:, cols].astype(F32) * (sg * (1.0 + gp * (1.0 - sg)))).astype(BF16)
            dup = (dact * (gp * sg)).astype(BF16)
            dgp_ref[:, cols] = dgp
            dup_ref[:, cols] = dup
            part = _dot(dgp, wg_ref[cols, :]) + _dot(dup, wu_ref[cols, :])
            df = part if df is None else df + part
            if j + 1 < len(chunks):
                dact = dact_next
        dx, dgpre = _rms_bwd(h_ref[...], gpre_ref[...], df)
        dh1_ref[...] = d + dx
        dgpre_ref[...] += dgpre

    wide = _row_spec(tm, D_FF)
    row = _row_spec(tm, D_MODEL)
    vec = _vec_spec(D_MODEL)
    return pl.pallas_call(
        body, name="ffn_bwd", grid=(t // tm,),
        in_specs=[row, row, wide, wide, row, vec, vec, _seg_spec(ROWS_A, D_MODEL, 3 * layer),
                  _seg_spec(ROWS_A, D_MODEL, 3 * layer + 1), _seg_spec(ROWS_A, D_MODEL, 3 * layer + 2)],
        out_specs=[row, wide, wide, row, vec, vec],
        out_shape=[jax.ShapeDtypeStruct((t, D_MODEL), BF16), jax.ShapeDtypeStruct((t, D_FF), BF16),
                   jax.ShapeDtypeStruct((t, D_FF), BF16), jax.ShapeDtypeStruct((t, D_MODEL), F32),
                   jax.ShapeDtypeStruct((1, D_MODEL), F32), jax.ShapeDtypeStruct((1, D_MODEL), F32)],
        compiler_params=_params("arbitrary"),
    )(dh2, fo, gp, up, h1, gpost, gpre, wa, wa, wa)


def _out_proj_bwd(dh1, mix, attn, gm, gpost, ga, gg, wb, layer, after):
    t = dh1.shape[0]
    tm = _tile(t, 512)

    def body(d_ref, mix_ref, a_ref, m_ref, gp_ref, ga_ref, gg_ref, w_ref, after_ref,
             dmix_ref, da_ref, dm_ref, dgp_ref, dga_ref, dgg_ref):
        _zero_at(pl.program_id(0) == 0, dgp_ref, dga_ref, dgg_ref)
        dmix, dgp = _rms_bwd(mix_ref[...], gp_ref[...], d_ref[...])
        dmix = dmix.astype(BF16)
        dmix_ref[...] = dmix
        da, dga = _rms_bwd(a_ref[...], ga_ref[...], _dot_nt(dmix, w_ref[0:512, :]))
        dm, dgg = _rms_bwd(m_ref[...], gg_ref[...], _dot_nt(dmix, w_ref[512:1024, :]))
        da_ref[...] = da.astype(BF16)
        dm_ref[...] = dm
        dgp_ref[...] += dgp
        dga_ref[...] += dga
        dgg_ref[...] += dgg

    return pl.pallas_call(
        body, name="out_proj_bwd", grid=(t // tm,),
        in_specs=[_row_spec(tm, D_MODEL), _row_spec(tm, D_MODEL), _row_spec(tm, ATTN_W), _row_spec(tm, GM_W),
                  _vec_spec(D_MODEL), _vec_spec(ATTN_W), _vec_spec(GM_W), _seg_spec(ROWS_B, D_MODEL, 2 * layer),
                  pl.BlockSpec(memory_space=pl.ANY)],
        out_specs=[_row_spec(tm, D_MODEL), _row_spec(tm, ATTN_W), _row_spec(tm, GM_W),
                   _vec_spec(D_MODEL), _vec_spec(ATTN_W), _vec_spec(GM_W)],
        out_shape=[jax.ShapeDtypeStruct((t, D_MODEL), BF16), jax.ShapeDtypeStruct((t, ATTN_W), BF16),
                   jax.ShapeDtypeStruct((t, GM_W), F32), jax.ShapeDtypeStruct((1, D_MODEL), F32),
                   jax.ShapeDtypeStruct((1, ATTN_W), F32), jax.ShapeDtypeStruct((1, GM_W), F32)],
        compiler_params=_params("arbitrary"),
    )(dh1, mix, attn, gm, gpost, ga, gg, wb, after)


def _split3(x):
    hi = x.astype(BF16)
    r1 = x - hi.astype(F32)
    mid = r1.astype(BF16)
    lo = (r1 - mid.astype(F32)).astype(BF16)
    return hi, mid, lo


def _gm_bwd(dgm, zu, zv, lng, lnb, wtril, bsx):
    t = zu.shape[0]
    tm = _tile(t, 512)
    nb = t // tm

    def body(d_ref, zu_ref, zv_ref, g_ref, b_ref, w_ref, bs_ref,
             dzu_ref, dzv_ref, dw_ref, dbs_ref, dlg_ref, dlb_ref, dbsx_ref):
        i = pl.program_id(0)
        _zero_at(i == 0, dw_ref, dlg_ref, dlb_ref, dbsx_ref)
        row = lax.broadcasted_iota(jnp.int32, (BLK, BLK), 0)
        lane = lax.broadcasted_iota(jnp.int32, (BLK, BLK), 1)
        low = lane < 64
        tril = row >= lane
        lng = g_ref[...]
        for b in range(tm // BLK):
            rows = slice(b * BLK, (b + 1) * BLK)
            zu = zu_ref[rows, :]
            zv = zv_ref[rows, :]
            gu, ln, xn, rstd, mixed = _gm_forward_block(zu, zv, lng, b_ref[...], w_ref, bs_ref[...], low)
            dgm = d_ref[rows, :]
            dgu_cols, dmx_cols, dln_cols = [], [], []
            for col in range(4):
                sl = slice(col * 128, (col + 1) * 128)
                dg = dgm[:, sl]
                dgu_cols.append(dg * mixed[col])
                dmx = dg * gu[:, sl]
                dmx_cols.append(dmx)
                lc = ln[:, sl]
                halves = (jnp.where(low, lc, 0.0).astype(BF16), jnp.where(low, 0.0, lc).astype(BF16))
                dmx16 = dmx.astype(BF16)
                dmx_half = (jnp.where(low, dmx, 0.0).astype(BF16), jnp.where(low, 0.0, dmx).astype(BF16))
                dln = None
                for half in range(2):
                    hd = 2 * col + half
                    dw_ref[hd] += jnp.where(tril, _dot_nt(dmx16, halves[half]), 0.0)
                    part = _dot_tn(w_ref[hd], dmx_half[half])
                    dln = part if dln is None else dln + part
                dln_cols.append(dln)
            dgu = jnp.concatenate(dgu_cols, axis=1)
            dmx = jnp.concatenate(dmx_cols, axis=1)
            dln = jnp.concatenate(dln_cols, axis=1)
            dzu_ref[rows, :] = (dgu * _gelu_grad(zu)).astype(BF16)
            dbsx_ref[...] += dmx
            dlg_ref[...] += jnp.sum(dln * xn, axis=0, keepdims=True)
            dlb_ref[...] += jnp.sum(dln, axis=0, keepdims=True)
            dxn = dln * lng
            dgv = rstd * (dxn - jnp.mean(dxn, axis=-1, keepdims=True) - xn * jnp.mean(dxn * xn, axis=-1, keepdims=True))
            dzv_ref[rows, :] = (dgv * _gelu_grad(zv)).astype(BF16)

        @pl.when(i == nb - 1)
        def _():
            r = lax.broadcasted_iota(jnp.int32, (GM_W, BLK), 0)
            c = lax.broadcasted_iota(jnp.int32, (GM_W, BLK), 1)
            e = jnp.where(jnp.logical_and(r >= c * 64, r < c * 64 + 64), 1.0, 0.0).astype(BF16)
            hi, mid, lo = _split3(dbsx_ref[...])
            dbs_ref[...] = _dot(hi, e) + _dot(mid, e) + _dot(lo, e)

    vec = _vec_spec(GM_W)
    return pl.pallas_call(
        body, name="gm_bwd", grid=(nb,),
        in_specs=[_row_spec(tm, GM_W)] * 3 + [vec, vec, pl.BlockSpec((8, BLK, BLK), lambda i: (0, 0, 0)),
                                              pl.BlockSpec((BLK, GM_W), lambda i: (0, 0))],
        out_specs=[_row_spec(tm, GM_W), _row_spec(tm, GM_W), pl.BlockSpec((8, BLK, BLK), lambda i: (0, 0, 0)),
                   pl.BlockSpec((BLK, BLK), lambda i: (0, 0)), vec, vec],
        out_shape=[jax.ShapeDtypeStruct((t, GM_W), BF16), jax.ShapeDtypeStruct((t, GM_W), BF16),
                   jax.ShapeDtypeStruct((8, BLK, BLK), F32), jax.ShapeDtypeStruct((BLK, BLK), F32),
                   jax.ShapeDtypeStruct((1, GM_W), F32), jax.ShapeDtypeStruct((1, GM_W), F32)],
        scratch_shapes=[pltpu.VMEM((BLK, GM_W), F32)],
        compiler_params=_params("arbitrary"),
    )(dgm, zu, zv, lng, lnb, wtril, bsx)


def _attn_bwd(q, kv, do, sinks, after):
    t = q.shape[0]
    tq = _tile(t, 512)
    blocks = tq // BLK

    def body(sink_ref, q_ref, kvc_ref, kvp_ref, do_ref, after_ref, dq_ref, dkv_ref, dkf_ref, ds_ref):
        i = pl.program_id(0)
        _zero_at(i == 0, ds_ref)
        low, vcur, dist, valid = _attn_masks(i)
        head_row = lax.broadcasted_iota(jnp.int32, (8, 128), 0)
        dsink_tile = jnp.zeros((8, 128), F32)
        kp, vp = _kv_variants(kvp_ref, slice(None), low)
        own = None
        for b in range(blocks):
            rows = slice(b * BLK, (b + 1) * BLK)
            kc, vc = _kv_variants(kvc_ref, rows, low)
            scores = _attn_scores(q_ref, rows, kc, kp, vcur)
            dprobs = _attn_scores(do_ref, rows, vc, vp, vcur)
            parts = []
            for h in range(8):
                p, ps = _attn_probs(scores[h], h, sink_ref[h], dist, valid if b == 0 else None)
                delta = jnp.sum(p * dprobs[h], axis=1, keepdims=True)
                ds = p * (dprobs[h] - delta) * ATTN_SCALE
                dsink = jnp.sum(-ps * delta, axis=0, keepdims=True)
                dsink_tile = jnp.where(head_row == h, dsink_tile + dsink, dsink_tile)
                parts.append((jnp.where(vcur, ds, 0.0).astype(BF16), jnp.where(vcur, 0.0, ds).astype(BF16),
                              jnp.where(vcur, p, 0.0).astype(BF16), jnp.where(vcur, 0.0, p).astype(BF16)))
            acc = {}

            def add(name, key, val):
                acc[(name, key)] = val if (name, key) not in acc else acc[(name, key)] + val

            for col in range(4):
                dq = None
                qh = q_ref[rows, col * 128:(col + 1) * 128]
                doh = do_ref[rows, col * 128:(col + 1) * 128]
                for half in range(2):
                    h = 2 * col + half
                    key = _head_key(h)
                    dsc, dsp, pc, pp = parts[h]
                    part = _dot(dsc, kc[key]) + _dot(dsp, kp[key])
                    dq = part if dq is None else dq + part
                    add("kc", key, _dot_tn(dsc, qh))
                    add("kp", key, _dot_tn(dsp, qh))
                    add("vc", key, _dot_tn(pc, doh))
                    add("vp", key, _dot_tn(pp, doh))
                dq_ref[rows, col * 128:(col + 1) * 128] = dq.astype(BF16)

            def place(name):
                head0 = acc[(name, (0, 0))] + pltpu.roll(acc[(name, (0, 1))], 64, axis=1)
                head1 = pltpu.roll(acc[(name, (1, 0))], 64, axis=1) + acc[(name, (1, 1))]
                return jnp.where(low, head0, head1)

            before = (place("kp"), place("vp"))
            if b == 0:
                dkf_ref[:, 0:128], dkf_ref[:, 128:256] = before
            else:
                last = slice((b - 1) * BLK, b * BLK)
                dkv_ref[last, 0:128] = own[0] + before[0]
                dkv_ref[last, 128:256] = own[1] + before[1]
            own = (place("kc"), place("vc"))
            kp, vp = kc, vc
        final = slice((blocks - 1) * BLK, blocks * BLK)
        dkv_ref[final, 0:128], dkv_ref[final, 128:256] = own
        ds_ref[...] += dsink_tile

    row_q = _row_spec(tq, ATTN_W)
    row_kv = _row_spec(tq, 2 * KV_W)
    return pl.pallas_call(
        body, name="attn_bwd", grid=(t // tq,),
        in_specs=[pl.BlockSpec(memory_space=pltpu.SMEM), row_q, row_kv, _kv_prev_spec(blocks), row_q,
                  pl.BlockSpec(memory_space=pl.ANY)],
        out_specs=[row_q, row_kv, _row_spec(BLK, 2 * KV_W), pl.BlockSpec((8, 128), lambda i: (0, 0))],
        out_shape=[jax.ShapeDtypeStruct((t, ATTN_W), BF16), jax.ShapeDtypeStruct((t, 2 * KV_W), F32),
                   jax.ShapeDtypeStruct((t // tq * BLK, 2 * KV_W), F32), jax.ShapeDtypeStruct((8, 128), F32)],
        compiler_params=_params("arbitrary"),
    )(sinks, q, kv, kv, do, after)


def _in_proj_bwd(dq, dkv, dkf, dzu, dzv, h, dres, g, wc, layer):
    t = h.shape[0]
    tm = _tile(t, 512)
    steps = t // tm

    def body(dq_ref, dkv_ref, dkn_ref, dzu_ref, dzv_ref, h_ref, d_ref, g_ref, w_ref, dz_ref, dh_ref, dg_ref):
        i = pl.program_id(0)
        _zero_at(i == 0, dg_ref)
        dq = dq_ref[...]
        tail = dkv_ref[tm - BLK:tm, :] + jnp.where(i < steps - 1, dkn_ref[...], 0.0)
        dkv = tail if tm == BLK else jnp.concatenate([dkv_ref[0:tm - BLK, :], tail], axis=0)
        dkv = dkv.astype(BF16)
        dzu = dzu_ref[...]
        dzv = dzv_ref[...]
        dz_ref[:, 0:512] = dq
        dz_ref[:, 512:768] = dkv
        dz_ref[:, 768:1280] = dzu
        dz_ref[:, 1280:1792] = dzv
        da = (_dot(dq, w_ref[0:512, :]) + _dot(dkv, w_ref[512:768, :]) + _dot(dzu, w_ref[768:1280, :])
              + _dot(dzv, w_ref[1280:1792, :]))
        dx, dg = _rms_bwd(h_ref[...], g_ref[...], da)
        dh_ref[...] = d_ref[...] + dx
        dg_ref[...] += dg

    return pl.pallas_call(
        body, name="in_proj_bwd", grid=(t // tm,),
        in_specs=[_row_spec(tm, ATTN_W), _row_spec(tm, 2 * KV_W),
                  pl.BlockSpec((BLK, 2 * KV_W), lambda i: (jnp.minimum(i + 1, steps - 1), 0)), _row_spec(tm, GM_W),
                  _row_spec(tm, GM_W), _row_spec(tm, D_MODEL), _row_spec(tm, D_MODEL), _vec_spec(D_MODEL),
                  _seg_spec(ROWS_C, D_MODEL, layer)],
        out_specs=[_row_spec(tm, D_IN), _row_spec(tm, D_MODEL), _vec_spec(D_MODEL)],
        out_shape=[jax.ShapeDtypeStruct((t, D_IN), BF16), jax.ShapeDtypeStruct((t, D_MODEL), F32),
                   jax.ShapeDtypeStruct((1, D_MODEL), F32)],
        compiler_params=_params("arbitrary"),
    )(dq, dkv, dkf, dzu, dzv, h, dres, g, wc)


def _weight_grad(a, b, buf, seg):
    t, m = a.shape
    n = b.shape[1]
    assert buf.shape[0] % m == 0 and buf.shape[1] == n
    tm = _tile(t, WGRAD_TOKENS)
    steps = t // tm
    half = m // 2

    def body(a_ref, b_ref, buf_ref, o_ref, acc_ref):
        i = pl.program_id(0)
        _zero_at(i == 0, acc_ref)
        b16 = b_ref[...].astype(BF16)
        for rows in (slice(0, half), slice(half, m)):
            acc_ref[rows, :] += _dot_tn(a_ref[:, rows], b16)

        @pl.when(i == steps - 1)
        def _():
            o_ref[...] = acc_ref[...].astype(o_ref.dtype)

    return pl.pallas_call(
        body, name="weight_grad", grid=(steps,),
        in_specs=[_row_spec(tm, m), _row_spec(tm, n), pl.BlockSpec(memory_space=pl.ANY)],
        out_specs=pl.BlockSpec((m, n), lambda i: (seg, 0)),
        out_shape=jax.ShapeDtypeStruct(buf.shape, buf.dtype),
        scratch_shapes=[pltpu.VMEM((m, n), F32)],
        input_output_aliases={2: 0},
        compiler_params=_params("arbitrary"),
    )(a, b, buf)


def _rows8(rows):
    return [jnp.pad(r, ((0, 7), (0, 0))) for r in rows]


def _small_pack_gating(d):
    rows = [jnp.concatenate([d["gm_ln_g"], d["gm_ln_b"]], axis=1), d["gm_bs"].reshape(1, 1024)]
    return jnp.concatenate(_rows8(rows) + [d["gm_ws"].reshape(128, 1024)], axis=0)


def _small_pack_rest(d):
    rows = [d["ln_mix_pre"], d["ln_mix_post"], d["ln_ffn_pre"], d["ln_ffn_post"], d["ln_ple_gate"],
            jnp.concatenate([d["g_attn_out"], d["g_gm_out"]], axis=1),
            jnp.pad(d["attn_sinks"].reshape(1, 8), ((0, 0), (0, 1016)))]
    return jnp.concatenate(_rows8(rows), axis=0)


def _small_unpack(g, s):
    return {
        "gm_ln_g": g[:, 0, :512], "gm_ln_b": g[:, 0, 512:], "gm_bs": g[:, 8].reshape(DEPTH, 8, 128),
        "gm_ws": g[:, 16:144].reshape(DEPTH, 8, 128, 128),
        "ln_mix_pre": s[:, 0], "ln_mix_post": s[:, 8], "ln_ffn_pre": s[:, 16], "ln_ffn_post": s[:, 24],
        "ln_ple_gate": s[:, 32], "g_attn_out": s[:, 40, :512], "g_gm_out": s[:, 40, 512:], "attn_sinks": s[:, 48, :8],
    }


def _row(v):
    return v.reshape(1, -1)


def _layer_fwd(h, p, sp, l, weights):
    tril = jnp.tril(jnp.ones((BLK, BLK), bool))
    wtril = jnp.where(tril[None], sp["gm_ws"][l], 0.0).astype(BF16)
    bsx = jnp.repeat(sp["gm_bs"][l].T, HEAD_DIM, axis=1)
    a, q, kv, zu, zv = _in_proj(h, _row(sp["ln_mix_pre"][l]), weights("c", h), 0)
    attn = _attn_fwd(q, kv, sp["attn_sinks"][l])
    gm = _gm_fwd(zu, zv, _row(sp["gm_ln_g"][l]), _row(sp["gm_ln_b"][l]), wtril, bsx)
    wb = weights("b", gm)
    heads, mix, h1 = _out_proj(attn, gm, h, _row(sp["g_attn_out"][l]), _row(sp["g_gm_out"][l]),
                               _row(sp["ln_mix_post"][l]), wb, 0)
    wa = weights("a", h1)
    f, gpre, up, act, fo, h2 = _ffn_fwd(h1, _row(sp["ln_ffn_pre"][l]), _row(sp["ln_ffn_post"][l]), wa, 0)
    hn, gate, h3 = _ple(h2, p, _row(sp["ln_ple_gate"][l]), wb, weights("p", gm), 0)
    saved = dict(h=h, a=a, q=q, kv=kv, zu=zu, zv=zv, attn=attn, gm=gm, heads=heads, mix=mix, h1=h1, f=f,
                 gpre=gpre, up=up, act=act, fo=fo, h2=h2, hn=hn, gate=gate, wtril=wtril, bsx=bsx)
    return h3, saved


def _layer_bwd_upper(dh, s, p, sp, l, wa, wb, wp, after):
    d = {}
    dh2, dgl, dpe, d["ln_ple_gate"] = _ple_bwd(dh, s["h2"], s["gate"], p, _row(sp["ln_ple_gate"][l]), wb, wp, 0, after)
    gb = _weight_grad(s["hn"], dgl, lax.empty((2 * D_MODEL, D_MODEL), BF16), 1)
    gp = _weight_grad(dpe, p, lax.empty((D_MODEL, PLE_DIM), BF16), 0)
    dfo, dgp, dup, dh1, d["ln_ffn_post"], d["ln_ffn_pre"] = _ffn_bwd(
        dh2, s["fo"], s["gpre"], s["up"], s["h1"], _row(sp["ln_ffn_post"][l]), _row(sp["ln_ffn_pre"][l]), wa, 0)
    ga = _weight_grad(s["act"], dfo, lax.empty((3 * D_FF, D_MODEL), BF16), 2)
    ga = _weight_grad(dgp, s["f"], ga, 0)
    ga = _weight_grad(dup, s["f"], ga, 1)
    return (dh1, d), ga, gp, gb


def _layer_bwd_middle(carry, s, sp, l, wb, gb, after):
    dh1, d = carry
    dmix, dattn, dgm, d["ln_mix_post"], d["g_attn_out"], d["g_gm_out"] = _out_proj_bwd(
        dh1, s["mix"], s["attn"], s["gm"], _row(sp["ln_mix_post"][l]), _row(sp["g_attn_out"][l]),
        _row(sp["g_gm_out"][l]), wb, 0, after)
    gb = _weight_grad(s["heads"], dmix, gb, 0)
    dzu, dzv, d["gm_ws"], dbs, d["gm_ln_g"], d["gm_ln_b"] = _gm_bwd(
        dgm, s["zu"], s["zv"], _row(sp["gm_ln_g"][l]), _row(sp["gm_ln_b"][l]), s["wtril"], s["bsx"])
    d["gm_bs"] = dbs[:, :8].T
    return (dh1, dattn, dzu, dzv, d), gb, _small_pack_gating(d)


def _layer_bwd_lower(carry, s, sp, l, wc, after):
    dh1, dattn, dzu, dzv, d = carry
    dq, dkv, dkf, dsink = _attn_bwd(s["q"], s["kv"], dattn, sp["attn_sinks"][l], after)
    d["attn_sinks"] = dsink[:, 0]
    dz, dh, d["ln_mix_pre"] = _in_proj_bwd(dq, dkv, dkf, dzu, dzv, s["h"], dh1, _row(sp["ln_mix_pre"][l]), wc, 0)
    gc = _weight_grad(dz, s["a"], lax.empty((D_IN, D_MODEL), BF16), 0)
    return dh, gc, _small_pack_rest(d)


ANY = pl.BlockSpec(memory_space=pl.ANY)


def _place():
    x, y, c = lax.axis_index("x"), lax.axis_index("y"), lax.axis_index("c")
    chips = [(1 - x, y), (x, 1 - y), (1 - x, 1 - y)]
    return x, y, c, chips


def _all_gather(shards):
    n = len(shards)

    def body(*refs):
        ins, outs = refs[:n], refs[n:2 * n]
        send_sems, recv_sems, local_sems = refs[2 * n:]
        x, y, c, chips = _place()
        me, sibling = (x, y, c), (x, y, 1 - c)

        def block(k, px, py, pc):
            return outs[k].at[:, pl.ds(4 * px + 2 * py + pc, 1)]

        def copy(k, j, who, to, src=None):
            return pltpu.make_async_remote_copy(
                src_ref=block(k, *who) if src is None else src, dst_ref=block(k, *who),
                send_sem=send_sems.at[7 * k + j], recv_sem=recv_sems.at[7 * k + j],
                device_id=to, device_id_type=MESH)

        mine = [pltpu.make_async_copy(ins[k], block(k, *me), local_sems.at[k]) for k in range(n)]
        for cp in mine:
            cp.start()
        first = []
        for k in range(n):
            first.append(copy(k, 0, me, sibling, src=ins[k]))
            first += [copy(k, 1 + j, me, (*chip, c), src=ins[k]) for j, chip in enumerate(chips)]
        for cp in first:
            cp.start()
        passed = []
        for j, chip in enumerate(chips):
            for k in range(n):
                copy(k, 1 + j, (*chip, c), me).wait_recv()
                cp = copy(k, 4 + j, (*chip, c), sibling)
                cp.start()
                passed.append(cp)
        for k in range(n):
            copy(k, 0, sibling, me).wait_recv()
            for j, chip in enumerate(chips):
                copy(k, 4 + j, (*chip, 1 - c), me).wait_recv()
        for cp in first + passed:
            cp.wait_send()
        for cp in mine:
            cp.wait()

    return pl.pallas_call(
        body, name="all_gather_weights",
        in_specs=[ANY] * n, out_specs=[ANY] * n,
        out_shape=[jax.ShapeDtypeStruct((s.shape[0], N_DEV) + s.shape[2:], s.dtype) for s in shards],
        scratch_shapes=[pltpu.SemaphoreType.DMA((7 * n,)), pltpu.SemaphoreType.DMA((7 * n,)),
                        pltpu.SemaphoreType.DMA((n,))],
        compiler_params=pltpu.CompilerParams(has_side_effects=True),
    )(*shards)


def _sibling_exchange(bufs, small):
    n = len(bufs)

    def body(*refs):
        ins, small_ref = refs[:n], refs[n]
        outs, both_ref = refs[n + 1:2 * n + 1], refs[2 * n + 1]
        send_sems, recv_sems, local_sem = refs[2 * n + 2:]
        x, y, c, _ = _place()
        sibling = (x, y, 1 - c)
        mine = pltpu.make_async_copy(small_ref, both_ref.at[c], local_sem)
        mine.start()
        copies = [pltpu.make_async_remote_copy(
            src_ref=ins[k].at[:, :, pl.ds(1 - c, 1)], dst_ref=outs[k], send_sem=send_sems.at[k],
            recv_sem=recv_sems.at[k], device_id=sibling, device_id_type=MESH) for k in range(n)]
        copies.append(pltpu.make_async_remote_copy(
            src_ref=small_ref, dst_ref=both_ref.at[c], send_sem=send_sems.at[n], recv_sem=recv_sems.at[n],
            device_id=sibling, device_id_type=MESH))
        for cp in copies:
            cp.start()
        for k in range(n):
            copies[k].wait_recv()
        pltpu.make_async_remote_copy(
            src_ref=small_ref, dst_ref=both_ref.at[1 - c], send_sem=send_sems.at[n], recv_sem=recv_sems.at[n],
            device_id=sibling, device_id_type=MESH).wait_recv()
        for cp in copies:
            cp.wait_send()
        mine.wait()

    return pl.pallas_call(
        body, name="sibling_exchange",
        in_specs=[ANY] * (n + 1), out_specs=[ANY] * (n + 1),
        out_shape=[jax.ShapeDtypeStruct(b.shape[:2] + (1,) + b.shape[3:], b.dtype) for b in bufs]
        + [jax.ShapeDtypeStruct((2,) + small.shape, small.dtype)],
        scratch_shapes=[pltpu.SemaphoreType.DMA((n + 1,)), pltpu.SemaphoreType.DMA((n + 1,)), pltpu.SemaphoreType.DMA],
        compiler_params=pltpu.CompilerParams(has_side_effects=True),
    )(*bufs, small)


def _chip_exchange(sends, small):
    n = len(sends)

    def body(*refs):
        ins, small_ref = refs[:n], refs[n]
        outs, all_ref = refs[n + 1:2 * n + 1], refs[2 * n + 1]
        send_sems, recv_sems, local_sem = refs[2 * n + 2:]
        x, y, c, chips = _place()
        mine = pltpu.make_async_copy(small_ref, all_ref.at[2 * x + y], local_sem)
        mine.start()
        copies = []
        for j, chip in enumerate(chips):
            for k in range(n):
                copies.append(pltpu.make_async_remote_copy(
                    src_ref=ins[k].at[j], dst_ref=outs[k].at[j], send_sem=send_sems.at[3 * k + j],
                    recv_sem=recv_sems.at[3 * k + j], device_id=(*chip, c), device_id_type=MESH))
            copies.append(pltpu.make_async_remote_copy(
                src_ref=small_ref, dst_ref=all_ref.at[2 * x + y], send_sem=send_sems.at[3 * n + j],
                recv_sem=recv_sems.at[3 * n + j], device_id=(*chip, c), device_id_type=MESH))
        for cp in copies:
            cp.start()
        for j, (px, py) in enumerate(chips):
            for k in range(n):
                copies[j * (n + 1) + k].wait_recv()
            pltpu.make_async_remote_copy(
                src_ref=small_ref, dst_ref=all_ref.at[2 * px + py], send_sem=send_sems.at[3 * n + j],
                recv_sem=recv_sems.at[3 * n + j], device_id=(px, py, c), device_id_type=MESH).wait_recv()
        for cp in copies:
            cp.wait_send()
        mine.wait()

    return pl.pallas_call(
        body, name="chip_exchange",
        in_specs=[ANY] * (n + 1), out_specs=[ANY] * (n + 1),
        out_shape=[jax.ShapeDtypeStruct(s.shape, s.dtype) for s in sends]
        + [jax.ShapeDtypeStruct((4,) + small.shape, small.dtype)],
        scratch_shapes=[pltpu.SemaphoreType.DMA((3 * n + 3,)), pltpu.SemaphoreType.DMA((3 * n + 3,)),
                        pltpu.SemaphoreType.DMA],
        compiler_params=pltpu.CompilerParams(has_side_effects=True),
    )(*sends, small)


def _pair_add(buf, got, chip_ids, dtype):
    nseg, _, _, rows, cols = buf.shape
    nr = chip_ids.shape[0] - 1

    def body(ids_ref, a_ref, b_ref, o_ref):
        o_ref[...] = (a_ref[...] + b_ref[...]).astype(dtype)

    return pl.pallas_call(
        body, name="pair_add",
        grid_spec=pltpu.PrefetchScalarGridSpec(
            num_scalar_prefetch=1, grid=(nr, nseg),
            in_specs=[pl.BlockSpec((None, None, None, rows, cols), lambda r, s, ids: (s, ids[r], ids[nr], 0, 0)),
                      pl.BlockSpec((None, None, None, rows, cols), lambda r, s, ids: (s, ids[r], 0, 0, 0))],
            out_specs=pl.BlockSpec((None, None, rows, cols), lambda r, s, ids: (r, s, 0, 0))),
        out_shape=jax.ShapeDtypeStruct((nr, nseg, rows, cols), dtype),
        compiler_params=_params("parallel", "parallel"),
    )(chip_ids, buf, got)


def _sum_slots(z, tr):
    n, rows, cols = z.shape

    def body(z_ref, o_ref):
        s = z_ref[0]
        for k in range(1, n):
            s = s + z_ref[k]
        o_ref[...] = s

    return pl.pallas_call(
        body, name="sum_slots", grid=(rows // tr,),
        in_specs=[pl.BlockSpec((n, tr, cols), lambda i: (0, i, 0))],
        out_specs=pl.BlockSpec((tr, cols), lambda i: (i, 0)),
        out_shape=jax.ShapeDtypeStruct((rows, cols), F32),
        compiler_params=_params("parallel"),
    )(z)


def _final_sum(own, got):
    _, nseg, rows, cols = own.shape

    def body(a_ref, b_ref, o_ref):
        s = a_ref[0]
        for k in range(3):
            s = s + b_ref[k].astype(F32)
        o_ref[...] = s

    return pl.pallas_call(
        body, name="final_sum", grid=(nseg,),
        in_specs=[pl.BlockSpec((1, None, rows, cols), lambda i: (0, i, 0, 0)),
                  pl.BlockSpec((3, None, rows, cols), lambda i: (0, i, 0, 0))],
        out_specs=pl.BlockSpec((None, rows, cols), lambda i: (i, 0, 0)),
        out_shape=jax.ShapeDtypeStruct((nseg, rows, cols), F32),
        compiler_params=_params("parallel"),
    )(own, got)


HBM = pl.BlockSpec(memory_space=pltpu.HBM)
SEM = pl.BlockSpec(memory_space=pltpu.SEMAPHORE)
N_PEERS = N_DEV - 1


def _peers():
    x, y, c = lax.axis_index("x"), lax.axis_index("y"), lax.axis_index("c")
    peers = []
    for r in range(1, N_DEV):
        px = 1 - x if r & 4 else x
        py = 1 - y if r & 2 else y
        pc = 1 - c if r & 1 else c
        peers.append(((px, py, pc), 4 * px + 2 * py + pc))
    return 4 * x + 2 * y + c, peers


GATHER, SCATTER, SPREAD = "gather", "scatter", "spread"


def _peer_copy(src, land, send_sems, recv_sems, r, me, peer, peer_slot, mode):
    return pltpu.make_async_remote_copy(
        src_ref=src.at[:, pl.ds(peer_slot, 1)] if mode == SCATTER else src,
        dst_ref=land.at[:, pl.ds(me, 1)] if mode == GATHER else land.at[:, pl.ds(r - 1, 1)],
        send_sem=send_sems.at[r - 1], recv_sem=recv_sems.at[r - 1], device_id=peer, device_id_type=MESH)


def _peer_arrival(src, land, send_sems, recv_sems, r, me, peer, peer_slot, mode):
    return pltpu.make_async_remote_copy(
        src_ref=src.at[:, pl.ds(me, 1)] if mode == SCATTER else src,
        dst_ref=land.at[:, pl.ds(peer_slot, 1)] if mode == GATHER else land.at[:, pl.ds(r - 1, 1)],
        send_sem=send_sems.at[r - 1], recv_sem=recv_sems.at[r - 1], device_id=peer, device_id_type=MESH)


def _send_start(name, srcs, lands, modes):
    n = len(srcs)

    def body(*refs):
        src_refs, land_refs = refs[:n], refs[n:2 * n]
        outs = refs[2 * n:]
        send_sems, recv_sems, token = outs[2 * n:3 * n], outs[3 * n:4 * n], outs[4 * n]
        me, peers = _peers()
        for k in range(n):
            for r, (peer, slot) in enumerate(peers, 1):
                _peer_copy(src_refs[k], land_refs[k], send_sems[k], recv_sems[k], r, me, peer, slot, modes[k]).start()
        token[...] = jnp.zeros_like(token)

    hbm = lambda a: pltpu.HBM(a.shape, a.dtype)
    sems = [pltpu.SemaphoreType.DMA((N_PEERS,))] * (2 * n)
    outs = pl.pallas_call(
        body, name=name, in_specs=[HBM] * (2 * n),
        out_specs=[HBM] * (2 * n) + [SEM] * (2 * n) + [pl.BlockSpec(memory_space=pltpu.VMEM)],
        out_shape=[hbm(a) for a in srcs] + [hbm(a) for a in lands] + sems + [jax.ShapeDtypeStruct((8, 128), F32)],
        input_output_aliases={k: k for k in range(2 * n)},
        compiler_params=pltpu.CompilerParams(has_side_effects=pltpu.SideEffectType.DATAFLOW_SIDE_EFFECTING),
    )(*[pltpu.with_memory_space_constraint(a, pltpu.HBM) for a in list(srcs) + list(lands)])
    return dict(srcs=outs[:n], lands=outs[n:2 * n], send=outs[2 * n:3 * n], recv=outs[3 * n:4 * n],
                modes=list(modes)), outs[4 * n]


def _send_wait(name, sent, ks, after):
    n = len(ks)
    srcs = [sent["srcs"][k] for k in ks]
    lands = [sent["lands"][k] for k in ks]
    modes = [sent["modes"][k] for k in ks]

    def body(*refs):
        src_refs, land_refs = refs[:n], refs[n:2 * n]
        send_sems, recv_sems = refs[2 * n:3 * n], refs[3 * n:4 * n]
        me, peers = _peers()
        for k in range(n):
            for r, (peer, slot) in enumerate(peers, 1):
                args = (src_refs[k], land_refs[k], send_sems[k], recv_sems[k], r, me, peer, slot, modes[k])
                _peer_copy(*args).wait_send()
                _peer_arrival(*args).wait_recv()

    hbm = lambda a: pltpu.HBM(a.shape, a.dtype)
    outs = pl.pallas_call(
        body, name=name, in_specs=[HBM] * (2 * n) + [SEM] * (2 * n) + [ANY],
        out_specs=[HBM] * (2 * n), out_shape=[hbm(a) for a in srcs] + [hbm(a) for a in lands],
        input_output_aliases={k: k for k in range(2 * n)},
        compiler_params=pltpu.CompilerParams(has_side_effects=pltpu.SideEffectType.DATAFLOW_SIDE_EFFECTING),
    )(*srcs, *lands, *[sent["send"][k] for k in ks], *[sent["recv"][k] for k in ks], after)
    return outs[n:], outs[:n]


def _sum_blocks(own, land, ids):
    nseg, _, rows, cols = land.shape

    def body(ids_ref, own_ref, land_ref, o_ref):
        me = ids_ref[1]
        total = None
        for j in range(N_DEV):
            slot = jnp.maximum(jnp.bitwise_xor(me, j) - 1, 0)
            term = jnp.where(me == j, own_ref[...], land_ref[slot]).astype(F32)
            total = term if total is None else total + term
        o_ref[...] = total

    return pl.pallas_call(
        body, name="sum_blocks",
        grid_spec=pltpu.PrefetchScalarGridSpec(
            num_scalar_prefetch=1, grid=(nseg,),
            in_specs=[pl.BlockSpec((None, None, rows, cols), lambda s, ids: (s, ids[0], 0, 0)),
                      pl.BlockSpec((None, N_PEERS, rows, cols), lambda s, ids: (s, 0, 0, 0))],
            out_specs=pl.BlockSpec((None, rows, cols), lambda s, ids: (s, 0, 0))),
        out_shape=jax.ShapeDtypeStruct((nseg, rows, cols), F32),
        compiler_params=_params("parallel"),
    )(ids, own, land)


def _adamw(w, g, m, v):
    shape = w.shape
    cols = shape[-1]
    rows = w.size // cols
    tr = rows
    for cand in (512, 256, 128, 64, 32, 16, 8):
        if rows % cand == 0:
            tr = cand
            break
    c1 = 1.0 / (1.0 - ADAM_B1 ** ADAM_STEP)
    c2 = 1.0 / (1.0 - ADAM_B2 ** ADAM_STEP)

    def body(w_ref, g_ref, m_ref, v_ref, d_ref, nm_ref, nv_ref):
        g = g_ref[...]
        m = ADAM_B1 * m_ref[...] + (1.0 - ADAM_B1) * g
        v = ADAM_B2 * v_ref[...] + (1.0 - ADAM_B2) * (g * g)
        nm_ref[...] = m
        nv_ref[...] = v
        d_ref[...] = -ADAM_LR * ((m * c1) / (jnp.sqrt(v * c2) + ADAM_EPS) + ADAM_WD * w_ref[...])

    spec = pl.BlockSpec((tr, cols), lambda i: (i, 0))
    outs = pl.pallas_call(
        body, name="adamw", grid=(rows // tr,),
        in_specs=[spec] * 4, out_specs=[spec] * 3,
        out_shape=[jax.ShapeDtypeStruct((rows, cols), F32)] * 3,
        compiler_params=_params("parallel"),
    )(*[a.reshape(rows, cols) for a in (w, g, m, v)])
    return [o.reshape(shape) for o in outs]


SMALL = ("ln_mix_pre", "attn_sinks", "gm_ln_g", "gm_ln_b", "gm_ws", "gm_bs", "g_attn_out", "g_gm_out",
         "ln_mix_post", "ln_ffn_pre", "ln_ffn_post", "ln_ple_gate")
WEIGHTS = ("ln_mix_pre", "w_in", "attn_sinks", "gm_ln_g", "gm_ln_b", "gm_ws", "gm_bs", "g_attn_out", "g_gm_out",
           "w_out", "ln_mix_post", "ln_ffn_pre", "w_ffn_gate", "w_ffn_up", "w_ffn_down", "ln_ffn_post", "w_ple",
           "ln_ple_gate", "w_ple_gate")


def _pack_shards(w, l):
    sa = jnp.stack([w["w_ffn_gate"][l].T, w["w_ffn_up"][l].T, w["w_ffn_down"][l]])[:, None]
    sb = jnp.stack([w["w_out"][l], w["w_ple_gate"][l]])[:, None]
    return [w["w_in"][l].T[None, None].astype(BF16), sb.astype(BF16), w["w_ple"][l].T[None, None].astype(BF16),
            sa.astype(BF16)]


def _unpack_grads(rc, rb, rp, ra):
    return {"w_in": rc[0].T, "w_out": rb[0], "w_ple_gate": rb[1], "w_ple": rp[0].T,
            "w_ffn_gate": ra[0].T, "w_ffn_up": ra[1].T, "w_ffn_down": ra[2]}


def kernel(x, p, ln_mix_pre, w_in, attn_sinks, gm_ln_g, gm_ln_b, gm_ws, gm_bs, g_attn_out, g_gm_out, w_out, ln_mix_post, ln_ffn_pre, w_ffn_gate, w_ffn_up, w_ffn_down, ln_ffn_post, w_ple, ln_ple_gate, w_ple_gate, loss_target, m_ln_mix_pre, m_w_in, m_attn_sinks, m_gm_ln_g, m_gm_ln_b, m_gm_ws, m_gm_bs, m_g_attn_out, m_g_gm_out, m_w_out, m_ln_mix_post, m_ln_ffn_pre, m_w_ffn_gate, m_w_ffn_up, m_w_ffn_down, m_ln_ffn_post, m_w_ple, m_ln_ple_gate, m_w_ple_gate, v_ln_mix_pre, v_w_in, v_attn_sinks, v_gm_ln_g, v_gm_ln_b, v_gm_ws, v_gm_bs, v_g_attn_out, v_g_gm_out, v_w_out, v_ln_mix_post, v_ln_ffn_pre, v_w_ffn_gate, v_w_ffn_up, v_w_ffn_down, v_ln_ffn_post, v_w_ple, v_ln_ple_gate, v_w_ple_gate):
    given = dict(locals())
    w = {n: given[n] for n in WEIGHTS}
    sp = {n: w[n] for n in SMALL}
    kinds = ("c", "b", "p", "a")

    me, _ = _peers()
    shards = [s for l in range(DEPTH) for s in _pack_shards(w, l)]
    lands = [lax.dynamic_update_slice(lax.empty((s.shape[0], N_DEV) + s.shape[2:], BF16), s, (0, me, 0, 0))
             for s in shards]
    gather, token = _send_start("gather_start", shards, lands, [GATHER] * len(shards))
    layer_weights = [{} for _ in range(DEPTH)]

    def weights_of(l):
        def get(kind, after):
            have = layer_weights[l]
            if kind not in have:
                if l == 0:
                    group = {"c": ("c",), "b": ("b", "p"), "p": ("b", "p"), "a": ("a",)}[kind]
                    after = token if kind == "c" else after
                else:
                    group = kinds
                got, _ = _send_wait(f"gather_wait_{l}{group[0]}", gather, [4 * l + kinds.index(k) for k in group], after)
                for k, g in zip(group, got):
                    have[k] = g.reshape(-1, g.shape[-1])
            return have[kind]
        return get

    h = x[0]
    saved = []
    for l in range(DEPTH):
        h, s = _layer_fwd(h, p[l, 0], sp, l, weights_of(l))
        saved.append(s)
    dh, sq = _loss_head(h, loss_target[0])

    reduces = []
    after = token
    view = lambda g, rows: g.reshape(-1, N_DEV, rows, g.shape[-1])
    pack16 = lambda s: s.astype(BF16)[None, None]
    landing = lambda a: lax.empty((a.shape[0], N_PEERS) + a.shape[2:], BF16)

    def send(name, bufs, modes):
        return _send_start(name, bufs, [landing(a) for a in bufs], modes)

    for l in reversed(range(DEPTH)):
        lw = layer_weights[l]
        carry, ga, gp, gb = _layer_bwd_upper(dh, saved[l], p[l, 0], sp, l, lw["a"], lw["b"], lw["p"], after)
        sent1, after = send(f"reduce_start_{l}a", [view(ga, ROWS_A), view(gp, ROWS_B)], [SCATTER, SCATTER])
        carry, gb, gating = _layer_bwd_middle(carry, saved[l], sp, l, lw["b"], gb, after)
        sent2, after = send(f"reduce_start_{l}b", [view(gb, ROWS_B), pack16(gating)], [SCATTER, SPREAD])
        dh, gc, rest = _layer_bwd_lower(carry, saved[l], sp, l, lw["c"], after)
        sent3, after = send(f"reduce_start_{l}c", [view(gc, ROWS_C), pack16(rest)], [SCATTER, SPREAD])
        reduces.append((l, sent1, sent2, sent3))

    mine = jnp.stack([me, me]).astype(jnp.int32)
    whole = jnp.stack([jnp.zeros_like(me), me]).astype(jnp.int32)
    per_layer, gatings, rests = [None] * DEPTH, [None] * DEPTH, [None] * DEPTH
    for l, sent1, sent2, sent3 in reduces:
        (la, lp), (ga, gp) = _send_wait(f"reduce_wait_{l}a", sent1, [0, 1], dh)
        (lb, lg), (gb, gg) = _send_wait(f"reduce_wait_{l}b", sent2, [0, 1], dh)
        (lc, lr), (gc, gr) = _send_wait(f"reduce_wait_{l}c", sent3, [0, 1], dh)
        per_layer[l] = _unpack_grads(_sum_blocks(gc, lc, mine), _sum_blocks(gb, lb, mine), _sum_blocks(gp, lp, mine),
                                     _sum_blocks(ga, la, mine))
        gatings[l] = _sum_blocks(gg, lg, whole)[0]
        rests[l] = _sum_blocks(gr, lr, whole)[0]
    grads = {n: jnp.stack([per_layer[l][n] for l in range(DEPTH)]) for n in per_layer[0]}
    grads.update(_small_unpack(jnp.stack(gatings), jnp.stack(rests)))
    grad_x = dh
    loss = lax.psum(sq[0, 0] * (0.5 / D_MODEL), AXES)
    delta, new_m, new_v = {}, {}, {}
    for n in WEIGHTS:
        delta[n], new_m[n], new_v[n] = _adamw(w[n], grads[n], given["m_" + n], given["v_" + n])
    return (loss, grad_x[None], *[grads[n] for n in WEIGHTS], *[delta[n] for n in WEIGHTS],
            *[new_m[n] for n in WEIGHTS], *[new_v[n] for n in WEIGHTS])
```

```python
import math

import jax
import jax.numpy as jnp
from jax import lax
from jax.experimental import pallas as pl
from jax.experimental.pallas import tpu as pltpu

F32 = jnp.float32
BF16 = jnp.bfloat16
MESH = pl.DeviceIdType.MESH
AXES = ("x", "y", "c")

D_MODEL = 1024
DEPTH = 4
N_DEV = 8
HEAD_DIM = 64
ATTN_W = 512
KV_W = 128
GM_W = 512
D_IN = 1792
D_FF = 2816
PLE_DIM = 256
BLK = 128
FF_CHUNK = 256
WGRAD_TOKENS = 1024
NORM_EPS = 1e-6
NEG_BIG = -1e30
ATTN_SCALE = HEAD_DIM ** -0.5

ADAM_LR = 0.001
ADAM_B1 = 0.9
ADAM_B2 = 0.999
ADAM_EPS = 1e-08
ADAM_WD = 0.01
ADAM_STEP = 10

ROWS_A = D_FF // N_DEV
ROWS_B = D_MODEL // N_DEV
ROWS_C = D_IN // N_DEV
GATING_ROWS = 144
REST_ROWS = 56
SMALL_ROWS = 200

VMEM_LIMIT = 56 * 2 ** 20


def _params(*sem):
    return pltpu.CompilerParams(dimension_semantics=sem, vmem_limit_bytes=VMEM_LIMIT)


def _dot(a, b):
    return jnp.dot(a, b, preferred_element_type=F32)


def _dot_nt(a, b):
    return lax.dot_general(a, b, (((1,), (1,)), ((), ())), preferred_element_type=F32)


def _dot_tn(a, b):
    return lax.dot_general(a, b, (((0,), (0,)), ((), ())), preferred_element_type=F32)


def _rms_fwd(x, g):
    r = lax.rsqrt(jnp.mean(x * x, axis=-1, keepdims=True) + NORM_EPS)
    return x * r * g


def _rms_bwd(x, g, dy):
    r = lax.rsqrt(jnp.mean(x * x, axis=-1, keepdims=True) + NORM_EPS)
    xh = x * r
    dg = jnp.sum(dy * xh, axis=0, keepdims=True)
    dxh = dy * g
    dx = r * (dxh - xh * jnp.mean(dxh * xh, axis=-1, keepdims=True))
    return dx, dg


_GELU_C = math.sqrt(2.0 / math.pi)


def _gelu(x):
    t = jnp.tanh(_GELU_C * (x + 0.044715 * (x * x * x)))
    return 0.5 * x * (1.0 + t)


def _gelu_grad(x):
    x2 = x * x
    t = jnp.tanh(_GELU_C * (x + 0.044715 * (x2 * x)))
    return 0.5 * (1.0 + t) + 0.5 * x * (1.0 - t * t) * (_GELU_C * (1.0 + 3.0 * 0.044715 * x2))


def _sigmoid(x):
    return 1.0 / (1.0 + jnp.exp(-x))


def _row_spec(tm, n):
    return pl.BlockSpec((tm, n), lambda i: (i, 0))


def _vec_spec(n):
    return pl.BlockSpec((1, n), lambda i: (0, 0))


def _seg_spec(rows, cols, seg):
    return pl.BlockSpec((N_DEV * rows, cols), lambda i: (seg, 0), pipeline_mode=pl.Buffered(1))


def _zero_at(first, *refs):
    @pl.when(first)
    def _():
        for r in refs:
            r[...] = jnp.zeros(r.shape, r.dtype)


def _tile(t, want):
    return min(t, want)


def _in_proj(h, g, wc, layer):
    t = h.shape[0]
    tm = _tile(t, 512)

    def body(h_ref, g_ref, w_ref, a_ref, q_ref, kv_ref, zu_ref, zv_ref):
        a = _rms_fwd(h_ref[...], g_ref[...]).astype(BF16)
        a_ref[...] = a
        q_ref[...] = _dot_nt(a, w_ref[0:512, :]).astype(BF16)
        kv_ref[...] = _dot_nt(a, w_ref[512:768, :]).astype(BF16)
        zu_ref[...] = _dot_nt(a, w_ref[768:1280, :])
        zv_ref[...] = _dot_nt(a, w_ref[1280:1792, :])

    return pl.pallas_call(
        body, name="in_proj", grid=(t // tm,),
        in_specs=[_row_spec(tm, D_MODEL), _vec_spec(D_MODEL), _seg_spec(ROWS_C, D_MODEL, layer)],
        out_specs=[_row_spec(tm, D_MODEL), _row_spec(tm, ATTN_W), _row_spec(tm, 2 * KV_W),
                   _row_spec(tm, GM_W), _row_spec(tm, GM_W)],
        out_shape=[jax.ShapeDtypeStruct((t, D_MODEL), BF16), jax.ShapeDtypeStruct((t, ATTN_W), BF16),
                   jax.ShapeDtypeStruct((t, 2 * KV_W), BF16), jax.ShapeDtypeStruct((t, GM_W), F32),
                   jax.ShapeDtypeStruct((t, GM_W), F32)],
        compiler_params=_params("parallel"),
    )(h, g, wc)


def _head_variants(x, low):
    xr = pltpu.roll(x, 64, axis=1)
    zero = jnp.zeros_like(x)
    return {
        (0, 0): jnp.where(low, x, zero).astype(BF16),
        (0, 1): jnp.where(low, zero, xr).astype(BF16),
        (1, 0): jnp.where(low, xr, zero).astype(BF16),
        (1, 1): jnp.where(low, zero, x).astype(BF16),
    }


def _attn_masks(i):
    row = lax.broadcasted_iota(jnp.int32, (BLK, BLK), 0)
    lane = lax.broadcasted_iota(jnp.int32, (BLK, BLK), 1)
    vcur = row >= lane
    dist = jnp.where(vcur, row - lane, row - lane + BLK).astype(F32)
    valid = jnp.logical_or(vcur, i > 0)
    return lane < 64, vcur, dist, valid


def _head_key(h):
    return (h // 4, h % 2)


def _stack_kv(prev, cur, g):
    return jnp.concatenate([prev[(g, 0)], cur[(g, 0)], prev[(g, 1)], cur[(g, 1)]], axis=0)


def _split_cols(p, vcur):
    return [jnp.where(vcur, 0.0, p).astype(BF16), jnp.where(vcur, p, 0.0).astype(BF16)]


def _attn_scores(q_ref, rows, stacked, vcur):
    out = []
    for col in range(4):
        big = _dot_nt(q_ref[rows, col * 128:(col + 1) * 128], stacked[col // 2])
        for half in range(2):
            out.append(jnp.where(vcur, big[:, half * 256 + 128:half * 256 + 256], big[:, half * 256:half * 256 + 128]))
    return out


def _attn_probs(s, h, sink, dist, valid):
    s = s * ATTN_SCALE - (2.0 ** -(h + 1)) * dist
    if valid is not None:
        s = jnp.where(valid, s, NEG_BIG)
    m = jnp.maximum(jnp.max(s, axis=1, keepdims=True), sink)
    e = jnp.exp(s - m)
    es = jnp.exp(sink - m)
    inv = 1.0 / (jnp.sum(e, axis=1, keepdims=True) + es)
    return e * inv, es * inv


def _kv_prev_spec(blocks):
    return pl.BlockSpec((BLK, 2 * KV_W), lambda i: (jnp.maximum(i * blocks - 1, 0), 0))


def _kv_variants(kv_ref, rows, low):
    return (_head_variants(kv_ref[rows, 0:128].astype(F32), low), _head_variants(kv_ref[rows, 128:256].astype(F32), low))


def _attn_fwd(q, kv, sinks):
    t = q.shape[0]
    tq = _tile(t, 512)
    blocks = tq // BLK

    def body(sink_ref, q_ref, kvc_ref, kvp_ref, o_ref):
        low, vcur, dist, valid = _attn_masks(pl.program_id(0))
        kp, vp = _kv_variants(kvp_ref, slice(None), low)
        for b in range(blocks):
            rows = slice(b * BLK, (b + 1) * BLK)
            kc, vc = _kv_variants(kvc_ref, rows, low)
            ks = [_stack_kv(kp, kc, g) for g in range(2)]
            vs = [_stack_kv(vp, vc, g) for g in range(2)]
            scores = _attn_scores(q_ref, rows, ks, vcur)
            probs = [_attn_probs(scores[h], h, sink_ref[h], dist, valid if b == 0 else None)[0] for h in range(8)]
            for col in range(4):
                p_col = jnp.concatenate(_split_cols(probs[2 * col], vcur) + _split_cols(probs[2 * col + 1], vcur), axis=1)
                o_ref[rows, col * 128:(col + 1) * 128] = _dot(p_col, vs[col // 2]).astype(BF16)
            kp, vp = kc, vc

    return pl.pallas_call(
        body, name="attn_fwd", grid=(t // tq,),
        in_specs=[pl.BlockSpec(memory_space=pltpu.SMEM), _row_spec(tq, ATTN_W), _row_spec(tq, 2 * KV_W),
                  _kv_prev_spec(blocks)],
        out_specs=_row_spec(tq, ATTN_W),
        out_shape=jax.ShapeDtypeStruct((t, ATTN_W), BF16),
        compiler_params=_params("parallel"),
    )(sinks, q, kv, kv)


def _gm_forward_block(zu, zv, lng, lnb, w_ref, bsx, low):
    gu = _gelu(zu)
    gv = _gelu(zv)
    mu = jnp.mean(gv, axis=-1, keepdims=True)
    xc = gv - mu
    rstd = lax.rsqrt(jnp.mean(xc * xc, axis=-1, keepdims=True) + NORM_EPS)
    xn = xc * rstd
    ln = xn * lng + lnb
    mixed = []
    for col in range(4):
        lc = ln[:, col * 128:(col + 1) * 128]
        lo = jnp.where(low, lc, 0.0).astype(BF16)
        hi = jnp.where(low, 0.0, lc).astype(BF16)
        mixed.append(_dot(w_ref[2 * col], lo) + _dot(w_ref[2 * col + 1], hi) + bsx[:, col * 128:(col + 1) * 128])
    return gu, ln, xn, rstd, mixed


def _gm_fwd(zu, zv, lng, lnb, wtril, bsx):
    t = zu.shape[0]
    tm = _tile(t, 512)

    def body(zu_ref, zv_ref, g_ref, b_ref, w_ref, bs_ref, o_ref):
        low = lax.broadcasted_iota(jnp.int32, (BLK, BLK), 1) < 64
        for b in range(tm // BLK):
            rows = slice(b * BLK, (b + 1) * BLK)
            gu, _, _, _, mixed = _gm_forward_block(zu_ref[rows, :], zv_ref[rows, :], g_ref[...], b_ref[...], w_ref,
                                                   bs_ref[...], low)
            for col in range(4):
                o_ref[rows, col * 128:(col + 1) * 128] = (gu[:, col * 128:(col + 1) * 128] * mixed[col]).astype(BF16)

    return pl.pallas_call(
        body, name="gm_fwd", grid=(t // tm,),
        in_specs=[_row_spec(tm, GM_W), _row_spec(tm, GM_W), _vec_spec(GM_W), _vec_spec(GM_W),
                  pl.BlockSpec((8, BLK, BLK), lambda i: (0, 0, 0)), pl.BlockSpec((BLK, GM_W), lambda i: (0, 0))],
        out_specs=_row_spec(tm, GM_W),
        out_shape=jax.ShapeDtypeStruct((t, GM_W), BF16),
        compiler_params=_params("parallel"),
    )(zu, zv, lng, lnb, wtril, bsx)


def _out_proj(attn, gm, h, ga, gg, gpost, wb, layer):
    t = h.shape[0]
    tm = _tile(t, 512)

    def body(a_ref, m_ref, h_ref, ga_ref, gg_ref, gp_ref, w_ref, heads_ref, mix_ref, h1_ref):
        ha = _rms_fwd(a_ref[...].astype(F32), ga_ref[...]).astype(BF16)
        hg = _rms_fwd(m_ref[...].astype(F32), gg_ref[...]).astype(BF16)
        heads_ref[:, 0:512] = ha
        heads_ref[:, 512:1024] = hg
        mix = _dot(ha, w_ref[0:512, :]) + _dot(hg, w_ref[512:1024, :])
        mix_ref[...] = mix.astype(BF16)
        h1_ref[...] = h_ref[...] + _rms_fwd(mix, gp_ref[...])

    return pl.pallas_call(
        body, name="out_proj", grid=(t // tm,),
        in_specs=[_row_spec(tm, ATTN_W), _row_spec(tm, GM_W), _row_spec(tm, D_MODEL), _vec_spec(ATTN_W),
                  _vec_spec(GM_W), _vec_spec(D_MODEL), _seg_spec(ROWS_B, D_MODEL, 2 * layer)],
        out_specs=[_row_spec(tm, D_MODEL), _row_spec(tm, D_MODEL), _row_spec(tm, D_MODEL)],
        out_shape=[jax.ShapeDtypeStruct((t, D_MODEL), BF16), jax.ShapeDtypeStruct((t, D_MODEL), BF16),
                   jax.ShapeDtypeStruct((t, D_MODEL), F32)],
        compiler_params=_params("parallel"),
    )(attn, gm, h, ga, gg, gpost, wb)


def _ffn_fwd(h1, p, gpre, gpost, gple, wa, wb, wp, layer):
    t = h1.shape[0]
    tm = _tile(t, 256)

    def body(h_ref, p_ref, gpre_ref, gpost_ref, gple_ref, wg_ref, wu_ref, wd_ref, wpg_ref, wpl_ref,
             f_ref, gp_ref, up_ref, act_ref, fo_ref, h2_ref, hn_ref, gate_ref, h3_ref):
        h = h_ref[...]
        pe = _dot_nt(p_ref[...].astype(BF16), wpl_ref[...])
        f = _rms_fwd(h, gpre_ref[...]).astype(BF16)
        f_ref[...] = f
        chunks = [slice(j * FF_CHUNK, (j + 1) * FF_CHUNK) for j in range(D_FF // FF_CHUNK)]
        fo = None
        gp, up = _dot_nt(f, wg_ref[chunks[0], :]), _dot_nt(f, wu_ref[chunks[0], :])
        for j, cols in enumerate(chunks):
            if j + 1 < len(chunks):
                gp_next, up_next = _dot_nt(f, wg_ref[chunks[j + 1], :]), _dot_nt(f, wu_ref[chunks[j + 1], :])
            act = (gp * _sigmoid(gp) * up).astype(BF16)
            gp_ref[:, cols] = gp.astype(BF16)
            up_ref[:, cols] = up.astype(BF16)
            act_ref[:, cols] = act
            part = _dot(act, wd_ref[cols, :])
            fo = part if fo is None else fo + part
            if j + 1 < len(chunks):
                gp, up = gp_next, up_next
        fo_ref[...] = fo
        h2 = h + _rms_fwd(fo, gpost_ref[...])
        h2_ref[...] = h2
        hn = _rms_fwd(h2, gple_ref[...]).astype(BF16)
        hn_ref[...] = hn
        gate = _sigmoid(_dot(hn, wpg_ref[...]))
        gate_ref[...] = gate.astype(BF16)
        h3_ref[...] = h2 + pe * gate

    wide = _row_spec(tm, D_FF)
    row = _row_spec(tm, D_MODEL)
    vec = _vec_spec(D_MODEL)
    return pl.pallas_call(
        body, name="ffn_fwd", grid=(t // tm,),
        in_specs=[row, _row_spec(tm, PLE_DIM), vec, vec, vec, _seg_spec(ROWS_A, D_MODEL, 3 * layer),
                  _seg_spec(ROWS_A, D_MODEL, 3 * layer + 1), _seg_spec(ROWS_A, D_MODEL, 3 * layer + 2),
                  _seg_spec(ROWS_B, D_MODEL, 2 * layer + 1), _seg_spec(ROWS_B, PLE_DIM, layer)],
        out_specs=[row, wide, wide, wide, row, row, row, row, row],
        out_shape=[jax.ShapeDtypeStruct((t, D_MODEL), BF16)] + [jax.ShapeDtypeStruct((t, D_FF), BF16)] * 3
        + [jax.ShapeDtypeStruct((t, D_MODEL), F32)] * 2 + [jax.ShapeDtypeStruct((t, D_MODEL), BF16)] * 2
        + [jax.ShapeDtypeStruct((t, D_MODEL), F32)],
        compiler_params=_params("parallel"),
    )(h1, p, gpre, gpost, gple, wa, wa, wa, wb, wp)


def _loss_head(y, target):
    t = y.shape[0]
    tm = _tile(t, 512)

    def body(y_ref, t_ref, dy_ref, l_ref):
        _zero_at(pl.program_id(0) == 0, l_ref)
        e = y_ref[...] - t_ref[...]
        dy_ref[...] = e * (1.0 / D_MODEL)
        s = jnp.sum(jnp.sum(e * e, axis=1, keepdims=True), axis=0, keepdims=True)
        l_ref[...] += jnp.broadcast_to(s, (1, 128))

    return pl.pallas_call(
        body, name="loss_head", grid=(t // tm,),
        in_specs=[_row_spec(tm, D_MODEL), _row_spec(tm, D_MODEL)],
        out_specs=[_row_spec(tm, D_MODEL), _vec_spec(128)],
        out_shape=[jax.ShapeDtypeStruct((t, D_MODEL), F32), jax.ShapeDtypeStruct((1, 128), F32)],
        compiler_params=_params("arbitrary"),
    )(y, target)


def _ffn_bwd(dh3, h2, gate, p, fo, gp, up, h1, gple, gpost, gpre, wa, wb, wp, layer, after):
    t = dh3.shape[0]
    tm = _tile(t, 256)

    def body(d3_ref, h2_ref, gate_ref, p_ref, fo_ref, gp_ref, up_ref, h_ref, gple_ref, gpost_ref, gpre_ref,
             wg_ref, wu_ref, wd_ref, wpg_ref, wpl_ref, after_ref,
             dgl_ref, dpe_ref, dfo_ref, dgp_ref, dup_ref, dh1_ref, dgple_ref, dgpost_ref, dgpre_ref):
        _zero_at(pl.program_id(0) == 0, dgple_ref, dgpost_ref, dgpre_ref)
        d3 = d3_ref[...]
        gate = gate_ref[...].astype(F32)
        pe = _dot_nt(p_ref[...].astype(BF16), wpl_ref[...])
        dpe_ref[...] = (d3 * gate).astype(BF16)
        dgl = (d3 * pe * gate * (1.0 - gate)).astype(BF16)
        dgl_ref[...] = dgl
        dx2, dgple = _rms_bwd(h2_ref[...], gple_ref[...], _dot_nt(dgl, wpg_ref[...]))
        dgple_ref[...] += dgple
        d = d3 + dx2
        dfo, dgpost = _rms_bwd(fo_ref[...], gpost_ref[...], d)
        dfo = dfo.astype(BF16)
        dfo_ref[...] = dfo
        dgpost_ref[...] += dgpost
        chunks = [slice(j * FF_CHUNK, (j + 1) * FF_CHUNK) for j in range(D_FF // FF_CHUNK)]
        df = None
        dact = _dot_nt(dfo, wd_ref[chunks[0], :])
        for j, cols in enumerate(chunks):
            if j + 1 < len(chunks):
                dact_next = _dot_nt(dfo, wd_ref[chunks[j + 1], :])
            gp = gp_ref[:, cols].astype(F32)
            sg = _sigmoid(gp)
            dgp = (dact * up_ref[:, cols].astype(F32) * (sg * (1.0 + gp * (1.0 - sg)))).astype(BF16)
            dup = (dact * (gp * sg)).astype(BF16)
            dgp_ref[:, cols] = dgp
            dup_ref[:, cols] = dup
            part = _dot(dgp, wg_ref[cols, :]) + _dot(dup, wu_ref[cols, :])
            df = part if df is None else df + part
            if j + 1 < len(chunks):
                dact = dact_next
        dx, dgpre = _rms_bwd(h_ref[...], gpre_ref[...], df)
        dh1_ref[...] = d + dx
        dgpre_ref[...] += dgpre

    wide = _row_spec(tm, D_FF)
    row = _row_spec(tm, D_MODEL)
    vec = _vec_spec(D_MODEL)
    narrow = jax.ShapeDtypeStruct((t, D_MODEL), BF16)
    return pl.pallas_call(
        body, name="ffn_bwd", grid=(t // tm,),
        in_specs=[row, row, row, _row_spec(tm, PLE_DIM), row, wide, wide, row, vec, vec, vec,
                  _seg_spec(ROWS_A, D_MODEL, 3 * layer), _seg_spec(ROWS_A, D_MODEL, 3 * layer + 1),
                  _seg_spec(ROWS_A, D_MODEL, 3 * layer + 2), _seg_spec(ROWS_B, D_MODEL, 2 * layer + 1),
                  _seg_spec(ROWS_B, PLE_DIM, layer), pl.BlockSpec(memory_space=pl.ANY)],
        out_specs=[row, row, row, wide, wide, row, vec, vec, vec],
        out_shape=[narrow, narrow, narrow, jax.ShapeDtypeStruct((t, D_FF), BF16), jax.ShapeDtypeStruct((t, D_FF), BF16),
                   jax.ShapeDtypeStruct((t, D_MODEL), F32)] + [jax.ShapeDtypeStruct((1, D_MODEL), F32)] * 3,
        compiler_params=_params("arbitrary"),
    )(dh3, h2, gate, p, fo, gp, up, h1, gple, gpost, gpre, wa, wa, wa, wb, wp, after)


def _out_proj_bwd(dh1, mix, attn, gm, gpost, ga, gg, wb, layer, after):
    t = dh1.shape[0]
    tm = _tile(t, 512)

    def body(d_ref, mix_ref, a_ref, m_ref, gp_ref, ga_ref, gg_ref, w_ref, after_ref,
             dmix_ref, da_ref, dm_ref, dgp_ref, dga_ref, dgg_ref):
        _zero_at(pl.program_id(0) == 0, dgp_ref, dga_ref, dgg_ref)
        dmix, dgp = _rms_bwd(mix_ref[...].astype(F32), gp_ref[...], d_ref[...])
        dmix = dmix.astype(BF16)
        dmix_ref[...] = dmix
        da, dga = _rms_bwd(a_ref[...].astype(F32), ga_ref[...], _dot_nt(dmix, w_ref[0:512, :]))
        dm, dgg = _rms_bwd(m_ref[...].astype(F32), gg_ref[...], _dot_nt(dmix, w_ref[512:1024, :]))
        da_ref[...] = da.astype(BF16)
        dm_ref[...] = dm
        dgp_ref[...] += dgp
        dga_ref[...] += dga
        dgg_ref[...] += dgg

    return pl.pallas_call(
        body, name="out_proj_bwd", grid=(t // tm,),
        in_specs=[_row_spec(tm, D_MODEL), _row_spec(tm, D_MODEL), _row_spec(tm, ATTN_W), _row_spec(tm, GM_W),
                  _vec_spec(D_MODEL), _vec_spec(ATTN_W), _vec_spec(GM_W), _seg_spec(ROWS_B, D_MODEL, 2 * layer),
                  pl.BlockSpec(memory_space=pl.ANY)],
        out_specs=[_row_spec(tm, D_MODEL), _row_spec(tm, ATTN_W), _row_spec(tm, GM_W),
                   _vec_spec(D_MODEL), _vec_spec(ATTN_W), _vec_spec(GM_W)],
        out_shape=[jax.ShapeDtypeStruct((t, D_MODEL), BF16), jax.ShapeDtypeStruct((t, ATTN_W), BF16),
                   jax.ShapeDtypeStruct((t, GM_W), F32), jax.ShapeDtypeStruct((1, D_MODEL), F32),
                   jax.ShapeDtypeStruct((1, ATTN_W), F32), jax.ShapeDtypeStruct((1, GM_W), F32)],
        compiler_params=_params("arbitrary"),
    )(dh1, mix, attn, gm, gpost, ga, gg, wb, after)


def _split3(x):
    hi = x.astype(BF16)
    r1 = x - hi.astype(F32)
    mid = r1.astype(BF16)
    lo = (r1 - mid.astype(F32)).astype(BF16)
    return hi, mid, lo


def _gm_bwd(dgm, zu, zv, lng, lnb, wtril, bsx):
    t = zu.shape[0]
    tm = _tile(t, 512)
    nb = t // tm

    def body(d_ref, zu_ref, zv_ref, g_ref, b_ref, w_ref, bs_ref,
             dzu_ref, dzv_ref, dw_ref, dbs_ref, dlg_ref, dlb_ref, dbsx_ref):
        i = pl.program_id(0)
        _zero_at(i == 0, dw_ref, dlg_ref, dlb_ref, dbsx_ref)
        row = lax.broadcasted_iota(jnp.int32, (BLK, BLK), 0)
        lane = lax.broadcasted_iota(jnp.int32, (BLK, BLK), 1)
        low = lane < 64
        tril = row >= lane
        lng = g_ref[...]
        for b in range(tm // BLK):
            rows = slice(b * BLK, (b + 1) * BLK)
            zu = zu_ref[rows, :]
            zv = zv_ref[rows, :]
            gu, ln, xn, rstd, mixed = _gm_forward_block(zu, zv, lng, b_ref[...], w_ref, bs_ref[...], low)
            dgm = d_ref[rows, :]
            dgu_cols, dmx_cols, dln_cols = [], [], []
            for col in range(4):
                sl = slice(col * 128, (col + 1) * 128)
                dg = dgm[:, sl]
                dgu_cols.append(dg * mixed[col])
                dmx = dg * gu[:, sl]
                dmx_cols.append(dmx)
                lc = ln[:, sl]
                halves = (jnp.where(low, lc, 0.0).astype(BF16), jnp.where(low, 0.0, lc).astype(BF16))
                dmx16 = dmx.astype(BF16)
                dmx_half = (jnp.where(low, dmx, 0.0).astype(BF16), jnp.where(low, 0.0, dmx).astype(BF16))
                dln = None
                for half in range(2):
                    hd = 2 * col + half
                    dw_ref[hd] += jnp.where(tril, _dot_nt(dmx16, halves[half]), 0.0)
                    part = _dot_tn(w_ref[hd], dmx_half[half])
                    dln = part if dln is None else dln + part
                dln_cols.append(dln)
            dgu = jnp.concatenate(dgu_cols, axis=1)
            dmx = jnp.concatenate(dmx_cols, axis=1)
            dln = jnp.concatenate(dln_cols, axis=1)
            dzu_ref[rows, :] = (dgu * _gelu_grad(zu)).astype(BF16)
            dbsx_ref[...] += dmx
            dlg_ref[...] += jnp.sum(dln * xn, axis=0, keepdims=True)
            dlb_ref[...] += jnp.sum(dln, axis=0, keepdims=True)
            dxn = dln * lng
            dgv = rstd * (dxn - jnp.mean(dxn, axis=-1, keepdims=True) - xn * jnp.mean(dxn * xn, axis=-1, keepdims=True))
            dzv_ref[rows, :] = (dgv * _gelu_grad(zv)).astype(BF16)

        @pl.when(i == nb - 1)
        def _():
            r = lax.broadcasted_iota(jnp.int32, (GM_W, BLK), 0)
            c = lax.broadcasted_iota(jnp.int32, (GM_W, BLK), 1)
            e = jnp.where(jnp.logical_and(r >= c * 64, r < c * 64 + 64), 1.0, 0.0).astype(BF16)
            hi, mid, lo = _split3(dbsx_ref[...])
            dbs_ref[...] = _dot(hi, e) + _dot(mid, e) + _dot(lo, e)

    vec = _vec_spec(GM_W)
    return pl.pallas_call(
        body, name="gm_bwd", grid=(nb,),
        in_specs=[_row_spec(tm, GM_W)] * 3 + [vec, vec, pl.BlockSpec((8, BLK, BLK), lambda i: (0, 0, 0)),
                                              pl.BlockSpec((BLK, GM_W), lambda i: (0, 0))],
        out_specs=[_row_spec(tm, GM_W), _row_spec(tm, GM_W), pl.BlockSpec((8, BLK, BLK), lambda i: (0, 0, 0)),
                   pl.BlockSpec((BLK, BLK), lambda i: (0, 0)), vec, vec],
        out_shape=[jax.ShapeDtypeStruct((t, GM_W), BF16), jax.ShapeDtypeStruct((t, GM_W), BF16),
                   jax.ShapeDtypeStruct((8, BLK, BLK), F32), jax.ShapeDtypeStruct((BLK, BLK), F32),
                   jax.ShapeDtypeStruct((1, GM_W), F32), jax.ShapeDtypeStruct((1, GM_W), F32)],
        scratch_shapes=[pltpu.VMEM((BLK, GM_W), F32)],
        compiler_params=_params("arbitrary"),
    )(dgm, zu, zv, lng, lnb, wtril, bsx)


def _attn_bwd(q, kv, do, sinks, after):
    t = q.shape[0]
    tq = _tile(t, 512)
    blocks = tq // BLK

    def body(sink_ref, q_ref, kvc_ref, kvp_ref, do_ref, after_ref, dq_ref, dkv_ref, dkf_ref, ds_ref):
        i = pl.program_id(0)
        _zero_at(i == 0, ds_ref)
        low, vcur, dist, valid = _attn_masks(i)
        head_row = lax.broadcasted_iota(jnp.int32, (8, 128), 0)
        dsink_tile = jnp.zeros((8, 128), F32)
        kp, vp = _kv_variants(kvp_ref, slice(None), low)
        own = None
        for b in range(blocks):
            rows = slice(b * BLK, (b + 1) * BLK)
            kc, vc = _kv_variants(kvc_ref, rows, low)
            ks = [_stack_kv(kp, kc, g) for g in range(2)]
            vs = [_stack_kv(vp, vc, g) for g in range(2)]
            scores = _attn_scores(q_ref, rows, ks, vcur)
            dprobs = _attn_scores(do_ref, rows, vs, vcur)
            ds_cols, p_cols = [], []
            for col in range(4):
                ds_parts, p_parts = [], []
                for h in (2 * col, 2 * col + 1):
                    p, ps = _attn_probs(scores[h], h, sink_ref[h], dist, valid if b == 0 else None)
                    delta = jnp.sum(p * dprobs[h], axis=1, keepdims=True)
                    ds = p * (dprobs[h] - delta) * ATTN_SCALE
                    dsink = jnp.sum(-ps * delta, axis=0, keepdims=True)
                    dsink_tile = jnp.where(head_row == h, dsink_tile + dsink, dsink_tile)
                    ds_parts += _split_cols(ds, vcur)
                    p_parts += _split_cols(p, vcur)
                ds_cols.append(jnp.concatenate(ds_parts, axis=1))
                p_cols.append(jnp.concatenate(p_parts, axis=1))
            for col in range(4):
                dq_ref[rows, col * 128:(col + 1) * 128] = _dot(ds_cols[col], ks[col // 2]).astype(BF16)
            acc = {}
            for g in range(2):
                cols = (2 * g, 2 * g + 1)
                q2 = jnp.concatenate([q_ref[rows, c * 128:(c + 1) * 128] for c in cols], axis=0)
                do2 = jnp.concatenate([do_ref[rows, c * 128:(c + 1) * 128] for c in cols], axis=0)
                dk = _dot_tn(jnp.concatenate([ds_cols[c] for c in cols], axis=0), q2)
                dv = _dot_tn(jnp.concatenate([p_cols[c] for c in cols], axis=0), do2)
                for half in range(2):
                    acc[("kp", (g, half))] = dk[half * 256:half * 256 + 128]
                    acc[("kc", (g, half))] = dk[half * 256 + 128:half * 256 + 256]
                    acc[("vp", (g, half))] = dv[half * 256:half * 256 + 128]
                    acc[("vc", (g, half))] = dv[half * 256 + 128:half * 256 + 256]

            def place(name):
                head0 = acc[(name, (0, 0))] + pltpu.roll(acc[(name, (0, 1))], 64, axis=1)
                head1 = pltpu.roll(acc[(name, (1, 0))], 64, axis=1) + acc[(name, (1, 1))]
                return jnp.where(low, head0, head1)

            before = (place("kp"), place("vp"))
            if b == 0:
                dkf_ref[:, 0:128], dkf_ref[:, 128:256] = before
            else:
                last = slice((b - 1) * BLK, b * BLK)
                dkv_ref[last, 0:128] = own[0] + before[0]
                dkv_ref[last, 128:256] = own[1] + before[1]
            own = (place("kc"), place("vc"))
            kp, vp = kc, vc
        final = slice((blocks - 1) * BLK, blocks * BLK)
        dkv_ref[final, 0:128], dkv_ref[final, 128:256] = own
        ds_ref[...] += dsink_tile

    row_q = _row_spec(tq, ATTN_W)
    row_kv = _row_spec(tq, 2 * KV_W)
    return pl.pallas_call(
        body, name="attn_bwd", grid=(t // tq,),
        in_specs=[pl.BlockSpec(memory_space=pltpu.SMEM), row_q, row_kv, _kv_prev_spec(blocks), row_q,
                  pl.BlockSpec(memory_space=pl.ANY)],
        out_specs=[row_q, row_kv, _row_spec(BLK, 2 * KV_W), pl.BlockSpec((8, 128), lambda i: (0, 0))],
        out_shape=[jax.ShapeDtypeStruct((t, ATTN_W), BF16), jax.ShapeDtypeStruct((t, 2 * KV_W), F32),
                   jax.ShapeDtypeStruct((t // tq * BLK, 2 * KV_W), F32), jax.ShapeDtypeStruct((8, 128), F32)],
        compiler_params=_params("arbitrary"),
    )(sinks, q, kv, kv, do, after)


def _in_proj_bwd(dq, dkv, dkf, dzu, dzv, h, dres, g, wc, layer):
    t = h.shape[0]
    tm = _tile(t, 512)
    steps = t // tm

    def body(dq_ref, dkv_ref, dkn_ref, dzu_ref, dzv_ref, h_ref, d_ref, g_ref, w_ref, dz_ref, dh_ref, dg_ref):
        i = pl.program_id(0)
        _zero_at(i == 0, dg_ref)
        dq = dq_ref[...]
        tail = dkv_ref[tm - BLK:tm, :] + jnp.where(i < steps - 1, dkn_ref[...], 0.0)
        dkv = tail if tm == BLK else jnp.concatenate([dkv_ref[0:tm - BLK, :], tail], axis=0)
        dkv = dkv.astype(BF16)
        dzu = dzu_ref[...]
        dzv = dzv_ref[...]
        dz_ref[:, 0:512] = dq
        dz_ref[:, 512:768] = dkv
        dz_ref[:, 768:1280] = dzu
        dz_ref[:, 1280:1792] = dzv
        da = (_dot(dq, w_ref[0:512, :]) + _dot(dkv, w_ref[512:768, :]) + _dot(dzu, w_ref[768:1280, :])
              + _dot(dzv, w_ref[1280:1792, :]))
        dx, dg = _rms_bwd(h_ref[...], g_ref[...], da)
        dh_ref[...] = d_ref[...] + dx
        dg_ref[...] += dg

    return pl.pallas_call(
        body, name="in_proj_bwd", grid=(t // tm,),
        in_specs=[_row_spec(tm, ATTN_W), _row_spec(tm, 2 * KV_W),
                  pl.BlockSpec((BLK, 2 * KV_W), lambda i: (jnp.minimum(i + 1, steps - 1), 0)), _row_spec(tm, GM_W),
                  _row_spec(tm, GM_W), _row_spec(tm, D_MODEL), _row_spec(tm, D_MODEL), _vec_spec(D_MODEL),
                  _seg_spec(ROWS_C, D_MODEL, layer)],
        out_specs=[_row_spec(tm, D_IN), _row_spec(tm, D_MODEL), _vec_spec(D_MODEL)],
        out_shape=[jax.ShapeDtypeStruct((t, D_IN), BF16), jax.ShapeDtypeStruct((t, D_MODEL), F32),
                   jax.ShapeDtypeStruct((1, D_MODEL), F32)],
        compiler_params=_params("arbitrary"),
    )(dq, dkv, dkf, dzu, dzv, h, dres, g, wc)


def _weight_grad(a, b, buf, seg):
    t, m = a.shape
    n = b.shape[1]
    assert buf.shape[0] % m == 0 and buf.shape[1] == n
    tm = _tile(t, WGRAD_TOKENS)
    steps = t // tm
    half = m // 2

    def body(a_ref, b_ref, buf_ref, o_ref, acc_ref):
        i = pl.program_id(0)
        _zero_at(i == 0, acc_ref)
        b16 = b_ref[...].astype(BF16)
        for rows in (slice(0, half), slice(half, m)):
            acc_ref[rows, :] += _dot_tn(a_ref[:, rows], b16)

        @pl.when(i == steps - 1)
        def _():
            o_ref[...] = acc_ref[...].astype(o_ref.dtype)

    return pl.pallas_call(
        body, name="weight_grad", grid=(steps,),
        in_specs=[_row_spec(tm, m), _row_spec(tm, n), pl.BlockSpec(memory_space=pl.ANY)],
        out_specs=pl.BlockSpec((m, n), lambda i: (seg, 0)),
        out_shape=jax.ShapeDtypeStruct(buf.shape, buf.dtype),
        scratch_shapes=[pltpu.VMEM((m, n), F32)],
        input_output_aliases={2: 0},
        compiler_params=_params("arbitrary"),
    )(a, b, buf)


def _rows8(rows):
    return [jnp.pad(r, ((0, 7), (0, 0))) for r in rows]


def _small_pack_gating(d):
    rows = [jnp.concatenate([d["gm_ln_g"], d["gm_ln_b"]], axis=1), d["gm_bs"].reshape(1, 1024)]
    return jnp.concatenate(_rows8(rows) + [d["gm_ws"].reshape(128, 1024)], axis=0)


def _small_pack_rest(d):
    rows = [d["ln_mix_pre"], d["ln_mix_post"], d["ln_ffn_pre"], d["ln_ffn_post"], d["ln_ple_gate"],
            jnp.concatenate([d["g_attn_out"], d["g_gm_out"]], axis=1),
            jnp.pad(d["attn_sinks"].reshape(1, 8), ((0, 0), (0, 1016)))]
    return jnp.concatenate(_rows8(rows), axis=0)


def _small_unpack(g, s):
    return {
        "gm_ln_g": g[:, 0, :512], "gm_ln_b": g[:, 0, 512:], "gm_bs": g[:, 8].reshape(DEPTH, 8, 128),
        "gm_ws": g[:, 16:144].reshape(DEPTH, 8, 128, 128),
        "ln_mix_pre": s[:, 0], "ln_mix_post": s[:, 8], "ln_ffn_pre": s[:, 16], "ln_ffn_post": s[:, 24],
        "ln_ple_gate": s[:, 32], "g_attn_out": s[:, 40, :512], "g_gm_out": s[:, 40, 512:], "attn_sinks": s[:, 48, :8],
    }


def _row(v):
    return v.reshape(1, -1)


def _layer_fwd(h, p, sp, l, weights):
    tril = jnp.tril(jnp.ones((BLK, BLK), bool))
    wtril = jnp.where(tril[None], sp["gm_ws"][l], 0.0).astype(BF16)
    bsx = jnp.repeat(sp["gm_bs"][l].T, HEAD_DIM, axis=1)
    a, q, kv, zu, zv = _in_proj(h, _row(sp["ln_mix_pre"][l]), weights("c", h), 0)
    attn = _attn_fwd(q, kv, sp["attn_sinks"][l])
    gm = _gm_fwd(zu, zv, _row(sp["gm_ln_g"][l]), _row(sp["gm_ln_b"][l]), wtril, bsx)
    wb = weights("b", gm)
    heads, mix, h1 = _out_proj(attn, gm, h, _row(sp["g_attn_out"][l]), _row(sp["g_gm_out"][l]),
                               _row(sp["ln_mix_post"][l]), wb, 0)
    wa = weights("a", h1)
    f, gpre, up, act, fo, h2, hn, gate, h3 = _ffn_fwd(
        h1, p, _row(sp["ln_ffn_pre"][l]), _row(sp["ln_ffn_post"][l]), _row(sp["ln_ple_gate"][l]), wa, wb,
        weights("p", gm), 0)
    saved = dict(h=h, a=a, q=q, kv=kv, zu=zu, zv=zv, attn=attn, gm=gm, heads=heads, mix=mix, h1=h1, f=f,
                 gpre=gpre, up=up, act=act, fo=fo, h2=h2, hn=hn, gate=gate, wtril=wtril, bsx=bsx)
    return h3, saved


def _layer_bwd_upper(dh, s, p, sp, l, wa, wb, wp, after):
    d = {}
    dgl, dpe, dfo, dgp, dup, dh1, d["ln_ple_gate"], d["ln_ffn_post"], d["ln_ffn_pre"] = _ffn_bwd(
        dh, s["h2"], s["gate"], p, s["fo"], s["gpre"], s["up"], s["h1"], _row(sp["ln_ple_gate"][l]),
        _row(sp["ln_ffn_post"][l]), _row(sp["ln_ffn_pre"][l]), wa, wb, wp, 0, after)
    gb = _weight_grad(s["hn"], dgl, lax.empty((2 * D_MODEL, D_MODEL), BF16), 1)
    gp = _weight_grad(dpe, p, lax.empty((D_MODEL, PLE_DIM), BF16), 0)
    ga = _weight_grad(s["act"], dfo, lax.empty((3 * D_FF, D_MODEL), BF16), 2)
    ga = _weight_grad(dgp, s["f"], ga, 0)
    ga = _weight_grad(dup, s["f"], ga, 1)
    return (dh1, d), ga, gp, gb


def _layer_bwd_middle(carry, s, sp, l, wb, gb, after):
    dh1, d = carry
    dmix, dattn, dgm, d["ln_mix_post"], d["g_attn_out"], d["g_gm_out"] = _out_proj_bwd(
        dh1, s["mix"], s["attn"], s["gm"], _row(sp["ln_mix_post"][l]), _row(sp["g_attn_out"][l]),
        _row(sp["g_gm_out"][l]), wb, 0, after)
    gb = _weight_grad(s["heads"], dmix, gb, 0)
    dzu, dzv, d["gm_ws"], dbs, d["gm_ln_g"], d["gm_ln_b"] = _gm_bwd(
        dgm, s["zu"], s["zv"], _row(sp["gm_ln_g"][l]), _row(sp["gm_ln_b"][l]), s["wtril"], s["bsx"])
    d["gm_bs"] = dbs[:, :8].T
    return (dh1, dattn, dzu, dzv, d), gb, _small_pack_gating(d)


def _layer_bwd_lower(carry, s, sp, l, wc, after):
    dh1, dattn, dzu, dzv, d = carry
    dq, dkv, dkf, dsink = _attn_bwd(s["q"], s["kv"], dattn, sp["attn_sinks"][l], after)
    d["attn_sinks"] = dsink[:, 0]
    dz, dh, d["ln_mix_pre"] = _in_proj_bwd(dq, dkv, dkf, dzu, dzv, s["h"], dh1, _row(sp["ln_mix_pre"][l]), wc, 0)
    gc = _weight_grad(dz, s["a"], lax.empty((D_IN, D_MODEL), BF16), 0)
    return dh, gc, _small_pack_rest(d)


ANY = pl.BlockSpec(memory_space=pl.ANY)


def _place():
    x, y, c = lax.axis_index("x"), lax.axis_index("y"), lax.axis_index("c")
    chips = [(1 - x, y), (x, 1 - y), (1 - x, 1 - y)]
    return x, y, c, chips


def _all_gather(shards):
    n = len(shards)

    def body(*refs):
        ins, outs = refs[:n], refs[n:2 * n]
        send_sems, recv_sems, local_sems = refs[2 * n:]
        x, y, c, chips = _place()
        me, sibling = (x, y, c), (x, y, 1 - c)

        def block(k, px, py, pc):
            return outs[k].at[:, pl.ds(4 * px + 2 * py + pc, 1)]

        def copy(k, j, who, to, src=None):
            return pltpu.make_async_remote_copy(
                src_ref=block(k, *who) if src is None else src, dst_ref=block(k, *who),
                send_sem=send_sems.at[7 * k + j], recv_sem=recv_sems.at[7 * k + j],
                device_id=to, device_id_type=MESH)

        mine = [pltpu.make_async_copy(ins[k], block(k, *me), local_sems.at[k]) for k in range(n)]
        for cp in mine:
            cp.start()
        first = []
        for k in range(n):
            first.append(copy(k, 0, me, sibling, src=ins[k]))
            first += [copy(k, 1 + j, me, (*chip, c), src=ins[k]) for j, chip in enumerate(chips)]
        for cp in first:
            cp.start()
        passed = []
        for j, chip in enumerate(chips):
            for k in range(n):
                copy(k, 1 + j, (*chip, c), me).wait_recv()
                cp = copy(k, 4 + j, (*chip, c), sibling)
                cp.start()
                passed.append(cp)
        for k in range(n):
            copy(k, 0, sibling, me).wait_recv()
            for j, chip in enumerate(chips):
                copy(k, 4 + j, (*chip, 1 - c), me).wait_recv()
        for cp in first + passed:
            cp.wait_send()
        for cp in mine:
            cp.wait()

    return pl.pallas_call(
        body, name="all_gather_weights",
        in_specs=[ANY] * n, out_specs=[ANY] * n,
        out_shape=[jax.ShapeDtypeStruct((s.shape[0], N_DEV) + s.shape[2:], s.dtype) for s in shards],
        scratch_shapes=[pltpu.SemaphoreType.DMA((7 * n,)), pltpu.SemaphoreType.DMA((7 * n,)),
                        pltpu.SemaphoreType.DMA((n,))],
        compiler_params=pltpu.CompilerParams(has_side_effects=True),
    )(*shards)


def _sibling_exchange(bufs, small):
    n = len(bufs)

    def body(*refs):
        ins, small_ref = refs[:n], refs[n]
        outs, both_ref = refs[n + 1:2 * n + 1], refs[2 * n + 1]
        send_sems, recv_sems, local_sem = refs[2 * n + 2:]
        x, y, c, _ = _place()
        sibling = (x, y, 1 - c)
        mine = pltpu.make_async_copy(small_ref, both_ref.at[c], local_sem)
        mine.start()
        copies = [pltpu.make_async_remote_copy(
            src_ref=ins[k].at[:, :, pl.ds(1 - c, 1)], dst_ref=outs[k], send_sem=send_sems.at[k],
            recv_sem=recv_sems.at[k], device_id=sibling, device_id_type=MESH) for k in range(n)]
        copies.append(pltpu.make_async_remote_copy(
            src_ref=small_ref, dst_ref=both_ref.at[c], send_sem=send_sems.at[n], recv_sem=recv_sems.at[n],
            device_id=sibling, device_id_type=MESH))
        for cp in copies:
            cp.start()
        for k in range(n):
            copies[k].wait_recv()
        pltpu.make_async_remote_copy(
            src_ref=small_ref, dst_ref=both_ref.at[1 - c], send_sem=send_sems.at[n], recv_sem=recv_sems.at[n],
            device_id=sibling, device_id_type=MESH).wait_recv()
        for cp in copies:
            cp.wait_send()
        mine.wait()

    return pl.pallas_call(
        body, name="sibling_exchange",
        in_specs=[ANY] * (n + 1), out_specs=[ANY] * (n + 1),
        out_shape=[jax.ShapeDtypeStruct(b.shape[:2] + (1,) + b.shape[3:], b.dtype) for b in bufs]
        + [jax.ShapeDtypeStruct((2,) + small.shape, small.dtype)],
        scratch_shapes=[pltpu.SemaphoreType.DMA((n + 1,)), pltpu.SemaphoreType.DMA((n + 1,)), pltpu.SemaphoreType.DMA],
        compiler_params=pltpu.CompilerParams(has_side_effects=True),
    )(*bufs, small)


def _chip_exchange(sends, small):
    n = len(sends)

    def body(*refs):
        ins, small_ref = refs[:n], refs[n]
        outs, all_ref = refs[n + 1:2 * n + 1], refs[2 * n + 1]
        send_sems, recv_sems, local_sem = refs[2 * n + 2:]
        x, y, c, chips = _place()
        mine = pltpu.make_async_copy(small_ref, all_ref.at[2 * x + y], local_sem)
        mine.start()
        copies = []
        for j, chip in enumerate(chips):
            for k in range(n):
                copies.append(pltpu.make_async_remote_copy(
                    src_ref=ins[k].at[j], dst_ref=outs[k].at[j], send_sem=send_sems.at[3 * k + j],
                    recv_sem=recv_sems.at[3 * k + j], device_id=(*chip, c), device_id_type=MESH))
            copies.append(pltpu.make_async_remote_copy(
                src_ref=small_ref, dst_ref=all_ref.at[2 * x + y], send_sem=send_sems.at[3 * n + j],
                recv_sem=recv_sems.at[3 * n + j], device_id=(*chip, c), device_id_type=MESH))
        for cp in copies:
            cp.start()
        for j, (px, py) in enumerate(chips):
            for k in range(n):
                copies[j * (n + 1) + k].wait_recv()
            pltpu.make_async_remote_copy(
                src_ref=small_ref, dst_ref=all_ref.at[2 * px + py], send_sem=send_sems.at[3 * n + j],
                recv_sem=recv_sems.at[3 * n + j], device_id=(px, py, c), device_id_type=MESH).wait_recv()
        for cp in copies:
            cp.wait_send()
        mine.wait()

    return pl.pallas_call(
        body, name="chip_exchange",
        in_specs=[ANY] * (n + 1), out_specs=[ANY] * (n + 1),
        out_shape=[jax.ShapeDtypeStruct(s.shape, s.dtype) for s in sends]
        + [jax.ShapeDtypeStruct((4,) + small.shape, small.dtype)],
        scratch_shapes=[pltpu.SemaphoreType.DMA((3 * n + 3,)), pltpu.SemaphoreType.DMA((3 * n + 3,)),
                        pltpu.SemaphoreType.DMA],
        compiler_params=pltpu.CompilerParams(has_side_effects=True),
    )(*sends, small)


def _pair_add(buf, got, chip_ids, dtype):
    nseg, _, _, rows, cols = buf.shape
    nr = chip_ids.shape[0] - 1

    def body(ids_ref, a_ref, b_ref, o_ref):
        o_ref[...] = (a_ref[...] + b_ref[...]).astype(dtype)

    return pl.pallas_call(
        body, name="pair_add",
        grid_spec=pltpu.PrefetchScalarGridSpec(
            num_scalar_prefetch=1, grid=(nr, nseg),
            in_specs=[pl.BlockSpec((None, None, None, rows, cols), lambda r, s, ids: (s, ids[r], ids[nr], 0, 0)),
                      pl.BlockSpec((None, None, None, rows, cols), lambda r, s, ids: (s, ids[r], 0, 0, 0))],
            out_specs=pl.BlockSpec((None, None, rows, cols), lambda r, s, ids: (r, s, 0, 0))),
        out_shape=jax.ShapeDtypeStruct((nr, nseg, rows, cols), dtype),
        compiler_params=_params("parallel", "parallel"),
    )(chip_ids, buf, got)


def _sum_slots(z, tr):
    n, rows, cols = z.shape

    def body(z_ref, o_ref):
        s = z_ref[0]
        for k in range(1, n):
            s = s + z_ref[k]
        o_ref[...] = s

    return pl.pallas_call(
        body, name="sum_slots", grid=(rows // tr,),
        in_specs=[pl.BlockSpec((n, tr, cols), lambda i: (0, i, 0))],
        out_specs=pl.BlockSpec((tr, cols), lambda i: (i, 0)),
        out_shape=jax.ShapeDtypeStruct((rows, cols), F32),
        compiler_params=_params("parallel"),
    )(z)


def _final_sum(own, got):
    _, nseg, rows, cols = own.shape

    def body(a_ref, b_ref, o_ref):
        s = a_ref[0]
        for k in range(3):
            s = s + b_ref[k].astype(F32)
        o_ref[...] = s

    return pl.pallas_call(
        body, name="final_sum", grid=(nseg,),
        in_specs=[pl.BlockSpec((1, None, rows, cols), lambda i: (0, i, 0, 0)),
                  pl.BlockSpec((3, None, rows, cols), lambda i: (0, i, 0, 0))],
        out_specs=pl.BlockSpec((None, rows, cols), lambda i: (i, 0, 0)),
        out_shape=jax.ShapeDtypeStruct((nseg, rows, cols), F32),
        compiler_params=_params("parallel"),
    )(own, got)


HBM = pl.BlockSpec(memory_space=pltpu.HBM)
SEM = pl.BlockSpec(memory_space=pltpu.SEMAPHORE)
N_PEERS = N_DEV - 1


def _peers():
    x, y, c = lax.axis_index("x"), lax.axis_index("y"), lax.axis_index("c")
    peers = []
    for r in range(1, N_DEV):
        px = 1 - x if r & 4 else x
        py = 1 - y if r & 2 else y
        pc = 1 - c if r & 1 else c
        peers.append(((px, py, pc), 4 * px + 2 * py + pc))
    return 4 * x + 2 * y + c, peers


GATHER, SCATTER, SPREAD = "gather", "scatter", "spread"


def _peer_copy(src, land, send_sems, recv_sems, r, me, peer, peer_slot, mode):
    return pltpu.make_async_remote_copy(
        src_ref=src.at[:, pl.ds(peer_slot, 1)] if mode == SCATTER else src,
        dst_ref=land.at[:, pl.ds(me, 1)] if mode == GATHER else land.at[:, pl.ds(r - 1, 1)],
        send_sem=send_sems.at[r - 1], recv_sem=recv_sems.at[r - 1], device_id=peer, device_id_type=MESH)


def _peer_arrival(src, land, send_sems, recv_sems, r, me, peer, peer_slot, mode):
    return pltpu.make_async_remote_copy(
        src_ref=src.at[:, pl.ds(me, 1)] if mode == SCATTER else src,
        dst_ref=land.at[:, pl.ds(peer_slot, 1)] if mode == GATHER else land.at[:, pl.ds(r - 1, 1)],
        send_sem=send_sems.at[r - 1], recv_sem=recv_sems.at[r - 1], device_id=peer, device_id_type=MESH)


def _send_start(name, srcs, lands, modes):
    n = len(srcs)

    def body(*refs):
        src_refs, land_refs = refs[:n], refs[n:2 * n]
        outs = refs[2 * n:]
        send_sems, recv_sems, token = outs[2 * n:3 * n], outs[3 * n:4 * n], outs[4 * n]
        me, peers = _peers()
        for k in range(n):
            for r, (peer, slot) in enumerate(peers, 1):
                _peer_copy(src_refs[k], land_refs[k], send_sems[k], recv_sems[k], r, me, peer, slot, modes[k]).start()
        token[...] = jnp.zeros_like(token)

    hbm = lambda a: pltpu.HBM(a.shape, a.dtype)
    sems = [pltpu.SemaphoreType.DMA((N_PEERS,))] * (2 * n)
    outs = pl.pallas_call(
        body, name=name, in_specs=[HBM] * (2 * n),
        out_specs=[HBM] * (2 * n) + [SEM] * (2 * n) + [pl.BlockSpec(memory_space=pltpu.VMEM)],
        out_shape=[hbm(a) for a in srcs] + [hbm(a) for a in lands] + sems + [jax.ShapeDtypeStruct((8, 128), F32)],
        input_output_aliases={k: k for k in range(2 * n)},
        compiler_params=pltpu.CompilerParams(has_side_effects=pltpu.SideEffectType.DATAFLOW_SIDE_EFFECTING),
    )(*[pltpu.with_memory_space_constraint(a, pltpu.HBM) for a in list(srcs) + list(lands)])
    return dict(srcs=outs[:n], lands=outs[n:2 * n], send=outs[2 * n:3 * n], recv=outs[3 * n:4 * n],
                modes=list(modes)), outs[4 * n]


def _send_wait(name, sent, ks, after):
    n = len(ks)
    srcs = [sent["srcs"][k] for k in ks]
    lands = [sent["lands"][k] for k in ks]
    modes = [sent["modes"][k] for k in ks]

    def body(*refs):
        src_refs, land_refs = refs[:n], refs[n:2 * n]
        send_sems, recv_sems = refs[2 * n:3 * n], refs[3 * n:4 * n]
        me, peers = _peers()
        for k in range(n):
            for r, (peer, slot) in enumerate(peers, 1):
                args = (src_refs[k], land_refs[k], send_sems[k], recv_sems[k], r, me, peer, slot, modes[k])
                _peer_copy(*args).wait_send()
                _peer_arrival(*args).wait_recv()

    hbm = lambda a: pltpu.HBM(a.shape, a.dtype)
    outs = pl.pallas_call(
        body, name=name, in_specs=[HBM] * (2 * n) + [SEM] * (2 * n) + [ANY],
        out_specs=[HBM] * (2 * n), out_shape=[hbm(a) for a in srcs] + [hbm(a) for a in lands],
        input_output_aliases={k: k for k in range(2 * n)},
        compiler_params=pltpu.CompilerParams(has_side_effects=pltpu.SideEffectType.DATAFLOW_SIDE_EFFECTING),
    )(*srcs, *lands, *[sent["send"][k] for k in ks], *[sent["recv"][k] for k in ks], after)
    return outs[n:], outs[:n]


def _sum_blocks(own, land, ids):
    nseg, _, rows, cols = land.shape

    def body(ids_ref, own_ref, land_ref, o_ref):
        me = ids_ref[1]
        total = None
        for j in range(N_DEV):
            slot = jnp.maximum(jnp.bitwise_xor(me, j) - 1, 0)
            term = jnp.where(me == j, own_ref[...], land_ref[slot]).astype(F32)
            total = term if total is None else total + term
        o_ref[...] = total

    return pl.pallas_call(
        body, name="sum_blocks",
        grid_spec=pltpu.PrefetchScalarGridSpec(
            num_scalar_prefetch=1, grid=(nseg,),
            in_specs=[pl.BlockSpec((None, None, rows, cols), lambda s, ids: (s, ids[0], 0, 0)),
                      pl.BlockSpec((None, N_PEERS, rows, cols), lambda s, ids: (s, 0, 0, 0))],
            out_specs=pl.BlockSpec((None, rows, cols), lambda s, ids: (s, 0, 0))),
        out_shape=jax.ShapeDtypeStruct((nseg, rows, cols), F32),
        compiler_params=_params("parallel"),
    )(ids, own, land)


def _adamw(w, g, m, v):
    shape = w.shape
    cols = shape[-1]
    rows = w.size // cols
    tr = rows
    for cand in (512, 256, 128, 64, 32, 16, 8):
        if rows % cand == 0:
            tr = cand
            break
    c1 = 1.0 / (1.0 - ADAM_B1 ** ADAM_STEP)
    c2 = 1.0 / (1.0 - ADAM_B2 ** ADAM_STEP)

    def body(w_ref, g_ref, m_ref, v_ref, d_ref, nm_ref, nv_ref):
        g = g_ref[...]
        m = ADAM_B1 * m_ref[...] + (1.0 - ADAM_B1) * g
        v = ADAM_B2 * v_ref[...] + (1.0 - ADAM_B2) * (g * g)
        nm_ref[...] = m
        nv_ref[...] = v
        d_ref[...] = -ADAM_LR * ((m * c1) / (jnp.sqrt(v * c2) + ADAM_EPS) + ADAM_WD * w_ref[...])

    spec = pl.BlockSpec((tr, cols), lambda i: (i, 0))
    outs = pl.pallas_call(
        body, name="adamw", grid=(rows // tr,),
        in_specs=[spec] * 4, out_specs=[spec] * 3,
        out_shape=[jax.ShapeDtypeStruct((rows, cols), F32)] * 3,
        compiler_params=_params("parallel"),
    )(*[a.reshape(rows, cols) for a in (w, g, m, v)])
    return [o.reshape(shape) for o in outs]


SMALL = ("ln_mix_pre", "attn_sinks", "gm_ln_g", "gm_ln_b", "gm_ws", "gm_bs", "g_attn_out", "g_gm_out",
         "ln_mix_post", "ln_ffn_pre", "ln_ffn_post", "ln_ple_gate")
WEIGHTS = ("ln_mix_pre", "w_in", "attn_sinks", "gm_ln_g", "gm_ln_b", "gm_ws", "gm_bs", "g_attn_out", "g_gm_out",
           "w_out", "ln_mix_post", "ln_ffn_pre", "w_ffn_gate", "w_ffn_up", "w_ffn_down", "ln_ffn_post", "w_ple",
           "ln_ple_gate", "w_ple_gate")


def _pack_shards(w, l):
    sa = jnp.stack([w["w_ffn_gate"][l].T, w["w_ffn_up"][l].T, w["w_ffn_down"][l]])[:, None]
    sb = jnp.stack([w["w_out"][l], w["w_ple_gate"][l]])[:, None]
    return [w["w_in"][l].T[None, None].astype(BF16), sb.astype(BF16), w["w_ple"][l].T[None, None].astype(BF16),
            sa.astype(BF16)]


def _unpack_grads(rc, rb, rp, ra):
    return {"w_in": rc[0].T, "w_out": rb[0], "w_ple_gate": rb[1], "w_ple": rp[0].T,
            "w_ffn_gate": ra[0].T, "w_ffn_up": ra[1].T, "w_ffn_down": ra[2]}


def kernel(x, p, ln_mix_pre, w_in, attn_sinks, gm_ln_g, gm_ln_b, gm_ws, gm_bs, g_attn_out, g_gm_out, w_out, ln_mix_post, ln_ffn_pre, w_ffn_gate, w_ffn_up, w_ffn_down, ln_ffn_post, w_ple, ln_ple_gate, w_ple_gate, loss_target, m_ln_mix_pre, m_w_in, m_attn_sinks, m_gm_ln_g, m_gm_ln_b, m_gm_ws, m_gm_bs, m_g_attn_out, m_g_gm_out, m_w_out, m_ln_mix_post, m_ln_ffn_pre, m_w_ffn_gate, m_w_ffn_up, m_w_ffn_down, m_ln_ffn_post, m_w_ple, m_ln_ple_gate, m_w_ple_gate, v_ln_mix_pre, v_w_in, v_attn_sinks, v_gm_ln_g, v_gm_ln_b, v_gm_ws, v_gm_bs, v_g_attn_out, v_g_gm_out, v_w_out, v_ln_mix_post, v_ln_ffn_pre, v_w_ffn_gate, v_w_ffn_up, v_w_ffn_down, v_ln_ffn_post, v_w_ple, v_ln_ple_gate, v_w_ple_gate):
    given = dict(locals())
    w = {n: given[n] for n in WEIGHTS}
    sp = {n: w[n] for n in SMALL}
    kinds = ("c", "b", "p", "a")

    me, _ = _peers()
    shards = [s for l in range(DEPTH) for s in _pack_shards(w, l)]
    lands = [lax.dynamic_update_slice(lax.empty((s.shape[0], N_DEV) + s.shape[2:], BF16), s, (0, me, 0, 0))
             for s in shards]
    gather, token = _send_start("gather_start", shards, lands, [GATHER] * len(shards))
    layer_weights = [{} for _ in range(DEPTH)]

    def weights_of(l):
        def get(kind, after):
            have = layer_weights[l]
            if kind not in have:
                if l == 0:
                    group = {"c": ("c",), "b": ("b", "p"), "p": ("b", "p"), "a": ("a",)}[kind]
                    after = token if kind == "c" else after
                else:
                    group = kinds
                got, _ = _send_wait(f"gather_wait_{l}{group[0]}", gather, [4 * l + kinds.index(k) for k in group], after)
                for k, g in zip(group, got):
                    have[k] = g.reshape(-1, g.shape[-1])
            return have[kind]
        return get

    h = x[0]
    saved = []
    for l in range(DEPTH):
        h, s = _layer_fwd(h, p[l, 0], sp, l, weights_of(l))
        saved.append(s)
    dh, sq = _loss_head(h, loss_target[0])

    reduces = []
    after = token
    view = lambda g, rows: g.reshape(-1, N_DEV, rows, g.shape[-1])
    pack16 = lambda s: s.astype(BF16)[None, None]
    landing = lambda a: lax.empty((a.shape[0], N_PEERS) + a.shape[2:], BF16)

    def send(name, bufs, modes):
        return _send_start(name, bufs, [landing(a) for a in bufs], modes)

    for l in reversed(range(DEPTH)):
        lw = layer_weights[l]
        carry, ga, gp, gb = _layer_bwd_upper(dh, saved[l], p[l, 0], sp, l, lw["a"], lw["b"], lw["p"], after)
        sent1, after = send(f"reduce_start_{l}a", [view(ga, ROWS_A), view(gp, ROWS_B)], [SCATTER, SCATTER])
        carry, gb, gating = _layer_bwd_middle(carry, saved[l], sp, l, lw["b"], gb, after)
        sent2, after = send(f"reduce_start_{l}b", [view(gb, ROWS_B), pack16(gating)], [SCATTER, SPREAD])
        dh, gc, rest = _layer_bwd_lower(carry, saved[l], sp, l, lw["c"], after)
        sent3, after = send(f"reduce_start_{l}c", [view(gc, ROWS_C), pack16(rest)], [SCATTER, SPREAD])
        reduces.append((l, sent1, sent2, sent3))

    mine = jnp.stack([me, me]).astype(jnp.int32)
    whole = jnp.stack([jnp.zeros_like(me), me]).astype(jnp.int32)
    per_layer, gatings, rests = [None] * DEPTH, [None] * DEPTH, [None] * DEPTH
    for l, sent1, sent2, sent3 in reduces:
        (la, lp), (ga, gp) = _send_wait(f"reduce_wait_{l}a", sent1, [0, 1], dh)
        (lb, lg), (gb, gg) = _send_wait(f"reduce_wait_{l}b", sent2, [0, 1], dh)
        (lc, lr), (gc, gr) = _send_wait(f"reduce_wait_{l}c", sent3, [0, 1], dh)
        per_layer[l] = _unpack_grads(_sum_blocks(gc, lc, mine), _sum_blocks(gb, lb, mine), _sum_blocks(gp, lp, mine),
                                     _sum_blocks(ga, la, mine))
        gatings[l] = _sum_blocks(gg, lg, whole)[0]
        rests[l] = _sum_blocks(gr, lr, whole)[0]
    grads = {n: jnp.stack([per_layer[l][n] for l in range(DEPTH)]) for n in per_layer[0]}
    grads.update(_small_unpack(jnp.stack(gatings), jnp.stack(rests)))
    grad_x = dh
    loss = lax.psum(sq[0, 0] * (0.5 / D_MODEL), AXES)
    delta, new_m, new_v = {}, {}, {}
    for n in WEIGHTS:
        delta[n], new_m[n], new_v[n] = _adamw(w[n], grads[n], given["m_" + n], given["v_" + n])
    return (loss, grad_x[None], *[grads[n] for n in WEIGHTS], *[delta[n] for n in WEIGHTS],
            *[new_m[n] for n in WEIGHTS], *[new_v[n] for n in WEIGHTS])
```

```python
import math

import jax
import jax.numpy as jnp
from jax import lax
from jax.experimental import pallas as pl
from jax.experimental.pallas import tpu as pltpu

F32 = jnp.float32
BF16 = jnp.bfloat16
MESH = pl.DeviceIdType.MESH
AXES = ("x", "y", "c")

D_MODEL = 1024
DEPTH = 4
N_DEV = 8
HEAD_DIM = 64
ATTN_W = 512
KV_W = 128
GM_W = 512
D_IN = 1792
D_FF = 2816
PLE_DIM = 256
BLK = 128
FF_CHUNK = 256
WGRAD_TOKENS = 1024
NORM_EPS = 1e-6
NEG_BIG = -1e30
ATTN_SCALE = HEAD_DIM ** -0.5

ADAM_LR = 0.001
ADAM_B1 = 0.9
ADAM_B2 = 0.999
ADAM_EPS = 1e-08
ADAM_WD = 0.01
ADAM_STEP = 10

ROWS_A = D_FF // N_DEV
ROWS_B = D_MODEL // N_DEV
ROWS_C = D_IN // N_DEV
GATING_ROWS = 144
REST_ROWS = 56
SMALL_ROWS = 200

VMEM_LIMIT = 56 * 2 ** 20


def _params(*sem):
    return pltpu.CompilerParams(dimension_semantics=sem, vmem_limit_bytes=VMEM_LIMIT)


def _dot(a, b):
    return jnp.dot(a, b, preferred_element_type=F32)


def _dot_nt(a, b):
    return lax.dot_general(a, b, (((1,), (1,)), ((), ())), preferred_element_type=F32)


def _dot_tn(a, b):
    return lax.dot_general(a, b, (((0,), (0,)), ((), ())), preferred_element_type=F32)


def _rms_fwd(x, g):
    r = lax.rsqrt(jnp.mean(x * x, axis=-1, keepdims=True) + NORM_EPS)
    return x * r * g


def _rms_bwd(x, g, dy):
    r = lax.rsqrt(jnp.mean(x * x, axis=-1, keepdims=True) + NORM_EPS)
    xh = x * r
    dg = jnp.sum(dy * xh, axis=0, keepdims=True)
    dxh = dy * g
    dx = r * (dxh - xh * jnp.mean(dxh * xh, axis=-1, keepdims=True))
    return dx, dg


_GELU_C = math.sqrt(2.0 / math.pi)


def _gelu(x):
    t = jnp.tanh(_GELU_C * (x + 0.044715 * (x * x * x)))
    return 0.5 * x * (1.0 + t)


def _gelu_grad(x):
    x2 = x * x
    t = jnp.tanh(_GELU_C * (x + 0.044715 * (x2 * x)))
    return 0.5 * (1.0 + t) + 0.5 * x * (1.0 - t * t) * (_GELU_C * (1.0 + 3.0 * 0.044715 * x2))


def _sigmoid(x):
    return 1.0 / (1.0 + jnp.exp(-x))


def _row_spec(tm, n):
    return pl.BlockSpec((tm, n), lambda i: (i, 0))


def _vec_spec(n):
    return pl.BlockSpec((1, n), lambda i: (0, 0))


def _seg_spec(rows, cols, seg):
    return pl.BlockSpec((N_DEV * rows, cols), lambda i: (seg, 0), pipeline_mode=pl.Buffered(1))


def _zero_at(first, *refs):
    @pl.when(first)
    def _():
        for r in refs:
            r[...] = jnp.zeros(r.shape, r.dtype)


def _tile(t, want):
    return min(t, want)


def _in_proj(h, g, wc, layer):
    t = h.shape[0]
    tm = _tile(t, 512)

    def body(h_ref, g_ref, w_ref, a_ref, q_ref, kv_ref, zu_ref, zv_ref):
        a = _rms_fwd(h_ref[...], g_ref[...]).astype(BF16)
        a_ref[...] = a
        q_ref[...] = _dot_nt(a, w_ref[0:512, :]).astype(BF16)
        kv_ref[...] = _dot_nt(a, w_ref[512:768, :]).astype(BF16)
        zu_ref[...] = _dot_nt(a, w_ref[768:1280, :])
        zv_ref[...] = _dot_nt(a, w_ref[1280:1792, :])

    return pl.pallas_call(
        body, name="in_proj", grid=(t // tm,),
        in_specs=[_row_spec(tm, D_MODEL), _vec_spec(D_MODEL), _seg_spec(ROWS_C, D_MODEL, layer)],
        out_specs=[_row_spec(tm, D_MODEL), _row_spec(tm, ATTN_W), _row_spec(tm, 2 * KV_W),
                   _row_spec(tm, GM_W), _row_spec(tm, GM_W)],
        out_shape=[jax.ShapeDtypeStruct((t, D_MODEL), BF16), jax.ShapeDtypeStruct((t, ATTN_W), BF16),
                   jax.ShapeDtypeStruct((t, 2 * KV_W), BF16), jax.ShapeDtypeStruct((t, GM_W), F32),
                   jax.ShapeDtypeStruct((t, GM_W), F32)],
        compiler_params=_params("parallel"),
    )(h, g, wc)


def _head_variants(x, low):
    xr = pltpu.roll(x, 64, axis=1)
    zero = jnp.zeros_like(x)
    return {
        (0, 0): jnp.where(low, x, zero).astype(BF16),
        (0, 1): jnp.where(low, zero, xr).astype(BF16),
        (1, 0): jnp.where(low, xr, zero).astype(BF16),
        (1, 1): jnp.where(low, zero, x).astype(BF16),
    }


def _attn_masks(i):
    row = lax.broadcasted_iota(jnp.int32, (BLK, BLK), 0)
    lane = lax.broadcasted_iota(jnp.int32, (BLK, BLK), 1)
    vcur = row >= lane
    dist = jnp.where(vcur, row - lane, row - lane + BLK).astype(F32)
    valid = jnp.logical_or(vcur, i > 0)
    return lane < 64, vcur, dist, valid


def _head_key(h):
    return (h // 4, h % 2)


def _stack_kv(prev, cur, g):
    return jnp.concatenate([prev[(g, 0)], cur[(g, 0)], prev[(g, 1)], cur[(g, 1)]], axis=0)


def _split_cols(p, vcur):
    return [jnp.where(vcur, 0.0, p).astype(BF16), jnp.where(vcur, p, 0.0).astype(BF16)]


def _attn_scores(q_ref, rows, stacked, vcur):
    out = []
    for col in range(4):
        big = _dot_nt(q_ref[rows, col * 128:(col + 1) * 128], stacked[col // 2])
        for half in range(2):
            out.append(jnp.where(vcur, big[:, half * 256 + 128:half * 256 + 256], big[:, half * 256:half * 256 + 128]))
    return out


def _attn_probs(s, h, sink, dist, valid):
    s = s * ATTN_SCALE - (2.0 ** -(h + 1)) * dist
    if valid is not None:
        s = jnp.where(valid, s, NEG_BIG)
    m = jnp.maximum(jnp.max(s, axis=1, keepdims=True), sink)
    e = jnp.exp(s - m)
    es = jnp.exp(sink - m)
    inv = 1.0 / (jnp.sum(e, axis=1, keepdims=True) + es)
    return e * inv, es * inv


def _kv_prev_spec(blocks):
    return pl.BlockSpec((BLK, 2 * KV_W), lambda i: (jnp.maximum(i * blocks - 1, 0), 0))


def _kv_variants(kv_ref, rows, low):
    return (_head_variants(kv_ref[rows, 0:128].astype(F32), low), _head_variants(kv_ref[rows, 128:256].astype(F32), low))


def _attn_fwd(q, kv, sinks):
    t = q.shape[0]
    tq = _tile(t, 512)
    blocks = tq // BLK

    def body(sink_ref, q_ref, kvc_ref, kvp_ref, o_ref):
        low, vcur, dist, valid = _attn_masks(pl.program_id(0))
        kp, vp = _kv_variants(kvp_ref, slice(None), low)
        for b in range(blocks):
            rows = slice(b * BLK, (b + 1) * BLK)
            kc, vc = _kv_variants(kvc_ref, rows, low)
            ks = [_stack_kv(kp, kc, g) for g in range(2)]
            vs = [_stack_kv(vp, vc, g) for g in range(2)]
            scores = _attn_scores(q_ref, rows, ks, vcur)
            probs = [_attn_probs(scores[h], h, sink_ref[h], dist, valid if b == 0 else None)[0] for h in range(8)]
            for col in range(4):
                p_col = jnp.concatenate(_split_cols(probs[2 * col], vcur) + _split_cols(probs[2 * col + 1], vcur), axis=1)
                o_ref[rows, col * 128:(col + 1) * 128] = _dot(p_col, vs[col // 2]).astype(BF16)
            kp, vp = kc, vc

    return pl.pallas_call(
        body, name="attn_fwd", grid=(t // tq,),
        in_specs=[pl.BlockSpec(memory_space=pltpu.SMEM), _row_spec(tq, ATTN_W), _row_spec(tq, 2 * KV_W),
                  _kv_prev_spec(blocks)],
        out_specs=_row_spec(tq, ATTN_W),
        out_shape=jax.ShapeDtypeStruct((t, ATTN_W), BF16),
        compiler_params=_params("parallel"),
    )(sinks, q, kv, kv)


def _gm_forward_block(zu, zv, lng, lnb, w_ref, bsx, low):
    gu = _gelu(zu)
    gv = _gelu(zv)
    mu = jnp.mean(gv, axis=-1, keepdims=True)
    xc = gv - mu
    rstd = lax.rsqrt(jnp.mean(xc * xc, axis=-1, keepdims=True) + NORM_EPS)
    xn = xc * rstd
    ln = xn * lng + lnb
    mixed = []
    for col in range(4):
        lc = ln[:, col * 128:(col + 1) * 128]
        lo = jnp.where(low, lc, 0.0).astype(BF16)
        hi = jnp.where(low, 0.0, lc).astype(BF16)
        mixed.append(_dot(w_ref[2 * col], lo) + _dot(w_ref[2 * col + 1], hi) + bsx[:, col * 128:(col + 1) * 128])
    return gu, ln, xn, rstd, mixed


def _gm_fwd(zu, zv, lng, lnb, wtril, bsx):
    t = zu.shape[0]
    tm = _tile(t, 512)

    def body(zu_ref, zv_ref, g_ref, b_ref, w_ref, bs_ref, o_ref):
        low = lax.broadcasted_iota(jnp.int32, (BLK, BLK), 1) < 64
        for b in range(tm // BLK):
            rows = slice(b * BLK, (b + 1) * BLK)
            gu, _, _, _, mixed = _gm_forward_block(zu_ref[rows, :], zv_ref[rows, :], g_ref[...], b_ref[...], w_ref,
                                                   bs_ref[...], low)
            for col in range(4):
                o_ref[rows, col * 128:(col + 1) * 128] = (gu[:, col * 128:(col + 1) * 128] * mixed[col]).astype(BF16)

    return pl.pallas_call(
        body, name="gm_fwd", grid=(t // tm,),
        in_specs=[_row_spec(tm, GM_W), _row_spec(tm, GM_W), _vec_spec(GM_W), _vec_spec(GM_W),
                  pl.BlockSpec((8, BLK, BLK), lambda i: (0, 0, 0)), pl.BlockSpec((BLK, GM_W), lambda i: (0, 0))],
        out_specs=_row_spec(tm, GM_W),
        out_shape=jax.ShapeDtypeStruct((t, GM_W), BF16),
        compiler_params=_params("parallel"),
    )(zu, zv, lng, lnb, wtril, bsx)


def _out_proj(attn, gm, h, ga, gg, gpost, wb, layer):
    t = h.shape[0]
    tm = _tile(t, 512)

    def body(a_ref, m_ref, h_ref, ga_ref, gg_ref, gp_ref, w_ref, heads_ref, mix_ref, h1_ref):
        ha = _rms_fwd(a_ref[...].astype(F32), ga_ref[...]).astype(BF16)
        hg = _rms_fwd(m_ref[...].astype(F32), gg_ref[...]).astype(BF16)
        heads_ref[:, 0:512] = ha
        heads_ref[:, 512:1024] = hg
        mix = _dot(ha, w_ref[0:512, :]) + _dot(hg, w_ref[512:1024, :])
        mix_ref[...] = mix.astype(BF16)
        h1_ref[...] = h_ref[...] + _rms_fwd(mix, gp_ref[...])

    return pl.pallas_call(
        body, name="out_proj", grid=(t // tm,),
        in_specs=[_row_spec(tm, ATTN_W), _row_spec(tm, GM_W), _row_spec(tm, D_MODEL), _vec_spec(ATTN_W),
                  _vec_spec(GM_W), _vec_spec(D_MODEL), _seg_spec(ROWS_B, D_MODEL, 2 * layer)],
        out_specs=[_row_spec(tm, D_MODEL), _row_spec(tm, D_MODEL), _row_spec(tm, D_MODEL)],
        out_shape=[jax.ShapeDtypeStruct((t, D_MODEL), BF16), jax.ShapeDtypeStruct((t, D_MODEL), BF16),
                   jax.ShapeDtypeStruct((t, D_MODEL), F32)],
        compiler_params=_params("parallel"),
    )(attn, gm, h, ga, gg, gpost, wb)


def _ffn_fwd(h1, p, gpre, gpost, gple, wa, wb, wp, layer):
    t = h1.shape[0]
    tm = _tile(t, 256)

    def body(h_ref, p_ref, gpre_ref, gpost_ref, gple_ref, wg_ref, wu_ref, wd_ref, wpg_ref, wpl_ref,
             f_ref, gp_ref, up_ref, act_ref, fo_ref, h2_ref, hn_ref, gate_ref, h3_ref):
        h = h_ref[...]
        pe = _dot_nt(p_ref[...].astype(BF16), wpl_ref[...])
        f = _rms_fwd(h, gpre_ref[...]).astype(BF16)
        f_ref[...] = f
        chunks = [slice(j * FF_CHUNK, (j + 1) * FF_CHUNK) for j in range(D_FF // FF_CHUNK)]
        fo = None
        gp, up = _dot_nt(f, wg_ref[chunks[0], :]), _dot_nt(f, wu_ref[chunks[0], :])
        for j, cols in enumerate(chunks):
            if j + 1 < len(chunks):
                gp_next, up_next = _dot_nt(f, wg_ref[chunks[j + 1], :]), _dot_nt(f, wu_ref[chunks[j + 1], :])
            act = (gp * _sigmoid(gp) * up).astype(BF16)
            gp_ref[:, cols] = gp.astype(BF16)
            up_ref[:, cols] = up.astype(BF16)
            act_ref[:, cols] = act
            part = _dot(act, wd_ref[cols, :])
            fo = part if fo is None else fo + part
            if j + 1 < len(chunks):
                gp, up = gp_next, up_next
        fo_ref[...] = fo
        h2 = h + _rms_fwd(fo, gpost_ref[...])
        h2_ref[...] = h2
        hn = _rms_fwd(h2, gple_ref[...]).astype(BF16)
        hn_ref[...] = hn
        gate = _sigmoid(_dot(hn, wpg_ref[...]))
        gate_ref[...] = gate.astype(BF16)
        h3_ref[...] = h2 + pe * gate

    wide = _row_spec(tm, D_FF)
    row = _row_spec(tm, D_MODEL)
    vec = _vec_spec(D_MODEL)
    return pl.pallas_call(
        body, name="ffn_fwd", grid=(t // tm,),
        in_specs=[row, _row_spec(tm, PLE_DIM), vec, vec, vec, _seg_spec(ROWS_A, D_MODEL, 3 * layer),
                  _seg_spec(ROWS_A, D_MODEL, 3 * layer + 1), _seg_spec(ROWS_A, D_MODEL, 3 * layer + 2),
                  _seg_spec(ROWS_B, D_MODEL, 2 * layer + 1), _seg_spec(ROWS_B, PLE_DIM, layer)],
        out_specs=[row, wide, wide, wide, row, row, row, row, row],
        out_shape=[jax.ShapeDtypeStruct((t, D_MODEL), BF16)] + [jax.ShapeDtypeStruct((t, D_FF), BF16)] * 3
        + [jax.ShapeDtypeStruct((t, D_MODEL), F32)] * 2 + [jax.ShapeDtypeStruct((t, D_MODEL), BF16)] * 2
        + [jax.ShapeDtypeStruct((t, D_MODEL), F32)],
        compiler_params=_params("parallel"),
    )(h1, p, gpre, gpost, gple, wa, wa, wa, wb, wp)


def _loss_head(y, target):
    t = y.shape[0]
    tm = _tile(t, 512)

    def body(y_ref, t_ref, dy_ref, l_ref):
        _zero_at(pl.program_id(0) == 0, l_ref)
        e = y_ref[...] - t_ref[...]
        dy_ref[...] = e * (1.0 / D_MODEL)
        s = jnp.sum(jnp.sum(e * e, axis=1, keepdims=True), axis=0, keepdims=True)
        l_ref[...] += jnp.broadcast_to(s, (1, 128))

    return pl.pallas_call(
        body, name="loss_head", grid=(t // tm,),
        in_specs=[_row_spec(tm, D_MODEL), _row_spec(tm, D_MODEL)],
        out_specs=[_row_spec(tm, D_MODEL), _vec_spec(128)],
        out_shape=[jax.ShapeDtypeStruct((t, D_MODEL), F32), jax.ShapeDtypeStruct((1, 128), F32)],
        compiler_params=_params("arbitrary"),
    )(y, target)


def _ffn_bwd(dh3, h2, gate, p, fo, gp, up, h1, gple, gpost, gpre, wa, wb, wp, layer, after):
    t = dh3.shape[0]
    tm = _tile(t, 256)

    def body(d3_ref, h2_ref, gate_ref, p_ref, fo_ref, gp_ref, up_ref, h_ref, gple_ref, gpost_ref, gpre_ref,
             wg_ref, wu_ref, wd_ref, wpg_ref, wpl_ref, after_ref,
             dgl_ref, dpe_ref, dfo_ref, dgp_ref, dup_ref, dh1_ref, dgple_ref, dgpost_ref, dgpre_ref):
        _zero_at(pl.program_id(0) == 0, dgple_ref, dgpost_ref, dgpre_ref)
        d3 = d3_ref[...]
        gate = gate_ref[...].astype(F32)
        pe = _dot_nt(p_ref[...].astype(BF16), wpl_ref[...])
        dpe_ref[...] = (d3 * gate).astype(BF16)
        dgl = (d3 * pe * gate * (1.0 - gate)).astype(BF16)
        dgl_ref[...] = dgl
        dx2, dgple = _rms_bwd(h2_ref[...], gple_ref[...], _dot_nt(dgl, wpg_ref[...]))
        dgple_ref[...] += dgple
        d = d3 + dx2
        dfo, dgpost = _rms_bwd(fo_ref[...], gpost_ref[...], d)
        dfo = dfo.astype(BF16)
        dfo_ref[...] = dfo
        dgpost_ref[...] += dgpost
        chunks = [slice(j * FF_CHUNK, (j + 1) * FF_CHUNK) for j in range(D_FF // FF_CHUNK)]
        df = None
        dact = _dot_nt(dfo, wd_ref[chunks[0], :])
        for j, cols in enumerate(chunks):
            if j + 1 < len(chunks):
                dact_next = _dot_nt(dfo, wd_ref[chunks[j + 1], :])
            gp = gp_ref[:, cols].astype(F32)
            sg = _sigmoid(gp)
            dgp = (dact * up_ref[:, cols].astype(F32) * (sg * (1.0 + gp * (1.0 - sg)))).astype(BF16)
            dup = (dact * (gp * sg)).astype(BF16)
            dgp_ref[:, cols] = dgp
            dup_ref[:, cols] = dup
            part = _dot(dgp, wg_ref[cols, :]) + _dot(dup, wu_ref[cols, :])
            df = part if df is None else df + part
            if j + 1 < len(chunks):
                dact = dact_next
        dx, dgpre = _rms_bwd(h_ref[...], gpre_ref[...], df)
        dh1_ref[...] = d + dx
        dgpre_ref[...] += dgpre

    wide = _row_spec(tm, D_FF)
    row = _row_spec(tm, D_MODEL)
    vec = _vec_spec(D_MODEL)
    narrow = jax.ShapeDtypeStruct((t, D_MODEL), BF16)
    return pl.pallas_call(
        body, name="ffn_bwd", grid=(t // tm,),
        in_specs=[row, row, row, _row_spec(tm, PLE_DIM), row, wide, wide, row, vec, vec, vec,
                  _seg_spec(ROWS_A, D_MODEL, 3 * layer), _seg_spec(ROWS_A, D_MODEL, 3 * layer + 1),
                  _seg_spec(ROWS_A, D_MODEL, 3 * layer + 2), _seg_spec(ROWS_B, D_MODEL, 2 * layer + 1),
                  _seg_spec(ROWS_B, PLE_DIM, layer), pl.BlockSpec(memory_space=pl.ANY)],
        out_specs=[row, row, row, wide, wide, row, vec, vec, vec],
        out_shape=[narrow, narrow, narrow, jax.ShapeDtypeStruct((t, D_FF), BF16), jax.ShapeDtypeStruct((t, D_FF), BF16),
                   jax.ShapeDtypeStruct((t, D_MODEL), F32)] + [jax.ShapeDtypeStruct((1, D_MODEL), F32)] * 3,
        compiler_params=_params("arbitrary"),
    )(dh3, h2, gate, p, fo, gp, up, h1, gple, gpost, gpre, wa, wa, wa, wb, wp, after)


def _out_proj_bwd(dh1, mix, attn, gm, gpost, ga, gg, wb, layer, after):
    t = dh1.shape[0]
    tm = _tile(t, 512)

    def body(d_ref, mix_ref, a_ref, m_ref, gp_ref, ga_ref, gg_ref, w_ref, after_ref,
             dmix_ref, da_ref, dm_ref, dgp_ref, dga_ref, dgg_ref):
        _zero_at(pl.program_id(0) == 0, dgp_ref, dga_ref, dgg_ref)
        dmix, dgp = _rms_bwd(mix_ref[...].astype(F32), gp_ref[...], d_ref[...])
        dmix = dmix.astype(BF16)
        dmix_ref[...] = dmix
        da, dga = _rms_bwd(a_ref[...].astype(F32), ga_ref[...], _dot_nt(dmix, w_ref[0:512, :]))
        dm, dgg = _rms_bwd(m_ref[...].astype(F32), gg_ref[...], _dot_nt(dmix, w_ref[512:1024, :]))
        da_ref[...] = da.astype(BF16)
        dm_ref[...] = dm
        dgp_ref[...] += dgp
        dga_ref[...] += dga
        dgg_ref[...] += dgg

    return pl.pallas_call(
        body, name="out_proj_bwd", grid=(t // tm,),
        in_specs=[_row_spec(tm, D_MODEL), _row_spec(tm, D_MODEL), _row_spec(tm, ATTN_W), _row_spec(tm, GM_W),
                  _vec_spec(D_MODEL), _vec_spec(ATTN_W), _vec_spec(GM_W), _seg_spec(ROWS_B, D_MODEL, 2 * layer),
                  pl.BlockSpec(memory_space=pl.ANY)],
        out_specs=[_row_spec(tm, D_MODEL), _row_spec(tm, ATTN_W), _row_spec(tm, GM_W),
                   _vec_spec(D_MODEL), _vec_spec(ATTN_W), _vec_spec(GM_W)],
        out_shape=[jax.ShapeDtypeStruct((t, D_MODEL), BF16), jax.ShapeDtypeStruct((t, ATTN_W), BF16),
                   jax.ShapeDtypeStruct((t, GM_W), F32), jax.ShapeDtypeStruct((1, D_MODEL), F32),
                   jax.ShapeDtypeStruct((1, ATTN_W), F32), jax.ShapeDtypeStruct((1, GM_W), F32)],
        compiler_params=_params("arbitrary"),
    )(dh1, mix, attn, gm, gpost, ga, gg, wb, after)


def _split3(x):
    hi = x.astype(BF16)
    r1 = x - hi.astype(F32)
    mid = r1.astype(BF16)
    lo = (r1 - mid.astype(F32)).astype(BF16)
    return hi, mid, lo


def _gm_bwd(dgm, zu, zv, lng, lnb, wtril, bsx):
    t = zu.shape[0]
    tm = _tile(t, 512)
    nb = t // tm

    def body(d_ref, zu_ref, zv_ref, g_ref, b_ref, w_ref, bs_ref,
             dzu_ref, dzv_ref, dw_ref, dbs_ref, dlg_ref, dlb_ref, dbsx_ref):
        i = pl.program_id(0)
        _zero_at(i == 0, dw_ref, dlg_ref, dlb_ref, dbsx_ref)
        row = lax.broadcasted_iota(jnp.int32, (BLK, BLK), 0)
        lane = lax.broadcasted_iota(jnp.int32, (BLK, BLK), 1)
        low = lane < 64
        tril = row >= lane
        lng = g_ref[...]
        for b in range(tm // BLK):
            rows = slice(b * BLK, (b + 1) * BLK)
            zu = zu_ref[rows, :]
            zv = zv_ref[rows, :]
            gu, ln, xn, rstd, mixed = _gm_forward_block(zu, zv, lng, b_ref[...], w_ref, bs_ref[...], low)
            dgm = d_ref[rows, :]
            dgu_cols, dmx_cols, dln_cols = [], [], []
            for col in range(4):
                sl = slice(col * 128, (col + 1) * 128)
                dg = dgm[:, sl]
                dgu_cols.append(dg * mixed[col])
                dmx = dg * gu[:, sl]
                dmx_cols.append(dmx)
                lc = ln[:, sl]
                halves = (jnp.where(low, lc, 0.0).astype(BF16), jnp.where(low, 0.0, lc).astype(BF16))
                dmx16 = dmx.astype(BF16)
                dmx_half = (jnp.where(low, dmx, 0.0).astype(BF16), jnp.where(low, 0.0, dmx).astype(BF16))
                dln = None
                for half in range(2):
                    hd = 2 * col + half
                    dw_ref[hd] += jnp.where(tril, _dot_nt(dmx16, halves[half]), 0.0)
                    part = _dot_tn(w_ref[hd], dmx_half[half])
                    dln = part if dln is None else dln + part
                dln_cols.append(dln)
            dgu = jnp.concatenate(dgu_cols, axis=1)
            dmx = jnp.concatenate(dmx_cols, axis=1)
            dln = jnp.concatenate(dln_cols, axis=1)
            dzu_ref[rows, :] = (dgu * _gelu_grad(zu)).astype(BF16)
            dbsx_ref[...] += dmx
            dlg_ref[...] += jnp.sum(dln * xn, axis=0, keepdims=True)
            dlb_ref[...] += jnp.sum(dln, axis=0, keepdims=True)
            dxn = dln * lng
            dgv = rstd * (dxn - jnp.mean(dxn, axis=-1, keepdims=True) - xn * jnp.mean(dxn * xn, axis=-1, keepdims=True))
            dzv_ref[rows, :] = (dgv * _gelu_grad(zv)).astype(BF16)

        @pl.when(i == nb - 1)
        def _():
            r = lax.broadcasted_iota(jnp.int32, (GM_W, BLK), 0)
            c = lax.broadcasted_iota(jnp.int32, (GM_W, BLK), 1)
            e = jnp.where(jnp.logical_and(r >= c * 64, r < c * 64 + 64), 1.0, 0.0).astype(BF16)
            hi, mid, lo = _split3(dbsx_ref[...])
            dbs_ref[...] = _dot(hi, e) + _dot(mid, e) + _dot(lo, e)

    vec = _vec_spec(GM_W)
    return pl.pallas_call(
        body, name="gm_bwd", grid=(nb,),
        in_specs=[_row_spec(tm, GM_W)] * 3 + [vec, vec, pl.BlockSpec((8, BLK, BLK), lambda i: (0, 0, 0)),
                                              pl.BlockSpec((BLK, GM_W), lambda i: (0, 0))],
        out_specs=[_row_spec(tm, GM_W), _row_spec(tm, GM_W), pl.BlockSpec((8, BLK, BLK), lambda i: (0, 0, 0)),
                   pl.BlockSpec((BLK, BLK), lambda i: (0, 0)), vec, vec],
        out_shape=[jax.ShapeDtypeStruct((t, GM_W), BF16), jax.ShapeDtypeStruct((t, GM_W), BF16),
                   jax.ShapeDtypeStruct((8, BLK, BLK), F32), jax.ShapeDtypeStruct((BLK, BLK), F32),
                   jax.ShapeDtypeStruct((1, GM_W), F32), jax.ShapeDtypeStruct((1, GM_W), F32)],
        scratch_shapes=[pltpu.VMEM((BLK, GM_W), F32)],
        compiler_params=_params("arbitrary"),
    )(dgm, zu, zv, lng, lnb, wtril, bsx)


def _attn_bwd(q, kv, do, sinks, after):
    t = q.shape[0]
    tq = _tile(t, 512)
    blocks = tq // BLK

    def body(sink_ref, q_ref, kvc_ref, kvp_ref, do_ref, after_ref, dq_ref, dkv_ref, dkf_ref, ds_ref):
        i = pl.program_id(0)
        _zero_at(i == 0, ds_ref)
        low, vcur, dist, valid = _attn_masks(i)
        head_row = lax.broadcasted_iota(jnp.int32, (8, 128), 0)
        dsink_tile = jnp.zeros((8, 128), F32)
        kp, vp = _kv_variants(kvp_ref, slice(None), low)
        own = None
        for b in range(blocks):
            rows = slice(b * BLK, (b + 1) * BLK)
            kc, vc = _kv_variants(kvc_ref, rows, low)
            ks = [_stack_kv(kp, kc, g) for g in range(2)]
            vs = [_stack_kv(vp, vc, g) for g in range(2)]
            scores = _attn_scores(q_ref, rows, ks, vcur)
            dprobs = _attn_scores(do_ref, rows, vs, vcur)
            parts = []
            for h in range(8):
                p, ps = _attn_probs(scores[h], h, sink_ref[h], dist, valid if b == 0 else None)
                delta = jnp.sum(p * dprobs[h], axis=1, keepdims=True)
                ds = p * (dprobs[h] - delta) * ATTN_SCALE
                dsink = jnp.sum(-ps * delta, axis=0, keepdims=True)
                dsink_tile = jnp.where(head_row == h, dsink_tile + dsink, dsink_tile)
                parts.append(_split_cols(ds, vcur) + _split_cols(p, vcur))
            acc = {}

            def add(name, key, val):
                acc[(name, key)] = val if (name, key) not in acc else acc[(name, key)] + val

            for col in range(4):
                qh = q_ref[rows, col * 128:(col + 1) * 128]
                doh = do_ref[rows, col * 128:(col + 1) * 128]
                ds_col = jnp.concatenate(parts[2 * col][0:2] + parts[2 * col + 1][0:2], axis=1)
                dq_ref[rows, col * 128:(col + 1) * 128] = _dot(ds_col, ks[col // 2]).astype(BF16)
                for half in range(2):
                    key = _head_key(2 * col + half)
                    dsp, dsc, pp, pc = parts[2 * col + half]
                    add("kc", key, _dot_tn(dsc, qh))
                    add("kp", key, _dot_tn(dsp, qh))
                    add("vc", key, _dot_tn(pc, doh))
                    add("vp", key, _dot_tn(pp, doh))

            def place(name):
                head0 = acc[(name, (0, 0))] + pltpu.roll(acc[(name, (0, 1))], 64, axis=1)
                head1 = pltpu.roll(acc[(name, (1, 0))], 64, axis=1) + acc[(name, (1, 1))]
                return jnp.where(low, head0, head1)

            before = (place("kp"), place("vp"))
            if b == 0:
                dkf_ref[:, 0:128], dkf_ref[:, 128:256] = before
            else:
                last = slice((b - 1) * BLK, b * BLK)
                dkv_ref[last, 0:128] = own[0] + before[0]
                dkv_ref[last, 128:256] = own[1] + before[1]
            own = (place("kc"), place("vc"))
            kp, vp = kc, vc
        final = slice((blocks - 1) * BLK, blocks * BLK)
        dkv_ref[final, 0:128], dkv_ref[final, 128:256] = own
        ds_ref[...] += dsink_tile

    row_q = _row_spec(tq, ATTN_W)
    row_kv = _row_spec(tq, 2 * KV_W)
    return pl.pallas_call(
        body, name="attn_bwd", grid=(t // tq,),
        in_specs=[pl.BlockSpec(memory_space=pltpu.SMEM), row_q, row_kv, _kv_prev_spec(blocks), row_q,
                  pl.BlockSpec(memory_space=pl.ANY)],
        out_specs=[row_q, row_kv, _row_spec(BLK, 2 * KV_W), pl.BlockSpec((8, 128), lambda i: (0, 0))],
        out_shape=[jax.ShapeDtypeStruct((t, ATTN_W), BF16), jax.ShapeDtypeStruct((t, 2 * KV_W), F32),
                   jax.ShapeDtypeStruct((t // tq * BLK, 2 * KV_W), F32), jax.ShapeDtypeStruct((8, 128), F32)],
        compiler_params=_params("arbitrary"),
    )(sinks, q, kv, kv, do, after)


def _in_proj_bwd(dq, dkv, dkf, dzu, dzv, h, dres, g, wc, layer):
    t = h.shape[0]
    tm = _tile(t, 512)
    steps = t // tm

    def body(dq_ref, dkv_ref, dkn_ref, dzu_ref, dzv_ref, h_ref, d_ref, g_ref, w_ref, dz_ref, dh_ref, dg_ref):
        i = pl.program_id(0)
        _zero_at(i == 0, dg_ref)
        dq = dq_ref[...]
        tail = dkv_ref[tm - BLK:tm, :] + jnp.where(i < steps - 1, dkn_ref[...], 0.0)
        dkv = tail if tm == BLK else jnp.concatenate([dkv_ref[0:tm - BLK, :], tail], axis=0)
        dkv = dkv.astype(BF16)
        dzu = dzu_ref[...]
        dzv = dzv_ref[...]
        dz_ref[:, 0:512] = dq
        dz_ref[:, 512:768] = dkv
        dz_ref[:, 768:1280] = dzu
        dz_ref[:, 1280:1792] = dzv
        da = (_dot(dq, w_ref[0:512, :]) + _dot(dkv, w_ref[512:768, :]) + _dot(dzu, w_ref[768:1280, :])
              + _dot(dzv, w_ref[1280:1792, :]))
        dx, dg = _rms_bwd(h_ref[...], g_ref[...], da)
        dh_ref[...] = d_ref[...] + dx
        dg_ref[...] += dg

    return pl.pallas_call(
        body, name="in_proj_bwd", grid=(t // tm,),
        in_specs=[_row_spec(tm, ATTN_W), _row_spec(tm, 2 * KV_W),
                  pl.BlockSpec((BLK, 2 * KV_W), lambda i: (jnp.minimum(i + 1, steps - 1), 0)), _row_spec(tm, GM_W),
                  _row_spec(tm, GM_W), _row_spec(tm, D_MODEL), _row_spec(tm, D_MODEL), _vec_spec(D_MODEL),
                  _seg_spec(ROWS_C, D_MODEL, layer)],
        out_specs=[_row_spec(tm, D_IN), _row_spec(tm, D_MODEL), _vec_spec(D_MODEL)],
        out_shape=[jax.ShapeDtypeStruct((t, D_IN), BF16), jax.ShapeDtypeStruct((t, D_MODEL), F32),
                   jax.ShapeDtypeStruct((1, D_MODEL), F32)],
        compiler_params=_params("arbitrary"),
    )(dq, dkv, dkf, dzu, dzv, h, dres, g, wc)


def _weight_grad(a, b, buf, seg):
    t, m = a.shape
    n = b.shape[1]
    assert buf.shape[0] % m == 0 and buf.shape[1] == n
    tm = _tile(t, WGRAD_TOKENS)
    steps = t // tm
    half = m // 2

    def body(a_ref, b_ref, buf_ref, o_ref, acc_ref):
        i = pl.program_id(0)
        _zero_at(i == 0, acc_ref)
        b16 = b_ref[...].astype(BF16)
        for rows in (slice(0, half), slice(half, m)):
            acc_ref[rows, :] += _dot_tn(a_ref[:, rows], b16)

        @pl.when(i == steps - 1)
        def _():
            o_ref[...] = acc_ref[...].astype(o_ref.dtype)

    return pl.pallas_call(
        body, name="weight_grad", grid=(steps,),
        in_specs=[_row_spec(tm, m), _row_spec(tm, n), pl.BlockSpec(memory_space=pl.ANY)],
        out_specs=pl.BlockSpec((m, n), lambda i: (seg, 0)),
        out_shape=jax.ShapeDtypeStruct(buf.shape, buf.dtype),
        scratch_shapes=[pltpu.VMEM((m, n), F32)],
        input_output_aliases={2: 0},
        compiler_params=_params("arbitrary"),
    )(a, b, buf)


def _rows8(rows):
    return [jnp.pad(r, ((0, 7), (0, 0))) for r in rows]


def _small_pack_gating(d):
    rows = [jnp.concatenate([d["gm_ln_g"], d["gm_ln_b"]], axis=1), d["gm_bs"].reshape(1, 1024)]
    return jnp.concatenate(_rows8(rows) + [d["gm_ws"].reshape(128, 1024)], axis=0)


def _small_pack_rest(d):
    rows = [d["ln_mix_pre"], d["ln_mix_post"], d["ln_ffn_pre"], d["ln_ffn_post"], d["ln_ple_gate"],
            jnp.concatenate([d["g_attn_out"], d["g_gm_out"]], axis=1),
            jnp.pad(d["attn_sinks"].reshape(1, 8), ((0, 0), (0, 1016)))]
    return jnp.concatenate(_rows8(rows), axis=0)


def _small_unpack(g, s):
    return {
        "gm_ln_g": g[:, 0, :512], "gm_ln_b": g[:, 0, 512:], "gm_bs": g[:, 8].reshape(DEPTH, 8, 128),
        "gm_ws": g[:, 16:144].reshape(DEPTH, 8, 128, 128),
        "ln_mix_pre": s[:, 0], "ln_mix_post": s[:, 8], "ln_ffn_pre": s[:, 16], "ln_ffn_post": s[:, 24],
        "ln_ple_gate": s[:, 32], "g_attn_out": s[:, 40, :512], "g_gm_out": s[:, 40, 512:], "attn_sinks": s[:, 48, :8],
    }


def _row(v):
    return v.reshape(1, -1)


def _layer_fwd(h, p, sp, l, weights):
    tril = jnp.tril(jnp.ones((BLK, BLK), bool))
    wtril = jnp.where(tril[None], sp["gm_ws"][l], 0.0).astype(BF16)
    bsx = jnp.repeat(sp["gm_bs"][l].T, HEAD_DIM, axis=1)
    a, q, kv, zu, zv = _in_proj(h, _row(sp["ln_mix_pre"][l]), weights("c", h), 0)
    attn = _attn_fwd(q, kv, sp["attn_sinks"][l])
    gm = _gm_fwd(zu, zv, _row(sp["gm_ln_g"][l]), _row(sp["gm_ln_b"][l]), wtril, bsx)
    wb = weights("b", gm)
    heads, mix, h1 = _out_proj(attn, gm, h, _row(sp["g_attn_out"][l]), _row(sp["g_gm_out"][l]),
                               _row(sp["ln_mix_post"][l]), wb, 0)
    wa = weights("a", h1)
    f, gpre, up, act, fo, h2, hn, gate, h3 = _ffn_fwd(
        h1, p, _row(sp["ln_ffn_pre"][l]), _row(sp["ln_ffn_post"][l]), _row(sp["ln_ple_gate"][l]), wa, wb,
        weights("p", gm), 0)
    saved = dict(h=h, a=a, q=q, kv=kv, zu=zu, zv=zv, attn=attn, gm=gm, heads=heads, mix=mix, h1=h1, f=f,
                 gpre=gpre, up=up, act=act, fo=fo, h2=h2, hn=hn, gate=gate, wtril=wtril, bsx=bsx)
    return h3, saved


def _layer_bwd_upper(dh, s, p, sp, l, wa, wb, wp, after):
    d = {}
    dgl, dpe, dfo, dgp, dup, dh1, d["ln_ple_gate"], d["ln_ffn_post"], d["ln_ffn_pre"] = _ffn_bwd(
        dh, s["h2"], s["gate"], p, s["fo"], s["gpre"], s["up"], s["h1"], _row(sp["ln_ple_gate"][l]),
        _row(sp["ln_ffn_post"][l]), _row(sp["ln_ffn_pre"][l]), wa, wb, wp, 0, after)
    gb = _weight_grad(s["hn"], dgl, lax.empty((2 * D_MODEL, D_MODEL), BF16), 1)
    gp = _weight_grad(dpe, p, lax.empty((D_MODEL, PLE_DIM), BF16), 0)
    ga = _weight_grad(s["act"], dfo, lax.empty((3 * D_FF, D_MODEL), BF16), 2)
    ga = _weight_grad(dgp, s["f"], ga, 0)
    ga = _weight_grad(dup, s["f"], ga, 1)
    return (dh1, d), ga, gp, gb


def _layer_bwd_middle(carry, s, sp, l, wb, gb, after):
    dh1, d = carry
    dmix, dattn, dgm, d["ln_mix_post"], d["g_attn_out"], d["g_gm_out"] = _out_proj_bwd(
        dh1, s["mix"], s["attn"], s["gm"], _row(sp["ln_mix_post"][l]), _row(sp["g_attn_out"][l]),
        _row(sp["g_gm_out"][l]), wb, 0, after)
    gb = _weight_grad(s["heads"], dmix, gb, 0)
    dzu, dzv, d["gm_ws"], dbs, d["gm_ln_g"], d["gm_ln_b"] = _gm_bwd(
        dgm, s["zu"], s["zv"], _row(sp["gm_ln_g"][l]), _row(sp["gm_ln_b"][l]), s["wtril"], s["bsx"])
    d["gm_bs"] = dbs[:, :8].T
    return (dh1, dattn, dzu, dzv, d), gb, _small_pack_gating(d)


def _layer_bwd_lower(carry, s, sp, l, wc, after):
    dh1, dattn, dzu, dzv, d = carry
    dq, dkv, dkf, dsink = _attn_bwd(s["q"], s["kv"], dattn, sp["attn_sinks"][l], after)
    d["attn_sinks"] = dsink[:, 0]
    dz, dh, d["ln_mix_pre"] = _in_proj_bwd(dq, dkv, dkf, dzu, dzv, s["h"], dh1, _row(sp["ln_mix_pre"][l]), wc, 0)
    gc = _weight_grad(dz, s["a"], lax.empty((D_IN, D_MODEL), BF16), 0)
    return dh, gc, _small_pack_rest(d)


ANY = pl.BlockSpec(memory_space=pl.ANY)


def _place():
    x, y, c = lax.axis_index("x"), lax.axis_index("y"), lax.axis_index("c")
    chips = [(1 - x, y), (x, 1 - y), (1 - x, 1 - y)]
    return x, y, c, chips


def _all_gather(shards):
    n = len(shards)

    def body(*refs):
        ins, outs = refs[:n], refs[n:2 * n]
        send_sems, recv_sems, local_sems = refs[2 * n:]
        x, y, c, chips = _place()
        me, sibling = (x, y, c), (x, y, 1 - c)

        def block(k, px, py, pc):
            return outs[k].at[:, pl.ds(4 * px + 2 * py + pc, 1)]

        def copy(k, j, who, to, src=None):
            return pltpu.make_async_remote_copy(
                src_ref=block(k, *who) if src is None else src, dst_ref=block(k, *who),
                send_sem=send_sems.at[7 * k + j], recv_sem=recv_sems.at[7 * k + j],
                device_id=to, device_id_type=MESH)

        mine = [pltpu.make_async_copy(ins[k], block(k, *me), local_sems.at[k]) for k in range(n)]
        for cp in mine:
            cp.start()
        first = []
        for k in range(n):
            first.append(copy(k, 0, me, sibling, src=ins[k]))
            first += [copy(k, 1 + j, me, (*chip, c), src=ins[k]) for j, chip in enumerate(chips)]
        for cp in first:
            cp.start()
        passed = []
        for j, chip in enumerate(chips):
            for k in range(n):
                copy(k, 1 + j, (*chip, c), me).wait_recv()
                cp = copy(k, 4 + j, (*chip, c), sibling)
                cp.start()
                passed.append(cp)
        for k in range(n):
            copy(k, 0, sibling, me).wait_recv()
            for j, chip in enumerate(chips):
                copy(k, 4 + j, (*chip, 1 - c), me).wait_recv()
        for cp in first + passed:
            cp.wait_send()
        for cp in mine:
            cp.wait()

    return pl.pallas_call(
        body, name="all_gather_weights",
        in_specs=[ANY] * n, out_specs=[ANY] * n,
        out_shape=[jax.ShapeDtypeStruct((s.shape[0], N_DEV) + s.shape[2:], s.dtype) for s in shards],
        scratch_shapes=[pltpu.SemaphoreType.DMA((7 * n,)), pltpu.SemaphoreType.DMA((7 * n,)),
                        pltpu.SemaphoreType.DMA((n,))],
        compiler_params=pltpu.CompilerParams(has_side_effects=True),
    )(*shards)


def _sibling_exchange(bufs, small):
    n = len(bufs)

    def body(*refs):
        ins, small_ref = refs[:n], refs[n]
        outs, both_ref = refs[n + 1:2 * n + 1], refs[2 * n + 1]
        send_sems, recv_sems, local_sem = refs[2 * n + 2:]
        x, y, c, _ = _place()
        sibling = (x, y, 1 - c)
        mine = pltpu.make_async_copy(small_ref, both_ref.at[c], local_sem)
        mine.start()
        copies = [pltpu.make_async_remote_copy(
            src_ref=ins[k].at[:, :, pl.ds(1 - c, 1)], dst_ref=outs[k], send_sem=send_sems.at[k],
            recv_sem=recv_sems.at[k], device_id=sibling, device_id_type=MESH) for k in range(n)]
        copies.append(pltpu.make_async_remote_copy(
            src_ref=small_ref, dst_ref=both_ref.at[c], send_sem=send_sems.at[n], recv_sem=recv_sems.at[n],
            device_id=sibling, device_id_type=MESH))
        for cp in copies:
            cp.start()
        for k in range(n):
            copies[k].wait_recv()
        pltpu.make_async_remote_copy(
            src_ref=small_ref, dst_ref=both_ref.at[1 - c], send_sem=send_sems.at[n], recv_sem=recv_sems.at[n],
            device_id=sibling, device_id_type=MESH).wait_recv()
        for cp in copies:
            cp.wait_send()
        mine.wait()

    return pl.pallas_call(
        body, name="sibling_exchange",
        in_specs=[ANY] * (n + 1), out_specs=[ANY] * (n + 1),
        out_shape=[jax.ShapeDtypeStruct(b.shape[:2] + (1,) + b.shape[3:], b.dtype) for b in bufs]
        + [jax.ShapeDtypeStruct((2,) + small.shape, small.dtype)],
        scratch_shapes=[pltpu.SemaphoreType.DMA((n + 1,)), pltpu.SemaphoreType.DMA((n + 1,)), pltpu.SemaphoreType.DMA],
        compiler_params=pltpu.CompilerParams(has_side_effects=True),
    )(*bufs, small)


def _chip_exchange(sends, small):
    n = len(sends)

    def body(*refs):
        ins, small_ref = refs[:n], refs[n]
        outs, all_ref = refs[n + 1:2 * n + 1], refs[2 * n + 1]
        send_sems, recv_sems, local_sem = refs[2 * n + 2:]
        x, y, c, chips = _place()
        mine = pltpu.make_async_copy(small_ref, all_ref.at[2 * x + y], local_sem)
        mine.start()
        copies = []
        for j, chip in enumerate(chips):
            for k in range(n):
                copies.append(pltpu.make_async_remote_copy(
                    src_ref=ins[k].at[j], dst_ref=outs[k].at[j], send_sem=send_sems.at[3 * k + j],
                    recv_sem=recv_sems.at[3 * k + j], device_id=(*chip, c), device_id_type=MESH))
            copies.append(pltpu.make_async_remote_copy(
                src_ref=small_ref, dst_ref=all_ref.at[2 * x + y], send_sem=send_sems.at[3 * n + j],
                recv_sem=recv_sems.at[3 * n + j], device_id=(*chip, c), device_id_type=MESH))
        for cp in copies:
            cp.start()
        for j, (px, py) in enumerate(chips):
            for k in range(n):
                copies[j * (n + 1) + k].wait_recv()
            pltpu.make_async_remote_copy(
                src_ref=small_ref, dst_ref=all_ref.at[2 * px + py], send_sem=send_sems.at[3 * n + j],
                recv_sem=recv_sems.at[3 * n + j], device_id=(px, py, c), device_id_type=MESH).wait_recv()
        for cp in copies:
            cp.wait_send()
        mine.wait()

    return pl.pallas_call(
        body, name="chip_exchange",
        in_specs=[ANY] * (n + 1), out_specs=[ANY] * (n + 1),
        out_shape=[jax.ShapeDtypeStruct(s.shape, s.dtype) for s in sends]
        + [jax.ShapeDtypeStruct((4,) + small.shape, small.dtype)],
        scratch_shapes=[pltpu.SemaphoreType.DMA((3 * n + 3,)), pltpu.SemaphoreType.DMA((3 * n + 3,)),
                        pltpu.SemaphoreType.DMA],
        compiler_params=pltpu.CompilerParams(has_side_effects=True),
    )(*sends, small)


def _pair_add(buf, got, chip_ids, dtype):
    nseg, _, _, rows, cols = buf.shape
    nr = chip_ids.shape[0] - 1

    def body(ids_ref, a_ref, b_ref, o_ref):
        o_ref[...] = (a_ref[...] + b_ref[...]).astype(dtype)

    return pl.pallas_call(
        body, name="pair_add",
        grid_spec=pltpu.PrefetchScalarGridSpec(
            num_scalar_prefetch=1, grid=(nr, nseg),
            in_specs=[pl.BlockSpec((None, None, None, rows, cols), lambda r, s, ids: (s, ids[r], ids[nr], 0, 0)),
                      pl.BlockSpec((None, None, None, rows, cols), lambda r, s, ids: (s, ids[r], 0, 0, 0))],
            out_specs=pl.BlockSpec((None, None, rows, cols), lambda r, s, ids: (r, s, 0, 0))),
        out_shape=jax.ShapeDtypeStruct((nr, nseg, rows, cols), dtype),
        compiler_params=_params("parallel", "parallel"),
    )(chip_ids, buf, got)


def _sum_slots(z, tr):
    n, rows, cols = z.shape

    def body(z_ref, o_ref):
        s = z_ref[0]
        for k in range(1, n):
            s = s + z_ref[k]
        o_ref[...] = s

    return pl.pallas_call(
        body, name="sum_slots", grid=(rows // tr,),
        in_specs=[pl.BlockSpec((n, tr, cols), lambda i: (0, i, 0))],
        out_specs=pl.BlockSpec((tr, cols), lambda i: (i, 0)),
        out_shape=jax.ShapeDtypeStruct((rows, cols), F32),
        compiler_params=_params("parallel"),
    )(z)


def _final_sum(own, got):
    _, nseg, rows, cols = own.shape

    def body(a_ref, b_ref, o_ref):
        s = a_ref[0]
        for k in range(3):
            s = s + b_ref[k].astype(F32)
        o_ref[...] = s

    return pl.pallas_call(
        body, name="final_sum", grid=(nseg,),
        in_specs=[pl.BlockSpec((1, None, rows, cols), lambda i: (0, i, 0, 0)),
                  pl.BlockSpec((3, None, rows, cols), lambda i: (0, i, 0, 0))],
        out_specs=pl.BlockSpec((None, rows, cols), lambda i: (i, 0, 0)),
        out_shape=jax.ShapeDtypeStruct((nseg, rows, cols), F32),
        compiler_params=_params("parallel"),
    )(own, got)


HBM = pl.BlockSpec(memory_space=pltpu.HBM)
SEM = pl.BlockSpec(memory_space=pltpu.SEMAPHORE)
N_PEERS = N_DEV - 1


def _peers():
    x, y, c = lax.axis_index("x"), lax.axis_index("y"), lax.axis_index("c")
    peers = []
    for r in range(1, N_DEV):
        px = 1 - x if r & 4 else x
        py = 1 - y if r & 2 else y
        pc = 1 - c if r & 1 else c
        peers.append(((px, py, pc), 4 * px + 2 * py + pc))
    return 4 * x + 2 * y + c, peers


GATHER, SCATTER, SPREAD = "gather", "scatter", "spread"


def _peer_copy(src, land, send_sems, recv_sems, r, me, peer, peer_slot, mode):
    return pltpu.make_async_remote_copy(
        src_ref=src.at[:, pl.ds(peer_slot, 1)] if mode == SCATTER else src,
        dst_ref=land.at[:, pl.ds(me, 1)] if mode == GATHER else land.at[:, pl.ds(r - 1, 1)],
        send_sem=send_sems.at[r - 1], recv_sem=recv_sems.at[r - 1], device_id=peer, device_id_type=MESH)


def _peer_arrival(src, land, send_sems, recv_sems, r, me, peer, peer_slot, mode):
    return pltpu.make_async_remote_copy(
        src_ref=src.at[:, pl.ds(me, 1)] if mode == SCATTER else src,
        dst_ref=land.at[:, pl.ds(peer_slot, 1)] if mode == GATHER else land.at[:, pl.ds(r - 1, 1)],
        send_sem=send_sems.at[r - 1], recv_sem=recv_sems.at[r - 1], device_id=peer, device_id_type=MESH)


def _send_start(name, srcs, lands, modes):
    n = len(srcs)

    def body(*refs):
        src_refs, land_refs = refs[:n], refs[n:2 * n]
        outs = refs[2 * n:]
        send_sems, recv_sems, token = outs[2 * n:3 * n], outs[3 * n:4 * n], outs[4 * n]
        me, peers = _peers()
        for k in range(n):
            for r, (peer, slot) in enumerate(peers, 1):
                _peer_copy(src_refs[k], land_refs[k], send_sems[k], recv_sems[k], r, me, peer, slot, modes[k]).start()
        token[...] = jnp.zeros_like(token)

    hbm = lambda a: pltpu.HBM(a.shape, a.dtype)
    sems = [pltpu.SemaphoreType.DMA((N_PEERS,))] * (2 * n)
    outs = pl.pallas_call(
        body, name=name, in_specs=[HBM] * (2 * n),
        out_specs=[HBM] * (2 * n) + [SEM] * (2 * n) + [pl.BlockSpec(memory_space=pltpu.VMEM)],
        out_shape=[hbm(a) for a in srcs] + [hbm(a) for a in lands] + sems + [jax.ShapeDtypeStruct((8, 128), F32)],
        input_output_aliases={k: k for k in range(2 * n)},
        compiler_params=pltpu.CompilerParams(has_side_effects=pltpu.SideEffectType.DATAFLOW_SIDE_EFFECTING),
    )(*[pltpu.with_memory_space_constraint(a, pltpu.HBM) for a in list(srcs) + list(lands)])
    return dict(srcs=outs[:n], lands=outs[n:2 * n], send=outs[2 * n:3 * n], recv=outs[3 * n:4 * n],
                modes=list(modes)), outs[4 * n]


def _send_wait(name, sent, ks, after):
    n = len(ks)
    srcs = [sent["srcs"][k] for k in ks]
    lands = [sent["lands"][k] for k in ks]
    modes = [sent["modes"][k] for k in ks]

    def body(*refs):
        src_refs, land_refs = refs[:n], refs[n:2 * n]
        send_sems, recv_sems = refs[2 * n:3 * n], refs[3 * n:4 * n]
        me, peers = _peers()
        for k in range(n):
            for r, (peer, slot) in enumerate(peers, 1):
                args = (src_refs[k], land_refs[k], send_sems[k], recv_sems[k], r, me, peer, slot, modes[k])
                _peer_copy(*args).wait_send()
                _peer_arrival(*args).wait_recv()

    hbm = lambda a: pltpu.HBM(a.shape, a.dtype)
    outs = pl.pallas_call(
        body, name=name, in_specs=[HBM] * (2 * n) + [SEM] * (2 * n) + [ANY],
        out_specs=[HBM] * (2 * n), out_shape=[hbm(a) for a in srcs] + [hbm(a) for a in lands],
        input_output_aliases={k: k for k in range(2 * n)},
        compiler_params=pltpu.CompilerParams(has_side_effects=pltpu.SideEffectType.DATAFLOW_SIDE_EFFECTING),
    )(*srcs, *lands, *[sent["send"][k] for k in ks], *[sent["recv"][k] for k in ks], after)
    return outs[n:], outs[:n]


def _sum_blocks(own, land, ids):
    nseg, _, rows, cols = land.shape

    def body(ids_ref, own_ref, land_ref, o_ref):
        me = ids_ref[1]
        total = None
        for j in range(N_DEV):
            slot = jnp.maximum(jnp.bitwise_xor(me, j) - 1, 0)
            term = jnp.where(me == j, own_ref[...], land_ref[slot]).astype(F32)
            total = term if total is None else total + term
        o_ref[...] = total

    return pl.pallas_call(
        body, name="sum_blocks",
        grid_spec=pltpu.PrefetchScalarGridSpec(
            num_scalar_prefetch=1, grid=(nseg,),
            in_specs=[pl.BlockSpec((None, None, rows, cols), lambda s, ids: (s, ids[0], 0, 0)),
                      pl.BlockSpec((None, N_PEERS, rows, cols), lambda s, ids: (s, 0, 0, 0))],
            out_specs=pl.BlockSpec((None, rows, cols), lambda s, ids: (s, 0, 0))),
        out_shape=jax.ShapeDtypeStruct((nseg, rows, cols), F32),
        compiler_params=_params("parallel"),
    )(ids, own, land)


def _adamw(w, g, m, v):
    shape = w.shape
    cols = shape[-1]
    rows = w.size // cols
    tr = rows
    for cand in (512, 256, 128, 64, 32, 16, 8):
        if rows % cand == 0:
            tr = cand
            break
    c1 = 1.0 / (1.0 - ADAM_B1 ** ADAM_STEP)
    c2 = 1.0 / (1.0 - ADAM_B2 ** ADAM_STEP)

    def body(w_ref, g_ref, m_ref, v_ref, d_ref, nm_ref, nv_ref):
        g = g_ref[...]
        m = ADAM_B1 * m_ref[...] + (1.0 - ADAM_B1) * g
        v = ADAM_B2 * v_ref[...] + (1.0 - ADAM_B2) * (g * g)
        nm_ref[...] = m
        nv_ref[...] = v
        d_ref[...] = -ADAM_LR * ((m * c1) / (jnp.sqrt(v * c2) + ADAM_EPS) + ADAM_WD * w_ref[...])

    spec = pl.BlockSpec((tr, cols), lambda i: (i, 0))
    outs = pl.pallas_call(
        body, name="adamw", grid=(rows // tr,),
        in_specs=[spec] * 4, out_specs=[spec] * 3,
        out_shape=[jax.ShapeDtypeStruct((rows, cols), F32)] * 3,
        compiler_params=_params("parallel"),
    )(*[a.reshape(rows, cols) for a in (w, g, m, v)])
    return [o.reshape(shape) for o in outs]


SMALL = ("ln_mix_pre", "attn_sinks", "gm_ln_g", "gm_ln_b", "gm_ws", "gm_bs", "g_attn_out", "g_gm_out",
         "ln_mix_post", "ln_ffn_pre", "ln_ffn_post", "ln_ple_gate")
WEIGHTS = ("ln_mix_pre", "w_in", "attn_sinks", "gm_ln_g", "gm_ln_b", "gm_ws", "gm_bs", "g_attn_out", "g_gm_out",
           "w_out", "ln_mix_post", "ln_ffn_pre", "w_ffn_gate", "w_ffn_up", "w_ffn_down", "ln_ffn_post", "w_ple",
           "ln_ple_gate", "w_ple_gate")


def _pack_shards(w, l):
    sa = jnp.stack([w["w_ffn_gate"][l].T, w["w_ffn_up"][l].T, w["w_ffn_down"][l]])[:, None]
    sb = jnp.stack([w["w_out"][l], w["w_ple_gate"][l]])[:, None]
    return [w["w_in"][l].T[None, None].astype(BF16), sb.astype(BF16), w["w_ple"][l].T[None, None].astype(BF16),
            sa.astype(BF16)]


def _unpack_grads(rc, rb, rp, ra):
    return {"w_in": rc[0].T, "w_out": rb[0], "w_ple_gate": rb[1], "w_ple": rp[0].T,
            "w_ffn_gate": ra[0].T, "w_ffn_up": ra[1].T, "w_ffn_down": ra[2]}


def kernel(x, p, ln_mix_pre, w_in, attn_sinks, gm_ln_g, gm_ln_b, gm_ws, gm_bs, g_attn_out, g_gm_out, w_out, ln_mix_post, ln_ffn_pre, w_ffn_gate, w_ffn_up, w_ffn_down, ln_ffn_post, w_ple, ln_ple_gate, w_ple_gate, loss_target, m_ln_mix_pre, m_w_in, m_attn_sinks, m_gm_ln_g, m_gm_ln_b, m_gm_ws, m_gm_bs, m_g_attn_out, m_g_gm_out, m_w_out, m_ln_mix_post, m_ln_ffn_pre, m_w_ffn_gate, m_w_ffn_up, m_w_ffn_down, m_ln_ffn_post, m_w_ple, m_ln_ple_gate, m_w_ple_gate, v_ln_mix_pre, v_w_in, v_attn_sinks, v_gm_ln_g, v_gm_ln_b, v_gm_ws, v_gm_bs, v_g_attn_out, v_g_gm_out, v_w_out, v_ln_mix_post, v_ln_ffn_pre, v_w_ffn_gate, v_w_ffn_up, v_w_ffn_down, v_ln_ffn_post, v_w_ple, v_ln_ple_gate, v_w_ple_gate):
    given = dict(locals())
    w = {n: given[n] for n in WEIGHTS}
    sp = {n: w[n] for n in SMALL}
    kinds = ("c", "b", "p", "a")

    me, _ = _peers()
    shards = [s for l in range(DEPTH) for s in _pack_shards(w, l)]
    lands = [lax.dynamic_update_slice(lax.empty((s.shape[0], N_DEV) + s.shape[2:], BF16), s, (0, me, 0, 0))
             for s in shards]
    gather, token = _send_start("gather_start", shards, lands, [GATHER] * len(shards))
    layer_weights = [{} for _ in range(DEPTH)]

    def weights_of(l):
        def get(kind, after):
            have = layer_weights[l]
            if kind not in have:
                if l < 2:
                    group = {"c": ("c",), "b": ("b", "p"), "p": ("b", "p"), "a": ("a",)}[kind]
                    after = token if (l == 0 and kind == "c") else after
                else:
                    group = kinds
                got, _ = _send_wait(f"gather_wait_{l}{group[0]}", gather, [4 * l + kinds.index(k) for k in group], after)
                for k, g in zip(group, got):
                    have[k] = g.reshape(-1, g.shape[-1])
            return have[kind]
        return get

    h = x[0]
    saved = []
    for l in range(DEPTH):
        h, s = _layer_fwd(h, p[l, 0], sp, l, weights_of(l))
        saved.append(s)
    dh, sq = _loss_head(h, loss_target[0])

    reduces = []
    after = token
    view = lambda g, rows: g.reshape(-1, N_DEV, rows, g.shape[-1])
    pack16 = lambda s: s.astype(BF16)[None, None]
    landing = lambda a: lax.empty((a.shape[0], N_PEERS) + a.shape[2:], BF16)

    def send(name, bufs, modes):
        return _send_start(name, bufs, [landing(a) for a in bufs], modes)

    for l in reversed(range(DEPTH)):
        lw = layer_weights[l]
        carry, ga, gp, gb = _layer_bwd_upper(dh, saved[l], p[l, 0], sp, l, lw["a"], lw["b"], lw["p"], after)
        sent1, after = send(f"reduce_start_{l}a", [view(ga, ROWS_A), view(gp, ROWS_B)], [SCATTER, SCATTER])
        carry, gb, gating = _layer_bwd_middle(carry, saved[l], sp, l, lw["b"], gb, after)
        sent2, after = send(f"reduce_start_{l}b", [view(gb, ROWS_B), pack16(gating)], [SCATTER, SPREAD])
        dh, gc, rest = _layer_bwd_lower(carry, saved[l], sp, l, lw["c"], after)
        sent3, after = send(f"reduce_start_{l}c", [view(gc, ROWS_C), pack16(rest)], [SCATTER, SPREAD])
        reduces.append((l, sent1, sent2, sent3))

    mine = jnp.stack([me, me]).astype(jnp.int32)
    whole = jnp.stack([jnp.zeros_like(me), me]).astype(jnp.int32)
    sums = {k: [None] * DEPTH for k in ("a", "p", "b", "c", "gating", "rest")}
    last = {}
    for l, sent1, sent2, sent3 in reduces:
        (la, lp), (ga, gp) = _send_wait(f"reduce_wait_{l}a", sent1, [0, 1], dh)
        (lb, lg), (gb, gg) = _send_wait(f"reduce_wait_{l}b", sent2, [0, 1], dh)
        sums["a"][l], sums["p"][l] = _sum_blocks(ga, la, mine), _sum_blocks(gp, lp, mine)
        sums["b"][l], sums["gating"][l] = _sum_blocks(gb, lb, mine), _sum_blocks(gg, lg, whole)[0]
        if l > 0:
            (lc, lr), (gc, gr) = _send_wait(f"reduce_wait_{l}c", sent3, [0, 1], dh)
            sums["c"][l], sums["rest"][l] = _sum_blocks(gc, lc, mine), _sum_blocks(gr, lr, whole)[0]
        else:
            last = sent3
    grad_x = dh
    loss = lax.psum(sq[0, 0] * (0.5 / D_MODEL), AXES)
    grads, delta, new_m, new_v = {}, {}, {}, {}

    def update(names):
        for n in names:
            delta[n], new_m[n], new_v[n] = _adamw(w[n], grads[n], given["m_" + n], given["v_" + n])

    stack = lambda f, xs: jnp.stack([f(x) for x in xs])
    gating = jnp.stack(sums["gating"])
    grads.update({
        "w_ffn_gate": stack(lambda r: r[0].T, sums["a"]), "w_ffn_up": stack(lambda r: r[1].T, sums["a"]),
        "w_ffn_down": stack(lambda r: r[2], sums["a"]), "w_ple": stack(lambda r: r[0].T, sums["p"]),
        "w_out": stack(lambda r: r[0], sums["b"]), "w_ple_gate": stack(lambda r: r[1], sums["b"]),
        "gm_ln_g": gating[:, 0, :512], "gm_ln_b": gating[:, 0, 512:], "gm_bs": gating[:, 8].reshape(DEPTH, 8, 128),
        "gm_ws": gating[:, 16:144].reshape(DEPTH, 8, 128, 128)})
    early = tuple(grads)
    update(early)
    (lc, lr), (gc, gr) = _send_wait("reduce_wait_0c", last, [0, 1], delta["w_ffn_down"])
    sums["c"][0], sums["rest"][0] = _sum_blocks(gc, lc, mine), _sum_blocks(gr, lr, whole)[0]
    rest = jnp.stack(sums["rest"])
    grads.update({
        "w_in": stack(lambda r: r[0].T, sums["c"]),
        "ln_mix_pre": rest[:, 0], "ln_mix_post": rest[:, 8], "ln_ffn_pre": rest[:, 16], "ln_ffn_post": rest[:, 24],
        "ln_ple_gate": rest[:, 32], "g_attn_out": rest[:, 40, :512], "g_gm_out": rest[:, 40, 512:],
        "attn_sinks": rest[:, 48, :8]})
    update([n for n in grads if n not in early])
    return (loss, grad_x[None], *[grads[n] for n in WEIGHTS], *[delta[n] for n in WEIGHTS],
            *[new_m[n] for n in WEIGHTS], *[new_v[n] for n in WEIGHTS])
```

```python
import math

import jax
import jax.numpy as jnp
from jax import lax
from jax.experimental import pallas as pl
from jax.experimental.pallas import tpu as pltpu

F32 = jnp.float32
BF16 = jnp.bfloat16
MESH = pl.DeviceIdType.MESH
AXES = ("x", "y", "c")

D_MODEL = 1024
DEPTH = 4
N_DEV = 8
HEAD_DIM = 64
ATTN_W = 512
KV_W = 128
GM_W = 512
D_IN = 1792
D_FF = 2816
PLE_DIM = 256
BLK = 128
FF_CHUNK = 256
WGRAD_TOKENS = 1024
NORM_EPS = 1e-6
NEG_BIG = -1e30
ATTN_SCALE = HEAD_DIM ** -0.5

ADAM_LR = 0.001
ADAM_B1 = 0.9
ADAM_B2 = 0.999
ADAM_EPS = 1e-08
ADAM_WD = 0.01
ADAM_STEP = 10

ROWS_A = D_FF // N_DEV
ROWS_B = D_MODEL // N_DEV
ROWS_C = D_IN // N_DEV
GATING_ROWS = 144
REST_ROWS = 56
SMALL_ROWS = 200

VMEM_LIMIT = 56 * 2 ** 20


def _params(*sem):
    return pltpu.CompilerParams(dimension_semantics=sem, vmem_limit_bytes=VMEM_LIMIT)


def _dot(a, b):
    return jnp.dot(a, b, preferred_element_type=F32)


def _dot_nt(a, b):
    return lax.dot_general(a, b, (((1,), (1,)), ((), ())), preferred_element_type=F32)


def _dot_tn(a, b):
    return lax.dot_general(a, b, (((0,), (0,)), ((), ())), preferred_element_type=F32)


def _rms_fwd(x, g):
    r = lax.rsqrt(jnp.mean(x * x, axis=-1, keepdims=True) + NORM_EPS)
    return x * r * g


def _rms_bwd(x, g, dy):
    r = lax.rsqrt(jnp.mean(x * x, axis=-1, keepdims=True) + NORM_EPS)
    xh = x * r
    dg = jnp.sum(dy * xh, axis=0, keepdims=True)
    dxh = dy * g
    dx = r * (dxh - xh * jnp.mean(dxh * xh, axis=-1, keepdims=True))
    return dx, dg


_GELU_C = math.sqrt(2.0 / math.pi)


def _gelu(x):
    t = jnp.tanh(_GELU_C * (x + 0.044715 * (x * x * x)))
    return 0.5 * x * (1.0 + t)


def _gelu_grad(x):
    x2 = x * x
    t = jnp.tanh(_GELU_C * (x + 0.044715 * (x2 * x)))
    return 0.5 * (1.0 + t) + 0.5 * x * (1.0 - t * t) * (_GELU_C * (1.0 + 3.0 * 0.044715 * x2))


def _sigmoid(x):
    return 1.0 / (1.0 + jnp.exp(-x))


def _row_spec(tm, n):
    return pl.BlockSpec((tm, n), lambda i: (i, 0))


def _layer_row_spec(tm, n, l):
    return pl.BlockSpec((None, tm, n), lambda i: (l, i, 0))


def _vec_spec(n):
    return pl.BlockSpec((1, n), lambda i: (0, 0))


def _seg_spec(rows, cols, seg):
    return pl.BlockSpec((N_DEV * rows, cols), lambda i: (seg, 0), pipeline_mode=pl.Buffered(1))


def _zero_at(first, *refs):
    @pl.when(first)
    def _():
        for r in refs:
            r[...] = jnp.zeros(r.shape, r.dtype)


def _tile(t, want):
    return min(t, want)


def _in_proj(h, g, wc, layer):
    t = h.shape[0]
    tm = _tile(t, 512)

    def body(h_ref, g_ref, w_ref, a_ref, q_ref, kv_ref, zu_ref, zv_ref):
        a = _rms_fwd(h_ref[...], g_ref[...]).astype(BF16)
        a_ref[...] = a
        q_ref[...] = _dot_nt(a, w_ref[0:512, :]).astype(BF16)
        kv_ref[...] = _dot_nt(a, w_ref[512:768, :]).astype(BF16)
        zu_ref[...] = _dot_nt(a, w_ref[768:1280, :])
        zv_ref[...] = _dot_nt(a, w_ref[1280:1792, :])

    return pl.pallas_call(
        body, name="in_proj", grid=(t // tm,),
        in_specs=[_row_spec(tm, D_MODEL), _vec_spec(D_MODEL), _seg_spec(ROWS_C, D_MODEL, layer)],
        out_specs=[_row_spec(tm, D_MODEL), _row_spec(tm, ATTN_W), _row_spec(tm, 2 * KV_W),
                   _row_spec(tm, GM_W), _row_spec(tm, GM_W)],
        out_shape=[jax.ShapeDtypeStruct((t, D_MODEL), BF16), jax.ShapeDtypeStruct((t, ATTN_W), BF16),
                   jax.ShapeDtypeStruct((t, 2 * KV_W), BF16), jax.ShapeDtypeStruct((t, GM_W), F32),
                   jax.ShapeDtypeStruct((t, GM_W), F32)],
        compiler_params=_params("parallel"),
    )(h, g, wc)


def _head_variants(x, low):
    xr = pltpu.roll(x, 64, axis=1)
    zero = jnp.zeros_like(x)
    return {
        (0, 0): jnp.where(low, x, zero).astype(BF16),
        (0, 1): jnp.where(low, zero, xr).astype(BF16),
        (1, 0): jnp.where(low, xr, zero).astype(BF16),
        (1, 1): jnp.where(low, zero, x).astype(BF16),
    }


def _attn_masks(i):
    row = lax.broadcasted_iota(jnp.int32, (BLK, BLK), 0)
    lane = lax.broadcasted_iota(jnp.int32, (BLK, BLK), 1)
    vcur = row >= lane
    dist = jnp.where(vcur, row - lane, row - lane + BLK).astype(F32)
    valid = jnp.logical_or(vcur, i > 0)
    return lane < 64, vcur, dist, valid


def _head_key(h):
    return (h // 4, h % 2)


def _stack_kv(prev, cur, g):
    return jnp.concatenate([prev[(g, 0)], cur[(g, 0)], prev[(g, 1)], cur[(g, 1)]], axis=0)


def _split_cols(p, vcur):
    return [jnp.where(vcur, 0.0, p).astype(BF16), jnp.where(vcur, p, 0.0).astype(BF16)]


def _attn_scores(q_ref, rows, stacked, vcur):
    out = []
    for col in range(4):
        big = _dot_nt(q_ref[rows, col * 128:(col + 1) * 128], stacked[col // 2])
        for half in range(2):
            out.append(jnp.where(vcur, big[:, half * 256 + 128:half * 256 + 256], big[:, half * 256:half * 256 + 128]))
    return out


def _attn_scores_by_head(q_ref, rows, kc, kp, vcur):
    out = []
    for h in range(8):
        qh = q_ref[rows, (h // 2) * 128:(h // 2 + 1) * 128]
        out.append(jnp.where(vcur, _dot_nt(qh, kc[_head_key(h)]), _dot_nt(qh, kp[_head_key(h)])))
    return out


def _attn_probs(s, h, sink, dist, valid):
    s = s * ATTN_SCALE - (2.0 ** -(h + 1)) * dist
    if valid is not None:
        s = jnp.where(valid, s, NEG_BIG)
    m = jnp.maximum(jnp.max(s, axis=1, keepdims=True), sink)
    e = jnp.exp(s - m)
    es = jnp.exp(sink - m)
    inv = 1.0 / (jnp.sum(e, axis=1, keepdims=True) + es)
    return e * inv, es * inv


def _kv_prev_spec(blocks):
    return pl.BlockSpec((BLK, 2 * KV_W), lambda i: (jnp.maximum(i * blocks - 1, 0), 0))


def _kv_variants(kv_ref, rows, low):
    return (_head_variants(kv_ref[rows, 0:128].astype(F32), low), _head_variants(kv_ref[rows, 128:256].astype(F32), low))


def _attn_fwd(q, kv, sinks):
    t = q.shape[0]
    tq = _tile(t, 512)
    blocks = tq // BLK

    def body(sink_ref, q_ref, kvc_ref, kvp_ref, o_ref):
        low, vcur, dist, valid = _attn_masks(pl.program_id(0))
        kp, vp = _kv_variants(kvp_ref, slice(None), low)
        for b in range(blocks):
            rows = slice(b * BLK, (b + 1) * BLK)
            kc, vc = _kv_variants(kvc_ref, rows, low)
            ks = [_stack_kv(kp, kc, g) for g in range(2)]
            vs = [_stack_kv(vp, vc, g) for g in range(2)]
            scores = _attn_scores(q_ref, rows, ks, vcur)
            probs = [_attn_probs(scores[h], h, sink_ref[h], dist, valid if b == 0 else None)[0] for h in range(8)]
            for col in range(4):
                p_col = jnp.concatenate(_split_cols(probs[2 * col], vcur) + _split_cols(probs[2 * col + 1], vcur), axis=1)
                o_ref[rows, col * 128:(col + 1) * 128] = _dot(p_col, vs[col // 2]).astype(BF16)
            kp, vp = kc, vc

    return pl.pallas_call(
        body, name="attn_fwd", grid=(t // tq,),
        in_specs=[pl.BlockSpec(memory_space=pltpu.SMEM), _row_spec(tq, ATTN_W), _row_spec(tq, 2 * KV_W),
                  _kv_prev_spec(blocks)],
        out_specs=_row_spec(tq, ATTN_W),
        out_shape=jax.ShapeDtypeStruct((t, ATTN_W), BF16),
        compiler_params=_params("parallel"),
    )(sinks, q, kv, kv)


def _gm_forward_block(zu, zv, lng, lnb, w_ref, bsx, low):
    gu = _gelu(zu)
    gv = _gelu(zv)
    mu = jnp.mean(gv, axis=-1, keepdims=True)
    xc = gv - mu
    rstd = lax.rsqrt(jnp.mean(xc * xc, axis=-1, keepdims=True) + NORM_EPS)
    xn = xc * rstd
    ln = xn * lng + lnb
    mixed = []
    for col in range(4):
        lc = ln[:, col * 128:(col + 1) * 128]
        lo = jnp.where(low, lc, 0.0).astype(BF16)
        hi = jnp.where(low, 0.0, lc).astype(BF16)
        mixed.append(_dot(w_ref[2 * col], lo) + _dot(w_ref[2 * col + 1], hi) + bsx[:, col * 128:(col + 1) * 128])
    return gu, ln, xn, rstd, mixed


def _gm_fwd(zu, zv, lng, lnb, wtril, bsx):
    t = zu.shape[0]
    tm = _tile(t, 512)

    def body(zu_ref, zv_ref, g_ref, b_ref, w_ref, bs_ref, o_ref):
        low = lax.broadcasted_iota(jnp.int32, (BLK, BLK), 1) < 64
        for b in range(tm // BLK):
            rows = slice(b * BLK, (b + 1) * BLK)
            gu, _, _, _, mixed = _gm_forward_block(zu_ref[rows, :], zv_ref[rows, :], g_ref[...], b_ref[...], w_ref,
                                                   bs_ref[...], low)
            for col in range(4):
                o_ref[rows, col * 128:(col + 1) * 128] = (gu[:, col * 128:(col + 1) * 128] * mixed[col]).astype(BF16)

    return pl.pallas_call(
        body, name="gm_fwd", grid=(t // tm,),
        in_specs=[_row_spec(tm, GM_W), _row_spec(tm, GM_W), _vec_spec(GM_W), _vec_spec(GM_W),
                  pl.BlockSpec((8, BLK, BLK), lambda i: (0, 0, 0)), pl.BlockSpec((BLK, GM_W), lambda i: (0, 0))],
        out_specs=_row_spec(tm, GM_W),
        out_shape=jax.ShapeDtypeStruct((t, GM_W), BF16),
        compiler_params=_params("parallel"),
    )(zu, zv, lng, lnb, wtril, bsx)


def _out_proj(attn, gm, h, ga, gg, gpost, wb, layer):
    t = h.shape[0]
    tm = _tile(t, 512)

    def body(a_ref, m_ref, h_ref, ga_ref, gg_ref, gp_ref, w_ref, heads_ref, mix_ref, h1_ref):
        ha = _rms_fwd(a_ref[...].astype(F32), ga_ref[...]).astype(BF16)
        hg = _rms_fwd(m_ref[...].astype(F32), gg_ref[...]).astype(BF16)
        heads_ref[:, 0:512] = ha
        heads_ref[:, 512:1024] = hg
        mix = _dot(ha, w_ref[0:512, :]) + _dot(hg, w_ref[512:1024, :])
        mix_ref[...] = mix.astype(BF16)
        h1_ref[...] = h_ref[...] + _rms_fwd(mix, gp_ref[...])

    return pl.pallas_call(
        body, name="out_proj", grid=(t // tm,),
        in_specs=[_row_spec(tm, ATTN_W), _row_spec(tm, GM_W), _row_spec(tm, D_MODEL), _vec_spec(ATTN_W),
                  _vec_spec(GM_W), _vec_spec(D_MODEL), _seg_spec(ROWS_B, D_MODEL, 2 * layer)],
        out_specs=[_row_spec(tm, D_MODEL), _row_spec(tm, D_MODEL), _row_spec(tm, D_MODEL)],
        out_shape=[jax.ShapeDtypeStruct((t, D_MODEL), BF16), jax.ShapeDtypeStruct((t, D_MODEL), BF16),
                   jax.ShapeDtypeStruct((t, D_MODEL), F32)],
        compiler_params=_params("parallel"),
    )(attn, gm, h, ga, gg, gpost, wb)


def _ffn_fwd(h1, p, p_layer, gpre, gpost, gple, wa, wb, wp, layer, target=None):
    t = h1.shape[0]
    tm = _tile(t, 256)

    def body(h_ref, p_ref, gpre_ref, gpost_ref, gple_ref, wg_ref, wu_ref, wd_ref, wpg_ref, wpl_ref, *rest):
        if target is None:
            f_ref, gp_ref, up_ref, act_ref, fo_ref, h2_ref, hn_ref, gate_ref, h3_ref = rest
        else:
            t_ref, f_ref, gp_ref, up_ref, act_ref, fo_ref, h2_ref, hn_ref, gate_ref, dy_ref, l_ref = rest
            _zero_at(pl.program_id(0) == 0, l_ref)
        h = h_ref[...]
        pe = _dot_nt(p_ref[...].astype(BF16), wpl_ref[...])
        f = _rms_fwd(h, gpre_ref[...]).astype(BF16)
        f_ref[...] = f
        chunks = [slice(j * FF_CHUNK, (j + 1) * FF_CHUNK) for j in range(D_FF // FF_CHUNK)]
        fo = None
        gp, up = _dot_nt(f, wg_ref[chunks[0], :]), _dot_nt(f, wu_ref[chunks[0], :])
        for j, cols in enumerate(chunks):
            if j + 1 < len(chunks):
                gp_next, up_next = _dot_nt(f, wg_ref[chunks[j + 1], :]), _dot_nt(f, wu_ref[chunks[j + 1], :])
            act = (gp * _sigmoid(gp) * up).astype(BF16)
            gp_ref[:, cols] = gp.astype(BF16)
            up_ref[:, cols] = up.astype(BF16)
            act_ref[:, cols] = act
            part = _dot(act, wd_ref[cols, :])
            fo = part if fo is None else fo + part
            if j + 1 < len(chunks):
                gp, up = gp_next, up_next
        fo_ref[...] = fo
        h2 = h + _rms_fwd(fo, gpost_ref[...])
        h2_ref[...] = h2
        hn = _rms_fwd(h2, gple_ref[...]).astype(BF16)
        hn_ref[...] = hn
        gate = _sigmoid(_dot(hn, wpg_ref[...]))
        gate_ref[...] = gate.astype(BF16)
        h3 = h2 + pe * gate
        if target is None:
            h3_ref[...] = h3
        else:
            e = h3 - t_ref[...]
            dy_ref[...] = e * (1.0 / D_MODEL)
            s = jnp.sum(jnp.sum(e * e, axis=1, keepdims=True), axis=0, keepdims=True)
            l_ref[...] += jnp.broadcast_to(s, (1, 128))

    wide = _row_spec(tm, D_FF)
    row = _row_spec(tm, D_MODEL)
    vec = _vec_spec(D_MODEL)
    last = target is not None
    return pl.pallas_call(
        body, name="ffn_loss" if last else "ffn_fwd", grid=(t // tm,),
        in_specs=[row, _layer_row_spec(tm, PLE_DIM, p_layer), vec, vec, vec, _seg_spec(ROWS_A, D_MODEL, 3 * layer),
                  _seg_spec(ROWS_A, D_MODEL, 3 * layer + 1), _seg_spec(ROWS_A, D_MODEL, 3 * layer + 2),
                  _seg_spec(ROWS_B, D_MODEL, 2 * layer + 1), _seg_spec(ROWS_B, PLE_DIM, layer)] + [row] * last,
        out_specs=[row, wide, wide, wide, row, row, row, row, row] + [_vec_spec(128)] * last,
        out_shape=[jax.ShapeDtypeStruct((t, D_MODEL), BF16)] + [jax.ShapeDtypeStruct((t, D_FF), BF16)] * 3
        + [jax.ShapeDtypeStruct((t, D_MODEL), F32)] * 2 + [jax.ShapeDtypeStruct((t, D_MODEL), BF16)] * 2
        + [jax.ShapeDtypeStruct((t, D_MODEL), F32)] + [jax.ShapeDtypeStruct((1, 128), F32)] * last,
        compiler_params=_params("arbitrary" if last else "parallel"),
    )(h1, p, gpre, gpost, gple, wa, wa, wa, wb, wp, *([target] if last else []))


def _loss_head(y, target):
    t = y.shape[0]
    tm = _tile(t, 512)

    def body(y_ref, t_ref, dy_ref, l_ref):
        _zero_at(pl.program_id(0) == 0, l_ref)
        e = y_ref[...] - t_ref[...]
        dy_ref[...] = e * (1.0 / D_MODEL)
        s = jnp.sum(jnp.sum(e * e, axis=1, keepdims=True), axis=0, keepdims=True)
        l_ref[...] += jnp.broadcast_to(s, (1, 128))

    return pl.pallas_call(
        body, name="loss_head", grid=(t // tm,),
        in_specs=[_row_spec(tm, D_MODEL), _row_spec(tm, D_MODEL)],
        out_specs=[_row_spec(tm, D_MODEL), _vec_spec(128)],
        out_shape=[jax.ShapeDtypeStruct((t, D_MODEL), F32), jax.ShapeDtypeStruct((1, 128), F32)],
        compiler_params=_params("arbitrary"),
    )(y, target)


def _ffn_bwd(dh3, h2, gate, p, p_layer, fo, gp, up, h1, gple, gpost, gpre, wa, wb, wp, layer, after):
    t = dh3.shape[0]
    tm = _tile(t, 256)

    def body(d3_ref, h2_ref, gate_ref, p_ref, fo_ref, gp_ref, up_ref, h_ref, gple_ref, gpost_ref, gpre_ref,
             wg_ref, wu_ref, wd_ref, wpg_ref, wpl_ref, after_ref,
             dgl_ref, dpe_ref, dfo_ref, dgp_ref, dup_ref, dh1_ref, dgple_ref, dgpost_ref, dgpre_ref):
        _zero_at(pl.program_id(0) == 0, dgple_ref, dgpost_ref, dgpre_ref)
        d3 = d3_ref[...]
        gate = gate_ref[...].astype(F32)
        pe = _dot_nt(p_ref[...].astype(BF16), wpl_ref[...])
        dpe_ref[...] = (d3 * gate).astype(BF16)
        dgl = (d3 * pe * gate * (1.0 - gate)).astype(BF16)
        dgl_ref[...] = dgl
        dx2, dgple = _rms_bwd(h2_ref[...], gple_ref[...], _dot_nt(dgl, wpg_ref[...]))
        dgple_ref[...] += dgple
        d = d3 + dx2
        dfo, dgpost = _rms_bwd(fo_ref[...], gpost_ref[...], d)
        dfo = dfo.astype(BF16)
        dfo_ref[...] = dfo
        dgpost_ref[...] += dgpost
        chunks = [slice(j * FF_CHUNK, (j + 1) * FF_CHUNK) for j in range(D_FF // FF_CHUNK)]
        df = None
        dact = _dot_nt(dfo, wd_ref[chunks[0], :])
        for j, cols in enumerate(chunks):
            if j + 1 < len(chunks):
                dact_next = _dot_nt(dfo, wd_ref[chunks[j + 1], :])
            gp = gp_ref[:, cols].astype(F32)
            sg = _sigmoid(gp)
            dgp = (dact * up_ref[:, cols].astype(F32) * (sg * (1.0 + gp * (1.0 - sg)))).astype(BF16)
            dup = (dact * (gp * sg)).astype(BF16)
            dgp_ref[:, cols] = dgp
            dup_ref[:, cols] = dup
            part = _dot(dgp, wg_ref[cols, :]) + _dot(dup, wu_ref[cols, :])
            df = part if df is None else df + part
            if j + 1 < len(chunks):
                dact = dact_next
        dx, dgpre = _rms_bwd(h_ref[...], gpre_ref[...], df)
        dh1_ref[...] = d + dx
        dgpre_ref[...] += dgpre

    wide = _row_spec(tm, D_FF)
    row = _row_spec(tm, D_MODEL)
    vec = _vec_spec(D_MODEL)
    narrow = jax.ShapeDtypeStruct((t, D_MODEL), BF16)
    return pl.pallas_call(
        body, name="ffn_bwd", grid=(t // tm,),
        in_specs=[row, row, row, _layer_row_spec(tm, PLE_DIM, p_layer), row, wide, wide, row, vec, vec, vec,
                  _seg_spec(ROWS_A, D_MODEL, 3 * layer), _seg_spec(ROWS_A, D_MODEL, 3 * layer + 1),
                  _seg_spec(ROWS_A, D_MODEL, 3 * layer + 2), _seg_spec(ROWS_B, D_MODEL, 2 * layer + 1),
                  _seg_spec(ROWS_B, PLE_DIM, layer), pl.BlockSpec(memory_space=pl.ANY)],
        out_specs=[row, row, row, wide, wide, row, vec, vec, vec],
        out_shape=[narrow, narrow, narrow, jax.ShapeDtypeStruct((t, D_FF), BF16), jax.ShapeDtypeStruct((t, D_FF), BF16),
                   jax.ShapeDtypeStruct((t, D_MODEL), F32)] + [jax.ShapeDtypeStruct((1, D_MODEL), F32)] * 3,
        compiler_params=_params("arbitrary"),
    )(dh3, h2, gate, p, fo, gp, up, h1, gple, gpost, gpre, wa, wa, wa, wb, wp, after)


def _out_proj_bwd(dh1, mix, attn, gm, gpost, ga, gg, wb, layer, after):
    t = dh1.shape[0]
    tm = _tile(t, 512)

    def body(d_ref, mix_ref, a_ref, m_ref, gp_ref, ga_ref, gg_ref, w_ref, after_ref,
             dmix_ref, da_ref, dm_ref, dgp_ref, dga_ref, dgg_ref):
        _zero_at(pl.program_id(0) == 0, dgp_ref, dga_ref, dgg_ref)
        dmix, dgp = _rms_bwd(mix_ref[...].astype(F32), gp_ref[...], d_ref[...])
        dmix = dmix.astype(BF16)
        dmix_ref[...] = dmix
        da, dga = _rms_bwd(a_ref[...].astype(F32), ga_ref[...], _dot_nt(dmix, w_ref[0:512, :]))
        dm, dgg = _rms_bwd(m_ref[...].astype(F32), gg_ref[...], _dot_nt(dmix, w_ref[512:1024, :]))
        da_ref[...] = da.astype(BF16)
        dm_ref[...] = dm
        dgp_ref[...] += dgp
        dga_ref[...] += dga
        dgg_ref[...] += dgg

    return pl.pallas_call(
        body, name="out_proj_bwd", grid=(t // tm,),
        in_specs=[_row_spec(tm, D_MODEL), _row_spec(tm, D_MODEL), _row_spec(tm, ATTN_W), _row_spec(tm, GM_W),
                  _vec_spec(D_MODEL), _vec_spec(ATTN_W), _vec_spec(GM_W), _seg_spec(ROWS_B, D_MODEL, 2 * layer),
                  pl.BlockSpec(memory_space=pl.ANY)],
        out_specs=[_row_spec(tm, D_MODEL), _row_spec(tm, ATTN_W), _row_spec(tm, GM_W),
                   _vec_spec(D_MODEL), _vec_spec(ATTN_W), _vec_spec(GM_W)],
        out_shape=[jax.ShapeDtypeStruct((t, D_MODEL), BF16), jax.ShapeDtypeStruct((t, ATTN_W), BF16),
                   jax.ShapeDtypeStruct((t, GM_W), F32), jax.ShapeDtypeStruct((1, D_MODEL), F32),
                   jax.ShapeDtypeStruct((1, ATTN_W), F32), jax.ShapeDtypeStruct((1, GM_W), F32)],
        compiler_params=_params("arbitrary"),
    )(dh1, mix, attn, gm, gpost, ga, gg, wb, after)


def _split3(x):
    hi = x.astype(BF16)
    r1 = x - hi.astype(F32)
    mid = r1.astype(BF16)
    lo = (r1 - mid.astype(F32)).astype(BF16)
    return hi, mid, lo


def _gm_bwd(dgm, zu, zv, lng, lnb, wtril, bsx):
    t = zu.shape[0]
    tm = _tile(t, 512)
    nb = t // tm

    def body(d_ref, zu_ref, zv_ref, g_ref, b_ref, w_ref, bs_ref,
             dzu_ref, dzv_ref, dw_ref, dbs_ref, dlg_ref, dlb_ref, dbsx_ref):
        i = pl.program_id(0)
        _zero_at(i == 0, dw_ref, dlg_ref, dlb_ref, dbsx_ref)
        row = lax.broadcasted_iota(jnp.int32, (BLK, BLK), 0)
        lane = lax.broadcasted_iota(jnp.int32, (BLK, BLK), 1)
        low = lane < 64
        tril = row >= lane
        lng = g_ref[...]
        for b in range(tm // BLK):
            rows = slice(b * BLK, (b + 1) * BLK)
            zu = zu_ref[rows, :]
            zv = zv_ref[rows, :]
            gu, ln, xn, rstd, mixed = _gm_forward_block(zu, zv, lng, b_ref[...], w_ref, bs_ref[...], low)
            dgm = d_ref[rows, :]
            dgu_cols, dmx_cols, dln_cols = [], [], []
            for col in range(4):
                sl = slice(col * 128, (col + 1) * 128)
                dg = dgm[:, sl]
                dgu_cols.append(dg * mixed[col])
                dmx = dg * gu[:, sl]
                dmx_cols.append(dmx)
                lc = ln[:, sl]
                halves = (jnp.where(low, lc, 0.0).astype(BF16), jnp.where(low, 0.0, lc).astype(BF16))
                dmx16 = dmx.astype(BF16)
                dmx_half = (jnp.where(low, dmx, 0.0).astype(BF16), jnp.where(low, 0.0, dmx).astype(BF16))
                dln = None
                for half in range(2):
                    hd = 2 * col + half
                    dw_ref[hd] += jnp.where(tril, _dot_nt(dmx16, halves[half]), 0.0)
                    part = _dot_tn(w_ref[hd], dmx_half[half])
                    dln = part if dln is None else dln + part
                dln_cols.append(dln)
            dgu = jnp.concatenate(dgu_cols, axis=1)
            dmx = jnp.concatenate(dmx_cols, axis=1)
            dln = jnp.concatenate(dln_cols, axis=1)
            dzu_ref[rows, :] = (dgu * _gelu_grad(zu)).astype(BF16)
            dbsx_ref[...] += dmx
            dlg_ref[...] += jnp.sum(dln * xn, axis=0, keepdims=True)
            dlb_ref[...] += jnp.sum(dln, axis=0, keepdims=True)
            dxn = dln * lng
            dgv = rstd * (dxn - jnp.mean(dxn, axis=-1, keepdims=True) - xn * jnp.mean(dxn * xn, axis=-1, keepdims=True))
            dzv_ref[rows, :] = (dgv * _gelu_grad(zv)).astype(BF16)

        @pl.when(i == nb - 1)
        def _():
            r = lax.broadcasted_iota(jnp.int32, (GM_W, BLK), 0)
            c = lax.broadcasted_iota(jnp.int32, (GM_W, BLK), 1)
            e = jnp.where(jnp.logical_and(r >= c * 64, r < c * 64 + 64), 1.0, 0.0).astype(BF16)
            hi, mid, lo = _split3(dbsx_ref[...])
            dbs_ref[...] = _dot(hi, e) + _dot(mid, e) + _dot(lo, e)

    vec = _vec_spec(GM_W)
    return pl.pallas_call(
        body, name="gm_bwd", grid=(nb,),
        in_specs=[_row_spec(tm, GM_W)] * 3 + [vec, vec, pl.BlockSpec((8, BLK, BLK), lambda i: (0, 0, 0)),
                                              pl.BlockSpec((BLK, GM_W), lambda i: (0, 0))],
        out_specs=[_row_spec(tm, GM_W), _row_spec(tm, GM_W), pl.BlockSpec((8, BLK, BLK), lambda i: (0, 0, 0)),
                   pl.BlockSpec((BLK, BLK), lambda i: (0, 0)), vec, vec],
        out_shape=[jax.ShapeDtypeStruct((t, GM_W), BF16), jax.ShapeDtypeStruct((t, GM_W), BF16),
                   jax.ShapeDtypeStruct((8, BLK, BLK), F32), jax.ShapeDtypeStruct((BLK, BLK), F32),
                   jax.ShapeDtypeStruct((1, GM_W), F32), jax.ShapeDtypeStruct((1, GM_W), F32)],
        scratch_shapes=[pltpu.VMEM((BLK, GM_W), F32)],
        compiler_params=_params("arbitrary"),
    )(dgm, zu, zv, lng, lnb, wtril, bsx)


def _attn_bwd(q, kv, do, sinks, after):
    t = q.shape[0]
    tq = _tile(t, 512)
    blocks = tq // BLK

    def body(sink_ref, q_ref, kvc_ref, kvp_ref, do_ref, after_ref, dq_ref, dkv_ref, dkf_ref, ds_ref):
        i = pl.program_id(0)
        _zero_at(i == 0, ds_ref)
        low, vcur, dist, valid = _attn_masks(i)
        head_row = lax.broadcasted_iota(jnp.int32, (8, 128), 0)
        dsink_tile = jnp.zeros((8, 128), F32)
        kp, vp = _kv_variants(kvp_ref, slice(None), low)
        own = None
        for b in range(blocks):
            rows = slice(b * BLK, (b + 1) * BLK)
            kc, vc = _kv_variants(kvc_ref, rows, low)
            scores = _attn_scores_by_head(q_ref, rows, kc, kp, vcur)
            dprobs = _attn_scores_by_head(do_ref, rows, vc, vp, vcur)
            parts = []
            for h in range(8):
                p, ps = _attn_probs(scores[h], h, sink_ref[h], dist, valid if b == 0 else None)
                delta = jnp.sum(p * dprobs[h], axis=1, keepdims=True)
                ds = p * (dprobs[h] - delta) * ATTN_SCALE
                dsink = jnp.sum(-ps * delta, axis=0, keepdims=True)
                dsink_tile = jnp.where(head_row == h, dsink_tile + dsink, dsink_tile)
                parts.append(_split_cols(ds, vcur) + _split_cols(p, vcur))
            acc = {}

            def add(name, key, val):
                acc[(name, key)] = val if (name, key) not in acc else acc[(name, key)] + val

            for col in range(4):
                qh = q_ref[rows, col * 128:(col + 1) * 128]
                doh = do_ref[rows, col * 128:(col + 1) * 128]
                dq = None
                for half in range(2):
                    key = _head_key(2 * col + half)
                    dsp, dsc, pp, pc = parts[2 * col + half]
                    part = _dot(dsc, kc[key]) + _dot(dsp, kp[key])
                    dq = part if dq is None else dq + part
                    add("kc", key, _dot_tn(dsc, qh))
                    add("kp", key, _dot_tn(dsp, qh))
                    add("vc", key, _dot_tn(pc, doh))
                    add("vp", key, _dot_tn(pp, doh))
                dq_ref[rows, col * 128:(col + 1) * 128] = dq.astype(BF16)

            def place(name):
                head0 = acc[(name, (0, 0))] + pltpu.roll(acc[(name, (0, 1))], 64, axis=1)
                head1 = pltpu.roll(acc[(name, (1, 0))], 64, axis=1) + acc[(name, (1, 1))]
                return jnp.where(low, head0, head1)

            before = (place("kp"), place("vp"))
            if b == 0:
                dkf_ref[:, 0:128], dkf_ref[:, 128:256] = before
            else:
                last = slice((b - 1) * BLK, b * BLK)
                dkv_ref[last, 0:128] = own[0] + before[0]
                dkv_ref[last, 128:256] = own[1] + before[1]
            own = (place("kc"), place("vc"))
            kp, vp = kc, vc
        final = slice((blocks - 1) * BLK, blocks * BLK)
        dkv_ref[final, 0:128], dkv_ref[final, 128:256] = own
        ds_ref[...] += dsink_tile

    row_q = _row_spec(tq, ATTN_W)
    row_kv = _row_spec(tq, 2 * KV_W)
    return pl.pallas_call(
        body, name="attn_bwd", grid=(t // tq,),
        in_specs=[pl.BlockSpec(memory_space=pltpu.SMEM), row_q, row_kv, _kv_prev_spec(blocks), row_q,
                  pl.BlockSpec(memory_space=pl.ANY)],
        out_specs=[row_q, row_kv, _row_spec(BLK, 2 * KV_W), pl.BlockSpec((8, 128), lambda i: (0, 0))],
        out_shape=[jax.ShapeDtypeStruct((t, ATTN_W), BF16), jax.ShapeDtypeStruct((t, 2 * KV_W), F32),
                   jax.ShapeDtypeStruct((t // tq * BLK, 2 * KV_W), F32), jax.ShapeDtypeStruct((8, 128), F32)],
        compiler_params=_params("arbitrary"),
    )(sinks, q, kv, kv, do, after)


def _in_proj_bwd(dq, dkv, dkf, dzu, dzv, h, dres, g, wc, layer):
    t = h.shape[0]
    tm = _tile(t, 512)
    steps = t // tm

    def body(dq_ref, dkv_ref, dkn_ref, dzu_ref, dzv_ref, h_ref, d_ref, g_ref, w_ref, dz_ref, dh_ref, dg_ref):
        i = pl.program_id(0)
        _zero_at(i == 0, dg_ref)
        dq = dq_ref[...]
        tail = dkv_ref[tm - BLK:tm, :] + jnp.where(i < steps - 1, dkn_ref[...], 0.0)
        dkv = tail if tm == BLK else jnp.concatenate([dkv_ref[0:tm - BLK, :], tail], axis=0)
        dkv = dkv.astype(BF16)
        dzu = dzu_ref[...]
        dzv = dzv_ref[...]
        dz_ref[:, 0:512] = dq
        dz_ref[:, 512:768] = dkv
        dz_ref[:, 768:1280] = dzu
        dz_ref[:, 1280:1792] = dzv
        da = (_dot(dq, w_ref[0:512, :]) + _dot(dkv, w_ref[512:768, :]) + _dot(dzu, w_ref[768:1280, :])
              + _dot(dzv, w_ref[1280:1792, :]))
        dx, dg = _rms_bwd(h_ref[...], g_ref[...], da)
        dh_ref[...] = d_ref[...] + dx
        dg_ref[...] += dg

    return pl.pallas_call(
        body, name="in_proj_bwd", grid=(t // tm,),
        in_specs=[_row_spec(tm, ATTN_W), _row_spec(tm, 2 * KV_W),
                  pl.BlockSpec((BLK, 2 * KV_W), lambda i: (jnp.minimum(i + 1, steps - 1), 0)), _row_spec(tm, GM_W),
                  _row_spec(tm, GM_W), _row_spec(tm, D_MODEL), _row_spec(tm, D_MODEL), _vec_spec(D_MODEL),
                  _seg_spec(ROWS_C, D_MODEL, layer)],
        out_specs=[_row_spec(tm, D_IN), _row_spec(tm, D_MODEL), _vec_spec(D_MODEL)],
        out_shape=[jax.ShapeDtypeStruct((t, D_IN), BF16), jax.ShapeDtypeStruct((t, D_MODEL), F32),
                   jax.ShapeDtypeStruct((1, D_MODEL), F32)],
        compiler_params=_params("arbitrary"),
    )(dq, dkv, dkf, dzu, dzv, h, dres, g, wc)


def _weight_grad(a, b, buf, seg, b_layer=None):
    t, m = a.shape
    n = b.shape[-1]
    assert buf.shape[0] % m == 0 and buf.shape[1] == n
    tm = _tile(t, WGRAD_TOKENS)
    steps = t // tm
    half = m // 2

    def body(a_ref, b_ref, buf_ref, o_ref, acc_ref):
        i = pl.program_id(0)
        _zero_at(i == 0, acc_ref)
        b16 = b_ref[...].astype(BF16)
        for rows in (slice(0, half), slice(half, m)):
            acc_ref[rows, :] += _dot_tn(a_ref[:, rows], b16)

        @pl.when(i == steps - 1)
        def _():
            o_ref[...] = acc_ref[...].astype(o_ref.dtype)

    return pl.pallas_call(
        body, name="weight_grad", grid=(steps,),
        in_specs=[_row_spec(tm, m), _row_spec(tm, n) if b_layer is None else _layer_row_spec(tm, n, b_layer),
                  pl.BlockSpec(memory_space=pl.ANY)],
        out_specs=pl.BlockSpec((m, n), lambda i: (seg, 0)),
        out_shape=jax.ShapeDtypeStruct(buf.shape, buf.dtype),
        scratch_shapes=[pltpu.VMEM((m, n), F32)],
        input_output_aliases={2: 0},
        compiler_params=_params("arbitrary"),
    )(a, b, buf)


def _rows8(rows):
    return [jnp.pad(r, ((0, 7), (0, 0))) for r in rows]


def _small_pack_gating(d):
    rows = [jnp.concatenate([d["gm_ln_g"], d["gm_ln_b"]], axis=1), d["gm_bs"].reshape(1, 1024)]
    return jnp.concatenate(_rows8(rows) + [d["gm_ws"].reshape(128, 1024)], axis=0)


def _small_pack_rest(d):
    rows = [d["ln_mix_pre"], d["ln_mix_post"], d["ln_ffn_pre"], d["ln_ffn_post"], d["ln_ple_gate"],
            jnp.concatenate([d["g_attn_out"], d["g_gm_out"]], axis=1),
            jnp.pad(d["attn_sinks"].reshape(1, 8), ((0, 0), (0, 1016)))]
    return jnp.concatenate(_rows8(rows), axis=0)


def _small_unpack(g, s):
    return {
        "gm_ln_g": g[:, 0, :512], "gm_ln_b": g[:, 0, 512:], "gm_bs": g[:, 8].reshape(DEPTH, 8, 128),
        "gm_ws": g[:, 16:144].reshape(DEPTH, 8, 128, 128),
        "ln_mix_pre": s[:, 0], "ln_mix_post": s[:, 8], "ln_ffn_pre": s[:, 16], "ln_ffn_post": s[:, 24],
        "ln_ple_gate": s[:, 32], "g_attn_out": s[:, 40, :512], "g_gm_out": s[:, 40, 512:], "attn_sinks": s[:, 48, :8],
    }


def _row(v):
    return v.reshape(1, -1)


def _layer_fwd(h, p, sp, l, weights, target=None):
    tril = jnp.tril(jnp.ones((BLK, BLK), bool))
    wtril = jnp.where(tril[None], sp["gm_ws"][l], 0.0).astype(BF16)
    bsx = jnp.repeat(sp["gm_bs"][l].T, HEAD_DIM, axis=1)
    a, q, kv, zu, zv = _in_proj(h, _row(sp["ln_mix_pre"][l]), weights("c", h), 0)
    attn = _attn_fwd(q, kv, sp["attn_sinks"][l])
    gm = _gm_fwd(zu, zv, _row(sp["gm_ln_g"][l]), _row(sp["gm_ln_b"][l]), wtril, bsx)
    wb = weights("b", gm)
    heads, mix, h1 = _out_proj(attn, gm, h, _row(sp["g_attn_out"][l]), _row(sp["g_gm_out"][l]),
                               _row(sp["ln_mix_post"][l]), wb, 0)
    wa = weights("a", h1)
    f, gpre, up, act, fo, h2, hn, gate, *out = _ffn_fwd(
        h1, p, l, _row(sp["ln_ffn_pre"][l]), _row(sp["ln_ffn_post"][l]), _row(sp["ln_ple_gate"][l]), wa, wb,
        weights("p", gm), 0, target)
    saved = dict(h=h, a=a, q=q, kv=kv, zu=zu, zv=zv, attn=attn, gm=gm, heads=heads, mix=mix, h1=h1, f=f,
                 gpre=gpre, up=up, act=act, fo=fo, h2=h2, hn=hn, gate=gate, wtril=wtril, bsx=bsx)
    return (out[0] if target is None else tuple(out)), saved


def _layer_bwd_upper(dh, s, p, sp, l, wa, wb, wp, after):
    d = {}
    dgl, dpe, dfo, dgp, dup, dh1, d["ln_ple_gate"], d["ln_ffn_post"], d["ln_ffn_pre"] = _ffn_bwd(
        dh, s["h2"], s["gate"], p, l, s["fo"], s["gpre"], s["up"], s["h1"], _row(sp["ln_ple_gate"][l]),
        _row(sp["ln_ffn_post"][l]), _row(sp["ln_ffn_pre"][l]), wa, wb, wp, 0, after)
    gb = _weight_grad(s["hn"], dgl, lax.empty((2 * D_MODEL, D_MODEL), BF16), 1)
    gp = _weight_grad(dpe, p, lax.empty((D_MODEL, PLE_DIM), BF16), 0, b_layer=l)
    ga = _weight_grad(s["act"], dfo, lax.empty((3 * D_FF, D_MODEL), BF16), 2)
    ga = _weight_grad(dgp, s["f"], ga, 0)
    ga = _weight_grad(dup, s["f"], ga, 1)
    return (dh1, d), ga, gp, gb


def _layer_bwd_middle(carry, s, sp, l, wb, gb, after):
    dh1, d = carry
    dmix, dattn, dgm, d["ln_mix_post"], d["g_attn_out"], d["g_gm_out"] = _out_proj_bwd(
        dh1, s["mix"], s["attn"], s["gm"], _row(sp["ln_mix_post"][l]), _row(sp["g_attn_out"][l]),
        _row(sp["g_gm_out"][l]), wb, 0, after)
    gb = _weight_grad(s["heads"], dmix, gb, 0)
    dzu, dzv, d["gm_ws"], dbs, d["gm_ln_g"], d["gm_ln_b"] = _gm_bwd(
        dgm, s["zu"], s["zv"], _row(sp["gm_ln_g"][l]), _row(sp["gm_ln_b"][l]), s["wtril"], s["bsx"])
    d["gm_bs"] = dbs[:, :8].T
    return (dh1, dattn, dzu, dzv, d), gb, _small_pack_gating(d)


def _layer_bwd_lower(carry, s, sp, l, wc, after):
    dh1, dattn, dzu, dzv, d = carry
    dq, dkv, dkf, dsink = _attn_bwd(s["q"], s["kv"], dattn, sp["attn_sinks"][l], after)
    d["attn_sinks"] = dsink[:, 0]
    dz, dh, d["ln_mix_pre"] = _in_proj_bwd(dq, dkv, dkf, dzu, dzv, s["h"], dh1, _row(sp["ln_mix_pre"][l]), wc, 0)
    gc = _weight_grad(dz, s["a"], lax.empty((D_IN, D_MODEL), BF16), 0)
    return dh, gc, _small_pack_rest(d)


ANY = pl.BlockSpec(memory_space=pl.ANY)


def _place():
    x, y, c = lax.axis_index("x"), lax.axis_index("y"), lax.axis_index("c")
    chips = [(1 - x, y), (x, 1 - y), (1 - x, 1 - y)]
    return x, y, c, chips


def _all_gather(shards):
    n = len(shards)

    def body(*refs):
        ins, outs = refs[:n], refs[n:2 * n]
        send_sems, recv_sems, local_sems = refs[2 * n:]
        x, y, c, chips = _place()
        me, sibling = (x, y, c), (x, y, 1 - c)

        def block(k, px, py, pc):
            return outs[k].at[:, pl.ds(4 * px + 2 * py + pc, 1)]

        def copy(k, j, who, to, src=None):
            return pltpu.make_async_remote_copy(
                src_ref=block(k, *who) if src is None else src, dst_ref=block(k, *who),
                send_sem=send_sems.at[7 * k + j], recv_sem=recv_sems.at[7 * k + j],
                device_id=to, device_id_type=MESH)

        mine = [pltpu.make_async_copy(ins[k], block(k, *me), local_sems.at[k]) for k in range(n)]
        for cp in mine:
            cp.start()
        first = []
        for k in range(n):
            first.append(copy(k, 0, me, sibling, src=ins[k]))
            first += [copy(k, 1 + j, me, (*chip, c), src=ins[k]) for j, chip in enumerate(chips)]
        for cp in first:
            cp.start()
        passed = []
        for j, chip in enumerate(chips):
            for k in range(n):
                copy(k, 1 + j, (*chip, c), me).wait_recv()
                cp = copy(k, 4 + j, (*chip, c), sibling)
                cp.start()
                passed.append(cp)
        for k in range(n):
            copy(k, 0, sibling, me).wait_recv()
            for j, chip in enumerate(chips):
                copy(k, 4 + j, (*chip, 1 - c), me).wait_recv()
        for cp in first + passed:
            cp.wait_send()
        for cp in mine:
            cp.wait()

    return pl.pallas_call(
        body, name="all_gather_weights",
        in_specs=[ANY] * n, out_specs=[ANY] * n,
        out_shape=[jax.ShapeDtypeStruct((s.shape[0], N_DEV) + s.shape[2:], s.dtype) for s in shards],
        scratch_shapes=[pltpu.SemaphoreType.DMA((7 * n,)), pltpu.SemaphoreType.DMA((7 * n,)),
                        pltpu.SemaphoreType.DMA((n,))],
        compiler_params=pltpu.CompilerParams(has_side_effects=True),
    )(*shards)


def _sibling_exchange(bufs, small):
    n = len(bufs)

    def body(*refs):
        ins, small_ref = refs[:n], refs[n]
        outs, both_ref = refs[n + 1:2 * n + 1], refs[2 * n + 1]
        send_sems, recv_sems, local_sem = refs[2 * n + 2:]
        x, y, c, _ = _place()
        sibling = (x, y, 1 - c)
        mine = pltpu.make_async_copy(small_ref, both_ref.at[c], local_sem)
        mine.start()
        copies = [pltpu.make_async_remote_copy(
            src_ref=ins[k].at[:, :, pl.ds(1 - c, 1)], dst_ref=outs[k], send_sem=send_sems.at[k],
            recv_sem=recv_sems.at[k], device_id=sibling, device_id_type=MESH) for k in range(n)]
        copies.append(pltpu.make_async_remote_copy(
            src_ref=small_ref, dst_ref=both_ref.at[c], send_sem=send_sems.at[n], recv_sem=recv_sems.at[n],
            device_id=sibling, device_id_type=MESH))
        for cp in copies:
            cp.start()
        for k in range(n):
            copies[k].wait_recv()
        pltpu.make_async_remote_copy(
            src_ref=small_ref, dst_ref=both_ref.at[1 - c], send_sem=send_sems.at[n], recv_sem=recv_sems.at[n],
            device_id=sibling, device_id_type=MESH).wait_recv()
        for cp in copies:
            cp.wait_send()
        mine.wait()

    return pl.pallas_call(
        body, name="sibling_exchange",
        in_specs=[ANY] * (n + 1), out_specs=[ANY] * (n + 1),
        out_shape=[jax.ShapeDtypeStruct(b.shape[:2] + (1,) + b.shape[3:], b.dtype) for b in bufs]
        + [jax.ShapeDtypeStruct((2,) + small.shape, small.dtype)],
        scratch_shapes=[pltpu.SemaphoreType.DMA((n + 1,)), pltpu.SemaphoreType.DMA((n + 1,)), pltpu.SemaphoreType.DMA],
        compiler_params=pltpu.CompilerParams(has_side_effects=True),
    )(*bufs, small)


def _chip_exchange(sends, small):
    n = len(sends)

    def body(*refs):
        ins, small_ref = refs[:n], refs[n]
        outs, all_ref = refs[n + 1:2 * n + 1], refs[2 * n + 1]
        send_sems, recv_sems, local_sem = refs[2 * n + 2:]
        x, y, c, chips = _place()
        mine = pltpu.make_async_copy(small_ref, all_ref.at[2 * x + y], local_sem)
        mine.start()
        copies = []
        for j, chip in enumerate(chips):
            for k in range(n):
                copies.append(pltpu.make_async_remote_copy(
                    src_ref=ins[k].at[j], dst_ref=outs[k].at[j], send_sem=send_sems.at[3 * k + j],
                    recv_sem=recv_sems.at[3 * k + j], device_id=(*chip, c), device_id_type=MESH))
            copies.append(pltpu.make_async_remote_copy(
                src_ref=small_ref, dst_ref=all_ref.at[2 * x + y], send_sem=send_sems.at[3 * n + j],
                recv_sem=recv_sems.at[3 * n + j], device_id=(*chip, c), device_id_type=MESH))
        for cp in copies:
            cp.start()
        for j, (px, py) in enumerate(chips):
            for k in range(n):
                copies[j * (n + 1) + k].wait_recv()
            pltpu.make_async_remote_copy(
                src_ref=small_ref, dst_ref=all_ref.at[2 * px + py], send_sem=send_sems.at[3 * n + j],
                recv_sem=recv_sems.at[3 * n + j], device_id=(px, py, c), device_id_type=MESH).wait_recv()
        for cp in copies:
            cp.wait_send()
        mine.wait()

    return pl.pallas_call(
        body, name="chip_exchange",
        in_specs=[ANY] * (n + 1), out_specs=[ANY] * (n + 1),
        out_shape=[jax.ShapeDtypeStruct(s.shape, s.dtype) for s in sends]
        + [jax.ShapeDtypeStruct((4,) + small.shape, small.dtype)],
        scratch_shapes=[pltpu.SemaphoreType.DMA((3 * n + 3,)), pltpu.SemaphoreType.DMA((3 * n + 3,)),
                        pltpu.SemaphoreType.DMA],
        compiler_params=pltpu.CompilerParams(has_side_effects=True),
    )(*sends, small)


def _pair_add(buf, got, chip_ids, dtype):
    nseg, _, _, rows, cols = buf.shape
    nr = chip_ids.shape[0] - 1

    def body(ids_ref, a_ref, b_ref, o_ref):
        o_ref[...] = (a_ref[...] + b_ref[...]).astype(dtype)

    return pl.pallas_call(
        body, name="pair_add",
        grid_spec=pltpu.PrefetchScalarGridSpec(
            num_scalar_prefetch=1, grid=(nr, nseg),
            in_specs=[pl.BlockSpec((None, None, None, rows, cols), lambda r, s, ids: (s, ids[r], ids[nr], 0, 0)),
                      pl.BlockSpec((None, None, None, rows, cols), lambda r, s, ids: (s, ids[r], 0, 0, 0))],
            out_specs=pl.BlockSpec((None, None, rows, cols), lambda r, s, ids: (r, s, 0, 0))),
        out_shape=jax.ShapeDtypeStruct((nr, nseg, rows, cols), dtype),
        compiler_params=_params("parallel", "parallel"),
    )(chip_ids, buf, got)


def _sum_slots(z, tr):
    n, rows, cols = z.shape

    def body(z_ref, o_ref):
        s = z_ref[0]
        for k in range(1, n):
            s = s + z_ref[k]
        o_ref[...] = s

    return pl.pallas_call(
        body, name="sum_slots", grid=(rows // tr,),
        in_specs=[pl.BlockSpec((n, tr, cols), lambda i: (0, i, 0))],
        out_specs=pl.BlockSpec((tr, cols), lambda i: (i, 0)),
        out_shape=jax.ShapeDtypeStruct((rows, cols), F32),
        compiler_params=_params("parallel"),
    )(z)


def _final_sum(own, got):
    _, nseg, rows, cols = own.shape

    def body(a_ref, b_ref, o_ref):
        s = a_ref[0]
        for k in range(3):
            s = s + b_ref[k].astype(F32)
        o_ref[...] = s

    return pl.pallas_call(
        body, name="final_sum", grid=(nseg,),
        in_specs=[pl.BlockSpec((1, None, rows, cols), lambda i: (0, i, 0, 0)),
                  pl.BlockSpec((3, None, rows, cols), lambda i: (0, i, 0, 0))],
        out_specs=pl.BlockSpec((None, rows, cols), lambda i: (i, 0, 0)),
        out_shape=jax.ShapeDtypeStruct((nseg, rows, cols), F32),
        compiler_params=_params("parallel"),
    )(own, got)


HBM = pl.BlockSpec(memory_space=pltpu.HBM)
SEM = pl.BlockSpec(memory_space=pltpu.SEMAPHORE)
N_PEERS = N_DEV - 1


def _peers():
    x, y, c = lax.axis_index("x"), lax.axis_index("y"), lax.axis_index("c")
    peers = []
    for r in range(1, N_DEV):
        px = 1 - x if r & 4 else x
        py = 1 - y if r & 2 else y
        pc = 1 - c if r & 1 else c
        peers.append(((px, py, pc), 4 * px + 2 * py + pc))
    return 4 * x + 2 * y + c, peers


GATHER, SCATTER, SPREAD = "gather", "scatter", "spread"


def _peer_copy(src, land, send_sems, recv_sems, r, me, peer, peer_slot, mode):
    return pltpu.make_async_remote_copy(
        src_ref=src.at[:, pl.ds(peer_slot, 1)] if mode == SCATTER else src,
        dst_ref=land.at[:, pl.ds(me, 1)] if mode == GATHER else land.at[:, pl.ds(r - 1, 1)],
        send_sem=send_sems.at[r - 1], recv_sem=recv_sems.at[r - 1], device_id=peer, device_id_type=MESH)


def _peer_arrival(src, land, send_sems, recv_sems, r, me, peer, peer_slot, mode):
    return pltpu.make_async_remote_copy(
        src_ref=src.at[:, pl.ds(me, 1)] if mode == SCATTER else src,
        dst_ref=land.at[:, pl.ds(peer_slot, 1)] if mode == GATHER else land.at[:, pl.ds(r - 1, 1)],
        send_sem=send_sems.at[r - 1], recv_sem=recv_sems.at[r - 1], device_id=peer, device_id_type=MESH)


def _send_start(name, srcs, lands, modes):
    n = len(srcs)

    def body(*refs):
        src_refs, land_refs = refs[:n], refs[n:2 * n]
        outs = refs[2 * n:]
        send_sems, recv_sems, token = outs[2 * n:3 * n], outs[3 * n:4 * n], outs[4 * n]
        me, peers = _peers()
        for k in range(n):
            for r, (peer, slot) in enumerate(peers, 1):
                _peer_copy(src_refs[k], land_refs[k], send_sems[k], recv_sems[k], r, me, peer, slot, modes[k]).start()
        token[...] = jnp.zeros_like(token)

    hbm = lambda a: pltpu.HBM(a.shape, a.dtype)
    sems = [pltpu.SemaphoreType.DMA((N_PEERS,))] * (2 * n)
    outs = pl.pallas_call(
        body, name=name, in_specs=[HBM] * (2 * n),
        out_specs=[HBM] * (2 * n) + [SEM] * (2 * n) + [pl.BlockSpec(memory_space=pltpu.VMEM)],
        out_shape=[hbm(a) for a in srcs] + [hbm(a) for a in lands] + sems + [jax.ShapeDtypeStruct((8, 128), F32)],
        input_output_aliases={k: k for k in range(2 * n)},
        compiler_params=pltpu.CompilerParams(has_side_effects=pltpu.SideEffectType.DATAFLOW_SIDE_EFFECTING),
    )(*[pltpu.with_memory_space_constraint(a, pltpu.HBM) for a in list(srcs) + list(lands)])
    return dict(srcs=outs[:n], lands=outs[n:2 * n], send=outs[2 * n:3 * n], recv=outs[3 * n:4 * n],
                modes=list(modes)), outs[4 * n]


def _send_wait(name, sent, ks, after):
    n = len(ks)
    srcs = [sent["srcs"][k] for k in ks]
    lands = [sent["lands"][k] for k in ks]
    modes = [sent["modes"][k] for k in ks]

    def body(*refs):
        src_refs, land_refs = refs[:n], refs[n:2 * n]
        send_sems, recv_sems = refs[2 * n:3 * n], refs[3 * n:4 * n]
        me, peers = _peers()
        for k in range(n):
            for r, (peer, slot) in enumerate(peers, 1):
                args = (src_refs[k], land_refs[k], send_sems[k], recv_sems[k], r, me, peer, slot, modes[k])
                _peer_copy(*args).wait_send()
                _peer_arrival(*args).wait_recv()

    hbm = lambda a: pltpu.HBM(a.shape, a.dtype)
    outs = pl.pallas_call(
        body, name=name, in_specs=[HBM] * (2 * n) + [SEM] * (2 * n) + [ANY],
        out_specs=[HBM] * (2 * n), out_shape=[hbm(a) for a in srcs] + [hbm(a) for a in lands],
        input_output_aliases={k: k for k in range(2 * n)},
        compiler_params=pltpu.CompilerParams(has_side_effects=pltpu.SideEffectType.DATAFLOW_SIDE_EFFECTING),
    )(*srcs, *lands, *[sent["send"][k] for k in ks], *[sent["recv"][k] for k in ks], after)
    return outs[n:], outs[:n]


def _sum_blocks(own, land, ids):
    nseg, _, rows, cols = land.shape

    def body(ids_ref, own_ref, land_ref, o_ref):
        me = ids_ref[1]
        total = None
        for j in range(N_DEV):
            slot = jnp.maximum(jnp.bitwise_xor(me, j) - 1, 0)
            term = jnp.where(me == j, own_ref[...], land_ref[slot]).astype(F32)
            total = term if total is None else total + term
        o_ref[...] = total

    return pl.pallas_call(
        body, name="sum_blocks",
        grid_spec=pltpu.PrefetchScalarGridSpec(
            num_scalar_prefetch=1, grid=(nseg,),
            in_specs=[pl.BlockSpec((None, None, rows, cols), lambda s, ids: (s, ids[0], 0, 0)),
                      pl.BlockSpec((None, N_PEERS, rows, cols), lambda s, ids: (s, 0, 0, 0))],
            out_specs=pl.BlockSpec((None, rows, cols), lambda s, ids: (s, 0, 0))),
        out_shape=jax.ShapeDtypeStruct((nseg, rows, cols), F32),
        compiler_params=_params("parallel"),
    )(ids, own, land)


def _adamw(w, g, m, v):
    shape = w.shape
    cols = shape[-1]
    rows = w.size // cols
    tr = rows
    for cand in (512, 256, 128, 64, 32, 16, 8):
        if rows % cand == 0:
            tr = cand
            break
    c1 = 1.0 / (1.0 - ADAM_B1 ** ADAM_STEP)
    c2 = 1.0 / (1.0 - ADAM_B2 ** ADAM_STEP)

    def body(w_ref, g_ref, m_ref, v_ref, d_ref, nm_ref, nv_ref):
        g = g_ref[...]
        m = ADAM_B1 * m_ref[...] + (1.0 - ADAM_B1) * g
        v = ADAM_B2 * v_ref[...] + (1.0 - ADAM_B2) * (g * g)
        nm_ref[...] = m
        nv_ref[...] = v
        d_ref[...] = -ADAM_LR * ((m * c1) / (jnp.sqrt(v * c2) + ADAM_EPS) + ADAM_WD * w_ref[...])

    spec = pl.BlockSpec((tr, cols), lambda i: (i, 0))
    outs = pl.pallas_call(
        body, name="adamw", grid=(rows // tr,),
        in_specs=[spec] * 4, out_specs=[spec] * 3,
        out_shape=[jax.ShapeDtypeStruct((rows, cols), F32)] * 3,
        compiler_params=_params("parallel"),
    )(*[a.reshape(rows, cols) for a in (w, g, m, v)])
    return [o.reshape(shape) for o in outs]


SMALL = ("ln_mix_pre", "attn_sinks", "gm_ln_g", "gm_ln_b", "gm_ws", "gm_bs", "g_attn_out", "g_gm_out",
         "ln_mix_post", "ln_ffn_pre", "ln_ffn_post", "ln_ple_gate")
WEIGHTS = ("ln_mix_pre", "w_in", "attn_sinks", "gm_ln_g", "gm_ln_b", "gm_ws", "gm_bs", "g_attn_out", "g_gm_out",
           "w_out", "ln_mix_post", "ln_ffn_pre", "w_ffn_gate", "w_ffn_up", "w_ffn_down", "ln_ffn_post", "w_ple",
           "ln_ple_gate", "w_ple_gate")


def _pack_shards(w, l):
    sa = jnp.stack([w["w_ffn_gate"][l].T, w["w_ffn_up"][l].T, w["w_ffn_down"][l]])[:, None]
    sb = jnp.stack([w["w_out"][l], w["w_ple_gate"][l]])[:, None]
    return [w["w_in"][l].T[None, None].astype(BF16), sb.astype(BF16), w["w_ple"][l].T[None, None].astype(BF16),
            sa.astype(BF16)]


def _unpack_grads(rc, rb, rp, ra):
    return {"w_in": rc[0].T, "w_out": rb[0], "w_ple_gate": rb[1], "w_ple": rp[0].T,
            "w_ffn_gate": ra[0].T, "w_ffn_up": ra[1].T, "w_ffn_down": ra[2]}


def kernel(x, p, ln_mix_pre, w_in, attn_sinks, gm_ln_g, gm_ln_b, gm_ws, gm_bs, g_attn_out, g_gm_out, w_out, ln_mix_post, ln_ffn_pre, w_ffn_gate, w_ffn_up, w_ffn_down, ln_ffn_post, w_ple, ln_ple_gate, w_ple_gate, loss_target, m_ln_mix_pre, m_w_in, m_attn_sinks, m_gm_ln_g, m_gm_ln_b, m_gm_ws, m_gm_bs, m_g_attn_out, m_g_gm_out, m_w_out, m_ln_mix_post, m_ln_ffn_pre, m_w_ffn_gate, m_w_ffn_up, m_w_ffn_down, m_ln_ffn_post, m_w_ple, m_ln_ple_gate, m_w_ple_gate, v_ln_mix_pre, v_w_in, v_attn_sinks, v_gm_ln_g, v_gm_ln_b, v_gm_ws, v_gm_bs, v_g_attn_out, v_g_gm_out, v_w_out, v_ln_mix_post, v_ln_ffn_pre, v_w_ffn_gate, v_w_ffn_up, v_w_ffn_down, v_ln_ffn_post, v_w_ple, v_ln_ple_gate, v_w_ple_gate):
    given = dict(locals())
    w = {n: given[n] for n in WEIGHTS}
    sp = {n: w[n] for n in SMALL}
    kinds = ("c", "b", "p", "a")

    me, _ = _peers()
    shards = [s for l in range(DEPTH) for s in _pack_shards(w, l)]
    lands = [lax.dynamic_update_slice(lax.empty((s.shape[0], N_DEV) + s.shape[2:], BF16), s, (0, me, 0, 0))
             for s in shards]
    gather, token = _send_start("gather_start", shards, lands, [GATHER] * len(shards))
    layer_weights = [{} for _ in range(DEPTH)]

    def weights_of(l):
        def get(kind, after):
            have = layer_weights[l]
            if kind not in have:
                if l < 2:
                    group = {"c": ("c",), "b": ("b", "p"), "p": ("b", "p"), "a": ("a",)}[kind]
                    after = token if (l == 0 and kind == "c") else after
                else:
                    group = kinds
                got, _ = _send_wait(f"gather_wait_{l}{group[0]}", gather, [4 * l + kinds.index(k) for k in group], after)
                for k, g in zip(group, got):
                    have[k] = g.reshape(-1, g.shape[-1])
            return have[kind]
        return get

    h = x[0]
    p3 = p.reshape(DEPTH, -1, PLE_DIM)
    saved = []
    for l in range(DEPTH):
        h, s = _layer_fwd(h, p3, sp, l, weights_of(l), loss_target[0] if l == DEPTH - 1 else None)
        saved.append(s)
    dh, sq = h

    reduces = []
    after = token
    view = lambda g, rows: g.reshape(-1, N_DEV, rows, g.shape[-1])
    pack16 = lambda s: s.astype(BF16)[None, None]
    landing = lambda a: lax.empty((a.shape[0], N_PEERS) + a.shape[2:], BF16)

    def send(name, bufs, modes):
        return _send_start(name, bufs, [landing(a) for a in bufs], modes)

    for l in reversed(range(DEPTH)):
        lw = layer_weights[l]
        carry, ga, gp, gb = _layer_bwd_upper(dh, saved[l], p3, sp, l, lw["a"], lw["b"], lw["p"], after)
        sent1, after = send(f"reduce_start_{l}a", [view(ga, ROWS_A), view(gp, ROWS_B)], [SCATTER, SCATTER])
        carry, gb, gating = _layer_bwd_middle(carry, saved[l], sp, l, lw["b"], gb, after)
        sent2, after = send(f"reduce_start_{l}b", [view(gb, ROWS_B), pack16(gating)], [SCATTER, SPREAD])
        dh, gc, rest = _layer_bwd_lower(carry, saved[l], sp, l, lw["c"], after)
        sent3, after = send(f"reduce_start_{l}c", [view(gc, ROWS_C), pack16(rest)], [SCATTER, SPREAD])
        reduces.append((l, sent1, sent2, sent3))

    mine = jnp.stack([me, me]).astype(jnp.int32)
    whole = jnp.stack([jnp.zeros_like(me), me]).astype(jnp.int32)
    sums = {k: [None] * DEPTH for k in ("a", "p", "b", "c", "gating", "rest")}
    last = {}
    for l, sent1, sent2, sent3 in reduces:
        (la, lp), (ga, gp) = _send_wait(f"reduce_wait_{l}a", sent1, [0, 1], dh)
        (lb, lg), (gb, gg) = _send_wait(f"reduce_wait_{l}b", sent2, [0, 1], dh)
        sums["a"][l], sums["p"][l] = _sum_blocks(ga, la, mine), _sum_blocks(gp, lp, mine)
        sums["b"][l], sums["gating"][l] = _sum_blocks(gb, lb, mine), _sum_blocks(gg, lg, whole)[0]
        if l > 0:
            (lc, lr), (gc, gr) = _send_wait(f"reduce_wait_{l}c", sent3, [0, 1], dh)
            sums["c"][l], sums["rest"][l] = _sum_blocks(gc, lc, mine), _sum_blocks(gr, lr, whole)[0]
        else:
            last = sent3
    grad_x = dh
    loss = lax.psum(sq[0, 0] * (0.5 / D_MODEL), AXES)
    grads, delta, new_m, new_v = {}, {}, {}, {}

    def update(names):
        for n in names:
            delta[n], new_m[n], new_v[n] = _adamw(w[n], grads[n], given["m_" + n], given["v_" + n])

    stack = lambda f, xs: jnp.stack([f(x) for x in xs])
    gating = jnp.stack(sums["gating"])
    grads.update({
        "w_ffn_gate": stack(lambda r: r[0].T, sums["a"]), "w_ffn_up": stack(lambda r: r[1].T, sums["a"]),
        "w_ffn_down": stack(lambda r: r[2], sums["a"]), "w_ple": stack(lambda r: r[0].T, sums["p"]),
        "w_out": stack(lambda r: r[0], sums["b"]), "w_ple_gate": stack(lambda r: r[1], sums["b"]),
        "gm_ln_g": gating[:, 0, :512], "gm_ln_b": gating[:, 0, 512:], "gm_bs": gating[:, 8].reshape(DEPTH, 8, 128),
        "gm_ws": gating[:, 16:144].reshape(DEPTH, 8, 128, 128)})
    early = tuple(grads)
    update(early)
    (lc, lr), (gc, gr) = _send_wait("reduce_wait_0c", last, [0, 1], delta["w_ffn_down"])
    sums["c"][0], sums["rest"][0] = _sum_blocks(gc, lc, mine), _sum_blocks(gr, lr, whole)[0]
    rest = jnp.stack(sums["rest"])
    grads.update({
        "w_in": stack(lambda r: r[0].T, sums["c"]),
        "ln_mix_pre": rest[:, 0], "ln_mix_post": rest[:, 8], "ln_ffn_pre": rest[:, 16], "ln_ffn_post": rest[:, 24],
        "ln_ple_gate": rest[:, 32], "g_attn_out": rest[:, 40, :512], "g_gm_out": rest[:, 40, 512:],
        "attn_sinks": rest[:, 48, :8]})
    update([n for n in grads if n not in early])
    return (loss, grad_x[None], *[grads[n] for n in WEIGHTS], *[delta[n] for n in WEIGHTS],
            *[new_m[n] for n in WEIGHTS], *[new_v[n] for n in WEIGHTS])
```

```python
import math

import jax
import jax.numpy as jnp
from jax import lax
from jax.experimental import pallas as pl
from jax.experimental.pallas import tpu as pltpu

F32 = jnp.float32
BF16 = jnp.bfloat16
MESH = pl.DeviceIdType.MESH
AXES = ("x", "y", "c")

D_MODEL = 1024
DEPTH = 4
N_DEV = 8
HEAD_DIM = 64
ATTN_W = 512
KV_W = 128
GM_W = 512
D_IN = 1792
D_FF = 2816
PLE_DIM = 256
BLK = 128
FF_CHUNK = 256
WGRAD_TOKENS = 1024
NORM_EPS = 1e-6
NEG_BIG = -1e30
ATTN_SCALE = HEAD_DIM ** -0.5

ADAM_LR = 0.001
ADAM_B1 = 0.9
ADAM_B2 = 0.999
ADAM_EPS = 1e-08
ADAM_WD = 0.01
ADAM_STEP = 10

ROWS_A = D_FF // N_DEV
ROWS_B = D_MODEL // N_DEV
ROWS_C = D_IN // N_DEV
GATING_ROWS = 144
REST_ROWS = 56
SMALL_ROWS = 200

VMEM_LIMIT = 56 * 2 ** 20


def _params(*sem):
    return pltpu.CompilerParams(dimension_semantics=sem, vmem_limit_bytes=VMEM_LIMIT)


def _dot(a, b):
    return jnp.dot(a, b, preferred_element_type=F32)


def _dot_nt(a, b):
    return lax.dot_general(a, b, (((1,), (1,)), ((), ())), preferred_element_type=F32)


def _dot_tn(a, b):
    return lax.dot_general(a, b, (((0,), (0,)), ((), ())), preferred_element_type=F32)


def _rms_fwd(x, g):
    r = lax.rsqrt(jnp.mean(x * x, axis=-1, keepdims=True) + NORM_EPS)
    return x * r * g


def _rms_bwd(x, g, dy):
    r = lax.rsqrt(jnp.mean(x * x, axis=-1, keepdims=True) + NORM_EPS)
    xh = x * r
    dg = jnp.sum(dy * xh, axis=0, keepdims=True)
    dxh = dy * g
    dx = r * (dxh - xh * jnp.mean(dxh * xh, axis=-1, keepdims=True))
    return dx, dg


_GELU_C = math.sqrt(2.0 / math.pi)


def _gelu(x):
    t = jnp.tanh(_GELU_C * (x + 0.044715 * (x * x * x)))
    return 0.5 * x * (1.0 + t)


def _gelu_grad(x):
    x2 = x * x
    t = jnp.tanh(_GELU_C * (x + 0.044715 * (x2 * x)))
    return 0.5 * (1.0 + t) + 0.5 * x * (1.0 - t * t) * (_GELU_C * (1.0 + 3.0 * 0.044715 * x2))


def _sigmoid(x):
    return 1.0 / (1.0 + jnp.exp(-x))


def _row_spec(tm, n):
    return pl.BlockSpec((tm, n), lambda i: (i, 0))


def _layer_row_spec(tm, n, l):
    return pl.BlockSpec((None, tm, n), lambda i: (l, i, 0))


def _vec_spec(n):
    return pl.BlockSpec((1, n), lambda i: (0, 0))


def _seg_spec(rows, cols, seg):
    return pl.BlockSpec((N_DEV * rows, cols), lambda i: (seg, 0), pipeline_mode=pl.Buffered(1))


def _zero_at(first, *refs):
    @pl.when(first)
    def _():
        for r in refs:
            r[...] = jnp.zeros(r.shape, r.dtype)


def _tile(t, want):
    return min(t, want)


def _in_proj(h, g, wc, layer):
    t = h.shape[0]
    tm = _tile(t, 512)

    def body(h_ref, g_ref, w_ref, a_ref, q_ref, kv_ref, zu_ref, zv_ref):
        a = _rms_fwd(h_ref[...], g_ref[...]).astype(BF16)
        a_ref[...] = a
        q_ref[...] = _dot_nt(a, w_ref[0:512, :]).astype(BF16)
        kv_ref[...] = _dot_nt(a, w_ref[512:768, :]).astype(BF16)
        zu_ref[...] = _dot_nt(a, w_ref[768:1280, :])
        zv_ref[...] = _dot_nt(a, w_ref[1280:1792, :])

    return pl.pallas_call(
        body, name="in_proj", grid=(t // tm,),
        in_specs=[_row_spec(tm, D_MODEL), _vec_spec(D_MODEL), _seg_spec(ROWS_C, D_MODEL, layer)],
        out_specs=[_row_spec(tm, D_MODEL), _row_spec(tm, ATTN_W), _row_spec(tm, 2 * KV_W),
                   _row_spec(tm, GM_W), _row_spec(tm, GM_W)],
        out_shape=[jax.ShapeDtypeStruct((t, D_MODEL), BF16), jax.ShapeDtypeStruct((t, ATTN_W), BF16),
                   jax.ShapeDtypeStruct((t, 2 * KV_W), BF16), jax.ShapeDtypeStruct((t, GM_W), F32),
                   jax.ShapeDtypeStruct((t, GM_W), F32)],
        compiler_params=_params("parallel"),
    )(h, g, wc)


def _head_variants(x, low):
    xr = pltpu.roll(x, 64, axis=1)
    zero = jnp.zeros_like(x)
    return {
        (0, 0): jnp.where(low, x, zero).astype(BF16),
        (0, 1): jnp.where(low, zero, xr).astype(BF16),
        (1, 0): jnp.where(low, xr, zero).astype(BF16),
        (1, 1): jnp.where(low, zero, x).astype(BF16),
    }


def _attn_masks(i):
    row = lax.broadcasted_iota(jnp.int32, (BLK, BLK), 0)
    lane = lax.broadcasted_iota(jnp.int32, (BLK, BLK), 1)
    vcur = row >= lane
    dist = jnp.where(vcur, row - lane, row - lane + BLK).astype(F32)
    valid = jnp.logical_or(vcur, i > 0)
    return lane < 64, vcur, dist, valid


def _head_key(h):
    return (h // 4, h % 2)


def _stack_kv(prev, cur, g):
    return jnp.concatenate([prev[(g, 0)], cur[(g, 0)], prev[(g, 1)], cur[(g, 1)]], axis=0)


def _split_cols(p, vcur):
    return [jnp.where(vcur, 0.0, p).astype(BF16), jnp.where(vcur, p, 0.0).astype(BF16)]


def _attn_scores(q_ref, rows, stacked, vcur):
    out = []
    for col in range(4):
        big = _dot_nt(q_ref[rows, col * 128:(col + 1) * 128], stacked[col // 2])
        for half in range(2):
            out.append(jnp.where(vcur, big[:, half * 256 + 128:half * 256 + 256], big[:, half * 256:half * 256 + 128]))
    return out


def _attn_scores_by_head(q_ref, rows, kc, kp, vcur):
    out = []
    for h in range(8):
        qh = q_ref[rows, (h // 2) * 128:(h // 2 + 1) * 128]
        out.append(jnp.where(vcur, _dot_nt(qh, kc[_head_key(h)]), _dot_nt(qh, kp[_head_key(h)])))
    return out


def _attn_probs(s, h, sink, dist, valid):
    s = s * ATTN_SCALE - (2.0 ** -(h + 1)) * dist
    if valid is not None:
        s = jnp.where(valid, s, NEG_BIG)
    m = jnp.maximum(jnp.max(s, axis=1, keepdims=True), sink)
    e = jnp.exp(s - m)
    es = jnp.exp(sink - m)
    inv = 1.0 / (jnp.sum(e, axis=1, keepdims=True) + es)
    return e * inv, es * inv


def _kv_prev_spec(blocks):
    return pl.BlockSpec((BLK, 2 * KV_W), lambda i: (jnp.maximum(i * blocks - 1, 0), 0))


def _kv_variants(kv_ref, rows, low):
    return (_head_variants(kv_ref[rows, 0:128].astype(F32), low), _head_variants(kv_ref[rows, 128:256].astype(F32), low))


def _attn_fwd(q, kv, sinks):
    t = q.shape[0]
    tq = _tile(t, 512)
    blocks = tq // BLK

    def body(sink_ref, q_ref, kvc_ref, kvp_ref, o_ref):
        low, vcur, dist, valid = _attn_masks(pl.program_id(0))
        kp, vp = _kv_variants(kvp_ref, slice(None), low)
        for b in range(blocks):
            rows = slice(b * BLK, (b + 1) * BLK)
            kc, vc = _kv_variants(kvc_ref, rows, low)
            ks = [_stack_kv(kp, kc, g) for g in range(2)]
            vs = [_stack_kv(vp, vc, g) for g in range(2)]
            scores = _attn_scores(q_ref, rows, ks, vcur)
            probs = [_attn_probs(scores[h], h, sink_ref[h], dist, valid if b == 0 else None)[0] for h in range(8)]
            for col in range(4):
                p_col = jnp.concatenate(_split_cols(probs[2 * col], vcur) + _split_cols(probs[2 * col + 1], vcur), axis=1)
                o_ref[rows, col * 128:(col + 1) * 128] = _dot(p_col, vs[col // 2]).astype(BF16)
            kp, vp = kc, vc

    return pl.pallas_call(
        body, name="attn_fwd", grid=(t // tq,),
        in_specs=[pl.BlockSpec(memory_space=pltpu.SMEM), _row_spec(tq, ATTN_W), _row_spec(tq, 2 * KV_W),
                  _kv_prev_spec(blocks)],
        out_specs=_row_spec(tq, ATTN_W),
        out_shape=jax.ShapeDtypeStruct((t, ATTN_W), BF16),
        compiler_params=_params("parallel"),
    )(sinks, q, kv, kv)


def _gm_forward_block(zu, zv, lng, lnb, w_ref, bsx, low):
    gu = _gelu(zu)
    gv = _gelu(zv)
    mu = jnp.mean(gv, axis=-1, keepdims=True)
    xc = gv - mu
    rstd = lax.rsqrt(jnp.mean(xc * xc, axis=-1, keepdims=True) + NORM_EPS)
    xn = xc * rstd
    ln = xn * lng + lnb
    mixed = []
    for col in range(4):
        lc = ln[:, col * 128:(col + 1) * 128]
        lo = jnp.where(low, lc, 0.0).astype(BF16)
        hi = jnp.where(low, 0.0, lc).astype(BF16)
        mixed.append(_dot(w_ref[2 * col], lo) + _dot(w_ref[2 * col + 1], hi) + bsx[:, col * 128:(col + 1) * 128])
    return gu, ln, xn, rstd, mixed


def _gm_fwd(zu, zv, lng, lnb, wtril, bsx):
    t = zu.shape[0]
    tm = _tile(t, 512)

    def body(zu_ref, zv_ref, g_ref, b_ref, w_ref, bs_ref, o_ref):
        low = lax.broadcasted_iota(jnp.int32, (BLK, BLK), 1) < 64
        for b in range(tm // BLK):
            rows = slice(b * BLK, (b + 1) * BLK)
            gu, _, _, _, mixed = _gm_forward_block(zu_ref[rows, :], zv_ref[rows, :], g_ref[...], b_ref[...], w_ref,
                                                   bs_ref[...], low)
            for col in range(4):
                o_ref[rows, col * 128:(col + 1) * 128] = (gu[:, col * 128:(col + 1) * 128] * mixed[col]).astype(BF16)

    return pl.pallas_call(
        body, name="gm_fwd", grid=(t // tm,),
        in_specs=[_row_spec(tm, GM_W), _row_spec(tm, GM_W), _vec_spec(GM_W), _vec_spec(GM_W),
                  pl.BlockSpec((8, BLK, BLK), lambda i: (0, 0, 0)), pl.BlockSpec((BLK, GM_W), lambda i: (0, 0))],
        out_specs=_row_spec(tm, GM_W),
        out_shape=jax.ShapeDtypeStruct((t, GM_W), BF16),
        compiler_params=_params("parallel"),
    )(zu, zv, lng, lnb, wtril, bsx)


def _out_proj(attn, gm, h, ga, gg, gpost, wb, layer):
    t = h.shape[0]
    tm = _tile(t, 512)

    def body(a_ref, m_ref, h_ref, ga_ref, gg_ref, gp_ref, w_ref, heads_ref, mix_ref, h1_ref):
        ha = _rms_fwd(a_ref[...].astype(F32), ga_ref[...]).astype(BF16)
        hg = _rms_fwd(m_ref[...].astype(F32), gg_ref[...]).astype(BF16)
        heads_ref[:, 0:512] = ha
        heads_ref[:, 512:1024] = hg
        mix = _dot(ha, w_ref[0:512, :]) + _dot(hg, w_ref[512:1024, :])
        mix_ref[...] = mix.astype(BF16)
        h1_ref[...] = h_ref[...] + _rms_fwd(mix, gp_ref[...])

    return pl.pallas_call(
        body, name="out_proj", grid=(t // tm,),
        in_specs=[_row_spec(tm, ATTN_W), _row_spec(tm, GM_W), _row_spec(tm, D_MODEL), _vec_spec(ATTN_W),
                  _vec_spec(GM_W), _vec_spec(D_MODEL), _seg_spec(ROWS_B, D_MODEL, 2 * layer)],
        out_specs=[_row_spec(tm, D_MODEL), _row_spec(tm, D_MODEL), _row_spec(tm, D_MODEL)],
        out_shape=[jax.ShapeDtypeStruct((t, D_MODEL), BF16), jax.ShapeDtypeStruct((t, D_MODEL), BF16),
                   jax.ShapeDtypeStruct((t, D_MODEL), F32)],
        compiler_params=_params("parallel"),
    )(attn, gm, h, ga, gg, gpost, wb)


def _ffn_fwd(h1, p, p_layer, gpre, gpost, gple, wa, wb, wp, layer, target=None):
    t = h1.shape[0]
    tm = _tile(t, 256)

    def body(h_ref, p_ref, gpre_ref, gpost_ref, gple_ref, wg_ref, wu_ref, wd_ref, wpg_ref, wpl_ref, *rest):
        if target is None:
            f_ref, gp_ref, up_ref, act_ref, fo_ref, h2_ref, hn_ref, gate_ref, h3_ref = rest
        else:
            t_ref, f_ref, gp_ref, up_ref, act_ref, fo_ref, h2_ref, hn_ref, gate_ref, dy_ref, l_ref = rest
            _zero_at(pl.program_id(0) == 0, l_ref)
        h = h_ref[...]
        pe = _dot_nt(p_ref[...].astype(BF16), wpl_ref[...])
        f = _rms_fwd(h, gpre_ref[...]).astype(BF16)
        f_ref[...] = f
        chunks = [slice(j * FF_CHUNK, (j + 1) * FF_CHUNK) for j in range(D_FF // FF_CHUNK)]
        fo = None
        gp, up = _dot_nt(f, wg_ref[chunks[0], :]), _dot_nt(f, wu_ref[chunks[0], :])
        for j, cols in enumerate(chunks):
            if j + 1 < len(chunks):
                gp_next, up_next = _dot_nt(f, wg_ref[chunks[j + 1], :]), _dot_nt(f, wu_ref[chunks[j + 1], :])
            act = (gp * _sigmoid(gp) * up).astype(BF16)
            gp_ref[:, cols] = gp.astype(BF16)
            up_ref[:, cols] = up.astype(BF16)
            act_ref[:, cols] = act
            part = _dot(act, wd_ref[cols, :])
            fo = part if fo is None else fo + part
            if j + 1 < len(chunks):
                gp, up = gp_next, up_next
        fo_ref[...] = fo
        h2 = h + _rms_fwd(fo, gpost_ref[...])
        h2_ref[...] = h2
        hn = _rms_fwd(h2, gple_ref[...]).astype(BF16)
        hn_ref[...] = hn
        gate = _sigmoid(_dot(hn, wpg_ref[...]))
        gate_ref[...] = gate.astype(BF16)
        h3 = h2 + pe * gate
        if target is None:
            h3_ref[...] = h3
        else:
            e = h3 - t_ref[...]
            dy_ref[...] = (e * (1.0 / D_MODEL)).astype(BF16)
            s = jnp.sum(jnp.sum(e * e, axis=1, keepdims=True), axis=0, keepdims=True)
            l_ref[...] += jnp.broadcast_to(s, (1, 128))

    wide = _row_spec(tm, D_FF)
    row = _row_spec(tm, D_MODEL)
    vec = _vec_spec(D_MODEL)
    last = target is not None
    return pl.pallas_call(
        body, name="ffn_loss" if last else "ffn_fwd", grid=(t // tm,),
        in_specs=[row, _layer_row_spec(tm, PLE_DIM, p_layer), vec, vec, vec, _seg_spec(ROWS_A, D_MODEL, 3 * layer),
                  _seg_spec(ROWS_A, D_MODEL, 3 * layer + 1), _seg_spec(ROWS_A, D_MODEL, 3 * layer + 2),
                  _seg_spec(ROWS_B, D_MODEL, 2 * layer + 1), _seg_spec(ROWS_B, PLE_DIM, layer)] + [row] * last,
        out_specs=[row, wide, wide, wide, row, row, row, row, row] + [_vec_spec(128)] * last,
        out_shape=[jax.ShapeDtypeStruct((t, D_MODEL), BF16)] + [jax.ShapeDtypeStruct((t, D_FF), BF16)] * 3
        + [jax.ShapeDtypeStruct((t, D_MODEL), F32)] * 2 + [jax.ShapeDtypeStruct((t, D_MODEL), BF16)] * 2
        + [jax.ShapeDtypeStruct((t, D_MODEL), BF16 if last else F32)] + [jax.ShapeDtypeStruct((1, 128), F32)] * last,
        compiler_params=_params("arbitrary" if last else "parallel"),
    )(h1, p, gpre, gpost, gple, wa, wa, wa, wb, wp, *([target] if last else []))


def _loss_head(y, target):
    t = y.shape[0]
    tm = _tile(t, 512)

    def body(y_ref, t_ref, dy_ref, l_ref):
        _zero_at(pl.program_id(0) == 0, l_ref)
        e = y_ref[...] - t_ref[...]
        dy_ref[...] = e * (1.0 / D_MODEL)
        s = jnp.sum(jnp.sum(e * e, axis=1, keepdims=True), axis=0, keepdims=True)
        l_ref[...] += jnp.broadcast_to(s, (1, 128))

    return pl.pallas_call(
        body, name="loss_head", grid=(t // tm,),
        in_specs=[_row_spec(tm, D_MODEL), _row_spec(tm, D_MODEL)],
        out_specs=[_row_spec(tm, D_MODEL), _vec_spec(128)],
        out_shape=[jax.ShapeDtypeStruct((t, D_MODEL), F32), jax.ShapeDtypeStruct((1, 128), F32)],
        compiler_params=_params("arbitrary"),
    )(y, target)


def _ffn_bwd(dh3, h2, gate, p, p_layer, fo, gp, up, h1, gple, gpost, gpre, wa, wb, wp, layer, after):
    t = dh3.shape[0]
    tm = _tile(t, 256)

    def body(d3_ref, h2_ref, gate_ref, p_ref, fo_ref, gp_ref, up_ref, h_ref, gple_ref, gpost_ref, gpre_ref,
             wg_ref, wu_ref, wd_ref, wpg_ref, wpl_ref, after_ref,
             dgl_ref, dpe_ref, dfo_ref, dgp_ref, dup_ref, dh1_ref, dgple_ref, dgpost_ref, dgpre_ref):
        _zero_at(pl.program_id(0) == 0, dgple_ref, dgpost_ref, dgpre_ref)
        d3 = d3_ref[...].astype(F32)
        gate = gate_ref[...].astype(F32)
        pe = _dot_nt(p_ref[...].astype(BF16), wpl_ref[...])
        dpe_ref[...] = (d3 * gate).astype(BF16)
        dgl = (d3 * pe * gate * (1.0 - gate)).astype(BF16)
        dgl_ref[...] = dgl
        dx2, dgple = _rms_bwd(h2_ref[...], gple_ref[...], _dot_nt(dgl, wpg_ref[...]))
        dgple_ref[...] += dgple
        d = d3 + dx2
        dfo, dgpost = _rms_bwd(fo_ref[...], gpost_ref[...], d)
        dfo = dfo.astype(BF16)
        dfo_ref[...] = dfo
        dgpost_ref[...] += dgpost
        chunks = [slice(j * FF_CHUNK, (j + 1) * FF_CHUNK) for j in range(D_FF // FF_CHUNK)]
        df = None
        dact = _dot_nt(dfo, wd_ref[chunks[0], :])
        for j, cols in enumerate(chunks):
            if j + 1 < len(chunks):
                dact_next = _dot_nt(dfo, wd_ref[chunks[j + 1], :])
            gp = gp_ref[:, cols].astype(F32)
            sg = _sigmoid(gp)
            dgp = (dact * up_ref[:, cols].astype(F32) * (sg * (1.0 + gp * (1.0 - sg)))).astype(BF16)
            dup = (dact * (gp * sg)).astype(BF16)
            dgp_ref[:, cols] = dgp
            dup_ref[:, cols] = dup
            part = _dot(dgp, wg_ref[cols, :]) + _dot(dup, wu_ref[cols, :])
            df = part if df is None else df + part
            if j + 1 < len(chunks):
                dact = dact_next
        dx, dgpre = _rms_bwd(h_ref[...], gpre_ref[...], df)
        dh1_ref[...] = (d + dx).astype(BF16)
        dgpre_ref[...] += dgpre

    wide = _row_spec(tm, D_FF)
    row = _row_spec(tm, D_MODEL)
    vec = _vec_spec(D_MODEL)
    narrow = jax.ShapeDtypeStruct((t, D_MODEL), BF16)
    return pl.pallas_call(
        body, name="ffn_bwd", grid=(t // tm,),
        in_specs=[row, row, row, _layer_row_spec(tm, PLE_DIM, p_layer), row, wide, wide, row, vec, vec, vec,
                  _seg_spec(ROWS_A, D_MODEL, 3 * layer), _seg_spec(ROWS_A, D_MODEL, 3 * layer + 1),
                  _seg_spec(ROWS_A, D_MODEL, 3 * layer + 2), _seg_spec(ROWS_B, D_MODEL, 2 * layer + 1),
                  _seg_spec(ROWS_B, PLE_DIM, layer), pl.BlockSpec(memory_space=pl.ANY)],
        out_specs=[row, row, row, wide, wide, row, vec, vec, vec],
        out_shape=[narrow, narrow, narrow, jax.ShapeDtypeStruct((t, D_FF), BF16), jax.ShapeDtypeStruct((t, D_FF), BF16),
                   narrow] + [jax.ShapeDtypeStruct((1, D_MODEL), F32)] * 3,
        compiler_params=_params("arbitrary"),
    )(dh3, h2, gate, p, fo, gp, up, h1, gple, gpost, gpre, wa, wa, wa, wb, wp, after)


def _out_proj_bwd(dh1, mix, attn, gm, gpost, ga, gg, wb, layer, after):
    t = dh1.shape[0]
    tm = _tile(t, 512)

    def body(d_ref, mix_ref, a_ref, m_ref, gp_ref, ga_ref, gg_ref, w_ref, after_ref,
             dmix_ref, da_ref, dm_ref, dgp_ref, dga_ref, dgg_ref):
        _zero_at(pl.program_id(0) == 0, dgp_ref, dga_ref, dgg_ref)
        dmix, dgp = _rms_bwd(mix_ref[...].astype(F32), gp_ref[...], d_ref[...].astype(F32))
        dmix = dmix.astype(BF16)
        dmix_ref[...] = dmix
        da, dga = _rms_bwd(a_ref[...].astype(F32), ga_ref[...], _dot_nt(dmix, w_ref[0:512, :]))
        dm, dgg = _rms_bwd(m_ref[...].astype(F32), gg_ref[...], _dot_nt(dmix, w_ref[512:1024, :]))
        da_ref[...] = da.astype(BF16)
        dm_ref[...] = dm
        dgp_ref[...] += dgp
        dga_ref[...] += dga
        dgg_ref[...] += dgg

    return pl.pallas_call(
        body, name="out_proj_bwd", grid=(t // tm,),
        in_specs=[_row_spec(tm, D_MODEL), _row_spec(tm, D_MODEL), _row_spec(tm, ATTN_W), _row_spec(tm, GM_W),
                  _vec_spec(D_MODEL), _vec_spec(ATTN_W), _vec_spec(GM_W), _seg_spec(ROWS_B, D_MODEL, 2 * layer),
                  pl.BlockSpec(memory_space=pl.ANY)],
        out_specs=[_row_spec(tm, D_MODEL), _row_spec(tm, ATTN_W), _row_spec(tm, GM_W),
                   _vec_spec(D_MODEL), _vec_spec(ATTN_W), _vec_spec(GM_W)],
        out_shape=[jax.ShapeDtypeStruct((t, D_MODEL), BF16), jax.ShapeDtypeStruct((t, ATTN_W), BF16),
                   jax.ShapeDtypeStruct((t, GM_W), F32), jax.ShapeDtypeStruct((1, D_MODEL), F32),
                   jax.ShapeDtypeStruct((1, ATTN_W), F32), jax.ShapeDtypeStruct((1, GM_W), F32)],
        compiler_params=_params("arbitrary"),
    )(dh1, mix, attn, gm, gpost, ga, gg, wb, after)


def _split3(x):
    hi = x.astype(BF16)
    r1 = x - hi.astype(F32)
    mid = r1.astype(BF16)
    lo = (r1 - mid.astype(F32)).astype(BF16)
    return hi, mid, lo


def _gm_bwd(dgm, zu, zv, lng, lnb, wtril, bsx):
    t = zu.shape[0]
    tm = _tile(t, 512)
    nb = t // tm

    def body(d_ref, zu_ref, zv_ref, g_ref, b_ref, w_ref, bs_ref,
             dzu_ref, dzv_ref, dw_ref, dbs_ref, dlg_ref, dlb_ref, dbsx_ref):
        i = pl.program_id(0)
        _zero_at(i == 0, dw_ref, dlg_ref, dlb_ref, dbsx_ref)
        row = lax.broadcasted_iota(jnp.int32, (BLK, BLK), 0)
        lane = lax.broadcasted_iota(jnp.int32, (BLK, BLK), 1)
        low = lane < 64
        tril = row >= lane
        lng = g_ref[...]
        for b in range(tm // BLK):
            rows = slice(b * BLK, (b + 1) * BLK)
            zu = zu_ref[rows, :]
            zv = zv_ref[rows, :]
            gu, ln, xn, rstd, mixed = _gm_forward_block(zu, zv, lng, b_ref[...], w_ref, bs_ref[...], low)
            dgm = d_ref[rows, :]
            dgu_cols, dmx_cols, dln_cols = [], [], []
            for col in range(4):
                sl = slice(col * 128, (col + 1) * 128)
                dg = dgm[:, sl]
                dgu_cols.append(dg * mixed[col])
                dmx = dg * gu[:, sl]
                dmx_cols.append(dmx)
                lc = ln[:, sl]
                halves = (jnp.where(low, lc, 0.0).astype(BF16), jnp.where(low, 0.0, lc).astype(BF16))
                dmx16 = dmx.astype(BF16)
                dmx_half = (jnp.where(low, dmx, 0.0).astype(BF16), jnp.where(low, 0.0, dmx).astype(BF16))
                dln = None
                for half in range(2):
                    hd = 2 * col + half
                    dw_ref[hd] += jnp.where(tril, _dot_nt(dmx16, halves[half]), 0.0)
                    part = _dot_tn(w_ref[hd], dmx_half[half])
                    dln = part if dln is None else dln + part
                dln_cols.append(dln)
            dgu = jnp.concatenate(dgu_cols, axis=1)
            dmx = jnp.concatenate(dmx_cols, axis=1)
            dln = jnp.concatenate(dln_cols, axis=1)
            dzu_ref[rows, :] = (dgu * _gelu_grad(zu)).astype(BF16)
            dbsx_ref[...] += dmx
            dlg_ref[...] += jnp.sum(dln * xn, axis=0, keepdims=True)
            dlb_ref[...] += jnp.sum(dln, axis=0, keepdims=True)
            dxn = dln * lng
            dgv = rstd * (dxn - jnp.mean(dxn, axis=-1, keepdims=True) - xn * jnp.mean(dxn * xn, axis=-1, keepdims=True))
            dzv_ref[rows, :] = (dgv * _gelu_grad(zv)).astype(BF16)

        @pl.when(i == nb - 1)
        def _():
            r = lax.broadcasted_iota(jnp.int32, (GM_W, BLK), 0)
            c = lax.broadcasted_iota(jnp.int32, (GM_W, BLK), 1)
            e = jnp.where(jnp.logical_and(r >= c * 64, r < c * 64 + 64), 1.0, 0.0).astype(BF16)
            hi, mid, lo = _split3(dbsx_ref[...])
            dbs_ref[...] = _dot(hi, e) + _dot(mid, e) + _dot(lo, e)

    vec = _vec_spec(GM_W)
    return pl.pallas_call(
        body, name="gm_bwd", grid=(nb,),
        in_specs=[_row_spec(tm, GM_W)] * 3 + [vec, vec, pl.BlockSpec((8, BLK, BLK), lambda i: (0, 0, 0)),
                                              pl.BlockSpec((BLK, GM_W), lambda i: (0, 0))],
        out_specs=[_row_spec(tm, GM_W), _row_spec(tm, GM_W), pl.BlockSpec((8, BLK, BLK), lambda i: (0, 0, 0)),
                   pl.BlockSpec((BLK, BLK), lambda i: (0, 0)), vec, vec],
        out_shape=[jax.ShapeDtypeStruct((t, GM_W), BF16), jax.ShapeDtypeStruct((t, GM_W), BF16),
                   jax.ShapeDtypeStruct((8, BLK, BLK), F32), jax.ShapeDtypeStruct((BLK, BLK), F32),
                   jax.ShapeDtypeStruct((1, GM_W), F32), jax.ShapeDtypeStruct((1, GM_W), F32)],
        scratch_shapes=[pltpu.VMEM((BLK, GM_W), F32)],
        compiler_params=_params("arbitrary"),
    )(dgm, zu, zv, lng, lnb, wtril, bsx)


def _attn_bwd(q, kv, do, sinks, after):
    t = q.shape[0]
    tq = _tile(t, 512)
    blocks = tq // BLK

    def body(sink_ref, q_ref, kvc_ref, kvp_ref, do_ref, after_ref, dq_ref, dkv_ref, dkf_ref, ds_ref):
        i = pl.program_id(0)
        _zero_at(i == 0, ds_ref)
        low, vcur, dist, valid = _attn_masks(i)
        head_row = lax.broadcasted_iota(jnp.int32, (8, 128), 0)
        dsink_tile = jnp.zeros((8, 128), F32)
        kp, vp = _kv_variants(kvp_ref, slice(None), low)
        own = None
        for b in range(blocks):
            rows = slice(b * BLK, (b + 1) * BLK)
            kc, vc = _kv_variants(kvc_ref, rows, low)
            scores = _attn_scores_by_head(q_ref, rows, kc, kp, vcur)
            dprobs = _attn_scores_by_head(do_ref, rows, vc, vp, vcur)
            parts = []
            for h in range(8):
                p, ps = _attn_probs(scores[h], h, sink_ref[h], dist, valid if b == 0 else None)
                delta = jnp.sum(p * dprobs[h], axis=1, keepdims=True)
                ds = p * (dprobs[h] - delta) * ATTN_SCALE
                dsink = jnp.sum(-ps * delta, axis=0, keepdims=True)
                dsink_tile = jnp.where(head_row == h, dsink_tile + dsink, dsink_tile)
                parts.append(_split_cols(ds, vcur) + _split_cols(p, vcur))
            acc = {}

            def add(name, key, val):
                acc[(name, key)] = val if (name, key) not in acc else acc[(name, key)] + val

            for col in range(4):
                qh = q_ref[rows, col * 128:(col + 1) * 128]
                doh = do_ref[rows, col * 128:(col + 1) * 128]
                dq = None
                for half in range(2):
                    key = _head_key(2 * col + half)
                    dsp, dsc, pp, pc = parts[2 * col + half]
                    part = _dot(dsc, kc[key]) + _dot(dsp, kp[key])
                    dq = part if dq is None else dq + part
                    add("kc", key, _dot_tn(dsc, qh))
                    add("kp", key, _dot_tn(dsp, qh))
                    add("vc", key, _dot_tn(pc, doh))
                    add("vp", key, _dot_tn(pp, doh))
                dq_ref[rows, col * 128:(col + 1) * 128] = dq.astype(BF16)

            def place(name):
                head0 = acc[(name, (0, 0))] + pltpu.roll(acc[(name, (0, 1))], 64, axis=1)
                head1 = pltpu.roll(acc[(name, (1, 0))], 64, axis=1) + acc[(name, (1, 1))]
                return jnp.where(low, head0, head1)

            before = (place("kp"), place("vp"))
            if b == 0:
                dkf_ref[:, 0:128], dkf_ref[:, 128:256] = before
            else:
                last = slice((b - 1) * BLK, b * BLK)
                dkv_ref[last, 0:128] = own[0] + before[0]
                dkv_ref[last, 128:256] = own[1] + before[1]
            own = (place("kc"), place("vc"))
            kp, vp = kc, vc
        final = slice((blocks - 1) * BLK, blocks * BLK)
        dkv_ref[final, 0:128], dkv_ref[final, 128:256] = own
        ds_ref[...] += dsink_tile

    row_q = _row_spec(tq, ATTN_W)
    row_kv = _row_spec(tq, 2 * KV_W)
    return pl.pallas_call(
        body, name="attn_bwd", grid=(t // tq,),
        in_specs=[pl.BlockSpec(memory_space=pltpu.SMEM), row_q, row_kv, _kv_prev_spec(blocks), row_q,
                  pl.BlockSpec(memory_space=pl.ANY)],
        out_specs=[row_q, row_kv, _row_spec(BLK, 2 * KV_W), pl.BlockSpec((8, 128), lambda i: (0, 0))],
        out_shape=[jax.ShapeDtypeStruct((t, ATTN_W), BF16), jax.ShapeDtypeStruct((t, 2 * KV_W), F32),
                   jax.ShapeDtypeStruct((t // tq * BLK, 2 * KV_W), F32), jax.ShapeDtypeStruct((8, 128), F32)],
        compiler_params=_params("arbitrary"),
    )(sinks, q, kv, kv, do, after)


def _in_proj_bwd(dq, dkv, dkf, dzu, dzv, h, dres, g, wc, layer, dh_dtype):
    t = h.shape[0]
    tm = _tile(t, 512)
    steps = t // tm

    def body(dq_ref, dkv_ref, dkn_ref, dzu_ref, dzv_ref, h_ref, d_ref, g_ref, w_ref, dz_ref, dh_ref, dg_ref):
        i = pl.program_id(0)
        _zero_at(i == 0, dg_ref)
        dq = dq_ref[...]
        tail = dkv_ref[tm - BLK:tm, :] + jnp.where(i < steps - 1, dkn_ref[...], 0.0)
        dkv = tail if tm == BLK else jnp.concatenate([dkv_ref[0:tm - BLK, :], tail], axis=0)
        dkv = dkv.astype(BF16)
        dzu = dzu_ref[...]
        dzv = dzv_ref[...]
        dz_ref[:, 0:512] = dq
        dz_ref[:, 512:768] = dkv
        dz_ref[:, 768:1280] = dzu
        dz_ref[:, 1280:1792] = dzv
        da = (_dot(dq, w_ref[0:512, :]) + _dot(dkv, w_ref[512:768, :]) + _dot(dzu, w_ref[768:1280, :])
              + _dot(dzv, w_ref[1280:1792, :]))
        dx, dg = _rms_bwd(h_ref[...], g_ref[...], da)
        dh_ref[...] = (d_ref[...].astype(F32) + dx).astype(dh_dtype)
        dg_ref[...] += dg

    return pl.pallas_call(
        body, name="in_proj_bwd", grid=(t // tm,),
        in_specs=[_row_spec(tm, ATTN_W), _row_spec(tm, 2 * KV_W),
                  pl.BlockSpec((BLK, 2 * KV_W), lambda i: (jnp.minimum(i + 1, steps - 1), 0)), _row_spec(tm, GM_W),
                  _row_spec(tm, GM_W), _row_spec(tm, D_MODEL), _row_spec(tm, D_MODEL), _vec_spec(D_MODEL),
                  _seg_spec(ROWS_C, D_MODEL, layer)],
        out_specs=[_row_spec(tm, D_IN), _row_spec(tm, D_MODEL), _vec_spec(D_MODEL)],
        out_shape=[jax.ShapeDtypeStruct((t, D_IN), BF16), jax.ShapeDtypeStruct((t, D_MODEL), dh_dtype),
                   jax.ShapeDtypeStruct((1, D_MODEL), F32)],
        compiler_params=_params("arbitrary"),
    )(dq, dkv, dkf, dzu, dzv, h, dres, g, wc)


def _weight_grad(a, b, buf, seg, b_layer=None):
    t, m = a.shape
    n = b.shape[-1]
    assert buf.shape[0] % m == 0 and buf.shape[1] == n
    tm = _tile(t, WGRAD_TOKENS)
    steps = t // tm
    half = m // 2

    def body(a_ref, b_ref, buf_ref, o_ref, acc_ref):
        i = pl.program_id(0)
        _zero_at(i == 0, acc_ref)
        b16 = b_ref[...].astype(BF16)
        for rows in (slice(0, half), slice(half, m)):
            acc_ref[rows, :] += _dot_tn(a_ref[:, rows], b16)

        @pl.when(i == steps - 1)
        def _():
            o_ref[...] = acc_ref[...].astype(o_ref.dtype)

    return pl.pallas_call(
        body, name="weight_grad", grid=(steps,),
        in_specs=[_row_spec(tm, m), _row_spec(tm, n) if b_layer is None else _layer_row_spec(tm, n, b_layer),
                  pl.BlockSpec(memory_space=pl.ANY)],
        out_specs=pl.BlockSpec((m, n), lambda i: (seg, 0)),
        out_shape=jax.ShapeDtypeStruct(buf.shape, buf.dtype),
        scratch_shapes=[pltpu.VMEM((m, n), F32)],
        input_output_aliases={2: 0},
        compiler_params=_params("arbitrary"),
    )(a, b, buf)


def _rows8(rows):
    return [jnp.pad(r, ((0, 7), (0, 0))) for r in rows]


def _small_pack_gating(d):
    rows = [jnp.concatenate([d["gm_ln_g"], d["gm_ln_b"]], axis=1), d["gm_bs"].reshape(1, 1024)]
    return jnp.concatenate(_rows8(rows) + [d["gm_ws"].reshape(128, 1024)], axis=0)


def _small_pack_rest(d):
    rows = [d["ln_mix_pre"], d["ln_mix_post"], d["ln_ffn_pre"], d["ln_ffn_post"], d["ln_ple_gate"],
            jnp.concatenate([d["g_attn_out"], d["g_gm_out"]], axis=1),
            jnp.pad(d["attn_sinks"].reshape(1, 8), ((0, 0), (0, 1016)))]
    return jnp.concatenate(_rows8(rows), axis=0)


def _small_unpack(g, s):
    return {
        "gm_ln_g": g[:, 0, :512], "gm_ln_b": g[:, 0, 512:], "gm_bs": g[:, 8].reshape(DEPTH, 8, 128),
        "gm_ws": g[:, 16:144].reshape(DEPTH, 8, 128, 128),
        "ln_mix_pre": s[:, 0], "ln_mix_post": s[:, 8], "ln_ffn_pre": s[:, 16], "ln_ffn_post": s[:, 24],
        "ln_ple_gate": s[:, 32], "g_attn_out": s[:, 40, :512], "g_gm_out": s[:, 40, 512:], "attn_sinks": s[:, 48, :8],
    }


def _row(v):
    return v.reshape(1, -1)


def _layer_fwd(h, p, sp, l, weights, target=None):
    tril = jnp.tril(jnp.ones((BLK, BLK), bool))
    wtril = jnp.where(tril[None], sp["gm_ws"][l], 0.0).astype(BF16)
    bsx = jnp.repeat(sp["gm_bs"][l].T, HEAD_DIM, axis=1)
    a, q, kv, zu, zv = _in_proj(h, _row(sp["ln_mix_pre"][l]), weights("c", h), 0)
    attn = _attn_fwd(q, kv, sp["attn_sinks"][l])
    gm = _gm_fwd(zu, zv, _row(sp["gm_ln_g"][l]), _row(sp["gm_ln_b"][l]), wtril, bsx)
    wb = weights("b", gm)
    heads, mix, h1 = _out_proj(attn, gm, h, _row(sp["g_attn_out"][l]), _row(sp["g_gm_out"][l]),
                               _row(sp["ln_mix_post"][l]), wb, 0)
    wa = weights("a", h1)
    f, gpre, up, act, fo, h2, hn, gate, *out = _ffn_fwd(
        h1, p, l, _row(sp["ln_ffn_pre"][l]), _row(sp["ln_ffn_post"][l]), _row(sp["ln_ple_gate"][l]), wa, wb,
        weights("p", gm), 0, target)
    saved = dict(h=h, a=a, q=q, kv=kv, zu=zu, zv=zv, attn=attn, gm=gm, heads=heads, mix=mix, h1=h1, f=f,
                 gpre=gpre, up=up, act=act, fo=fo, h2=h2, hn=hn, gate=gate, wtril=wtril, bsx=bsx)
    return (out[0] if target is None else tuple(out)), saved


def _layer_bwd_upper(dh, s, p, sp, l, wa, wb, wp, after):
    d = {}
    dgl, dpe, dfo, dgp, dup, dh1, d["ln_ple_gate"], d["ln_ffn_post"], d["ln_ffn_pre"] = _ffn_bwd(
        dh, s["h2"], s["gate"], p, l, s["fo"], s["gpre"], s["up"], s["h1"], _row(sp["ln_ple_gate"][l]),
        _row(sp["ln_ffn_post"][l]), _row(sp["ln_ffn_pre"][l]), wa, wb, wp, 0, after)
    gb = _weight_grad(s["hn"], dgl, lax.empty((2 * D_MODEL, D_MODEL), BF16), 1)
    gp = _weight_grad(dpe, p, lax.empty((D_MODEL, PLE_DIM), BF16), 0, b_layer=l)
    ga = _weight_grad(s["act"], dfo, lax.empty((3 * D_FF, D_MODEL), BF16), 2)
    ga = _weight_grad(dgp, s["f"], ga, 0)
    ga = _weight_grad(dup, s["f"], ga, 1)
    return (dh1, d), ga, gp, gb


def _layer_bwd_middle(carry, s, sp, l, wb, gb, after):
    dh1, d = carry
    dmix, dattn, dgm, d["ln_mix_post"], d["g_attn_out"], d["g_gm_out"] = _out_proj_bwd(
        dh1, s["mix"], s["attn"], s["gm"], _row(sp["ln_mix_post"][l]), _row(sp["g_attn_out"][l]),
        _row(sp["g_gm_out"][l]), wb, 0, after)
    gb = _weight_grad(s["heads"], dmix, gb, 0)
    dzu, dzv, d["gm_ws"], dbs, d["gm_ln_g"], d["gm_ln_b"] = _gm_bwd(
        dgm, s["zu"], s["zv"], _row(sp["gm_ln_g"][l]), _row(sp["gm_ln_b"][l]), s["wtril"], s["bsx"])
    d["gm_bs"] = dbs[:, :8].T
    return (dh1, dattn, dzu, dzv, d), gb, _small_pack_gating(d)


def _layer_bwd_lower(carry, s, sp, l, wc, after):
    dh1, dattn, dzu, dzv, d = carry
    dq, dkv, dkf, dsink = _attn_bwd(s["q"], s["kv"], dattn, sp["attn_sinks"][l], after)
    d["attn_sinks"] = dsink[:, 0]
    dz, dh, d["ln_mix_pre"] = _in_proj_bwd(dq, dkv, dkf, dzu, dzv, s["h"], dh1, _row(sp["ln_mix_pre"][l]), wc, 0,
                                           F32 if l == 0 else BF16)
    gc = _weight_grad(dz, s["a"], lax.empty((D_IN, D_MODEL), BF16), 0)
    return dh, gc, _small_pack_rest(d)


ANY = pl.BlockSpec(memory_space=pl.ANY)


def _place():
    x, y, c = lax.axis_index("x"), lax.axis_index("y"), lax.axis_index("c")
    chips = [(1 - x, y), (x, 1 - y), (1 - x, 1 - y)]
    return x, y, c, chips


def _all_gather(shards):
    n = len(shards)

    def body(*refs):
        ins, outs = refs[:n], refs[n:2 * n]
        send_sems, recv_sems, local_sems = refs[2 * n:]
        x, y, c, chips = _place()
        me, sibling = (x, y, c), (x, y, 1 - c)

        def block(k, px, py, pc):
            return outs[k].at[:, pl.ds(4 * px + 2 * py + pc, 1)]

        def copy(k, j, who, to, src=None):
            return pltpu.make_async_remote_copy(
                src_ref=block(k, *who) if src is None else src, dst_ref=block(k, *who),
                send_sem=send_sems.at[7 * k + j], recv_sem=recv_sems.at[7 * k + j],
                device_id=to, device_id_type=MESH)

        mine = [pltpu.make_async_copy(ins[k], block(k, *me), local_sems.at[k]) for k in range(n)]
        for cp in mine:
            cp.start()
        first = []
        for k in range(n):
            first.append(copy(k, 0, me, sibling, src=ins[k]))
            first += [copy(k, 1 + j, me, (*chip, c), src=ins[k]) for j, chip in enumerate(chips)]
        for cp in first:
            cp.start()
        passed = []
        for j, chip in enumerate(chips):
            for k in range(n):
                copy(k, 1 + j, (*chip, c), me).wait_recv()
                cp = copy(k, 4 + j, (*chip, c), sibling)
                cp.start()
                passed.append(cp)
        for k in range(n):
            copy(k, 0, sibling, me).wait_recv()
            for j, chip in enumerate(chips):
                copy(k, 4 + j, (*chip, 1 - c), me).wait_recv()
        for cp in first + passed:
            cp.wait_send()
        for cp in mine:
            cp.wait()

    return pl.pallas_call(
        body, name="all_gather_weights",
        in_specs=[ANY] * n, out_specs=[ANY] * n,
        out_shape=[jax.ShapeDtypeStruct((s.shape[0], N_DEV) + s.shape[2:], s.dtype) for s in shards],
        scratch_shapes=[pltpu.SemaphoreType.DMA((7 * n,)), pltpu.SemaphoreType.DMA((7 * n,)),
                        pltpu.SemaphoreType.DMA((n,))],
        compiler_params=pltpu.CompilerParams(has_side_effects=True),
    )(*shards)


def _sibling_exchange(bufs, small):
    n = len(bufs)

    def body(*refs):
        ins, small_ref = refs[:n], refs[n]
        outs, both_ref = refs[n + 1:2 * n + 1], refs[2 * n + 1]
        send_sems, recv_sems, local_sem = refs[2 * n + 2:]
        x, y, c, _ = _place()
        sibling = (x, y, 1 - c)
        mine = pltpu.make_async_copy(small_ref, both_ref.at[c], local_sem)
        mine.start()
        copies = [pltpu.make_async_remote_copy(
            src_ref=ins[k].at[:, :, pl.ds(1 - c, 1)], dst_ref=outs[k], send_sem=send_sems.at[k],
            recv_sem=recv_sems.at[k], device_id=sibling, device_id_type=MESH) for k in range(n)]
        copies.append(pltpu.make_async_remote_copy(
            src_ref=small_ref, dst_ref=both_ref.at[c], send_sem=send_sems.at[n], recv_sem=recv_sems.at[n],
            device_id=sibling, device_id_type=MESH))
        for cp in copies:
            cp.start()
        for k in range(n):
            copies[k].wait_recv()
        pltpu.make_async_remote_copy(
            src_ref=small_ref, dst_ref=both_ref.at[1 - c], send_sem=send_sems.at[n], recv_sem=recv_sems.at[n],
            device_id=sibling, device_id_type=MESH).wait_recv()
        for cp in copies:
            cp.wait_send()
        mine.wait()

    return pl.pallas_call(
        body, name="sibling_exchange",
        in_specs=[ANY] * (n + 1), out_specs=[ANY] * (n + 1),
        out_shape=[jax.ShapeDtypeStruct(b.shape[:2] + (1,) + b.shape[3:], b.dtype) for b in bufs]
        + [jax.ShapeDtypeStruct((2,) + small.shape, small.dtype)],
        scratch_shapes=[pltpu.SemaphoreType.DMA((n + 1,)), pltpu.SemaphoreType.DMA((n + 1,)), pltpu.SemaphoreType.DMA],
        compiler_params=pltpu.CompilerParams(has_side_effects=True),
    )(*bufs, small)


def _chip_exchange(sends, small):
    n = len(sends)

    def body(*refs):
        ins, small_ref = refs[:n], refs[n]
        outs, all_ref = refs[n + 1:2 * n + 1], refs[2 * n + 1]
        send_sems, recv_sems, local_sem = refs[2 * n + 2:]
        x, y, c, chips = _place()
        mine = pltpu.make_async_copy(small_ref, all_ref.at[2 * x + y], local_sem)
        mine.start()
        copies = []
        for j, chip in enumerate(chips):
            for k in range(n):
                copies.append(pltpu.make_async_remote_copy(
                    src_ref=ins[k].at[j], dst_ref=outs[k].at[j], send_sem=send_sems.at[3 * k + j],
                    recv_sem=recv_sems.at[3 * k + j], device_id=(*chip, c), device_id_type=MESH))
            copies.append(pltpu.make_async_remote_copy(
                src_ref=small_ref, dst_ref=all_ref.at[2 * x + y], send_sem=send_sems.at[3 * n + j],
                recv_sem=recv_sems.at[3 * n + j], device_id=(*chip, c), device_id_type=MESH))
        for cp in copies:
            cp.start()
        for j, (px, py) in enumerate(chips):
            for k in range(n):
                copies[j * (n + 1) + k].wait_recv()
            pltpu.make_async_remote_copy(
                src_ref=small_ref, dst_ref=all_ref.at[2 * px + py], send_sem=send_sems.at[3 * n + j],
                recv_sem=recv_sems.at[3 * n + j], device_id=(px, py, c), device_id_type=MESH).wait_recv()
        for cp in copies:
            cp.wait_send()
        mine.wait()

    return pl.pallas_call(
        body, name="chip_exchange",
        in_specs=[ANY] * (n + 1), out_specs=[ANY] * (n + 1),
        out_shape=[jax.ShapeDtypeStruct(s.shape, s.dtype) for s in sends]
        + [jax.ShapeDtypeStruct((4,) + small.shape, small.dtype)],
        scratch_shapes=[pltpu.SemaphoreType.DMA((3 * n + 3,)), pltpu.SemaphoreType.DMA((3 * n + 3,)),
                        pltpu.SemaphoreType.DMA],
        compiler_params=pltpu.CompilerParams(has_side_effects=True),
    )(*sends, small)


def _pair_add(buf, got, chip_ids, dtype):
    nseg, _, _, rows, cols = buf.shape
    nr = chip_ids.shape[0] - 1

    def body(ids_ref, a_ref, b_ref, o_ref):
        o_ref[...] = (a_ref[...] + b_ref[...]).astype(dtype)

    return pl.pallas_call(
        body, name="pair_add",
        grid_spec=pltpu.PrefetchScalarGridSpec(
            num_scalar_prefetch=1, grid=(nr, nseg),
            in_specs=[pl.BlockSpec((None, None, None, rows, cols), lambda r, s, ids: (s, ids[r], ids[nr], 0, 0)),
                      pl.BlockSpec((None, None, None, rows, cols), lambda r, s, ids: (s, ids[r], 0, 0, 0))],
            out_specs=pl.BlockSpec((None, None, rows, cols), lambda r, s, ids: (r, s, 0, 0))),
        out_shape=jax.ShapeDtypeStruct((nr, nseg, rows, cols), dtype),
        compiler_params=_params("parallel", "parallel"),
    )(chip_ids, buf, got)


def _sum_slots(z, tr):
    n, rows, cols = z.shape

    def body(z_ref, o_ref):
        s = z_ref[0]
        for k in range(1, n):
            s = s + z_ref[k]
        o_ref[...] = s

    return pl.pallas_call(
        body, name="sum_slots", grid=(rows // tr,),
        in_specs=[pl.BlockSpec((n, tr, cols), lambda i: (0, i, 0))],
        out_specs=pl.BlockSpec((tr, cols), lambda i: (i, 0)),
        out_shape=jax.ShapeDtypeStruct((rows, cols), F32),
        compiler_params=_params("parallel"),
    )(z)


def _final_sum(own, got):
    _, nseg, rows, cols = own.shape

    def body(a_ref, b_ref, o_ref):
        s = a_ref[0]
        for k in range(3):
            s = s + b_ref[k].astype(F32)
        o_ref[...] = s

    return pl.pallas_call(
        body, name="final_sum", grid=(nseg,),
        in_specs=[pl.BlockSpec((1, None, rows, cols), lambda i: (0, i, 0, 0)),
                  pl.BlockSpec((3, None, rows, cols), lambda i: (0, i, 0, 0))],
        out_specs=pl.BlockSpec((None, rows, cols), lambda i: (i, 0, 0)),
        out_shape=jax.ShapeDtypeStruct((nseg, rows, cols), F32),
        compiler_params=_params("parallel"),
    )(own, got)


HBM = pl.BlockSpec(memory_space=pltpu.HBM)
SEM = pl.BlockSpec(memory_space=pltpu.SEMAPHORE)
N_PEERS = N_DEV - 1


def _peers():
    x, y, c = lax.axis_index("x"), lax.axis_index("y"), lax.axis_index("c")
    peers = []
    for r in range(1, N_DEV):
        px = 1 - x if r & 4 else x
        py = 1 - y if r & 2 else y
        pc = 1 - c if r & 1 else c
        peers.append(((px, py, pc), 4 * px + 2 * py + pc))
    return 4 * x + 2 * y + c, peers


GATHER, SCATTER, SPREAD = "gather", "scatter", "spread"


def _peer_copy(src, land, send_sems, recv_sems, r, me, peer, peer_slot, mode):
    return pltpu.make_async_remote_copy(
        src_ref=src.at[:, pl.ds(peer_slot, 1)] if mode == SCATTER else src,
        dst_ref=land.at[:, pl.ds(me, 1)] if mode == GATHER else land.at[:, pl.ds(r - 1, 1)],
        send_sem=send_sems.at[r - 1], recv_sem=recv_sems.at[r - 1], device_id=peer, device_id_type=MESH)


def _peer_arrival(src, land, send_sems, recv_sems, r, me, peer, peer_slot, mode):
    return pltpu.make_async_remote_copy(
        src_ref=src.at[:, pl.ds(me, 1)] if mode == SCATTER else src,
        dst_ref=land.at[:, pl.ds(peer_slot, 1)] if mode == GATHER else land.at[:, pl.ds(r - 1, 1)],
        send_sem=send_sems.at[r - 1], recv_sem=recv_sems.at[r - 1], device_id=peer, device_id_type=MESH)


def _send_start(name, srcs, lands, modes):
    n = len(srcs)

    def body(*refs):
        src_refs, land_refs = refs[:n], refs[n:2 * n]
        outs = refs[2 * n:]
        send_sems, recv_sems, token = outs[2 * n:3 * n], outs[3 * n:4 * n], outs[4 * n]
        me, peers = _peers()
        for k in range(n):
            for r, (peer, slot) in enumerate(peers, 1):
                _peer_copy(src_refs[k], land_refs[k], send_sems[k], recv_sems[k], r, me, peer, slot, modes[k]).start()
        token[...] = jnp.zeros_like(token)

    hbm = lambda a: pltpu.HBM(a.shape, a.dtype)
    sems = [pltpu.SemaphoreType.DMA((N_PEERS,))] * (2 * n)
    outs = pl.pallas_call(
        body, name=name, in_specs=[HBM] * (2 * n),
        out_specs=[HBM] * (2 * n) + [SEM] * (2 * n) + [pl.BlockSpec(memory_space=pltpu.VMEM)],
        out_shape=[hbm(a) for a in srcs] + [hbm(a) for a in lands] + sems + [jax.ShapeDtypeStruct((8, 128), F32)],
        input_output_aliases={k: k for k in range(2 * n)},
        compiler_params=pltpu.CompilerParams(has_side_effects=pltpu.SideEffectType.DATAFLOW_SIDE_EFFECTING),
    )(*[pltpu.with_memory_space_constraint(a, pltpu.HBM) for a in list(srcs) + list(lands)])
    return dict(srcs=outs[:n], lands=outs[n:2 * n], send=outs[2 * n:3 * n], recv=outs[3 * n:4 * n],
                modes=list(modes)), outs[4 * n]


def _send_wait(name, sent, ks, after):
    n = len(ks)
    srcs = [sent["srcs"][k] for k in ks]
    lands = [sent["lands"][k] for k in ks]
    modes = [sent["modes"][k] for k in ks]

    def body(*refs):
        src_refs, land_refs = refs[:n], refs[n:2 * n]
        send_sems, recv_sems = refs[2 * n:3 * n], refs[3 * n:4 * n]
        me, peers = _peers()
        for k in range(n):
            for r, (peer, slot) in enumerate(peers, 1):
                args = (src_refs[k], land_refs[k], send_sems[k], recv_sems[k], r, me, peer, slot, modes[k])
                _peer_copy(*args).wait_send()
                _peer_arrival(*args).wait_recv()

    hbm = lambda a: pltpu.HBM(a.shape, a.dtype)
    outs = pl.pallas_call(
        body, name=name, in_specs=[HBM] * (2 * n) + [SEM] * (2 * n) + [ANY],
        out_specs=[HBM] * (2 * n), out_shape=[hbm(a) for a in srcs] + [hbm(a) for a in lands],
        input_output_aliases={k: k for k in range(2 * n)},
        compiler_params=pltpu.CompilerParams(has_side_effects=pltpu.SideEffectType.DATAFLOW_SIDE_EFFECTING),
    )(*srcs, *lands, *[sent["send"][k] for k in ks], *[sent["recv"][k] for k in ks], after)
    return outs[n:], outs[:n]


def _sum_blocks(own, land, ids):
    nseg, _, rows, cols = land.shape

    def body(ids_ref, own_ref, land_ref, o_ref):
        me = ids_ref[1]
        total = None
        for j in range(N_DEV):
            slot = jnp.maximum(jnp.bitwise_xor(me, j) - 1, 0)
            term = jnp.where(me == j, own_ref[...], land_ref[slot]).astype(F32)
            total = term if total is None else total + term
        o_ref[...] = total

    return pl.pallas_call(
        body, name="sum_blocks",
        grid_spec=pltpu.PrefetchScalarGridSpec(
            num_scalar_prefetch=1, grid=(nseg,),
            in_specs=[pl.BlockSpec((None, None, rows, cols), lambda s, ids: (s, ids[0], 0, 0)),
                      pl.BlockSpec((None, N_PEERS, rows, cols), lambda s, ids: (s, 0, 0, 0))],
            out_specs=pl.BlockSpec((None, rows, cols), lambda s, ids: (s, 0, 0))),
        out_shape=jax.ShapeDtypeStruct((nseg, rows, cols), F32),
        compiler_params=_params("parallel"),
    )(ids, own, land)


def _adamw(w, g, m, v):
    shape = w.shape
    cols = shape[-1]
    rows = w.size // cols
    tr = rows
    for cand in (512, 256, 128, 64, 32, 16, 8):
        if rows % cand == 0:
            tr = cand
            break
    c1 = 1.0 / (1.0 - ADAM_B1 ** ADAM_STEP)
    c2 = 1.0 / (1.0 - ADAM_B2 ** ADAM_STEP)

    def body(w_ref, g_ref, m_ref, v_ref, d_ref, nm_ref, nv_ref):
        g = g_ref[...]
        m = ADAM_B1 * m_ref[...] + (1.0 - ADAM_B1) * g
        v = ADAM_B2 * v_ref[...] + (1.0 - ADAM_B2) * (g * g)
        nm_ref[...] = m
        nv_ref[...] = v
        d_ref[...] = -ADAM_LR * ((m * c1) / (jnp.sqrt(v * c2) + ADAM_EPS) + ADAM_WD * w_ref[...])

    spec = pl.BlockSpec((tr, cols), lambda i: (i, 0))
    outs = pl.pallas_call(
        body, name="adamw", grid=(rows // tr,),
        in_specs=[spec] * 4, out_specs=[spec] * 3,
        out_shape=[jax.ShapeDtypeStruct((rows, cols), F32)] * 3,
        compiler_params=_params("parallel"),
    )(*[a.reshape(rows, cols) for a in (w, g, m, v)])
    return [o.reshape(shape) for o in outs]


SMALL = ("ln_mix_pre", "attn_sinks", "gm_ln_g", "gm_ln_b", "gm_ws", "gm_bs", "g_attn_out", "g_gm_out",
         "ln_mix_post", "ln_ffn_pre", "ln_ffn_post", "ln_ple_gate")
WEIGHTS = ("ln_mix_pre", "w_in", "attn_sinks", "gm_ln_g", "gm_ln_b", "gm_ws", "gm_bs", "g_attn_out", "g_gm_out",
           "w_out", "ln_mix_post", "ln_ffn_pre", "w_ffn_gate", "w_ffn_up", "w_ffn_down", "ln_ffn_post", "w_ple",
           "ln_ple_gate", "w_ple_gate")


def _pack_shards(w, l):
    sa = jnp.stack([w["w_ffn_gate"][l].T, w["w_ffn_up"][l].T, w["w_ffn_down"][l]])[:, None]
    sb = jnp.stack([w["w_out"][l], w["w_ple_gate"][l]])[:, None]
    return [w["w_in"][l].T[None, None].astype(BF16), sb.astype(BF16), w["w_ple"][l].T[None, None].astype(BF16),
            sa.astype(BF16)]


def _unpack_grads(rc, rb, rp, ra):
    return {"w_in": rc[0].T, "w_out": rb[0], "w_ple_gate": rb[1], "w_ple": rp[0].T,
            "w_ffn_gate": ra[0].T, "w_ffn_up": ra[1].T, "w_ffn_down": ra[2]}


def kernel(x, p, ln_mix_pre, w_in, attn_sinks, gm_ln_g, gm_ln_b, gm_ws, gm_bs, g_attn_out, g_gm_out, w_out, ln_mix_post, ln_ffn_pre, w_ffn_gate, w_ffn_up, w_ffn_down, ln_ffn_post, w_ple, ln_ple_gate, w_ple_gate, loss_target, m_ln_mix_pre, m_w_in, m_attn_sinks, m_gm_ln_g, m_gm_ln_b, m_gm_ws, m_gm_bs, m_g_attn_out, m_g_gm_out, m_w_out, m_ln_mix_post, m_ln_ffn_pre, m_w_ffn_gate, m_w_ffn_up, m_w_ffn_down, m_ln_ffn_post, m_w_ple, m_ln_ple_gate, m_w_ple_gate, v_ln_mix_pre, v_w_in, v_attn_sinks, v_gm_ln_g, v_gm_ln_b, v_gm_ws, v_gm_bs, v_g_attn_out, v_g_gm_out, v_w_out, v_ln_mix_post, v_ln_ffn_pre, v_w_ffn_gate, v_w_ffn_up, v_w_ffn_down, v_ln_ffn_post, v_w_ple, v_ln_ple_gate, v_w_ple_gate):
    given = dict(locals())
    w = {n: given[n] for n in WEIGHTS}
    sp = {n: w[n] for n in SMALL}
    kinds = ("c", "b", "p", "a")

    me, _ = _peers()
    shards = [s for l in range(DEPTH) for s in _pack_shards(w, l)]
    lands = [lax.dynamic_update_slice(lax.empty((s.shape[0], N_DEV) + s.shape[2:], BF16), s, (0, me, 0, 0))
             for s in shards]
    gather, token = _send_start("gather_start", shards, lands, [GATHER] * len(shards))
    layer_weights = [{} for _ in range(DEPTH)]

    def weights_of(l):
        def get(kind, after):
            have = layer_weights[l]
            if kind not in have:
                if l < 2:
                    group = {"c": ("c",), "b": ("b", "p"), "p": ("b", "p"), "a": ("a",)}[kind]
                    after = token if (l == 0 and kind == "c") else after
                else:
                    group = kinds
                got, _ = _send_wait(f"gather_wait_{l}{group[0]}", gather, [4 * l + kinds.index(k) for k in group], after)
                for k, g in zip(group, got):
                    have[k] = g.reshape(-1, g.shape[-1])
            return have[kind]
        return get

    h = x[0]
    p3 = p.reshape(DEPTH, -1, PLE_DIM)
    saved = []
    for l in range(DEPTH):
        h, s = _layer_fwd(h, p3, sp, l, weights_of(l), loss_target[0] if l == DEPTH - 1 else None)
        saved.append(s)
    dh, sq = h

    reduces = []
    after = token
    view = lambda g, rows: g.reshape(-1, N_DEV, rows, g.shape[-1])
    pack16 = lambda s: s.astype(BF16)[None, None]
    landing = lambda a: lax.empty((a.shape[0], N_PEERS) + a.shape[2:], BF16)

    def send(name, bufs, modes):
        return _send_start(name, bufs, [landing(a) for a in bufs], modes)

    for l in reversed(range(DEPTH)):
        lw = layer_weights[l]
        carry, ga, gp, gb = _layer_bwd_upper(dh, saved[l], p3, sp, l, lw["a"], lw["b"], lw["p"], after)
        sent1, after = send(f"reduce_start_{l}a", [view(ga, ROWS_A), view(gp, ROWS_B)], [SCATTER, SCATTER])
        carry, gb, gating = _layer_bwd_middle(carry, saved[l], sp, l, lw["b"], gb, after)
        sent2, after = send(f"reduce_start_{l}b", [view(gb, ROWS_B), pack16(gating)], [SCATTER, SPREAD])
        dh, gc, rest = _layer_bwd_lower(carry, saved[l], sp, l, lw["c"], after)
        sent3, after = send(f"reduce_start_{l}c", [view(gc, ROWS_C), pack16(rest)], [SCATTER, SPREAD])
        reduces.append((l, sent1, sent2, sent3))

    mine = jnp.stack([me, me]).astype(jnp.int32)
    whole = jnp.stack([jnp.zeros_like(me), me]).astype(jnp.int32)
    sums = {k: [None] * DEPTH for k in ("a", "p", "b", "c", "gating", "rest")}
    last = {}
    for l, sent1, sent2, sent3 in reduces:
        (la, lp), (ga, gp) = _send_wait(f"reduce_wait_{l}a", sent1, [0, 1], dh)
        (lb, lg), (gb, gg) = _send_wait(f"reduce_wait_{l}b", sent2, [0, 1], dh)
        sums["a"][l], sums["p"][l] = _sum_blocks(ga, la, mine), _sum_blocks(gp, lp, mine)
        sums["b"][l], sums["gating"][l] = _sum_blocks(gb, lb, mine), _sum_blocks(gg, lg, whole)[0]
        if l > 0:
            (lc, lr), (gc, gr) = _send_wait(f"reduce_wait_{l}c", sent3, [0, 1], dh)
            sums["c"][l], sums["rest"][l] = _sum_blocks(gc, lc, mine), _sum_blocks(gr, lr, whole)[0]
        else:
            last = sent3
    grad_x = dh
    loss = lax.psum(sq[0, 0] * (0.5 / D_MODEL), AXES)
    grads, delta, new_m, new_v = {}, {}, {}, {}

    def update(names):
        for n in names:
            delta[n], new_m[n], new_v[n] = _adamw(w[n], grads[n], given["m_" + n], given["v_" + n])

    stack = lambda f, xs: jnp.stack([f(x) for x in xs])
    gating = jnp.stack(sums["gating"])
    grads.update({
        "w_ffn_gate": stack(lambda r: r[0].T, sums["a"]), "w_ffn_up": stack(lambda r: r[1].T, sums["a"]),
        "w_ffn_down": stack(lambda r: r[2], sums["a"]), "w_ple": stack(lambda r: r[0].T, sums["p"]),
        "w_out": stack(lambda r: r[0], sums["b"]), "w_ple_gate": stack(lambda r: r[1], sums["b"]),
        "gm_ln_g": gating[:, 0, :512], "gm_ln_b": gating[:, 0, 512:], "gm_bs": gating[:, 8].reshape(DEPTH, 8, 128),
        "gm_ws": gating[:, 16:144].reshape(DEPTH, 8, 128, 128)})
    early = tuple(grads)
    update(early)
    (lc, lr), (gc, gr) = _send_wait("reduce_wait_0c", last, [0, 1], delta["w_ffn_down"])
    sums["c"][0], sums["rest"][0] = _sum_blocks(gc, lc, mine), _sum_blocks(gr, lr, whole)[0]
    rest = jnp.stack(sums["rest"])
    grads.update({
        "w_in": stack(lambda r: r[0].T, sums["c"]),
        "ln_mix_pre": rest[:, 0], "ln_mix_post": rest[:, 8], "ln_ffn_pre": rest[:, 16], "ln_ffn_post": rest[:, 24],
        "ln_ple_gate": rest[:, 32], "g_attn_out": rest[:, 40, :512], "g_gm_out": rest[:, 40, 512:],
        "attn_sinks": rest[:, 48, :8]})
    update([n for n in grads if n not in early])
    return (loss, grad_x[None], *[grads[n] for n in WEIGHTS], *[delta[n] for n in WEIGHTS],
            *[new_m[n] for n in WEIGHTS], *[new_v[n] for n in WEIGHTS])
```

```python
import math

import jax
import jax.numpy as jnp
from jax import lax
from jax.experimental import pallas as pl
from jax.experimental.pallas import tpu as pltpu

F32 = jnp.float32
BF16 = jnp.bfloat16
MESH = pl.DeviceIdType.MESH
AXES = ("x", "y", "c")

D_MODEL = 1024
DEPTH = 4
N_DEV = 8
HEAD_DIM = 64
ATTN_W = 512
KV_W = 128
GM_W = 512
D_IN = 1792
D_FF = 2816
PLE_DIM = 256
BLK = 128
FF_CHUNK = 256
WGRAD_TOKENS = 1024
NORM_EPS = 1e-6
NEG_BIG = -1e30
ATTN_SCALE = HEAD_DIM ** -0.5

ADAM_LR = 0.001
ADAM_B1 = 0.9
ADAM_B2 = 0.999
ADAM_EPS = 1e-08
ADAM_WD = 0.01
ADAM_STEP = 10

ROWS_A = D_FF // N_DEV
ROWS_B = D_MODEL // N_DEV
ROWS_C = D_IN // N_DEV
GATING_ROWS = 144
REST_ROWS = 56

VMEM_LIMIT = 56 * 2 ** 20


def _params(*sem):
    return pltpu.CompilerParams(dimension_semantics=sem, vmem_limit_bytes=VMEM_LIMIT)


def _dot(a, b):
    return jnp.dot(a, b, preferred_element_type=F32)


def _dot_nt(a, b):
    return lax.dot_general(a, b, (((1,), (1,)), ((), ())), preferred_element_type=F32)


def _dot_tn(a, b):
    return lax.dot_general(a, b, (((0,), (0,)), ((), ())), preferred_element_type=F32)


def _rms_fwd(x, g):
    r = lax.rsqrt(jnp.mean(x * x, axis=-1, keepdims=True) + NORM_EPS)
    return x * r * g


def _rms_bwd(x, g, dy):
    r = lax.rsqrt(jnp.mean(x * x, axis=-1, keepdims=True) + NORM_EPS)
    xh = x * r
    dg = jnp.sum(dy * xh, axis=0, keepdims=True)
    dxh = dy * g
    dx = r * (dxh - xh * jnp.mean(dxh * xh, axis=-1, keepdims=True))
    return dx, dg


_GELU_C = math.sqrt(2.0 / math.pi)


def _gelu(x):
    t = jnp.tanh(_GELU_C * (x + 0.044715 * (x * x * x)))
    return 0.5 * x * (1.0 + t)


def _gelu_grad(x):
    x2 = x * x
    t = jnp.tanh(_GELU_C * (x + 0.044715 * (x2 * x)))
    return 0.5 * (1.0 + t) + 0.5 * x * (1.0 - t * t) * (_GELU_C * (1.0 + 3.0 * 0.044715 * x2))


def _sigmoid(x):
    return 1.0 / (1.0 + jnp.exp(-x))


def _row_spec(tm, n):
    return pl.BlockSpec((tm, n), lambda i: (i, 0))


def _layer_row_spec(tm, n, l):
    return pl.BlockSpec((None, tm, n), lambda i: (l, i, 0))


def _vec_spec(n):
    return pl.BlockSpec((1, n), lambda i: (0, 0))


def _seg_spec(rows, cols, seg):
    return pl.BlockSpec((N_DEV * rows, cols), lambda i: (seg, 0), pipeline_mode=pl.Buffered(1))


def _zero_at(first, *refs):
    @pl.when(first)
    def _():
        for r in refs:
            r[...] = jnp.zeros(r.shape, r.dtype)


def _tile(t, want):
    return min(t, want)


def _in_proj(h, g, wc, layer):
    t = h.shape[0]
    tm = _tile(t, 512)

    def body(h_ref, g_ref, w_ref, a_ref, q_ref, kv_ref, zu_ref, zv_ref):
        a = _rms_fwd(h_ref[...], g_ref[...]).astype(BF16)
        a_ref[...] = a
        q_ref[...] = _dot_nt(a, w_ref[0:512, :]).astype(BF16)
        kv_ref[...] = _dot_nt(a, w_ref[512:768, :]).astype(BF16)
        zu_ref[...] = _dot_nt(a, w_ref[768:1280, :])
        zv_ref[...] = _dot_nt(a, w_ref[1280:1792, :])

    return pl.pallas_call(
        body, name="in_proj", grid=(t // tm,),
        in_specs=[_row_spec(tm, D_MODEL), _vec_spec(D_MODEL), _seg_spec(ROWS_C, D_MODEL, layer)],
        out_specs=[_row_spec(tm, D_MODEL), _row_spec(tm, ATTN_W), _row_spec(tm, 2 * KV_W),
                   _row_spec(tm, GM_W), _row_spec(tm, GM_W)],
        out_shape=[jax.ShapeDtypeStruct((t, D_MODEL), BF16), jax.ShapeDtypeStruct((t, ATTN_W), BF16),
                   jax.ShapeDtypeStruct((t, 2 * KV_W), BF16), jax.ShapeDtypeStruct((t, GM_W), F32),
                   jax.ShapeDtypeStruct((t, GM_W), F32)],
        compiler_params=_params("parallel"),
    )(h, g, wc)


def _head_variants(x, low):
    xr = pltpu.roll(x, 64, axis=1)
    zero = jnp.zeros_like(x)
    return {
        (0, 0): jnp.where(low, x, zero).astype(BF16),
        (0, 1): jnp.where(low, zero, xr).astype(BF16),
        (1, 0): jnp.where(low, xr, zero).astype(BF16),
        (1, 1): jnp.where(low, zero, x).astype(BF16),
    }


def _attn_masks(i):
    row = lax.broadcasted_iota(jnp.int32, (BLK, BLK), 0)
    lane = lax.broadcasted_iota(jnp.int32, (BLK, BLK), 1)
    vcur = row >= lane
    dist = jnp.where(vcur, row - lane, row - lane + BLK).astype(F32)
    valid = jnp.logical_or(vcur, i > 0)
    return lane < 64, vcur, dist, valid


def _head_key(h):
    return (h // 4, h % 2)


def _stack_kv(prev, cur, g):
    return jnp.concatenate([prev[(g, 0)], cur[(g, 0)], prev[(g, 1)], cur[(g, 1)]], axis=0)


def _split_cols(p, vcur):
    return [jnp.where(vcur, 0.0, p).astype(BF16), jnp.where(vcur, p, 0.0).astype(BF16)]


def _attn_scores(q_ref, rows, stacked, vcur):
    out = []
    for col in range(4):
        big = _dot_nt(q_ref[rows, col * 128:(col + 1) * 128], stacked[col // 2])
        for half in range(2):
            out.append(jnp.where(vcur, big[:, half * 256 + 128:half * 256 + 256], big[:, half * 256:half * 256 + 128]))
    return out


def _attn_scores_by_head(q_ref, rows, kc, kp, vcur):
    out = []
    for h in range(8):
        qh = q_ref[rows, (h // 2) * 128:(h // 2 + 1) * 128]
        out.append(jnp.where(vcur, _dot_nt(qh, kc[_head_key(h)]), _dot_nt(qh, kp[_head_key(h)])))
    return out


def _attn_probs(s, h, sink, dist, valid):
    s = s * ATTN_SCALE - (2.0 ** -(h + 1)) * dist
    if valid is not None:
        s = jnp.where(valid, s, NEG_BIG)
    m = jnp.maximum(jnp.max(s, axis=1, keepdims=True), sink)
    e = jnp.exp(s - m)
    es = jnp.exp(sink - m)
    inv = 1.0 / (jnp.sum(e, axis=1, keepdims=True) + es)
    return e * inv, es * inv


def _kv_prev_spec(blocks):
    return pl.BlockSpec((BLK, 2 * KV_W), lambda i: (jnp.maximum(i * blocks - 1, 0), 0))


def _kv_variants(kv_ref, rows, low):
    return (_head_variants(kv_ref[rows, 0:128].astype(F32), low), _head_variants(kv_ref[rows, 128:256].astype(F32), low))


def _attn_fwd(q, kv, sinks):
    t = q.shape[0]
    tq = _tile(t, 512)
    blocks = tq // BLK

    def body(sink_ref, q_ref, kvc_ref, kvp_ref, o_ref):
        low, vcur, dist, valid = _attn_masks(pl.program_id(0))
        kp, vp = _kv_variants(kvp_ref, slice(None), low)
        for b in range(blocks):
            rows = slice(b * BLK, (b + 1) * BLK)
            kc, vc = _kv_variants(kvc_ref, rows, low)
            ks = [_stack_kv(kp, kc, g) for g in range(2)]
            vs = [_stack_kv(vp, vc, g) for g in range(2)]
            scores = _attn_scores(q_ref, rows, ks, vcur)
            probs = [_attn_probs(scores[h], h, sink_ref[h], dist, valid if b == 0 else None)[0] for h in range(8)]
            for col in range(4):
                p_col = jnp.concatenate(_split_cols(probs[2 * col], vcur) + _split_cols(probs[2 * col + 1], vcur), axis=1)
                o_ref[rows, col * 128:(col + 1) * 128] = _dot(p_col, vs[col // 2]).astype(BF16)
            kp, vp = kc, vc

    return pl.pallas_call(
        body, name="attn_fwd", grid=(t // tq,),
        in_specs=[pl.BlockSpec(memory_space=pltpu.SMEM), _row_spec(tq, ATTN_W), _row_spec(tq, 2 * KV_W),
                  _kv_prev_spec(blocks)],
        out_specs=_row_spec(tq, ATTN_W),
        out_shape=jax.ShapeDtypeStruct((t, ATTN_W), BF16),
        compiler_params=_params("parallel"),
    )(sinks, q, kv, kv)


def _gm_forward_block(zu, zv, lng, lnb, w_ref, bsx, low):
    gu = _gelu(zu)
    gv = _gelu(zv)
    mu = jnp.mean(gv, axis=-1, keepdims=True)
    xc = gv - mu
    rstd = lax.rsqrt(jnp.mean(xc * xc, axis=-1, keepdims=True) + NORM_EPS)
    xn = xc * rstd
    ln = xn * lng + lnb
    mixed = []
    for col in range(4):
        lc = ln[:, col * 128:(col + 1) * 128]
        lo = jnp.where(low, lc, 0.0).astype(BF16)
        hi = jnp.where(low, 0.0, lc).astype(BF16)
        mixed.append(_dot(w_ref[2 * col], lo) + _dot(w_ref[2 * col + 1], hi) + bsx[:, col * 128:(col + 1) * 128])
    return gu, ln, xn, rstd, mixed


def _gm_fwd(zu, zv, lng, lnb, wtril, bsx):
    t = zu.shape[0]
    tm = _tile(t, 512)

    def body(zu_ref, zv_ref, g_ref, b_ref, w_ref, bs_ref, o_ref):
        low = lax.broadcasted_iota(jnp.int32, (BLK, BLK), 1) < 64
        for b in range(tm // BLK):
            rows = slice(b * BLK, (b + 1) * BLK)
            gu, _, _, _, mixed = _gm_forward_block(zu_ref[rows, :], zv_ref[rows, :], g_ref[...], b_ref[...], w_ref,
                                                   bs_ref[...], low)
            for col in range(4):
                o_ref[rows, col * 128:(col + 1) * 128] = (gu[:, col * 128:(col + 1) * 128] * mixed[col]).astype(BF16)

    return pl.pallas_call(
        body, name="gm_fwd", grid=(t // tm,),
        in_specs=[_row_spec(tm, GM_W), _row_spec(tm, GM_W), _vec_spec(GM_W), _vec_spec(GM_W),
                  pl.BlockSpec((8, BLK, BLK), lambda i: (0, 0, 0)), pl.BlockSpec((BLK, GM_W), lambda i: (0, 0))],
        out_specs=_row_spec(tm, GM_W),
        out_shape=jax.ShapeDtypeStruct((t, GM_W), BF16),
        compiler_params=_params("parallel"),
    )(zu, zv, lng, lnb, wtril, bsx)


def _out_proj(attn, gm, h, ga, gg, gpost, wb, layer):
    t = h.shape[0]
    tm = _tile(t, 512)

    def body(a_ref, m_ref, h_ref, ga_ref, gg_ref, gp_ref, w_ref, heads_ref, mix_ref, h1_ref):
        ha = _rms_fwd(a_ref[...].astype(F32), ga_ref[...]).astype(BF16)
        hg = _rms_fwd(m_ref[...].astype(F32), gg_ref[...]).astype(BF16)
        heads_ref[:, 0:512] = ha
        heads_ref[:, 512:1024] = hg
        mix = _dot(ha, w_ref[0:512, :]) + _dot(hg, w_ref[512:1024, :])
        mix_ref[...] = mix.astype(BF16)
        h1_ref[...] = h_ref[...] + _rms_fwd(mix, gp_ref[...])

    return pl.pallas_call(
        body, name="out_proj", grid=(t // tm,),
        in_specs=[_row_spec(tm, ATTN_W), _row_spec(tm, GM_W), _row_spec(tm, D_MODEL), _vec_spec(ATTN_W),
                  _vec_spec(GM_W), _vec_spec(D_MODEL), _seg_spec(ROWS_B, D_MODEL, 2 * layer)],
        out_specs=[_row_spec(tm, D_MODEL), _row_spec(tm, D_MODEL), _row_spec(tm, D_MODEL)],
        out_shape=[jax.ShapeDtypeStruct((t, D_MODEL), BF16), jax.ShapeDtypeStruct((t, D_MODEL), BF16),
                   jax.ShapeDtypeStruct((t, D_MODEL), F32)],
        compiler_params=_params("parallel"),
    )(attn, gm, h, ga, gg, gpost, wb)


def _ffn_fwd(h1, p, p_layer, gpre, gpost, gple, wa, wb, wp, layer, target=None):
    t = h1.shape[0]
    tm = _tile(t, 256)

    def body(h_ref, p_ref, gpre_ref, gpost_ref, gple_ref, wg_ref, wu_ref, wd_ref, wpg_ref, wpl_ref, *rest):
        if target is None:
            f_ref, gp_ref, up_ref, act_ref, fo_ref, h2_ref, hn_ref, gate_ref, h3_ref = rest
        else:
            t_ref, f_ref, gp_ref, up_ref, act_ref, fo_ref, h2_ref, hn_ref, gate_ref, dy_ref, l_ref = rest
            _zero_at(pl.program_id(0) == 0, l_ref)
        h = h_ref[...]
        pe = _dot_nt(p_ref[...].astype(BF16), wpl_ref[...])
        f = _rms_fwd(h, gpre_ref[...]).astype(BF16)
        f_ref[...] = f
        chunks = [slice(j * FF_CHUNK, (j + 1) * FF_CHUNK) for j in range(D_FF // FF_CHUNK)]
        fo = None
        gp, up = _dot_nt(f, wg_ref[chunks[0], :]), _dot_nt(f, wu_ref[chunks[0], :])
        for j, cols in enumerate(chunks):
            if j + 1 < len(chunks):
                gp_next, up_next = _dot_nt(f, wg_ref[chunks[j + 1], :]), _dot_nt(f, wu_ref[chunks[j + 1], :])
            act = (gp * _sigmoid(gp) * up).astype(BF16)
            gp_ref[:, cols] = gp.astype(BF16)
            up_ref[:, cols] = up.astype(BF16)
            act_ref[:, cols] = act
            part = _dot(act, wd_ref[cols, :])
            fo = part if fo is None else fo + part
            if j + 1 < len(chunks):
                gp, up = gp_next, up_next
        fo_ref[...] = fo
        h2 = h + _rms_fwd(fo, gpost_ref[...])
        h2_ref[...] = h2
        hn = _rms_fwd(h2, gple_ref[...]).astype(BF16)
        hn_ref[...] = hn
        gate = _sigmoid(_dot(hn, wpg_ref[...]))
        gate_ref[...] = gate.astype(BF16)
        h3 = h2 + pe * gate
        if target is None:
            h3_ref[...] = h3
        else:
            e = h3 - t_ref[...]
            dy_ref[...] = (e * (1.0 / D_MODEL)).astype(BF16)
            s = jnp.sum(jnp.sum(e * e, axis=1, keepdims=True), axis=0, keepdims=True)
            l_ref[...] += jnp.broadcast_to(s, (1, 128))

    wide = _row_spec(tm, D_FF)
    row = _row_spec(tm, D_MODEL)
    vec = _vec_spec(D_MODEL)
    last = target is not None
    return pl.pallas_call(
        body, name="ffn_loss" if last else "ffn_fwd", grid=(t // tm,),
        in_specs=[row, _layer_row_spec(tm, PLE_DIM, p_layer), vec, vec, vec, _seg_spec(ROWS_A, D_MODEL, 3 * layer),
                  _seg_spec(ROWS_A, D_MODEL, 3 * layer + 1), _seg_spec(ROWS_A, D_MODEL, 3 * layer + 2),
                  _seg_spec(ROWS_B, D_MODEL, 2 * layer + 1), _seg_spec(ROWS_B, PLE_DIM, layer)] + [row] * last,
        out_specs=[row, wide, wide, wide, row, row, row, row, row] + [_vec_spec(128)] * last,
        out_shape=[jax.ShapeDtypeStruct((t, D_MODEL), BF16)] + [jax.ShapeDtypeStruct((t, D_FF), BF16)] * 3
        + [jax.ShapeDtypeStruct((t, D_MODEL), F32)] * 2 + [jax.ShapeDtypeStruct((t, D_MODEL), BF16)] * 2
        + [jax.ShapeDtypeStruct((t, D_MODEL), BF16 if last else F32)] + [jax.ShapeDtypeStruct((1, 128), F32)] * last,
        compiler_params=_params("arbitrary" if last else "parallel"),
    )(h1, p, gpre, gpost, gple, wa, wa, wa, wb, wp, *([target] if last else []))


def _ffn_bwd(dh3, h2, gate, p, p_layer, fo, gp, up, h1, gple, gpost, gpre, wa, wb, wp, layer, after):
    t = dh3.shape[0]
    tm = _tile(t, 256)

    def body(d3_ref, h2_ref, gate_ref, p_ref, fo_ref, gp_ref, up_ref, h_ref, gple_ref, gpost_ref, gpre_ref,
             wg_ref, wu_ref, wd_ref, wpg_ref, wpl_ref, after_ref,
             dgl_ref, dpe_ref, dfo_ref, dgp_ref, dup_ref, dh1_ref, dgple_ref, dgpost_ref, dgpre_ref):
        _zero_at(pl.program_id(0) == 0, dgple_ref, dgpost_ref, dgpre_ref)
        d3 = d3_ref[...].astype(F32)
        gate = gate_ref[...].astype(F32)
        pe = _dot_nt(p_ref[...].astype(BF16), wpl_ref[...])
        dpe_ref[...] = (d3 * gate).astype(BF16)
        dgl = (d3 * pe * gate * (1.0 - gate)).astype(BF16)
        dgl_ref[...] = dgl
        dx2, dgple = _rms_bwd(h2_ref[...], gple_ref[...], _dot_nt(dgl, wpg_ref[...]))
        dgple_ref[...] += dgple
        d = d3 + dx2
        dfo, dgpost = _rms_bwd(fo_ref[...], gpost_ref[...], d)
        dfo = dfo.astype(BF16)
        dfo_ref[...] = dfo
        dgpost_ref[...] += dgpost
        chunks = [slice(j * FF_CHUNK, (j + 1) * FF_CHUNK) for j in range(D_FF // FF_CHUNK)]
        df = None
        dact = _dot_nt(dfo, wd_ref[chunks[0], :])
        for j, cols in enumerate(chunks):
            if j + 1 < len(chunks):
                dact_next = _dot_nt(dfo, wd_ref[chunks[j + 1], :])
            gp = gp_ref[:, cols].astype(F32)
            sg = _sigmoid(gp)
            dgp = (dact * up_ref[:, cols].astype(F32) * (sg * (1.0 + gp * (1.0 - sg)))).astype(BF16)
            dup = (dact * (gp * sg)).astype(BF16)
            dgp_ref[:, cols] = dgp
            dup_ref[:, cols] = dup
            part = _dot(dgp, wg_ref[cols, :]) + _dot(dup, wu_ref[cols, :])
            df = part if df is None else df + part
            if j + 1 < len(chunks):
                dact = dact_next
        dx, dgpre = _rms_bwd(h_ref[...], gpre_ref[...], df)
        dh1_ref[...] = (d + dx).astype(BF16)
        dgpre_ref[...] += dgpre

    wide = _row_spec(tm, D_FF)
    row = _row_spec(tm, D_MODEL)
    vec = _vec_spec(D_MODEL)
    narrow = jax.ShapeDtypeStruct((t, D_MODEL), BF16)
    return pl.pallas_call(
        body, name="ffn_bwd", grid=(t // tm,),
        in_specs=[row, row, row, _layer_row_spec(tm, PLE_DIM, p_layer), row, wide, wide, row, vec, vec, vec,
                  _seg_spec(ROWS_A, D_MODEL, 3 * layer), _seg_spec(ROWS_A, D_MODEL, 3 * layer + 1),
                  _seg_spec(ROWS_A, D_MODEL, 3 * layer + 2), _seg_spec(ROWS_B, D_MODEL, 2 * layer + 1),
                  _seg_spec(ROWS_B, PLE_DIM, layer), pl.BlockSpec(memory_space=pl.ANY)],
        out_specs=[row, row, row, wide, wide, row, vec, vec, vec],
        out_shape=[narrow, narrow, narrow, jax.ShapeDtypeStruct((t, D_FF), BF16), jax.ShapeDtypeStruct((t, D_FF), BF16),
                   narrow] + [jax.ShapeDtypeStruct((1, D_MODEL), F32)] * 3,
        compiler_params=_params("arbitrary"),
    )(dh3, h2, gate, p, fo, gp, up, h1, gple, gpost, gpre, wa, wa, wa, wb, wp, after)


def _out_proj_bwd(dh1, mix, attn, gm, gpost, ga, gg, wb, layer, after):
    t = dh1.shape[0]
    tm = _tile(t, 512)

    def body(d_ref, mix_ref, a_ref, m_ref, gp_ref, ga_ref, gg_ref, w_ref, after_ref,
             dmix_ref, da_ref, dm_ref, dgp_ref, dga_ref, dgg_ref):
        _zero_at(pl.program_id(0) == 0, dgp_ref, dga_ref, dgg_ref)
        dmix, dgp = _rms_bwd(mix_ref[...].astype(F32), gp_ref[...], d_ref[...].astype(F32))
        dmix = dmix.astype(BF16)
        dmix_ref[...] = dmix
        da, dga = _rms_bwd(a_ref[...].astype(F32), ga_ref[...], _dot_nt(dmix, w_ref[0:512, :]))
        dm, dgg = _rms_bwd(m_ref[...].astype(F32), gg_ref[...], _dot_nt(dmix, w_ref[512:1024, :]))
        da_ref[...] = da.astype(BF16)
        dm_ref[...] = dm
        dgp_ref[...] += dgp
        dga_ref[...] += dga
        dgg_ref[...] += dgg

    return pl.pallas_call(
        body, name="out_proj_bwd", grid=(t // tm,),
        in_specs=[_row_spec(tm, D_MODEL), _row_spec(tm, D_MODEL), _row_spec(tm, ATTN_W), _row_spec(tm, GM_W),
                  _vec_spec(D_MODEL), _vec_spec(ATTN_W), _vec_spec(GM_W), _seg_spec(ROWS_B, D_MODEL, 2 * layer),
                  pl.BlockSpec(memory_space=pl.ANY)],
        out_specs=[_row_spec(tm, D_MODEL), _row_spec(tm, ATTN_W), _row_spec(tm, GM_W),
                   _vec_spec(D_MODEL), _vec_spec(ATTN_W), _vec_spec(GM_W)],
        out_shape=[jax.ShapeDtypeStruct((t, D_MODEL), BF16), jax.ShapeDtypeStruct((t, ATTN_W), BF16),
                   jax.ShapeDtypeStruct((t, GM_W), F32), jax.ShapeDtypeStruct((1, D_MODEL), F32),
                   jax.ShapeDtypeStruct((1, ATTN_W), F32), jax.ShapeDtypeStruct((1, GM_W), F32)],
        compiler_params=_params("arbitrary"),
    )(dh1, mix, attn, gm, gpost, ga, gg, wb, after)


def _split3(x):
    hi = x.astype(BF16)
    r1 = x - hi.astype(F32)
    mid = r1.astype(BF16)
    lo = (r1 - mid.astype(F32)).astype(BF16)
    return hi, mid, lo


def _rider_pieces(rider, blocks):
    rm = rider[0].shape[1]
    per = -(-rm // (blocks * 256)) * 256
    return [slice(k * per, min((k + 1) * per, rm)) for k in range(blocks) if k * per < rm]


def _gm_bwd(dgm, zu, zv, lng, lnb, wtril, bsx, rider=None):
    t = zu.shape[0]
    tm = _tile(t, 512)
    nb = t // tm
    if rider is not None:
        ra, rb, rbuf, rseg = rider
        rm, rn = ra.shape[1], rb.shape[1]
        pieces = _rider_pieces(rider, tm // BLK)

    def body(d_ref, zu_ref, zv_ref, g_ref, b_ref, w_ref, bs_ref, *rest):
        if rider is None:
            dzu_ref, dzv_ref, dw_ref, dbs_ref, dlg_ref, dlb_ref, dbsx_ref = rest
        else:
            ra_ref, rb_ref, rbuf_ref, dzu_ref, dzv_ref, dw_ref, dbs_ref, dlg_ref, dlb_ref, ro_ref, dbsx_ref, acc_ref = rest
        i = pl.program_id(0)
        _zero_at(i == 0, dw_ref, dlg_ref, dlb_ref, dbsx_ref)
        if rider is not None:
            _zero_at(i == 0, acc_ref)
            rb16 = rb_ref[...].astype(BF16)
        row = lax.broadcasted_iota(jnp.int32, (BLK, BLK), 0)
        lane = lax.broadcasted_iota(jnp.int32, (BLK, BLK), 1)
        low = lane < 64
        tril = row >= lane
        lng = g_ref[...]
        for b in range(tm // BLK):
            rows = slice(b * BLK, (b + 1) * BLK)
            if rider is not None and b < len(pieces):
                acc_ref[pieces[b], :] += _dot_tn(ra_ref[:, pieces[b]], rb16)
            zu = zu_ref[rows, :]
            zv = zv_ref[rows, :]
            gu, ln, xn, rstd, mixed = _gm_forward_block(zu, zv, lng, b_ref[...], w_ref, bs_ref[...], low)
            dgm = d_ref[rows, :]
            dgu_cols, dmx_cols, dln_cols = [], [], []
            for col in range(4):
                sl = slice(col * 128, (col + 1) * 128)
                dg = dgm[:, sl]
                dgu_cols.append(dg * mixed[col])
                dmx = dg * gu[:, sl]
                dmx_cols.append(dmx)
                lc = ln[:, sl]
                halves = (jnp.where(low, lc, 0.0).astype(BF16), jnp.where(low, 0.0, lc).astype(BF16))
                dmx16 = dmx.astype(BF16)
                dmx_half = (jnp.where(low, dmx, 0.0).astype(BF16), jnp.where(low, 0.0, dmx).astype(BF16))
                dln = None
                for half in range(2):
                    hd = 2 * col + half
                    dw_ref[hd] += jnp.where(tril, _dot_nt(dmx16, halves[half]), 0.0)
                    part = _dot_tn(w_ref[hd], dmx_half[half])
                    dln = part if dln is None else dln + part
                dln_cols.append(dln)
            dgu = jnp.concatenate(dgu_cols, axis=1)
            dmx = jnp.concatenate(dmx_cols, axis=1)
            dln = jnp.concatenate(dln_cols, axis=1)
            dzu_ref[rows, :] = (dgu * _gelu_grad(zu)).astype(BF16)
            dbsx_ref[...] += dmx
            dlg_ref[...] += jnp.sum(dln * xn, axis=0, keepdims=True)
            dlb_ref[...] += jnp.sum(dln, axis=0, keepdims=True)
            dxn = dln * lng
            dgv = rstd * (dxn - jnp.mean(dxn, axis=-1, keepdims=True) - xn * jnp.mean(dxn * xn, axis=-1, keepdims=True))
            dzv_ref[rows, :] = (dgv * _gelu_grad(zv)).astype(BF16)

        @pl.when(i == nb - 1)
        def _():
            r = lax.broadcasted_iota(jnp.int32, (GM_W, BLK), 0)
            c = lax.broadcasted_iota(jnp.int32, (GM_W, BLK), 1)
            e = jnp.where(jnp.logical_and(r >= c * 64, r < c * 64 + 64), 1.0, 0.0).astype(BF16)
            hi, mid, lo = _split3(dbsx_ref[...])
            dbs_ref[...] = _dot(hi, e) + _dot(mid, e) + _dot(lo, e)
            if rider is not None:
                ro_ref[...] = acc_ref[...].astype(ro_ref.dtype)

    vec = _vec_spec(GM_W)
    in_specs = [_row_spec(tm, GM_W)] * 3 + [vec, vec, pl.BlockSpec((8, BLK, BLK), lambda i: (0, 0, 0)),
                                            pl.BlockSpec((BLK, GM_W), lambda i: (0, 0))]
    out_specs = [_row_spec(tm, GM_W), _row_spec(tm, GM_W), pl.BlockSpec((8, BLK, BLK), lambda i: (0, 0, 0)),
                 pl.BlockSpec((BLK, BLK), lambda i: (0, 0)), vec, vec]
    out_shape = [jax.ShapeDtypeStruct((t, GM_W), BF16), jax.ShapeDtypeStruct((t, GM_W), BF16),
                 jax.ShapeDtypeStruct((8, BLK, BLK), F32), jax.ShapeDtypeStruct((BLK, BLK), F32),
                 jax.ShapeDtypeStruct((1, GM_W), F32), jax.ShapeDtypeStruct((1, GM_W), F32)]
    scratch = [pltpu.VMEM((BLK, GM_W), F32)]
    operands = [dgm, zu, zv, lng, lnb, wtril, bsx]
    extra = {}
    if rider is not None:
        in_specs += [_row_spec(tm, rm), _row_spec(tm, rn), pl.BlockSpec(memory_space=pl.ANY)]
        out_specs.append(pl.BlockSpec((rm, rn), lambda i: (rseg, 0)))
        out_shape.append(jax.ShapeDtypeStruct(rbuf.shape, rbuf.dtype))
        scratch.append(pltpu.VMEM((rm, rn), F32))
        operands += [ra, rb, rbuf]
        extra = dict(input_output_aliases={9: 6})
    return pl.pallas_call(
        body, name="gm_bwd" if rider is None else "gm_bwd_rider", grid=(nb,),
        in_specs=in_specs, out_specs=out_specs, out_shape=out_shape, scratch_shapes=scratch,
        compiler_params=_params("arbitrary"), **extra,
    )(*operands)


def _attn_bwd(q, kv, do, sinks, after, rider=None):
    t = q.shape[0]
    tq = _tile(t, 512)
    blocks = tq // BLK
    steps = t // tq
    if rider is not None:
        ra, rb, rbuf, rseg = rider
        rm, rn = ra.shape[1], rb.shape[1]
        per = -(-rm // (blocks * 256)) * 256
        pieces = [slice(k * per, min((k + 1) * per, rm)) for k in range(blocks)]

    def body(sink_ref, q_ref, kvc_ref, kvp_ref, do_ref, after_ref, *rest):
        if rider is None:
            dq_ref, dkv_ref, dkf_ref, ds_ref = rest
        else:
            ra_ref, rb_ref, rbuf_ref, dq_ref, dkv_ref, dkf_ref, ds_ref, ro_ref, acc_ref = rest
        i = pl.program_id(0)
        _zero_at(i == 0, ds_ref)
        if rider is not None:
            _zero_at(i == 0, acc_ref)
            rb16 = rb_ref[...].astype(BF16)
        low, vcur, dist, valid = _attn_masks(i)
        head_row = lax.broadcasted_iota(jnp.int32, (8, 128), 0)
        dsink_tile = jnp.zeros((8, 128), F32)
        kp, vp = _kv_variants(kvp_ref, slice(None), low)
        own = None
        for b in range(blocks):
            rows = slice(b * BLK, (b + 1) * BLK)
            kc, vc = _kv_variants(kvc_ref, rows, low)
            scores = _attn_scores_by_head(q_ref, rows, kc, kp, vcur)
            dprobs = _attn_scores_by_head(do_ref, rows, vc, vp, vcur)
            if rider is not None and pieces[b].start < rm:
                acc_ref[pieces[b], :] += _dot_tn(ra_ref[:, pieces[b]], rb16)
            parts = []
            for h in range(8):
                p, ps = _attn_probs(scores[h], h, sink_ref[h], dist, valid if b == 0 else None)
                delta = jnp.sum(p * dprobs[h], axis=1, keepdims=True)
                ds = p * (dprobs[h] - delta) * ATTN_SCALE
                dsink = jnp.sum(-ps * delta, axis=0, keepdims=True)
                dsink_tile = jnp.where(head_row == h, dsink_tile + dsink, dsink_tile)
                parts.append(_split_cols(ds, vcur) + _split_cols(p, vcur))
            acc = {}

            def add(name, key, val):
                acc[(name, key)] = val if (name, key) not in acc else acc[(name, key)] + val

            for col in range(4):
                qh = q_ref[rows, col * 128:(col + 1) * 128]
                doh = do_ref[rows, col * 128:(col + 1) * 128]
                dq = None
                for half in range(2):
                    key = _head_key(2 * col + half)
                    dsp, dsc, pp, pc = parts[2 * col + half]
                    part = _dot(dsc, kc[key]) + _dot(dsp, kp[key])
                    dq = part if dq is None else dq + part
                    add("kc", key, _dot_tn(dsc, qh))
                    add("kp", key, _dot_tn(dsp, qh))
                    add("vc", key, _dot_tn(pc, doh))
                    add("vp", key, _dot_tn(pp, doh))
                dq_ref[rows, col * 128:(col + 1) * 128] = dq.astype(BF16)

            def place(name):
                head0 = acc[(name, (0, 0))] + pltpu.roll(acc[(name, (0, 1))], 64, axis=1)
                head1 = pltpu.roll(acc[(name, (1, 0))], 64, axis=1) + acc[(name, (1, 1))]
                return jnp.where(low, head0, head1)

            before = (place("kp"), place("vp"))
            if b == 0:
                dkf_ref[:, 0:128], dkf_ref[:, 128:256] = before
            else:
                last = slice((b - 1) * BLK, b * BLK)
                dkv_ref[last, 0:128] = own[0] + before[0]
                dkv_ref[last, 128:256] = own[1] + before[1]
            own = (place("kc"), place("vc"))
            kp, vp = kc, vc
        final = slice((blocks - 1) * BLK, blocks * BLK)
        dkv_ref[final, 0:128], dkv_ref[final, 128:256] = own
        ds_ref[...] += dsink_tile
        if rider is not None:
            @pl.when(i == steps - 1)
            def _():
                ro_ref[...] = acc_ref[...].astype(ro_ref.dtype)

    row_q = _row_spec(tq, ATTN_W)
    row_kv = _row_spec(tq, 2 * KV_W)
    in_specs = [pl.BlockSpec(memory_space=pltpu.SMEM), row_q, row_kv, _kv_prev_spec(blocks), row_q,
                pl.BlockSpec(memory_space=pl.ANY)]
    out_specs = [row_q, row_kv, _row_spec(BLK, 2 * KV_W), pl.BlockSpec((8, 128), lambda i: (0, 0))]
    out_shape = [jax.ShapeDtypeStruct((t, ATTN_W), BF16), jax.ShapeDtypeStruct((t, 2 * KV_W), F32),
                 jax.ShapeDtypeStruct((t // tq * BLK, 2 * KV_W), F32), jax.ShapeDtypeStruct((8, 128), F32)]
    operands = [sinks, q, kv, kv, do, after]
    extra = {}
    if rider is not None:
        in_specs += [_row_spec(tq, rm), _row_spec(tq, rn), pl.BlockSpec(memory_space=pl.ANY)]
        out_specs.append(pl.BlockSpec((rm, rn), lambda i: (rseg, 0)))
        out_shape.append(jax.ShapeDtypeStruct(rbuf.shape, rbuf.dtype))
        operands += [ra, rb, rbuf]
        extra = dict(scratch_shapes=[pltpu.VMEM((rm, rn), F32)], input_output_aliases={8: 4})
    return pl.pallas_call(
        body, name="attn_bwd" if rider is None else "attn_bwd_rider", grid=(steps,),
        in_specs=in_specs, out_specs=out_specs, out_shape=out_shape,
        compiler_params=_params("arbitrary"), **extra,
    )(*operands)


def _in_proj_bwd(dq, dkv, dkf, dzu, dzv, h, dres, g, wc, layer, dh_dtype):
    t = h.shape[0]
    tm = _tile(t, 512)
    steps = t // tm

    def body(dq_ref, dkv_ref, dkn_ref, dzu_ref, dzv_ref, h_ref, d_ref, g_ref, w_ref, dz_ref, dh_ref, dg_ref):
        i = pl.program_id(0)
        _zero_at(i == 0, dg_ref)
        dq = dq_ref[...]
        tail = dkv_ref[tm - BLK:tm, :] + jnp.where(i < steps - 1, dkn_ref[...], 0.0)
        dkv = tail if tm == BLK else jnp.concatenate([dkv_ref[0:tm - BLK, :], tail], axis=0)
        dkv = dkv.astype(BF16)
        dzu = dzu_ref[...]
        dzv = dzv_ref[...]
        dz_ref[:, 0:512] = dq
        dz_ref[:, 512:768] = dkv
        dz_ref[:, 768:1280] = dzu
        dz_ref[:, 1280:1792] = dzv
        da = (_dot(dq, w_ref[0:512, :]) + _dot(dkv, w_ref[512:768, :]) + _dot(dzu, w_ref[768:1280, :])
              + _dot(dzv, w_ref[1280:1792, :]))
        dx, dg = _rms_bwd(h_ref[...], g_ref[...], da)
        dh_ref[...] = (d_ref[...].astype(F32) + dx).astype(dh_dtype)
        dg_ref[...] += dg

    return pl.pallas_call(
        body, name="in_proj_bwd", grid=(t // tm,),
        in_specs=[_row_spec(tm, ATTN_W), _row_spec(tm, 2 * KV_W),
                  pl.BlockSpec((BLK, 2 * KV_W), lambda i: (jnp.minimum(i + 1, steps - 1), 0)), _row_spec(tm, GM_W),
                  _row_spec(tm, GM_W), _row_spec(tm, D_MODEL), _row_spec(tm, D_MODEL), _vec_spec(D_MODEL),
                  _seg_spec(ROWS_C, D_MODEL, layer)],
        out_specs=[_row_spec(tm, D_IN), _row_spec(tm, D_MODEL), _vec_spec(D_MODEL)],
        out_shape=[jax.ShapeDtypeStruct((t, D_IN), BF16), jax.ShapeDtypeStruct((t, D_MODEL), dh_dtype),
                   jax.ShapeDtypeStruct((1, D_MODEL), F32)],
        compiler_params=_params("arbitrary"),
    )(dq, dkv, dkf, dzu, dzv, h, dres, g, wc)


def _weight_grad(a, b, buf, seg, b_layer=None):
    t, m = a.shape
    n = b.shape[-1]
    assert buf.shape[0] % m == 0 and buf.shape[1] == n
    tm = _tile(t, WGRAD_TOKENS)
    steps = t // tm
    half = m // 2

    def body(a_ref, b_ref, buf_ref, o_ref, acc_ref):
        i = pl.program_id(0)
        _zero_at(i == 0, acc_ref)
        b16 = b_ref[...].astype(BF16)
        for rows in (slice(0, half), slice(half, m)):
            acc_ref[rows, :] += _dot_tn(a_ref[:, rows], b16)

        @pl.when(i == steps - 1)
        def _():
            o_ref[...] = acc_ref[...].astype(o_ref.dtype)

    return pl.pallas_call(
        body, name="weight_grad", grid=(steps,),
        in_specs=[_row_spec(tm, m), _row_spec(tm, n) if b_layer is None else _layer_row_spec(tm, n, b_layer),
                  pl.BlockSpec(memory_space=pl.ANY)],
        out_specs=pl.BlockSpec((m, n), lambda i: (seg, 0)),
        out_shape=jax.ShapeDtypeStruct(buf.shape, buf.dtype),
        scratch_shapes=[pltpu.VMEM((m, n), F32)],
        input_output_aliases={2: 0},
        compiler_params=_params("arbitrary"),
    )(a, b, buf)


def _rows8(rows):
    return [jnp.pad(r, ((0, 7), (0, 0))) for r in rows]


def _small_pack_gating(d):
    rows = [jnp.concatenate([d["gm_ln_g"], d["gm_ln_b"]], axis=1), d["gm_bs"].reshape(1, 1024)]
    return jnp.concatenate(_rows8(rows) + [d["gm_ws"].reshape(128, 1024)], axis=0)


def _small_pack_rest(d):
    rows = [d["ln_mix_pre"], d["ln_mix_post"], d["ln_ffn_pre"], d["ln_ffn_post"], d["ln_ple_gate"],
            jnp.concatenate([d["g_attn_out"], d["g_gm_out"]], axis=1),
            jnp.pad(d["attn_sinks"].reshape(1, 8), ((0, 0), (0, 1016)))]
    return jnp.concatenate(_rows8(rows), axis=0)


def _unpack_gating(g):
    return {"gm_ln_g": g[:, 0, :512], "gm_ln_b": g[:, 0, 512:], "gm_bs": g[:, 8].reshape(DEPTH, 8, 128),
            "gm_ws": g[:, 16:GATING_ROWS].reshape(DEPTH, 8, 128, 128)}


def _unpack_rest(s):
    return {"ln_mix_pre": s[:, 0], "ln_mix_post": s[:, 8], "ln_ffn_pre": s[:, 16], "ln_ffn_post": s[:, 24],
            "ln_ple_gate": s[:, 32], "g_attn_out": s[:, 40, :512], "g_gm_out": s[:, 40, 512:], "attn_sinks": s[:, 48, :8]}


def _row(v):
    return v.reshape(1, -1)


def _layer_fwd(h, p, sp, l, weights, target=None):
    tril = jnp.tril(jnp.ones((BLK, BLK), bool))
    wtril = jnp.where(tril[None], sp["gm_ws"][l], 0.0).astype(BF16)
    bsx = jnp.repeat(sp["gm_bs"][l].T, HEAD_DIM, axis=1)
    a, q, kv, zu, zv = _in_proj(h, _row(sp["ln_mix_pre"][l]), weights("c", h), 0)
    attn = _attn_fwd(q, kv, sp["attn_sinks"][l])
    gm = _gm_fwd(zu, zv, _row(sp["gm_ln_g"][l]), _row(sp["gm_ln_b"][l]), wtril, bsx)
    wb = weights("b", gm)
    heads, mix, h1 = _out_proj(attn, gm, h, _row(sp["g_attn_out"][l]), _row(sp["g_gm_out"][l]),
                               _row(sp["ln_mix_post"][l]), wb, 0)
    wa = weights("a", h1)
    f, gpre, up, act, fo, h2, hn, gate, *out = _ffn_fwd(
        h1, p, l, _row(sp["ln_ffn_pre"][l]), _row(sp["ln_ffn_post"][l]), _row(sp["ln_ple_gate"][l]), wa, wb,
        weights("p", gm), 0, target)
    saved = dict(h=h, a=a, q=q, kv=kv, zu=zu, zv=zv, attn=attn, gm=gm, heads=heads, mix=mix, h1=h1, f=f,
                 gpre=gpre, up=up, act=act, fo=fo, h2=h2, hn=hn, gate=gate, wtril=wtril, bsx=bsx)
    return (out[0] if target is None else tuple(out)), saved


def _layer_bwd_upper(dh, s, p, sp, l, wa, wb, wp, after, ride):
    d = {}
    dgl, dpe, dfo, dgp, dup, dh1, d["ln_ple_gate"], d["ln_ffn_post"], d["ln_ffn_pre"] = _ffn_bwd(
        dh, s["h2"], s["gate"], p, l, s["fo"], s["gpre"], s["up"], s["h1"], _row(sp["ln_ple_gate"][l]),
        _row(sp["ln_ffn_post"][l]), _row(sp["ln_ffn_pre"][l]), wa, wb, wp, 0, after)
    gb = _weight_grad(s["hn"], dgl, lax.empty((2 * D_MODEL, D_MODEL), BF16), 1)
    gp = _weight_grad(dpe, p, lax.empty((D_MODEL, PLE_DIM), BF16), 0, b_layer=l)
    ga = _weight_grad(s["act"], dfo, lax.empty((3 * D_FF, D_MODEL), BF16), 2)
    if not ride:
        ga = _weight_grad(dgp, s["f"], ga, 0)
        ga = _weight_grad(dup, s["f"], ga, 1)
    return (dh1, d, dgp, dup), ga, gp, gb


def _layer_bwd_middle(carry, s, sp, l, wb, gb, ga, after, ride):
    dh1, d, dgp, dup = carry
    dmix, dattn, dgm, d["ln_mix_post"], d["g_attn_out"], d["g_gm_out"] = _out_proj_bwd(
        dh1, s["mix"], s["attn"], s["gm"], _row(sp["ln_mix_post"][l]), _row(sp["g_attn_out"][l]),
        _row(sp["g_gm_out"][l]), wb, 0, after)
    gb = _weight_grad(s["heads"], dmix, gb, 0)
    dzu, dzv, d["gm_ws"], dbs, d["gm_ln_g"], d["gm_ln_b"], *rode = _gm_bwd(
        dgm, s["zu"], s["zv"], _row(sp["gm_ln_g"][l]), _row(sp["gm_ln_b"][l]), s["wtril"], s["bsx"],
        (dup, s["f"], ga, 1) if ride else None)
    d["gm_bs"] = dbs[:, :8].T
    return (dh1, dattn, dzu, dzv, d, dgp), gb, (rode[0] if ride else ga), _small_pack_gating(d)


def _layer_bwd_lower(carry, s, sp, l, wc, ga, after, ride):
    dh1, dattn, dzu, dzv, d, dgp = carry
    dq, dkv, dkf, dsink, *rode = _attn_bwd(s["q"], s["kv"], dattn, sp["attn_sinks"][l], after,
                                            (dgp, s["f"], ga, 0) if ride else None)
    ga = rode[0] if ride else ga
    d["attn_sinks"] = dsink[:, 0]
    dz, dh, d["ln_mix_pre"] = _in_proj_bwd(dq, dkv, dkf, dzu, dzv, s["h"], dh1, _row(sp["ln_mix_pre"][l]), wc, 0,
                                           F32 if l == 0 else BF16)
    gc = _weight_grad(dz, s["a"], lax.empty((D_IN, D_MODEL), BF16), 0)
    return dh, gc, ga, _small_pack_rest(d)


ANY = pl.BlockSpec(memory_space=pl.ANY)


HBM = pl.BlockSpec(memory_space=pltpu.HBM)
SEM = pl.BlockSpec(memory_space=pltpu.SEMAPHORE)
N_PEERS = N_DEV - 1


def _peers():
    x, y, c = lax.axis_index("x"), lax.axis_index("y"), lax.axis_index("c")
    peers = []
    for r in range(1, N_DEV):
        px = 1 - x if r & 4 else x
        py = 1 - y if r & 2 else y
        pc = 1 - c if r & 1 else c
        peers.append(((px, py, pc), 4 * px + 2 * py + pc))
    return 4 * x + 2 * y + c, peers


GATHER, SCATTER, SPREAD = "gather", "scatter", "spread"


def _peer_copy(src, land, send_sems, recv_sems, r, me, peer, peer_slot, mode):
    return pltpu.make_async_remote_copy(
        src_ref=src.at[:, pl.ds(peer_slot, 1)] if mode == SCATTER else src,
        dst_ref=land.at[:, pl.ds(me, 1)] if mode == GATHER else land.at[:, pl.ds(r - 1, 1)],
        send_sem=send_sems.at[r - 1], recv_sem=recv_sems.at[r - 1], device_id=peer, device_id_type=MESH)


def _peer_arrival(src, land, send_sems, recv_sems, r, me, peer, peer_slot, mode):
    return pltpu.make_async_remote_copy(
        src_ref=src.at[:, pl.ds(me, 1)] if mode == SCATTER else src,
        dst_ref=land.at[:, pl.ds(peer_slot, 1)] if mode == GATHER else land.at[:, pl.ds(r - 1, 1)],
        send_sem=send_sems.at[r - 1], recv_sem=recv_sems.at[r - 1], device_id=peer, device_id_type=MESH)


def _send_start(name, srcs, lands, modes):
    n = len(srcs)

    def body(*refs):
        src_refs, land_refs = refs[:n], refs[n:2 * n]
        outs = refs[2 * n:]
        send_sems, recv_sems, token = outs[2 * n:3 * n], outs[3 * n:4 * n], outs[4 * n]
        me, peers = _peers()
        for k in range(n):
            for r, (peer, slot) in enumerate(peers, 1):
                _peer_copy(src_refs[k], land_refs[k], send_sems[k], recv_sems[k], r, me, peer, slot, modes[k]).start()
        token[...] = jnp.zeros_like(token)

    hbm = lambda a: pltpu.HBM(a.shape, a.dtype)
    sems = [pltpu.SemaphoreType.DMA((N_PEERS,))] * (2 * n)
    outs = pl.pallas_call(
        body, name=name, in_specs=[HBM] * (2 * n),
        out_specs=[HBM] * (2 * n) + [SEM] * (2 * n) + [pl.BlockSpec(memory_space=pltpu.VMEM)],
        out_shape=[hbm(a) for a in srcs] + [hbm(a) for a in lands] + sems + [jax.ShapeDtypeStruct((8, 128), F32)],
        input_output_aliases={k: k for k in range(2 * n)},
        compiler_params=pltpu.CompilerParams(has_side_effects=pltpu.SideEffectType.DATAFLOW_SIDE_EFFECTING),
    )(*[pltpu.with_memory_space_constraint(a, pltpu.HBM) for a in list(srcs) + list(lands)])
    return dict(srcs=outs[:n], lands=outs[n:2 * n], send=outs[2 * n:3 * n], recv=outs[3 * n:4 * n],
                modes=list(modes)), outs[4 * n]


def _send_wait(name, sent, ks, after):
    n = len(ks)
    srcs = [sent["srcs"][k] for k in ks]
    lands = [sent["lands"][k] for k in ks]
    modes = [sent["modes"][k] for k in ks]

    def body(*refs):
        src_refs, land_refs = refs[:n], refs[n:2 * n]
        send_sems, recv_sems = refs[2 * n:3 * n], refs[3 * n:4 * n]
        me, peers = _peers()
        for k in range(n):
            for r, (peer, slot) in enumerate(peers, 1):
                args = (src_refs[k], land_refs[k], send_sems[k], recv_sems[k], r, me, peer, slot, modes[k])
                _peer_copy(*args).wait_send()
                _peer_arrival(*args).wait_recv()

    hbm = lambda a: pltpu.HBM(a.shape, a.dtype)
    outs = pl.pallas_call(
        body, name=name, in_specs=[HBM] * (2 * n) + [SEM] * (2 * n) + [ANY],
        out_specs=[HBM] * (2 * n), out_shape=[hbm(a) for a in srcs] + [hbm(a) for a in lands],
        input_output_aliases={k: k for k in range(2 * n)},
        compiler_params=pltpu.CompilerParams(has_side_effects=pltpu.SideEffectType.DATAFLOW_SIDE_EFFECTING),
    )(*srcs, *lands, *[sent["send"][k] for k in ks], *[sent["recv"][k] for k in ks], after)
    return outs[n:], outs[:n]


def _sum_blocks(own, land, ids):
    nseg, _, rows, cols = land.shape

    def body(ids_ref, own_ref, land_ref, o_ref):
        me = ids_ref[1]
        total = None
        for j in range(N_DEV):
            slot = jnp.maximum(jnp.bitwise_xor(me, j) - 1, 0)
            term = jnp.where(me == j, own_ref[...], land_ref[slot]).astype(F32)
            total = term if total is None else total + term
        o_ref[...] = total

    return pl.pallas_call(
        body, name="sum_blocks",
        grid_spec=pltpu.PrefetchScalarGridSpec(
            num_scalar_prefetch=1, grid=(nseg,),
            in_specs=[pl.BlockSpec((None, None, rows, cols), lambda s, ids: (s, ids[0], 0, 0)),
                      pl.BlockSpec((None, N_PEERS, rows, cols), lambda s, ids: (s, 0, 0, 0))],
            out_specs=pl.BlockSpec((None, rows, cols), lambda s, ids: (s, 0, 0))),
        out_shape=jax.ShapeDtypeStruct((nseg, rows, cols), F32),
        compiler_params=_params("parallel"),
    )(ids, own, land)


def _adamw(w, g, m, v):
    shape = w.shape
    cols = shape[-1]
    rows = w.size // cols
    tr = rows
    for cand in (512, 256, 128, 64, 32, 16, 8):
        if rows % cand == 0:
            tr = cand
            break
    c1 = 1.0 / (1.0 - ADAM_B1 ** ADAM_STEP)
    c2 = 1.0 / (1.0 - ADAM_B2 ** ADAM_STEP)

    def body(w_ref, g_ref, m_ref, v_ref, d_ref, nm_ref, nv_ref):
        g = g_ref[...]
        m = ADAM_B1 * m_ref[...] + (1.0 - ADAM_B1) * g
        v = ADAM_B2 * v_ref[...] + (1.0 - ADAM_B2) * (g * g)
        nm_ref[...] = m
        nv_ref[...] = v
        d_ref[...] = -ADAM_LR * ((m * c1) / (jnp.sqrt(v * c2) + ADAM_EPS) + ADAM_WD * w_ref[...])

    spec = pl.BlockSpec((tr, cols), lambda i: (i, 0))
    outs = pl.pallas_call(
        body, name="adamw", grid=(rows // tr,),
        in_specs=[spec] * 4, out_specs=[spec] * 3,
        out_shape=[jax.ShapeDtypeStruct((rows, cols), F32)] * 3,
        compiler_params=_params("parallel"),
    )(*[a.reshape(rows, cols) for a in (w, g, m, v)])
    return [o.reshape(shape) for o in outs]


SMALL = ("ln_mix_pre", "attn_sinks", "gm_ln_g", "gm_ln_b", "gm_ws", "gm_bs", "g_attn_out", "g_gm_out",
         "ln_mix_post", "ln_ffn_pre", "ln_ffn_post", "ln_ple_gate")
WEIGHTS = ("ln_mix_pre", "w_in", "attn_sinks", "gm_ln_g", "gm_ln_b", "gm_ws", "gm_bs", "g_attn_out", "g_gm_out",
           "w_out", "ln_mix_post", "ln_ffn_pre", "w_ffn_gate", "w_ffn_up", "w_ffn_down", "ln_ffn_post", "w_ple",
           "ln_ple_gate", "w_ple_gate")


def _pack_shards(w, l):
    sa = jnp.stack([w["w_ffn_gate"][l].T, w["w_ffn_up"][l].T, w["w_ffn_down"][l]])[:, None]
    sb = jnp.stack([w["w_out"][l], w["w_ple_gate"][l]])[:, None]
    return [w["w_in"][l].T[None, None].astype(BF16), sb.astype(BF16), w["w_ple"][l].T[None, None].astype(BF16),
            sa.astype(BF16)]


def kernel(x, p, ln_mix_pre, w_in, attn_sinks, gm_ln_g, gm_ln_b, gm_ws, gm_bs, g_attn_out, g_gm_out, w_out, ln_mix_post, ln_ffn_pre, w_ffn_gate, w_ffn_up, w_ffn_down, ln_ffn_post, w_ple, ln_ple_gate, w_ple_gate, loss_target, m_ln_mix_pre, m_w_in, m_attn_sinks, m_gm_ln_g, m_gm_ln_b, m_gm_ws, m_gm_bs, m_g_attn_out, m_g_gm_out, m_w_out, m_ln_mix_post, m_ln_ffn_pre, m_w_ffn_gate, m_w_ffn_up, m_w_ffn_down, m_ln_ffn_post, m_w_ple, m_ln_ple_gate, m_w_ple_gate, v_ln_mix_pre, v_w_in, v_attn_sinks, v_gm_ln_g, v_gm_ln_b, v_gm_ws, v_gm_bs, v_g_attn_out, v_g_gm_out, v_w_out, v_ln_mix_post, v_ln_ffn_pre, v_w_ffn_gate, v_w_ffn_up, v_w_ffn_down, v_ln_ffn_post, v_w_ple, v_ln_ple_gate, v_w_ple_gate):
    given = dict(locals())
    w = {n: given[n] for n in WEIGHTS}
    sp = {n: w[n] for n in SMALL}
    kinds = ("c", "b", "p", "a")

    me, _ = _peers()
    shards = [s for l in range(DEPTH) for s in _pack_shards(w, l)]
    lands = [lax.dynamic_update_slice(lax.empty((s.shape[0], N_DEV) + s.shape[2:], BF16), s, (0, me, 0, 0))
             for s in shards]
    gather, token = _send_start("gather_start", shards, lands, [GATHER] * len(shards))
    layer_weights = [{} for _ in range(DEPTH)]

    def weights_of(l):
        def get(kind, after):
            have = layer_weights[l]
            if kind not in have:
                if l < 2:
                    group = {"c": ("c",), "b": ("b", "p"), "p": ("b", "p"), "a": ("a",)}[kind]
                    after = token if (l == 0 and kind == "c") else after
                else:
                    group = kinds
                got, _ = _send_wait(f"gather_wait_{l}{group[0]}", gather, [4 * l + kinds.index(k) for k in group], after)
                for k, g in zip(group, got):
                    have[k] = g.reshape(-1, g.shape[-1])
            return have[kind]
        return get

    h = x[0]
    p3 = p.reshape(DEPTH, -1, PLE_DIM)
    saved = []
    for l in range(DEPTH):
        h, s = _layer_fwd(h, p3, sp, l, weights_of(l), loss_target[0] if l == DEPTH - 1 else None)
        saved.append(s)
    dh, sq = h

    reduces = []
    after = token
    view = lambda g, rows: g.reshape(-1, N_DEV, rows, g.shape[-1])
    pack16 = lambda s: s.astype(BF16)[None, None]
    landing = lambda a: lax.empty((a.shape[0], N_PEERS) + a.shape[2:], BF16)

    def send(name, bufs, modes):
        return _send_start(name, bufs, [landing(a) for a in bufs], modes)

    for l in reversed(range(DEPTH)):
        lw = layer_weights[l]
        ride = l > 0
        carry, ga, gp, gb = _layer_bwd_upper(dh, saved[l], p3, sp, l, lw["a"], lw["b"], lw["p"], after, ride)
        early = [view(gp, ROWS_B)] + ([] if ride else [view(ga, ROWS_A)])
        sent1, after = send(f"reduce_start_{l}a", early, [SCATTER] * len(early))
        carry, gb, ga, gating = _layer_bwd_middle(carry, saved[l], sp, l, lw["b"], gb, ga, after, ride)
        sent2, after = send(f"reduce_start_{l}b", [view(gb, ROWS_B), pack16(gating)], [SCATTER, SPREAD])
        dh, gc, ga, rest = _layer_bwd_lower(carry, saved[l], sp, l, lw["c"], ga, after, ride)
        late = [view(gc, ROWS_C), pack16(rest)] + ([view(ga, ROWS_A)] if ride else [])
        sent3, after = send(f"reduce_start_{l}c", late, [SCATTER, SPREAD] + [SCATTER] * ride)
        reduces.append((l, sent1, sent2, sent3))

    mine = jnp.stack([me, me]).astype(jnp.int32)
    whole = jnp.stack([jnp.zeros_like(me), me]).astype(jnp.int32)
    sums = {k: [None] * DEPTH for k in ("a", "p", "b", "c", "gating", "rest")}
    last = {}
    for l, sent1, sent2, sent3 in reduces:
        l1, g1 = _send_wait(f"reduce_wait_{l}a", sent1, list(range(len(sent1["srcs"]))), dh)
        (lb, lg), (gb, gg) = _send_wait(f"reduce_wait_{l}b", sent2, [0, 1], dh)
        sums["p"][l] = _sum_blocks(g1[0], l1[0], mine)
        sums["b"][l], sums["gating"][l] = _sum_blocks(gb, lb, mine), _sum_blocks(gg, lg, whole)[0]
        if l > 0:
            (lc, lr, la), (gc, gr, ga) = _send_wait(f"reduce_wait_{l}c", sent3, [0, 1, 2], dh)
            sums["c"][l], sums["rest"][l] = _sum_blocks(gc, lc, mine), _sum_blocks(gr, lr, whole)[0]
            sums["a"][l] = _sum_blocks(ga, la, mine)
        else:
            sums["a"][l] = _sum_blocks(g1[1], l1[1], mine)
            last = sent3
    grad_x = dh
    loss = lax.psum(sq[0, 0] * (0.5 / D_MODEL), AXES)
    grads, delta, new_m, new_v = {}, {}, {}, {}

    def update(names):
        for n in names:
            delta[n], new_m[n], new_v[n] = _adamw(w[n], grads[n], given["m_" + n], given["v_" + n])

    stack = lambda f, xs: jnp.stack([f(x) for x in xs])
    grads.update({
        "w_ffn_gate": stack(lambda r: r[0].T, sums["a"]), "w_ffn_up": stack(lambda r: r[1].T, sums["a"]),
        "w_ffn_down": stack(lambda r: r[2], sums["a"]), "w_ple": stack(lambda r: r[0].T, sums["p"]),
        "w_out": stack(lambda r: r[0], sums["b"]), "w_ple_gate": stack(lambda r: r[1], sums["b"])})
    grads.update(_unpack_gating(jnp.stack(sums["gating"])))
    early = tuple(grads)
    update(early)
    (lc, lr), (gc, gr) = _send_wait("reduce_wait_0c", last, [0, 1], delta["w_ffn_down"])
    sums["c"][0], sums["rest"][0] = _sum_blocks(gc, lc, mine), _sum_blocks(gr, lr, whole)[0]
    grads["w_in"] = stack(lambda r: r[0].T, sums["c"])
    grads.update(_unpack_rest(jnp.stack(sums["rest"])))
    update([n for n in grads if n not in early])
    return (loss, grad_x[None], *[grads[n] for n in WEIGHTS], *[delta[n] for n in WEIGHTS],
            *[new_m[n] for n in WEIGHTS], *[new_v[n] for n in WEIGHTS])
```

```python
import math

import jax
import jax.numpy as jnp
from jax import lax
from jax.experimental import pallas as pl
from jax.experimental.pallas import tpu as pltpu

F32 = jnp.float32
BF16 = jnp.bfloat16
MESH = pl.DeviceIdType.MESH
AXES = ("x", "y", "c")

D_MODEL = 1024
DEPTH = 4
N_DEV = 8
HEAD_DIM = 64
ATTN_W = 512
KV_W = 128
GM_W = 512
D_IN = 1792
D_FF = 2816
PLE_DIM = 256
BLK = 128
FF_CHUNK = 256
WGRAD_TOKENS = 1024
NORM_EPS = 1e-6
NEG_BIG = -1e30
ATTN_SCALE = HEAD_DIM ** -0.5

ADAM_LR = 0.001
ADAM_B1 = 0.9
ADAM_B2 = 0.999
ADAM_EPS = 1e-08
ADAM_WD = 0.01
ADAM_STEP = 10

ROWS_A = D_FF // N_DEV
ROWS_B = D_MODEL // N_DEV
ROWS_C = D_IN // N_DEV
GATING_ROWS = 144
REST_ROWS = 56

VMEM_LIMIT = 56 * 2 ** 20


def _params(*sem):
    return pltpu.CompilerParams(dimension_semantics=sem, vmem_limit_bytes=VMEM_LIMIT)


def _dot(a, b):
    return jnp.dot(a, b, preferred_element_type=F32)


def _dot_nt(a, b):
    return lax.dot_general(a, b, (((1,), (1,)), ((), ())), preferred_element_type=F32)


def _dot_tn(a, b):
    return lax.dot_general(a, b, (((0,), (0,)), ((), ())), preferred_element_type=F32)


def _rms_fwd(x, g):
    r = lax.rsqrt(jnp.mean(x * x, axis=-1, keepdims=True) + NORM_EPS)
    return x * r * g


def _rms_bwd(x, g, dy):
    r = lax.rsqrt(jnp.mean(x * x, axis=-1, keepdims=True) + NORM_EPS)
    xh = x * r
    dg = jnp.sum(dy * xh, axis=0, keepdims=True)
    dxh = dy * g
    dx = r * (dxh - xh * jnp.mean(dxh * xh, axis=-1, keepdims=True))
    return dx, dg


_GELU_C = math.sqrt(2.0 / math.pi)


def _gelu(x):
    t = jnp.tanh(_GELU_C * (x + 0.044715 * (x * x * x)))
    return 0.5 * x * (1.0 + t)


def _gelu_grad(x):
    x2 = x * x
    t = jnp.tanh(_GELU_C * (x + 0.044715 * (x2 * x)))
    return 0.5 * (1.0 + t) + 0.5 * x * (1.0 - t * t) * (_GELU_C * (1.0 + 3.0 * 0.044715 * x2))


def _sigmoid(x):
    return 1.0 / (1.0 + jnp.exp(-x))


def _row_spec(tm, n):
    return pl.BlockSpec((tm, n), lambda i: (i, 0))


def _layer_row_spec(tm, n, l):
    return pl.BlockSpec((None, tm, n), lambda i: (l, i, 0))


def _vec_spec(n):
    return pl.BlockSpec((1, n), lambda i: (0, 0))


def _seg_spec(rows, cols, seg):
    return pl.BlockSpec((N_DEV * rows, cols), lambda i: (seg, 0), pipeline_mode=pl.Buffered(1))


def _zero_at(first, *refs):
    @pl.when(first)
    def _():
        for r in refs:
            r[...] = jnp.zeros(r.shape, r.dtype)


def _tile(t, want):
    return min(t, want)


def _in_proj(h, g, wc, layer):
    t = h.shape[0]
    tm = _tile(t, 512)

    def body(h_ref, g_ref, w_ref, a_ref, q_ref, kv_ref, zu_ref, zv_ref):
        a = _rms_fwd(h_ref[...], g_ref[...]).astype(BF16)
        a_ref[...] = a
        q_ref[...] = _dot_nt(a, w_ref[0:512, :]).astype(BF16)
        kv_ref[...] = _dot_nt(a, w_ref[512:768, :]).astype(BF16)
        zu_ref[...] = _dot_nt(a, w_ref[768:1280, :])
        zv_ref[...] = _dot_nt(a, w_ref[1280:1792, :])

    return pl.pallas_call(
        body, name="in_proj", grid=(t // tm,),
        in_specs=[_row_spec(tm, D_MODEL), _vec_spec(D_MODEL), _seg_spec(ROWS_C, D_MODEL, layer)],
        out_specs=[_row_spec(tm, D_MODEL), _row_spec(tm, ATTN_W), _row_spec(tm, 2 * KV_W),
                   _row_spec(tm, GM_W), _row_spec(tm, GM_W)],
        out_shape=[jax.ShapeDtypeStruct((t, D_MODEL), BF16), jax.ShapeDtypeStruct((t, ATTN_W), BF16),
                   jax.ShapeDtypeStruct((t, 2 * KV_W), BF16), jax.ShapeDtypeStruct((t, GM_W), F32),
                   jax.ShapeDtypeStruct((t, GM_W), F32)],
        compiler_params=_params("parallel"),
    )(h, g, wc)


def _head_variants(x, low):
    xr = pltpu.roll(x, 64, axis=1)
    zero = jnp.zeros_like(x)
    return {
        (0, 0): jnp.where(low, x, zero).astype(BF16),
        (0, 1): jnp.where(low, zero, xr).astype(BF16),
        (1, 0): jnp.where(low, xr, zero).astype(BF16),
        (1, 1): jnp.where(low, zero, x).astype(BF16),
    }


def _attn_masks(i):
    row = lax.broadcasted_iota(jnp.int32, (BLK, BLK), 0)
    lane = lax.broadcasted_iota(jnp.int32, (BLK, BLK), 1)
    vcur = row >= lane
    dist = jnp.where(vcur, row - lane, row - lane + BLK).astype(F32)
    valid = jnp.logical_or(vcur, i > 0)
    return lane < 64, vcur, dist, valid


def _head_key(h):
    return (h // 4, h % 2)


def _stack_kv(prev, cur, g):
    return jnp.concatenate([prev[(g, 0)], cur[(g, 0)], prev[(g, 1)], cur[(g, 1)]], axis=0)


def _split_cols(p, vcur):
    return [jnp.where(vcur, 0.0, p).astype(BF16), jnp.where(vcur, p, 0.0).astype(BF16)]


def _attn_scores(q_ref, rows, stacked, vcur):
    out = []
    for col in range(4):
        big = _dot_nt(q_ref[rows, col * 128:(col + 1) * 128], stacked[col // 2])
        for half in range(2):
            out.append(jnp.where(vcur, big[:, half * 256 + 128:half * 256 + 256], big[:, half * 256:half * 256 + 128]))
    return out


def _attn_scores_by_head(q_ref, rows, kc, kp, vcur):
    out = []
    for h in range(8):
        qh = q_ref[rows, (h // 2) * 128:(h // 2 + 1) * 128]
        out.append(jnp.where(vcur, _dot_nt(qh, kc[_head_key(h)]), _dot_nt(qh, kp[_head_key(h)])))
    return out


def _attn_probs(s, h, sink, dist, valid):
    s = s * ATTN_SCALE - (2.0 ** -(h + 1)) * dist
    if valid is not None:
        s = jnp.where(valid, s, NEG_BIG)
    m = jnp.maximum(jnp.max(s, axis=1, keepdims=True), sink)
    e = jnp.exp(s - m)
    es = jnp.exp(sink - m)
    inv = 1.0 / (jnp.sum(e, axis=1, keepdims=True) + es)
    return e * inv, es * inv


def _kv_prev_spec(blocks):
    return pl.BlockSpec((BLK, 2 * KV_W), lambda i: (jnp.maximum(i * blocks - 1, 0), 0))


def _kv_variants(kv_ref, rows, low):
    return (_head_variants(kv_ref[rows, 0:128].astype(F32), low), _head_variants(kv_ref[rows, 128:256].astype(F32), low))


def _gm_forward_block(zu, zv, lng, lnb, w_ref, bsx, low):
    gu = _gelu(zu)
    gv = _gelu(zv)
    mu = jnp.mean(gv, axis=-1, keepdims=True)
    xc = gv - mu
    rstd = lax.rsqrt(jnp.mean(xc * xc, axis=-1, keepdims=True) + NORM_EPS)
    xn = xc * rstd
    ln = xn * lng + lnb
    mixed = []
    for col in range(4):
        lc = ln[:, col * 128:(col + 1) * 128]
        lo = jnp.where(low, lc, 0.0).astype(BF16)
        hi = jnp.where(low, 0.0, lc).astype(BF16)
        mixed.append(_dot(w_ref[2 * col], lo) + _dot(w_ref[2 * col + 1], hi) + bsx[:, col * 128:(col + 1) * 128])
    return gu, ln, xn, rstd, mixed


def _mix_fwd(q, kv, sinks, zu, zv, lng, lnb, wtril, bsx):
    t = q.shape[0]
    tq = _tile(t, 512)
    blocks = tq // BLK

    def body(sink_ref, q_ref, kvc_ref, kvp_ref, zu_ref, zv_ref, g_ref, b_ref, w_ref, bs_ref, attn_ref, gm_ref):
        low, vcur, dist, valid = _attn_masks(pl.program_id(0))
        kp, vp = _kv_variants(kvp_ref, slice(None), low)
        for b in range(blocks):
            rows = slice(b * BLK, (b + 1) * BLK)
            kc, vc = _kv_variants(kvc_ref, rows, low)
            ks = [_stack_kv(kp, kc, g) for g in range(2)]
            vs = [_stack_kv(vp, vc, g) for g in range(2)]
            scores = _attn_scores(q_ref, rows, ks, vcur)
            gu, _, _, _, mixed = _gm_forward_block(zu_ref[rows, :], zv_ref[rows, :], g_ref[...], b_ref[...], w_ref,
                                                   bs_ref[...], low)
            probs = [_attn_probs(scores[h], h, sink_ref[h], dist, valid if b == 0 else None)[0] for h in range(8)]
            for col in range(4):
                gm_ref[rows, col * 128:(col + 1) * 128] = (gu[:, col * 128:(col + 1) * 128] * mixed[col]).astype(BF16)
                p_col = jnp.concatenate(_split_cols(probs[2 * col], vcur) + _split_cols(probs[2 * col + 1], vcur), axis=1)
                attn_ref[rows, col * 128:(col + 1) * 128] = _dot(p_col, vs[col // 2]).astype(BF16)
            kp, vp = kc, vc

    wide = _row_spec(tq, GM_W)
    return pl.pallas_call(
        body, name="mix_fwd", grid=(t // tq,),
        in_specs=[pl.BlockSpec(memory_space=pltpu.SMEM), _row_spec(tq, ATTN_W), _row_spec(tq, 2 * KV_W),
                  _kv_prev_spec(blocks), wide, wide, _vec_spec(GM_W), _vec_spec(GM_W),
                  pl.BlockSpec((8, BLK, BLK), lambda i: (0, 0, 0)), pl.BlockSpec((BLK, GM_W), lambda i: (0, 0))],
        out_specs=[_row_spec(tq, ATTN_W), wide],
        out_shape=[jax.ShapeDtypeStruct((t, ATTN_W), BF16), jax.ShapeDtypeStruct((t, GM_W), BF16)],
        compiler_params=_params("parallel"),
    )(sinks, q, kv, kv, zu, zv, lng, lnb, wtril, bsx)


def _out_proj(attn, gm, h, ga, gg, gpost, wb, layer):
    t = h.shape[0]
    tm = _tile(t, 512)

    def body(a_ref, m_ref, h_ref, ga_ref, gg_ref, gp_ref, w_ref, heads_ref, mix_ref, h1_ref):
        ha = _rms_fwd(a_ref[...].astype(F32), ga_ref[...]).astype(BF16)
        hg = _rms_fwd(m_ref[...].astype(F32), gg_ref[...]).astype(BF16)
        heads_ref[:, 0:512] = ha
        heads_ref[:, 512:1024] = hg
        mix = _dot(ha, w_ref[0:512, :]) + _dot(hg, w_ref[512:1024, :])
        mix_ref[...] = mix.astype(BF16)
        h1_ref[...] = h_ref[...] + _rms_fwd(mix, gp_ref[...])

    return pl.pallas_call(
        body, name="out_proj", grid=(t // tm,),
        in_specs=[_row_spec(tm, ATTN_W), _row_spec(tm, GM_W), _row_spec(tm, D_MODEL), _vec_spec(ATTN_W),
                  _vec_spec(GM_W), _vec_spec(D_MODEL), _seg_spec(ROWS_B, D_MODEL, 2 * layer)],
        out_specs=[_row_spec(tm, D_MODEL), _row_spec(tm, D_MODEL), _row_spec(tm, D_MODEL)],
        out_shape=[jax.ShapeDtypeStruct((t, D_MODEL), BF16), jax.ShapeDtypeStruct((t, D_MODEL), BF16),
                   jax.ShapeDtypeStruct((t, D_MODEL), F32)],
        compiler_params=_params("parallel"),
    )(attn, gm, h, ga, gg, gpost, wb)


def _ffn_fwd(h1, p, p_layer, gpre, gpost, gple, wa, wb, wp, layer, target=None):
    t = h1.shape[0]
    tm = _tile(t, 256)

    def body(h_ref, p_ref, gpre_ref, gpost_ref, gple_ref, wg_ref, wu_ref, wd_ref, wpg_ref, wpl_ref, *rest):
        if target is None:
            f_ref, gp_ref, up_ref, act_ref, fo_ref, h2_ref, hn_ref, gate_ref, h3_ref = rest
        else:
            t_ref, f_ref, gp_ref, up_ref, act_ref, fo_ref, h2_ref, hn_ref, gate_ref, dy_ref, l_ref = rest
            _zero_at(pl.program_id(0) == 0, l_ref)
        h = h_ref[...]
        pe = _dot_nt(p_ref[...].astype(BF16), wpl_ref[...])
        f = _rms_fwd(h, gpre_ref[...]).astype(BF16)
        f_ref[...] = f
        chunks = [slice(j * FF_CHUNK, (j + 1) * FF_CHUNK) for j in range(D_FF // FF_CHUNK)]
        fo = None
        gp, up = _dot_nt(f, wg_ref[chunks[0], :]), _dot_nt(f, wu_ref[chunks[0], :])
        for j, cols in enumerate(chunks):
            if j + 1 < len(chunks):
                gp_next, up_next = _dot_nt(f, wg_ref[chunks[j + 1], :]), _dot_nt(f, wu_ref[chunks[j + 1], :])
            act = (gp * _sigmoid(gp) * up).astype(BF16)
            gp_ref[:, cols] = gp.astype(BF16)
            up_ref[:, cols] = up.astype(BF16)
            act_ref[:, cols] = act
            part = _dot(act, wd_ref[cols, :])
            fo = part if fo is None else fo + part
            if j + 1 < len(chunks):
                gp, up = gp_next, up_next
        fo_ref[...] = fo
        h2 = h + _rms_fwd(fo, gpost_ref[...])
        h2_ref[...] = h2
        hn = _rms_fwd(h2, gple_ref[...]).astype(BF16)
        hn_ref[...] = hn
        gate = _sigmoid(_dot(hn, wpg_ref[...]))
        gate_ref[...] = gate.astype(BF16)
        h3 = h2 + pe * gate
        if target is None:
            h3_ref[...] = h3
        else:
            e = h3 - t_ref[...]
            dy_ref[...] = (e * (1.0 / D_MODEL)).astype(BF16)
            s = jnp.sum(jnp.sum(e * e, axis=1, keepdims=True), axis=0, keepdims=True)
            l_ref[...] += jnp.broadcast_to(s, (1, 128))

    wide = _row_spec(tm, D_FF)
    row = _row_spec(tm, D_MODEL)
    vec = _vec_spec(D_MODEL)
    last = target is not None
    return pl.pallas_call(
        body, name="ffn_loss" if last else "ffn_fwd", grid=(t // tm,),
        in_specs=[row, _layer_row_spec(tm, PLE_DIM, p_layer), vec, vec, vec, _seg_spec(ROWS_A, D_MODEL, 3 * layer),
                  _seg_spec(ROWS_A, D_MODEL, 3 * layer + 1), _seg_spec(ROWS_A, D_MODEL, 3 * layer + 2),
                  _seg_spec(ROWS_B, D_MODEL, 2 * layer + 1), _seg_spec(ROWS_B, PLE_DIM, layer)] + [row] * last,
        out_specs=[row, wide, wide, wide, row, row, row, row, row] + [_vec_spec(128)] * last,
        out_shape=[jax.ShapeDtypeStruct((t, D_MODEL), BF16)] + [jax.ShapeDtypeStruct((t, D_FF), BF16)] * 3
        + [jax.ShapeDtypeStruct((t, D_MODEL), F32)] * 2 + [jax.ShapeDtypeStruct((t, D_MODEL), BF16)] * 2
        + [jax.ShapeDtypeStruct((t, D_MODEL), BF16 if last else F32)] + [jax.ShapeDtypeStruct((1, 128), F32)] * last,
        compiler_params=_params("arbitrary" if last else "parallel"),
    )(h1, p, gpre, gpost, gple, wa, wa, wa, wb, wp, *([target] if last else []))


def _ffn_bwd(dh3, h2, gate, p, p_layer, fo, gp, up, h1, gple, gpost, gpre, wa, wb, wp, layer, after):
    t = dh3.shape[0]
    tm = _tile(t, 256)

    def body(d3_ref, h2_ref, gate_ref, p_ref, fo_ref, gp_ref, up_ref, h_ref, gple_ref, gpost_ref, gpre_ref,
             wg_ref, wu_ref, wd_ref, wpg_ref, wpl_ref, after_ref,
             dgl_ref, dpe_ref, dfo_ref, dgp_ref, dup_ref, dh1_ref, dgple_ref, dgpost_ref, dgpre_ref):
        _zero_at(pl.program_id(0) == 0, dgple_ref, dgpost_ref, dgpre_ref)
        d3 = d3_ref[...].astype(F32)
        gate = gate_ref[...].astype(F32)
        pe = _dot_nt(p_ref[...].astype(BF16), wpl_ref[...])
        dpe_ref[...] = (d3 * gate).astype(BF16)
        dgl = (d3 * pe * gate * (1.0 - gate)).astype(BF16)
        dgl_ref[...] = dgl
        dx2, dgple = _rms_bwd(h2_ref[...], gple_ref[...], _dot_nt(dgl, wpg_ref[...]))
        dgple_ref[...] += dgple
        d = d3 + dx2
        dfo, dgpost = _rms_bwd(fo_ref[...], gpost_ref[...], d)
        dfo = dfo.astype(BF16)
        dfo_ref[...] = dfo
        dgpost_ref[...] += dgpost
        chunks = [slice(j * FF_CHUNK, (j + 1) * FF_CHUNK) for j in range(D_FF // FF_CHUNK)]
        df = None
        dact = _dot_nt(dfo, wd_ref[chunks[0], :])
        for j, cols in enumerate(chunks):
            if j + 1 < len(chunks):
                dact_next = _dot_nt(dfo, wd_ref[chunks[j + 1], :])
            gp = gp_ref[:, cols].astype(F32)
            sg = _sigmoid(gp)
            dgp = (dact * up_ref[:, cols].astype(F32) * (sg * (1.0 + gp * (1.0 - sg)))).astype(BF16)
            dup = (dact * (gp * sg)).astype(BF16)
            dgp_ref[:, cols] = dgp
            dup_ref[:, cols] = dup
            part = _dot(dgp, wg_ref[cols, :]) + _dot(dup, wu_ref[cols, :])
            df = part if df is None else df + part
            if j + 1 < len(chunks):
                dact = dact_next
        dx, dgpre = _rms_bwd(h_ref[...], gpre_ref[...], df)
        dh1_ref[...] = (d + dx).astype(BF16)
        dgpre_ref[...] += dgpre

    wide = _row_spec(tm, D_FF)
    row = _row_spec(tm, D_MODEL)
    vec = _vec_spec(D_MODEL)
    narrow = jax.ShapeDtypeStruct((t, D_MODEL), BF16)
    return pl.pallas_call(
        body, name="ffn_bwd", grid=(t // tm,),
        in_specs=[row, row, row, _layer_row_spec(tm, PLE_DIM, p_layer), row, wide, wide, row, vec, vec, vec,
                  _seg_spec(ROWS_A, D_MODEL, 3 * layer), _seg_spec(ROWS_A, D_MODEL, 3 * layer + 1),
                  _seg_spec(ROWS_A, D_MODEL, 3 * layer + 2), _seg_spec(ROWS_B, D_MODEL, 2 * layer + 1),
                  _seg_spec(ROWS_B, PLE_DIM, layer), pl.BlockSpec(memory_space=pl.ANY)],
        out_specs=[row, row, row, wide, wide, row, vec, vec, vec],
        out_shape=[narrow, narrow, narrow, jax.ShapeDtypeStruct((t, D_FF), BF16), jax.ShapeDtypeStruct((t, D_FF), BF16),
                   narrow] + [jax.ShapeDtypeStruct((1, D_MODEL), F32)] * 3,
        compiler_params=_params("arbitrary"),
    )(dh3, h2, gate, p, fo, gp, up, h1, gple, gpost, gpre, wa, wa, wa, wb, wp, after)


def _out_proj_bwd(dh1, mix, attn, gm, gpost, ga, gg, wb, layer, after):
    t = dh1.shape[0]
    tm = _tile(t, 512)

    def body(d_ref, mix_ref, a_ref, m_ref, gp_ref, ga_ref, gg_ref, w_ref, after_ref,
             dmix_ref, da_ref, dm_ref, dgp_ref, dga_ref, dgg_ref):
        _zero_at(pl.program_id(0) == 0, dgp_ref, dga_ref, dgg_ref)
        dmix, dgp = _rms_bwd(mix_ref[...].astype(F32), gp_ref[...], d_ref[...].astype(F32))
        dmix = dmix.astype(BF16)
        dmix_ref[...] = dmix
        da, dga = _rms_bwd(a_ref[...].astype(F32), ga_ref[...], _dot_nt(dmix, w_ref[0:512, :]))
        dm, dgg = _rms_bwd(m_ref[...].astype(F32), gg_ref[...], _dot_nt(dmix, w_ref[512:1024, :]))
        da_ref[...] = da.astype(BF16)
        dm_ref[...] = dm
        dgp_ref[...] += dgp
        dga_ref[...] += dga
        dgg_ref[...] += dgg

    return pl.pallas_call(
        body, name="out_proj_bwd", grid=(t // tm,),
        in_specs=[_row_spec(tm, D_MODEL), _row_spec(tm, D_MODEL), _row_spec(tm, ATTN_W), _row_spec(tm, GM_W),
                  _vec_spec(D_MODEL), _vec_spec(ATTN_W), _vec_spec(GM_W), _seg_spec(ROWS_B, D_MODEL, 2 * layer),
                  pl.BlockSpec(memory_space=pl.ANY)],
        out_specs=[_row_spec(tm, D_MODEL), _row_spec(tm, ATTN_W), _row_spec(tm, GM_W),
                   _vec_spec(D_MODEL), _vec_spec(ATTN_W), _vec_spec(GM_W)],
        out_shape=[jax.ShapeDtypeStruct((t, D_MODEL), BF16), jax.ShapeDtypeStruct((t, ATTN_W), BF16),
                   jax.ShapeDtypeStruct((t, GM_W), F32), jax.ShapeDtypeStruct((1, D_MODEL), F32),
                   jax.ShapeDtypeStruct((1, ATTN_W), F32), jax.ShapeDtypeStruct((1, GM_W), F32)],
        compiler_params=_params("arbitrary"),
    )(dh1, mix, attn, gm, gpost, ga, gg, wb, after)


def _split3(x):
    hi = x.astype(BF16)
    r1 = x - hi.astype(F32)
    mid = r1.astype(BF16)
    lo = (r1 - mid.astype(F32)).astype(BF16)
    return hi, mid, lo


def _rider_pieces(rider, blocks):
    rm = rider[0].shape[1]
    per = -(-rm // (blocks * 256)) * 256
    return [slice(k * per, min((k + 1) * per, rm)) for k in range(blocks) if k * per < rm]


def _gm_bwd(dgm, zu, zv, lng, lnb, wtril, bsx, rider=None):
    t = zu.shape[0]
    tm = _tile(t, 512)
    nb = t // tm
    if rider is not None:
        ra, rb, rbuf, rseg = rider
        rm, rn = ra.shape[1], rb.shape[1]
        pieces = _rider_pieces(rider, tm // BLK)

    def body(d_ref, zu_ref, zv_ref, g_ref, b_ref, w_ref, bs_ref, *rest):
        if rider is None:
            dzu_ref, dzv_ref, dw_ref, dbs_ref, dlg_ref, dlb_ref, dbsx_ref = rest
        else:
            ra_ref, rb_ref, rbuf_ref, dzu_ref, dzv_ref, dw_ref, dbs_ref, dlg_ref, dlb_ref, ro_ref, dbsx_ref, acc_ref = rest
        i = pl.program_id(0)
        _zero_at(i == 0, dw_ref, dlg_ref, dlb_ref, dbsx_ref)
        if rider is not None:
            _zero_at(i == 0, acc_ref)
            rb16 = rb_ref[...].astype(BF16)
        row = lax.broadcasted_iota(jnp.int32, (BLK, BLK), 0)
        lane = lax.broadcasted_iota(jnp.int32, (BLK, BLK), 1)
        low = lane < 64
        tril = row >= lane
        lng = g_ref[...]
        for b in range(tm // BLK):
            rows = slice(b * BLK, (b + 1) * BLK)
            if rider is not None and b < len(pieces):
                acc_ref[pieces[b], :] += _dot_tn(ra_ref[:, pieces[b]], rb16)
            zu = zu_ref[rows, :]
            zv = zv_ref[rows, :]
            gu, ln, xn, rstd, mixed = _gm_forward_block(zu, zv, lng, b_ref[...], w_ref, bs_ref[...], low)
            dgm = d_ref[rows, :]
            dgu_cols, dmx_cols, dln_cols = [], [], []
            for col in range(4):
                sl = slice(col * 128, (col + 1) * 128)
                dg = dgm[:, sl]
                dgu_cols.append(dg * mixed[col])
                dmx = dg * gu[:, sl]
                dmx_cols.append(dmx)
                lc = ln[:, sl]
                halves = (jnp.where(low, lc, 0.0).astype(BF16), jnp.where(low, 0.0, lc).astype(BF16))
                dmx16 = dmx.astype(BF16)
                dmx_half = (jnp.where(low, dmx, 0.0).astype(BF16), jnp.where(low, 0.0, dmx).astype(BF16))
                dln = None
                for half in range(2):
                    hd = 2 * col + half
                    dw_ref[hd] += jnp.where(tril, _dot_nt(dmx16, halves[half]), 0.0)
                    part = _dot_tn(w_ref[hd], dmx_half[half])
                    dln = part if dln is None else dln + part
                dln_cols.append(dln)
            dgu = jnp.concatenate(dgu_cols, axis=1)
            dmx = jnp.concatenate(dmx_cols, axis=1)
            dln = jnp.concatenate(dln_cols, axis=1)
            dzu_ref[rows, :] = (dgu * _gelu_grad(zu)).astype(BF16)
            dbsx_ref[...] += dmx
            dlg_ref[...] += jnp.sum(dln * xn, axis=0, keepdims=True)
            dlb_ref[...] += jnp.sum(dln, axis=0, keepdims=True)
            dxn = dln * lng
            dgv = rstd * (dxn - jnp.mean(dxn, axis=-1, keepdims=True) - xn * jnp.mean(dxn * xn, axis=-1, keepdims=True))
            dzv_ref[rows, :] = (dgv * _gelu_grad(zv)).astype(BF16)

        @pl.when(i == nb - 1)
        def _():
            r = lax.broadcasted_iota(jnp.int32, (GM_W, BLK), 0)
            c = lax.broadcasted_iota(jnp.int32, (GM_W, BLK), 1)
            e = jnp.where(jnp.logical_and(r >= c * 64, r < c * 64 + 64), 1.0, 0.0).astype(BF16)
            hi, mid, lo = _split3(dbsx_ref[...])
            dbs_ref[...] = _dot(hi, e) + _dot(mid, e) + _dot(lo, e)
            if rider is not None:
                ro_ref[...] = acc_ref[...].astype(ro_ref.dtype)

    vec = _vec_spec(GM_W)
    in_specs = [_row_spec(tm, GM_W)] * 3 + [vec, vec, pl.BlockSpec((8, BLK, BLK), lambda i: (0, 0, 0)),
                                            pl.BlockSpec((BLK, GM_W), lambda i: (0, 0))]
    out_specs = [_row_spec(tm, GM_W), _row_spec(tm, GM_W), pl.BlockSpec((8, BLK, BLK), lambda i: (0, 0, 0)),
                 pl.BlockSpec((BLK, BLK), lambda i: (0, 0)), vec, vec]
    out_shape = [jax.ShapeDtypeStruct((t, GM_W), BF16), jax.ShapeDtypeStruct((t, GM_W), BF16),
                 jax.ShapeDtypeStruct((8, BLK, BLK), F32), jax.ShapeDtypeStruct((BLK, BLK), F32),
                 jax.ShapeDtypeStruct((1, GM_W), F32), jax.ShapeDtypeStruct((1, GM_W), F32)]
    scratch = [pltpu.VMEM((BLK, GM_W), F32)]
    operands = [dgm, zu, zv, lng, lnb, wtril, bsx]
    extra = {}
    if rider is not None:
        in_specs += [_row_spec(tm, rm), _row_spec(tm, rn), pl.BlockSpec(memory_space=pl.ANY)]
        out_specs.append(pl.BlockSpec((rm, rn), lambda i: (rseg, 0)))
        out_shape.append(jax.ShapeDtypeStruct(rbuf.shape, rbuf.dtype))
        scratch.append(pltpu.VMEM((rm, rn), F32))
        operands += [ra, rb, rbuf]
        extra = dict(input_output_aliases={9: 6})
    return pl.pallas_call(
        body, name="gm_bwd" if rider is None else "gm_bwd_rider", grid=(nb,),
        in_specs=in_specs, out_specs=out_specs, out_shape=out_shape, scratch_shapes=scratch,
        compiler_params=_params("arbitrary"), **extra,
    )(*operands)


def _attn_bwd(q, kv, do, sinks, after, rider=None):
    t = q.shape[0]
    tq = _tile(t, 512)
    blocks = tq // BLK
    steps = t // tq
    if rider is not None:
        ra, rb, rbuf, rseg = rider
        rm, rn = ra.shape[1], rb.shape[1]
        per = -(-rm // (blocks * 256)) * 256
        pieces = [slice(k * per, min((k + 1) * per, rm)) for k in range(blocks)]

    def body(sink_ref, q_ref, kvc_ref, kvp_ref, do_ref, after_ref, *rest):
        if rider is None:
            dq_ref, dkv_ref, dkf_ref, ds_ref = rest
        else:
            ra_ref, rb_ref, rbuf_ref, dq_ref, dkv_ref, dkf_ref, ds_ref, ro_ref, acc_ref = rest
        i = pl.program_id(0)
        _zero_at(i == 0, ds_ref)
        if rider is not None:
            _zero_at(i == 0, acc_ref)
            rb16 = rb_ref[...].astype(BF16)
        low, vcur, dist, valid = _attn_masks(i)
        head_row = lax.broadcasted_iota(jnp.int32, (8, 128), 0)
        dsink_tile = jnp.zeros((8, 128), F32)
        kp, vp = _kv_variants(kvp_ref, slice(None), low)
        own = None
        for b in range(blocks):
            rows = slice(b * BLK, (b + 1) * BLK)
            kc, vc = _kv_variants(kvc_ref, rows, low)
            scores = _attn_scores_by_head(q_ref, rows, kc, kp, vcur)
            dprobs = _attn_scores_by_head(do_ref, rows, vc, vp, vcur)
            if rider is not None and pieces[b].start < rm:
                acc_ref[pieces[b], :] += _dot_tn(ra_ref[:, pieces[b]], rb16)
            parts = []
            for h in range(8):
                p, ps = _attn_probs(scores[h], h, sink_ref[h], dist, valid if b == 0 else None)
                delta = jnp.sum(p * dprobs[h], axis=1, keepdims=True)
                ds = p * (dprobs[h] - delta) * ATTN_SCALE
                dsink = jnp.sum(-ps * delta, axis=0, keepdims=True)
                dsink_tile = jnp.where(head_row == h, dsink_tile + dsink, dsink_tile)
                parts.append(_split_cols(ds, vcur) + _split_cols(p, vcur))
            acc = {}

            def add(name, key, val):
                acc[(name, key)] = val if (name, key) not in acc else acc[(name, key)] + val

            for col in range(4):
                qh = q_ref[rows, col * 128:(col + 1) * 128]
                doh = do_ref[rows, col * 128:(col + 1) * 128]
                dq = None
                for half in range(2):
                    key = _head_key(2 * col + half)
                    dsp, dsc, pp, pc = parts[2 * col + half]
                    part = _dot(dsc, kc[key]) + _dot(dsp, kp[key])
                    dq = part if dq is None else dq + part
                    add("kc", key, _dot_tn(dsc, qh))
                    add("kp", key, _dot_tn(dsp, qh))
                    add("vc", key, _dot_tn(pc, doh))
                    add("vp", key, _dot_tn(pp, doh))
                dq_ref[rows, col * 128:(col + 1) * 128] = dq.astype(BF16)

            def place(name):
                head0 = acc[(name, (0, 0))] + pltpu.roll(acc[(name, (0, 1))], 64, axis=1)
                head1 = pltpu.roll(acc[(name, (1, 0))], 64, axis=1) + acc[(name, (1, 1))]
                return jnp.where(low, head0, head1)

            before = (place("kp"), place("vp"))
            if b == 0:
                dkf_ref[:, 0:128], dkf_ref[:, 128:256] = before
            else:
                last = slice((b - 1) * BLK, b * BLK)
                dkv_ref[last, 0:128] = own[0] + before[0]
                dkv_ref[last, 128:256] = own[1] + before[1]
            own = (place("kc"), place("vc"))
            kp, vp = kc, vc
        final = slice((blocks - 1) * BLK, blocks * BLK)
        dkv_ref[final, 0:128], dkv_ref[final, 128:256] = own
        ds_ref[...] += dsink_tile
        if rider is not None:
            @pl.when(i == steps - 1)
            def _():
                ro_ref[...] = acc_ref[...].astype(ro_ref.dtype)

    row_q = _row_spec(tq, ATTN_W)
    row_kv = _row_spec(tq, 2 * KV_W)
    in_specs = [pl.BlockSpec(memory_space=pltpu.SMEM), row_q, row_kv, _kv_prev_spec(blocks), row_q,
                pl.BlockSpec(memory_space=pl.ANY)]
    out_specs = [row_q, row_kv, _row_spec(BLK, 2 * KV_W), pl.BlockSpec((8, 128), lambda i: (0, 0))]
    out_shape = [jax.ShapeDtypeStruct((t, ATTN_W), BF16), jax.ShapeDtypeStruct((t, 2 * KV_W), F32),
                 jax.ShapeDtypeStruct((t // tq * BLK, 2 * KV_W), F32), jax.ShapeDtypeStruct((8, 128), F32)]
    operands = [sinks, q, kv, kv, do, after]
    extra = {}
    if rider is not None:
        in_specs += [_row_spec(tq, rm), _row_spec(tq, rn), pl.BlockSpec(memory_space=pl.ANY)]
        out_specs.append(pl.BlockSpec((rm, rn), lambda i: (rseg, 0)))
        out_shape.append(jax.ShapeDtypeStruct(rbuf.shape, rbuf.dtype))
        operands += [ra, rb, rbuf]
        extra = dict(scratch_shapes=[pltpu.VMEM((rm, rn), F32)], input_output_aliases={8: 4})
    return pl.pallas_call(
        body, name="attn_bwd" if rider is None else "attn_bwd_rider", grid=(steps,),
        in_specs=in_specs, out_specs=out_specs, out_shape=out_shape,
        compiler_params=_params("arbitrary"), **extra,
    )(*operands)


def _in_proj_bwd(dq, dkv, dkf, dzu, dzv, h, dres, g, wc, layer, dh_dtype):
    t = h.shape[0]
    tm = _tile(t, 512)
    steps = t // tm

    def body(dq_ref, dkv_ref, dkn_ref, dzu_ref, dzv_ref, h_ref, d_ref, g_ref, w_ref, dz_ref, dh_ref, dg_ref):
        i = pl.program_id(0)
        _zero_at(i == 0, dg_ref)
        dq = dq_ref[...]
        tail = dkv_ref[tm - BLK:tm, :] + jnp.where(i < steps - 1, dkn_ref[...], 0.0)
        dkv = tail if tm == BLK else jnp.concatenate([dkv_ref[0:tm - BLK, :], tail], axis=0)
        dkv = dkv.astype(BF16)
        dzu = dzu_ref[...]
        dzv = dzv_ref[...]
        dz_ref[:, 0:512] = dq
        dz_ref[:, 512:768] = dkv
        dz_ref[:, 768:1280] = dzu
        dz_ref[:, 1280:1792] = dzv
        da = (_dot(dq, w_ref[0:512, :]) + _dot(dkv, w_ref[512:768, :]) + _dot(dzu, w_ref[768:1280, :])
              + _dot(dzv, w_ref[1280:1792, :]))
        dx, dg = _rms_bwd(h_ref[...], g_ref[...], da)
        dh_ref[...] = (d_ref[...].astype(F32) + dx).astype(dh_dtype)
        dg_ref[...] += dg

    return pl.pallas_call(
        body, name="in_proj_bwd", grid=(t // tm,),
        in_specs=[_row_spec(tm, ATTN_W), _row_spec(tm, 2 * KV_W),
                  pl.BlockSpec((BLK, 2 * KV_W), lambda i: (jnp.minimum(i + 1, steps - 1), 0)), _row_spec(tm, GM_W),
                  _row_spec(tm, GM_W), _row_spec(tm, D_MODEL), _row_spec(tm, D_MODEL), _vec_spec(D_MODEL),
                  _seg_spec(ROWS_C, D_MODEL, layer)],
        out_specs=[_row_spec(tm, D_IN), _row_spec(tm, D_MODEL), _vec_spec(D_MODEL)],
        out_shape=[jax.ShapeDtypeStruct((t, D_IN), BF16), jax.ShapeDtypeStruct((t, D_MODEL), dh_dtype),
                   jax.ShapeDtypeStruct((1, D_MODEL), F32)],
        compiler_params=_params("arbitrary"),
    )(dq, dkv, dkf, dzu, dzv, h, dres, g, wc)


def _weight_grad(a, b, buf, seg, b_layer=None):
    t, m = a.shape
    n = b.shape[-1]
    assert buf.shape[0] % m == 0 and buf.shape[1] == n
    tm = _tile(t, WGRAD_TOKENS)
    steps = t // tm
    half = m // 2

    def body(a_ref, b_ref, buf_ref, o_ref, acc_ref):
        i = pl.program_id(0)
        _zero_at(i == 0, acc_ref)
        b16 = b_ref[...].astype(BF16)
        for rows in (slice(0, half), slice(half, m)):
            acc_ref[rows, :] += _dot_tn(a_ref[:, rows], b16)

        @pl.when(i == steps - 1)
        def _():
            o_ref[...] = acc_ref[...].astype(o_ref.dtype)

    return pl.pallas_call(
        body, name="weight_grad", grid=(steps,),
        in_specs=[_row_spec(tm, m), _row_spec(tm, n) if b_layer is None else _layer_row_spec(tm, n, b_layer),
                  pl.BlockSpec(memory_space=pl.ANY)],
        out_specs=pl.BlockSpec((m, n), lambda i: (seg, 0)),
        out_shape=jax.ShapeDtypeStruct(buf.shape, buf.dtype),
        scratch_shapes=[pltpu.VMEM((m, n), F32)],
        input_output_aliases={2: 0},
        compiler_params=_params("arbitrary"),
    )(a, b, buf)


def _rows8(rows):
    return [jnp.pad(r, ((0, 7), (0, 0))) for r in rows]


def _small_pack_gating(d):
    rows = [jnp.concatenate([d["gm_ln_g"], d["gm_ln_b"]], axis=1), d["gm_bs"].reshape(1, 1024)]
    return jnp.concatenate(_rows8(rows) + [d["gm_ws"].reshape(128, 1024)], axis=0)


def _small_pack_rest(d):
    rows = [d["ln_mix_pre"], d["ln_mix_post"], d["ln_ffn_pre"], d["ln_ffn_post"], d["ln_ple_gate"],
            jnp.concatenate([d["g_attn_out"], d["g_gm_out"]], axis=1),
            jnp.pad(d["attn_sinks"].reshape(1, 8), ((0, 0), (0, 1016)))]
    return jnp.concatenate(_rows8(rows), axis=0)


def _unpack_gating(g):
    return {"gm_ln_g": g[:, 0, :512], "gm_ln_b": g[:, 0, 512:], "gm_bs": g[:, 8].reshape(DEPTH, 8, 128),
            "gm_ws": g[:, 16:GATING_ROWS].reshape(DEPTH, 8, 128, 128)}


def _unpack_rest(s):
    return {"ln_mix_pre": s[:, 0], "ln_mix_post": s[:, 8], "ln_ffn_pre": s[:, 16], "ln_ffn_post": s[:, 24],
            "ln_ple_gate": s[:, 32], "g_attn_out": s[:, 40, :512], "g_gm_out": s[:, 40, 512:], "attn_sinks": s[:, 48, :8]}


def _row(v):
    return v.reshape(1, -1)


def _layer_fwd(h, p, sp, l, weights, target=None):
    tril = jnp.tril(jnp.ones((BLK, BLK), bool))
    wtril = jnp.where(tril[None], sp["gm_ws"][l], 0.0).astype(BF16)
    bsx = jnp.repeat(sp["gm_bs"][l].T, HEAD_DIM, axis=1)
    a, q, kv, zu, zv = _in_proj(h, _row(sp["ln_mix_pre"][l]), weights("c", h), 0)
    attn, gm = _mix_fwd(q, kv, sp["attn_sinks"][l], zu, zv, _row(sp["gm_ln_g"][l]), _row(sp["gm_ln_b"][l]), wtril, bsx)
    wb = weights("b", gm)
    heads, mix, h1 = _out_proj(attn, gm, h, _row(sp["g_attn_out"][l]), _row(sp["g_gm_out"][l]),
                               _row(sp["ln_mix_post"][l]), wb, 0)
    wa = weights("a", h1)
    f, gpre, up, act, fo, h2, hn, gate, *out = _ffn_fwd(
        h1, p, l, _row(sp["ln_ffn_pre"][l]), _row(sp["ln_ffn_post"][l]), _row(sp["ln_ple_gate"][l]), wa, wb,
        weights("p", gm), 0, target)
    saved = dict(h=h, a=a, q=q, kv=kv, zu=zu, zv=zv, attn=attn, gm=gm, heads=heads, mix=mix, h1=h1, f=f,
                 gpre=gpre, up=up, act=act, fo=fo, h2=h2, hn=hn, gate=gate, wtril=wtril, bsx=bsx)
    return (out[0] if target is None else tuple(out)), saved


def _layer_bwd_upper(dh, s, p, sp, l, wa, wb, wp, after, ride):
    d = {}
    dgl, dpe, dfo, dgp, dup, dh1, d["ln_ple_gate"], d["ln_ffn_post"], d["ln_ffn_pre"] = _ffn_bwd(
        dh, s["h2"], s["gate"], p, l, s["fo"], s["gpre"], s["up"], s["h1"], _row(sp["ln_ple_gate"][l]),
        _row(sp["ln_ffn_post"][l]), _row(sp["ln_ffn_pre"][l]), wa, wb, wp, 0, after)
    gb = _weight_grad(s["hn"], dgl, lax.empty((2 * D_MODEL, D_MODEL), BF16), 1)
    gp = _weight_grad(dpe, p, lax.empty((D_MODEL, PLE_DIM), BF16), 0, b_layer=l)
    ga = _weight_grad(s["act"], dfo, lax.empty((3 * D_FF, D_MODEL), BF16), 2)
    if not ride:
        ga = _weight_grad(dgp, s["f"], ga, 0)
        ga = _weight_grad(dup, s["f"], ga, 1)
    return (dh1, d, dgp, dup), ga, gp, gb


def _layer_bwd_middle(carry, s, sp, l, wb, gb, ga, after, ride):
    dh1, d, dgp, dup = carry
    dmix, dattn, dgm, d["ln_mix_post"], d["g_attn_out"], d["g_gm_out"] = _out_proj_bwd(
        dh1, s["mix"], s["attn"], s["gm"], _row(sp["ln_mix_post"][l]), _row(sp["g_attn_out"][l]),
        _row(sp["g_gm_out"][l]), wb, 0, after)
    gb = _weight_grad(s["heads"], dmix, gb, 0)
    dzu, dzv, d["gm_ws"], dbs, d["gm_ln_g"], d["gm_ln_b"], *rode = _gm_bwd(
        dgm, s["zu"], s["zv"], _row(sp["gm_ln_g"][l]), _row(sp["gm_ln_b"][l]), s["wtril"], s["bsx"],
        (dup, s["f"], ga, 1) if ride else None)
    d["gm_bs"] = dbs[:, :8].T
    return (dh1, dattn, dzu, dzv, d, dgp), gb, (rode[0] if ride else ga), _small_pack_gating(d)


def _layer_bwd_lower(carry, s, sp, l, wc, ga, after, ride):
    dh1, dattn, dzu, dzv, d, dgp = carry
    dq, dkv, dkf, dsink, *rode = _attn_bwd(s["q"], s["kv"], dattn, sp["attn_sinks"][l], after,
                                            (dgp, s["f"], ga, 0) if ride else None)
    ga = rode[0] if ride else ga
    d["attn_sinks"] = dsink[:, 0]
    dz, dh, d["ln_mix_pre"] = _in_proj_bwd(dq, dkv, dkf, dzu, dzv, s["h"], dh1, _row(sp["ln_mix_pre"][l]), wc, 0,
                                           F32 if l == 0 else BF16)
    gc = _weight_grad(dz, s["a"], lax.empty((D_IN, D_MODEL), BF16), 0)
    return dh, gc, ga, _small_pack_rest(d)


ANY = pl.BlockSpec(memory_space=pl.ANY)


HBM = pl.BlockSpec(memory_space=pltpu.HBM)
SEM = pl.BlockSpec(memory_space=pltpu.SEMAPHORE)
N_PEERS = N_DEV - 1


def _peers():
    x, y, c = lax.axis_index("x"), lax.axis_index("y"), lax.axis_index("c")
    peers = []
    for r in range(1, N_DEV):
        px = 1 - x if r & 4 else x
        py = 1 - y if r & 2 else y
        pc = 1 - c if r & 1 else c
        peers.append(((px, py, pc), 4 * px + 2 * py + pc))
    return 4 * x + 2 * y + c, peers


GATHER, SCATTER, SPREAD = "gather", "scatter", "spread"


def _peer_copy(src, land, send_sems, recv_sems, r, me, peer, peer_slot, mode):
    return pltpu.make_async_remote_copy(
        src_ref=src.at[:, pl.ds(peer_slot, 1)] if mode == SCATTER else src,
        dst_ref=land.at[:, pl.ds(me, 1)] if mode == GATHER else land.at[:, pl.ds(r - 1, 1)],
        send_sem=send_sems.at[r - 1], recv_sem=recv_sems.at[r - 1], device_id=peer, device_id_type=MESH)


def _peer_arrival(src, land, send_sems, recv_sems, r, me, peer, peer_slot, mode):
    return pltpu.make_async_remote_copy(
        src_ref=src.at[:, pl.ds(me, 1)] if mode == SCATTER else src,
        dst_ref=land.at[:, pl.ds(peer_slot, 1)] if mode == GATHER else land.at[:, pl.ds(r - 1, 1)],
        send_sem=send_sems.at[r - 1], recv_sem=recv_sems.at[r - 1], device_id=peer, device_id_type=MESH)


def _send_start(name, srcs, lands, modes):
    n = len(srcs)

    def body(*refs):
        src_refs, land_refs = refs[:n], refs[n:2 * n]
        outs = refs[2 * n:]
        send_sems, recv_sems, token = outs[2 * n:3 * n], outs[3 * n:4 * n], outs[4 * n]
        me, peers = _peers()
        for k in range(n):
            for r, (peer, slot) in enumerate(peers, 1):
                _peer_copy(src_refs[k], land_refs[k], send_sems[k], recv_sems[k], r, me, peer, slot, modes[k]).start()
        token[...] = jnp.zeros_like(token)

    hbm = lambda a: pltpu.HBM(a.shape, a.dtype)
    sems = [pltpu.SemaphoreType.DMA((N_PEERS,))] * (2 * n)
    outs = pl.pallas_call(
        body, name=name, in_specs=[HBM] * (2 * n),
        out_specs=[HBM] * (2 * n) + [SEM] * (2 * n) + [pl.BlockSpec(memory_space=pltpu.VMEM)],
        out_shape=[hbm(a) for a in srcs] + [hbm(a) for a in lands] + sems + [jax.ShapeDtypeStruct((8, 128), F32)],
        input_output_aliases={k: k for k in range(2 * n)},
        compiler_params=pltpu.CompilerParams(has_side_effects=pltpu.SideEffectType.DATAFLOW_SIDE_EFFECTING),
    )(*[pltpu.with_memory_space_constraint(a, pltpu.HBM) for a in list(srcs) + list(lands)])
    return dict(srcs=outs[:n], lands=outs[n:2 * n], send=outs[2 * n:3 * n], recv=outs[3 * n:4 * n],
                modes=list(modes)), outs[4 * n]


def _send_wait(name, sent, ks, after):
    n = len(ks)
    srcs = [sent["srcs"][k] for k in ks]
    lands = [sent["lands"][k] for k in ks]
    modes = [sent["modes"][k] for k in ks]

    def body(*refs):
        src_refs, land_refs = refs[:n], refs[n:2 * n]
        send_sems, recv_sems = refs[2 * n:3 * n], refs[3 * n:4 * n]
        me, peers = _peers()
        for k in range(n):
            for r, (peer, slot) in enumerate(peers, 1):
                args = (src_refs[k], land_refs[k], send_sems[k], recv_sems[k], r, me, peer, slot, modes[k])
                _peer_copy(*args).wait_send()
                _peer_arrival(*args).wait_recv()

    hbm = lambda a: pltpu.HBM(a.shape, a.dtype)
    outs = pl.pallas_call(
        body, name=name, in_specs=[HBM] * (2 * n) + [SEM] * (2 * n) + [ANY],
        out_specs=[HBM] * (2 * n), out_shape=[hbm(a) for a in srcs] + [hbm(a) for a in lands],
        input_output_aliases={k: k for k in range(2 * n)},
        compiler_params=pltpu.CompilerParams(has_side_effects=pltpu.SideEffectType.DATAFLOW_SIDE_EFFECTING),
    )(*srcs, *lands, *[sent["send"][k] for k in ks], *[sent["recv"][k] for k in ks], after)
    return outs[n:], outs[:n]


def _sum_blocks(own, land, ids):
    nseg, _, rows, cols = land.shape

    def body(ids_ref, own_ref, land_ref, o_ref):
        me = ids_ref[1]
        total = None
        for j in range(N_DEV):
            slot = jnp.maximum(jnp.bitwise_xor(me, j) - 1, 0)
            term = jnp.where(me == j, own_ref[...], land_ref[slot]).astype(F32)
            total = term if total is None else total + term
        o_ref[...] = total

    return pl.pallas_call(
        body, name="sum_blocks",
        grid_spec=pltpu.PrefetchScalarGridSpec(
            num_scalar_prefetch=1, grid=(nseg,),
            in_specs=[pl.BlockSpec((None, None, rows, cols), lambda s, ids: (s, ids[0], 0, 0)),
                      pl.BlockSpec((None, N_PEERS, rows, cols), lambda s, ids: (s, 0, 0, 0))],
            out_specs=pl.BlockSpec((None, rows, cols), lambda s, ids: (s, 0, 0))),
        out_shape=jax.ShapeDtypeStruct((nseg, rows, cols), F32),
        compiler_params=_params("parallel"),
    )(ids, own, land)


def _adamw(w, g, m, v):
    shape = w.shape
    cols = shape[-1]
    rows = w.size // cols
    tr = rows
    for cand in (512, 256, 128, 64, 32, 16, 8):
        if rows % cand == 0:
            tr = cand
            break
    c1 = 1.0 / (1.0 - ADAM_B1 ** ADAM_STEP)
    c2 = 1.0 / (1.0 - ADAM_B2 ** ADAM_STEP)

    def body(w_ref, g_ref, m_ref, v_ref, d_ref, nm_ref, nv_ref):
        g = g_ref[...]
        m = ADAM_B1 * m_ref[...] + (1.0 - ADAM_B1) * g
        v = ADAM_B2 * v_ref[...] + (1.0 - ADAM_B2) * (g * g)
        nm_ref[...] = m
        nv_ref[...] = v
        d_ref[...] = -ADAM_LR * ((m * c1) / (jnp.sqrt(v * c2) + ADAM_EPS) + ADAM_WD * w_ref[...])

    spec = pl.BlockSpec((tr, cols), lambda i: (i, 0))
    outs = pl.pallas_call(
        body, name="adamw", grid=(rows // tr,),
        in_specs=[spec] * 4, out_specs=[spec] * 3,
        out_shape=[jax.ShapeDtypeStruct((rows, cols), F32)] * 3,
        compiler_params=_params("parallel"),
    )(*[a.reshape(rows, cols) for a in (w, g, m, v)])
    return [o.reshape(shape) for o in outs]


SMALL = ("ln_mix_pre", "attn_sinks", "gm_ln_g", "gm_ln_b", "gm_ws", "gm_bs", "g_attn_out", "g_gm_out",
         "ln_mix_post", "ln_ffn_pre", "ln_ffn_post", "ln_ple_gate")
WEIGHTS = ("ln_mix_pre", "w_in", "attn_sinks", "gm_ln_g", "gm_ln_b", "gm_ws", "gm_bs", "g_attn_out", "g_gm_out",
           "w_out", "ln_mix_post", "ln_ffn_pre", "w_ffn_gate", "w_ffn_up", "w_ffn_down", "ln_ffn_post", "w_ple",
           "ln_ple_gate", "w_ple_gate")


def _pack_shards(w, l):
    sa = jnp.stack([w["w_ffn_gate"][l].T, w["w_ffn_up"][l].T, w["w_ffn_down"][l]])[:, None]
    sb = jnp.stack([w["w_out"][l], w["w_ple_gate"][l]])[:, None]
    return [w["w_in"][l].T[None, None].astype(BF16), sb.astype(BF16), w["w_ple"][l].T[None, None].astype(BF16),
            sa.astype(BF16)]


def kernel(x, p, ln_mix_pre, w_in, attn_sinks, gm_ln_g, gm_ln_b, gm_ws, gm_bs, g_attn_out, g_gm_out, w_out, ln_mix_post, ln_ffn_pre, w_ffn_gate, w_ffn_up, w_ffn_down, ln_ffn_post, w_ple, ln_ple_gate, w_ple_gate, loss_target, m_ln_mix_pre, m_w_in, m_attn_sinks, m_gm_ln_g, m_gm_ln_b, m_gm_ws, m_gm_bs, m_g_attn_out, m_g_gm_out, m_w_out, m_ln_mix_post, m_ln_ffn_pre, m_w_ffn_gate, m_w_ffn_up, m_w_ffn_down, m_ln_ffn_post, m_w_ple, m_ln_ple_gate, m_w_ple_gate, v_ln_mix_pre, v_w_in, v_attn_sinks, v_gm_ln_g, v_gm_ln_b, v_gm_ws, v_gm_bs, v_g_attn_out, v_g_gm_out, v_w_out, v_ln_mix_post, v_ln_ffn_pre, v_w_ffn_gate, v_w_ffn_up, v_w_ffn_down, v_ln_ffn_post, v_w_ple, v_ln_ple_gate, v_w_ple_gate):
    given = dict(locals())
    w = {n: given[n] for n in WEIGHTS}
    sp = {n: w[n] for n in SMALL}
    kinds = ("c", "b", "p", "a")

    me, _ = _peers()
    shards = [s for l in range(DEPTH) for s in _pack_shards(w, l)]
    lands = [lax.dynamic_update_slice(lax.empty((s.shape[0], N_DEV) + s.shape[2:], BF16), s, (0, me, 0, 0))
             for s in shards]
    gather, token = _send_start("gather_start", shards, lands, [GATHER] * len(shards))
    layer_weights = [{} for _ in range(DEPTH)]

    def weights_of(l):
        def get(kind, after):
            have = layer_weights[l]
            if kind not in have:
                if l < 2:
                    group = {"c": ("c",), "b": ("b", "p"), "p": ("b", "p"), "a": ("a",)}[kind]
                    after = token if (l == 0 and kind == "c") else after
                else:
                    group = kinds
                got, _ = _send_wait(f"gather_wait_{l}{group[0]}", gather, [4 * l + kinds.index(k) for k in group], after)
                for k, g in zip(group, got):
                    have[k] = g.reshape(-1, g.shape[-1])
            return have[kind]
        return get

    h = x[0]
    p3 = p.reshape(DEPTH, -1, PLE_DIM)
    saved = []
    for l in range(DEPTH):
        h, s = _layer_fwd(h, p3, sp, l, weights_of(l), loss_target[0] if l == DEPTH - 1 else None)
        saved.append(s)
    dh, sq = h

    reduces = []
    after = token
    view = lambda g, rows: g.reshape(-1, N_DEV, rows, g.shape[-1])
    pack16 = lambda s: s.astype(BF16)[None, None]
    landing = lambda a: lax.empty((a.shape[0], N_PEERS) + a.shape[2:], BF16)

    def send(name, bufs, modes):
        return _send_start(name, bufs, [landing(a) for a in bufs], modes)

    for l in reversed(range(DEPTH)):
        lw = layer_weights[l]
        ride = l > 0
        carry, ga, gp, gb = _layer_bwd_upper(dh, saved[l], p3, sp, l, lw["a"], lw["b"], lw["p"], after, ride)
        early = [view(gp, ROWS_B)] + ([] if ride else [view(ga, ROWS_A)])
        sent1, after = send(f"reduce_start_{l}a", early, [SCATTER] * len(early))
        carry, gb, ga, gating = _layer_bwd_middle(carry, saved[l], sp, l, lw["b"], gb, ga, after, ride)
        sent2, after = send(f"reduce_start_{l}b", [view(gb, ROWS_B), pack16(gating)], [SCATTER, SPREAD])
        dh, gc, ga, rest = _layer_bwd_lower(carry, saved[l], sp, l, lw["c"], ga, after, ride)
        late = [view(gc, ROWS_C), pack16(rest)] + ([view(ga, ROWS_A)] if ride else [])
        sent3, after = send(f"reduce_start_{l}c", late, [SCATTER, SPREAD] + [SCATTER] * ride)
        reduces.append((l, sent1, sent2, sent3))

    mine = jnp.stack([me, me]).astype(jnp.int32)
    whole = jnp.stack([jnp.zeros_like(me), me]).astype(jnp.int32)
    sums = {k: [None] * DEPTH for k in ("a", "p", "b", "c", "gating", "rest")}
    last = {}
    for l, sent1, sent2, sent3 in reduces:
        l1, g1 = _send_wait(f"reduce_wait_{l}a", sent1, list(range(len(sent1["srcs"]))), dh)
        (lb, lg), (gb, gg) = _send_wait(f"reduce_wait_{l}b", sent2, [0, 1], dh)
        sums["p"][l] = _sum_blocks(g1[0], l1[0], mine)
        sums["b"][l], sums["gating"][l] = _sum_blocks(gb, lb, mine), _sum_blocks(gg, lg, whole)[0]
        if l > 0:
            (lc, lr, la), (gc, gr, ga) = _send_wait(f"reduce_wait_{l}c", sent3, [0, 1, 2], dh)
            sums["c"][l], sums["rest"][l] = _sum_blocks(gc, lc, mine), _sum_blocks(gr, lr, whole)[0]
            sums["a"][l] = _sum_blocks(ga, la, mine)
        else:
            sums["a"][l] = _sum_blocks(g1[1], l1[1], mine)
            last = sent3
    grad_x = dh
    loss = lax.psum(sq[0, 0] * (0.5 / D_MODEL), AXES)
    grads, delta, new_m, new_v = {}, {}, {}, {}

    def update(names):
        for n in names:
            delta[n], new_m[n], new_v[n] = _adamw(w[n], grads[n], given["m_" + n], given["v_" + n])

    stack = lambda f, xs: jnp.stack([f(x) for x in xs])
    grads.update({
        "w_ffn_gate": stack(lambda r: r[0].T, sums["a"]), "w_ffn_up": stack(lambda r: r[1].T, sums["a"]),
        "w_ffn_down": stack(lambda r: r[2], sums["a"]), "w_ple": stack(lambda r: r[0].T, sums["p"]),
        "w_out": stack(lambda r: r[0], sums["b"]), "w_ple_gate": stack(lambda r: r[1], sums["b"])})
    grads.update(_unpack_gating(jnp.stack(sums["gating"])))
    early = tuple(grads)
    update(early)
    (lc, lr), (gc, gr) = _send_wait("reduce_wait_0c", last, [0, 1], delta["w_ffn_down"])
    sums["c"][0], sums["rest"][0] = _sum_blocks(gc, lc, mine), _sum_blocks(gr, lr, whole)[0]
    grads["w_in"] = stack(lambda r: r[0].T, sums["c"])
    grads.update(_unpack_rest(jnp.stack(sums["rest"])))
    update([n for n in grads if n not in early])
    return (loss, grad_x[None], *[grads[n] for n in WEIGHTS], *[delta[n] for n in WEIGHTS],
            *[new_m[n] for n in WEIGHTS], *[new_v[n] for n in WEIGHTS])
```

```python
import math

import jax
import jax.numpy as jnp
from jax import lax
from jax.experimental import pallas as pl
from jax.experimental.pallas import tpu as pltpu

F32 = jnp.float32
BF16 = jnp.bfloat16
MESH = pl.DeviceIdType.MESH
AXES = ("x", "y", "c")

D_MODEL = 1024
DEPTH = 4
N_DEV = 8
HEAD_DIM = 64
ATTN_W = 512
KV_W = 128
GM_W = 512
D_IN = 1792
D_FF = 2816
PLE_DIM = 256
BLK = 128
FF_CHUNK = 256
WGRAD_TOKENS = 1024
NORM_EPS = 1e-6
NEG_BIG = -1e30
ATTN_SCALE = HEAD_DIM ** -0.5

ADAM_LR = 0.001
ADAM_B1 = 0.9
ADAM_B2 = 0.999
ADAM_EPS = 1e-08
ADAM_WD = 0.01
ADAM_STEP = 10

ROWS_A = D_FF // N_DEV
ROWS_B = D_MODEL // N_DEV
ROWS_C = D_IN // N_DEV
GATING_ROWS = 144
REST_ROWS = 56

VMEM_LIMIT = 56 * 2 ** 20


def _params(*sem):
    return pltpu.CompilerParams(dimension_semantics=sem, vmem_limit_bytes=VMEM_LIMIT)


def _dot(a, b):
    return jnp.dot(a, b, preferred_element_type=F32)


def _dot_nt(a, b):
    return lax.dot_general(a, b, (((1,), (1,)), ((), ())), preferred_element_type=F32)


def _dot_tn(a, b):
    return lax.dot_general(a, b, (((0,), (0,)), ((), ())), preferred_element_type=F32)


def _rms_fwd(x, g):
    r = lax.rsqrt(jnp.mean(x * x, axis=-1, keepdims=True) + NORM_EPS)
    return x * r * g


def _rms_bwd(x, g, dy):
    r = lax.rsqrt(jnp.mean(x * x, axis=-1, keepdims=True) + NORM_EPS)
    xh = x * r
    dg = jnp.sum(dy * xh, axis=0, keepdims=True)
    dxh = dy * g
    dx = r * (dxh - xh * jnp.mean(dxh * xh, axis=-1, keepdims=True))
    return dx, dg


_GELU_C = math.sqrt(2.0 / math.pi)


def _gelu(x):
    t = jnp.tanh(_GELU_C * (x + 0.044715 * (x * x * x)))
    return 0.5 * x * (1.0 + t)


def _gelu_grad(x):
    x2 = x * x
    t = jnp.tanh(_GELU_C * (x + 0.044715 * (x2 * x)))
    return 0.5 * (1.0 + t) + 0.5 * x * (1.0 - t * t) * (_GELU_C * (1.0 + 3.0 * 0.044715 * x2))


def _sigmoid(x):
    return 1.0 / (1.0 + jnp.exp(-x))


def _row_spec(tm, n):
    return pl.BlockSpec((tm, n), lambda i: (i, 0))


def _layer_row_spec(tm, n, l):
    return pl.BlockSpec((None, tm, n), lambda i: (l, i, 0))


def _vec_spec(n):
    return pl.BlockSpec((1, n), lambda i: (0, 0))


def _seg_spec(rows, cols, seg):
    return pl.BlockSpec((N_DEV * rows, cols), lambda i: (seg, 0), pipeline_mode=pl.Buffered(1))


def _zero_at(first, *refs):
    @pl.when(first)
    def _():
        for r in refs:
            r[...] = jnp.zeros(r.shape, r.dtype)


def _tile(t, want):
    return min(t, want)


def _in_proj(h, g, wc, layer):
    t = h.shape[0]
    tm = _tile(t, 512)

    def body(h_ref, g_ref, w_ref, a_ref, q_ref, kv_ref, zu_ref, zv_ref):
        a = _rms_fwd(h_ref[...], g_ref[...]).astype(BF16)
        a_ref[...] = a
        q_ref[...] = _dot_nt(a, w_ref[0:512, :]).astype(BF16)
        kv_ref[...] = _dot_nt(a, w_ref[512:768, :]).astype(BF16)
        zu_ref[...] = _dot_nt(a, w_ref[768:1280, :])
        zv_ref[...] = _dot_nt(a, w_ref[1280:1792, :])

    return pl.pallas_call(
        body, name="in_proj", grid=(t // tm,),
        in_specs=[_row_spec(tm, D_MODEL), _vec_spec(D_MODEL), _seg_spec(ROWS_C, D_MODEL, layer)],
        out_specs=[_row_spec(tm, D_MODEL), _row_spec(tm, ATTN_W), _row_spec(tm, 2 * KV_W),
                   _row_spec(tm, GM_W), _row_spec(tm, GM_W)],
        out_shape=[jax.ShapeDtypeStruct((t, D_MODEL), BF16), jax.ShapeDtypeStruct((t, ATTN_W), BF16),
                   jax.ShapeDtypeStruct((t, 2 * KV_W), BF16), jax.ShapeDtypeStruct((t, GM_W), F32),
                   jax.ShapeDtypeStruct((t, GM_W), F32)],
        compiler_params=_params("parallel"),
    )(h, g, wc)


def _head_variants(x, low):
    xr = pltpu.roll(x, 64, axis=1)
    zero = jnp.zeros_like(x)
    return {
        (0, 0): jnp.where(low, x, zero).astype(BF16),
        (0, 1): jnp.where(low, zero, xr).astype(BF16),
        (1, 0): jnp.where(low, xr, zero).astype(BF16),
        (1, 1): jnp.where(low, zero, x).astype(BF16),
    }


def _attn_masks(i):
    row = lax.broadcasted_iota(jnp.int32, (BLK, BLK), 0)
    lane = lax.broadcasted_iota(jnp.int32, (BLK, BLK), 1)
    vcur = row >= lane
    dist = jnp.where(vcur, row - lane, row - lane + BLK).astype(F32)
    valid = jnp.logical_or(vcur, i > 0)
    return lane < 64, vcur, dist, valid


def _head_key(h):
    return (h // 4, h % 2)


def _stack_kv(prev, cur, g):
    return jnp.concatenate([prev[(g, 0)], cur[(g, 0)], prev[(g, 1)], cur[(g, 1)]], axis=0)


def _split_cols(p, vcur):
    return [jnp.where(vcur, 0.0, p).astype(BF16), jnp.where(vcur, p, 0.0).astype(BF16)]


def _attn_scores(q_ref, rows, stacked, vcur):
    out = []
    for col in range(4):
        big = _dot_nt(q_ref[rows, col * 128:(col + 1) * 128], stacked[col // 2])
        for half in range(2):
            out.append(jnp.where(vcur, big[:, half * 256 + 128:half * 256 + 256], big[:, half * 256:half * 256 + 128]))
    return out


def _attn_scores_by_head(q_ref, rows, kc, kp, vcur):
    out = []
    for h in range(8):
        qh = q_ref[rows, (h // 2) * 128:(h // 2 + 1) * 128]
        out.append(jnp.where(vcur, _dot_nt(qh, kc[_head_key(h)]), _dot_nt(qh, kp[_head_key(h)])))
    return out


def _attn_probs(s, h, sink, dist, valid):
    s = s * ATTN_SCALE - (2.0 ** -(h + 1)) * dist
    if valid is not None:
        s = jnp.where(valid, s, NEG_BIG)
    m = jnp.maximum(jnp.max(s, axis=1, keepdims=True), sink)
    e = jnp.exp(s - m)
    es = jnp.exp(sink - m)
    inv = 1.0 / (jnp.sum(e, axis=1, keepdims=True) + es)
    return e * inv, es * inv


def _kv_prev_spec(blocks):
    return pl.BlockSpec((BLK, 2 * KV_W), lambda i: (jnp.maximum(i * blocks - 1, 0), 0))


def _kv_variants(kv_ref, rows, low):
    return (_head_variants(kv_ref[rows, 0:128].astype(F32), low), _head_variants(kv_ref[rows, 128:256].astype(F32), low))


def _gm_forward_block(zu, zv, lng, lnb, w_ref, bsx, low):
    gu = _gelu(zu)
    gv = _gelu(zv)
    mu = jnp.mean(gv, axis=-1, keepdims=True)
    xc = gv - mu
    rstd = lax.rsqrt(jnp.mean(xc * xc, axis=-1, keepdims=True) + NORM_EPS)
    xn = xc * rstd
    ln = xn * lng + lnb
    mixed = []
    for col in range(4):
        lc = ln[:, col * 128:(col + 1) * 128]
        lo = jnp.where(low, lc, 0.0).astype(BF16)
        hi = jnp.where(low, 0.0, lc).astype(BF16)
        mixed.append(_dot(w_ref[2 * col], lo) + _dot(w_ref[2 * col + 1], hi) + bsx[:, col * 128:(col + 1) * 128])
    return gu, ln, xn, rstd, mixed


def _mix_fwd(q, kv, sinks, zu, zv, lng, lnb, wtril, bsx, h, ga, gg, gpost, wb, layer):
    t = q.shape[0]
    tq = _tile(t, 512)
    blocks = tq // BLK

    def body(sink_ref, q_ref, kvc_ref, kvp_ref, zu_ref, zv_ref, g_ref, b_ref, w_ref, bs_ref, h_ref, ga_ref, gg_ref,
             gp_ref, wo_ref, attn_ref, gm_ref, heads_ref, mix_ref, h1_ref):
        low, vcur, dist, valid = _attn_masks(pl.program_id(0))
        kp, vp = _kv_variants(kvp_ref, slice(None), low)

        def project(rows, mix):
            mix_ref[rows, :] = mix.astype(BF16)
            h1_ref[rows, :] = h_ref[rows, :] + _rms_fwd(mix, gp_ref[...])

        pending = None
        for b in range(blocks):
            rows = slice(b * BLK, (b + 1) * BLK)
            kc, vc = _kv_variants(kvc_ref, rows, low)
            ks = [_stack_kv(kp, kc, g) for g in range(2)]
            vs = [_stack_kv(vp, vc, g) for g in range(2)]
            scores = _attn_scores(q_ref, rows, ks, vcur)
            if pending is not None:
                project(pending[0], _dot(pending[1], wo_ref[...]))
            gu, _, _, _, mixed = _gm_forward_block(zu_ref[rows, :], zv_ref[rows, :], g_ref[...], b_ref[...], w_ref,
                                                   bs_ref[...], low)
            probs = [_attn_probs(scores[h], h, sink_ref[h], dist, valid if b == 0 else None)[0] for h in range(8)]
            attn_cols, gm_cols = [], []
            for col in range(4):
                gm_cols.append((gu[:, col * 128:(col + 1) * 128] * mixed[col]).astype(BF16))
                p_col = jnp.concatenate(_split_cols(probs[2 * col], vcur) + _split_cols(probs[2 * col + 1], vcur), axis=1)
                attn_cols.append(_dot(p_col, vs[col // 2]).astype(BF16))
            attn = jnp.concatenate(attn_cols, axis=1)
            gm = jnp.concatenate(gm_cols, axis=1)
            attn_ref[rows, :] = attn
            gm_ref[rows, :] = gm
            heads = jnp.concatenate([_rms_fwd(attn.astype(F32), ga_ref[...]).astype(BF16),
                                     _rms_fwd(gm.astype(F32), gg_ref[...]).astype(BF16)], axis=1)
            heads_ref[rows, :] = heads
            pending = (rows, heads)
            kp, vp = kc, vc
        project(pending[0], _dot(pending[1], wo_ref[...]))

    wide = _row_spec(tq, GM_W)
    row = _row_spec(tq, D_MODEL)
    return pl.pallas_call(
        body, name="mix_fwd", grid=(t // tq,),
        in_specs=[pl.BlockSpec(memory_space=pltpu.SMEM), _row_spec(tq, ATTN_W), _row_spec(tq, 2 * KV_W),
                  _kv_prev_spec(blocks), wide, wide, _vec_spec(GM_W), _vec_spec(GM_W),
                  pl.BlockSpec((8, BLK, BLK), lambda i: (0, 0, 0)), pl.BlockSpec((BLK, GM_W), lambda i: (0, 0)),
                  row, _vec_spec(ATTN_W), _vec_spec(GM_W), _vec_spec(D_MODEL), _seg_spec(ROWS_B, D_MODEL, 2 * layer)],
        out_specs=[_row_spec(tq, ATTN_W), wide, row, row, row],
        out_shape=[jax.ShapeDtypeStruct((t, ATTN_W), BF16), jax.ShapeDtypeStruct((t, GM_W), BF16),
                   jax.ShapeDtypeStruct((t, D_MODEL), BF16), jax.ShapeDtypeStruct((t, D_MODEL), BF16),
                   jax.ShapeDtypeStruct((t, D_MODEL), F32)],
        compiler_params=_params("parallel"),
    )(sinks, q, kv, kv, zu, zv, lng, lnb, wtril, bsx, h, ga, gg, gpost, wb)


def _ffn_fwd(h1, p, p_layer, gpre, gpost, gple, wa, wb, wp, layer, target=None):
    t = h1.shape[0]
    tm = _tile(t, 256)

    def body(h_ref, p_ref, gpre_ref, gpost_ref, gple_ref, wg_ref, wu_ref, wd_ref, wpg_ref, wpl_ref, *rest):
        if target is None:
            f_ref, gp_ref, up_ref, act_ref, fo_ref, h2_ref, hn_ref, gate_ref, h3_ref = rest
        else:
            t_ref, f_ref, gp_ref, up_ref, act_ref, fo_ref, h2_ref, hn_ref, gate_ref, dy_ref, l_ref = rest
            _zero_at(pl.program_id(0) == 0, l_ref)
        h = h_ref[...]
        pe = _dot_nt(p_ref[...].astype(BF16), wpl_ref[...])
        f = _rms_fwd(h, gpre_ref[...]).astype(BF16)
        f_ref[...] = f
        chunks = [slice(j * FF_CHUNK, (j + 1) * FF_CHUNK) for j in range(D_FF // FF_CHUNK)]
        fo = None
        gp, up = _dot_nt(f, wg_ref[chunks[0], :]), _dot_nt(f, wu_ref[chunks[0], :])
        for j, cols in enumerate(chunks):
            if j + 1 < len(chunks):
                gp_next, up_next = _dot_nt(f, wg_ref[chunks[j + 1], :]), _dot_nt(f, wu_ref[chunks[j + 1], :])
            act = (gp * _sigmoid(gp) * up).astype(BF16)
            gp_ref[:, cols] = gp.astype(BF16)
            up_ref[:, cols] = up.astype(BF16)
            act_ref[:, cols] = act
            part = _dot(act, wd_ref[cols, :])
            fo = part if fo is None else fo + part
            if j + 1 < len(chunks):
                gp, up = gp_next, up_next
        fo_ref[...] = fo
        h2 = h + _rms_fwd(fo, gpost_ref[...])
        h2_ref[...] = h2
        hn = _rms_fwd(h2, gple_ref[...]).astype(BF16)
        hn_ref[...] = hn
        gate = _sigmoid(_dot(hn, wpg_ref[...]))
        gate_ref[...] = gate.astype(BF16)
        h3 = h2 + pe * gate
        if target is None:
            h3_ref[...] = h3
        else:
            e = h3 - t_ref[...]
            dy_ref[...] = (e * (1.0 / D_MODEL)).astype(BF16)
            s = jnp.sum(jnp.sum(e * e, axis=1, keepdims=True), axis=0, keepdims=True)
            l_ref[...] += jnp.broadcast_to(s, (1, 128))

    wide = _row_spec(tm, D_FF)
    row = _row_spec(tm, D_MODEL)
    vec = _vec_spec(D_MODEL)
    last = target is not None
    return pl.pallas_call(
        body, name="ffn_loss" if last else "ffn_fwd", grid=(t // tm,),
        in_specs=[row, _layer_row_spec(tm, PLE_DIM, p_layer), vec, vec, vec, _seg_spec(ROWS_A, D_MODEL, 3 * layer),
                  _seg_spec(ROWS_A, D_MODEL, 3 * layer + 1), _seg_spec(ROWS_A, D_MODEL, 3 * layer + 2),
                  _seg_spec(ROWS_B, D_MODEL, 2 * layer + 1), _seg_spec(ROWS_B, PLE_DIM, layer)] + [row] * last,
        out_specs=[row, wide, wide, wide, row, row, row, row, row] + [_vec_spec(128)] * last,
        out_shape=[jax.ShapeDtypeStruct((t, D_MODEL), BF16)] + [jax.ShapeDtypeStruct((t, D_FF), BF16)] * 3
        + [jax.ShapeDtypeStruct((t, D_MODEL), F32)] * 2 + [jax.ShapeDtypeStruct((t, D_MODEL), BF16)] * 2
        + [jax.ShapeDtypeStruct((t, D_MODEL), BF16 if last else F32)] + [jax.ShapeDtypeStruct((1, 128), F32)] * last,
        compiler_params=_params("arbitrary" if last else "parallel"),
    )(h1, p, gpre, gpost, gple, wa, wa, wa, wb, wp, *([target] if last else []))


def _ffn_bwd(dh3, h2, gate, p, p_layer, fo, gp, up, h1, gple, gpost, gpre, wa, wb, wp, layer, after):
    t = dh3.shape[0]
    tm = _tile(t, 256)

    def body(d3_ref, h2_ref, gate_ref, p_ref, fo_ref, gp_ref, up_ref, h_ref, gple_ref, gpost_ref, gpre_ref,
             wg_ref, wu_ref, wd_ref, wpg_ref, wpl_ref, after_ref,
             dgl_ref, dpe_ref, dfo_ref, dgp_ref, dup_ref, dh1_ref, dgple_ref, dgpost_ref, dgpre_ref):
        _zero_at(pl.program_id(0) == 0, dgple_ref, dgpost_ref, dgpre_ref)
        d3 = d3_ref[...].astype(F32)
        gate = gate_ref[...].astype(F32)
        pe = _dot_nt(p_ref[...].astype(BF16), wpl_ref[...])
        dpe_ref[...] = (d3 * gate).astype(BF16)
        dgl = (d3 * pe * gate * (1.0 - gate)).astype(BF16)
        dgl_ref[...] = dgl
        dx2, dgple = _rms_bwd(h2_ref[...], gple_ref[...], _dot_nt(dgl, wpg_ref[...]))
        dgple_ref[...] += dgple
        d = d3 + dx2
        dfo, dgpost = _rms_bwd(fo_ref[...], gpost_ref[...], d)
        dfo = dfo.astype(BF16)
        dfo_ref[...] = dfo
        dgpost_ref[...] += dgpost
        chunks = [slice(j * FF_CHUNK, (j + 1) * FF_CHUNK) for j in range(D_FF // FF_CHUNK)]
        df = None
        dact = _dot_nt(dfo, wd_ref[chunks[0], :])
        for j, cols in enumerate(chunks):
            if j + 1 < len(chunks):
                dact_next = _dot_nt(dfo, wd_ref[chunks[j + 1], :])
            gp = gp_ref[:, cols].astype(F32)
            sg = _sigmoid(gp)
            dgp = (dact * up_ref[:, cols].astype(F32) * (sg * (1.0 + gp * (1.0 - sg)))).astype(BF16)
            dup = (dact * (gp * sg)).astype(BF16)
            dgp_ref[:, cols] = dgp
            dup_ref[:, cols] = dup
            part = _dot(dgp, wg_ref[cols, :]) + _dot(dup, wu_ref[cols, :])
            df = part if df is None else df + part
            if j + 1 < len(chunks):
                dact = dact_next
        dx, dgpre = _rms_bwd(h_ref[...], gpre_ref[...], df)
        dh1_ref[...] = (d + dx).astype(BF16)
        dgpre_ref[...] += dgpre

    wide = _row_spec(tm, D_FF)
    row = _row_spec(tm, D_MODEL)
    vec = _vec_spec(D_MODEL)
    narrow = jax.ShapeDtypeStruct((t, D_MODEL), BF16)
    return pl.pallas_call(
        body, name="ffn_bwd", grid=(t // tm,),
        in_specs=[row, row, row, _layer_row_spec(tm, PLE_DIM, p_layer), row, wide, wide, row, vec, vec, vec,
                  _seg_spec(ROWS_A, D_MODEL, 3 * layer), _seg_spec(ROWS_A, D_MODEL, 3 * layer + 1),
                  _seg_spec(ROWS_A, D_MODEL, 3 * layer + 2), _seg_spec(ROWS_B, D_MODEL, 2 * layer + 1),
                  _seg_spec(ROWS_B, PLE_DIM, layer), pl.BlockSpec(memory_space=pl.ANY)],
        out_specs=[row, row, row, wide, wide, row, vec, vec, vec],
        out_shape=[narrow, narrow, narrow, jax.ShapeDtypeStruct((t, D_FF), BF16), jax.ShapeDtypeStruct((t, D_FF), BF16),
                   narrow] + [jax.ShapeDtypeStruct((1, D_MODEL), F32)] * 3,
        compiler_params=_params("arbitrary"),
    )(dh3, h2, gate, p, fo, gp, up, h1, gple, gpost, gpre, wa, wa, wa, wb, wp, after)


def _out_proj_bwd(dh1, mix, attn, gm, gpost, ga, gg, wb, layer, after):
    t = dh1.shape[0]
    tm = _tile(t, 512)

    def body(d_ref, mix_ref, a_ref, m_ref, gp_ref, ga_ref, gg_ref, w_ref, after_ref,
             dmix_ref, da_ref, dm_ref, dgp_ref, dga_ref, dgg_ref):
        _zero_at(pl.program_id(0) == 0, dgp_ref, dga_ref, dgg_ref)
        dmix, dgp = _rms_bwd(mix_ref[...].astype(F32), gp_ref[...], d_ref[...].astype(F32))
        dmix = dmix.astype(BF16)
        dmix_ref[...] = dmix
        da, dga = _rms_bwd(a_ref[...].astype(F32), ga_ref[...], _dot_nt(dmix, w_ref[0:512, :]))
        dm, dgg = _rms_bwd(m_ref[...].astype(F32), gg_ref[...], _dot_nt(dmix, w_ref[512:1024, :]))
        da_ref[...] = da.astype(BF16)
        dm_ref[...] = dm
        dgp_ref[...] += dgp
        dga_ref[...] += dga
        dgg_ref[...] += dgg

    return pl.pallas_call(
        body, name="out_proj_bwd", grid=(t // tm,),
        in_specs=[_row_spec(tm, D_MODEL), _row_spec(tm, D_MODEL), _row_spec(tm, ATTN_W), _row_spec(tm, GM_W),
                  _vec_spec(D_MODEL), _vec_spec(ATTN_W), _vec_spec(GM_W), _seg_spec(ROWS_B, D_MODEL, 2 * layer),
                  pl.BlockSpec(memory_space=pl.ANY)],
        out_specs=[_row_spec(tm, D_MODEL), _row_spec(tm, ATTN_W), _row_spec(tm, GM_W),
                   _vec_spec(D_MODEL), _vec_spec(ATTN_W), _vec_spec(GM_W)],
        out_shape=[jax.ShapeDtypeStruct((t, D_MODEL), BF16), jax.ShapeDtypeStruct((t, ATTN_W), BF16),
                   jax.ShapeDtypeStruct((t, GM_W), F32), jax.ShapeDtypeStruct((1, D_MODEL), F32),
                   jax.ShapeDtypeStruct((1, ATTN_W), F32), jax.ShapeDtypeStruct((1, GM_W), F32)],
        compiler_params=_params("arbitrary"),
    )(dh1, mix, attn, gm, gpost, ga, gg, wb, after)


def _split3(x):
    hi = x.astype(BF16)
    r1 = x - hi.astype(F32)
    mid = r1.astype(BF16)
    lo = (r1 - mid.astype(F32)).astype(BF16)
    return hi, mid, lo


def _rider_pieces(rider, blocks):
    rm = rider[0].shape[1]
    per = -(-rm // (blocks * 256)) * 256
    return [slice(k * per, min((k + 1) * per, rm)) for k in range(blocks) if k * per < rm]


def _gm_bwd(dgm, zu, zv, lng, lnb, wtril, bsx, rider=None):
    t = zu.shape[0]
    tm = _tile(t, 512)
    nb = t // tm
    if rider is not None:
        ra, rb, rbuf, rseg = rider
        rm, rn = ra.shape[1], rb.shape[1]
        pieces = _rider_pieces(rider, tm // BLK)

    def body(d_ref, zu_ref, zv_ref, g_ref, b_ref, w_ref, bs_ref, *rest):
        if rider is None:
            dzu_ref, dzv_ref, dw_ref, dbs_ref, dlg_ref, dlb_ref, dbsx_ref = rest
        else:
            ra_ref, rb_ref, rbuf_ref, dzu_ref, dzv_ref, dw_ref, dbs_ref, dlg_ref, dlb_ref, ro_ref, dbsx_ref, acc_ref = rest
        i = pl.program_id(0)
        _zero_at(i == 0, dw_ref, dlg_ref, dlb_ref, dbsx_ref)
        if rider is not None:
            _zero_at(i == 0, acc_ref)
            rb16 = rb_ref[...].astype(BF16)
        row = lax.broadcasted_iota(jnp.int32, (BLK, BLK), 0)
        lane = lax.broadcasted_iota(jnp.int32, (BLK, BLK), 1)
        low = lane < 64
        tril = row >= lane
        lng = g_ref[...]
        for b in range(tm // BLK):
            rows = slice(b * BLK, (b + 1) * BLK)
            if rider is not None and b < len(pieces):
                acc_ref[pieces[b], :] += _dot_tn(ra_ref[:, pieces[b]], rb16)
            zu = zu_ref[rows, :]
            zv = zv_ref[rows, :]
            gu, ln, xn, rstd, mixed = _gm_forward_block(zu, zv, lng, b_ref[...], w_ref, bs_ref[...], low)
            dgm = d_ref[rows, :]
            dgu_cols, dmx_cols, dln_cols = [], [], []
            for col in range(4):
                sl = slice(col * 128, (col + 1) * 128)
                dg = dgm[:, sl]
                dgu_cols.append(dg * mixed[col])
                dmx = dg * gu[:, sl]
                dmx_cols.append(dmx)
                lc = ln[:, sl]
                halves = (jnp.where(low, lc, 0.0).astype(BF16), jnp.where(low, 0.0, lc).astype(BF16))
                dmx16 = dmx.astype(BF16)
                dmx_half = (jnp.where(low, dmx, 0.0).astype(BF16), jnp.where(low, 0.0, dmx).astype(BF16))
                dln = None
                for half in range(2):
                    hd = 2 * col + half
                    dw_ref[hd] += jnp.where(tril, _dot_nt(dmx16, halves[half]), 0.0)
                    part = _dot_tn(w_ref[hd], dmx_half[half])
                    dln = part if dln is None else dln + part
                dln_cols.append(dln)
            dgu = jnp.concatenate(dgu_cols, axis=1)
            dmx = jnp.concatenate(dmx_cols, axis=1)
            dln = jnp.concatenate(dln_cols, axis=1)
            dzu_ref[rows, :] = (dgu * _gelu_grad(zu)).astype(BF16)
            dbsx_ref[...] += dmx
            dlg_ref[...] += jnp.sum(dln * xn, axis=0, keepdims=True)
            dlb_ref[...] += jnp.sum(dln, axis=0, keepdims=True)
            dxn = dln * lng
            dgv = rstd * (dxn - jnp.mean(dxn, axis=-1, keepdims=True) - xn * jnp.mean(dxn * xn, axis=-1, keepdims=True))
            dzv_ref[rows, :] = (dgv * _gelu_grad(zv)).astype(BF16)

        @pl.when(i == nb - 1)
        def _():
            r = lax.broadcasted_iota(jnp.int32, (GM_W, BLK), 0)
            c = lax.broadcasted_iota(jnp.int32, (GM_W, BLK), 1)
            e = jnp.where(jnp.logical_and(r >= c * 64, r < c * 64 + 64), 1.0, 0.0).astype(BF16)
            hi, mid, lo = _split3(dbsx_ref[...])
            dbs_ref[...] = _dot(hi, e) + _dot(mid, e) + _dot(lo, e)
            if rider is not None:
                ro_ref[...] = acc_ref[...].astype(ro_ref.dtype)

    vec = _vec_spec(GM_W)
    in_specs = [_row_spec(tm, GM_W)] * 3 + [vec, vec, pl.BlockSpec((8, BLK, BLK), lambda i: (0, 0, 0)),
                                            pl.BlockSpec((BLK, GM_W), lambda i: (0, 0))]
    out_specs = [_row_spec(tm, GM_W), _row_spec(tm, GM_W), pl.BlockSpec((8, BLK, BLK), lambda i: (0, 0, 0)),
                 pl.BlockSpec((BLK, BLK), lambda i: (0, 0)), vec, vec]
    out_shape = [jax.ShapeDtypeStruct((t, GM_W), BF16), jax.ShapeDtypeStruct((t, GM_W), BF16),
                 jax.ShapeDtypeStruct((8, BLK, BLK), F32), jax.ShapeDtypeStruct((BLK, BLK), F32),
                 jax.ShapeDtypeStruct((1, GM_W), F32), jax.ShapeDtypeStruct((1, GM_W), F32)]
    scratch = [pltpu.VMEM((BLK, GM_W), F32)]
    operands = [dgm, zu, zv, lng, lnb, wtril, bsx]
    extra = {}
    if rider is not None:
        in_specs += [_row_spec(tm, rm), _row_spec(tm, rn), pl.BlockSpec(memory_space=pl.ANY)]
        out_specs.append(pl.BlockSpec((rm, rn), lambda i: (rseg, 0)))
        out_shape.append(jax.ShapeDtypeStruct(rbuf.shape, rbuf.dtype))
        scratch.append(pltpu.VMEM((rm, rn), F32))
        operands += [ra, rb, rbuf]
        extra = dict(input_output_aliases={9: 6})
    return pl.pallas_call(
        body, name="gm_bwd" if rider is None else "gm_bwd_rider", grid=(nb,),
        in_specs=in_specs, out_specs=out_specs, out_shape=out_shape, scratch_shapes=scratch,
        compiler_params=_params("arbitrary"), **extra,
    )(*operands)


def _attn_bwd(q, kv, do, sinks, after, rider=None):
    t = q.shape[0]
    tq = _tile(t, 512)
    blocks = tq // BLK
    steps = t // tq
    if rider is not None:
        ra, rb, rbuf, rseg = rider
        rm, rn = ra.shape[1], rb.shape[1]
        per = -(-rm // (blocks * 256)) * 256
        pieces = [slice(k * per, min((k + 1) * per, rm)) for k in range(blocks)]

    def body(sink_ref, q_ref, kvc_ref, kvp_ref, do_ref, after_ref, *rest):
        if rider is None:
            dq_ref, dkv_ref, dkf_ref, ds_ref = rest
        else:
            ra_ref, rb_ref, rbuf_ref, dq_ref, dkv_ref, dkf_ref, ds_ref, ro_ref, acc_ref = rest
        i = pl.program_id(0)
        _zero_at(i == 0, ds_ref)
        if rider is not None:
            _zero_at(i == 0, acc_ref)
            rb16 = rb_ref[...].astype(BF16)
        low, vcur, dist, valid = _attn_masks(i)
        head_row = lax.broadcasted_iota(jnp.int32, (8, 128), 0)
        dsink_tile = jnp.zeros((8, 128), F32)
        kp, vp = _kv_variants(kvp_ref, slice(None), low)
        own = None
        for b in range(blocks):
            rows = slice(b * BLK, (b + 1) * BLK)
            kc, vc = _kv_variants(kvc_ref, rows, low)
            scores = _attn_scores_by_head(q_ref, rows, kc, kp, vcur)
            dprobs = _attn_scores_by_head(do_ref, rows, vc, vp, vcur)
            if rider is not None and pieces[b].start < rm:
                acc_ref[pieces[b], :] += _dot_tn(ra_ref[:, pieces[b]], rb16)
            parts = []
            for h in range(8):
                p, ps = _attn_probs(scores[h], h, sink_ref[h], dist, valid if b == 0 else None)
                delta = jnp.sum(p * dprobs[h], axis=1, keepdims=True)
                ds = p * (dprobs[h] - delta) * ATTN_SCALE
                dsink = jnp.sum(-ps * delta, axis=0, keepdims=True)
                dsink_tile = jnp.where(head_row == h, dsink_tile + dsink, dsink_tile)
                parts.append(_split_cols(ds, vcur) + _split_cols(p, vcur))
            acc = {}

            def add(name, key, val):
                acc[(name, key)] = val if (name, key) not in acc else acc[(name, key)] + val

            for col in range(4):
                qh = q_ref[rows, col * 128:(col + 1) * 128]
                doh = do_ref[rows, col * 128:(col + 1) * 128]
                dq = None
                for half in range(2):
                    key = _head_key(2 * col + half)
                    dsp, dsc, pp, pc = parts[2 * col + half]
                    part = _dot(dsc, kc[key]) + _dot(dsp, kp[key])
                    dq = part if dq is None else dq + part
                    add("kc", key, _dot_tn(dsc, qh))
                    add("kp", key, _dot_tn(dsp, qh))
                    add("vc", key, _dot_tn(pc, doh))
                    add("vp", key, _dot_tn(pp, doh))
                dq_ref[rows, col * 128:(col + 1) * 128] = dq.astype(BF16)

            def place(name):
                head0 = acc[(name, (0, 0))] + pltpu.roll(acc[(name, (0, 1))], 64, axis=1)
                head1 = pltpu.roll(acc[(name, (1, 0))], 64, axis=1) + acc[(name, (1, 1))]
                return jnp.where(low, head0, head1)

            before = (place("kp"), place("vp"))
            if b == 0:
                dkf_ref[:, 0:128], dkf_ref[:, 128:256] = before
            else:
                last = slice((b - 1) * BLK, b * BLK)
                dkv_ref[last, 0:128] = own[0] + before[0]
                dkv_ref[last, 128:256] = own[1] + before[1]
            own = (place("kc"), place("vc"))
            kp, vp = kc, vc
        final = slice((blocks - 1) * BLK, blocks * BLK)
        dkv_ref[final, 0:128], dkv_ref[final, 128:256] = own
        ds_ref[...] += dsink_tile
        if rider is not None:
            @pl.when(i == steps - 1)
            def _():
                ro_ref[...] = acc_ref[...].astype(ro_ref.dtype)

    row_q = _row_spec(tq, ATTN_W)
    row_kv = _row_spec(tq, 2 * KV_W)
    in_specs = [pl.BlockSpec(memory_space=pltpu.SMEM), row_q, row_kv, _kv_prev_spec(blocks), row_q,
                pl.BlockSpec(memory_space=pl.ANY)]
    out_specs = [row_q, row_kv, _row_spec(BLK, 2 * KV_W), pl.BlockSpec((8, 128), lambda i: (0, 0))]
    out_shape = [jax.ShapeDtypeStruct((t, ATTN_W), BF16), jax.ShapeDtypeStruct((t, 2 * KV_W), F32),
                 jax.ShapeDtypeStruct((t // tq * BLK, 2 * KV_W), F32), jax.ShapeDtypeStruct((8, 128), F32)]
    operands = [sinks, q, kv, kv, do, after]
    extra = {}
    if rider is not None:
        in_specs += [_row_spec(tq, rm), _row_spec(tq, rn), pl.BlockSpec(memory_space=pl.ANY)]
        out_specs.append(pl.BlockSpec((rm, rn), lambda i: (rseg, 0)))
        out_shape.append(jax.ShapeDtypeStruct(rbuf.shape, rbuf.dtype))
        operands += [ra, rb, rbuf]
        extra = dict(scratch_shapes=[pltpu.VMEM((rm, rn), F32)], input_output_aliases={8: 4})
    return pl.pallas_call(
        body, name="attn_bwd" if rider is None else "attn_bwd_rider", grid=(steps,),
        in_specs=in_specs, out_specs=out_specs, out_shape=out_shape,
        compiler_params=_params("arbitrary"), **extra,
    )(*operands)


def _in_proj_bwd(dq, dkv, dkf, dzu, dzv, h, dres, g, wc, layer, dh_dtype):
    t = h.shape[0]
    tm = _tile(t, 512)
    steps = t // tm

    def body(dq_ref, dkv_ref, dkn_ref, dzu_ref, dzv_ref, h_ref, d_ref, g_ref, w_ref, dz_ref, dh_ref, dg_ref):
        i = pl.program_id(0)
        _zero_at(i == 0, dg_ref)
        dq = dq_ref[...]
        tail = dkv_ref[tm - BLK:tm, :] + jnp.where(i < steps - 1, dkn_ref[...], 0.0)
        dkv = tail if tm == BLK else jnp.concatenate([dkv_ref[0:tm - BLK, :], tail], axis=0)
        dkv = dkv.astype(BF16)
        dzu = dzu_ref[...]
        dzv = dzv_ref[...]
        dz_ref[:, 0:512] = dq
        dz_ref[:, 512:768] = dkv
        dz_ref[:, 768:1280] = dzu
        dz_ref[:, 1280:1792] = dzv
        da = (_dot(dq, w_ref[0:512, :]) + _dot(dkv, w_ref[512:768, :]) + _dot(dzu, w_ref[768:1280, :])
              + _dot(dzv, w_ref[1280:1792, :]))
        dx, dg = _rms_bwd(h_ref[...], g_ref[...], da)
        dh_ref[...] = (d_ref[...].astype(F32) + dx).astype(dh_dtype)
        dg_ref[...] += dg

    return pl.pallas_call(
        body, name="in_proj_bwd", grid=(t // tm,),
        in_specs=[_row_spec(tm, ATTN_W), _row_spec(tm, 2 * KV_W),
                  pl.BlockSpec((BLK, 2 * KV_W), lambda i: (jnp.minimum(i + 1, steps - 1), 0)), _row_spec(tm, GM_W),
                  _row_spec(tm, GM_W), _row_spec(tm, D_MODEL), _row_spec(tm, D_MODEL), _vec_spec(D_MODEL),
                  _seg_spec(ROWS_C, D_MODEL, layer)],
        out_specs=[_row_spec(tm, D_IN), _row_spec(tm, D_MODEL), _vec_spec(D_MODEL)],
        out_shape=[jax.ShapeDtypeStruct((t, D_IN), BF16), jax.ShapeDtypeStruct((t, D_MODEL), dh_dtype),
                   jax.ShapeDtypeStruct((1, D_MODEL), F32)],
        compiler_params=_params("arbitrary"),
    )(dq, dkv, dkf, dzu, dzv, h, dres, g, wc)


def _weight_grad(a, b, buf, seg, b_layer=None):
    t, m = a.shape
    n = b.shape[-1]
    assert buf.shape[0] % m == 0 and buf.shape[1] == n
    tm = _tile(t, WGRAD_TOKENS)
    steps = t // tm
    half = m // 2

    def body(a_ref, b_ref, buf_ref, o_ref, acc_ref):
        i = pl.program_id(0)
        _zero_at(i == 0, acc_ref)
        b16 = b_ref[...].astype(BF16)
        for rows in (slice(0, half), slice(half, m)):
            acc_ref[rows, :] += _dot_tn(a_ref[:, rows], b16)

        @pl.when(i == steps - 1)
        def _():
            o_ref[...] = acc_ref[...].astype(o_ref.dtype)

    return pl.pallas_call(
        body, name="weight_grad", grid=(steps,),
        in_specs=[_row_spec(tm, m), _row_spec(tm, n) if b_layer is None else _layer_row_spec(tm, n, b_layer),
                  pl.BlockSpec(memory_space=pl.ANY)],
        out_specs=pl.BlockSpec((m, n), lambda i: (seg, 0)),
        out_shape=jax.ShapeDtypeStruct(buf.shape, buf.dtype),
        scratch_shapes=[pltpu.VMEM((m, n), F32)],
        input_output_aliases={2: 0},
        compiler_params=_params("arbitrary"),
    )(a, b, buf)


def _rows8(rows):
    return [jnp.pad(r, ((0, 7), (0, 0))) for r in rows]


def _small_pack_gating(d):
    rows = [jnp.concatenate([d["gm_ln_g"], d["gm_ln_b"]], axis=1), d["gm_bs"].reshape(1, 1024)]
    return jnp.concatenate(_rows8(rows) + [d["gm_ws"].reshape(128, 1024)], axis=0)


def _small_pack_rest(d):
    rows = [d["ln_mix_pre"], d["ln_mix_post"], d["ln_ffn_pre"], d["ln_ffn_post"], d["ln_ple_gate"],
            jnp.concatenate([d["g_attn_out"], d["g_gm_out"]], axis=1),
            jnp.pad(d["attn_sinks"].reshape(1, 8), ((0, 0), (0, 1016)))]
    return jnp.concatenate(_rows8(rows), axis=0)


def _unpack_gating(g):
    return {"gm_ln_g": g[:, 0, :512], "gm_ln_b": g[:, 0, 512:], "gm_bs": g[:, 8].reshape(DEPTH, 8, 128),
            "gm_ws": g[:, 16:GATING_ROWS].reshape(DEPTH, 8, 128, 128)}


def _unpack_rest(s):
    return {"ln_mix_pre": s[:, 0], "ln_mix_post": s[:, 8], "ln_ffn_pre": s[:, 16], "ln_ffn_post": s[:, 24],
            "ln_ple_gate": s[:, 32], "g_attn_out": s[:, 40, :512], "g_gm_out": s[:, 40, 512:], "attn_sinks": s[:, 48, :8]}


def _row(v):
    return v.reshape(1, -1)


def _layer_fwd(h, p, sp, l, weights, target=None):
    tril = jnp.tril(jnp.ones((BLK, BLK), bool))
    wtril = jnp.where(tril[None], sp["gm_ws"][l], 0.0).astype(BF16)
    bsx = jnp.repeat(sp["gm_bs"][l].T, HEAD_DIM, axis=1)
    a, q, kv, zu, zv = _in_proj(h, _row(sp["ln_mix_pre"][l]), weights("c", h), 0)
    wb = weights("b", zu)
    attn, gm, heads, mix, h1 = _mix_fwd(
        q, kv, sp["attn_sinks"][l], zu, zv, _row(sp["gm_ln_g"][l]), _row(sp["gm_ln_b"][l]), wtril, bsx, h,
        _row(sp["g_attn_out"][l]), _row(sp["g_gm_out"][l]), _row(sp["ln_mix_post"][l]), wb, 0)
    wa = weights("a", h1)
    f, gpre, up, act, fo, h2, hn, gate, *out = _ffn_fwd(
        h1, p, l, _row(sp["ln_ffn_pre"][l]), _row(sp["ln_ffn_post"][l]), _row(sp["ln_ple_gate"][l]), wa, wb,
        weights("p", zu), 0, target)
    saved = dict(h=h, a=a, q=q, kv=kv, zu=zu, zv=zv, attn=attn, gm=gm, heads=heads, mix=mix, h1=h1, f=f,
                 gpre=gpre, up=up, act=act, fo=fo, h2=h2, hn=hn, gate=gate, wtril=wtril, bsx=bsx)
    return (out[0] if target is None else tuple(out)), saved


def _layer_bwd_upper(dh, s, p, sp, l, wa, wb, wp, after, ride):
    d = {}
    dgl, dpe, dfo, dgp, dup, dh1, d["ln_ple_gate"], d["ln_ffn_post"], d["ln_ffn_pre"] = _ffn_bwd(
        dh, s["h2"], s["gate"], p, l, s["fo"], s["gpre"], s["up"], s["h1"], _row(sp["ln_ple_gate"][l]),
        _row(sp["ln_ffn_post"][l]), _row(sp["ln_ffn_pre"][l]), wa, wb, wp, 0, after)
    gb = _weight_grad(s["hn"], dgl, lax.empty((2 * D_MODEL, D_MODEL), BF16), 1)
    gp = _weight_grad(dpe, p, lax.empty((D_MODEL, PLE_DIM), BF16), 0, b_layer=l)
    ga = _weight_grad(s["act"], dfo, lax.empty((3 * D_FF, D_MODEL), BF16), 2)
    if not ride:
        ga = _weight_grad(dgp, s["f"], ga, 0)
        ga = _weight_grad(dup, s["f"], ga, 1)
    return (dh1, d, dgp, dup), ga, gp, gb


def _layer_bwd_middle(carry, s, sp, l, wb, gb, ga, after, ride):
    dh1, d, dgp, dup = carry
    dmix, dattn, dgm, d["ln_mix_post"], d["g_attn_out"], d["g_gm_out"] = _out_proj_bwd(
        dh1, s["mix"], s["attn"], s["gm"], _row(sp["ln_mix_post"][l]), _row(sp["g_attn_out"][l]),
        _row(sp["g_gm_out"][l]), wb, 0, after)
    gb = _weight_grad(s["heads"], dmix, gb, 0)
    dzu, dzv, d["gm_ws"], dbs, d["gm_ln_g"], d["gm_ln_b"], *rode = _gm_bwd(
        dgm, s["zu"], s["zv"], _row(sp["gm_ln_g"][l]), _row(sp["gm_ln_b"][l]), s["wtril"], s["bsx"],
        (dup, s["f"], ga, 1) if ride else None)
    d["gm_bs"] = dbs[:, :8].T
    return (dh1, dattn, dzu, dzv, d, dgp), gb, (rode[0] if ride else ga), _small_pack_gating(d)


def _layer_bwd_lower(carry, s, sp, l, wc, ga, after, ride):
    dh1, dattn, dzu, dzv, d, dgp = carry
    dq, dkv, dkf, dsink, *rode = _attn_bwd(s["q"], s["kv"], dattn, sp["attn_sinks"][l], after,
                                            (dgp, s["f"], ga, 0) if ride else None)
    ga = rode[0] if ride else ga
    d["attn_sinks"] = dsink[:, 0]
    dz, dh, d["ln_mix_pre"] = _in_proj_bwd(dq, dkv, dkf, dzu, dzv, s["h"], dh1, _row(sp["ln_mix_pre"][l]), wc, 0,
                                           F32 if l == 0 else BF16)
    gc = _weight_grad(dz, s["a"], lax.empty((D_IN, D_MODEL), BF16), 0)
    return dh, gc, ga, _small_pack_rest(d)


ANY = pl.BlockSpec(memory_space=pl.ANY)


HBM = pl.BlockSpec(memory_space=pltpu.HBM)
SEM = pl.BlockSpec(memory_space=pltpu.SEMAPHORE)
N_PEERS = N_DEV - 1


def _peers():
    x, y, c = lax.axis_index("x"), lax.axis_index("y"), lax.axis_index("c")
    peers = []
    for r in range(1, N_DEV):
        px = 1 - x if r & 4 else x
        py = 1 - y if r & 2 else y
        pc = 1 - c if r & 1 else c
        peers.append(((px, py, pc), 4 * px + 2 * py + pc))
    return 4 * x + 2 * y + c, peers


GATHER, SCATTER, SPREAD = "gather", "scatter", "spread"


def _peer_copy(src, land, send_sems, recv_sems, r, me, peer, peer_slot, mode):
    return pltpu.make_async_remote_copy(
        src_ref=src.at[:, pl.ds(peer_slot, 1)] if mode == SCATTER else src,
        dst_ref=land.at[:, pl.ds(me, 1)] if mode == GATHER else land.at[:, pl.ds(r - 1, 1)],
        send_sem=send_sems.at[r - 1], recv_sem=recv_sems.at[r - 1], device_id=peer, device_id_type=MESH)


def _peer_arrival(src, land, send_sems, recv_sems, r, me, peer, peer_slot, mode):
    return pltpu.make_async_remote_copy(
        src_ref=src.at[:, pl.ds(me, 1)] if mode == SCATTER else src,
        dst_ref=land.at[:, pl.ds(peer_slot, 1)] if mode == GATHER else land.at[:, pl.ds(r - 1, 1)],
        send_sem=send_sems.at[r - 1], recv_sem=recv_sems.at[r - 1], device_id=peer, device_id_type=MESH)


def _send_start(name, srcs, lands, modes):
    n = len(srcs)

    def body(*refs):
        src_refs, land_refs = refs[:n], refs[n:2 * n]
        outs = refs[2 * n:]
        send_sems, recv_sems, token = outs[2 * n:3 * n], outs[3 * n:4 * n], outs[4 * n]
        me, peers = _peers()
        for k in range(n):
            for r, (peer, slot) in enumerate(peers, 1):
                _peer_copy(src_refs[k], land_refs[k], send_sems[k], recv_sems[k], r, me, peer, slot, modes[k]).start()
        token[...] = jnp.zeros_like(token)

    hbm = lambda a: pltpu.HBM(a.shape, a.dtype)
    sems = [pltpu.SemaphoreType.DMA((N_PEERS,))] * (2 * n)
    outs = pl.pallas_call(
        body, name=name, in_specs=[HBM] * (2 * n),
        out_specs=[HBM] * (2 * n) + [SEM] * (2 * n) + [pl.BlockSpec(memory_space=pltpu.VMEM)],
        out_shape=[hbm(a) for a in srcs] + [hbm(a) for a in lands] + sems + [jax.ShapeDtypeStruct((8, 128), F32)],
        input_output_aliases={k: k for k in range(2 * n)},
        compiler_params=pltpu.CompilerParams(has_side_effects=pltpu.SideEffectType.DATAFLOW_SIDE_EFFECTING),
    )(*[pltpu.with_memory_space_constraint(a, pltpu.HBM) for a in list(srcs) + list(lands)])
    return dict(srcs=outs[:n], lands=outs[n:2 * n], send=outs[2 * n:3 * n], recv=outs[3 * n:4 * n],
                modes=list(modes)), outs[4 * n]


def _send_wait(name, sent, ks, after):
    n = len(ks)
    srcs = [sent["srcs"][k] for k in ks]
    lands = [sent["lands"][k] for k in ks]
    modes = [sent["modes"][k] for k in ks]

    def body(*refs):
        src_refs, land_refs = refs[:n], refs[n:2 * n]
        send_sems, recv_sems = refs[2 * n:3 * n], refs[3 * n:4 * n]
        me, peers = _peers()
        for k in range(n):
            for r, (peer, slot) in enumerate(peers, 1):
                args = (src_refs[k], land_refs[k], send_sems[k], recv_sems[k], r, me, peer, slot, modes[k])
                _peer_copy(*args).wait_send()
                _peer_arrival(*args).wait_recv()

    hbm = lambda a: pltpu.HBM(a.shape, a.dtype)
    outs = pl.pallas_call(
        body, name=name, in_specs=[HBM] * (2 * n) + [SEM] * (2 * n) + [ANY],
        out_specs=[HBM] * (2 * n), out_shape=[hbm(a) for a in srcs] + [hbm(a) for a in lands],
        input_output_aliases={k: k for k in range(2 * n)},
        compiler_params=pltpu.CompilerParams(has_side_effects=pltpu.SideEffectType.DATAFLOW_SIDE_EFFECTING),
    )(*srcs, *lands, *[sent["send"][k] for k in ks], *[sent["recv"][k] for k in ks], after)
    return outs[n:], outs[:n]


def _sum_blocks(own, land, ids):
    nseg, _, rows, cols = land.shape

    def body(ids_ref, own_ref, land_ref, o_ref):
        me = ids_ref[1]
        total = None
        for j in range(N_DEV):
            slot = jnp.maximum(jnp.bitwise_xor(me, j) - 1, 0)
            term = jnp.where(me == j, own_ref[...], land_ref[slot]).astype(F32)
            total = term if total is None else total + term
        o_ref[...] = total

    return pl.pallas_call(
        body, name="sum_blocks",
        grid_spec=pltpu.PrefetchScalarGridSpec(
            num_scalar_prefetch=1, grid=(nseg,),
            in_specs=[pl.BlockSpec((None, None, rows, cols), lambda s, ids: (s, ids[0], 0, 0)),
                      pl.BlockSpec((None, N_PEERS, rows, cols), lambda s, ids: (s, 0, 0, 0))],
            out_specs=pl.BlockSpec((None, rows, cols), lambda s, ids: (s, 0, 0))),
        out_shape=jax.ShapeDtypeStruct((nseg, rows, cols), F32),
        compiler_params=_params("parallel"),
    )(ids, own, land)


def _adamw(w, g, m, v):
    shape = w.shape
    cols = shape[-1]
    rows = w.size // cols
    tr = rows
    for cand in (512, 256, 128, 64, 32, 16, 8):
        if rows % cand == 0:
            tr = cand
            break
    c1 = 1.0 / (1.0 - ADAM_B1 ** ADAM_STEP)
    c2 = 1.0 / (1.0 - ADAM_B2 ** ADAM_STEP)

    def body(w_ref, g_ref, m_ref, v_ref, d_ref, nm_ref, nv_ref):
        g = g_ref[...]
        m = ADAM_B1 * m_ref[...] + (1.0 - ADAM_B1) * g
        v = ADAM_B2 * v_ref[...] + (1.0 - ADAM_B2) * (g * g)
        nm_ref[...] = m
        nv_ref[...] = v
        d_ref[...] = -ADAM_LR * ((m * c1) / (jnp.sqrt(v * c2) + ADAM_EPS) + ADAM_WD * w_ref[...])

    spec = pl.BlockSpec((tr, cols), lambda i: (i, 0))
    outs = pl.pallas_call(
        body, name="adamw", grid=(rows // tr,),
        in_specs=[spec] * 4, out_specs=[spec] * 3,
        out_shape=[jax.ShapeDtypeStruct((rows, cols), F32)] * 3,
        compiler_params=_params("parallel"),
    )(*[a.reshape(rows, cols) for a in (w, g, m, v)])
    return [o.reshape(shape) for o in outs]


SMALL = ("ln_mix_pre", "attn_sinks", "gm_ln_g", "gm_ln_b", "gm_ws", "gm_bs", "g_attn_out", "g_gm_out",
         "ln_mix_post", "ln_ffn_pre", "ln_ffn_post", "ln_ple_gate")
WEIGHTS = ("ln_mix_pre", "w_in", "attn_sinks", "gm_ln_g", "gm_ln_b", "gm_ws", "gm_bs", "g_attn_out", "g_gm_out",
           "w_out", "ln_mix_post", "ln_ffn_pre", "w_ffn_gate", "w_ffn_up", "w_ffn_down", "ln_ffn_post", "w_ple",
           "ln_ple_gate", "w_ple_gate")


def _pack_shards(w, l):
    sa = jnp.stack([w["w_ffn_gate"][l].T, w["w_ffn_up"][l].T, w["w_ffn_down"][l]])[:, None]
    sb = jnp.stack([w["w_out"][l], w["w_ple_gate"][l]])[:, None]
    return [w["w_in"][l].T[None, None].astype(BF16), sb.astype(BF16), w["w_ple"][l].T[None, None].astype(BF16),
            sa.astype(BF16)]


def kernel(x, p, ln_mix_pre, w_in, attn_sinks, gm_ln_g, gm_ln_b, gm_ws, gm_bs, g_attn_out, g_gm_out, w_out, ln_mix_post, ln_ffn_pre, w_ffn_gate, w_ffn_up, w_ffn_down, ln_ffn_post, w_ple, ln_ple_gate, w_ple_gate, loss_target, m_ln_mix_pre, m_w_in, m_attn_sinks, m_gm_ln_g, m_gm_ln_b, m_gm_ws, m_gm_bs, m_g_attn_out, m_g_gm_out, m_w_out, m_ln_mix_post, m_ln_ffn_pre, m_w_ffn_gate, m_w_ffn_up, m_w_ffn_down, m_ln_ffn_post, m_w_ple, m_ln_ple_gate, m_w_ple_gate, v_ln_mix_pre, v_w_in, v_attn_sinks, v_gm_ln_g, v_gm_ln_b, v_gm_ws, v_gm_bs, v_g_attn_out, v_g_gm_out, v_w_out, v_ln_mix_post, v_ln_ffn_pre, v_w_ffn_gate, v_w_ffn_up, v_w_ffn_down, v_ln_ffn_post, v_w_ple, v_ln_ple_gate, v_w_ple_gate):
    given = dict(locals())
    w = {n: given[n] for n in WEIGHTS}
    sp = {n: w[n] for n in SMALL}
    kinds = ("c", "b", "p", "a")

    me, _ = _peers()
    shards = [s for l in range(DEPTH) for s in _pack_shards(w, l)]
    lands = [lax.dynamic_update_slice(lax.empty((s.shape[0], N_DEV) + s.shape[2:], BF16), s, (0, me, 0, 0))
             for s in shards]
    gather, token = _send_start("gather_start", shards, lands, [GATHER] * len(shards))
    layer_weights = [{} for _ in range(DEPTH)]

    def weights_of(l):
        def get(kind, after):
            have = layer_weights[l]
            if kind not in have:
                if l < 2:
                    group = {"c": ("c",), "b": ("b", "p"), "p": ("b", "p"), "a": ("a",)}[kind]
                    after = token if (l == 0 and kind == "c") else after
                else:
                    group = kinds
                got, _ = _send_wait(f"gather_wait_{l}{group[0]}", gather, [4 * l + kinds.index(k) for k in group], after)
                for k, g in zip(group, got):
                    have[k] = g.reshape(-1, g.shape[-1])
            return have[kind]
        return get

    h = x[0]
    p3 = p.reshape(DEPTH, -1, PLE_DIM)
    saved = []
    for l in range(DEPTH):
        h, s = _layer_fwd(h, p3, sp, l, weights_of(l), loss_target[0] if l == DEPTH - 1 else None)
        saved.append(s)
    dh, sq = h

    reduces = []
    after = token
    view = lambda g, rows: g.reshape(-1, N_DEV, rows, g.shape[-1])
    pack16 = lambda s: s.astype(BF16)[None, None]
    landing = lambda a: lax.empty((a.shape[0], N_PEERS) + a.shape[2:], BF16)

    def send(name, bufs, modes):
        return _send_start(name, bufs, [landing(a) for a in bufs], modes)

    for l in reversed(range(DEPTH)):
        lw = layer_weights[l]
        ride = l > 0
        carry, ga, gp, gb = _layer_bwd_upper(dh, saved[l], p3, sp, l, lw["a"], lw["b"], lw["p"], after, ride)
        early = [view(gp, ROWS_B)] + ([] if ride else [view(ga, ROWS_A)])
        sent1, after = send(f"reduce_start_{l}a", early, [SCATTER] * len(early))
        carry, gb, ga, gating = _layer_bwd_middle(carry, saved[l], sp, l, lw["b"], gb, ga, after, ride)
        sent2, after = send(f"reduce_start_{l}b", [view(gb, ROWS_B), pack16(gating)], [SCATTER, SPREAD])
        dh, gc, ga, rest = _layer_bwd_lower(carry, saved[l], sp, l, lw["c"], ga, after, ride)
        late = [view(gc, ROWS_C), pack16(rest)] + ([view(ga, ROWS_A)] if ride else [])
        sent3, after = send(f"reduce_start_{l}c", late, [SCATTER, SPREAD] + [SCATTER] * ride)
        reduces.append((l, sent1, sent2, sent3))

    mine = jnp.stack([me, me]).astype(jnp.int32)
    whole = jnp.stack([jnp.zeros_like(me), me]).astype(jnp.int32)
    sums = {k: [None] * DEPTH for k in ("a", "p", "b", "c", "gating", "rest")}
    last = {}
    for l, sent1, sent2, sent3 in reduces:
        l1, g1 = _send_wait(f"reduce_wait_{l}a", sent1, list(range(len(sent1["srcs"]))), dh)
        (lb, lg), (gb, gg) = _send_wait(f"reduce_wait_{l}b", sent2, [0, 1], dh)
        sums["p"][l] = _sum_blocks(g1[0], l1[0], mine)
        sums["b"][l], sums["gating"][l] = _sum_blocks(gb, lb, mine), _sum_blocks(gg, lg, whole)[0]
        if l > 0:
            (lc, lr, la), (gc, gr, ga) = _send_wait(f"reduce_wait_{l}c", sent3, [0, 1, 2], dh)
            sums["c"][l], sums["rest"][l] = _sum_blocks(gc, lc, mine), _sum_blocks(gr, lr, whole)[0]
            sums["a"][l] = _sum_blocks(ga, la, mine)
        else:
            sums["a"][l] = _sum_blocks(g1[1], l1[1], mine)
            last = sent3
    grad_x = dh
    loss = lax.psum(sq[0, 0] * (0.5 / D_MODEL), AXES)
    grads, delta, new_m, new_v = {}, {}, {}, {}

    def update(names):
        for n in names:
            delta[n], new_m[n], new_v[n] = _adamw(w[n], grads[n], given["m_" + n], given["v_" + n])

    stack = lambda f, xs: jnp.stack([f(x) for x in xs])
    grads.update({
        "w_ffn_gate": stack(lambda r: r[0].T, sums["a"]), "w_ffn_up": stack(lambda r: r[1].T, sums["a"]),
        "w_ffn_down": stack(lambda r: r[2], sums["a"]), "w_ple": stack(lambda r: r[0].T, sums["p"]),
        "w_out": stack(lambda r: r[0], sums["b"]), "w_ple_gate": stack(lambda r: r[1], sums["b"])})
    grads.update(_unpack_gating(jnp.stack(sums["gating"])))
    early = tuple(grads)
    update(early)
    (lc, lr), (gc, gr) = _send_wait("reduce_wait_0c", last, [0, 1], delta["w_ffn_down"])
    sums["c"][0], sums["rest"][0] = _sum_blocks(gc, lc, mine), _sum_blocks(gr, lr, whole)[0]
    grads["w_in"] = stack(lambda r: r[0].T, sums["c"])
    grads.update(_unpack_rest(jnp.stack(sums["rest"])))
    update([n for n in grads if n not in early])
    return (loss, grad_x[None], *[grads[n] for n in WEIGHTS], *[delta[n] for n in WEIGHTS],
            *[new_m[n] for n in WEIGHTS], *[new_v[n] for n in WEIGHTS])
```

```python
import math

import jax
import jax.numpy as jnp
from jax import lax
from jax.experimental import pallas as pl
from jax.experimental.pallas import tpu as pltpu

F32 = jnp.float32
BF16 = jnp.bfloat16
MESH = pl.DeviceIdType.MESH
AXES = ("x", "y", "c")

D_MODEL = 1024
DEPTH = 4
N_DEV = 8
HEAD_DIM = 64
ATTN_W = 512
KV_W = 128
GM_W = 512
D_IN = 1792
D_FF = 2816
PLE_DIM = 256
BLK = 128
FF_CHUNK = 256
WGRAD_TOKENS = 1024
NORM_EPS = 1e-6
NEG_BIG = -1e30
ATTN_SCALE = HEAD_DIM ** -0.5

ADAM_LR = 0.001
ADAM_B1 = 0.9
ADAM_B2 = 0.999
ADAM_EPS = 1e-08
ADAM_WD = 0.01
ADAM_STEP = 10

ROWS_A = D_FF // N_DEV
ROWS_B = D_MODEL // N_DEV
ROWS_C = D_IN // N_DEV
GATING_ROWS = 144
REST_ROWS = 56

VMEM_LIMIT = 56 * 2 ** 20


def _params(*sem):
    return pltpu.CompilerParams(dimension_semantics=sem, vmem_limit_bytes=VMEM_LIMIT)


def _dot(a, b):
    return jnp.dot(a, b, preferred_element_type=F32)


def _dot_nt(a, b):
    return lax.dot_general(a, b, (((1,), (1,)), ((), ())), preferred_element_type=F32)


def _dot_tn(a, b):
    return lax.dot_general(a, b, (((0,), (0,)), ((), ())), preferred_element_type=F32)


def _rms_fwd(x, g):
    r = lax.rsqrt(jnp.mean(x * x, axis=-1, keepdims=True) + NORM_EPS)
    return x * r * g


def _rms_bwd(x, g, dy):
    r = lax.rsqrt(jnp.mean(x * x, axis=-1, keepdims=True) + NORM_EPS)
    xh = x * r
    dg = jnp.sum(dy * xh, axis=0, keepdims=True)
    dxh = dy * g
    dx = r * (dxh - xh * jnp.mean(dxh * xh, axis=-1, keepdims=True))
    return dx, dg


_GELU_C = math.sqrt(2.0 / math.pi)


def _gelu(x):
    t = jnp.tanh(_GELU_C * (x + 0.044715 * (x * x * x)))
    return 0.5 * x * (1.0 + t)


def _gelu_grad(x):
    x2 = x * x
    t = jnp.tanh(_GELU_C * (x + 0.044715 * (x2 * x)))
    return 0.5 * (1.0 + t) + 0.5 * x * (1.0 - t * t) * (_GELU_C * (1.0 + 3.0 * 0.044715 * x2))


def _sigmoid(x):
    return 1.0 / (1.0 + jnp.exp(-x))


def _row_spec(tm, n):
    return pl.BlockSpec((tm, n), lambda i: (i, 0))


def _layer_row_spec(tm, n, l):
    return pl.BlockSpec((None, tm, n), lambda i: (l, i, 0))


def _vec_spec(n):
    return pl.BlockSpec((1, n), lambda i: (0, 0))


def _seg_spec(rows, cols, seg):
    return pl.BlockSpec((N_DEV * rows, cols), lambda i: (seg, 0), pipeline_mode=pl.Buffered(1))


def _zero_at(first, *refs):
    @pl.when(first)
    def _():
        for r in refs:
            r[...] = jnp.zeros(r.shape, r.dtype)


def _tile(t, want):
    return min(t, want)


def _in_proj(h, g, wc, layer):
    t = h.shape[0]
    tm = _tile(t, 512)

    def body(h_ref, g_ref, w_ref, a_ref, q_ref, kv_ref, zu_ref, zv_ref):
        a = _rms_fwd(h_ref[...], g_ref[...]).astype(BF16)
        a_ref[...] = a
        q_ref[...] = _dot_nt(a, w_ref[0:512, :]).astype(BF16)
        kv_ref[...] = _dot_nt(a, w_ref[512:768, :]).astype(BF16)
        zu_ref[...] = _dot_nt(a, w_ref[768:1280, :])
        zv_ref[...] = _dot_nt(a, w_ref[1280:1792, :])

    return pl.pallas_call(
        body, name="in_proj", grid=(t // tm,),
        in_specs=[_row_spec(tm, D_MODEL), _vec_spec(D_MODEL), _seg_spec(ROWS_C, D_MODEL, layer)],
        out_specs=[_row_spec(tm, D_MODEL), _row_spec(tm, ATTN_W), _row_spec(tm, 2 * KV_W),
                   _row_spec(tm, GM_W), _row_spec(tm, GM_W)],
        out_shape=[jax.ShapeDtypeStruct((t, D_MODEL), BF16), jax.ShapeDtypeStruct((t, ATTN_W), BF16),
                   jax.ShapeDtypeStruct((t, 2 * KV_W), BF16), jax.ShapeDtypeStruct((t, GM_W), F32),
                   jax.ShapeDtypeStruct((t, GM_W), F32)],
        compiler_params=_params("parallel"),
    )(h, g, wc)


def _head_variants(x, low):
    xr = pltpu.roll(x, 64, axis=1)
    zero = jnp.zeros_like(x)
    return {
        (0, 0): jnp.where(low, x, zero).astype(BF16),
        (0, 1): jnp.where(low, zero, xr).astype(BF16),
        (1, 0): jnp.where(low, xr, zero).astype(BF16),
        (1, 1): jnp.where(low, zero, x).astype(BF16),
    }


def _attn_masks(i):
    row = lax.broadcasted_iota(jnp.int32, (BLK, BLK), 0)
    lane = lax.broadcasted_iota(jnp.int32, (BLK, BLK), 1)
    vcur = row >= lane
    dist = jnp.where(vcur, row - lane, row - lane + BLK).astype(F32)
    valid = jnp.logical_or(vcur, i > 0)
    return lane < 64, vcur, dist, valid


def _head_key(h):
    return (h // 4, h % 2)


def _stack_kv(prev, cur, g):
    return jnp.concatenate([prev[(g, 0)], cur[(g, 0)], prev[(g, 1)], cur[(g, 1)]], axis=0)


def _split_cols(p, vcur):
    return [jnp.where(vcur, 0.0, p).astype(BF16), jnp.where(vcur, p, 0.0).astype(BF16)]


def _attn_scores(q_ref, rows, stacked, vcur):
    out = []
    for col in range(4):
        big = _dot_nt(q_ref[rows, col * 128:(col + 1) * 128], stacked[col // 2])
        for half in range(2):
            out.append(jnp.where(vcur, big[:, half * 256 + 128:half * 256 + 256], big[:, half * 256:half * 256 + 128]))
    return out


def _attn_scores_by_head(q_ref, rows, kc, kp, vcur):
    out = []
    for h in range(8):
        qh = q_ref[rows, (h // 2) * 128:(h // 2 + 1) * 128]
        out.append(jnp.where(vcur, _dot_nt(qh, kc[_head_key(h)]), _dot_nt(qh, kp[_head_key(h)])))
    return out


def _attn_probs(s, h, sink, dist, valid):
    s = s * ATTN_SCALE - (2.0 ** -(h + 1)) * dist
    if valid is not None:
        s = jnp.where(valid, s, NEG_BIG)
    m = jnp.maximum(jnp.max(s, axis=1, keepdims=True), sink)
    e = jnp.exp(s - m)
    es = jnp.exp(sink - m)
    inv = 1.0 / (jnp.sum(e, axis=1, keepdims=True) + es)
    return e * inv, es * inv


def _kv_prev_spec(blocks):
    return pl.BlockSpec((BLK, 2 * KV_W), lambda i: (jnp.maximum(i * blocks - 1, 0), 0))


def _kv_variants(kv_ref, rows, low):
    return (_head_variants(kv_ref[rows, 0:128].astype(F32), low), _head_variants(kv_ref[rows, 128:256].astype(F32), low))


def _gm_forward_block(zu, zv, lng, lnb, w_ref, bsx, low):
    gu = _gelu(zu)
    gv = _gelu(zv)
    mu = jnp.mean(gv, axis=-1, keepdims=True)
    xc = gv - mu
    rstd = lax.rsqrt(jnp.mean(xc * xc, axis=-1, keepdims=True) + NORM_EPS)
    xn = xc * rstd
    ln = xn * lng + lnb
    mixed = []
    for col in range(4):
        lc = ln[:, col * 128:(col + 1) * 128]
        lo = jnp.where(low, lc, 0.0).astype(BF16)
        hi = jnp.where(low, 0.0, lc).astype(BF16)
        mixed.append(_dot(w_ref[2 * col], lo) + _dot(w_ref[2 * col + 1], hi) + bsx[:, col * 128:(col + 1) * 128])
    return gu, ln, xn, rstd, mixed


def _mix_fwd(q, kv, sinks, zu, zv, lng, lnb, wtril, bsx, h, ga, gg, gpost, wb, layer):
    t = q.shape[0]
    tq = _tile(t, 512)
    blocks = tq // BLK

    def body(sink_ref, q_ref, kvc_ref, kvp_ref, zu_ref, zv_ref, g_ref, b_ref, w_ref, bs_ref, h_ref, ga_ref, gg_ref,
             gp_ref, wo_ref, attn_ref, gm_ref, heads_ref, mix_ref, h1_ref):
        low, vcur, dist, valid = _attn_masks(pl.program_id(0))
        kp, vp = _kv_variants(kvp_ref, slice(None), low)

        def project(rows, mix):
            mix_ref[rows, :] = mix.astype(BF16)
            h1_ref[rows, :] = h_ref[rows, :] + _rms_fwd(mix, gp_ref[...])

        pending = None
        for b in range(blocks):
            rows = slice(b * BLK, (b + 1) * BLK)
            kc, vc = _kv_variants(kvc_ref, rows, low)
            ks = [_stack_kv(kp, kc, g) for g in range(2)]
            vs = [_stack_kv(vp, vc, g) for g in range(2)]
            scores = _attn_scores(q_ref, rows, ks, vcur)
            if pending is not None:
                project(pending[0], _dot(pending[1], wo_ref[...]))
            gu, _, _, _, mixed = _gm_forward_block(zu_ref[rows, :], zv_ref[rows, :], g_ref[...], b_ref[...], w_ref,
                                                   bs_ref[...], low)
            probs = [_attn_probs(scores[h], h, sink_ref[h], dist, valid if b == 0 else None)[0] for h in range(8)]
            attn_cols, gm_cols = [], []
            for col in range(4):
                gm_cols.append((gu[:, col * 128:(col + 1) * 128] * mixed[col]).astype(BF16))
                p_col = jnp.concatenate(_split_cols(probs[2 * col], vcur) + _split_cols(probs[2 * col + 1], vcur), axis=1)
                attn_cols.append(_dot(p_col, vs[col // 2]).astype(BF16))
            attn = jnp.concatenate(attn_cols, axis=1)
            gm = jnp.concatenate(gm_cols, axis=1)
            attn_ref[rows, :] = attn
            gm_ref[rows, :] = gm
            heads = jnp.concatenate([_rms_fwd(attn.astype(F32), ga_ref[...]).astype(BF16),
                                     _rms_fwd(gm.astype(F32), gg_ref[...]).astype(BF16)], axis=1)
            heads_ref[rows, :] = heads
            pending = (rows, heads)
            kp, vp = kc, vc
        project(pending[0], _dot(pending[1], wo_ref[...]))

    wide = _row_spec(tq, GM_W)
    row = _row_spec(tq, D_MODEL)
    return pl.pallas_call(
        body, name="mix_fwd", grid=(t // tq,),
        in_specs=[pl.BlockSpec(memory_space=pltpu.SMEM), _row_spec(tq, ATTN_W), _row_spec(tq, 2 * KV_W),
                  _kv_prev_spec(blocks), wide, wide, _vec_spec(GM_W), _vec_spec(GM_W),
                  pl.BlockSpec((8, BLK, BLK), lambda i: (0, 0, 0)), pl.BlockSpec((BLK, GM_W), lambda i: (0, 0)),
                  row, _vec_spec(ATTN_W), _vec_spec(GM_W), _vec_spec(D_MODEL), _seg_spec(ROWS_B, D_MODEL, 2 * layer)],
        out_specs=[_row_spec(tq, ATTN_W), wide, row, row, row],
        out_shape=[jax.ShapeDtypeStruct((t, ATTN_W), BF16), jax.ShapeDtypeStruct((t, GM_W), BF16),
                   jax.ShapeDtypeStruct((t, D_MODEL), BF16), jax.ShapeDtypeStruct((t, D_MODEL), BF16),
                   jax.ShapeDtypeStruct((t, D_MODEL), F32)],
        compiler_params=_params("parallel"),
    )(sinks, q, kv, kv, zu, zv, lng, lnb, wtril, bsx, h, ga, gg, gpost, wb)


def _ffn_fwd(h1, p, p_layer, gpre, gpost, gple, wa, wb, wp, layer, target=None):
    t = h1.shape[0]
    tm = _tile(t, 256)

    def body(h_ref, p_ref, gpre_ref, gpost_ref, gple_ref, wg_ref, wu_ref, wd_ref, wpg_ref, wpl_ref, *rest):
        if target is None:
            f_ref, gp_ref, up_ref, act_ref, fo_ref, h2_ref, hn_ref, gate_ref, h3_ref = rest
        else:
            t_ref, f_ref, gp_ref, up_ref, act_ref, fo_ref, h2_ref, hn_ref, gate_ref, dy_ref, l_ref = rest
            _zero_at(pl.program_id(0) == 0, l_ref)
        h = h_ref[...]
        pe = _dot_nt(p_ref[...].astype(BF16), wpl_ref[...])
        f = _rms_fwd(h, gpre_ref[...]).astype(BF16)
        f_ref[...] = f
        chunks = [slice(j * FF_CHUNK, (j + 1) * FF_CHUNK) for j in range(D_FF // FF_CHUNK)]
        fo = None
        gp, up = _dot_nt(f, wg_ref[chunks[0], :]), _dot_nt(f, wu_ref[chunks[0], :])
        for j, cols in enumerate(chunks):
            if j + 1 < len(chunks):
                gp_next, up_next = _dot_nt(f, wg_ref[chunks[j + 1], :]), _dot_nt(f, wu_ref[chunks[j + 1], :])
            act = (gp * _sigmoid(gp) * up).astype(BF16)
            gp_ref[:, cols] = gp.astype(BF16)
            up_ref[:, cols] = up.astype(BF16)
            act_ref[:, cols] = act
            part = _dot(act, wd_ref[cols, :])
            fo = part if fo is None else fo + part
            if j + 1 < len(chunks):
                gp, up = gp_next, up_next
        fo_ref[...] = fo
        h2 = h + _rms_fwd(fo, gpost_ref[...])
        h2_ref[...] = h2
        hn = _rms_fwd(h2, gple_ref[...]).astype(BF16)
        hn_ref[...] = hn
        gate = _sigmoid(_dot(hn, wpg_ref[...]))
        gate_ref[...] = gate.astype(BF16)
        h3 = h2 + pe * gate
        if target is None:
            h3_ref[...] = h3
        else:
            e = h3 - t_ref[...]
            dy_ref[...] = (e * (1.0 / D_MODEL)).astype(BF16)
            s = jnp.sum(jnp.sum(e * e, axis=1, keepdims=True), axis=0, keepdims=True)
            l_ref[...] += jnp.broadcast_to(s, (1, 128))

    wide = _row_spec(tm, D_FF)
    row = _row_spec(tm, D_MODEL)
    vec = _vec_spec(D_MODEL)
    last = target is not None
    return pl.pallas_call(
        body, name="ffn_loss" if last else "ffn_fwd", grid=(t // tm,),
        in_specs=[row, _layer_row_spec(tm, PLE_DIM, p_layer), vec, vec, vec, _seg_spec(ROWS_A, D_MODEL, 3 * layer),
                  _seg_spec(ROWS_A, D_MODEL, 3 * layer + 1), _seg_spec(ROWS_A, D_MODEL, 3 * layer + 2),
                  _seg_spec(ROWS_B, D_MODEL, 2 * layer + 1), _seg_spec(ROWS_B, PLE_DIM, layer)] + [row] * last,
        out_specs=[row, wide, wide, wide, row, row, row, row, row] + [_vec_spec(128)] * last,
        out_shape=[jax.ShapeDtypeStruct((t, D_MODEL), BF16)] + [jax.ShapeDtypeStruct((t, D_FF), BF16)] * 3
        + [jax.ShapeDtypeStruct((t, D_MODEL), F32)] * 2 + [jax.ShapeDtypeStruct((t, D_MODEL), BF16)] * 2
        + [jax.ShapeDtypeStruct((t, D_MODEL), BF16 if last else F32)] + [jax.ShapeDtypeStruct((1, 128), F32)] * last,
        compiler_params=_params("arbitrary" if last else "parallel"),
    )(h1, p, gpre, gpost, gple, wa, wa, wa, wb, wp, *([target] if last else []))


def _ffn_bwd(dh3, h2, gate, p, p_layer, fo, gp, up, h1, gple, gpost, gpre, wa, wb, wp, layer, after):
    t = dh3.shape[0]
    tm = _tile(t, 256)

    def body(d3_ref, h2_ref, gate_ref, p_ref, fo_ref, gp_ref, up_ref, h_ref, gple_ref, gpost_ref, gpre_ref,
             wg_ref, wu_ref, wd_ref, wpg_ref, wpl_ref, after_ref,
             dgl_ref, dpe_ref, dfo_ref, dgp_ref, dup_ref, dh1_ref, dgple_ref, dgpost_ref, dgpre_ref):
        _zero_at(pl.program_id(0) == 0, dgple_ref, dgpost_ref, dgpre_ref)
        d3 = d3_ref[...].astype(F32)
        gate = gate_ref[...].astype(F32)
        pe = _dot_nt(p_ref[...].astype(BF16), wpl_ref[...])
        dpe_ref[...] = (d3 * gate).astype(BF16)
        dgl = (d3 * pe * gate * (1.0 - gate)).astype(BF16)
        dgl_ref[...] = dgl
        dx2, dgple = _rms_bwd(h2_ref[...], gple_ref[...], _dot_nt(dgl, wpg_ref[...]))
        dgple_ref[...] += dgple
        d = d3 + dx2
        dfo, dgpost = _rms_bwd(fo_ref[...], gpost_ref[...], d)
        dfo = dfo.astype(BF16)
        dfo_ref[...] = dfo
        dgpost_ref[...] += dgpost
        chunks = [slice(j * FF_CHUNK, (j + 1) * FF_CHUNK) for j in range(D_FF // FF_CHUNK)]
        df = None
        dact = _dot_nt(dfo, wd_ref[chunks[0], :])
        for j, cols in enumerate(chunks):
            if j + 1 < len(chunks):
                dact_next = _dot_nt(dfo, wd_ref[chunks[j + 1], :])
            gp = gp_ref[:, cols].astype(F32)
            sg = _sigmoid(gp)
            dgp = (dact * up_ref[:, cols].astype(F32) * (sg * (1.0 + gp * (1.0 - sg)))).astype(BF16)
            dup = (dact * (gp * sg)).astype(BF16)
            dgp_ref[:, cols] = dgp
            dup_ref[:, cols] = dup
            part = _dot(dgp, wg_ref[cols, :]) + _dot(dup, wu_ref[cols, :])
            df = part if df is None else df + part
            if j + 1 < len(chunks):
                dact = dact_next
        dx, dgpre = _rms_bwd(h_ref[...], gpre_ref[...], df)
        dh1_ref[...] = (d + dx).astype(BF16)
        dgpre_ref[...] += dgpre

    wide = _row_spec(tm, D_FF)
    row = _row_spec(tm, D_MODEL)
    vec = _vec_spec(D_MODEL)
    narrow = jax.ShapeDtypeStruct((t, D_MODEL), BF16)
    return pl.pallas_call(
        body, name="ffn_bwd", grid=(t // tm,),
        in_specs=[row, row, row, _layer_row_spec(tm, PLE_DIM, p_layer), row, wide, wide, row, vec, vec, vec,
                  _seg_spec(ROWS_A, D_MODEL, 3 * layer), _seg_spec(ROWS_A, D_MODEL, 3 * layer + 1),
                  _seg_spec(ROWS_A, D_MODEL, 3 * layer + 2), _seg_spec(ROWS_B, D_MODEL, 2 * layer + 1),
                  _seg_spec(ROWS_B, PLE_DIM, layer), pl.BlockSpec(memory_space=pl.ANY)],
        out_specs=[row, row, row, wide, wide, row, vec, vec, vec],
        out_shape=[narrow, narrow, narrow, jax.ShapeDtypeStruct((t, D_FF), BF16), jax.ShapeDtypeStruct((t, D_FF), BF16),
                   narrow] + [jax.ShapeDtypeStruct((1, D_MODEL), F32)] * 3,
        compiler_params=_params("arbitrary"),
    )(dh3, h2, gate, p, fo, gp, up, h1, gple, gpost, gpre, wa, wa, wa, wb, wp, after)


def _out_proj_bwd(dh1, mix, attn, gm, gpost, ga, gg, wb, layer, after):
    t = dh1.shape[0]
    tm = _tile(t, 512)

    def body(d_ref, mix_ref, a_ref, m_ref, gp_ref, ga_ref, gg_ref, w_ref, after_ref,
             dmix_ref, da_ref, dm_ref, dgp_ref, dga_ref, dgg_ref):
        _zero_at(pl.program_id(0) == 0, dgp_ref, dga_ref, dgg_ref)
        dmix, dgp = _rms_bwd(mix_ref[...].astype(F32), gp_ref[...], d_ref[...].astype(F32))
        dmix = dmix.astype(BF16)
        dmix_ref[...] = dmix
        da, dga = _rms_bwd(a_ref[...].astype(F32), ga_ref[...], _dot_nt(dmix, w_ref[0:512, :]))
        dm, dgg = _rms_bwd(m_ref[...].astype(F32), gg_ref[...], _dot_nt(dmix, w_ref[512:1024, :]))
        da_ref[...] = da.astype(BF16)
        dm_ref[...] = dm
        dgp_ref[...] += dgp
        dga_ref[...] += dga
        dgg_ref[...] += dgg

    return pl.pallas_call(
        body, name="out_proj_bwd", grid=(t // tm,),
        in_specs=[_row_spec(tm, D_MODEL), _row_spec(tm, D_MODEL), _row_spec(tm, ATTN_W), _row_spec(tm, GM_W),
                  _vec_spec(D_MODEL), _vec_spec(ATTN_W), _vec_spec(GM_W), _seg_spec(ROWS_B, D_MODEL, 2 * layer),
                  pl.BlockSpec(memory_space=pl.ANY)],
        out_specs=[_row_spec(tm, D_MODEL), _row_spec(tm, ATTN_W), _row_spec(tm, GM_W),
                   _vec_spec(D_MODEL), _vec_spec(ATTN_W), _vec_spec(GM_W)],
        out_shape=[jax.ShapeDtypeStruct((t, D_MODEL), BF16), jax.ShapeDtypeStruct((t, ATTN_W), BF16),
                   jax.ShapeDtypeStruct((t, GM_W), F32), jax.ShapeDtypeStruct((1, D_MODEL), F32),
                   jax.ShapeDtypeStruct((1, ATTN_W), F32), jax.ShapeDtypeStruct((1, GM_W), F32)],
        compiler_params=_params("arbitrary"),
    )(dh1, mix, attn, gm, gpost, ga, gg, wb, after)


def _split3(x):
    hi = x.astype(BF16)
    r1 = x - hi.astype(F32)
    mid = r1.astype(BF16)
    lo = (r1 - mid.astype(F32)).astype(BF16)
    return hi, mid, lo


def _rider_pieces(rider, blocks):
    rm = rider[0].shape[1]
    per = -(-rm // (blocks * 256)) * 256
    return [slice(k * per, min((k + 1) * per, rm)) for k in range(blocks) if k * per < rm]


def _gm_bwd(dgm, zu, zv, lng, lnb, wtril, bsx, rider=None):
    t = zu.shape[0]
    tm = _tile(t, 512)
    nb = t // tm
    if rider is not None:
        ra, rb, rbuf, rseg = rider
        rm, rn = ra.shape[1], rb.shape[1]
        pieces = _rider_pieces(rider, tm // BLK)

    def body(d_ref, zu_ref, zv_ref, g_ref, b_ref, w_ref, bs_ref, *rest):
        if rider is None:
            dzu_ref, dzv_ref, dw_ref, dbs_ref, dlg_ref, dlb_ref, dbsx_ref = rest
        else:
            ra_ref, rb_ref, rbuf_ref, dzu_ref, dzv_ref, dw_ref, dbs_ref, dlg_ref, dlb_ref, ro_ref, dbsx_ref, acc_ref = rest
        i = pl.program_id(0)
        _zero_at(i == 0, dw_ref, dlg_ref, dlb_ref, dbsx_ref)
        if rider is not None:
            _zero_at(i == 0, acc_ref)
            rb16 = rb_ref[...].astype(BF16)
        row = lax.broadcasted_iota(jnp.int32, (BLK, BLK), 0)
        lane = lax.broadcasted_iota(jnp.int32, (BLK, BLK), 1)
        low = lane < 64
        tril = row >= lane
        lng = g_ref[...]
        for b in range(tm // BLK):
            rows = slice(b * BLK, (b + 1) * BLK)
            if rider is not None and b < len(pieces):
                acc_ref[pieces[b], :] += _dot_tn(ra_ref[:, pieces[b]], rb16)
            zu = zu_ref[rows, :]
            zv = zv_ref[rows, :]
            gu, ln, xn, rstd, mixed = _gm_forward_block(zu, zv, lng, b_ref[...], w_ref, bs_ref[...], low)
            dgm = d_ref[rows, :]
            dgu_cols, dmx_cols, dln_cols = [], [], []
            for col in range(4):
                sl = slice(col * 128, (col + 1) * 128)
                dg = dgm[:, sl]
                dgu_cols.append(dg * mixed[col])
                dmx = dg * gu[:, sl]
                dmx_cols.append(dmx)
                lc = ln[:, sl]
                halves = (jnp.where(low, lc, 0.0).astype(BF16), jnp.where(low, 0.0, lc).astype(BF16))
                dmx16 = dmx.astype(BF16)
                dmx_half = (jnp.where(low, dmx, 0.0).astype(BF16), jnp.where(low, 0.0, dmx).astype(BF16))
                dln = None
                for half in range(2):
                    hd = 2 * col + half
                    dw_ref[hd] += jnp.where(tril, _dot_nt(dmx16, halves[half]), 0.0)
                    part = _dot_tn(w_ref[hd], dmx_half[half])
                    dln = part if dln is None else dln + part
                dln_cols.append(dln)
            dgu = jnp.concatenate(dgu_cols, axis=1)
            dmx = jnp.concatenate(dmx_cols, axis=1)
            dln = jnp.concatenate(dln_cols, axis=1)
            dzu_ref[rows, :] = (dgu * _gelu_grad(zu)).astype(BF16)
            dbsx_ref[...] += dmx
            dlg_ref[...] += jnp.sum(dln * xn, axis=0, keepdims=True)
            dlb_ref[...] += jnp.sum(dln, axis=0, keepdims=True)
            dxn = dln * lng
            dgv = rstd * (dxn - jnp.mean(dxn, axis=-1, keepdims=True) - xn * jnp.mean(dxn * xn, axis=-1, keepdims=True))
            dzv_ref[rows, :] = (dgv * _gelu_grad(zv)).astype(BF16)

        @pl.when(i == nb - 1)
        def _():
            r = lax.broadcasted_iota(jnp.int32, (GM_W, BLK), 0)
            c = lax.broadcasted_iota(jnp.int32, (GM_W, BLK), 1)
            e = jnp.where(jnp.logical_and(r >= c * 64, r < c * 64 + 64), 1.0, 0.0).astype(BF16)
            hi, mid, lo = _split3(dbsx_ref[...])
            dbs_ref[...] = _dot(hi, e) + _dot(mid, e) + _dot(lo, e)
            if rider is not None:
                ro_ref[...] = acc_ref[...].astype(ro_ref.dtype)

    vec = _vec_spec(GM_W)
    in_specs = [_row_spec(tm, GM_W)] * 3 + [vec, vec, pl.BlockSpec((8, BLK, BLK), lambda i: (0, 0, 0)),
                                            pl.BlockSpec((BLK, GM_W), lambda i: (0, 0))]
    out_specs = [_row_spec(tm, GM_W), _row_spec(tm, GM_W), pl.BlockSpec((8, BLK, BLK), lambda i: (0, 0, 0)),
                 pl.BlockSpec((BLK, BLK), lambda i: (0, 0)), vec, vec]
    out_shape = [jax.ShapeDtypeStruct((t, GM_W), BF16), jax.ShapeDtypeStruct((t, GM_W), BF16),
                 jax.ShapeDtypeStruct((8, BLK, BLK), F32), jax.ShapeDtypeStruct((BLK, BLK), F32),
                 jax.ShapeDtypeStruct((1, GM_W), F32), jax.ShapeDtypeStruct((1, GM_W), F32)]
    scratch = [pltpu.VMEM((BLK, GM_W), F32)]
    operands = [dgm, zu, zv, lng, lnb, wtril, bsx]
    extra = {}
    if rider is not None:
        in_specs += [_row_spec(tm, rm), _row_spec(tm, rn), pl.BlockSpec(memory_space=pl.ANY)]
        out_specs.append(pl.BlockSpec((rm, rn), lambda i: (rseg, 0)))
        out_shape.append(jax.ShapeDtypeStruct(rbuf.shape, rbuf.dtype))
        scratch.append(pltpu.VMEM((rm, rn), F32))
        operands += [ra, rb, rbuf]
        extra = dict(input_output_aliases={9: 6})
    return pl.pallas_call(
        body, name="gm_bwd" if rider is None else "gm_bwd_rider", grid=(nb,),
        in_specs=in_specs, out_specs=out_specs, out_shape=out_shape, scratch_shapes=scratch,
        compiler_params=_params("arbitrary"), **extra,
    )(*operands)


def _attn_bwd(q, kv, do, sinks, after, rider=None):
    t = q.shape[0]
    tq = _tile(t, 512)
    blocks = tq // BLK
    steps = t // tq
    if rider is not None:
        ra, rb, rbuf, rseg = rider
        rm, rn = ra.shape[1], rb.shape[1]
        per = -(-rm // (blocks * 256)) * 256
        pieces = [slice(k * per, min((k + 1) * per, rm)) for k in range(blocks)]

    def body(sink_ref, q_ref, kvc_ref, kvp_ref, do_ref, after_ref, *rest):
        if rider is None:
            dq_ref, dkv_ref, dkf_ref, ds_ref = rest
        else:
            ra_ref, rb_ref, rbuf_ref, dq_ref, dkv_ref, dkf_ref, ds_ref, ro_ref, acc_ref = rest
        i = pl.program_id(0)
        _zero_at(i == 0, ds_ref)
        if rider is not None:
            _zero_at(i == 0, acc_ref)
            rb16 = rb_ref[...].astype(BF16)
        low, vcur, dist, valid = _attn_masks(i)
        head_row = lax.broadcasted_iota(jnp.int32, (8, 128), 0)
        dsink_tile = jnp.zeros((8, 128), F32)
        kp, vp = _kv_variants(kvp_ref, slice(None), low)
        own = None
        for b in range(blocks):
            rows = slice(b * BLK, (b + 1) * BLK)
            kc, vc = _kv_variants(kvc_ref, rows, low)
            scores = _attn_scores_by_head(q_ref, rows, kc, kp, vcur)
            dprobs = _attn_scores_by_head(do_ref, rows, vc, vp, vcur)
            if rider is not None and pieces[b].start < rm:
                acc_ref[pieces[b], :] += _dot_tn(ra_ref[:, pieces[b]], rb16)
            parts = []
            for h in range(8):
                p, ps = _attn_probs(scores[h], h, sink_ref[h], dist, valid if b == 0 else None)
                delta = jnp.sum(p * dprobs[h], axis=1, keepdims=True)
                ds = p * (dprobs[h] - delta) * ATTN_SCALE
                dsink = jnp.sum(-ps * delta, axis=0, keepdims=True)
                dsink_tile = jnp.where(head_row == h, dsink_tile + dsink, dsink_tile)
                parts.append(_split_cols(ds, vcur) + _split_cols(p, vcur))
            acc = {}

            def add(name, key, val):
                acc[(name, key)] = val if (name, key) not in acc else acc[(name, key)] + val

            for col in range(4):
                qh = q_ref[rows, col * 128:(col + 1) * 128]
                doh = do_ref[rows, col * 128:(col + 1) * 128]
                dq = None
                for half in range(2):
                    key = _head_key(2 * col + half)
                    dsp, dsc, pp, pc = parts[2 * col + half]
                    part = _dot(dsc, kc[key]) + _dot(dsp, kp[key])
                    dq = part if dq is None else dq + part
                    add("kc", key, _dot_tn(dsc, qh))
                    add("kp", key, _dot_tn(dsp, qh))
                    add("vc", key, _dot_tn(pc, doh))
                    add("vp", key, _dot_tn(pp, doh))
                dq_ref[rows, col * 128:(col + 1) * 128] = dq.astype(BF16)

            def place(name):
                head0 = acc[(name, (0, 0))] + pltpu.roll(acc[(name, (0, 1))], 64, axis=1)
                head1 = pltpu.roll(acc[(name, (1, 0))], 64, axis=1) + acc[(name, (1, 1))]
                return jnp.where(low, head0, head1)

            before = (place("kp"), place("vp"))
            if b == 0:
                dkf_ref[:, 0:128], dkf_ref[:, 128:256] = before
            else:
                last = slice((b - 1) * BLK, b * BLK)
                dkv_ref[last, 0:128] = own[0] + before[0]
                dkv_ref[last, 128:256] = own[1] + before[1]
            own = (place("kc"), place("vc"))
            kp, vp = kc, vc
        final = slice((blocks - 1) * BLK, blocks * BLK)
        dkv_ref[final, 0:128], dkv_ref[final, 128:256] = own
        ds_ref[...] += dsink_tile
        if rider is not None:
            @pl.when(i == steps - 1)
            def _():
                ro_ref[...] = acc_ref[...].astype(ro_ref.dtype)

    row_q = _row_spec(tq, ATTN_W)
    row_kv = _row_spec(tq, 2 * KV_W)
    in_specs = [pl.BlockSpec(memory_space=pltpu.SMEM), row_q, row_kv, _kv_prev_spec(blocks), row_q,
                pl.BlockSpec(memory_space=pl.ANY)]
    out_specs = [row_q, row_kv, _row_spec(BLK, 2 * KV_W), pl.BlockSpec((8, 128), lambda i: (0, 0))]
    out_shape = [jax.ShapeDtypeStruct((t, ATTN_W), BF16), jax.ShapeDtypeStruct((t, 2 * KV_W), F32),
                 jax.ShapeDtypeStruct((t // tq * BLK, 2 * KV_W), F32), jax.ShapeDtypeStruct((8, 128), F32)]
    operands = [sinks, q, kv, kv, do, after]
    extra = {}
    if rider is not None:
        in_specs += [_row_spec(tq, rm), _row_spec(tq, rn), pl.BlockSpec(memory_space=pl.ANY)]
        out_specs.append(pl.BlockSpec((rm, rn), lambda i: (rseg, 0)))
        out_shape.append(jax.ShapeDtypeStruct(rbuf.shape, rbuf.dtype))
        operands += [ra, rb, rbuf]
        extra = dict(scratch_shapes=[pltpu.VMEM((rm, rn), F32)], input_output_aliases={8: 4})
    return pl.pallas_call(
        body, name="attn_bwd" if rider is None else "attn_bwd_rider", grid=(steps,),
        in_specs=in_specs, out_specs=out_specs, out_shape=out_shape,
        compiler_params=_params("arbitrary"), **extra,
    )(*operands)


def _in_proj_bwd(dq, dkv, dkf, dzu, dzv, h, dres, g, wc, layer, dh_dtype):
    t = h.shape[0]
    tm = _tile(t, 512)
    steps = t // tm

    def body(dq_ref, dkv_ref, dkn_ref, dzu_ref, dzv_ref, h_ref, d_ref, g_ref, w_ref, dz_ref, dh_ref, dg_ref):
        i = pl.program_id(0)
        _zero_at(i == 0, dg_ref)
        dq = dq_ref[...]
        tail = dkv_ref[tm - BLK:tm, :] + jnp.where(i < steps - 1, dkn_ref[...], 0.0)
        dkv = tail if tm == BLK else jnp.concatenate([dkv_ref[0:tm - BLK, :], tail], axis=0)
        dkv = dkv.astype(BF16)
        dzu = dzu_ref[...]
        dzv = dzv_ref[...]
        dz_ref[:, 0:512] = dq
        dz_ref[:, 512:768] = dkv
        dz_ref[:, 768:1280] = dzu
        dz_ref[:, 1280:1792] = dzv
        da = (_dot(dq, w_ref[0:512, :]) + _dot(dkv, w_ref[512:768, :]) + _dot(dzu, w_ref[768:1280, :])
              + _dot(dzv, w_ref[1280:1792, :]))
        dx, dg = _rms_bwd(h_ref[...], g_ref[...], da)
        dh_ref[...] = (d_ref[...].astype(F32) + dx).astype(dh_dtype)
        dg_ref[...] += dg

    return pl.pallas_call(
        body, name="in_proj_bwd", grid=(t // tm,),
        in_specs=[_row_spec(tm, ATTN_W), _row_spec(tm, 2 * KV_W),
                  pl.BlockSpec((BLK, 2 * KV_W), lambda i: (jnp.minimum(i + 1, steps - 1), 0)), _row_spec(tm, GM_W),
                  _row_spec(tm, GM_W), _row_spec(tm, D_MODEL), _row_spec(tm, D_MODEL), _vec_spec(D_MODEL),
                  _seg_spec(ROWS_C, D_MODEL, layer)],
        out_specs=[_row_spec(tm, D_IN), _row_spec(tm, D_MODEL), _vec_spec(D_MODEL)],
        out_shape=[jax.ShapeDtypeStruct((t, D_IN), BF16), jax.ShapeDtypeStruct((t, D_MODEL), dh_dtype),
                   jax.ShapeDtypeStruct((1, D_MODEL), F32)],
        compiler_params=_params("arbitrary"),
    )(dq, dkv, dkf, dzu, dzv, h, dres, g, wc)


def _weight_grad(a, b, buf, seg, b_layer=None):
    t, m = a.shape
    n = b.shape[-1]
    assert buf.shape[0] % m == 0 and buf.shape[1] == n
    tm = _tile(t, WGRAD_TOKENS)
    steps = t // tm
    half = m // 2

    def body(a_ref, b_ref, buf_ref, o_ref, acc_ref):
        i = pl.program_id(0)
        _zero_at(i == 0, acc_ref)
        b16 = b_ref[...].astype(BF16)
        for rows in (slice(0, half), slice(half, m)):
            acc_ref[rows, :] += _dot_tn(a_ref[:, rows], b16)

        @pl.when(i == steps - 1)
        def _():
            o_ref[...] = acc_ref[...].astype(o_ref.dtype)

    return pl.pallas_call(
        body, name="weight_grad", grid=(steps,),
        in_specs=[_row_spec(tm, m), _row_spec(tm, n) if b_layer is None else _layer_row_spec(tm, n, b_layer),
                  pl.BlockSpec(memory_space=pl.ANY)],
        out_specs=pl.BlockSpec((m, n), lambda i: (seg, 0)),
        out_shape=jax.ShapeDtypeStruct(buf.shape, buf.dtype),
        scratch_shapes=[pltpu.VMEM((m, n), F32)],
        input_output_aliases={2: 0},
        compiler_params=_params("arbitrary"),
    )(a, b, buf)


def _rows8(rows):
    return [jnp.pad(r, ((0, 7), (0, 0))) for r in rows]


def _small_pack_gating(d):
    rows = [jnp.concatenate([d["gm_ln_g"], d["gm_ln_b"]], axis=1), d["gm_bs"].reshape(1, 1024)]
    return jnp.concatenate(_rows8(rows) + [d["gm_ws"].reshape(128, 1024)], axis=0)


def _small_pack_rest(d):
    rows = [d["ln_mix_pre"], d["ln_mix_post"], d["ln_ffn_pre"], d["ln_ffn_post"], d["ln_ple_gate"],
            jnp.concatenate([d["g_attn_out"], d["g_gm_out"]], axis=1),
            jnp.pad(d["attn_sinks"].reshape(1, 8), ((0, 0), (0, 1016)))]
    return jnp.concatenate(_rows8(rows), axis=0)


def _unpack_gating(g):
    return {"gm_ln_g": g[:, 0, :512], "gm_ln_b": g[:, 0, 512:], "gm_bs": g[:, 8].reshape(DEPTH, 8, 128),
            "gm_ws": g[:, 16:GATING_ROWS].reshape(DEPTH, 8, 128, 128)}


def _unpack_rest(s):
    return {"ln_mix_pre": s[:, 0], "ln_mix_post": s[:, 8], "ln_ffn_pre": s[:, 16], "ln_ffn_post": s[:, 24],
            "ln_ple_gate": s[:, 32], "g_attn_out": s[:, 40, :512], "g_gm_out": s[:, 40, 512:], "attn_sinks": s[:, 48, :8]}


def _row(v):
    return v.reshape(1, -1)


def _layer_fwd(h, p, sp, l, weights, target=None):
    tril = jnp.tril(jnp.ones((BLK, BLK), bool))
    wtril = jnp.where(tril[None], sp["gm_ws"][l], 0.0).astype(BF16)
    bsx = jnp.repeat(sp["gm_bs"][l].T, HEAD_DIM, axis=1)
    a, q, kv, zu, zv = _in_proj(h, _row(sp["ln_mix_pre"][l]), weights("c", h), 0)
    wb = weights("b", zu)
    attn, gm, heads, mix, h1 = _mix_fwd(
        q, kv, sp["attn_sinks"][l], zu, zv, _row(sp["gm_ln_g"][l]), _row(sp["gm_ln_b"][l]), wtril, bsx, h,
        _row(sp["g_attn_out"][l]), _row(sp["g_gm_out"][l]), _row(sp["ln_mix_post"][l]), wb, 0)
    wa = weights("a", h1)
    f, gpre, up, act, fo, h2, hn, gate, *out = _ffn_fwd(
        h1, p, l, _row(sp["ln_ffn_pre"][l]), _row(sp["ln_ffn_post"][l]), _row(sp["ln_ple_gate"][l]), wa, wb,
        weights("p", zu), 0, target)
    saved = dict(h=h, a=a, q=q, kv=kv, zu=zu, zv=zv, attn=attn, gm=gm, heads=heads, mix=mix, h1=h1, f=f,
                 gpre=gpre, up=up, act=act, fo=fo, h2=h2, hn=hn, gate=gate, wtril=wtril, bsx=bsx)
    return (out[0] if target is None else tuple(out)), saved


def _layer_bwd_upper(dh, s, p, sp, l, wa, wb, wp, after, ride):
    d = {}
    dgl, dpe, dfo, dgp, dup, dh1, d["ln_ple_gate"], d["ln_ffn_post"], d["ln_ffn_pre"] = _ffn_bwd(
        dh, s["h2"], s["gate"], p, l, s["fo"], s["gpre"], s["up"], s["h1"], _row(sp["ln_ple_gate"][l]),
        _row(sp["ln_ffn_post"][l]), _row(sp["ln_ffn_pre"][l]), wa, wb, wp, 0, after)
    gb = _weight_grad(s["hn"], dgl, lax.empty((2 * D_MODEL, D_MODEL), BF16), 1)
    gp = _weight_grad(dpe, p, lax.empty((D_MODEL, PLE_DIM), BF16), 0, b_layer=l)
    ga = _weight_grad(s["act"], dfo, lax.empty((3 * D_FF, D_MODEL), BF16), 2)
    if not ride:
        ga = _weight_grad(dgp, s["f"], ga, 0)
        ga = _weight_grad(dup, s["f"], ga, 1)
    return (dh1, d, dgp, dup), ga, gp, gb


def _layer_bwd_middle(carry, s, sp, l, wb, gb, ga, after, ride):
    dh1, d, dgp, dup = carry
    dmix, dattn, dgm, d["ln_mix_post"], d["g_attn_out"], d["g_gm_out"] = _out_proj_bwd(
        dh1, s["mix"], s["attn"], s["gm"], _row(sp["ln_mix_post"][l]), _row(sp["g_attn_out"][l]),
        _row(sp["g_gm_out"][l]), wb, 0, after)
    gb = _weight_grad(s["heads"], dmix, gb, 0)
    dzu, dzv, d["gm_ws"], dbs, d["gm_ln_g"], d["gm_ln_b"], *rode = _gm_bwd(
        dgm, s["zu"], s["zv"], _row(sp["gm_ln_g"][l]), _row(sp["gm_ln_b"][l]), s["wtril"], s["bsx"],
        (dup, s["f"], ga, 1) if ride else None)
    d["gm_bs"] = dbs[:, :8].T
    return (dh1, dattn, dzu, dzv, d, dgp), gb, (rode[0] if ride else ga), _small_pack_gating(d)


def _layer_bwd_lower(carry, s, sp, l, wc, ga, after, ride):
    dh1, dattn, dzu, dzv, d, dgp = carry
    dq, dkv, dkf, dsink, *rode = _attn_bwd(s["q"], s["kv"], dattn, sp["attn_sinks"][l], after,
                                            (dgp, s["f"], ga, 0) if ride else None)
    ga = rode[0] if ride else ga
    d["attn_sinks"] = dsink[:, 0]
    dz, dh, d["ln_mix_pre"] = _in_proj_bwd(dq, dkv, dkf, dzu, dzv, s["h"], dh1, _row(sp["ln_mix_pre"][l]), wc, 0,
                                           F32 if l == 0 else BF16)
    gc = _weight_grad(dz, s["a"], lax.empty((D_IN, D_MODEL), BF16), 0)
    return dh, gc, ga, _small_pack_rest(d)


ANY = pl.BlockSpec(memory_space=pl.ANY)


HBM = pl.BlockSpec(memory_space=pltpu.HBM)
SEM = pl.BlockSpec(memory_space=pltpu.SEMAPHORE)
N_PEERS = N_DEV - 1


def _peers():
    x, y, c = lax.axis_index("x"), lax.axis_index("y"), lax.axis_index("c")
    peers = []
    for r in range(1, N_DEV):
        px = 1 - x if r & 4 else x
        py = 1 - y if r & 2 else y
        pc = 1 - c if r & 1 else c
        peers.append(((px, py, pc), 4 * px + 2 * py + pc))
    return 4 * x + 2 * y + c, peers


GATHER, SCATTER, SPREAD = "gather", "scatter", "spread"


def _peer_copy(src, land, send_sems, recv_sems, r, me, peer, peer_slot, mode):
    return pltpu.make_async_remote_copy(
        src_ref=src.at[:, pl.ds(peer_slot, 1)] if mode == SCATTER else src,
        dst_ref=land.at[:, pl.ds(me, 1)] if mode == GATHER else land.at[:, pl.ds(r - 1, 1)],
        send_sem=send_sems.at[r - 1], recv_sem=recv_sems.at[r - 1], device_id=peer, device_id_type=MESH)


def _peer_arrival(src, land, send_sems, recv_sems, r, me, peer, peer_slot, mode):
    return pltpu.make_async_remote_copy(
        src_ref=src.at[:, pl.ds(me, 1)] if mode == SCATTER else src,
        dst_ref=land.at[:, pl.ds(peer_slot, 1)] if mode == GATHER else land.at[:, pl.ds(r - 1, 1)],
        send_sem=send_sems.at[r - 1], recv_sem=recv_sems.at[r - 1], device_id=peer, device_id_type=MESH)


def _send_start(name, srcs, lands, modes):
    n = len(srcs)

    def body(*refs):
        src_refs, land_refs = refs[:n], refs[n:2 * n]
        outs = refs[2 * n:]
        send_sems, recv_sems, token = outs[2 * n:3 * n], outs[3 * n:4 * n], outs[4 * n]
        me, peers = _peers()
        for k in range(n):
            for r, (peer, slot) in enumerate(peers, 1):
                _peer_copy(src_refs[k], land_refs[k], send_sems[k], recv_sems[k], r, me, peer, slot, modes[k]).start()
        token[...] = jnp.zeros_like(token)

    hbm = lambda a: pltpu.HBM(a.shape, a.dtype)
    sems = [pltpu.SemaphoreType.DMA((N_PEERS,))] * (2 * n)
    outs = pl.pallas_call(
        body, name=name, in_specs=[HBM] * (2 * n),
        out_specs=[HBM] * (2 * n) + [SEM] * (2 * n) + [pl.BlockSpec(memory_space=pltpu.VMEM)],
        out_shape=[hbm(a) for a in srcs] + [hbm(a) for a in lands] + sems + [jax.ShapeDtypeStruct((8, 128), F32)],
        input_output_aliases={k: k for k in range(2 * n)},
        compiler_params=pltpu.CompilerParams(has_side_effects=pltpu.SideEffectType.DATAFLOW_SIDE_EFFECTING),
    )(*[pltpu.with_memory_space_constraint(a, pltpu.HBM) for a in list(srcs) + list(lands)])
    return dict(srcs=outs[:n], lands=outs[n:2 * n], send=outs[2 * n:3 * n], recv=outs[3 * n:4 * n],
                modes=list(modes)), outs[4 * n]


def _send_wait(name, sent, ks, after):
    n = len(ks)
    srcs = [sent["srcs"][k] for k in ks]
    lands = [sent["lands"][k] for k in ks]
    modes = [sent["modes"][k] for k in ks]

    def body(*refs):
        src_refs, land_refs = refs[:n], refs[n:2 * n]
        send_sems, recv_sems = refs[2 * n:3 * n], refs[3 * n:4 * n]
        me, peers = _peers()
        for k in range(n):
            for r, (peer, slot) in enumerate(peers, 1):
                args = (src_refs[k], land_refs[k], send_sems[k], recv_sems[k], r, me, peer, slot, modes[k])
                _peer_copy(*args).wait_send()
                _peer_arrival(*args).wait_recv()

    hbm = lambda a: pltpu.HBM(a.shape, a.dtype)
    outs = pl.pallas_call(
        body, name=name, in_specs=[HBM] * (2 * n) + [SEM] * (2 * n) + [ANY],
        out_specs=[HBM] * (2 * n), out_shape=[hbm(a) for a in srcs] + [hbm(a) for a in lands],
        input_output_aliases={k: k for k in range(2 * n)},
        compiler_params=pltpu.CompilerParams(has_side_effects=pltpu.SideEffectType.DATAFLOW_SIDE_EFFECTING),
    )(*srcs, *lands, *[sent["send"][k] for k in ks], *[sent["recv"][k] for k in ks], after)
    return outs[n:], outs[:n]


def _sum_blocks(own, land, ids):
    nseg, _, rows, cols = land.shape

    def body(ids_ref, own_ref, land_ref, o_ref):
        me = ids_ref[1]
        total = None
        for j in range(N_DEV):
            slot = jnp.maximum(jnp.bitwise_xor(me, j) - 1, 0)
            term = jnp.where(me == j, own_ref[...], land_ref[slot]).astype(F32)
            total = term if total is None else total + term
        o_ref[...] = total

    return pl.pallas_call(
        body, name="sum_blocks",
        grid_spec=pltpu.PrefetchScalarGridSpec(
            num_scalar_prefetch=1, grid=(nseg,),
            in_specs=[pl.BlockSpec((None, None, rows, cols), lambda s, ids: (s, ids[0], 0, 0)),
                      pl.BlockSpec((None, N_PEERS, rows, cols), lambda s, ids: (s, 0, 0, 0))],
            out_specs=pl.BlockSpec((None, rows, cols), lambda s, ids: (s, 0, 0))),
        out_shape=jax.ShapeDtypeStruct((nseg, rows, cols), F32),
        compiler_params=_params("parallel"),
    )(ids, own, land)


def _adamw(w, g, m, v):
    shape = w.shape
    cols = shape[-1]
    rows = w.size // cols
    tr = rows
    for cand in (512, 256, 128, 64, 32, 16, 8):
        if rows % cand == 0:
            tr = cand
            break
    c1 = 1.0 / (1.0 - ADAM_B1 ** ADAM_STEP)
    c2 = 1.0 / (1.0 - ADAM_B2 ** ADAM_STEP)

    def body(w_ref, g_ref, m_ref, v_ref, d_ref, nm_ref, nv_ref):
        g = g_ref[...]
        m = ADAM_B1 * m_ref[...] + (1.0 - ADAM_B1) * g
        v = ADAM_B2 * v_ref[...] + (1.0 - ADAM_B2) * (g * g)
        nm_ref[...] = m
        nv_ref[...] = v
        d_ref[...] = -ADAM_LR * ((m * c1) / (jnp.sqrt(v * c2) + ADAM_EPS) + ADAM_WD * w_ref[...])

    spec = pl.BlockSpec((tr, cols), lambda i: (i, 0))
    outs = pl.pallas_call(
        body, name="adamw", grid=(rows // tr,),
        in_specs=[spec] * 4, out_specs=[spec] * 3,
        out_shape=[jax.ShapeDtypeStruct((rows, cols), F32)] * 3,
        compiler_params=_params("parallel"),
    )(*[a.reshape(rows, cols) for a in (w, g, m, v)])
    return [o.reshape(shape) for o in outs]


SMALL = ("ln_mix_pre", "attn_sinks", "gm_ln_g", "gm_ln_b", "gm_ws", "gm_bs", "g_attn_out", "g_gm_out",
         "ln_mix_post", "ln_ffn_pre", "ln_ffn_post", "ln_ple_gate")
WEIGHTS = ("ln_mix_pre", "w_in", "attn_sinks", "gm_ln_g", "gm_ln_b", "gm_ws", "gm_bs", "g_attn_out", "g_gm_out",
           "w_out", "ln_mix_post", "ln_ffn_pre", "w_ffn_gate", "w_ffn_up", "w_ffn_down", "ln_ffn_post", "w_ple",
           "ln_ple_gate", "w_ple_gate")


def _pack_shards(w, l):
    sa = jnp.stack([w["w_ffn_gate"][l].T, w["w_ffn_up"][l].T, w["w_ffn_down"][l]])[:, None]
    sb = jnp.stack([w["w_out"][l], w["w_ple_gate"][l]])[:, None]
    return [w["w_in"][l].T[None, None].astype(BF16), sb.astype(BF16), w["w_ple"][l].T[None, None].astype(BF16),
            sa.astype(BF16)]


def kernel(x, p, ln_mix_pre, w_in, attn_sinks, gm_ln_g, gm_ln_b, gm_ws, gm_bs, g_attn_out, g_gm_out, w_out, ln_mix_post, ln_ffn_pre, w_ffn_gate, w_ffn_up, w_ffn_down, ln_ffn_post, w_ple, ln_ple_gate, w_ple_gate, loss_target, m_ln_mix_pre, m_w_in, m_attn_sinks, m_gm_ln_g, m_gm_ln_b, m_gm_ws, m_gm_bs, m_g_attn_out, m_g_gm_out, m_w_out, m_ln_mix_post, m_ln_ffn_pre, m_w_ffn_gate, m_w_ffn_up, m_w_ffn_down, m_ln_ffn_post, m_w_ple, m_ln_ple_gate, m_w_ple_gate, v_ln_mix_pre, v_w_in, v_attn_sinks, v_gm_ln_g, v_gm_ln_b, v_gm_ws, v_gm_bs, v_g_attn_out, v_g_gm_out, v_w_out, v_ln_mix_post, v_ln_ffn_pre, v_w_ffn_gate, v_w_ffn_up, v_w_ffn_down, v_ln_ffn_post, v_w_ple, v_ln_ple_gate, v_w_ple_gate):
    given = dict(locals())
    w = {n: given[n] for n in WEIGHTS}
    sp = {n: w[n] for n in SMALL}
    kinds = ("c", "b", "p", "a")

    me, _ = _peers()

    def gather_start(name, layers):
        shards = [s for l in layers for s in _pack_shards(w, l)]
        lands = [lax.dynamic_update_slice(lax.empty((s.shape[0], N_DEV) + s.shape[2:], BF16), s, (0, me, 0, 0))
                 for s in shards]
        return _send_start(name, shards, lands, [GATHER] * len(shards))

    gather_first, token = gather_start("gather_start_0", [0])
    gather_rest, _ = gather_start("gather_start", range(1, DEPTH))
    layer_weights = [{} for _ in range(DEPTH)]

    def weights_of(l):
        def get(kind, after):
            have = layer_weights[l]
            if kind not in have:
                if l < 2:
                    group = {"c": ("c",), "b": ("b", "p"), "p": ("b", "p"), "a": ("a",)}[kind]
                    after = token if (l == 0 and kind == "c") else after
                else:
                    group = kinds
                sent, first = (gather_first, 0) if l == 0 else (gather_rest, 4 * (l - 1))
                got, _ = _send_wait(f"gather_wait_{l}{group[0]}", sent, [first + kinds.index(k) for k in group], after)
                for k, g in zip(group, got):
                    have[k] = g.reshape(-1, g.shape[-1])
            return have[kind]
        return get

    h = x[0]
    p3 = p.reshape(DEPTH, -1, PLE_DIM)
    saved = []
    for l in range(DEPTH):
        h, s = _layer_fwd(h, p3, sp, l, weights_of(l), loss_target[0] if l == DEPTH - 1 else None)
        saved.append(s)
    dh, sq = h

    started = []
    after = token
    view = lambda g, rows: g.reshape(-1, N_DEV, rows, g.shape[-1])
    pack16 = lambda s: s.astype(BF16)[None, None]
    packs = ("gating", "rest")

    def send(l, tag, items):
        bufs = list(items.values())
        lands = [lax.empty((a.shape[0], N_PEERS) + a.shape[2:], BF16) for a in bufs]
        sent, tok = _send_start(f"reduce_start_{l}{tag}", bufs, lands, [SPREAD if k in packs else SCATTER for k in items])
        started.append((l, tag, sent, list(items)))
        return tok

    for l in reversed(range(DEPTH)):
        lw = layer_weights[l]
        ride = l > 0
        carry, ga, gp, gb = _layer_bwd_upper(dh, saved[l], p3, sp, l, lw["a"], lw["b"], lw["p"], after, ride)
        if not ride:
            after = send(l, "a", {"a": view(ga, ROWS_A)})
        carry, gb, ga, gating = _layer_bwd_middle(carry, saved[l], sp, l, lw["b"], gb, ga, after, ride)
        after = send(l, "b", {"b": view(gb, ROWS_B), "gating": pack16(gating), "p": view(gp, ROWS_B)})
        dh, gc, ga, rest = _layer_bwd_lower(carry, saved[l], sp, l, lw["c"], ga, after, ride)
        after = send(l, "c", {"c": view(gc, ROWS_C), "rest": pack16(rest), **({"a": view(ga, ROWS_A)} if ride else {})})

    mine = jnp.stack([me, me]).astype(jnp.int32)
    whole = jnp.stack([jnp.zeros_like(me), me]).astype(jnp.int32)
    sums = {k: [None] * DEPTH for k in ("a", "p", "b", "c", "gating", "rest")}

    def collect(group, behind):
        l, tag, sent, keys = group
        lands, srcs = _send_wait(f"reduce_wait_{l}{tag}", sent, list(range(len(keys))), behind)
        for key, land, src in zip(keys, lands, srcs):
            sums[key][l] = _sum_blocks(src, land, whole)[0] if key in packs else _sum_blocks(src, land, mine)

    for group in started[:-1]:
        collect(group, dh)
    grad_x = dh
    loss = lax.psum(sq[0, 0] * (0.5 / D_MODEL), AXES)
    grads, delta, new_m, new_v = {}, {}, {}, {}

    def update(names):
        for n in names:
            delta[n], new_m[n], new_v[n] = _adamw(w[n], grads[n], given["m_" + n], given["v_" + n])

    stack = lambda f, xs: jnp.stack([f(x) for x in xs])
    grads.update({
        "w_ffn_gate": stack(lambda r: r[0].T, sums["a"]), "w_ffn_up": stack(lambda r: r[1].T, sums["a"]),
        "w_ffn_down": stack(lambda r: r[2], sums["a"]), "w_ple": stack(lambda r: r[0].T, sums["p"]),
        "w_out": stack(lambda r: r[0], sums["b"]), "w_ple_gate": stack(lambda r: r[1], sums["b"])})
    grads.update(_unpack_gating(jnp.stack(sums["gating"])))
    early = tuple(grads)
    update(early)
    collect(started[-1], jnp.concatenate([delta[n].reshape(-1)[:1] for n in early]))
    grads["w_in"] = stack(lambda r: r[0].T, sums["c"])
    grads.update(_unpack_rest(jnp.stack(sums["rest"])))
    update([n for n in grads if n not in early])
    return (loss, grad_x[None], *[grads[n] for n in WEIGHTS], *[delta[n] for n in WEIGHTS],
            *[new_m[n] for n in WEIGHTS], *[new_v[n] for n in WEIGHTS])
```

```python
import math

import jax
import jax.numpy as jnp
from jax import lax
from jax.experimental import pallas as pl
from jax.experimental.pallas import tpu as pltpu

F32 = jnp.float32
BF16 = jnp.bfloat16
MESH = pl.DeviceIdType.MESH
AXES = ("x", "y", "c")

D_MODEL = 1024
DEPTH = 4
N_DEV = 8
HEAD_DIM = 64
ATTN_W = 512
KV_W = 128
GM_W = 512
D_IN = 1792
D_FF = 2816
PLE_DIM = 256
BLK = 128
FF_CHUNK = 256
WGRAD_TOKENS = 1024
NORM_EPS = 1e-6
NEG_BIG = -1e30
ATTN_SCALE = HEAD_DIM ** -0.5

ADAM_LR = 0.001
ADAM_B1 = 0.9
ADAM_B2 = 0.999
ADAM_EPS = 1e-08
ADAM_WD = 0.01
ADAM_STEP = 10

ROWS_A = D_FF // N_DEV
ROWS_B = D_MODEL // N_DEV
ROWS_C = D_IN // N_DEV
GATING_ROWS = 144
REST_ROWS = 56

VMEM_LIMIT = 56 * 2 ** 20


def _params(*sem):
    return pltpu.CompilerParams(dimension_semantics=sem, vmem_limit_bytes=VMEM_LIMIT)


def _dot(a, b):
    return jnp.dot(a, b, preferred_element_type=F32)


def _dot_nt(a, b):
    return lax.dot_general(a, b, (((1,), (1,)), ((), ())), preferred_element_type=F32)


def _dot_tn(a, b):
    return lax.dot_general(a, b, (((0,), (0,)), ((), ())), preferred_element_type=F32)


def _rms_fwd(x, g):
    r = lax.rsqrt(jnp.mean(x * x, axis=-1, keepdims=True) + NORM_EPS)
    return x * r * g


def _rms_bwd(x, g, dy):
    r = lax.rsqrt(jnp.mean(x * x, axis=-1, keepdims=True) + NORM_EPS)
    xh = x * r
    dg = jnp.sum(dy * xh, axis=0, keepdims=True)
    dxh = dy * g
    dx = r * (dxh - xh * jnp.mean(dxh * xh, axis=-1, keepdims=True))
    return dx, dg


_GELU_C = math.sqrt(2.0 / math.pi)


def _gelu(x):
    t = jnp.tanh(_GELU_C * (x + 0.044715 * (x * x * x)))
    return 0.5 * x * (1.0 + t)


def _gelu_grad(x):
    x2 = x * x
    t = jnp.tanh(_GELU_C * (x + 0.044715 * (x2 * x)))
    return 0.5 * (1.0 + t) + 0.5 * x * (1.0 - t * t) * (_GELU_C * (1.0 + 3.0 * 0.044715 * x2))


def _sigmoid(x):
    return 1.0 / (1.0 + jnp.exp(-x))


def _row_spec(tm, n):
    return pl.BlockSpec((tm, n), lambda i: (i, 0))


def _layer_row_spec(tm, n, l):
    return pl.BlockSpec((None, tm, n), lambda i: (l, i, 0))


def _vec_spec(n):
    return pl.BlockSpec((1, n), lambda i: (0, 0))


def _seg_spec(rows, cols, seg):
    return pl.BlockSpec((N_DEV * rows, cols), lambda i: (seg, 0), pipeline_mode=pl.Buffered(1))


def _zero_at(first, *refs):
    @pl.when(first)
    def _():
        for r in refs:
            r[...] = jnp.zeros(r.shape, r.dtype)


def _tile(t, want):
    return min(t, want)


def _in_proj(h, g, wc, layer):
    t = h.shape[0]
    tm = _tile(t, 512)

    def body(h_ref, g_ref, w_ref, a_ref, q_ref, kv_ref, zu_ref, zv_ref):
        a = _rms_fwd(h_ref[...], g_ref[...]).astype(BF16)
        a_ref[...] = a
        q_ref[...] = _dot_nt(a, w_ref[0:512, :]).astype(BF16)
        kv_ref[...] = _dot_nt(a, w_ref[512:768, :]).astype(BF16)
        zu_ref[...] = _dot_nt(a, w_ref[768:1280, :])
        zv_ref[...] = _dot_nt(a, w_ref[1280:1792, :])

    return pl.pallas_call(
        body, name="in_proj", grid=(t // tm,),
        in_specs=[_row_spec(tm, D_MODEL), _vec_spec(D_MODEL), _seg_spec(ROWS_C, D_MODEL, layer)],
        out_specs=[_row_spec(tm, D_MODEL), _row_spec(tm, ATTN_W), _row_spec(tm, 2 * KV_W),
                   _row_spec(tm, GM_W), _row_spec(tm, GM_W)],
        out_shape=[jax.ShapeDtypeStruct((t, D_MODEL), BF16), jax.ShapeDtypeStruct((t, ATTN_W), BF16),
                   jax.ShapeDtypeStruct((t, 2 * KV_W), BF16), jax.ShapeDtypeStruct((t, GM_W), F32),
                   jax.ShapeDtypeStruct((t, GM_W), F32)],
        compiler_params=_params("parallel"),
    )(h, g, wc)


def _head_variants(x, low):
    xr = pltpu.roll(x, 64, axis=1)
    zero = jnp.zeros_like(x)
    return {
        (0, 0): jnp.where(low, x, zero).astype(BF16),
        (0, 1): jnp.where(low, zero, xr).astype(BF16),
        (1, 0): jnp.where(low, xr, zero).astype(BF16),
        (1, 1): jnp.where(low, zero, x).astype(BF16),
    }


def _attn_masks(i):
    row = lax.broadcasted_iota(jnp.int32, (BLK, BLK), 0)
    lane = lax.broadcasted_iota(jnp.int32, (BLK, BLK), 1)
    vcur = row >= lane
    dist = jnp.where(vcur, row - lane, row - lane + BLK).astype(F32)
    valid = jnp.logical_or(vcur, i > 0)
    return lane < 64, vcur, dist, valid


def _head_key(h):
    return (h // 4, h % 2)


def _stack_kv(prev, cur, g):
    return jnp.concatenate([prev[(g, 0)], cur[(g, 0)], prev[(g, 1)], cur[(g, 1)]], axis=0)


def _split_cols(p, vcur):
    return [jnp.where(vcur, 0.0, p).astype(BF16), jnp.where(vcur, p, 0.0).astype(BF16)]


def _attn_scores(q_ref, rows, stacked, vcur):
    out = []
    for col in range(4):
        big = _dot_nt(q_ref[rows, col * 128:(col + 1) * 128], stacked[col // 2])
        for half in range(2):
            out.append(jnp.where(vcur, big[:, half * 256 + 128:half * 256 + 256], big[:, half * 256:half * 256 + 128]))
    return out


def _attn_scores_by_head(q_ref, rows, kc, kp, vcur):
    out = []
    for h in range(8):
        qh = q_ref[rows, (h // 2) * 128:(h // 2 + 1) * 128]
        out.append(jnp.where(vcur, _dot_nt(qh, kc[_head_key(h)]), _dot_nt(qh, kp[_head_key(h)])))
    return out


def _attn_probs(s, h, sink, dist, valid):
    s = s * ATTN_SCALE - (2.0 ** -(h + 1)) * dist
    if valid is not None:
        s = jnp.where(valid, s, NEG_BIG)
    m = jnp.maximum(jnp.max(s, axis=1, keepdims=True), sink)
    e = jnp.exp(s - m)
    es = jnp.exp(sink - m)
    inv = 1.0 / (jnp.sum(e, axis=1, keepdims=True) + es)
    return e * inv, es * inv


def _kv_prev_spec(blocks):
    return pl.BlockSpec((BLK, 2 * KV_W), lambda i: (jnp.maximum(i * blocks - 1, 0), 0))


def _kv_variants(kv_ref, rows, low):
    return (_head_variants(kv_ref[rows, 0:128].astype(F32), low), _head_variants(kv_ref[rows, 128:256].astype(F32), low))


def _gm_forward_block(zu, zv, lng, lnb, w_ref, bsx, low):
    gu = _gelu(zu)
    gv = _gelu(zv)
    mu = jnp.mean(gv, axis=-1, keepdims=True)
    xc = gv - mu
    rstd = lax.rsqrt(jnp.mean(xc * xc, axis=-1, keepdims=True) + NORM_EPS)
    xn = xc * rstd
    ln = xn * lng + lnb
    mixed = []
    for col in range(4):
        lc = ln[:, col * 128:(col + 1) * 128]
        lo = jnp.where(low, lc, 0.0).astype(BF16)
        hi = jnp.where(low, 0.0, lc).astype(BF16)
        mixed.append(_dot(w_ref[2 * col], lo) + _dot(w_ref[2 * col + 1], hi) + bsx[:, col * 128:(col + 1) * 128])
    return gu, ln, xn, rstd, mixed


def _mix_fwd(q, kv, sinks, zu, zv, lng, lnb, wtril, bsx, h, ga, gg, gpost, wb, layer):
    t = q.shape[0]
    tq = _tile(t, 512)
    blocks = tq // BLK

    def body(sink_ref, q_ref, kvc_ref, kvp_ref, zu_ref, zv_ref, g_ref, b_ref, w_ref, bs_ref, h_ref, ga_ref, gg_ref,
             gp_ref, wo_ref, attn_ref, gm_ref, heads_ref, mix_ref, h1_ref):
        low, vcur, dist, valid = _attn_masks(pl.program_id(0))
        kp, vp = _kv_variants(kvp_ref, slice(None), low)

        def project(rows, mix):
            mix_ref[rows, :] = mix.astype(BF16)
            h1_ref[rows, :] = h_ref[rows, :] + _rms_fwd(mix, gp_ref[...])

        pending = None
        for b in range(blocks):
            rows = slice(b * BLK, (b + 1) * BLK)
            kc, vc = _kv_variants(kvc_ref, rows, low)
            ks = [_stack_kv(kp, kc, g) for g in range(2)]
            vs = [_stack_kv(vp, vc, g) for g in range(2)]
            scores = _attn_scores(q_ref, rows, ks, vcur)
            if pending is not None:
                project(pending[0], _dot(pending[1], wo_ref[...]))
            gu, _, _, _, mixed = _gm_forward_block(zu_ref[rows, :], zv_ref[rows, :], g_ref[...], b_ref[...], w_ref,
                                                   bs_ref[...], low)
            probs = [_attn_probs(scores[h], h, sink_ref[h], dist, valid if b == 0 else None)[0] for h in range(8)]
            attn_cols, gm_cols = [], []
            for col in range(4):
                gm_cols.append((gu[:, col * 128:(col + 1) * 128] * mixed[col]).astype(BF16))
                p_col = jnp.concatenate(_split_cols(probs[2 * col], vcur) + _split_cols(probs[2 * col + 1], vcur), axis=1)
                attn_cols.append(_dot(p_col, vs[col // 2]).astype(BF16))
            attn = jnp.concatenate(attn_cols, axis=1)
            gm = jnp.concatenate(gm_cols, axis=1)
            attn_ref[rows, :] = attn
            gm_ref[rows, :] = gm
            heads = jnp.concatenate([_rms_fwd(attn.astype(F32), ga_ref[...]).astype(BF16),
                                     _rms_fwd(gm.astype(F32), gg_ref[...]).astype(BF16)], axis=1)
            heads_ref[rows, :] = heads
            pending = (rows, heads)
            kp, vp = kc, vc
        project(pending[0], _dot(pending[1], wo_ref[...]))

    wide = _row_spec(tq, GM_W)
    row = _row_spec(tq, D_MODEL)
    return pl.pallas_call(
        body, name="mix_fwd", grid=(t // tq,),
        in_specs=[pl.BlockSpec(memory_space=pltpu.SMEM), _row_spec(tq, ATTN_W), _row_spec(tq, 2 * KV_W),
                  _kv_prev_spec(blocks), wide, wide, _vec_spec(GM_W), _vec_spec(GM_W),
                  pl.BlockSpec((8, BLK, BLK), lambda i: (0, 0, 0)), pl.BlockSpec((BLK, GM_W), lambda i: (0, 0)),
                  row, _vec_spec(ATTN_W), _vec_spec(GM_W), _vec_spec(D_MODEL), _seg_spec(ROWS_B, D_MODEL, 2 * layer)],
        out_specs=[_row_spec(tq, ATTN_W), wide, row, row, row],
        out_shape=[jax.ShapeDtypeStruct((t, ATTN_W), BF16), jax.ShapeDtypeStruct((t, GM_W), BF16),
                   jax.ShapeDtypeStruct((t, D_MODEL), BF16), jax.ShapeDtypeStruct((t, D_MODEL), BF16),
                   jax.ShapeDtypeStruct((t, D_MODEL), F32)],
        compiler_params=_params("parallel"),
    )(sinks, q, kv, kv, zu, zv, lng, lnb, wtril, bsx, h, ga, gg, gpost, wb)


def _ffn_fwd(h1, p, p_layer, gpre, gpost, gple, wa, wb, wp, layer, target=None):
    t = h1.shape[0]
    tm = _tile(t, 256)

    def body(h_ref, p_ref, gpre_ref, gpost_ref, gple_ref, wg_ref, wu_ref, wd_ref, wpg_ref, wpl_ref, *rest):
        if target is None:
            f_ref, gp_ref, up_ref, act_ref, fo_ref, h2_ref, hn_ref, gate_ref, h3_ref = rest
        else:
            t_ref, f_ref, gp_ref, up_ref, act_ref, fo_ref, h2_ref, hn_ref, gate_ref, dy_ref, l_ref = rest
            _zero_at(pl.program_id(0) == 0, l_ref)
        h = h_ref[...]
        pe = _dot_nt(p_ref[...].astype(BF16), wpl_ref[...])
        f = _rms_fwd(h, gpre_ref[...]).astype(BF16)
        f_ref[...] = f
        chunks = [slice(j * FF_CHUNK, (j + 1) * FF_CHUNK) for j in range(D_FF // FF_CHUNK)]
        fo = None
        gp, up = _dot_nt(f, wg_ref[chunks[0], :]), _dot_nt(f, wu_ref[chunks[0], :])
        for j, cols in enumerate(chunks):
            if j + 1 < len(chunks):
                gp_next, up_next = _dot_nt(f, wg_ref[chunks[j + 1], :]), _dot_nt(f, wu_ref[chunks[j + 1], :])
            act = (gp * _sigmoid(gp) * up).astype(BF16)
            gp_ref[:, cols] = gp.astype(BF16)
            up_ref[:, cols] = up.astype(BF16)
            act_ref[:, cols] = act
            part = _dot(act, wd_ref[cols, :])
            fo = part if fo is None else fo + part
            if j + 1 < len(chunks):
                gp, up = gp_next, up_next
        fo_ref[...] = fo
        h2 = h + _rms_fwd(fo, gpost_ref[...])
        h2_ref[...] = h2
        hn = _rms_fwd(h2, gple_ref[...]).astype(BF16)
        hn_ref[...] = hn
        gate = _sigmoid(_dot(hn, wpg_ref[...]))
        gate_ref[...] = gate.astype(BF16)
        h3 = h2 + pe * gate
        if target is None:
            h3_ref[...] = h3
        else:
            e = h3 - t_ref[...]
            dy_ref[...] = (e * (1.0 / D_MODEL)).astype(BF16)
            s = jnp.sum(jnp.sum(e * e, axis=1, keepdims=True), axis=0, keepdims=True)
            l_ref[...] += jnp.broadcast_to(s, (1, 128))

    wide = _row_spec(tm, D_FF)
    row = _row_spec(tm, D_MODEL)
    vec = _vec_spec(D_MODEL)
    last = target is not None
    return pl.pallas_call(
        body, name="ffn_loss" if last else "ffn_fwd", grid=(t // tm,),
        in_specs=[row, _layer_row_spec(tm, PLE_DIM, p_layer), vec, vec, vec, _seg_spec(ROWS_A, D_MODEL, 3 * layer),
                  _seg_spec(ROWS_A, D_MODEL, 3 * layer + 1), _seg_spec(ROWS_A, D_MODEL, 3 * layer + 2),
                  _seg_spec(ROWS_B, D_MODEL, 2 * layer + 1), _seg_spec(ROWS_B, PLE_DIM, layer)] + [row] * last,
        out_specs=[row, wide, wide, wide, row, row, row, row, row] + [_vec_spec(128)] * last,
        out_shape=[jax.ShapeDtypeStruct((t, D_MODEL), BF16)] + [jax.ShapeDtypeStruct((t, D_FF), BF16)] * 3
        + [jax.ShapeDtypeStruct((t, D_MODEL), F32)] * 2 + [jax.ShapeDtypeStruct((t, D_MODEL), BF16)] * 2
        + [jax.ShapeDtypeStruct((t, D_MODEL), BF16 if last else F32)] + [jax.ShapeDtypeStruct((1, 128), F32)] * last,
        compiler_params=_params("arbitrary" if last else "parallel"),
    )(h1, p, gpre, gpost, gple, wa, wa, wa, wb, wp, *([target] if last else []))


def _ffn_bwd(dh3, h2, gate, p, p_layer, fo, gp, up, h1, gple, gpost, gpre, wa, wb, wp, layer, after):
    t = dh3.shape[0]
    tm = _tile(t, 256)

    def body(d3_ref, h2_ref, gate_ref, p_ref, fo_ref, gp_ref, up_ref, h_ref, gple_ref, gpost_ref, gpre_ref,
             wg_ref, wu_ref, wd_ref, wpg_ref, wpl_ref, after_ref,
             dgl_ref, dpe_ref, dfo_ref, dgp_ref, dup_ref, dh1_ref, dgple_ref, dgpost_ref, dgpre_ref):
        _zero_at(pl.program_id(0) == 0, dgple_ref, dgpost_ref, dgpre_ref)
        d3 = d3_ref[...].astype(F32)
        gate = gate_ref[...].astype(F32)
        pe = _dot_nt(p_ref[...].astype(BF16), wpl_ref[...])
        dpe_ref[...] = (d3 * gate).astype(BF16)
        dgl = (d3 * pe * gate * (1.0 - gate)).astype(BF16)
        dgl_ref[...] = dgl
        dx2, dgple = _rms_bwd(h2_ref[...], gple_ref[...], _dot_nt(dgl, wpg_ref[...]))
        dgple_ref[...] += dgple
        d = d3 + dx2
        dfo, dgpost = _rms_bwd(fo_ref[...], gpost_ref[...], d)
        dfo = dfo.astype(BF16)
        dfo_ref[...] = dfo
        dgpost_ref[...] += dgpost
        chunks = [slice(j * FF_CHUNK, (j + 1) * FF_CHUNK) for j in range(D_FF // FF_CHUNK)]
        df = None
        dact = _dot_nt(dfo, wd_ref[chunks[0], :])
        for j, cols in enumerate(chunks):
            if j + 1 < len(chunks):
                dact_next = _dot_nt(dfo, wd_ref[chunks[j + 1], :])
            gp = gp_ref[:, cols].astype(F32)
            sg = _sigmoid(gp)
            dgp = (dact * up_ref[:, cols].astype(F32) * (sg * (1.0 + gp * (1.0 - sg)))).astype(BF16)
            dup = (dact * (gp * sg)).astype(BF16)
            dgp_ref[:, cols] = dgp
            dup_ref[:, cols] = dup
            part = _dot(dgp, wg_ref[cols, :]) + _dot(dup, wu_ref[cols, :])
            df = part if df is None else df + part
            if j + 1 < len(chunks):
                dact = dact_next
        dx, dgpre = _rms_bwd(h_ref[...], gpre_ref[...], df)
        dh1_ref[...] = (d + dx).astype(BF16)
        dgpre_ref[...] += dgpre

    wide = _row_spec(tm, D_FF)
    row = _row_spec(tm, D_MODEL)
    vec = _vec_spec(D_MODEL)
    narrow = jax.ShapeDtypeStruct((t, D_MODEL), BF16)
    return pl.pallas_call(
        body, name="ffn_bwd", grid=(t // tm,),
        in_specs=[row, row, row, _layer_row_spec(tm, PLE_DIM, p_layer), row, wide, wide, row, vec, vec, vec,
                  _seg_spec(ROWS_A, D_MODEL, 3 * layer), _seg_spec(ROWS_A, D_MODEL, 3 * layer + 1),
                  _seg_spec(ROWS_A, D_MODEL, 3 * layer + 2), _seg_spec(ROWS_B, D_MODEL, 2 * layer + 1),
                  _seg_spec(ROWS_B, PLE_DIM, layer), pl.BlockSpec(memory_space=pl.ANY)],
        out_specs=[row, row, row, wide, wide, row, vec, vec, vec],
        out_shape=[narrow, narrow, narrow, jax.ShapeDtypeStruct((t, D_FF), BF16), jax.ShapeDtypeStruct((t, D_FF), BF16),
                   narrow] + [jax.ShapeDtypeStruct((1, D_MODEL), F32)] * 3,
        compiler_params=_params("arbitrary"),
    )(dh3, h2, gate, p, fo, gp, up, h1, gple, gpost, gpre, wa, wa, wa, wb, wp, after)


def _out_proj_bwd(dh1, mix, attn, gm, gpost, ga, gg, wb, layer, after):
    t = dh1.shape[0]
    tm = _tile(t, 512)

    def body(d_ref, mix_ref, a_ref, m_ref, gp_ref, ga_ref, gg_ref, w_ref, after_ref,
             dmix_ref, da_ref, dm_ref, dgp_ref, dga_ref, dgg_ref):
        _zero_at(pl.program_id(0) == 0, dgp_ref, dga_ref, dgg_ref)
        dmix, dgp = _rms_bwd(mix_ref[...].astype(F32), gp_ref[...], d_ref[...].astype(F32))
        dmix = dmix.astype(BF16)
        dmix_ref[...] = dmix
        da, dga = _rms_bwd(a_ref[...].astype(F32), ga_ref[...], _dot_nt(dmix, w_ref[0:512, :]))
        dm, dgg = _rms_bwd(m_ref[...].astype(F32), gg_ref[...], _dot_nt(dmix, w_ref[512:1024, :]))
        da_ref[...] = da.astype(BF16)
        dm_ref[...] = dm
        dgp_ref[...] += dgp
        dga_ref[...] += dga
        dgg_ref[...] += dgg

    return pl.pallas_call(
        body, name="out_proj_bwd", grid=(t // tm,),
        in_specs=[_row_spec(tm, D_MODEL), _row_spec(tm, D_MODEL), _row_spec(tm, ATTN_W), _row_spec(tm, GM_W),
                  _vec_spec(D_MODEL), _vec_spec(ATTN_W), _vec_spec(GM_W), _seg_spec(ROWS_B, D_MODEL, 2 * layer),
                  pl.BlockSpec(memory_space=pl.ANY)],
        out_specs=[_row_spec(tm, D_MODEL), _row_spec(tm, ATTN_W), _row_spec(tm, GM_W),
                   _vec_spec(D_MODEL), _vec_spec(ATTN_W), _vec_spec(GM_W)],
        out_shape=[jax.ShapeDtypeStruct((t, D_MODEL), BF16), jax.ShapeDtypeStruct((t, ATTN_W), BF16),
                   jax.ShapeDtypeStruct((t, GM_W), F32), jax.ShapeDtypeStruct((1, D_MODEL), F32),
                   jax.ShapeDtypeStruct((1, ATTN_W), F32), jax.ShapeDtypeStruct((1, GM_W), F32)],
        compiler_params=_params("arbitrary"),
    )(dh1, mix, attn, gm, gpost, ga, gg, wb, after)


def _split3(x):
    hi = x.astype(BF16)
    r1 = x - hi.astype(F32)
    mid = r1.astype(BF16)
    lo = (r1 - mid.astype(F32)).astype(BF16)
    return hi, mid, lo


def _rider_pieces(rider, blocks):
    rm = rider[0].shape[1]
    per = -(-rm // (blocks * 256)) * 256
    return [slice(k * per, min((k + 1) * per, rm)) for k in range(blocks) if k * per < rm]


def _gm_bwd(dgm, zu, zv, lng, lnb, wtril, bsx, rider=None):
    t = zu.shape[0]
    tm = _tile(t, 512)
    nb = t // tm
    if rider is not None:
        ra, rb, rbuf, rseg = rider
        rm, rn = ra.shape[1], rb.shape[1]
        pieces = _rider_pieces(rider, tm // BLK)

    def body(d_ref, zu_ref, zv_ref, g_ref, b_ref, w_ref, bs_ref, *rest):
        if rider is None:
            dzu_ref, dzv_ref, dw_ref, dbs_ref, dlg_ref, dlb_ref, dbsx_ref = rest
        else:
            ra_ref, rb_ref, rbuf_ref, dzu_ref, dzv_ref, dw_ref, dbs_ref, dlg_ref, dlb_ref, ro_ref, dbsx_ref, acc_ref = rest
        i = pl.program_id(0)
        _zero_at(i == 0, dw_ref, dlg_ref, dlb_ref, dbsx_ref)
        if rider is not None:
            _zero_at(i == 0, acc_ref)
            rb16 = rb_ref[...].astype(BF16)
        row = lax.broadcasted_iota(jnp.int32, (BLK, BLK), 0)
        lane = lax.broadcasted_iota(jnp.int32, (BLK, BLK), 1)
        low = lane < 64
        tril = row >= lane
        lng = g_ref[...]
        for b in range(tm // BLK):
            rows = slice(b * BLK, (b + 1) * BLK)
            if rider is not None and b < len(pieces):
                acc_ref[pieces[b], :] += _dot_tn(ra_ref[:, pieces[b]], rb16)
            zu = zu_ref[rows, :]
            zv = zv_ref[rows, :]
            gu, ln, xn, rstd, mixed = _gm_forward_block(zu, zv, lng, b_ref[...], w_ref, bs_ref[...], low)
            dgm = d_ref[rows, :]
            dgu_cols, dmx_cols, dln_cols = [], [], []
            for col in range(4):
                sl = slice(col * 128, (col + 1) * 128)
                dg = dgm[:, sl]
                dgu_cols.append(dg * mixed[col])
                dmx = dg * gu[:, sl]
                dmx_cols.append(dmx)
                lc = ln[:, sl]
                halves = (jnp.where(low, lc, 0.0).astype(BF16), jnp.where(low, 0.0, lc).astype(BF16))
                dmx16 = dmx.astype(BF16)
                dmx_half = (jnp.where(low, dmx, 0.0).astype(BF16), jnp.where(low, 0.0, dmx).astype(BF16))
                dln = None
                for half in range(2):
                    hd = 2 * col + half
                    dw_ref[hd] += jnp.where(tril, _dot_nt(dmx16, halves[half]), 0.0)
                    part = _dot_tn(w_ref[hd], dmx_half[half])
                    dln = part if dln is None else dln + part
                dln_cols.append(dln)
            dgu = jnp.concatenate(dgu_cols, axis=1)
            dmx = jnp.concatenate(dmx_cols, axis=1)
            dln = jnp.concatenate(dln_cols, axis=1)
            dzu_ref[rows, :] = (dgu * _gelu_grad(zu)).astype(BF16)
            dbsx_ref[...] += dmx
            dlg_ref[...] += jnp.sum(dln * xn, axis=0, keepdims=True)
            dlb_ref[...] += jnp.sum(dln, axis=0, keepdims=True)
            dxn = dln * lng
            dgv = rstd * (dxn - jnp.mean(dxn, axis=-1, keepdims=True) - xn * jnp.mean(dxn * xn, axis=-1, keepdims=True))
            dzv_ref[rows, :] = (dgv * _gelu_grad(zv)).astype(BF16)

        @pl.when(i == nb - 1)
        def _():
            r = lax.broadcasted_iota(jnp.int32, (GM_W, BLK), 0)
            c = lax.broadcasted_iota(jnp.int32, (GM_W, BLK), 1)
            e = jnp.where(jnp.logical_and(r >= c * 64, r < c * 64 + 64), 1.0, 0.0).astype(BF16)
            hi, mid, lo = _split3(dbsx_ref[...])
            dbs_ref[...] = _dot(hi, e) + _dot(mid, e) + _dot(lo, e)
            if rider is not None:
                ro_ref[...] = acc_ref[...].astype(ro_ref.dtype)

    vec = _vec_spec(GM_W)
    in_specs = [_row_spec(tm, GM_W)] * 3 + [vec, vec, pl.BlockSpec((8, BLK, BLK), lambda i: (0, 0, 0)),
                                            pl.BlockSpec((BLK, GM_W), lambda i: (0, 0))]
    out_specs = [_row_spec(tm, GM_W), _row_spec(tm, GM_W), pl.BlockSpec((8, BLK, BLK), lambda i: (0, 0, 0)),
                 pl.BlockSpec((BLK, BLK), lambda i: (0, 0)), vec, vec]
    out_shape = [jax.ShapeDtypeStruct((t, GM_W), BF16), jax.ShapeDtypeStruct((t, GM_W), BF16),
                 jax.ShapeDtypeStruct((8, BLK, BLK), F32), jax.ShapeDtypeStruct((BLK, BLK), F32),
                 jax.ShapeDtypeStruct((1, GM_W), F32), jax.ShapeDtypeStruct((1, GM_W), F32)]
    scratch = [pltpu.VMEM((BLK, GM_W), F32)]
    operands = [dgm, zu, zv, lng, lnb, wtril, bsx]
    extra = {}
    if rider is not None:
        in_specs += [_row_spec(tm, rm), _row_spec(tm, rn), pl.BlockSpec(memory_space=pl.ANY)]
        out_specs.append(pl.BlockSpec((rm, rn), lambda i: (rseg, 0)))
        out_shape.append(jax.ShapeDtypeStruct(rbuf.shape, rbuf.dtype))
        scratch.append(pltpu.VMEM((rm, rn), F32))
        operands += [ra, rb, rbuf]
        extra = dict(input_output_aliases={9: 6})
    return pl.pallas_call(
        body, name="gm_bwd" if rider is None else "gm_bwd_rider", grid=(nb,),
        in_specs=in_specs, out_specs=out_specs, out_shape=out_shape, scratch_shapes=scratch,
        compiler_params=_params("arbitrary"), **extra,
    )(*operands)


def _attn_bwd(q, kv, do, sinks, after, rider=None):
    t = q.shape[0]
    tq = _tile(t, 512)
    blocks = tq // BLK
    steps = t // tq
    if rider is not None:
        ra, rb, rbuf, rseg = rider
        rm, rn = ra.shape[1], rb.shape[1]
        per = -(-rm // (blocks * 256)) * 256
        pieces = [slice(k * per, min((k + 1) * per, rm)) for k in range(blocks)]

    def body(sink_ref, q_ref, kvc_ref, kvp_ref, do_ref, after_ref, *rest):
        if rider is None:
            dq_ref, dkv_ref, dkf_ref, ds_ref = rest
        else:
            ra_ref, rb_ref, rbuf_ref, dq_ref, dkv_ref, dkf_ref, ds_ref, ro_ref, acc_ref = rest
        i = pl.program_id(0)
        _zero_at(i == 0, ds_ref)
        if rider is not None:
            _zero_at(i == 0, acc_ref)
            rb16 = rb_ref[...].astype(BF16)
        low, vcur, dist, valid = _attn_masks(i)
        head_row = lax.broadcasted_iota(jnp.int32, (8, 128), 0)
        dsink_tile = jnp.zeros((8, 128), F32)
        kp, vp = _kv_variants(kvp_ref, slice(None), low)
        own = None
        for b in range(blocks):
            rows = slice(b * BLK, (b + 1) * BLK)
            kc, vc = _kv_variants(kvc_ref, rows, low)
            scores = _attn_scores_by_head(q_ref, rows, kc, kp, vcur)
            dprobs = _attn_scores_by_head(do_ref, rows, vc, vp, vcur)
            if rider is not None and pieces[b].start < rm:
                acc_ref[pieces[b], :] += _dot_tn(ra_ref[:, pieces[b]], rb16)
            parts = []
            for h in range(8):
                p, ps = _attn_probs(scores[h], h, sink_ref[h], dist, valid if b == 0 else None)
                delta = jnp.sum(p * dprobs[h], axis=1, keepdims=True)
                ds = p * (dprobs[h] - delta) * ATTN_SCALE
                dsink = jnp.sum(-ps * delta, axis=0, keepdims=True)
                dsink_tile = jnp.where(head_row == h, dsink_tile + dsink, dsink_tile)
                parts.append(_split_cols(ds, vcur) + _split_cols(p, vcur))
            acc = {}

            def add(name, key, val):
                acc[(name, key)] = val if (name, key) not in acc else acc[(name, key)] + val

            for col in range(4):
                qh = q_ref[rows, col * 128:(col + 1) * 128]
                doh = do_ref[rows, col * 128:(col + 1) * 128]
                dq = None
                for half in range(2):
                    key = _head_key(2 * col + half)
                    dsp, dsc, pp, pc = parts[2 * col + half]
                    part = _dot(dsc, kc[key]) + _dot(dsp, kp[key])
                    dq = part if dq is None else dq + part
                    add("kc", key, _dot_tn(dsc, qh))
                    add("kp", key, _dot_tn(dsp, qh))
                    add("vc", key, _dot_tn(pc, doh))
                    add("vp", key, _dot_tn(pp, doh))
                dq_ref[rows, col * 128:(col + 1) * 128] = dq.astype(BF16)

            def place(name):
                head0 = acc[(name, (0, 0))] + pltpu.roll(acc[(name, (0, 1))], 64, axis=1)
                head1 = pltpu.roll(acc[(name, (1, 0))], 64, axis=1) + acc[(name, (1, 1))]
                return jnp.where(low, head0, head1)

            before = (place("kp"), place("vp"))
            if b == 0:
                dkf_ref[:, 0:128], dkf_ref[:, 128:256] = before
            else:
                last = slice((b - 1) * BLK, b * BLK)
                dkv_ref[last, 0:128] = own[0] + before[0]
                dkv_ref[last, 128:256] = own[1] + before[1]
            own = (place("kc"), place("vc"))
            kp, vp = kc, vc
        final = slice((blocks - 1) * BLK, blocks * BLK)
        dkv_ref[final, 0:128], dkv_ref[final, 128:256] = own
        ds_ref[...] += dsink_tile
        if rider is not None:
            @pl.when(i == steps - 1)
            def _():
                ro_ref[...] = acc_ref[...].astype(ro_ref.dtype)

    row_q = _row_spec(tq, ATTN_W)
    row_kv = _row_spec(tq, 2 * KV_W)
    in_specs = [pl.BlockSpec(memory_space=pltpu.SMEM), row_q, row_kv, _kv_prev_spec(blocks), row_q,
                pl.BlockSpec(memory_space=pl.ANY)]
    out_specs = [row_q, row_kv, _row_spec(BLK, 2 * KV_W), pl.BlockSpec((8, 128), lambda i: (0, 0))]
    out_shape = [jax.ShapeDtypeStruct((t, ATTN_W), BF16), jax.ShapeDtypeStruct((t, 2 * KV_W), F32),
                 jax.ShapeDtypeStruct((t // tq * BLK, 2 * KV_W), F32), jax.ShapeDtypeStruct((8, 128), F32)]
    operands = [sinks, q, kv, kv, do, after]
    extra = {}
    if rider is not None:
        in_specs += [_row_spec(tq, rm), _row_spec(tq, rn), pl.BlockSpec(memory_space=pl.ANY)]
        out_specs.append(pl.BlockSpec((rm, rn), lambda i: (rseg, 0)))
        out_shape.append(jax.ShapeDtypeStruct(rbuf.shape, rbuf.dtype))
        operands += [ra, rb, rbuf]
        extra = dict(scratch_shapes=[pltpu.VMEM((rm, rn), F32)], input_output_aliases={8: 4})
    return pl.pallas_call(
        body, name="attn_bwd" if rider is None else "attn_bwd_rider", grid=(steps,),
        in_specs=in_specs, out_specs=out_specs, out_shape=out_shape,
        compiler_params=_params("arbitrary"), **extra,
    )(*operands)


def _in_proj_bwd(dq, dkv, dkf, dzu, dzv, h, dres, g, wc, layer, dh_dtype):
    t = h.shape[0]
    tm = _tile(t, 512)
    steps = t // tm

    def body(dq_ref, dkv_ref, dkn_ref, dzu_ref, dzv_ref, h_ref, d_ref, g_ref, w_ref, dz_ref, dh_ref, dg_ref):
        i = pl.program_id(0)
        _zero_at(i == 0, dg_ref)
        dq = dq_ref[...]
        tail = dkv_ref[tm - BLK:tm, :] + jnp.where(i < steps - 1, dkn_ref[...], 0.0)
        dkv = tail if tm == BLK else jnp.concatenate([dkv_ref[0:tm - BLK, :], tail], axis=0)
        dkv = dkv.astype(BF16)
        dzu = dzu_ref[...]
        dzv = dzv_ref[...]
        dz_ref[:, 0:512] = dq
        dz_ref[:, 512:768] = dkv
        dz_ref[:, 768:1280] = dzu
        dz_ref[:, 1280:1792] = dzv
        da = (_dot(dq, w_ref[0:512, :]) + _dot(dkv, w_ref[512:768, :]) + _dot(dzu, w_ref[768:1280, :])
              + _dot(dzv, w_ref[1280:1792, :]))
        dx, dg = _rms_bwd(h_ref[...], g_ref[...], da)
        dh_ref[...] = (d_ref[...].astype(F32) + dx).astype(dh_dtype)
        dg_ref[...] += dg

    return pl.pallas_call(
        body, name="in_proj_bwd", grid=(t // tm,),
        in_specs=[_row_spec(tm, ATTN_W), _row_spec(tm, 2 * KV_W),
                  pl.BlockSpec((BLK, 2 * KV_W), lambda i: (jnp.minimum(i + 1, steps - 1), 0)), _row_spec(tm, GM_W),
                  _row_spec(tm, GM_W), _row_spec(tm, D_MODEL), _row_spec(tm, D_MODEL), _vec_spec(D_MODEL),
                  _seg_spec(ROWS_C, D_MODEL, layer)],
        out_specs=[_row_spec(tm, D_IN), _row_spec(tm, D_MODEL), _vec_spec(D_MODEL)],
        out_shape=[jax.ShapeDtypeStruct((t, D_IN), BF16), jax.ShapeDtypeStruct((t, D_MODEL), dh_dtype),
                   jax.ShapeDtypeStruct((1, D_MODEL), F32)],
        compiler_params=_params("arbitrary"),
    )(dq, dkv, dkf, dzu, dzv, h, dres, g, wc)


def _weight_grad(a, b, buf, seg, b_layer=None):
    t, m = a.shape
    n = b.shape[-1]
    assert buf.shape[0] % m == 0 and buf.shape[1] == n
    tm = _tile(t, WGRAD_TOKENS)
    steps = t // tm
    half = m // 2

    def body(a_ref, b_ref, buf_ref, o_ref, acc_ref):
        i = pl.program_id(0)
        _zero_at(i == 0, acc_ref)
        b16 = b_ref[...].astype(BF16)
        for rows in (slice(0, half), slice(half, m)):
            acc_ref[rows, :] += _dot_tn(a_ref[:, rows], b16)

        @pl.when(i == steps - 1)
        def _():
            o_ref[...] = acc_ref[...].astype(o_ref.dtype)

    return pl.pallas_call(
        body, name="weight_grad", grid=(steps,),
        in_specs=[_row_spec(tm, m), _row_spec(tm, n) if b_layer is None else _layer_row_spec(tm, n, b_layer),
                  pl.BlockSpec(memory_space=pl.ANY)],
        out_specs=pl.BlockSpec((m, n), lambda i: (seg, 0)),
        out_shape=jax.ShapeDtypeStruct(buf.shape, buf.dtype),
        scratch_shapes=[pltpu.VMEM((m, n), F32)],
        input_output_aliases={2: 0},
        compiler_params=_params("arbitrary"),
    )(a, b, buf)


def _rows8(rows):
    return [jnp.pad(r, ((0, 7), (0, 0))) for r in rows]


def _small_pack_gating(d):
    rows = [jnp.concatenate([d["gm_ln_g"], d["gm_ln_b"]], axis=1), d["gm_bs"].reshape(1, 1024)]
    return jnp.concatenate(_rows8(rows) + [d["gm_ws"].reshape(128, 1024)], axis=0)


def _small_pack_rest(d):
    rows = [d["ln_mix_pre"], d["ln_mix_post"], d["ln_ffn_pre"], d["ln_ffn_post"], d["ln_ple_gate"],
            jnp.concatenate([d["g_attn_out"], d["g_gm_out"]], axis=1),
            jnp.pad(d["attn_sinks"].reshape(1, 8), ((0, 0), (0, 1016)))]
    return jnp.concatenate(_rows8(rows), axis=0)


def _unpack_gating(g):
    return {"gm_ln_g": g[:, 0, :512], "gm_ln_b": g[:, 0, 512:], "gm_bs": g[:, 8].reshape(DEPTH, 8, 128),
            "gm_ws": g[:, 16:GATING_ROWS].reshape(DEPTH, 8, 128, 128)}


def _unpack_rest(s):
    return {"ln_mix_pre": s[:, 0], "ln_mix_post": s[:, 8], "ln_ffn_pre": s[:, 16], "ln_ffn_post": s[:, 24],
            "ln_ple_gate": s[:, 32], "g_attn_out": s[:, 40, :512], "g_gm_out": s[:, 40, 512:], "attn_sinks": s[:, 48, :8]}


def _row(v):
    return v.reshape(1, -1)


def _layer_fwd(h, p, sp, l, weights, target=None):
    tril = jnp.tril(jnp.ones((BLK, BLK), bool))
    wtril = jnp.where(tril[None], sp["gm_ws"][l], 0.0).astype(BF16)
    bsx = jnp.repeat(sp["gm_bs"][l].T, HEAD_DIM, axis=1)
    a, q, kv, zu, zv = _in_proj(h, _row(sp["ln_mix_pre"][l]), weights("c", h), 0)
    wb = weights("b", zu)
    attn, gm, heads, mix, h1 = _mix_fwd(
        q, kv, sp["attn_sinks"][l], zu, zv, _row(sp["gm_ln_g"][l]), _row(sp["gm_ln_b"][l]), wtril, bsx, h,
        _row(sp["g_attn_out"][l]), _row(sp["g_gm_out"][l]), _row(sp["ln_mix_post"][l]), wb, 0)
    wa = weights("a", h1)
    f, gpre, up, act, fo, h2, hn, gate, *out = _ffn_fwd(
        h1, p, l, _row(sp["ln_ffn_pre"][l]), _row(sp["ln_ffn_post"][l]), _row(sp["ln_ple_gate"][l]), wa, wb,
        weights("p", zu), 0, target)
    saved = dict(h=h, a=a, q=q, kv=kv, zu=zu, zv=zv, attn=attn, gm=gm, heads=heads, mix=mix, h1=h1, f=f,
                 gpre=gpre, up=up, act=act, fo=fo, h2=h2, hn=hn, gate=gate, wtril=wtril, bsx=bsx)
    return (out[0] if target is None else tuple(out)), saved


def _layer_bwd_upper(dh, s, p, sp, l, wa, wb, wp, after, ride):
    d = {}
    dgl, dpe, dfo, dgp, dup, dh1, d["ln_ple_gate"], d["ln_ffn_post"], d["ln_ffn_pre"] = _ffn_bwd(
        dh, s["h2"], s["gate"], p, l, s["fo"], s["gpre"], s["up"], s["h1"], _row(sp["ln_ple_gate"][l]),
        _row(sp["ln_ffn_post"][l]), _row(sp["ln_ffn_pre"][l]), wa, wb, wp, 0, after)
    gb = _weight_grad(s["hn"], dgl, lax.empty((2 * D_MODEL, D_MODEL), BF16), 1)
    gp = _weight_grad(dpe, p, lax.empty((D_MODEL, PLE_DIM), BF16), 0, b_layer=l)
    ga = _weight_grad(s["act"], dfo, lax.empty((3 * D_FF, D_MODEL), BF16), 2)
    if not ride:
        ga = _weight_grad(dgp, s["f"], ga, 0)
        ga = _weight_grad(dup, s["f"], ga, 1)
    return (dh1, d, dgp, dup), ga, gp, gb


def _layer_bwd_middle(carry, s, sp, l, wb, gb, ga, after, ride):
    dh1, d, dgp, dup = carry
    dmix, dattn, dgm, d["ln_mix_post"], d["g_attn_out"], d["g_gm_out"] = _out_proj_bwd(
        dh1, s["mix"], s["attn"], s["gm"], _row(sp["ln_mix_post"][l]), _row(sp["g_attn_out"][l]),
        _row(sp["g_gm_out"][l]), wb, 0, after)
    gb = _weight_grad(s["heads"], dmix, gb, 0)
    dzu, dzv, d["gm_ws"], dbs, d["gm_ln_g"], d["gm_ln_b"], *rode = _gm_bwd(
        dgm, s["zu"], s["zv"], _row(sp["gm_ln_g"][l]), _row(sp["gm_ln_b"][l]), s["wtril"], s["bsx"],
        (dup, s["f"], ga, 1) if ride else None)
    d["gm_bs"] = dbs[:, :8].T
    return (dh1, dattn, dzu, dzv, d, dgp), gb, (rode[0] if ride else ga), _small_pack_gating(d)


def _layer_bwd_lower(carry, s, sp, l, wc, ga, after, ride):
    dh1, dattn, dzu, dzv, d, dgp = carry
    dq, dkv, dkf, dsink, *rode = _attn_bwd(s["q"], s["kv"], dattn, sp["attn_sinks"][l], after,
                                            (dgp, s["f"], ga, 0) if ride else None)
    ga = rode[0] if ride else ga
    d["attn_sinks"] = dsink[:, 0]
    dz, dh, d["ln_mix_pre"] = _in_proj_bwd(dq, dkv, dkf, dzu, dzv, s["h"], dh1, _row(sp["ln_mix_pre"][l]), wc, 0,
                                           F32 if l == 0 else BF16)
    gc = _weight_grad(dz, s["a"], lax.empty((D_IN, D_MODEL), BF16), 0)
    return dh, gc, ga, _small_pack_rest(d)


ANY = pl.BlockSpec(memory_space=pl.ANY)


HBM = pl.BlockSpec(memory_space=pltpu.HBM)
SEM = pl.BlockSpec(memory_space=pltpu.SEMAPHORE)
N_PEERS = N_DEV - 1


def _peers():
    x, y, c = lax.axis_index("x"), lax.axis_index("y"), lax.axis_index("c")
    peers = []
    for r in range(1, N_DEV):
        px = 1 - x if r & 4 else x
        py = 1 - y if r & 2 else y
        pc = 1 - c if r & 1 else c
        peers.append(((px, py, pc), 4 * px + 2 * py + pc))
    return 4 * x + 2 * y + c, peers


GATHER, SCATTER, SPREAD = "gather", "scatter", "spread"


def _peer_copy(src, land, send_sems, recv_sems, r, me, peer, peer_slot, mode):
    return pltpu.make_async_remote_copy(
        src_ref=src.at[:, pl.ds(peer_slot, 1)] if mode == SCATTER else src,
        dst_ref=land.at[:, pl.ds(me, 1)] if mode == GATHER else land.at[:, pl.ds(r - 1, 1)],
        send_sem=send_sems.at[r - 1], recv_sem=recv_sems.at[r - 1], device_id=peer, device_id_type=MESH)


def _peer_arrival(src, land, send_sems, recv_sems, r, me, peer, peer_slot, mode):
    return pltpu.make_async_remote_copy(
        src_ref=src.at[:, pl.ds(me, 1)] if mode == SCATTER else src,
        dst_ref=land.at[:, pl.ds(peer_slot, 1)] if mode == GATHER else land.at[:, pl.ds(r - 1, 1)],
        send_sem=send_sems.at[r - 1], recv_sem=recv_sems.at[r - 1], device_id=peer, device_id_type=MESH)


def _send_start(name, srcs, lands, modes):
    n = len(srcs)

    def body(*refs):
        src_refs, land_refs = refs[:n], refs[n:2 * n]
        outs = refs[2 * n:]
        send_sems, recv_sems, token = outs[2 * n:3 * n], outs[3 * n:4 * n], outs[4 * n]
        me, peers = _peers()
        for k in range(n):
            for r, (peer, slot) in enumerate(peers, 1):
                _peer_copy(src_refs[k], land_refs[k], send_sems[k], recv_sems[k], r, me, peer, slot, modes[k]).start()
        token[...] = jnp.zeros_like(token)

    hbm = lambda a: pltpu.HBM(a.shape, a.dtype)
    sems = [pltpu.SemaphoreType.DMA((N_PEERS,))] * (2 * n)
    outs = pl.pallas_call(
        body, name=name, in_specs=[HBM] * (2 * n),
        out_specs=[HBM] * (2 * n) + [SEM] * (2 * n) + [pl.BlockSpec(memory_space=pltpu.VMEM)],
        out_shape=[hbm(a) for a in srcs] + [hbm(a) for a in lands] + sems + [jax.ShapeDtypeStruct((8, 128), F32)],
        input_output_aliases={k: k for k in range(2 * n)},
        compiler_params=pltpu.CompilerParams(has_side_effects=pltpu.SideEffectType.DATAFLOW_SIDE_EFFECTING),
    )(*[pltpu.with_memory_space_constraint(a, pltpu.HBM) for a in list(srcs) + list(lands)])
    return dict(srcs=outs[:n], lands=outs[n:2 * n], send=outs[2 * n:3 * n], recv=outs[3 * n:4 * n],
                modes=list(modes)), outs[4 * n]


def _send_wait(name, sent, ks, after):
    n = len(ks)
    srcs = [sent["srcs"][k] for k in ks]
    lands = [sent["lands"][k] for k in ks]
    modes = [sent["modes"][k] for k in ks]

    def body(*refs):
        src_refs, land_refs = refs[:n], refs[n:2 * n]
        send_sems, recv_sems = refs[2 * n:3 * n], refs[3 * n:4 * n]
        me, peers = _peers()
        for k in range(n):
            for r, (peer, slot) in enumerate(peers, 1):
                args = (src_refs[k], land_refs[k], send_sems[k], recv_sems[k], r, me, peer, slot, modes[k])
                _peer_copy(*args).wait_send()
                _peer_arrival(*args).wait_recv()

    hbm = lambda a: pltpu.HBM(a.shape, a.dtype)
    outs = pl.pallas_call(
        body, name=name, in_specs=[HBM] * (2 * n) + [SEM] * (2 * n) + [ANY],
        out_specs=[HBM] * (2 * n), out_shape=[hbm(a) for a in srcs] + [hbm(a) for a in lands],
        input_output_aliases={k: k for k in range(2 * n)},
        compiler_params=pltpu.CompilerParams(has_side_effects=pltpu.SideEffectType.DATAFLOW_SIDE_EFFECTING),
    )(*srcs, *lands, *[sent["send"][k] for k in ks], *[sent["recv"][k] for k in ks], after)
    return outs[n:], outs[:n]


def _sum_blocks(own, land, ids):
    nseg, _, rows, cols = land.shape

    def body(ids_ref, own_ref, land_ref, o_ref):
        me = ids_ref[1]
        total = None
        for j in range(N_DEV):
            slot = jnp.maximum(jnp.bitwise_xor(me, j) - 1, 0)
            term = jnp.where(me == j, own_ref[...], land_ref[slot]).astype(F32)
            total = term if total is None else total + term
        o_ref[...] = total

    return pl.pallas_call(
        body, name="sum_blocks",
        grid_spec=pltpu.PrefetchScalarGridSpec(
            num_scalar_prefetch=1, grid=(nseg,),
            in_specs=[pl.BlockSpec((None, None, rows, cols), lambda s, ids: (s, ids[0], 0, 0)),
                      pl.BlockSpec((None, N_PEERS, rows, cols), lambda s, ids: (s, 0, 0, 0))],
            out_specs=pl.BlockSpec((None, rows, cols), lambda s, ids: (s, 0, 0))),
        out_shape=jax.ShapeDtypeStruct((nseg, rows, cols), F32),
        compiler_params=_params("parallel"),
    )(ids, own, land)


def _adamw(w, g, m, v):
    shape = w.shape
    cols = shape[-1]
    rows = w.size // cols
    tr = rows
    for cand in (512, 256, 128, 64, 32, 16, 8):
        if rows % cand == 0:
            tr = cand
            break
    c1 = 1.0 / (1.0 - ADAM_B1 ** ADAM_STEP)
    c2 = 1.0 / (1.0 - ADAM_B2 ** ADAM_STEP)

    def body(w_ref, g_ref, m_ref, v_ref, d_ref, nm_ref, nv_ref):
        g = g_ref[...]
        m = ADAM_B1 * m_ref[...] + (1.0 - ADAM_B1) * g
        v = ADAM_B2 * v_ref[...] + (1.0 - ADAM_B2) * (g * g)
        nm_ref[...] = m
        nv_ref[...] = v
        d_ref[...] = -ADAM_LR * ((m * c1) / (jnp.sqrt(v * c2) + ADAM_EPS) + ADAM_WD * w_ref[...])

    spec = pl.BlockSpec((tr, cols), lambda i: (i, 0))
    outs = pl.pallas_call(
        body, name="adamw", grid=(rows // tr,),
        in_specs=[spec] * 4, out_specs=[spec] * 3,
        out_shape=[jax.ShapeDtypeStruct((rows, cols), F32)] * 3,
        compiler_params=_params("parallel"),
    )(*[a.reshape(rows, cols) for a in (w, g, m, v)])
    return [o.reshape(shape) for o in outs]


SMALL = ("ln_mix_pre", "attn_sinks", "gm_ln_g", "gm_ln_b", "gm_ws", "gm_bs", "g_attn_out", "g_gm_out",
         "ln_mix_post", "ln_ffn_pre", "ln_ffn_post", "ln_ple_gate")
WEIGHTS = ("ln_mix_pre", "w_in", "attn_sinks", "gm_ln_g", "gm_ln_b", "gm_ws", "gm_bs", "g_attn_out", "g_gm_out",
           "w_out", "ln_mix_post", "ln_ffn_pre", "w_ffn_gate", "w_ffn_up", "w_ffn_down", "ln_ffn_post", "w_ple",
           "ln_ple_gate", "w_ple_gate")


def _pack_shards(w, l):
    sa = jnp.stack([w["w_ffn_gate"][l].T, w["w_ffn_up"][l].T, w["w_ffn_down"][l]])[:, None]
    sb = jnp.stack([w["w_out"][l], w["w_ple_gate"][l]])[:, None]
    return [w["w_in"][l].T[None, None].astype(BF16), sb.astype(BF16), w["w_ple"][l].T[None, None].astype(BF16),
            sa.astype(BF16)]


def kernel(x, p, ln_mix_pre, w_in, attn_sinks, gm_ln_g, gm_ln_b, gm_ws, gm_bs, g_attn_out, g_gm_out, w_out, ln_mix_post, ln_ffn_pre, w_ffn_gate, w_ffn_up, w_ffn_down, ln_ffn_post, w_ple, ln_ple_gate, w_ple_gate, loss_target, m_ln_mix_pre, m_w_in, m_attn_sinks, m_gm_ln_g, m_gm_ln_b, m_gm_ws, m_gm_bs, m_g_attn_out, m_g_gm_out, m_w_out, m_ln_mix_post, m_ln_ffn_pre, m_w_ffn_gate, m_w_ffn_up, m_w_ffn_down, m_ln_ffn_post, m_w_ple, m_ln_ple_gate, m_w_ple_gate, v_ln_mix_pre, v_w_in, v_attn_sinks, v_gm_ln_g, v_gm_ln_b, v_gm_ws, v_gm_bs, v_g_attn_out, v_g_gm_out, v_w_out, v_ln_mix_post, v_ln_ffn_pre, v_w_ffn_gate, v_w_ffn_up, v_w_ffn_down, v_ln_ffn_post, v_w_ple, v_ln_ple_gate, v_w_ple_gate):
    given = dict(locals())
    w = {n: given[n] for n in WEIGHTS}
    sp = {n: w[n] for n in SMALL}
    kinds = ("c", "b", "p", "a")

    me, _ = _peers()
    shards = [s for l in range(DEPTH) for s in _pack_shards(w, l)]
    lands = [lax.dynamic_update_slice(lax.empty((s.shape[0], N_DEV) + s.shape[2:], BF16), s, (0, me, 0, 0))
             for s in shards]
    gather, token = _send_start("gather_start", shards, lands, [GATHER] * len(shards))
    layer_weights = [{} for _ in range(DEPTH)]

    def weights_of(l):
        def get(kind, after):
            have = layer_weights[l]
            if kind not in have:
                if l < 2:
                    group = {"c": ("c",), "b": ("b", "p"), "p": ("b", "p"), "a": ("a",)}[kind]
                    after = token if (l == 0 and kind == "c") else after
                else:
                    group = kinds
                got, _ = _send_wait(f"gather_wait_{l}{group[0]}", gather, [4 * l + kinds.index(k) for k in group], after)
                for k, g in zip(group, got):
                    have[k] = g.reshape(-1, g.shape[-1])
            return have[kind]
        return get

    h = x[0]
    p3 = p.reshape(DEPTH, -1, PLE_DIM)
    saved = []
    for l in range(DEPTH):
        h, s = _layer_fwd(h, p3, sp, l, weights_of(l), loss_target[0] if l == DEPTH - 1 else None)
        saved.append(s)
    dh, sq = h

    started = []
    after = token
    view = lambda g, rows: g.reshape(-1, N_DEV, rows, g.shape[-1])
    pack16 = lambda s: s.astype(BF16)[None, None]
    packs = ("gating", "rest")

    def send(l, tag, items):
        bufs = list(items.values())
        lands = [lax.empty((a.shape[0], N_PEERS) + a.shape[2:], BF16) for a in bufs]
        sent, tok = _send_start(f"reduce_start_{l}{tag}", bufs, lands, [SPREAD if k in packs else SCATTER for k in items])
        started.append((l, tag, sent, list(items)))
        return tok

    for l in reversed(range(DEPTH)):
        lw = layer_weights[l]
        ride = l > 0
        carry, ga, gp, gb = _layer_bwd_upper(dh, saved[l], p3, sp, l, lw["a"], lw["b"], lw["p"], after, ride)
        if not ride:
            after = send(l, "a", {"a": view(ga, ROWS_A)})
        carry, gb, ga, gating = _layer_bwd_middle(carry, saved[l], sp, l, lw["b"], gb, ga, after, ride)
        after = send(l, "b", {"b": view(gb, ROWS_B), "gating": pack16(gating), "p": view(gp, ROWS_B)})
        dh, gc, ga, rest = _layer_bwd_lower(carry, saved[l], sp, l, lw["c"], ga, after, ride)
        after = send(l, "c", {"c": view(gc, ROWS_C), "rest": pack16(rest), **({"a": view(ga, ROWS_A)} if ride else {})})

    mine = jnp.stack([me, me]).astype(jnp.int32)
    whole = jnp.stack([jnp.zeros_like(me), me]).astype(jnp.int32)
    sums = {k: [None] * DEPTH for k in ("a", "p", "b", "c", "gating", "rest")}

    def collect(group, behind):
        l, tag, sent, keys = group
        lands, srcs = _send_wait(f"reduce_wait_{l}{tag}", sent, list(range(len(keys))), behind)
        for key, land, src in zip(keys, lands, srcs):
            sums[key][l] = _sum_blocks(src, land, whole)[0] if key in packs else _sum_blocks(src, land, mine)

    for group in started[:-1]:
        collect(group, dh)
    grad_x = dh
    loss = lax.psum(sq[0, 0] * (0.5 / D_MODEL), AXES)
    grads, delta, new_m, new_v = {}, {}, {}, {}

    def update(names):
        for n in names:
            delta[n], new_m[n], new_v[n] = _adamw(w[n], grads[n], given["m_" + n], given["v_" + n])

    stack = lambda f, xs: jnp.stack([f(x) for x in xs])
    grads.update({
        "w_ffn_gate": stack(lambda r: r[0].T, sums["a"]), "w_ffn_up": stack(lambda r: r[1].T, sums["a"]),
        "w_ffn_down": stack(lambda r: r[2], sums["a"]), "w_ple": stack(lambda r: r[0].T, sums["p"]),
        "w_out": stack(lambda r: r[0], sums["b"]), "w_ple_gate": stack(lambda r: r[1], sums["b"])})
    grads.update(_unpack_gating(jnp.stack(sums["gating"])))
    early = tuple(grads)
    update(early)
    collect(started[-1], jnp.concatenate([delta[n].reshape(-1)[:1] for n in early]))
    grads["w_in"] = stack(lambda r: r[0].T, sums["c"])
    grads.update(_unpack_rest(jnp.stack(sums["rest"])))
    update([n for n in grads if n not in early])
    return (loss, grad_x[None], *[grads[n] for n in WEIGHTS], *[delta[n] for n in WEIGHTS],
            *[new_m[n] for n in WEIGHTS], *[new_v[n] for n in WEIGHTS])
```

```python
import math

import jax
import jax.numpy as jnp
from jax import lax
from jax.experimental import pallas as pl
from jax.experimental.pallas import tpu as pltpu

F32 = jnp.float32
BF16 = jnp.bfloat16
MESH = pl.DeviceIdType.MESH
AXES = ("x", "y", "c")

D_MODEL = 1024
DEPTH = 4
N_DEV = 8
HEAD_DIM = 64
ATTN_W = 512
KV_W = 128
GM_W = 512
D_IN = 1792
D_FF = 2816
PLE_DIM = 256
BLK = 128
FF_CHUNK = 256
WGRAD_TOKENS = 1024
NORM_EPS = 1e-6
NEG_BIG = -1e30
ATTN_SCALE = HEAD_DIM ** -0.5

ADAM_LR = 0.001
ADAM_B1 = 0.9
ADAM_B2 = 0.999
ADAM_EPS = 1e-08
ADAM_WD = 0.01
ADAM_STEP = 10

ROWS_A = D_FF // N_DEV
ROWS_B = D_MODEL // N_DEV
ROWS_C = D_IN // N_DEV
GATING_ROWS = 144
REST_ROWS = 56

VMEM_LIMIT = 56 * 2 ** 20


def _params(*sem):
    return pltpu.CompilerParams(dimension_semantics=sem, vmem_limit_bytes=VMEM_LIMIT)


def _dot(a, b):
    return jnp.dot(a, b, preferred_element_type=F32)


def _dot_nt(a, b):
    return lax.dot_general(a, b, (((1,), (1,)), ((), ())), preferred_element_type=F32)


def _dot_tn(a, b):
    return lax.dot_general(a, b, (((0,), (0,)), ((), ())), preferred_element_type=F32)


def _rms_fwd(x, g):
    r = lax.rsqrt(jnp.mean(x * x, axis=-1, keepdims=True) + NORM_EPS)
    return x * r * g


def _rms_bwd(x, g, dy):
    r = lax.rsqrt(jnp.mean(x * x, axis=-1, keepdims=True) + NORM_EPS)
    xh = x * r
    dg = jnp.sum(dy * xh, axis=0, keepdims=True)
    dxh = dy * g
    dx = r * (dxh - xh * jnp.mean(dxh * xh, axis=-1, keepdims=True))
    return dx, dg


_GELU_C = math.sqrt(2.0 / math.pi)


def _gelu(x):
    t = jnp.tanh(_GELU_C * (x + 0.044715 * (x * x * x)))
    return 0.5 * x * (1.0 + t)


def _gelu_grad(x):
    x2 = x * x
    t = jnp.tanh(_GELU_C * (x + 0.044715 * (x2 * x)))
    return 0.5 * (1.0 + t) + 0.5 * x * (1.0 - t * t) * (_GELU_C * (1.0 + 3.0 * 0.044715 * x2))


def _sigmoid(x):
    return 1.0 / (1.0 + jnp.exp(-x))


def _row_spec(tm, n):
    return pl.BlockSpec((tm, n), lambda i: (i, 0))


def _layer_row_spec(tm, n, l):
    return pl.BlockSpec((None, tm, n), lambda i: (l, i, 0))


def _vec_spec(n):
    return pl.BlockSpec((1, n), lambda i: (0, 0))


def _seg_spec(rows, cols, seg):
    return pl.BlockSpec((N_DEV * rows, cols), lambda i: (seg, 0), pipeline_mode=pl.Buffered(1))


def _zero_at(first, *refs):
    @pl.when(first)
    def _():
        for r in refs:
            r[...] = jnp.zeros(r.shape, r.dtype)


def _tile(t, want):
    return min(t, want)


def _in_proj(h, g, wc, layer):
    t = h.shape[0]
    tm = _tile(t, 512)

    def body(h_ref, g_ref, w_ref, a_ref, q_ref, kv_ref, zu_ref, zv_ref):
        a = _rms_fwd(h_ref[...], g_ref[...]).astype(BF16)
        a_ref[...] = a
        q_ref[...] = _dot_nt(a, w_ref[0:512, :]).astype(BF16)
        kv_ref[...] = _dot_nt(a, w_ref[512:768, :]).astype(BF16)
        zu_ref[...] = _dot_nt(a, w_ref[768:1280, :])
        zv_ref[...] = _dot_nt(a, w_ref[1280:1792, :])

    return pl.pallas_call(
        body, name="in_proj", grid=(t // tm,),
        in_specs=[_row_spec(tm, D_MODEL), _vec_spec(D_MODEL), _seg_spec(ROWS_C, D_MODEL, layer)],
        out_specs=[_row_spec(tm, D_MODEL), _row_spec(tm, ATTN_W), _row_spec(tm, 2 * KV_W),
                   _row_spec(tm, GM_W), _row_spec(tm, GM_W)],
        out_shape=[jax.ShapeDtypeStruct((t, D_MODEL), BF16), jax.ShapeDtypeStruct((t, ATTN_W), BF16),
                   jax.ShapeDtypeStruct((t, 2 * KV_W), BF16), jax.ShapeDtypeStruct((t, GM_W), F32),
                   jax.ShapeDtypeStruct((t, GM_W), F32)],
        compiler_params=_params("parallel"),
    )(h, g, wc)


def _head_variants(x, low):
    xr = pltpu.roll(x, 64, axis=1)
    zero = jnp.zeros_like(x)
    return {
        (0, 0): jnp.where(low, x, zero).astype(BF16),
        (0, 1): jnp.where(low, zero, xr).astype(BF16),
        (1, 0): jnp.where(low, xr, zero).astype(BF16),
        (1, 1): jnp.where(low, zero, x).astype(BF16),
    }


def _attn_masks(i):
    row = lax.broadcasted_iota(jnp.int32, (BLK, BLK), 0)
    lane = lax.broadcasted_iota(jnp.int32, (BLK, BLK), 1)
    vcur = row >= lane
    dist = jnp.where(vcur, row - lane, row - lane + BLK).astype(F32)
    valid = jnp.logical_or(vcur, i > 0)
    return lane < 64, vcur, dist, valid


def _head_key(h):
    return (h // 4, h % 2)


def _stack_kv(prev, cur, g):
    return jnp.concatenate([prev[(g, 0)], cur[(g, 0)], prev[(g, 1)], cur[(g, 1)]], axis=0)


def _split_cols(p, vcur):
    return [jnp.where(vcur, 0.0, p).astype(BF16), jnp.where(vcur, p, 0.0).astype(BF16)]


def _attn_scores(q_ref, rows, stacked, vcur):
    out = []
    for col in range(4):
        big = _dot_nt(q_ref[rows, col * 128:(col + 1) * 128], stacked[col // 2])
        for half in range(2):
            out.append(jnp.where(vcur, big[:, half * 256 + 128:half * 256 + 256], big[:, half * 256:half * 256 + 128]))
    return out


def _attn_scores_by_head(q_ref, rows, kc, kp, vcur):
    out = []
    for h in range(8):
        qh = q_ref[rows, (h // 2) * 128:(h // 2 + 1) * 128]
        out.append(jnp.where(vcur, _dot_nt(qh, kc[_head_key(h)]), _dot_nt(qh, kp[_head_key(h)])))
    return out


def _attn_probs(s, h, sink, dist, valid):
    s = s * ATTN_SCALE - (2.0 ** -(h + 1)) * dist
    if valid is not None:
        s = jnp.where(valid, s, NEG_BIG)
    m = jnp.maximum(jnp.max(s, axis=1, keepdims=True), sink)
    e = jnp.exp(s - m)
    es = jnp.exp(sink - m)
    inv = 1.0 / (jnp.sum(e, axis=1, keepdims=True) + es)
    return e * inv, es * inv


def _kv_prev_spec(blocks):
    return pl.BlockSpec((BLK, 2 * KV_W), lambda i: (jnp.maximum(i * blocks - 1, 0), 0))


def _kv_variants(kv_ref, rows, low):
    return (_head_variants(kv_ref[rows, 0:128].astype(F32), low), _head_variants(kv_ref[rows, 128:256].astype(F32), low))


def _gm_forward_block(zu, zv, lng, lnb, w_ref, bsx, low):
    gu = _gelu(zu)
    gv = _gelu(zv)
    mu = jnp.mean(gv, axis=-1, keepdims=True)
    xc = gv - mu
    rstd = lax.rsqrt(jnp.mean(xc * xc, axis=-1, keepdims=True) + NORM_EPS)
    xn = xc * rstd
    ln = xn * lng + lnb
    mixed = []
    for col in range(4):
        lc = ln[:, col * 128:(col + 1) * 128]
        lo = jnp.where(low, lc, 0.0).astype(BF16)
        hi = jnp.where(low, 0.0, lc).astype(BF16)
        mixed.append(_dot(w_ref[2 * col], lo) + _dot(w_ref[2 * col + 1], hi) + bsx[:, col * 128:(col + 1) * 128])
    return gu, ln, xn, rstd, mixed


def _mix_fwd(q, kv, sinks, zu, zv, lng, lnb, wtril, bsx, h, ga, gg, gpost, wb, layer):
    t = q.shape[0]
    tq = _tile(t, 512)
    blocks = tq // BLK

    def body(sink_ref, q_ref, kvc_ref, kvp_ref, zu_ref, zv_ref, g_ref, b_ref, w_ref, bs_ref, h_ref, ga_ref, gg_ref,
             gp_ref, wo_ref, attn_ref, gm_ref, heads_ref, mix_ref, h1_ref):
        low, vcur, dist, valid = _attn_masks(pl.program_id(0))
        kp, vp = _kv_variants(kvp_ref, slice(None), low)

        def project(rows, mix):
            mix_ref[rows, :] = mix.astype(BF16)
            h1_ref[rows, :] = h_ref[rows, :] + _rms_fwd(mix, gp_ref[...])

        pending = None
        for b in range(blocks):
            rows = slice(b * BLK, (b + 1) * BLK)
            kc, vc = _kv_variants(kvc_ref, rows, low)
            ks = [_stack_kv(kp, kc, g) for g in range(2)]
            vs = [_stack_kv(vp, vc, g) for g in range(2)]
            scores = _attn_scores(q_ref, rows, ks, vcur)
            if pending is not None:
                project(pending[0], _dot(pending[1], wo_ref[...]))
            gu, _, _, _, mixed = _gm_forward_block(zu_ref[rows, :], zv_ref[rows, :], g_ref[...], b_ref[...], w_ref,
                                                   bs_ref[...], low)
            probs = [_attn_probs(scores[h], h, sink_ref[h], dist, valid if b == 0 else None)[0] for h in range(8)]
            attn_cols, gm_cols = [], []
            for col in range(4):
                gm_cols.append((gu[:, col * 128:(col + 1) * 128] * mixed[col]).astype(BF16))
                p_col = jnp.concatenate(_split_cols(probs[2 * col], vcur) + _split_cols(probs[2 * col + 1], vcur), axis=1)
                attn_cols.append(_dot(p_col, vs[col // 2]).astype(BF16))
            attn = jnp.concatenate(attn_cols, axis=1)
            gm = jnp.concatenate(gm_cols, axis=1)
            attn_ref[rows, :] = attn
            gm_ref[rows, :] = gm
            heads = jnp.concatenate([_rms_fwd(attn.astype(F32), ga_ref[...]).astype(BF16),
                                     _rms_fwd(gm.astype(F32), gg_ref[...]).astype(BF16)], axis=1)
            heads_ref[rows, :] = heads
            pending = (rows, heads)
            kp, vp = kc, vc
        project(pending[0], _dot(pending[1], wo_ref[...]))

    wide = _row_spec(tq, GM_W)
    row = _row_spec(tq, D_MODEL)
    return pl.pallas_call(
        body, name="mix_fwd", grid=(t // tq,),
        in_specs=[pl.BlockSpec(memory_space=pltpu.SMEM), _row_spec(tq, ATTN_W), _row_spec(tq, 2 * KV_W),
                  _kv_prev_spec(blocks), wide, wide, _vec_spec(GM_W), _vec_spec(GM_W),
                  pl.BlockSpec((8, BLK, BLK), lambda i: (0, 0, 0)), pl.BlockSpec((BLK, GM_W), lambda i: (0, 0)),
                  row, _vec_spec(ATTN_W), _vec_spec(GM_W), _vec_spec(D_MODEL), _seg_spec(ROWS_B, D_MODEL, 2 * layer)],
        out_specs=[_row_spec(tq, ATTN_W), wide, row, row, row],
        out_shape=[jax.ShapeDtypeStruct((t, ATTN_W), BF16), jax.ShapeDtypeStruct((t, GM_W), BF16),
                   jax.ShapeDtypeStruct((t, D_MODEL), BF16), jax.ShapeDtypeStruct((t, D_MODEL), BF16),
                   jax.ShapeDtypeStruct((t, D_MODEL), F32)],
        compiler_params=_params("parallel"),
    )(sinks, q, kv, kv, zu, zv, lng, lnb, wtril, bsx, h, ga, gg, gpost, wb)


def _ffn_fwd(h1, p, p_layer, gpre, gpost, gple, wa, wb, wp, layer, target=None):
    t = h1.shape[0]
    tm = _tile(t, 256)

    def body(h_ref, p_ref, gpre_ref, gpost_ref, gple_ref, wg_ref, wu_ref, wd_ref, wpg_ref, wpl_ref, *rest):
        if target is None:
            f_ref, gp_ref, up_ref, act_ref, fo_ref, h2_ref, hn_ref, gate_ref, h3_ref = rest
        else:
            t_ref, f_ref, gp_ref, up_ref, act_ref, fo_ref, h2_ref, hn_ref, gate_ref, dy_ref, l_ref = rest
            _zero_at(pl.program_id(0) == 0, l_ref)
        h = h_ref[...]
        pe = _dot_nt(p_ref[...].astype(BF16), wpl_ref[...])
        f = _rms_fwd(h, gpre_ref[...]).astype(BF16)
        f_ref[...] = f
        chunks = [slice(j * FF_CHUNK, (j + 1) * FF_CHUNK) for j in range(D_FF // FF_CHUNK)]
        fo = None
        gp, up = _dot_nt(f, wg_ref[chunks[0], :]), _dot_nt(f, wu_ref[chunks[0], :])
        for j, cols in enumerate(chunks):
            if j + 1 < len(chunks):
                gp_next, up_next = _dot_nt(f, wg_ref[chunks[j + 1], :]), _dot_nt(f, wu_ref[chunks[j + 1], :])
            act = (gp * _sigmoid(gp) * up).astype(BF16)
            gp_ref[:, cols] = gp.astype(BF16)
            up_ref[:, cols] = up.astype(BF16)
            act_ref[:, cols] = act
            part = _dot(act, wd_ref[cols, :])
            fo = part if fo is None else fo + part
            if j + 1 < len(chunks):
                gp, up = gp_next, up_next
        fo_ref[...] = fo
        h2 = h + _rms_fwd(fo, gpost_ref[...])
        h2_ref[...] = h2
        hn = _rms_fwd(h2, gple_ref[...]).astype(BF16)
        hn_ref[...] = hn
        gate = _sigmoid(_dot(hn, wpg_ref[...]))
        gate_ref[...] = gate.astype(BF16)
        h3 = h2 + pe * gate
        if target is None:
            h3_ref[...] = h3
        else:
            e = h3 - t_ref[...]
            dy_ref[...] = (e * (1.0 / D_MODEL)).astype(BF16)
            s = jnp.sum(jnp.sum(e * e, axis=1, keepdims=True), axis=0, keepdims=True)
            l_ref[...] += jnp.broadcast_to(s, (1, 128))

    wide = _row_spec(tm, D_FF)
    row = _row_spec(tm, D_MODEL)
    vec = _vec_spec(D_MODEL)
    last = target is not None
    return pl.pallas_call(
        body, name="ffn_loss" if last else "ffn_fwd", grid=(t // tm,),
        in_specs=[row, _layer_row_spec(tm, PLE_DIM, p_layer), vec, vec, vec, _seg_spec(ROWS_A, D_MODEL, 3 * layer),
                  _seg_spec(ROWS_A, D_MODEL, 3 * layer + 1), _seg_spec(ROWS_A, D_MODEL, 3 * layer + 2),
                  _seg_spec(ROWS_B, D_MODEL, 2 * layer + 1), _seg_spec(ROWS_B, PLE_DIM, layer)] + [row] * last,
        out_specs=[row, wide, wide, wide, row, row, row, row, row] + [_vec_spec(128)] * last,
        out_shape=[jax.ShapeDtypeStruct((t, D_MODEL), BF16)] + [jax.ShapeDtypeStruct((t, D_FF), BF16)] * 3
        + [jax.ShapeDtypeStruct((t, D_MODEL), F32)] * 2 + [jax.ShapeDtypeStruct((t, D_MODEL), BF16)] * 2
        + [jax.ShapeDtypeStruct((t, D_MODEL), BF16 if last else F32)] + [jax.ShapeDtypeStruct((1, 128), F32)] * last,
        compiler_params=_params("arbitrary" if last else "parallel"),
    )(h1, p, gpre, gpost, gple, wa, wa, wa, wb, wp, *([target] if last else []))


def _ffn_bwd(dh3, h2, gate, p, p_layer, fo, gp, up, h1, gple, gpost, gpre, wa, wb, wp, layer, after):
    t = dh3.shape[0]
    tm = _tile(t, 256)

    def body(d3_ref, h2_ref, gate_ref, p_ref, fo_ref, gp_ref, up_ref, h_ref, gple_ref, gpost_ref, gpre_ref,
             wg_ref, wu_ref, wd_ref, wpg_ref, wpl_ref, after_ref,
             dgl_ref, dpe_ref, dfo_ref, dgp_ref, dup_ref, dh1_ref, dgple_ref, dgpost_ref, dgpre_ref):
        _zero_at(pl.program_id(0) == 0, dgple_ref, dgpost_ref, dgpre_ref)
        d3 = d3_ref[...].astype(F32)
        gate = gate_ref[...].astype(F32)
        pe = _dot_nt(p_ref[...].astype(BF16), wpl_ref[...])
        dpe_ref[...] = (d3 * gate).astype(BF16)
        dgl = (d3 * pe * gate * (1.0 - gate)).astype(BF16)
        dgl_ref[...] = dgl
        dx2, dgple = _rms_bwd(h2_ref[...], gple_ref[...], _dot_nt(dgl, wpg_ref[...]))
        dgple_ref[...] += dgple
        d = d3 + dx2
        dfo, dgpost = _rms_bwd(fo_ref[...], gpost_ref[...], d)
        dfo = dfo.astype(BF16)
        dfo_ref[...] = dfo
        dgpost_ref[...] += dgpost
        chunks = [slice(j * FF_CHUNK, (j + 1) * FF_CHUNK) for j in range(D_FF // FF_CHUNK)]
        df = None
        dact = _dot_nt(dfo, wd_ref[chunks[0], :])
        for j, cols in enumerate(chunks):
            if j + 1 < len(chunks):
                dact_next = _dot_nt(dfo, wd_ref[chunks[j + 1], :])
            gp = gp_ref[:, cols].astype(F32)
            sg = _sigmoid(gp)
            dgp = (dact * up_ref[:, cols].astype(F32) * (sg * (1.0 + gp * (1.0 - sg)))).astype(BF16)
            dup = (dact * (gp * sg)).astype(BF16)
            dgp_ref[:, cols] = dgp
            dup_ref[:, cols] = dup
            part = _dot(dgp, wg_ref[cols, :]) + _dot(dup, wu_ref[cols, :])
            df = part if df is None else df + part
            if j + 1 < len(chunks):
                dact = dact_next
        dx, dgpre = _rms_bwd(h_ref[...], gpre_ref[...], df)
        dh1_ref[...] = (d + dx).astype(BF16)
        dgpre_ref[...] += dgpre

    wide = _row_spec(tm, D_FF)
    row = _row_spec(tm, D_MODEL)
    vec = _vec_spec(D_MODEL)
    narrow = jax.ShapeDtypeStruct((t, D_MODEL), BF16)
    return pl.pallas_call(
        body, name="ffn_bwd", grid=(t // tm,),
        in_specs=[row, row, row, _layer_row_spec(tm, PLE_DIM, p_layer), row, wide, wide, row, vec, vec, vec,
                  _seg_spec(ROWS_A, D_MODEL, 3 * layer), _seg_spec(ROWS_A, D_MODEL, 3 * layer + 1),
                  _seg_spec(ROWS_A, D_MODEL, 3 * layer + 2), _seg_spec(ROWS_B, D_MODEL, 2 * layer + 1),
                  _seg_spec(ROWS_B, PLE_DIM, layer), pl.BlockSpec(memory_space=pl.ANY)],
        out_specs=[row, row, row, wide, wide, row, vec, vec, vec],
        out_shape=[narrow, narrow, narrow, jax.ShapeDtypeStruct((t, D_FF), BF16), jax.ShapeDtypeStruct((t, D_FF), BF16),
                   narrow] + [jax.ShapeDtypeStruct((1, D_MODEL), F32)] * 3,
        compiler_params=_params("arbitrary"),
    )(dh3, h2, gate, p, fo, gp, up, h1, gple, gpost, gpre, wa, wa, wa, wb, wp, after)


def _out_proj_bwd(dh1, mix, attn, gm, gpost, ga, gg, wb, layer, after):
    t = dh1.shape[0]
    tm = _tile(t, 512)

    def body(d_ref, mix_ref, a_ref, m_ref, gp_ref, ga_ref, gg_ref, w_ref, after_ref,
             dmix_ref, da_ref, dm_ref, dgp_ref, dga_ref, dgg_ref):
        _zero_at(pl.program_id(0) == 0, dgp_ref, dga_ref, dgg_ref)
        dmix, dgp = _rms_bwd(mix_ref[...].astype(F32), gp_ref[...], d_ref[...].astype(F32))
        dmix = dmix.astype(BF16)
        dmix_ref[...] = dmix
        da, dga = _rms_bwd(a_ref[...].astype(F32), ga_ref[...], _dot_nt(dmix, w_ref[0:512, :]))
        dm, dgg = _rms_bwd(m_ref[...].astype(F32), gg_ref[...], _dot_nt(dmix, w_ref[512:1024, :]))
        da_ref[...] = da.astype(BF16)
        dm_ref[...] = dm
        dgp_ref[...] += dgp
        dga_ref[...] += dga
        dgg_ref[...] += dgg

    return pl.pallas_call(
        body, name="out_proj_bwd", grid=(t // tm,),
        in_specs=[_row_spec(tm, D_MODEL), _row_spec(tm, D_MODEL), _row_spec(tm, ATTN_W), _row_spec(tm, GM_W),
                  _vec_spec(D_MODEL), _vec_spec(ATTN_W), _vec_spec(GM_W), _seg_spec(ROWS_B, D_MODEL, 2 * layer),
                  pl.BlockSpec(memory_space=pl.ANY)],
        out_specs=[_row_spec(tm, D_MODEL), _row_spec(tm, ATTN_W), _row_spec(tm, GM_W),
                   _vec_spec(D_MODEL), _vec_spec(ATTN_W), _vec_spec(GM_W)],
        out_shape=[jax.ShapeDtypeStruct((t, D_MODEL), BF16), jax.ShapeDtypeStruct((t, ATTN_W), BF16),
                   jax.ShapeDtypeStruct((t, GM_W), F32), jax.ShapeDtypeStruct((1, D_MODEL), F32),
                   jax.ShapeDtypeStruct((1, ATTN_W), F32), jax.ShapeDtypeStruct((1, GM_W), F32)],
        compiler_params=_params("arbitrary"),
    )(dh1, mix, attn, gm, gpost, ga, gg, wb, after)


def _split3(x):
    hi = x.astype(BF16)
    r1 = x - hi.astype(F32)
    mid = r1.astype(BF16)
    lo = (r1 - mid.astype(F32)).astype(BF16)
    return hi, mid, lo


def _rider_pieces(rider, blocks):
    rm = rider[0].shape[1]
    per = -(-rm // (blocks * 256)) * 256
    return [slice(k * per, min((k + 1) * per, rm)) for k in range(blocks) if k * per < rm]


def _gm_bwd(dgm, zu, zv, lng, lnb, wtril, bsx, rider=None):
    t = zu.shape[0]
    tm = _tile(t, 512)
    nb = t // tm
    if rider is not None:
        ra, rb, rbuf, rseg = rider
        rm, rn = ra.shape[1], rb.shape[1]
        pieces = _rider_pieces(rider, tm // BLK)

    def body(d_ref, zu_ref, zv_ref, g_ref, b_ref, w_ref, bs_ref, *rest):
        if rider is None:
            dzu_ref, dzv_ref, dw_ref, dbs_ref, dlg_ref, dlb_ref, dbsx_ref = rest
        else:
            ra_ref, rb_ref, rbuf_ref, dzu_ref, dzv_ref, dw_ref, dbs_ref, dlg_ref, dlb_ref, ro_ref, dbsx_ref, acc_ref = rest
        i = pl.program_id(0)
        _zero_at(i == 0, dw_ref, dlg_ref, dlb_ref, dbsx_ref)
        if rider is not None:
            _zero_at(i == 0, acc_ref)
            rb16 = rb_ref[...].astype(BF16)
        row = lax.broadcasted_iota(jnp.int32, (BLK, BLK), 0)
        lane = lax.broadcasted_iota(jnp.int32, (BLK, BLK), 1)
        low = lane < 64
        tril = row >= lane
        lng = g_ref[...]
        for b in range(tm // BLK):
            rows = slice(b * BLK, (b + 1) * BLK)
            if rider is not None and b < len(pieces):
                acc_ref[pieces[b], :] += _dot_tn(ra_ref[:, pieces[b]], rb16)
            zu = zu_ref[rows, :]
            zv = zv_ref[rows, :]
            gu, ln, xn, rstd, mixed = _gm_forward_block(zu, zv, lng, b_ref[...], w_ref, bs_ref[...], low)
            dgm = d_ref[rows, :]
            dgu_cols, dmx_cols, dln_cols = [], [], []
            for col in range(4):
                sl = slice(col * 128, (col + 1) * 128)
                dg = dgm[:, sl]
                dgu_cols.append(dg * mixed[col])
                dmx = dg * gu[:, sl]
                dmx_cols.append(dmx)
                lc = ln[:, sl]
                halves = (jnp.where(low, lc, 0.0).astype(BF16), jnp.where(low, 0.0, lc).astype(BF16))
                dmx16 = dmx.astype(BF16)
                dmx_half = (jnp.where(low, dmx, 0.0).astype(BF16), jnp.where(low, 0.0, dmx).astype(BF16))
                dln = None
                for half in range(2):
                    hd = 2 * col + half
                    dw_ref[hd] += jnp.where(tril, _dot_nt(dmx16, halves[half]), 0.0)
                    part = _dot_tn(w_ref[hd], dmx_half[half])
                    dln = part if dln is None else dln + part
                dln_cols.append(dln)
            dgu = jnp.concatenate(dgu_cols, axis=1)
            dmx = jnp.concatenate(dmx_cols, axis=1)
            dln = jnp.concatenate(dln_cols, axis=1)
            dzu_ref[rows, :] = (dgu * _gelu_grad(zu)).astype(BF16)
            dbsx_ref[...] += dmx
            dlg_ref[...] += jnp.sum(dln * xn, axis=0, keepdims=True)
            dlb_ref[...] += jnp.sum(dln, axis=0, keepdims=True)
            dxn = dln * lng
            dgv = rstd * (dxn - jnp.mean(dxn, axis=-1, keepdims=True) - xn * jnp.mean(dxn * xn, axis=-1, keepdims=True))
            dzv_ref[rows, :] = (dgv * _gelu_grad(zv)).astype(BF16)

        @pl.when(i == nb - 1)
        def _():
            r = lax.broadcasted_iota(jnp.int32, (GM_W, BLK), 0)
            c = lax.broadcasted_iota(jnp.int32, (GM_W, BLK), 1)
            e = jnp.where(jnp.logical_and(r >= c * 64, r < c * 64 + 64), 1.0, 0.0).astype(BF16)
            hi, mid, lo = _split3(dbsx_ref[...])
            dbs_ref[...] = _dot(hi, e) + _dot(mid, e) + _dot(lo, e)
            if rider is not None:
                ro_ref[...] = acc_ref[...].astype(ro_ref.dtype)

    vec = _vec_spec(GM_W)
    in_specs = [_row_spec(tm, GM_W)] * 3 + [vec, vec, pl.BlockSpec((8, BLK, BLK), lambda i: (0, 0, 0)),
                                            pl.BlockSpec((BLK, GM_W), lambda i: (0, 0))]
    out_specs = [_row_spec(tm, GM_W), _row_spec(tm, GM_W), pl.BlockSpec((8, BLK, BLK), lambda i: (0, 0, 0)),
                 pl.BlockSpec((BLK, BLK), lambda i: (0, 0)), vec, vec]
    out_shape = [jax.ShapeDtypeStruct((t, GM_W), BF16), jax.ShapeDtypeStruct((t, GM_W), BF16),
                 jax.ShapeDtypeStruct((8, BLK, BLK), F32), jax.ShapeDtypeStruct((BLK, BLK), F32),
                 jax.ShapeDtypeStruct((1, GM_W), F32), jax.ShapeDtypeStruct((1, GM_W), F32)]
    scratch = [pltpu.VMEM((BLK, GM_W), F32)]
    operands = [dgm, zu, zv, lng, lnb, wtril, bsx]
    extra = {}
    if rider is not None:
        in_specs += [_row_spec(tm, rm), _row_spec(tm, rn), pl.BlockSpec(memory_space=pl.ANY)]
        out_specs.append(pl.BlockSpec((rm, rn), lambda i: (rseg, 0)))
        out_shape.append(jax.ShapeDtypeStruct(rbuf.shape, rbuf.dtype))
        scratch.append(pltpu.VMEM((rm, rn), F32))
        operands += [ra, rb, rbuf]
        extra = dict(input_output_aliases={9: 6})
    return pl.pallas_call(
        body, name="gm_bwd" if rider is None else "gm_bwd_rider", grid=(nb,),
        in_specs=in_specs, out_specs=out_specs, out_shape=out_shape, scratch_shapes=scratch,
        compiler_params=_params("arbitrary"), **extra,
    )(*operands)


def _attn_bwd(q, kv, do, sinks, after, rider=None):
    t = q.shape[0]
    tq = _tile(t, 512)
    blocks = tq // BLK
    steps = t // tq
    if rider is not None:
        ra, rb, rbuf, rseg = rider
        rm, rn = ra.shape[1], rb.shape[1]
        per = -(-rm // (blocks * 256)) * 256
        pieces = [slice(k * per, min((k + 1) * per, rm)) for k in range(blocks)]

    def body(sink_ref, q_ref, kvc_ref, kvp_ref, do_ref, after_ref, *rest):
        if rider is None:
            dq_ref, dkv_ref, dkf_ref, ds_ref = rest
        else:
            ra_ref, rb_ref, rbuf_ref, dq_ref, dkv_ref, dkf_ref, ds_ref, ro_ref, acc_ref = rest
        i = pl.program_id(0)
        _zero_at(i == 0, ds_ref)
        if rider is not None:
            _zero_at(i == 0, acc_ref)
            rb16 = rb_ref[...].astype(BF16)
        low, vcur, dist, valid = _attn_masks(i)
        head_row = lax.broadcasted_iota(jnp.int32, (8, 128), 0)
        dsink_tile = jnp.zeros((8, 128), F32)
        kp, vp = _kv_variants(kvp_ref, slice(None), low)
        own = None
        for b in range(blocks):
            rows = slice(b * BLK, (b + 1) * BLK)
            kc, vc = _kv_variants(kvc_ref, rows, low)
            scores = _attn_scores_by_head(q_ref, rows, kc, kp, vcur)
            dprobs = _attn_scores_by_head(do_ref, rows, vc, vp, vcur)
            if rider is not None and pieces[b].start < rm:
                acc_ref[pieces[b], :] += _dot_tn(ra_ref[:, pieces[b]], rb16)
            parts = []
            for h in range(8):
                p, ps = _attn_probs(scores[h], h, sink_ref[h], dist, valid if b == 0 else None)
                delta = jnp.sum(p * dprobs[h], axis=1, keepdims=True)
                ds = p * (dprobs[h] - delta) * ATTN_SCALE
                dsink = jnp.sum(-ps * delta, axis=0, keepdims=True)
                dsink_tile = jnp.where(head_row == h, dsink_tile + dsink, dsink_tile)
                parts.append(_split_cols(ds, vcur) + _split_cols(p, vcur))
            acc = {}

            def add(name, key, val):
                acc[(name, key)] = val if (name, key) not in acc else acc[(name, key)] + val

            for col in range(4):
                qh = q_ref[rows, col * 128:(col + 1) * 128]
                doh = do_ref[rows, col * 128:(col + 1) * 128]
                dq = None
                for half in range(2):
                    key = _head_key(2 * col + half)
                    dsp, dsc, pp, pc = parts[2 * col + half]
                    part = _dot(dsc, kc[key]) + _dot(dsp, kp[key])
                    dq = part if dq is None else dq + part
                    add("kc", key, _dot_tn(dsc, qh))
                    add("kp", key, _dot_tn(dsp, qh))
                    add("vc", key, _dot_tn(pc, doh))
                    add("vp", key, _dot_tn(pp, doh))
                dq_ref[rows, col * 128:(col + 1) * 128] = dq.astype(BF16)

            def place(name):
                head0 = acc[(name, (0, 0))] + pltpu.roll(acc[(name, (0, 1))], 64, axis=1)
                head1 = pltpu.roll(acc[(name, (1, 0))], 64, axis=1) + acc[(name, (1, 1))]
                return jnp.where(low, head0, head1)

            before = (place("kp"), place("vp"))
            if b == 0:
                dkf_ref[:, 0:128], dkf_ref[:, 128:256] = before
            else:
                last = slice((b - 1) * BLK, b * BLK)
                dkv_ref[last, 0:128] = own[0] + before[0]
                dkv_ref[last, 128:256] = own[1] + before[1]
            own = (place("kc"), place("vc"))
            kp, vp = kc, vc
        final = slice((blocks - 1) * BLK, blocks * BLK)
        dkv_ref[final, 0:128], dkv_ref[final, 128:256] = own
        ds_ref[...] += dsink_tile
        if rider is not None:
            @pl.when(i == steps - 1)
            def _():
                ro_ref[...] = acc_ref[...].astype(ro_ref.dtype)

    row_q = _row_spec(tq, ATTN_W)
    row_kv = _row_spec(tq, 2 * KV_W)
    in_specs = [pl.BlockSpec(memory_space=pltpu.SMEM), row_q, row_kv, _kv_prev_spec(blocks), row_q,
                pl.BlockSpec(memory_space=pl.ANY)]
    out_specs = [row_q, row_kv, _row_spec(BLK, 2 * KV_W), pl.BlockSpec((8, 128), lambda i: (0, 0))]
    out_shape = [jax.ShapeDtypeStruct((t, ATTN_W), BF16), jax.ShapeDtypeStruct((t, 2 * KV_W), F32),
                 jax.ShapeDtypeStruct((t // tq * BLK, 2 * KV_W), F32), jax.ShapeDtypeStruct((8, 128), F32)]
    operands = [sinks, q, kv, kv, do, after]
    extra = {}
    if rider is not None:
        in_specs += [_row_spec(tq, rm), _row_spec(tq, rn), pl.BlockSpec(memory_space=pl.ANY)]
        out_specs.append(pl.BlockSpec((rm, rn), lambda i: (rseg, 0)))
        out_shape.append(jax.ShapeDtypeStruct(rbuf.shape, rbuf.dtype))
        operands += [ra, rb, rbuf]
        extra = dict(scratch_shapes=[pltpu.VMEM((rm, rn), F32)], input_output_aliases={8: 4})
    return pl.pallas_call(
        body, name="attn_bwd" if rider is None else "attn_bwd_rider", grid=(steps,),
        in_specs=in_specs, out_specs=out_specs, out_shape=out_shape,
        compiler_params=_params("arbitrary"), **extra,
    )(*operands)


def _in_proj_bwd(dq, dkv, dkf, dzu, dzv, h, a, dres, g, wc, layer, dh_dtype):
    t = h.shape[0]
    tm = _tile(t, 512)
    steps = t // tm
    pieces = ((0, 512), (512, 768), (768, 1280), (1280, 1792))

    def body(dq_ref, dkv_ref, dkn_ref, dzu_ref, dzv_ref, h_ref, a_ref, d_ref, g_ref, w_ref,
             dh_ref, dg_ref, gw_ref, acc_ref):
        i = pl.program_id(0)
        _zero_at(i == 0, dg_ref, acc_ref)
        tail = dkv_ref[tm - BLK:tm, :] + jnp.where(i < steps - 1, dkn_ref[...], 0.0)
        dkv = tail if tm == BLK else jnp.concatenate([dkv_ref[0:tm - BLK, :], tail], axis=0)
        dz = (dq_ref[...], dkv.astype(BF16), dzu_ref[...], dzv_ref[...])
        da = None
        for part, (lo, hi) in zip(dz, pieces):
            term = _dot(part, w_ref[lo:hi, :])
            da = term if da is None else da + term
        a16 = a_ref[...]
        for part, (lo, hi) in zip(dz, pieces):
            acc_ref[lo:hi, :] += _dot_tn(part, a16)
        dx, dg = _rms_bwd(h_ref[...], g_ref[...], da)
        dh_ref[...] = (d_ref[...].astype(F32) + dx).astype(dh_dtype)
        dg_ref[...] += dg

        @pl.when(i == steps - 1)
        def _():
            gw_ref[...] = acc_ref[...].astype(BF16)

    return pl.pallas_call(
        body, name="in_proj_bwd", grid=(t // tm,),
        in_specs=[_row_spec(tm, ATTN_W), _row_spec(tm, 2 * KV_W),
                  pl.BlockSpec((BLK, 2 * KV_W), lambda i: (jnp.minimum(i + 1, steps - 1), 0)), _row_spec(tm, GM_W),
                  _row_spec(tm, GM_W), _row_spec(tm, D_MODEL), _row_spec(tm, D_MODEL), _row_spec(tm, D_MODEL),
                  _vec_spec(D_MODEL), _seg_spec(ROWS_C, D_MODEL, layer)],
        out_specs=[_row_spec(tm, D_MODEL), _vec_spec(D_MODEL), pl.BlockSpec((D_IN, D_MODEL), lambda i: (0, 0))],
        out_shape=[jax.ShapeDtypeStruct((t, D_MODEL), dh_dtype), jax.ShapeDtypeStruct((1, D_MODEL), F32),
                   jax.ShapeDtypeStruct((D_IN, D_MODEL), BF16)],
        scratch_shapes=[pltpu.VMEM((D_IN, D_MODEL), F32)],
        compiler_params=_params("arbitrary"),
    )(dq, dkv, dkf, dzu, dzv, h, a, dres, g, wc)


def _weight_grad(a, b, buf, seg, b_layer=None):
    t, m = a.shape
    n = b.shape[-1]
    assert buf.shape[0] % m == 0 and buf.shape[1] == n
    tm = _tile(t, WGRAD_TOKENS)
    steps = t // tm
    half = m // 2

    def body(a_ref, b_ref, buf_ref, o_ref, acc_ref):
        i = pl.program_id(0)
        _zero_at(i == 0, acc_ref)
        b16 = b_ref[...].astype(BF16)
        for rows in (slice(0, half), slice(half, m)):
            acc_ref[rows, :] += _dot_tn(a_ref[:, rows], b16)

        @pl.when(i == steps - 1)
        def _():
            o_ref[...] = acc_ref[...].astype(o_ref.dtype)

    return pl.pallas_call(
        body, name="weight_grad", grid=(steps,),
        in_specs=[_row_spec(tm, m), _row_spec(tm, n) if b_layer is None else _layer_row_spec(tm, n, b_layer),
                  pl.BlockSpec(memory_space=pl.ANY)],
        out_specs=pl.BlockSpec((m, n), lambda i: (seg, 0)),
        out_shape=jax.ShapeDtypeStruct(buf.shape, buf.dtype),
        scratch_shapes=[pltpu.VMEM((m, n), F32)],
        input_output_aliases={2: 0},
        compiler_params=_params("arbitrary"),
    )(a, b, buf)


def _rows8(rows):
    return [jnp.pad(r, ((0, 7), (0, 0))) for r in rows]


def _small_pack_gating(d):
    rows = [jnp.concatenate([d["gm_ln_g"], d["gm_ln_b"]], axis=1), d["gm_bs"].reshape(1, 1024)]
    return jnp.concatenate(_rows8(rows) + [d["gm_ws"].reshape(128, 1024)], axis=0)


def _small_pack_rest(d):
    rows = [d["ln_mix_pre"], d["ln_mix_post"], d["ln_ffn_pre"], d["ln_ffn_post"], d["ln_ple_gate"],
            jnp.concatenate([d["g_attn_out"], d["g_gm_out"]], axis=1),
            jnp.pad(d["attn_sinks"].reshape(1, 8), ((0, 0), (0, 1016)))]
    return jnp.concatenate(_rows8(rows), axis=0)


def _unpack_gating(g):
    return {"gm_ln_g": g[:, 0, :512], "gm_ln_b": g[:, 0, 512:], "gm_bs": g[:, 8].reshape(DEPTH, 8, 128),
            "gm_ws": g[:, 16:GATING_ROWS].reshape(DEPTH, 8, 128, 128)}


def _unpack_rest(s):
    return {"ln_mix_pre": s[:, 0], "ln_mix_post": s[:, 8], "ln_ffn_pre": s[:, 16], "ln_ffn_post": s[:, 24],
            "ln_ple_gate": s[:, 32], "g_attn_out": s[:, 40, :512], "g_gm_out": s[:, 40, 512:], "attn_sinks": s[:, 48, :8]}


def _row(v):
    return v.reshape(1, -1)


def _layer_fwd(h, p, sp, l, weights, target=None):
    tril = jnp.tril(jnp.ones((BLK, BLK), bool))
    wtril = jnp.where(tril[None], sp["gm_ws"][l], 0.0).astype(BF16)
    bsx = jnp.repeat(sp["gm_bs"][l].T, HEAD_DIM, axis=1)
    a, q, kv, zu, zv = _in_proj(h, _row(sp["ln_mix_pre"][l]), weights("c", h), 0)
    wb = weights("b", zu)
    attn, gm, heads, mix, h1 = _mix_fwd(
        q, kv, sp["attn_sinks"][l], zu, zv, _row(sp["gm_ln_g"][l]), _row(sp["gm_ln_b"][l]), wtril, bsx, h,
        _row(sp["g_attn_out"][l]), _row(sp["g_gm_out"][l]), _row(sp["ln_mix_post"][l]), wb, 0)
    wa = weights("a", h1)
    f, gpre, up, act, fo, h2, hn, gate, *out = _ffn_fwd(
        h1, p, l, _row(sp["ln_ffn_pre"][l]), _row(sp["ln_ffn_post"][l]), _row(sp["ln_ple_gate"][l]), wa, wb,
        weights("p", zu), 0, target)
    saved = dict(h=h, a=a, q=q, kv=kv, zu=zu, zv=zv, attn=attn, gm=gm, heads=heads, mix=mix, h1=h1, f=f,
                 gpre=gpre, up=up, act=act, fo=fo, h2=h2, hn=hn, gate=gate, wtril=wtril, bsx=bsx)
    return (out[0] if target is None else tuple(out)), saved


def _layer_bwd_upper(dh, s, p, sp, l, wa, wb, wp, after, ride):
    d = {}
    dgl, dpe, dfo, dgp, dup, dh1, d["ln_ple_gate"], d["ln_ffn_post"], d["ln_ffn_pre"] = _ffn_bwd(
        dh, s["h2"], s["gate"], p, l, s["fo"], s["gpre"], s["up"], s["h1"], _row(sp["ln_ple_gate"][l]),
        _row(sp["ln_ffn_post"][l]), _row(sp["ln_ffn_pre"][l]), wa, wb, wp, 0, after)
    gb = _weight_grad(s["hn"], dgl, lax.empty((2 * D_MODEL, D_MODEL), BF16), 1)
    gp = _weight_grad(dpe, p, lax.empty((D_MODEL, PLE_DIM), BF16), 0, b_layer=l)
    ga = _weight_grad(s["act"], dfo, lax.empty((3 * D_FF, D_MODEL), BF16), 2)
    if not ride:
        ga = _weight_grad(dgp, s["f"], ga, 0)
        ga = _weight_grad(dup, s["f"], ga, 1)
    return (dh1, d, dgp, dup), ga, gp, gb


def _layer_bwd_middle(carry, s, sp, l, wb, gb, ga, after, ride):
    dh1, d, dgp, dup = carry
    dmix, dattn, dgm, d["ln_mix_post"], d["g_attn_out"], d["g_gm_out"] = _out_proj_bwd(
        dh1, s["mix"], s["attn"], s["gm"], _row(sp["ln_mix_post"][l]), _row(sp["g_attn_out"][l]),
        _row(sp["g_gm_out"][l]), wb, 0, after)
    gb = _weight_grad(s["heads"], dmix, gb, 0)
    dzu, dzv, d["gm_ws"], dbs, d["gm_ln_g"], d["gm_ln_b"], *rode = _gm_bwd(
        dgm, s["zu"], s["zv"], _row(sp["gm_ln_g"][l]), _row(sp["gm_ln_b"][l]), s["wtril"], s["bsx"],
        (dup, s["f"], ga, 1) if ride else None)
    d["gm_bs"] = dbs[:, :8].T
    return (dh1, dattn, dzu, dzv, d, dgp), gb, (rode[0] if ride else ga), _small_pack_gating(d)


def _layer_bwd_lower(carry, s, sp, l, wc, ga, after, ride):
    dh1, dattn, dzu, dzv, d, dgp = carry
    dq, dkv, dkf, dsink, *rode = _attn_bwd(s["q"], s["kv"], dattn, sp["attn_sinks"][l], after,
                                            (dgp, s["f"], ga, 0) if ride else None)
    ga = rode[0] if ride else ga
    d["attn_sinks"] = dsink[:, 0]
    dh, d["ln_mix_pre"], gc = _in_proj_bwd(dq, dkv, dkf, dzu, dzv, s["h"], s["a"], dh1, _row(sp["ln_mix_pre"][l]), wc,
                                           0, F32 if l == 0 else BF16)
    return dh, gc, ga, _small_pack_rest(d)


ANY = pl.BlockSpec(memory_space=pl.ANY)


HBM = pl.BlockSpec(memory_space=pltpu.HBM)
SEM = pl.BlockSpec(memory_space=pltpu.SEMAPHORE)
N_PEERS = N_DEV - 1


def _peers():
    x, y, c = lax.axis_index("x"), lax.axis_index("y"), lax.axis_index("c")
    peers = []
    for r in range(1, N_DEV):
        px = 1 - x if r & 4 else x
        py = 1 - y if r & 2 else y
        pc = 1 - c if r & 1 else c
        peers.append(((px, py, pc), 4 * px + 2 * py + pc))
    return 4 * x + 2 * y + c, peers


GATHER, SCATTER, SPREAD = "gather", "scatter", "spread"


def _peer_copy(src, land, send_sems, recv_sems, r, me, peer, peer_slot, mode):
    return pltpu.make_async_remote_copy(
        src_ref=src.at[:, pl.ds(peer_slot, 1)] if mode == SCATTER else src,
        dst_ref=land.at[:, pl.ds(me, 1)] if mode == GATHER else land.at[:, pl.ds(r - 1, 1)],
        send_sem=send_sems.at[r - 1], recv_sem=recv_sems.at[r - 1], device_id=peer, device_id_type=MESH)


def _peer_arrival(src, land, send_sems, recv_sems, r, me, peer, peer_slot, mode):
    return pltpu.make_async_remote_copy(
        src_ref=src.at[:, pl.ds(me, 1)] if mode == SCATTER else src,
        dst_ref=land.at[:, pl.ds(peer_slot, 1)] if mode == GATHER else land.at[:, pl.ds(r - 1, 1)],
        send_sem=send_sems.at[r - 1], recv_sem=recv_sems.at[r - 1], device_id=peer, device_id_type=MESH)


def _send_start(name, srcs, lands, modes):
    n = len(srcs)

    def body(*refs):
        src_refs, land_refs = refs[:n], refs[n:2 * n]
        outs = refs[2 * n:]
        send_sems, recv_sems, token = outs[2 * n:3 * n], outs[3 * n:4 * n], outs[4 * n]
        me, peers = _peers()
        for k in range(n):
            for r, (peer, slot) in enumerate(peers, 1):
                _peer_copy(src_refs[k], land_refs[k], send_sems[k], recv_sems[k], r, me, peer, slot, modes[k]).start()
        token[...] = jnp.zeros_like(token)

    hbm = lambda a: pltpu.HBM(a.shape, a.dtype)
    sems = [pltpu.SemaphoreType.DMA((N_PEERS,))] * (2 * n)
    outs = pl.pallas_call(
        body, name=name, in_specs=[HBM] * (2 * n),
        out_specs=[HBM] * (2 * n) + [SEM] * (2 * n) + [pl.BlockSpec(memory_space=pltpu.VMEM)],
        out_shape=[hbm(a) for a in srcs] + [hbm(a) for a in lands] + sems + [jax.ShapeDtypeStruct((8, 128), F32)],
        input_output_aliases={k: k for k in range(2 * n)},
        compiler_params=pltpu.CompilerParams(has_side_effects=pltpu.SideEffectType.DATAFLOW_SIDE_EFFECTING),
    )(*[pltpu.with_memory_space_constraint(a, pltpu.HBM) for a in list(srcs) + list(lands)])
    return dict(srcs=outs[:n], lands=outs[n:2 * n], send=outs[2 * n:3 * n], recv=outs[3 * n:4 * n],
                modes=list(modes)), outs[4 * n]


def _send_wait(name, sent, ks, after):
    n = len(ks)
    srcs = [sent["srcs"][k] for k in ks]
    lands = [sent["lands"][k] for k in ks]
    modes = [sent["modes"][k] for k in ks]

    def body(*refs):
        src_refs, land_refs = refs[:n], refs[n:2 * n]
        send_sems, recv_sems = refs[2 * n:3 * n], refs[3 * n:4 * n]
        me, peers = _peers()
        for k in range(n):
            for r, (peer, slot) in enumerate(peers, 1):
                args = (src_refs[k], land_refs[k], send_sems[k], recv_sems[k], r, me, peer, slot, modes[k])
                _peer_copy(*args).wait_send()
                _peer_arrival(*args).wait_recv()

    hbm = lambda a: pltpu.HBM(a.shape, a.dtype)
    outs = pl.pallas_call(
        body, name=name, in_specs=[HBM] * (2 * n) + [SEM] * (2 * n) + [ANY],
        out_specs=[HBM] * (2 * n), out_shape=[hbm(a) for a in srcs] + [hbm(a) for a in lands],
        input_output_aliases={k: k for k in range(2 * n)},
        compiler_params=pltpu.CompilerParams(has_side_effects=pltpu.SideEffectType.DATAFLOW_SIDE_EFFECTING),
    )(*srcs, *lands, *[sent["send"][k] for k in ks], *[sent["recv"][k] for k in ks], after)
    return outs[n:], outs[:n]


def _sum_blocks(own, land, ids):
    nseg, _, rows, cols = land.shape

    def body(ids_ref, own_ref, land_ref, o_ref):
        me = ids_ref[1]
        total = None
        for j in range(N_DEV):
            slot = jnp.maximum(jnp.bitwise_xor(me, j) - 1, 0)
            term = jnp.where(me == j, own_ref[...], land_ref[slot]).astype(F32)
            total = term if total is None else total + term
        o_ref[...] = total

    return pl.pallas_call(
        body, name="sum_blocks",
        grid_spec=pltpu.PrefetchScalarGridSpec(
            num_scalar_prefetch=1, grid=(nseg,),
            in_specs=[pl.BlockSpec((None, None, rows, cols), lambda s, ids: (s, ids[0], 0, 0)),
                      pl.BlockSpec((None, N_PEERS, rows, cols), lambda s, ids: (s, 0, 0, 0))],
            out_specs=pl.BlockSpec((None, rows, cols), lambda s, ids: (s, 0, 0))),
        out_shape=jax.ShapeDtypeStruct((nseg, rows, cols), F32),
        compiler_params=_params("parallel"),
    )(ids, own, land)


def _adamw(w, g, m, v):
    shape = w.shape
    cols = shape[-1]
    rows = w.size // cols
    tr = rows
    for cand in (512, 256, 128, 64, 32, 16, 8):
        if rows % cand == 0:
            tr = cand
            break
    c1 = 1.0 / (1.0 - ADAM_B1 ** ADAM_STEP)
    c2 = 1.0 / (1.0 - ADAM_B2 ** ADAM_STEP)

    def body(w_ref, g_ref, m_ref, v_ref, d_ref, nm_ref, nv_ref):
        g = g_ref[...]
        m = ADAM_B1 * m_ref[...] + (1.0 - ADAM_B1) * g
        v = ADAM_B2 * v_ref[...] + (1.0 - ADAM_B2) * (g * g)
        nm_ref[...] = m
        nv_ref[...] = v
        d_ref[...] = -ADAM_LR * ((m * c1) / (jnp.sqrt(v * c2) + ADAM_EPS) + ADAM_WD * w_ref[...])

    spec = pl.BlockSpec((tr, cols), lambda i: (i, 0))
    outs = pl.pallas_call(
        body, name="adamw", grid=(rows // tr,),
        in_specs=[spec] * 4, out_specs=[spec] * 3,
        out_shape=[jax.ShapeDtypeStruct((rows, cols), F32)] * 3,
        compiler_params=_params("parallel"),
    )(*[a.reshape(rows, cols) for a in (w, g, m, v)])
    return [o.reshape(shape) for o in outs]


SMALL = ("ln_mix_pre", "attn_sinks", "gm_ln_g", "gm_ln_b", "gm_ws", "gm_bs", "g_attn_out", "g_gm_out",
         "ln_mix_post", "ln_ffn_pre", "ln_ffn_post", "ln_ple_gate")
WEIGHTS = ("ln_mix_pre", "w_in", "attn_sinks", "gm_ln_g", "gm_ln_b", "gm_ws", "gm_bs", "g_attn_out", "g_gm_out",
           "w_out", "ln_mix_post", "ln_ffn_pre", "w_ffn_gate", "w_ffn_up", "w_ffn_down", "ln_ffn_post", "w_ple",
           "ln_ple_gate", "w_ple_gate")


def _pack_shards(w, l):
    sa = jnp.stack([w["w_ffn_gate"][l].T, w["w_ffn_up"][l].T, w["w_ffn_down"][l]])[:, None]
    sb = jnp.stack([w["w_out"][l], w["w_ple_gate"][l]])[:, None]
    return [w["w_in"][l].T[None, None].astype(BF16), sb.astype(BF16), w["w_ple"][l].T[None, None].astype(BF16),
            sa.astype(BF16)]


def kernel(x, p, ln_mix_pre, w_in, attn_sinks, gm_ln_g, gm_ln_b, gm_ws, gm_bs, g_attn_out, g_gm_out, w_out, ln_mix_post, ln_ffn_pre, w_ffn_gate, w_ffn_up, w_ffn_down, ln_ffn_post, w_ple, ln_ple_gate, w_ple_gate, loss_target, m_ln_mix_pre, m_w_in, m_attn_sinks, m_gm_ln_g, m_gm_ln_b, m_gm_ws, m_gm_bs, m_g_attn_out, m_g_gm_out, m_w_out, m_ln_mix_post, m_ln_ffn_pre, m_w_ffn_gate, m_w_ffn_up, m_w_ffn_down, m_ln_ffn_post, m_w_ple, m_ln_ple_gate, m_w_ple_gate, v_ln_mix_pre, v_w_in, v_attn_sinks, v_gm_ln_g, v_gm_ln_b, v_gm_ws, v_gm_bs, v_g_attn_out, v_g_gm_out, v_w_out, v_ln_mix_post, v_ln_ffn_pre, v_w_ffn_gate, v_w_ffn_up, v_w_ffn_down, v_ln_ffn_post, v_w_ple, v_ln_ple_gate, v_w_ple_gate):
    given = dict(locals())
    w = {n: given[n] for n in WEIGHTS}
    sp = {n: w[n] for n in SMALL}
    kinds = ("c", "b", "p", "a")

    me, _ = _peers()
    shards = [s for l in range(DEPTH) for s in _pack_shards(w, l)]
    lands = [lax.dynamic_update_slice(lax.empty((s.shape[0], N_DEV) + s.shape[2:], BF16), s, (0, me, 0, 0))
             for s in shards]
    gather, token = _send_start("gather_start", shards, lands, [GATHER] * len(shards))
    layer_weights = [{} for _ in range(DEPTH)]

    def weights_of(l):
        def get(kind, after):
            have = layer_weights[l]
            if kind not in have:
                if l < 2:
                    group = {"c": ("c",), "b": ("b", "p"), "p": ("b", "p"), "a": ("a",)}[kind]
                    after = token if (l == 0 and kind == "c") else after
                else:
                    group = kinds
                got, _ = _send_wait(f"gather_wait_{l}{group[0]}", gather, [4 * l + kinds.index(k) for k in group], after)
                for k, g in zip(group, got):
                    have[k] = g.reshape(-1, g.shape[-1])
            return have[kind]
        return get

    h = x[0]
    p3 = p.reshape(DEPTH, -1, PLE_DIM)
    saved = []
    for l in range(DEPTH):
        h, s = _layer_fwd(h, p3, sp, l, weights_of(l), loss_target[0] if l == DEPTH - 1 else None)
        saved.append(s)
    dh, sq = h

    started = []
    after = token
    view = lambda g, rows: g.reshape(-1, N_DEV, rows, g.shape[-1])
    pack16 = lambda s: s.astype(BF16)[None, None]
    packs = ("gating", "rest")

    def send(l, tag, items):
        bufs = list(items.values())
        lands = [lax.empty((a.shape[0], N_PEERS) + a.shape[2:], BF16) for a in bufs]
        sent, tok = _send_start(f"reduce_start_{l}{tag}", bufs, lands, [SPREAD if k in packs else SCATTER for k in items])
        started.append((l, tag, sent, list(items)))
        return tok

    for l in reversed(range(DEPTH)):
        lw = layer_weights[l]
        ride = l > 0
        carry, ga, gp, gb = _layer_bwd_upper(dh, saved[l], p3, sp, l, lw["a"], lw["b"], lw["p"], after, ride)
        if not ride:
            after = send(l, "a", {"a": view(ga, ROWS_A)})
        carry, gb, ga, gating = _layer_bwd_middle(carry, saved[l], sp, l, lw["b"], gb, ga, after, ride)
        after = send(l, "b", {"b": view(gb, ROWS_B), "gating": pack16(gating), "p": view(gp, ROWS_B)})
        dh, gc, ga, rest = _layer_bwd_lower(carry, saved[l], sp, l, lw["c"], ga, after, ride)
        after = send(l, "c", {"c": view(gc, ROWS_C), "rest": pack16(rest), **({"a": view(ga, ROWS_A)} if ride else {})})

    mine = jnp.stack([me, me]).astype(jnp.int32)
    whole = jnp.stack([jnp.zeros_like(me), me]).astype(jnp.int32)
    sums = {k: [None] * DEPTH for k in ("a", "p", "b", "c", "gating", "rest")}

    def collect(group, behind):
        l, tag, sent, keys = group
        lands, srcs = _send_wait(f"reduce_wait_{l}{tag}", sent, list(range(len(keys))), behind)
        for key, land, src in zip(keys, lands, srcs):
            sums[key][l] = _sum_blocks(src, land, whole)[0] if key in packs else _sum_blocks(src, land, mine)

    for group in started[:-1]:
        collect(group, dh)
    grad_x = dh
    loss = lax.psum(sq[0, 0] * (0.5 / D_MODEL), AXES)
    grads, delta, new_m, new_v = {}, {}, {}, {}

    def update(names):
        for n in names:
            delta[n], new_m[n], new_v[n] = _adamw(w[n], grads[n], given["m_" + n], given["v_" + n])

    stack = lambda f, xs: jnp.stack([f(x) for x in xs])
    grads.update({
        "w_ffn_gate": stack(lambda r: r[0].T, sums["a"]), "w_ffn_up": stack(lambda r: r[1].T, sums["a"]),
        "w_ffn_down": stack(lambda r: r[2], sums["a"]), "w_ple": stack(lambda r: r[0].T, sums["p"]),
        "w_out": stack(lambda r: r[0], sums["b"]), "w_ple_gate": stack(lambda r: r[1], sums["b"])})
    grads.update(_unpack_gating(jnp.stack(sums["gating"])))
    early = tuple(grads)
    update(early)
    collect(started[-1], jnp.concatenate([delta[n].reshape(-1)[:1] for n in early]))
    grads["w_in"] = stack(lambda r: r[0].T, sums["c"])
    grads.update(_unpack_rest(jnp.stack(sums["rest"])))
    update([n for n in grads if n not in early])
    return (loss, grad_x[None], *[grads[n] for n in WEIGHTS], *[delta[n] for n in WEIGHTS],
            *[new_m[n] for n in WEIGHTS], *[new_v[n] for n in WEIGHTS])
```

```python
import math

import jax
import jax.numpy as jnp
from jax import lax
from jax.experimental import pallas as pl
from jax.experimental.pallas import tpu as pltpu

F32 = jnp.float32
BF16 = jnp.bfloat16
MESH = pl.DeviceIdType.MESH
AXES = ("x", "y", "c")

D_MODEL = 1024
DEPTH = 4
N_DEV = 8
HEAD_DIM = 64
ATTN_W = 512
KV_W = 128
GM_W = 512
D_IN = 1792
D_FF = 2816
PLE_DIM = 256
BLK = 128
FF_CHUNK = 256
WGRAD_TOKENS = 1024
NORM_EPS = 1e-6
NEG_BIG = -1e30
ATTN_SCALE = HEAD_DIM ** -0.5

ADAM_LR = 0.001
ADAM_B1 = 0.9
ADAM_B2 = 0.999
ADAM_EPS = 1e-08
ADAM_WD = 0.01
ADAM_STEP = 10

ROWS_A = D_FF // N_DEV
ROWS_B = D_MODEL // N_DEV
ROWS_C = D_IN // N_DEV
GATING_ROWS = 144
REST_ROWS = 56

VMEM_LIMIT = 56 * 2 ** 20


def _params(*sem):
    return pltpu.CompilerParams(dimension_semantics=sem, vmem_limit_bytes=VMEM_LIMIT)


def _dot(a, b):
    return jnp.dot(a, b, preferred_element_type=F32)


def _dot_nt(a, b):
    return lax.dot_general(a, b, (((1,), (1,)), ((), ())), preferred_element_type=F32)


def _dot_tn(a, b):
    return lax.dot_general(a, b, (((0,), (0,)), ((), ())), preferred_element_type=F32)


def _rms_fwd(x, g):
    r = lax.rsqrt(jnp.mean(x * x, axis=-1, keepdims=True) + NORM_EPS)
    return x * r * g


def _rms_bwd(x, g, dy):
    r = lax.rsqrt(jnp.mean(x * x, axis=-1, keepdims=True) + NORM_EPS)
    xh = x * r
    dg = jnp.sum(dy * xh, axis=0, keepdims=True)
    dxh = dy * g
    dx = r * (dxh - xh * jnp.mean(dxh * xh, axis=-1, keepdims=True))
    return dx, dg


_GELU_C = math.sqrt(2.0 / math.pi)


def _gelu(x):
    t = jnp.tanh(_GELU_C * (x + 0.044715 * (x * x * x)))
    return 0.5 * x * (1.0 + t)


def _gelu_grad(x):
    x2 = x * x
    t = jnp.tanh(_GELU_C * (x + 0.044715 * (x2 * x)))
    return 0.5 * (1.0 + t) + 0.5 * x * (1.0 - t * t) * (_GELU_C * (1.0 + 3.0 * 0.044715 * x2))


def _sigmoid(x):
    return 1.0 / (1.0 + jnp.exp(-x))


def _row_spec(tm, n):
    return pl.BlockSpec((tm, n), lambda i: (i, 0))


def _layer_row_spec(tm, n, l):
    return pl.BlockSpec((None, tm, n), lambda i: (l, i, 0))


def _vec_spec(n):
    return pl.BlockSpec((1, n), lambda i: (0, 0))


def _seg_spec(rows, cols, seg):
    return pl.BlockSpec((N_DEV * rows, cols), lambda i: (seg, 0), pipeline_mode=pl.Buffered(1))


def _zero_at(first, *refs):
    @pl.when(first)
    def _():
        for r in refs:
            r[...] = jnp.zeros(r.shape, r.dtype)


def _tile(t, want):
    return min(t, want)


def _in_proj(h, g, wc, layer):
    t = h.shape[0]
    tm = _tile(t, 512)

    def body(h_ref, g_ref, w_ref, a_ref, q_ref, kv_ref, zu_ref, zv_ref):
        a = _rms_fwd(h_ref[...], g_ref[...]).astype(BF16)
        a_ref[...] = a
        q_ref[...] = _dot_nt(a, w_ref[0:512, :]).astype(BF16)
        kv_ref[...] = _dot_nt(a, w_ref[512:768, :]).astype(BF16)
        zu_ref[...] = _dot_nt(a, w_ref[768:1280, :])
        zv_ref[...] = _dot_nt(a, w_ref[1280:1792, :])

    return pl.pallas_call(
        body, name="in_proj", grid=(t // tm,),
        in_specs=[_row_spec(tm, D_MODEL), _vec_spec(D_MODEL), _seg_spec(ROWS_C, D_MODEL, layer)],
        out_specs=[_row_spec(tm, D_MODEL), _row_spec(tm, ATTN_W), _row_spec(tm, 2 * KV_W),
                   _row_spec(tm, GM_W), _row_spec(tm, GM_W)],
        out_shape=[jax.ShapeDtypeStruct((t, D_MODEL), BF16), jax.ShapeDtypeStruct((t, ATTN_W), BF16),
                   jax.ShapeDtypeStruct((t, 2 * KV_W), BF16), jax.ShapeDtypeStruct((t, GM_W), F32),
                   jax.ShapeDtypeStruct((t, GM_W), F32)],
        compiler_params=_params("parallel"),
    )(h, g, wc)


def _head_variants(x, low):
    xr = pltpu.roll(x, 64, axis=1)
    zero = jnp.zeros_like(x)
    return {
        (0, 0): jnp.where(low, x, zero).astype(BF16),
        (0, 1): jnp.where(low, zero, xr).astype(BF16),
        (1, 0): jnp.where(low, xr, zero).astype(BF16),
        (1, 1): jnp.where(low, zero, x).astype(BF16),
    }


def _attn_masks(i):
    row = lax.broadcasted_iota(jnp.int32, (BLK, BLK), 0)
    lane = lax.broadcasted_iota(jnp.int32, (BLK, BLK), 1)
    vcur = row >= lane
    dist = jnp.where(vcur, row - lane, row - lane + BLK).astype(F32)
    valid = jnp.logical_or(vcur, i > 0)
    return lane < 64, vcur, dist, valid


def _head_key(h):
    return (h // 4, h % 2)


def _stack_kv(prev, cur, g):
    return jnp.concatenate([prev[(g, 0)], cur[(g, 0)], prev[(g, 1)], cur[(g, 1)]], axis=0)


def _split_cols(p, vcur):
    return [jnp.where(vcur, 0.0, p).astype(BF16), jnp.where(vcur, p, 0.0).astype(BF16)]


def _attn_scores(q_ref, rows, stacked, vcur):
    out = []
    for col in range(4):
        big = _dot_nt(q_ref[rows, col * 128:(col + 1) * 128], stacked[col // 2])
        for half in range(2):
            out.append(jnp.where(vcur, big[:, half * 256 + 128:half * 256 + 256], big[:, half * 256:half * 256 + 128]))
    return out


def _attn_scores_by_head(q_ref, rows, kc, kp, vcur):
    out = []
    for h in range(8):
        qh = q_ref[rows, (h // 2) * 128:(h // 2 + 1) * 128]
        out.append(jnp.where(vcur, _dot_nt(qh, kc[_head_key(h)]), _dot_nt(qh, kp[_head_key(h)])))
    return out


def _attn_probs(s, h, sink, dist, valid):
    s = s * ATTN_SCALE - (2.0 ** -(h + 1)) * dist
    if valid is not None:
        s = jnp.where(valid, s, NEG_BIG)
    m = jnp.maximum(jnp.max(s, axis=1, keepdims=True), sink)
    e = jnp.exp(s - m)
    es = jnp.exp(sink - m)
    inv = 1.0 / (jnp.sum(e, axis=1, keepdims=True) + es)
    return e * inv, es * inv


def _kv_prev_spec(blocks):
    return pl.BlockSpec((BLK, 2 * KV_W), lambda i: (jnp.maximum(i * blocks - 1, 0), 0))


def _kv_variants(kv_ref, rows, low):
    return (_head_variants(kv_ref[rows, 0:128].astype(F32), low), _head_variants(kv_ref[rows, 128:256].astype(F32), low))


def _gm_forward_block(zu, zv, lng, lnb, w_ref, bsx, low):
    gu = _gelu(zu)
    gv = _gelu(zv)
    mu = jnp.mean(gv, axis=-1, keepdims=True)
    xc = gv - mu
    rstd = lax.rsqrt(jnp.mean(xc * xc, axis=-1, keepdims=True) + NORM_EPS)
    xn = xc * rstd
    ln = xn * lng + lnb
    mixed = []
    for col in range(4):
        lc = ln[:, col * 128:(col + 1) * 128]
        lo = jnp.where(low, lc, 0.0).astype(BF16)
        hi = jnp.where(low, 0.0, lc).astype(BF16)
        mixed.append(_dot(w_ref[2 * col], lo) + _dot(w_ref[2 * col + 1], hi) + bsx[:, col * 128:(col + 1) * 128])
    return gu, ln, xn, rstd, mixed


def _mix_fwd(q, kv, sinks, zu, zv, lng, lnb, wtril, bsx, h, ga, gg, gpost, wb, layer):
    t = q.shape[0]
    tq = _tile(t, 512)
    blocks = tq // BLK

    def body(sink_ref, q_ref, kvc_ref, kvp_ref, zu_ref, zv_ref, g_ref, b_ref, w_ref, bs_ref, h_ref, ga_ref, gg_ref,
             gp_ref, wo_ref, attn_ref, gm_ref, heads_ref, mix_ref, h1_ref):
        low, vcur, dist, valid = _attn_masks(pl.program_id(0))
        kp, vp = _kv_variants(kvp_ref, slice(None), low)

        def project(rows, mix):
            mix_ref[rows, :] = mix.astype(BF16)
            h1_ref[rows, :] = h_ref[rows, :] + _rms_fwd(mix, gp_ref[...])

        pending = None
        for b in range(blocks):
            rows = slice(b * BLK, (b + 1) * BLK)
            kc, vc = _kv_variants(kvc_ref, rows, low)
            ks = [_stack_kv(kp, kc, g) for g in range(2)]
            vs = [_stack_kv(vp, vc, g) for g in range(2)]
            scores = _attn_scores(q_ref, rows, ks, vcur)
            if pending is not None:
                project(pending[0], _dot(pending[1], wo_ref[...]))
            gu, _, _, _, mixed = _gm_forward_block(zu_ref[rows, :], zv_ref[rows, :], g_ref[...], b_ref[...], w_ref,
                                                   bs_ref[...], low)
            probs = [_attn_probs(scores[h], h, sink_ref[h], dist, valid if b == 0 else None)[0] for h in range(8)]
            attn_cols, gm_cols = [], []
            for col in range(4):
                gm_cols.append((gu[:, col * 128:(col + 1) * 128] * mixed[col]).astype(BF16))
                p_col = jnp.concatenate(_split_cols(probs[2 * col], vcur) + _split_cols(probs[2 * col + 1], vcur), axis=1)
                attn_cols.append(_dot(p_col, vs[col // 2]).astype(BF16))
            attn = jnp.concatenate(attn_cols, axis=1)
            gm = jnp.concatenate(gm_cols, axis=1)
            attn_ref[rows, :] = attn
            gm_ref[rows, :] = gm
            heads = jnp.concatenate([_rms_fwd(attn.astype(F32), ga_ref[...]).astype(BF16),
                                     _rms_fwd(gm.astype(F32), gg_ref[...]).astype(BF16)], axis=1)
            heads_ref[rows, :] = heads
            pending = (rows, heads)
            kp, vp = kc, vc
        project(pending[0], _dot(pending[1], wo_ref[...]))

    wide = _row_spec(tq, GM_W)
    row = _row_spec(tq, D_MODEL)
    return pl.pallas_call(
        body, name="mix_fwd", grid=(t // tq,),
        in_specs=[pl.BlockSpec(memory_space=pltpu.SMEM), _row_spec(tq, ATTN_W), _row_spec(tq, 2 * KV_W),
                  _kv_prev_spec(blocks), wide, wide, _vec_spec(GM_W), _vec_spec(GM_W),
                  pl.BlockSpec((8, BLK, BLK), lambda i: (0, 0, 0)), pl.BlockSpec((BLK, GM_W), lambda i: (0, 0)),
                  row, _vec_spec(ATTN_W), _vec_spec(GM_W), _vec_spec(D_MODEL), _seg_spec(ROWS_B, D_MODEL, 2 * layer)],
        out_specs=[_row_spec(tq, ATTN_W), wide, row, row, row],
        out_shape=[jax.ShapeDtypeStruct((t, ATTN_W), BF16), jax.ShapeDtypeStruct((t, GM_W), BF16),
                   jax.ShapeDtypeStruct((t, D_MODEL), BF16), jax.ShapeDtypeStruct((t, D_MODEL), BF16),
                   jax.ShapeDtypeStruct((t, D_MODEL), F32)],
        compiler_params=_params("parallel"),
    )(sinks, q, kv, kv, zu, zv, lng, lnb, wtril, bsx, h, ga, gg, gpost, wb)


def _ffn_fwd(h1, p, p_layer, gpre, gpost, gple, wa, wb, wp, layer, target=None):
    t = h1.shape[0]
    tm = _tile(t, 256)

    def body(h_ref, p_ref, gpre_ref, gpost_ref, gple_ref, wg_ref, wu_ref, wd_ref, wpg_ref, wpl_ref, *rest):
        if target is None:
            f_ref, gp_ref, up_ref, act_ref, fo_ref, h2_ref, hn_ref, gate_ref, h3_ref = rest
        else:
            t_ref, f_ref, gp_ref, up_ref, act_ref, fo_ref, h2_ref, hn_ref, gate_ref, dy_ref, l_ref = rest
            _zero_at(pl.program_id(0) == 0, l_ref)
        h = h_ref[...]
        pe = _dot_nt(p_ref[...].astype(BF16), wpl_ref[...])
        f = _rms_fwd(h, gpre_ref[...]).astype(BF16)
        f_ref[...] = f
        chunks = [slice(j * FF_CHUNK, (j + 1) * FF_CHUNK) for j in range(D_FF // FF_CHUNK)]
        fo = None
        gp, up = _dot_nt(f, wg_ref[chunks[0], :]), _dot_nt(f, wu_ref[chunks[0], :])
        for j, cols in enumerate(chunks):
            if j + 1 < len(chunks):
                gp_next, up_next = _dot_nt(f, wg_ref[chunks[j + 1], :]), _dot_nt(f, wu_ref[chunks[j + 1], :])
            act = (gp * _sigmoid(gp) * up).astype(BF16)
            gp_ref[:, cols] = gp.astype(BF16)
            up_ref[:, cols] = up.astype(BF16)
            act_ref[:, cols] = act
            part = _dot(act, wd_ref[cols, :])
            fo = part if fo is None else fo + part
            if j + 1 < len(chunks):
                gp, up = gp_next, up_next
        fo_ref[...] = fo
        h2 = h + _rms_fwd(fo, gpost_ref[...])
        h2_ref[...] = h2
        hn = _rms_fwd(h2, gple_ref[...]).astype(BF16)
        hn_ref[...] = hn
        gate = _sigmoid(_dot(hn, wpg_ref[...]))
        gate_ref[...] = gate.astype(BF16)
        h3 = h2 + pe * gate
        if target is None:
            h3_ref[...] = h3
        else:
            e = h3 - t_ref[...]
            dy_ref[...] = (e * (1.0 / D_MODEL)).astype(BF16)
            s = jnp.sum(jnp.sum(e * e, axis=1, keepdims=True), axis=0, keepdims=True)
            l_ref[...] += jnp.broadcast_to(s, (1, 128))

    wide = _row_spec(tm, D_FF)
    row = _row_spec(tm, D_MODEL)
    vec = _vec_spec(D_MODEL)
    last = target is not None
    return pl.pallas_call(
        body, name="ffn_loss" if last else "ffn_fwd", grid=(t // tm,),
        in_specs=[row, _layer_row_spec(tm, PLE_DIM, p_layer), vec, vec, vec, _seg_spec(ROWS_A, D_MODEL, 3 * layer),
                  _seg_spec(ROWS_A, D_MODEL, 3 * layer + 1), _seg_spec(ROWS_A, D_MODEL, 3 * layer + 2),
                  _seg_spec(ROWS_B, D_MODEL, 2 * layer + 1), _seg_spec(ROWS_B, PLE_DIM, layer)] + [row] * last,
        out_specs=[row, wide, wide, wide, row, row, row, row, row] + [_vec_spec(128)] * last,
        out_shape=[jax.ShapeDtypeStruct((t, D_MODEL), BF16)] + [jax.ShapeDtypeStruct((t, D_FF), BF16)] * 3
        + [jax.ShapeDtypeStruct((t, D_MODEL), F32)] * 2 + [jax.ShapeDtypeStruct((t, D_MODEL), BF16)] * 2
        + [jax.ShapeDtypeStruct((t, D_MODEL), BF16 if last else F32)] + [jax.ShapeDtypeStruct((1, 128), F32)] * last,
        compiler_params=_params("arbitrary" if last else "parallel"),
    )(h1, p, gpre, gpost, gple, wa, wa, wa, wb, wp, *([target] if last else []))


def _ffn_bwd(dh3, h2, gate, p, p_layer, fo, gp, up, h1, gple, gpost, gpre, wa, wb, wp, layer, after):
    t = dh3.shape[0]
    tm = _tile(t, 256)

    def body(d3_ref, h2_ref, gate_ref, p_ref, fo_ref, gp_ref, up_ref, h_ref, gple_ref, gpost_ref, gpre_ref,
             wg_ref, wu_ref, wd_ref, wpg_ref, wpl_ref, after_ref,
             dgl_ref, dpe_ref, dfo_ref, dgp_ref, dup_ref, dh1_ref, dgple_ref, dgpost_ref, dgpre_ref):
        _zero_at(pl.program_id(0) == 0, dgple_ref, dgpost_ref, dgpre_ref)
        d3 = d3_ref[...].astype(F32)
        gate = gate_ref[...].astype(F32)
        pe = _dot_nt(p_ref[...].astype(BF16), wpl_ref[...])
        dpe_ref[...] = (d3 * gate).astype(BF16)
        dgl = (d3 * pe * gate * (1.0 - gate)).astype(BF16)
        dgl_ref[...] = dgl
        dx2, dgple = _rms_bwd(h2_ref[...], gple_ref[...], _dot_nt(dgl, wpg_ref[...]))
        dgple_ref[...] += dgple
        d = d3 + dx2
        dfo, dgpost = _rms_bwd(fo_ref[...], gpost_ref[...], d)
        dfo = dfo.astype(BF16)
        dfo_ref[...] = dfo
        dgpost_ref[...] += dgpost
        chunks = [slice(j * FF_CHUNK, (j + 1) * FF_CHUNK) for j in range(D_FF // FF_CHUNK)]
        df = None
        dact = _dot_nt(dfo, wd_ref[chunks[0], :])
        for j, cols in enumerate(chunks):
            if j + 1 < len(chunks):
                dact_next = _dot_nt(dfo, wd_ref[chunks[j + 1], :])
            gp = gp_ref[:, cols].astype(F32)
            sg = _sigmoid(gp)
            dgp = (dact * up_ref[:, cols].astype(F32) * (sg * (1.0 + gp * (1.0 - sg)))).astype(BF16)
            dup = (dact * (gp * sg)).astype(BF16)
            dgp_ref[:, cols] = dgp
            dup_ref[:, cols] = dup
            part = _dot(dgp, wg_ref[cols, :]) + _dot(dup, wu_ref[cols, :])
            df = part if df is None else df + part
            if j + 1 < len(chunks):
                dact = dact_next
        dx, dgpre = _rms_bwd(h_ref[...], gpre_ref[...], df)
        dh1_ref[...] = (d + dx).astype(BF16)
        dgpre_ref[...] += dgpre

    wide = _row_spec(tm, D_FF)
    row = _row_spec(tm, D_MODEL)
    vec = _vec_spec(D_MODEL)
    narrow = jax.ShapeDtypeStruct((t, D_MODEL), BF16)
    return pl.pallas_call(
        body, name="ffn_bwd", grid=(t // tm,),
        in_specs=[row, row, row, _layer_row_spec(tm, PLE_DIM, p_layer), row, wide, wide, row, vec, vec, vec,
                  _seg_spec(ROWS_A, D_MODEL, 3 * layer), _seg_spec(ROWS_A, D_MODEL, 3 * layer + 1),
                  _seg_spec(ROWS_A, D_MODEL, 3 * layer + 2), _seg_spec(ROWS_B, D_MODEL, 2 * layer + 1),
                  _seg_spec(ROWS_B, PLE_DIM, layer), pl.BlockSpec(memory_space=pl.ANY)],
        out_specs=[row, row, row, wide, wide, row, vec, vec, vec],
        out_shape=[narrow, narrow, narrow, jax.ShapeDtypeStruct((t, D_FF), BF16), jax.ShapeDtypeStruct((t, D_FF), BF16),
                   narrow] + [jax.ShapeDtypeStruct((1, D_MODEL), F32)] * 3,
        compiler_params=_params("arbitrary"),
    )(dh3, h2, gate, p, fo, gp, up, h1, gple, gpost, gpre, wa, wa, wa, wb, wp, after)


def _out_proj_bwd(dh1, mix, attn, gm, gpost, ga, gg, wb, layer, after):
    t = dh1.shape[0]
    tm = _tile(t, 512)

    def body(d_ref, mix_ref, a_ref, m_ref, gp_ref, ga_ref, gg_ref, w_ref, after_ref,
             dmix_ref, da_ref, dm_ref, dgp_ref, dga_ref, dgg_ref):
        _zero_at(pl.program_id(0) == 0, dgp_ref, dga_ref, dgg_ref)
        dmix, dgp = _rms_bwd(mix_ref[...].astype(F32), gp_ref[...], d_ref[...].astype(F32))
        dmix = dmix.astype(BF16)
        dmix_ref[...] = dmix
        da, dga = _rms_bwd(a_ref[...].astype(F32), ga_ref[...], _dot_nt(dmix, w_ref[0:512, :]))
        dm, dgg = _rms_bwd(m_ref[...].astype(F32), gg_ref[...], _dot_nt(dmix, w_ref[512:1024, :]))
        da_ref[...] = da.astype(BF16)
        dm_ref[...] = dm
        dgp_ref[...] += dgp
        dga_ref[...] += dga
        dgg_ref[...] += dgg

    return pl.pallas_call(
        body, name="out_proj_bwd", grid=(t // tm,),
        in_specs=[_row_spec(tm, D_MODEL), _row_spec(tm, D_MODEL), _row_spec(tm, ATTN_W), _row_spec(tm, GM_W),
                  _vec_spec(D_MODEL), _vec_spec(ATTN_W), _vec_spec(GM_W), _seg_spec(ROWS_B, D_MODEL, 2 * layer),
                  pl.BlockSpec(memory_space=pl.ANY)],
        out_specs=[_row_spec(tm, D_MODEL), _row_spec(tm, ATTN_W), _row_spec(tm, GM_W),
                   _vec_spec(D_MODEL), _vec_spec(ATTN_W), _vec_spec(GM_W)],
        out_shape=[jax.ShapeDtypeStruct((t, D_MODEL), BF16), jax.ShapeDtypeStruct((t, ATTN_W), BF16),
                   jax.ShapeDtypeStruct((t, GM_W), F32), jax.ShapeDtypeStruct((1, D_MODEL), F32),
                   jax.ShapeDtypeStruct((1, ATTN_W), F32), jax.ShapeDtypeStruct((1, GM_W), F32)],
        compiler_params=_params("arbitrary"),
    )(dh1, mix, attn, gm, gpost, ga, gg, wb, after)


def _split3(x):
    hi = x.astype(BF16)
    r1 = x - hi.astype(F32)
    mid = r1.astype(BF16)
    lo = (r1 - mid.astype(F32)).astype(BF16)
    return hi, mid, lo


def _rider_pieces(rider, blocks):
    rm = rider[0].shape[1]
    per = -(-rm // (blocks * 256)) * 256
    return [slice(k * per, min((k + 1) * per, rm)) for k in range(blocks) if k * per < rm]


def _gm_bwd(dgm, zu, zv, lng, lnb, wtril, bsx, rider=None):
    t = zu.shape[0]
    tm = _tile(t, 512)
    nb = t // tm
    if rider is not None:
        ra, rb, rbuf, rseg = rider
        rm, rn = ra.shape[1], rb.shape[1]
        pieces = _rider_pieces(rider, tm // BLK)

    def body(d_ref, zu_ref, zv_ref, g_ref, b_ref, w_ref, bs_ref, *rest):
        if rider is None:
            dzu_ref, dzv_ref, dw_ref, dbs_ref, dlg_ref, dlb_ref, dbsx_ref = rest
        else:
            ra_ref, rb_ref, rbuf_ref, dzu_ref, dzv_ref, dw_ref, dbs_ref, dlg_ref, dlb_ref, ro_ref, dbsx_ref, acc_ref = rest
        i = pl.program_id(0)
        _zero_at(i == 0, dw_ref, dlg_ref, dlb_ref, dbsx_ref)
        if rider is not None:
            _zero_at(i == 0, acc_ref)
            rb16 = rb_ref[...].astype(BF16)
        row = lax.broadcasted_iota(jnp.int32, (BLK, BLK), 0)
        lane = lax.broadcasted_iota(jnp.int32, (BLK, BLK), 1)
        low = lane < 64
        tril = row >= lane
        lng = g_ref[...]
        for b in range(tm // BLK):
            rows = slice(b * BLK, (b + 1) * BLK)
            if rider is not None and b < len(pieces):
                acc_ref[pieces[b], :] += _dot_tn(ra_ref[:, pieces[b]], rb16)
            zu = zu_ref[rows, :]
            zv = zv_ref[rows, :]
            gu, ln, xn, rstd, mixed = _gm_forward_block(zu, zv, lng, b_ref[...], w_ref, bs_ref[...], low)
            dgm = d_ref[rows, :]
            dgu_cols, dmx_cols, dln_cols = [], [], []
            for col in range(4):
                sl = slice(col * 128, (col + 1) * 128)
                dg = dgm[:, sl]
                dgu_cols.append(dg * mixed[col])
                dmx = dg * gu[:, sl]
                dmx_cols.append(dmx)
                lc = ln[:, sl]
                halves = (jnp.where(low, lc, 0.0).astype(BF16), jnp.where(low, 0.0, lc).astype(BF16))
                dmx16 = dmx.astype(BF16)
                dmx_half = (jnp.where(low, dmx, 0.0).astype(BF16), jnp.where(low, 0.0, dmx).astype(BF16))
                dln = None
                for half in range(2):
                    hd = 2 * col + half
                    dw_ref[hd] += jnp.where(tril, _dot_nt(dmx16, halves[half]), 0.0)
                    part = _dot_tn(w_ref[hd], dmx_half[half])
                    dln = part if dln is None else dln + part
                dln_cols.append(dln)
            dgu = jnp.concatenate(dgu_cols, axis=1)
            dmx = jnp.concatenate(dmx_cols, axis=1)
            dln = jnp.concatenate(dln_cols, axis=1)
            dzu_ref[rows, :] = (dgu * _gelu_grad(zu)).astype(BF16)
            dbsx_ref[...] += dmx
            dlg_ref[...] += jnp.sum(dln * xn, axis=0, keepdims=True)
            dlb_ref[...] += jnp.sum(dln, axis=0, keepdims=True)
            dxn = dln * lng
            dgv = rstd * (dxn - jnp.mean(dxn, axis=-1, keepdims=True) - xn * jnp.mean(dxn * xn, axis=-1, keepdims=True))
            dzv_ref[rows, :] = (dgv * _gelu_grad(zv)).astype(BF16)

        @pl.when(i == nb - 1)
        def _():
            r = lax.broadcasted_iota(jnp.int32, (GM_W, BLK), 0)
            c = lax.broadcasted_iota(jnp.int32, (GM_W, BLK), 1)
            e = jnp.where(jnp.logical_and(r >= c * 64, r < c * 64 + 64), 1.0, 0.0).astype(BF16)
            hi, mid, lo = _split3(dbsx_ref[...])
            dbs_ref[...] = _dot(hi, e) + _dot(mid, e) + _dot(lo, e)
            if rider is not None:
                ro_ref[...] = acc_ref[...].astype(ro_ref.dtype)

    vec = _vec_spec(GM_W)
    in_specs = [_row_spec(tm, GM_W)] * 3 + [vec, vec, pl.BlockSpec((8, BLK, BLK), lambda i: (0, 0, 0)),
                                            pl.BlockSpec((BLK, GM_W), lambda i: (0, 0))]
    out_specs = [_row_spec(tm, GM_W), _row_spec(tm, GM_W), pl.BlockSpec((8, BLK, BLK), lambda i: (0, 0, 0)),
                 pl.BlockSpec((BLK, BLK), lambda i: (0, 0)), vec, vec]
    out_shape = [jax.ShapeDtypeStruct((t, GM_W), BF16), jax.ShapeDtypeStruct((t, GM_W), BF16),
                 jax.ShapeDtypeStruct((8, BLK, BLK), F32), jax.ShapeDtypeStruct((BLK, BLK), F32),
                 jax.ShapeDtypeStruct((1, GM_W), F32), jax.ShapeDtypeStruct((1, GM_W), F32)]
    scratch = [pltpu.VMEM((BLK, GM_W), F32)]
    operands = [dgm, zu, zv, lng, lnb, wtril, bsx]
    extra = {}
    if rider is not None:
        in_specs += [_row_spec(tm, rm), _row_spec(tm, rn), pl.BlockSpec(memory_space=pl.ANY)]
        out_specs.append(pl.BlockSpec((rm, rn), lambda i: (rseg, 0)))
        out_shape.append(jax.ShapeDtypeStruct(rbuf.shape, rbuf.dtype))
        scratch.append(pltpu.VMEM((rm, rn), F32))
        operands += [ra, rb, rbuf]
        extra = dict(input_output_aliases={9: 6})
    return pl.pallas_call(
        body, name="gm_bwd" if rider is None else "gm_bwd_rider", grid=(nb,),
        in_specs=in_specs, out_specs=out_specs, out_shape=out_shape, scratch_shapes=scratch,
        compiler_params=_params("arbitrary"), **extra,
    )(*operands)


def _attn_bwd(q, kv, do, sinks, after, rider=None):
    t = q.shape[0]
    tq = _tile(t, 512)
    blocks = tq // BLK
    steps = t // tq
    if rider is not None:
        ra, rb, rbuf, rseg = rider
        rm, rn = ra.shape[1], rb.shape[1]
        per = -(-rm // (blocks * 256)) * 256
        pieces = [slice(k * per, min((k + 1) * per, rm)) for k in range(blocks)]

    def body(sink_ref, q_ref, kvc_ref, kvp_ref, do_ref, after_ref, *rest):
        if rider is None:
            dq_ref, dkv_ref, dkf_ref, ds_ref = rest
        else:
            ra_ref, rb_ref, rbuf_ref, dq_ref, dkv_ref, dkf_ref, ds_ref, ro_ref, acc_ref = rest
        i = pl.program_id(0)
        _zero_at(i == 0, ds_ref)
        if rider is not None:
            _zero_at(i == 0, acc_ref)
            rb16 = rb_ref[...].astype(BF16)
        low, vcur, dist, valid = _attn_masks(i)
        head_row = lax.broadcasted_iota(jnp.int32, (8, 128), 0)
        dsink_tile = jnp.zeros((8, 128), F32)
        kp, vp = _kv_variants(kvp_ref, slice(None), low)
        own = None
        for b in range(blocks):
            rows = slice(b * BLK, (b + 1) * BLK)
            kc, vc = _kv_variants(kvc_ref, rows, low)
            scores = _attn_scores_by_head(q_ref, rows, kc, kp, vcur)
            dprobs = _attn_scores_by_head(do_ref, rows, vc, vp, vcur)
            if rider is not None and pieces[b].start < rm:
                acc_ref[pieces[b], :] += _dot_tn(ra_ref[:, pieces[b]], rb16)
            parts = []
            for h in range(8):
                p, ps = _attn_probs(scores[h], h, sink_ref[h], dist, valid if b == 0 else None)
                delta = jnp.sum(p * dprobs[h], axis=1, keepdims=True)
                ds = p * (dprobs[h] - delta) * ATTN_SCALE
                dsink = jnp.sum(-ps * delta, axis=0, keepdims=True)
                dsink_tile = jnp.where(head_row == h, dsink_tile + dsink, dsink_tile)
                parts.append(_split_cols(ds, vcur) + _split_cols(p, vcur))
            acc = {}

            def add(name, key, val):
                acc[(name, key)] = val if (name, key) not in acc else acc[(name, key)] + val

            for col in range(4):
                qh = q_ref[rows, col * 128:(col + 1) * 128]
                doh = do_ref[rows, col * 128:(col + 1) * 128]
                dq = None
                for half in range(2):
                    key = _head_key(2 * col + half)
                    dsp, dsc, pp, pc = parts[2 * col + half]
                    part = _dot(dsc, kc[key]) + _dot(dsp, kp[key])
                    dq = part if dq is None else dq + part
                    add("kc", key, _dot_tn(dsc, qh))
                    add("kp", key, _dot_tn(dsp, qh))
                    add("vc", key, _dot_tn(pc, doh))
                    add("vp", key, _dot_tn(pp, doh))
                dq_ref[rows, col * 128:(col + 1) * 128] = dq.astype(BF16)

            def place(name):
                head0 = acc[(name, (0, 0))] + pltpu.roll(acc[(name, (0, 1))], 64, axis=1)
                head1 = pltpu.roll(acc[(name, (1, 0))], 64, axis=1) + acc[(name, (1, 1))]
                return jnp.where(low, head0, head1)

            before = (place("kp"), place("vp"))
            if b == 0:
                dkf_ref[:, 0:128], dkf_ref[:, 128:256] = before
            else:
                last = slice((b - 1) * BLK, b * BLK)
                dkv_ref[last, 0:128] = own[0] + before[0]
                dkv_ref[last, 128:256] = own[1] + before[1]
            own = (place("kc"), place("vc"))
            kp, vp = kc, vc
        final = slice((blocks - 1) * BLK, blocks * BLK)
        dkv_ref[final, 0:128], dkv_ref[final, 128:256] = own
        ds_ref[...] += dsink_tile
        if rider is not None:
            @pl.when(i == steps - 1)
            def _():
                ro_ref[...] = acc_ref[...].astype(ro_ref.dtype)

    row_q = _row_spec(tq, ATTN_W)
    row_kv = _row_spec(tq, 2 * KV_W)
    in_specs = [pl.BlockSpec(memory_space=pltpu.SMEM), row_q, row_kv, _kv_prev_spec(blocks), row_q,
                pl.BlockSpec(memory_space=pl.ANY)]
    out_specs = [row_q, row_kv, _row_spec(BLK, 2 * KV_W), pl.BlockSpec((8, 128), lambda i: (0, 0))]
    out_shape = [jax.ShapeDtypeStruct((t, ATTN_W), BF16), jax.ShapeDtypeStruct((t, 2 * KV_W), F32),
                 jax.ShapeDtypeStruct((t // tq * BLK, 2 * KV_W), F32), jax.ShapeDtypeStruct((8, 128), F32)]
    operands = [sinks, q, kv, kv, do, after]
    extra = {}
    if rider is not None:
        in_specs += [_row_spec(tq, rm), _row_spec(tq, rn), pl.BlockSpec(memory_space=pl.ANY)]
        out_specs.append(pl.BlockSpec((rm, rn), lambda i: (rseg, 0)))
        out_shape.append(jax.ShapeDtypeStruct(rbuf.shape, rbuf.dtype))
        operands += [ra, rb, rbuf]
        extra = dict(scratch_shapes=[pltpu.VMEM((rm, rn), F32)], input_output_aliases={8: 4})
    return pl.pallas_call(
        body, name="attn_bwd" if rider is None else "attn_bwd_rider", grid=(steps,),
        in_specs=in_specs, out_specs=out_specs, out_shape=out_shape,
        compiler_params=_params("arbitrary"), **extra,
    )(*operands)


def _in_proj_bwd(dq, dkv, dkf, dzu, dzv, h, a, dres, g, wc, layer, dh_dtype):
    t = h.shape[0]
    tm = _tile(t, 512)
    steps = t // tm
    pieces = ((0, 512), (512, 768), (768, 1280), (1280, 1792))

    def body(dq_ref, dkv_ref, dkn_ref, dzu_ref, dzv_ref, h_ref, a_ref, d_ref, g_ref, w_ref,
             dh_ref, dg_ref, gw_ref, acc_ref):
        i = pl.program_id(0)
        _zero_at(i == 0, dg_ref, acc_ref)
        tail = dkv_ref[tm - BLK:tm, :] + jnp.where(i < steps - 1, dkn_ref[...], 0.0)
        dkv = tail if tm == BLK else jnp.concatenate([dkv_ref[0:tm - BLK, :], tail], axis=0)
        dz = (dq_ref[...], dkv.astype(BF16), dzu_ref[...], dzv_ref[...])
        da = None
        for part, (lo, hi) in zip(dz, pieces):
            term = _dot(part, w_ref[lo:hi, :])
            da = term if da is None else da + term
        a16 = a_ref[...]
        for part, (lo, hi) in zip(dz, pieces):
            acc_ref[lo:hi, :] += _dot_tn(part, a16)
        dx, dg = _rms_bwd(h_ref[...], g_ref[...], da)
        dh_ref[...] = (d_ref[...].astype(F32) + dx).astype(dh_dtype)
        dg_ref[...] += dg

        @pl.when(i == steps - 1)
        def _():
            gw_ref[...] = acc_ref[...].astype(BF16)

    return pl.pallas_call(
        body, name="in_proj_bwd", grid=(t // tm,),
        in_specs=[_row_spec(tm, ATTN_W), _row_spec(tm, 2 * KV_W),
                  pl.BlockSpec((BLK, 2 * KV_W), lambda i: (jnp.minimum(i + 1, steps - 1), 0)), _row_spec(tm, GM_W),
                  _row_spec(tm, GM_W), _row_spec(tm, D_MODEL), _row_spec(tm, D_MODEL), _row_spec(tm, D_MODEL),
                  _vec_spec(D_MODEL), _seg_spec(ROWS_C, D_MODEL, layer)],
        out_specs=[_row_spec(tm, D_MODEL), _vec_spec(D_MODEL), pl.BlockSpec((D_IN, D_MODEL), lambda i: (0, 0))],
        out_shape=[jax.ShapeDtypeStruct((t, D_MODEL), dh_dtype), jax.ShapeDtypeStruct((1, D_MODEL), F32),
                   jax.ShapeDtypeStruct((D_IN, D_MODEL), BF16)],
        scratch_shapes=[pltpu.VMEM((D_IN, D_MODEL), F32)],
        compiler_params=_params("arbitrary"),
    )(dq, dkv, dkf, dzu, dzv, h, a, dres, g, wc)


def _weight_grad(a, b, buf, seg, b_layer=None):
    t, m = a.shape
    n = b.shape[-1]
    assert buf.shape[0] % m == 0 and buf.shape[1] == n
    tm = _tile(t, WGRAD_TOKENS)
    steps = t // tm
    half = m // 2

    def body(a_ref, b_ref, buf_ref, o_ref, acc_ref):
        i = pl.program_id(0)
        _zero_at(i == 0, acc_ref)
        b16 = b_ref[...].astype(BF16)
        for rows in (slice(0, half), slice(half, m)):
            acc_ref[rows, :] += _dot_tn(a_ref[:, rows], b16)

        @pl.when(i == steps - 1)
        def _():
            o_ref[...] = acc_ref[...].astype(o_ref.dtype)

    return pl.pallas_call(
        body, name="weight_grad", grid=(steps,),
        in_specs=[_row_spec(tm, m), _row_spec(tm, n) if b_layer is None else _layer_row_spec(tm, n, b_layer),
                  pl.BlockSpec(memory_space=pl.ANY)],
        out_specs=pl.BlockSpec((m, n), lambda i: (seg, 0)),
        out_shape=jax.ShapeDtypeStruct(buf.shape, buf.dtype),
        scratch_shapes=[pltpu.VMEM((m, n), F32)],
        input_output_aliases={2: 0},
        compiler_params=_params("arbitrary"),
    )(a, b, buf)


def _rows8(rows):
    return [jnp.pad(r, ((0, 7), (0, 0))) for r in rows]


def _small_pack_gating(d):
    rows = [jnp.concatenate([d["gm_ln_g"], d["gm_ln_b"]], axis=1), d["gm_bs"].reshape(1, 1024)]
    return jnp.concatenate(_rows8(rows) + [d["gm_ws"].reshape(128, 1024)], axis=0)


def _small_pack_rest(d):
    rows = [d["ln_mix_pre"], d["ln_mix_post"], d["ln_ffn_pre"], d["ln_ffn_post"], d["ln_ple_gate"],
            jnp.concatenate([d["g_attn_out"], d["g_gm_out"]], axis=1),
            jnp.pad(d["attn_sinks"].reshape(1, 8), ((0, 0), (0, 1016)))]
    return jnp.concatenate(_rows8(rows), axis=0)


def _unpack_gating(g):
    return {"gm_ln_g": g[:, 0, :512], "gm_ln_b": g[:, 0, 512:], "gm_bs": g[:, 8].reshape(DEPTH, 8, 128),
            "gm_ws": g[:, 16:GATING_ROWS].reshape(DEPTH, 8, 128, 128)}


def _unpack_rest(s):
    return {"ln_mix_pre": s[:, 0], "ln_mix_post": s[:, 8], "ln_ffn_pre": s[:, 16], "ln_ffn_post": s[:, 24],
            "ln_ple_gate": s[:, 32], "g_attn_out": s[:, 40, :512], "g_gm_out": s[:, 40, 512:], "attn_sinks": s[:, 48, :8]}


def _row(v):
    return v.reshape(1, -1)


def _layer_fwd(h, p, sp, l, weights, target=None):
    tril = jnp.tril(jnp.ones((BLK, BLK), bool))
    wtril = jnp.where(tril[None], sp["gm_ws"][l], 0.0).astype(BF16)
    bsx = jnp.repeat(sp["gm_bs"][l].T, HEAD_DIM, axis=1)
    a, q, kv, zu, zv = _in_proj(h, _row(sp["ln_mix_pre"][l]), weights("c", h), 0)
    wb = weights("b", zu)
    attn, gm, heads, mix, h1 = _mix_fwd(
        q, kv, sp["attn_sinks"][l], zu, zv, _row(sp["gm_ln_g"][l]), _row(sp["gm_ln_b"][l]), wtril, bsx, h,
        _row(sp["g_attn_out"][l]), _row(sp["g_gm_out"][l]), _row(sp["ln_mix_post"][l]), wb, 0)
    wa = weights("a", h1)
    f, gpre, up, act, fo, h2, hn, gate, *out = _ffn_fwd(
        h1, p, l, _row(sp["ln_ffn_pre"][l]), _row(sp["ln_ffn_post"][l]), _row(sp["ln_ple_gate"][l]), wa, wb,
        weights("p", zu), 0, target)
    saved = dict(h=h, a=a, q=q, kv=kv, zu=zu, zv=zv, attn=attn, gm=gm, heads=heads, mix=mix, h1=h1, f=f,
                 gpre=gpre, up=up, act=act, fo=fo, h2=h2, hn=hn, gate=gate, wtril=wtril, bsx=bsx)
    return (out[0] if target is None else tuple(out)), saved


def _layer_bwd_upper(dh, s, p, sp, l, wa, wb, wp, after, ride):
    d = {}
    dgl, dpe, dfo, dgp, dup, dh1, d["ln_ple_gate"], d["ln_ffn_post"], d["ln_ffn_pre"] = _ffn_bwd(
        dh, s["h2"], s["gate"], p, l, s["fo"], s["gpre"], s["up"], s["h1"], _row(sp["ln_ple_gate"][l]),
        _row(sp["ln_ffn_post"][l]), _row(sp["ln_ffn_pre"][l]), wa, wb, wp, 0, after)
    gb = _weight_grad(s["hn"], dgl, lax.empty((2 * D_MODEL, D_MODEL), BF16), 1)
    gp = _weight_grad(dpe, p, lax.empty((D_MODEL, PLE_DIM), BF16), 0, b_layer=l)
    ga = _weight_grad(s["act"], dfo, lax.empty((3 * D_FF, D_MODEL), BF16), 2)
    if not ride:
        ga = _weight_grad(dgp, s["f"], ga, 0)
        ga = _weight_grad(dup, s["f"], ga, 1)
    return (dh1, d, dgp, dup), ga, gp, gb


def _layer_bwd_middle(carry, s, sp, l, wb, gb, ga, after, ride):
    dh1, d, dgp, dup = carry
    dmix, dattn, dgm, d["ln_mix_post"], d["g_attn_out"], d["g_gm_out"] = _out_proj_bwd(
        dh1, s["mix"], s["attn"], s["gm"], _row(sp["ln_mix_post"][l]), _row(sp["g_attn_out"][l]),
        _row(sp["g_gm_out"][l]), wb, 0, after)
    gb = _weight_grad(s["heads"], dmix, gb, 0)
    dzu, dzv, d["gm_ws"], dbs, d["gm_ln_g"], d["gm_ln_b"], *rode = _gm_bwd(
        dgm, s["zu"], s["zv"], _row(sp["gm_ln_g"][l]), _row(sp["gm_ln_b"][l]), s["wtril"], s["bsx"],
        (dup, s["f"], ga, 1) if ride else None)
    d["gm_bs"] = dbs[:, :8].T
    return (dh1, dattn, dzu, dzv, d, dgp), gb, (rode[0] if ride else ga), _small_pack_gating(d)


def _layer_bwd_lower(carry, s, sp, l, wc, ga, after, ride):
    dh1, dattn, dzu, dzv, d, dgp = carry
    dq, dkv, dkf, dsink, *rode = _attn_bwd(s["q"], s["kv"], dattn, sp["attn_sinks"][l], after,
                                            (dgp, s["f"], ga, 0) if ride else None)
    ga = rode[0] if ride else ga
    d["attn_sinks"] = dsink[:, 0]
    dh, d["ln_mix_pre"], gc = _in_proj_bwd(dq, dkv, dkf, dzu, dzv, s["h"], s["a"], dh1, _row(sp["ln_mix_pre"][l]), wc,
                                           0, F32 if l == 0 else BF16)
    return dh, gc, ga, _small_pack_rest(d)


ANY = pl.BlockSpec(memory_space=pl.ANY)


HBM = pl.BlockSpec(memory_space=pltpu.HBM)
SEM = pl.BlockSpec(memory_space=pltpu.SEMAPHORE)
N_PEERS = N_DEV - 1


def _peers():
    x, y, c = lax.axis_index("x"), lax.axis_index("y"), lax.axis_index("c")
    peers = []
    for r in range(1, N_DEV):
        px = 1 - x if r & 4 else x
        py = 1 - y if r & 2 else y
        pc = 1 - c if r & 1 else c
        peers.append(((px, py, pc), 4 * px + 2 * py + pc))
    return 4 * x + 2 * y + c, peers


GATHER, SCATTER, SPREAD = "gather", "scatter", "spread"


def _peer_copy(src, land, send_sems, recv_sems, r, me, peer, peer_slot, mode):
    return pltpu.make_async_remote_copy(
        src_ref=src.at[:, pl.ds(peer_slot, 1)] if mode == SCATTER else src,
        dst_ref=land.at[:, pl.ds(me, 1)] if mode == GATHER else land.at[:, pl.ds(r - 1, 1)],
        send_sem=send_sems.at[r - 1], recv_sem=recv_sems.at[r - 1], device_id=peer, device_id_type=MESH)


def _peer_arrival(src, land, send_sems, recv_sems, r, me, peer, peer_slot, mode):
    return pltpu.make_async_remote_copy(
        src_ref=src.at[:, pl.ds(me, 1)] if mode == SCATTER else src,
        dst_ref=land.at[:, pl.ds(peer_slot, 1)] if mode == GATHER else land.at[:, pl.ds(r - 1, 1)],
        send_sem=send_sems.at[r - 1], recv_sem=recv_sems.at[r - 1], device_id=peer, device_id_type=MESH)


def _send_start(name, srcs, lands, modes):
    n = len(srcs)

    def body(*refs):
        src_refs, land_refs = refs[:n], refs[n:2 * n]
        outs = refs[2 * n:]
        send_sems, recv_sems, token = outs[2 * n:3 * n], outs[3 * n:4 * n], outs[4 * n]
        me, peers = _peers()
        for k in range(n):
            for r, (peer, slot) in enumerate(peers, 1):
                _peer_copy(src_refs[k], land_refs[k], send_sems[k], recv_sems[k], r, me, peer, slot, modes[k]).start()
        token[...] = jnp.zeros_like(token)

    hbm = lambda a: pltpu.HBM(a.shape, a.dtype)
    sems = [pltpu.SemaphoreType.DMA((N_PEERS,))] * (2 * n)
    outs = pl.pallas_call(
        body, name=name, in_specs=[HBM] * (2 * n),
        out_specs=[HBM] * (2 * n) + [SEM] * (2 * n) + [pl.BlockSpec(memory_space=pltpu.VMEM)],
        out_shape=[hbm(a) for a in srcs] + [hbm(a) for a in lands] + sems + [jax.ShapeDtypeStruct((8, 128), F32)],
        input_output_aliases={k: k for k in range(2 * n)},
        compiler_params=pltpu.CompilerParams(has_side_effects=pltpu.SideEffectType.DATAFLOW_SIDE_EFFECTING),
    )(*[pltpu.with_memory_space_constraint(a, pltpu.HBM) for a in list(srcs) + list(lands)])
    return dict(srcs=outs[:n], lands=outs[n:2 * n], send=outs[2 * n:3 * n], recv=outs[3 * n:4 * n],
                modes=list(modes)), outs[4 * n]


def _send_wait(name, sent, ks, after):
    n = len(ks)
    srcs = [sent["srcs"][k] for k in ks]
    lands = [sent["lands"][k] for k in ks]
    modes = [sent["modes"][k] for k in ks]

    def body(*refs):
        src_refs, land_refs = refs[:n], refs[n:2 * n]
        send_sems, recv_sems = refs[2 * n:3 * n], refs[3 * n:4 * n]
        me, peers = _peers()
        for k in range(n):
            for r, (peer, slot) in enumerate(peers, 1):
                args = (src_refs[k], land_refs[k], send_sems[k], recv_sems[k], r, me, peer, slot, modes[k])
                _peer_copy(*args).wait_send()
                _peer_arrival(*args).wait_recv()

    hbm = lambda a: pltpu.HBM(a.shape, a.dtype)
    outs = pl.pallas_call(
        body, name=name, in_specs=[HBM] * (2 * n) + [SEM] * (2 * n) + [ANY],
        out_specs=[HBM] * (2 * n), out_shape=[hbm(a) for a in srcs] + [hbm(a) for a in lands],
        input_output_aliases={k: k for k in range(2 * n)},
        compiler_params=pltpu.CompilerParams(has_side_effects=pltpu.SideEffectType.DATAFLOW_SIDE_EFFECTING),
    )(*srcs, *lands, *[sent["send"][k] for k in ks], *[sent["recv"][k] for k in ks], after)
    return outs[n:], outs[:n]


def _sum_blocks(own, land, ids):
    nseg, _, rows, cols = land.shape

    def body(ids_ref, own_ref, land_ref, o_ref):
        me = ids_ref[1]
        total = None
        for j in range(N_DEV):
            slot = jnp.maximum(jnp.bitwise_xor(me, j) - 1, 0)
            term = jnp.where(me == j, own_ref[...], land_ref[slot]).astype(F32)
            total = term if total is None else total + term
        o_ref[...] = total

    return pl.pallas_call(
        body, name="sum_blocks",
        grid_spec=pltpu.PrefetchScalarGridSpec(
            num_scalar_prefetch=1, grid=(nseg,),
            in_specs=[pl.BlockSpec((None, None, rows, cols), lambda s, ids: (s, ids[0], 0, 0)),
                      pl.BlockSpec((None, N_PEERS, rows, cols), lambda s, ids: (s, 0, 0, 0))],
            out_specs=pl.BlockSpec((None, rows, cols), lambda s, ids: (s, 0, 0))),
        out_shape=jax.ShapeDtypeStruct((nseg, rows, cols), F32),
        compiler_params=_params("parallel"),
    )(ids, own, land)


def _adamw(w, g, m, v):
    shape = w.shape
    cols = shape[-1]
    rows = w.size // cols
    tr = rows
    for cand in (512, 256, 128, 64, 32, 16, 8):
        if rows % cand == 0:
            tr = cand
            break
    c1 = 1.0 / (1.0 - ADAM_B1 ** ADAM_STEP)
    c2 = 1.0 / (1.0 - ADAM_B2 ** ADAM_STEP)

    def body(w_ref, g_ref, m_ref, v_ref, d_ref, nm_ref, nv_ref):
        g = g_ref[...]
        m = ADAM_B1 * m_ref[...] + (1.0 - ADAM_B1) * g
        v = ADAM_B2 * v_ref[...] + (1.0 - ADAM_B2) * (g * g)
        nm_ref[...] = m
        nv_ref[...] = v
        d_ref[...] = -ADAM_LR * ((m * c1) / (jnp.sqrt(v * c2) + ADAM_EPS) + ADAM_WD * w_ref[...])

    spec = pl.BlockSpec((tr, cols), lambda i: (i, 0))
    outs = pl.pallas_call(
        body, name="adamw", grid=(rows // tr,),
        in_specs=[spec] * 4, out_specs=[spec] * 3,
        out_shape=[jax.ShapeDtypeStruct((rows, cols), F32)] * 3,
        compiler_params=_params("parallel"),
    )(*[a.reshape(rows, cols) for a in (w, g, m, v)])
    return [o.reshape(shape) for o in outs]


SMALL = ("ln_mix_pre", "attn_sinks", "gm_ln_g", "gm_ln_b", "gm_ws", "gm_bs", "g_attn_out", "g_gm_out",
         "ln_mix_post", "ln_ffn_pre", "ln_ffn_post", "ln_ple_gate")
WEIGHTS = ("ln_mix_pre", "w_in", "attn_sinks", "gm_ln_g", "gm_ln_b", "gm_ws", "gm_bs", "g_attn_out", "g_gm_out",
           "w_out", "ln_mix_post", "ln_ffn_pre", "w_ffn_gate", "w_ffn_up", "w_ffn_down", "ln_ffn_post", "w_ple",
           "ln_ple_gate", "w_ple_gate")


def _pack_shards(w, l):
    sa = jnp.stack([w["w_ffn_gate"][l].T, w["w_ffn_up"][l].T, w["w_ffn_down"][l]])[:, None]
    sb = jnp.stack([w["w_out"][l], w["w_ple_gate"][l]])[:, None]
    return [w["w_in"][l].T[None, None].astype(BF16), sb.astype(BF16), w["w_ple"][l].T[None, None].astype(BF16),
            sa.astype(BF16)]


def kernel(x, p, ln_mix_pre, w_in, attn_sinks, gm_ln_g, gm_ln_b, gm_ws, gm_bs, g_attn_out, g_gm_out, w_out, ln_mix_post, ln_ffn_pre, w_ffn_gate, w_ffn_up, w_ffn_down, ln_ffn_post, w_ple, ln_ple_gate, w_ple_gate, loss_target, m_ln_mix_pre, m_w_in, m_attn_sinks, m_gm_ln_g, m_gm_ln_b, m_gm_ws, m_gm_bs, m_g_attn_out, m_g_gm_out, m_w_out, m_ln_mix_post, m_ln_ffn_pre, m_w_ffn_gate, m_w_ffn_up, m_w_ffn_down, m_ln_ffn_post, m_w_ple, m_ln_ple_gate, m_w_ple_gate, v_ln_mix_pre, v_w_in, v_attn_sinks, v_gm_ln_g, v_gm_ln_b, v_gm_ws, v_gm_bs, v_g_attn_out, v_g_gm_out, v_w_out, v_ln_mix_post, v_ln_ffn_pre, v_w_ffn_gate, v_w_ffn_up, v_w_ffn_down, v_ln_ffn_post, v_w_ple, v_ln_ple_gate, v_w_ple_gate):
    given = dict(locals())
    w = {n: given[n] for n in WEIGHTS}
    sp = {n: w[n] for n in SMALL}
    kinds = ("c", "b", "p", "a")

    me, _ = _peers()
    shards = [s for l in range(DEPTH) for s in _pack_shards(w, l)]
    lands = [lax.dynamic_update_slice(lax.empty((s.shape[0], N_DEV) + s.shape[2:], BF16), s, (0, me, 0, 0))
             for s in shards]
    gather, token = _send_start("gather_start", shards, lands, [GATHER] * len(shards))
    layer_weights = [{} for _ in range(DEPTH)]

    def weights_of(l):
        def get(kind, after):
            have = layer_weights[l]
            if kind not in have:
                if l < 2:
                    group = {"c": ("c",), "b": ("b", "p"), "p": ("b", "p"), "a": ("a",)}[kind]
                    after = token if (l == 0 and kind == "c") else after
                else:
                    group = kinds
                got, _ = _send_wait(f"gather_wait_{l}{group[0]}", gather, [4 * l + kinds.index(k) for k in group], after)
                for k, g in zip(group, got):
                    have[k] = g.reshape(-1, g.shape[-1])
            return have[kind]
        return get

    h = x[0]
    p3 = p.reshape(DEPTH, -1, PLE_DIM)
    saved = []
    for l in range(DEPTH):
        h, s = _layer_fwd(h, p3, sp, l, weights_of(l), loss_target[0] if l == DEPTH - 1 else None)
        saved.append(s)
    dh, sq = h

    started = []
    after = token
    view = lambda g, rows: g.reshape(-1, N_DEV, rows, g.shape[-1])
    pack16 = lambda s: s.astype(BF16)[None, None]
    packs = ("gating", "rest")

    def send(l, tag, items):
        bufs = list(items.values())
        lands = [lax.empty((a.shape[0], N_PEERS) + a.shape[2:], BF16) for a in bufs]
        sent, tok = _send_start(f"reduce_start_{l}{tag}", bufs, lands, [SPREAD if k in packs else SCATTER for k in items])
        started.append((l, tag, sent, list(items)))
        return tok

    for l in reversed(range(DEPTH)):
        lw = layer_weights[l]
        ride = l > 0
        carry, ga, gp, gb = _layer_bwd_upper(dh, saved[l], p3, sp, l, lw["a"], lw["b"], lw["p"], after, ride)
        if not ride:
            after = send(l, "a", {"a": view(ga, ROWS_A)})
        carry, gb, ga, gating = _layer_bwd_middle(carry, saved[l], sp, l, lw["b"], gb, ga, after, ride)
        after = send(l, "b", {"b": view(gb, ROWS_B), "gating": pack16(gating), "p": view(gp, ROWS_B)})
        dh, gc, ga, rest = _layer_bwd_lower(carry, saved[l], sp, l, lw["c"], ga, after, ride)
        after = send(l, "c", {"c": view(gc, ROWS_C), "rest": pack16(rest), **({"a": view(ga, ROWS_A)} if ride else {})})

    mine = jnp.stack([me, me]).astype(jnp.int32)
    whole = jnp.stack([jnp.zeros_like(me), me]).astype(jnp.int32)
    sums = {k: [None] * DEPTH for k in ("a", "p", "b", "c", "gating", "rest")}

    def collect(group, behind):
        l, tag, sent, keys = group
        lands, srcs = _send_wait(f"reduce_wait_{l}{tag}", sent, list(range(len(keys))), behind)
        for key, land, src in zip(keys, lands, srcs):
            sums[key][l] = _sum_blocks(src, land, whole)[0] if key in packs else _sum_blocks(src, land, mine)

    for group in started[:-1]:
        collect(group, after)
    grad_x = dh
    loss = lax.psum(sq[0, 0] * (0.5 / D_MODEL), AXES)
    grads, delta, new_m, new_v = {}, {}, {}, {}
    row_form = {}
    flip = lambda a: jnp.swapaxes(a, 1, 2)

    def update(names):
        for n in names:
            if n in row_form:
                grads[n] = flip(row_form[n])
                outs = _adamw(flip(w[n]), row_form[n], flip(given["m_" + n]), flip(given["v_" + n]))
                delta[n], new_m[n], new_v[n] = [flip(o) for o in outs]
            else:
                delta[n], new_m[n], new_v[n] = _adamw(w[n], grads[n], given["m_" + n], given["v_" + n])

    stack = lambda f, xs: jnp.stack([f(x) for x in xs])
    row_form.update({"w_ffn_gate": stack(lambda r: r[0], sums["a"]), "w_ffn_up": stack(lambda r: r[1], sums["a"])})
    grads.update({
        "w_ffn_down": stack(lambda r: r[2], sums["a"]), "w_ple": stack(lambda r: r[0].T, sums["p"]),
        "w_out": stack(lambda r: r[0], sums["b"]), "w_ple_gate": stack(lambda r: r[1], sums["b"])})
    grads.update(_unpack_gating(jnp.stack(sums["gating"])))
    early = tuple(row_form) + tuple(grads)
    update(early)
    collect(started[-1], jnp.concatenate([delta[n][(0,) * delta[n].ndim].reshape(1) for n in early]))
    row_form["w_in"] = stack(lambda r: r[0], sums["c"])
    late = _unpack_rest(jnp.stack(sums["rest"]))
    grads.update(late)
    update(["w_in", *late])
    return (loss, grad_x[None], *[grads[n] for n in WEIGHTS], *[delta[n] for n in WEIGHTS],
            *[new_m[n] for n in WEIGHTS], *[new_v[n] for n in WEIGHTS])
```

```python
import math

import jax
import jax.numpy as jnp
from jax import lax
from jax.experimental import pallas as pl
from jax.experimental.pallas import tpu as pltpu

F32 = jnp.float32
BF16 = jnp.bfloat16
MESH = pl.DeviceIdType.MESH
AXES = ("x", "y", "c")

D_MODEL = 1024
DEPTH = 4
N_DEV = 8
HEAD_DIM = 64
ATTN_W = 512
KV_W = 128
GM_W = 512
D_IN = 1792
D_FF = 2816
PLE_DIM = 256
BLK = 128
FF_CHUNK = 256
WGRAD_TOKENS = 1024
NORM_EPS = 1e-6
NEG_BIG = -1e30
ATTN_SCALE = HEAD_DIM ** -0.5

ADAM_LR = 0.001
ADAM_B1 = 0.9
ADAM_B2 = 0.999
ADAM_EPS = 1e-08
ADAM_WD = 0.01
ADAM_STEP = 10

ROWS_A = D_FF // N_DEV
ROWS_B = D_MODEL // N_DEV
ROWS_C = D_IN // N_DEV
GATING_ROWS = 144
REST_ROWS = 56

VMEM_LIMIT = 56 * 2 ** 20


def _params(*sem):
    return pltpu.CompilerParams(dimension_semantics=sem, vmem_limit_bytes=VMEM_LIMIT)


def _dot(a, b):
    return jnp.dot(a, b, preferred_element_type=F32)


def _dot_nt(a, b):
    return lax.dot_general(a, b, (((1,), (1,)), ((), ())), preferred_element_type=F32)


def _dot_tn(a, b):
    return lax.dot_general(a, b, (((0,), (0,)), ((), ())), preferred_element_type=F32)


def _rms_fwd(x, g):
    r = lax.rsqrt(jnp.mean(x * x, axis=-1, keepdims=True) + NORM_EPS)
    return x * r * g


def _rms_bwd(x, g, dy):
    r = lax.rsqrt(jnp.mean(x * x, axis=-1, keepdims=True) + NORM_EPS)
    xh = x * r
    dg = jnp.sum(dy * xh, axis=0, keepdims=True)
    dxh = dy * g
    dx = r * (dxh - xh * jnp.mean(dxh * xh, axis=-1, keepdims=True))
    return dx, dg


_GELU_C = math.sqrt(2.0 / math.pi)


def _gelu(x):
    t = jnp.tanh(_GELU_C * (x + 0.044715 * (x * x * x)))
    return 0.5 * x * (1.0 + t)


def _gelu_grad(x):
    x2 = x * x
    t = jnp.tanh(_GELU_C * (x + 0.044715 * (x2 * x)))
    return 0.5 * (1.0 + t) + 0.5 * x * (1.0 - t * t) * (_GELU_C * (1.0 + 3.0 * 0.044715 * x2))


def _sigmoid(x):
    return 1.0 / (1.0 + jnp.exp(-x))


def _row_spec(tm, n):
    return pl.BlockSpec((tm, n), lambda i: (i, 0))


def _layer_row_spec(tm, n, l):
    return pl.BlockSpec((None, tm, n), lambda i: (l, i, 0))


def _vec_spec(n):
    return pl.BlockSpec((1, n), lambda i: (0, 0))


def _seg_spec(rows, cols, seg):
    return pl.BlockSpec((N_DEV * rows, cols), lambda i: (seg, 0), pipeline_mode=pl.Buffered(1))


def _zero_at(first, *refs):
    @pl.when(first)
    def _():
        for r in refs:
            r[...] = jnp.zeros(r.shape, r.dtype)


def _tile(t, want):
    return min(t, want)


def _in_proj(h, g, wc, layer):
    t = h.shape[0]
    tm = _tile(t, 512)

    def body(h_ref, g_ref, w_ref, a_ref, q_ref, kv_ref, zu_ref, zv_ref):
        a = _rms_fwd(h_ref[...], g_ref[...]).astype(BF16)
        a_ref[...] = a
        q_ref[...] = _dot_nt(a, w_ref[0:512, :]).astype(BF16)
        kv_ref[...] = _dot_nt(a, w_ref[512:768, :]).astype(BF16)
        zu_ref[...] = _dot_nt(a, w_ref[768:1280, :])
        zv_ref[...] = _dot_nt(a, w_ref[1280:1792, :])

    return pl.pallas_call(
        body, name="in_proj", grid=(t // tm,),
        in_specs=[_row_spec(tm, D_MODEL), _vec_spec(D_MODEL), _seg_spec(ROWS_C, D_MODEL, layer)],
        out_specs=[_row_spec(tm, D_MODEL), _row_spec(tm, ATTN_W), _row_spec(tm, 2 * KV_W),
                   _row_spec(tm, GM_W), _row_spec(tm, GM_W)],
        out_shape=[jax.ShapeDtypeStruct((t, D_MODEL), BF16), jax.ShapeDtypeStruct((t, ATTN_W), BF16),
                   jax.ShapeDtypeStruct((t, 2 * KV_W), BF16), jax.ShapeDtypeStruct((t, GM_W), F32),
                   jax.ShapeDtypeStruct((t, GM_W), F32)],
        compiler_params=_params("parallel"),
    )(h, g, wc)


def _head_variants(x, low):
    xr = pltpu.roll(x, 64, axis=1)
    zero = jnp.zeros_like(x)
    return {
        (0, 0): jnp.where(low, x, zero).astype(BF16),
        (0, 1): jnp.where(low, zero, xr).astype(BF16),
        (1, 0): jnp.where(low, xr, zero).astype(BF16),
        (1, 1): jnp.where(low, zero, x).astype(BF16),
    }


def _attn_masks(i):
    row = lax.broadcasted_iota(jnp.int32, (BLK, BLK), 0)
    lane = lax.broadcasted_iota(jnp.int32, (BLK, BLK), 1)
    vcur = row >= lane
    dist = jnp.where(vcur, row - lane, row - lane + BLK).astype(F32)
    valid = jnp.logical_or(vcur, i > 0)
    return lane < 64, vcur, dist, valid


def _head_key(h):
    return (h // 4, h % 2)


def _stack_kv(prev, cur, g):
    return jnp.concatenate([prev[(g, 0)], cur[(g, 0)], prev[(g, 1)], cur[(g, 1)]], axis=0)


def _split_cols(p, vcur):
    return [jnp.where(vcur, 0.0, p).astype(BF16), jnp.where(vcur, p, 0.0).astype(BF16)]


def _attn_scores(q_ref, rows, stacked, vcur):
    out = []
    for col in range(4):
        big = _dot_nt(q_ref[rows, col * 128:(col + 1) * 128], stacked[col // 2])
        for half in range(2):
            out.append(jnp.where(vcur, big[:, half * 256 + 128:half * 256 + 256], big[:, half * 256:half * 256 + 128]))
    return out


def _attn_scores_by_head(q_ref, rows, kc, kp, vcur):
    out = []
    for h in range(8):
        qh = q_ref[rows, (h // 2) * 128:(h // 2 + 1) * 128]
        out.append(jnp.where(vcur, _dot_nt(qh, kc[_head_key(h)]), _dot_nt(qh, kp[_head_key(h)])))
    return out


def _attn_probs(s, h, sink, dist, valid):
    s = s * ATTN_SCALE - (2.0 ** -(h + 1)) * dist
    if valid is not None:
        s = jnp.where(valid, s, NEG_BIG)
    m = jnp.maximum(jnp.max(s, axis=1, keepdims=True), sink)
    e = jnp.exp(s - m)
    es = jnp.exp(sink - m)
    inv = 1.0 / (jnp.sum(e, axis=1, keepdims=True) + es)
    return e * inv, es * inv


def _kv_prev_spec(blocks):
    return pl.BlockSpec((BLK, 2 * KV_W), lambda i: (jnp.maximum(i * blocks - 1, 0), 0))


def _kv_variants(kv_ref, rows, low):
    return (_head_variants(kv_ref[rows, 0:128].astype(F32), low), _head_variants(kv_ref[rows, 128:256].astype(F32), low))


def _gm_forward_block(zu, zv, lng, lnb, w_ref, bsx, low):
    gu = _gelu(zu)
    gv = _gelu(zv)
    mu = jnp.mean(gv, axis=-1, keepdims=True)
    xc = gv - mu
    rstd = lax.rsqrt(jnp.mean(xc * xc, axis=-1, keepdims=True) + NORM_EPS)
    xn = xc * rstd
    ln = xn * lng + lnb
    mixed = []
    for col in range(4):
        lc = ln[:, col * 128:(col + 1) * 128]
        lo = jnp.where(low, lc, 0.0).astype(BF16)
        hi = jnp.where(low, 0.0, lc).astype(BF16)
        mixed.append(_dot(w_ref[2 * col], lo) + _dot(w_ref[2 * col + 1], hi) + bsx[:, col * 128:(col + 1) * 128])
    return gu, ln, xn, rstd, mixed


def _mix_fwd(q, kv, sinks, zu, zv, lng, lnb, wtril, bsx, h, ga, gg, gpost, wb, layer):
    t = q.shape[0]
    tq = _tile(t, 512)
    blocks = tq // BLK

    def body(sink_ref, q_ref, kvc_ref, kvp_ref, zu_ref, zv_ref, g_ref, b_ref, w_ref, bs_ref, h_ref, ga_ref, gg_ref,
             gp_ref, wo_ref, attn_ref, gm_ref, heads_ref, mix_ref, h1_ref):
        low, vcur, dist, valid = _attn_masks(pl.program_id(0))
        kp, vp = _kv_variants(kvp_ref, slice(None), low)

        def project(rows, mix):
            mix_ref[rows, :] = mix.astype(BF16)
            h1_ref[rows, :] = h_ref[rows, :] + _rms_fwd(mix, gp_ref[...])

        pending = None
        for b in range(blocks):
            rows = slice(b * BLK, (b + 1) * BLK)
            kc, vc = _kv_variants(kvc_ref, rows, low)
            ks = [_stack_kv(kp, kc, g) for g in range(2)]
            vs = [_stack_kv(vp, vc, g) for g in range(2)]
            scores = _attn_scores(q_ref, rows, ks, vcur)
            if pending is not None:
                project(pending[0], _dot(pending[1], wo_ref[...]))
            gu, _, _, _, mixed = _gm_forward_block(zu_ref[rows, :], zv_ref[rows, :], g_ref[...], b_ref[...], w_ref,
                                                   bs_ref[...], low)
            probs = [_attn_probs(scores[h], h, sink_ref[h], dist, valid if b == 0 else None)[0] for h in range(8)]
            attn_cols, gm_cols = [], []
            for col in range(4):
                gm_cols.append((gu[:, col * 128:(col + 1) * 128] * mixed[col]).astype(BF16))
                p_col = jnp.concatenate(_split_cols(probs[2 * col], vcur) + _split_cols(probs[2 * col + 1], vcur), axis=1)
                attn_cols.append(_dot(p_col, vs[col // 2]).astype(BF16))
            attn = jnp.concatenate(attn_cols, axis=1)
            gm = jnp.concatenate(gm_cols, axis=1)
            attn_ref[rows, :] = attn
            gm_ref[rows, :] = gm
            heads = jnp.concatenate([_rms_fwd(attn.astype(F32), ga_ref[...]).astype(BF16),
                                     _rms_fwd(gm.astype(F32), gg_ref[...]).astype(BF16)], axis=1)
            heads_ref[rows, :] = heads
            pending = (rows, heads)
            kp, vp = kc, vc
        project(pending[0], _dot(pending[1], wo_ref[...]))

    wide = _row_spec(tq, GM_W)
    row = _row_spec(tq, D_MODEL)
    return pl.pallas_call(
        body, name="mix_fwd", grid=(t // tq,),
        in_specs=[pl.BlockSpec(memory_space=pltpu.SMEM), _row_spec(tq, ATTN_W), _row_spec(tq, 2 * KV_W),
                  _kv_prev_spec(blocks), wide, wide, _vec_spec(GM_W), _vec_spec(GM_W),
                  pl.BlockSpec((8, BLK, BLK), lambda i: (0, 0, 0)), pl.BlockSpec((BLK, GM_W), lambda i: (0, 0)),
                  row, _vec_spec(ATTN_W), _vec_spec(GM_W), _vec_spec(D_MODEL), _seg_spec(ROWS_B, D_MODEL, 2 * layer)],
        out_specs=[_row_spec(tq, ATTN_W), wide, row, row, row],
        out_shape=[jax.ShapeDtypeStruct((t, ATTN_W), BF16), jax.ShapeDtypeStruct((t, GM_W), BF16),
                   jax.ShapeDtypeStruct((t, D_MODEL), BF16), jax.ShapeDtypeStruct((t, D_MODEL), BF16),
                   jax.ShapeDtypeStruct((t, D_MODEL), F32)],
        compiler_params=_params("parallel"),
    )(sinks, q, kv, kv, zu, zv, lng, lnb, wtril, bsx, h, ga, gg, gpost, wb)


def _ffn_fwd(h1, p, p_layer, gpre, gpost, gple, wa, wb, wp, layer, target=None):
    t = h1.shape[0]
    tm = _tile(t, 256)

    def body(h_ref, p_ref, gpre_ref, gpost_ref, gple_ref, wg_ref, wu_ref, wd_ref, wpg_ref, wpl_ref, *rest):
        if target is None:
            f_ref, gp_ref, up_ref, act_ref, fo_ref, h2_ref, hn_ref, gate_ref, h3_ref = rest
        else:
            t_ref, f_ref, gp_ref, up_ref, act_ref, fo_ref, h2_ref, hn_ref, gate_ref, dy_ref, l_ref = rest
            _zero_at(pl.program_id(0) == 0, l_ref)
        h = h_ref[...]
        pe = _dot_nt(p_ref[...].astype(BF16), wpl_ref[...])
        f = _rms_fwd(h, gpre_ref[...]).astype(BF16)
        f_ref[...] = f
        chunks = [slice(j * FF_CHUNK, (j + 1) * FF_CHUNK) for j in range(D_FF // FF_CHUNK)]
        fo = None
        gp, up = _dot_nt(f, wg_ref[chunks[0], :]), _dot_nt(f, wu_ref[chunks[0], :])
        for j, cols in enumerate(chunks):
            if j + 1 < len(chunks):
                gp_next, up_next = _dot_nt(f, wg_ref[chunks[j + 1], :]), _dot_nt(f, wu_ref[chunks[j + 1], :])
            act = (gp * _sigmoid(gp) * up).astype(BF16)
            gp_ref[:, cols] = gp.astype(BF16)
            up_ref[:, cols] = up.astype(BF16)
            act_ref[:, cols] = act
            part = _dot(act, wd_ref[cols, :])
            fo = part if fo is None else fo + part
            if j + 1 < len(chunks):
                gp, up = gp_next, up_next
        fo_ref[...] = fo
        h2 = h + _rms_fwd(fo, gpost_ref[...])
        h2_ref[...] = h2
        hn = _rms_fwd(h2, gple_ref[...]).astype(BF16)
        hn_ref[...] = hn
        gate = _sigmoid(_dot(hn, wpg_ref[...]))
        gate_ref[...] = gate.astype(BF16)
        h3 = h2 + pe * gate
        if target is None:
            h3_ref[...] = h3
        else:
            e = h3 - t_ref[...]
            dy_ref[...] = (e * (1.0 / D_MODEL)).astype(BF16)
            s = jnp.sum(jnp.sum(e * e, axis=1, keepdims=True), axis=0, keepdims=True)
            l_ref[...] += jnp.broadcast_to(s, (1, 128))

    wide = _row_spec(tm, D_FF)
    row = _row_spec(tm, D_MODEL)
    vec = _vec_spec(D_MODEL)
    last = target is not None
    return pl.pallas_call(
        body, name="ffn_loss" if last else "ffn_fwd", grid=(t // tm,),
        in_specs=[row, _layer_row_spec(tm, PLE_DIM, p_layer), vec, vec, vec, _seg_spec(ROWS_A, D_MODEL, 3 * layer),
                  _seg_spec(ROWS_A, D_MODEL, 3 * layer + 1), _seg_spec(ROWS_A, D_MODEL, 3 * layer + 2),
                  _seg_spec(ROWS_B, D_MODEL, 2 * layer + 1), _seg_spec(ROWS_B, PLE_DIM, layer)] + [row] * last,
        out_specs=[row, wide, wide, wide, row, row, row, row, row] + [_vec_spec(128)] * last,
        out_shape=[jax.ShapeDtypeStruct((t, D_MODEL), BF16)] + [jax.ShapeDtypeStruct((t, D_FF), BF16)] * 3
        + [jax.ShapeDtypeStruct((t, D_MODEL), F32)] * 2 + [jax.ShapeDtypeStruct((t, D_MODEL), BF16)] * 2
        + [jax.ShapeDtypeStruct((t, D_MODEL), BF16 if last else F32)] + [jax.ShapeDtypeStruct((1, 128), F32)] * last,
        compiler_params=_params("arbitrary" if last else "parallel"),
    )(h1, p, gpre, gpost, gple, wa, wa, wa, wb, wp, *([target] if last else []))


def _ffn_bwd(dh3, h2, gate, p, p_layer, fo, gp, up, h1, gple, gpost, gpre, wa, wb, wp, layer, after):
    t = dh3.shape[0]
    tm = _tile(t, 256)

    def body(d3_ref, h2_ref, gate_ref, p_ref, fo_ref, gp_ref, up_ref, h_ref, gple_ref, gpost_ref, gpre_ref,
             wg_ref, wu_ref, wd_ref, wpg_ref, wpl_ref, after_ref,
             dgl_ref, dpe_ref, dfo_ref, dgp_ref, dup_ref, dh1_ref, dgple_ref, dgpost_ref, dgpre_ref):
        _zero_at(pl.program_id(0) == 0, dgple_ref, dgpost_ref, dgpre_ref)
        d3 = d3_ref[...].astype(F32)
        gate = gate_ref[...].astype(F32)
        pe = _dot_nt(p_ref[...].astype(BF16), wpl_ref[...])
        dpe_ref[...] = (d3 * gate).astype(BF16)
        dgl = (d3 * pe * gate * (1.0 - gate)).astype(BF16)
        dgl_ref[...] = dgl
        dx2, dgple = _rms_bwd(h2_ref[...], gple_ref[...], _dot_nt(dgl, wpg_ref[...]))
        dgple_ref[...] += dgple
        d = d3 + dx2
        dfo, dgpost = _rms_bwd(fo_ref[...], gpost_ref[...], d)
        dfo = dfo.astype(BF16)
        dfo_ref[...] = dfo
        dgpost_ref[...] += dgpost
        chunks = [slice(j * FF_CHUNK, (j + 1) * FF_CHUNK) for j in range(D_FF // FF_CHUNK)]
        df = None
        dact = _dot_nt(dfo, wd_ref[chunks[0], :])
        for j, cols in enumerate(chunks):
            if j + 1 < len(chunks):
                dact_next = _dot_nt(dfo, wd_ref[chunks[j + 1], :])
            gp = gp_ref[:, cols].astype(F32)
            sg = _sigmoid(gp)
            dgp = (dact * up_ref[:, cols].astype(F32) * (sg * (1.0 + gp * (1.0 - sg)))).astype(BF16)
            dup = (dact * (gp * sg)).astype(BF16)
            dgp_ref[:, cols] = dgp
            dup_ref[:, cols] = dup
            part = _dot(dgp, wg_ref[cols, :]) + _dot(dup, wu_ref[cols, :])
            df = part if df is None else df + part
            if j + 1 < len(chunks):
                dact = dact_next
        dx, dgpre = _rms_bwd(h_ref[...], gpre_ref[...], df)
        dh1_ref[...] = (d + dx).astype(BF16)
        dgpre_ref[...] += dgpre

    wide = _row_spec(tm, D_FF)
    row = _row_spec(tm, D_MODEL)
    vec = _vec_spec(D_MODEL)
    narrow = jax.ShapeDtypeStruct((t, D_MODEL), BF16)
    return pl.pallas_call(
        body, name="ffn_bwd", grid=(t // tm,),
        in_specs=[row, row, row, _layer_row_spec(tm, PLE_DIM, p_layer), row, wide, wide, row, vec, vec, vec,
                  _seg_spec(ROWS_A, D_MODEL, 3 * layer), _seg_spec(ROWS_A, D_MODEL, 3 * layer + 1),
                  _seg_spec(ROWS_A, D_MODEL, 3 * layer + 2), _seg_spec(ROWS_B, D_MODEL, 2 * layer + 1),
                  _seg_spec(ROWS_B, PLE_DIM, layer), pl.BlockSpec(memory_space=pl.ANY)],
        out_specs=[row, row, row, wide, wide, row, vec, vec, vec],
        out_shape=[narrow, narrow, narrow, jax.ShapeDtypeStruct((t, D_FF), BF16), jax.ShapeDtypeStruct((t, D_FF), BF16),
                   narrow] + [jax.ShapeDtypeStruct((1, D_MODEL), F32)] * 3,
        compiler_params=_params("arbitrary"),
    )(dh3, h2, gate, p, fo, gp, up, h1, gple, gpost, gpre, wa, wa, wa, wb, wp, after)


def _out_proj_bwd(dh1, mix, attn, gm, gpost, ga, gg, wb, layer, after):
    t = dh1.shape[0]
    tm = _tile(t, 512)

    def body(d_ref, mix_ref, a_ref, m_ref, gp_ref, ga_ref, gg_ref, w_ref, after_ref,
             dmix_ref, da_ref, dm_ref, dgp_ref, dga_ref, dgg_ref):
        _zero_at(pl.program_id(0) == 0, dgp_ref, dga_ref, dgg_ref)
        dmix, dgp = _rms_bwd(mix_ref[...].astype(F32), gp_ref[...], d_ref[...].astype(F32))
        dmix = dmix.astype(BF16)
        dmix_ref[...] = dmix
        da, dga = _rms_bwd(a_ref[...].astype(F32), ga_ref[...], _dot_nt(dmix, w_ref[0:512, :]))
        dm, dgg = _rms_bwd(m_ref[...].astype(F32), gg_ref[...], _dot_nt(dmix, w_ref[512:1024, :]))
        da_ref[...] = da.astype(BF16)
        dm_ref[...] = dm
        dgp_ref[...] += dgp
        dga_ref[...] += dga
        dgg_ref[...] += dgg

    return pl.pallas_call(
        body, name="out_proj_bwd", grid=(t // tm,),
        in_specs=[_row_spec(tm, D_MODEL), _row_spec(tm, D_MODEL), _row_spec(tm, ATTN_W), _row_spec(tm, GM_W),
                  _vec_spec(D_MODEL), _vec_spec(ATTN_W), _vec_spec(GM_W), _seg_spec(ROWS_B, D_MODEL, 2 * layer),
                  pl.BlockSpec(memory_space=pl.ANY)],
        out_specs=[_row_spec(tm, D_MODEL), _row_spec(tm, ATTN_W), _row_spec(tm, GM_W),
                   _vec_spec(D_MODEL), _vec_spec(ATTN_W), _vec_spec(GM_W)],
        out_shape=[jax.ShapeDtypeStruct((t, D_MODEL), BF16), jax.ShapeDtypeStruct((t, ATTN_W), BF16),
                   jax.ShapeDtypeStruct((t, GM_W), F32), jax.ShapeDtypeStruct((1, D_MODEL), F32),
                   jax.ShapeDtypeStruct((1, ATTN_W), F32), jax.ShapeDtypeStruct((1, GM_W), F32)],
        compiler_params=_params("arbitrary"),
    )(dh1, mix, attn, gm, gpost, ga, gg, wb, after)


def _split3(x):
    hi = x.astype(BF16)
    r1 = x - hi.astype(F32)
    mid = r1.astype(BF16)
    lo = (r1 - mid.astype(F32)).astype(BF16)
    return hi, mid, lo


def _rider_pieces(rider, blocks):
    rm = rider[0].shape[1]
    per = -(-rm // (blocks * 256)) * 256
    return [slice(k * per, min((k + 1) * per, rm)) for k in range(blocks) if k * per < rm]


def _gm_bwd(dgm, zu, zv, lng, lnb, wtril, bsx, rider=None):
    t = zu.shape[0]
    tm = _tile(t, 512)
    nb = t // tm
    if rider is not None:
        ra, rb, rbuf, rseg = rider
        rm, rn = ra.shape[1], rb.shape[1]
        pieces = _rider_pieces(rider, tm // BLK)

    def body(d_ref, zu_ref, zv_ref, g_ref, b_ref, w_ref, bs_ref, *rest):
        if rider is None:
            dzu_ref, dzv_ref, dw_ref, dbs_ref, dlg_ref, dlb_ref, dbsx_ref = rest
        else:
            ra_ref, rb_ref, rbuf_ref, dzu_ref, dzv_ref, dw_ref, dbs_ref, dlg_ref, dlb_ref, ro_ref, dbsx_ref, acc_ref = rest
        i = pl.program_id(0)
        _zero_at(i == 0, dw_ref, dlg_ref, dlb_ref, dbsx_ref)
        if rider is not None:
            _zero_at(i == 0, acc_ref)
            rb16 = rb_ref[...].astype(BF16)
        row = lax.broadcasted_iota(jnp.int32, (BLK, BLK), 0)
        lane = lax.broadcasted_iota(jnp.int32, (BLK, BLK), 1)
        low = lane < 64
        tril = row >= lane
        lng = g_ref[...]
        for b in range(tm // BLK):
            rows = slice(b * BLK, (b + 1) * BLK)
            if rider is not None and b < len(pieces):
                acc_ref[pieces[b], :] += _dot_tn(ra_ref[:, pieces[b]], rb16)
            zu = zu_ref[rows, :]
            zv = zv_ref[rows, :]
            gu, ln, xn, rstd, mixed = _gm_forward_block(zu, zv, lng, b_ref[...], w_ref, bs_ref[...], low)
            dgm = d_ref[rows, :]
            dgu_cols, dmx_cols, dln_cols = [], [], []
            for col in range(4):
                sl = slice(col * 128, (col + 1) * 128)
                dg = dgm[:, sl]
                dgu_cols.append(dg * mixed[col])
                dmx = dg * gu[:, sl]
                dmx_cols.append(dmx)
                lc = ln[:, sl]
                halves = (jnp.where(low, lc, 0.0).astype(BF16), jnp.where(low, 0.0, lc).astype(BF16))
                dmx16 = dmx.astype(BF16)
                dmx_half = (jnp.where(low, dmx, 0.0).astype(BF16), jnp.where(low, 0.0, dmx).astype(BF16))
                dln = None
                for half in range(2):
                    hd = 2 * col + half
                    dw_ref[hd] += jnp.where(tril, _dot_nt(dmx16, halves[half]), 0.0)
                    part = _dot_tn(w_ref[hd], dmx_half[half])
                    dln = part if dln is None else dln + part
                dln_cols.append(dln)
            dgu = jnp.concatenate(dgu_cols, axis=1)
            dmx = jnp.concatenate(dmx_cols, axis=1)
            dln = jnp.concatenate(dln_cols, axis=1)
            dzu_ref[rows, :] = (dgu * _gelu_grad(zu)).astype(BF16)
            dbsx_ref[...] += dmx
            dlg_ref[...] += jnp.sum(dln * xn, axis=0, keepdims=True)
            dlb_ref[...] += jnp.sum(dln, axis=0, keepdims=True)
            dxn = dln * lng
            dgv = rstd * (dxn - jnp.mean(dxn, axis=-1, keepdims=True) - xn * jnp.mean(dxn * xn, axis=-1, keepdims=True))
            dzv_ref[rows, :] = (dgv * _gelu_grad(zv)).astype(BF16)

        @pl.when(i == nb - 1)
        def _():
            r = lax.broadcasted_iota(jnp.int32, (GM_W, BLK), 0)
            c = lax.broadcasted_iota(jnp.int32, (GM_W, BLK), 1)
            e = jnp.where(jnp.logical_and(r >= c * 64, r < c * 64 + 64), 1.0, 0.0).astype(BF16)
            hi, mid, lo = _split3(dbsx_ref[...])
            dbs_ref[...] = _dot(hi, e) + _dot(mid, e) + _dot(lo, e)
            if rider is not None:
                ro_ref[...] = acc_ref[...].astype(ro_ref.dtype)

    vec = _vec_spec(GM_W)
    in_specs = [_row_spec(tm, GM_W)] * 3 + [vec, vec, pl.BlockSpec((8, BLK, BLK), lambda i: (0, 0, 0)),
                                            pl.BlockSpec((BLK, GM_W), lambda i: (0, 0))]
    out_specs = [_row_spec(tm, GM_W), _row_spec(tm, GM_W), pl.BlockSpec((8, BLK, BLK), lambda i: (0, 0, 0)),
                 pl.BlockSpec((BLK, BLK), lambda i: (0, 0)), vec, vec]
    out_shape = [jax.ShapeDtypeStruct((t, GM_W), BF16), jax.ShapeDtypeStruct((t, GM_W), BF16),
                 jax.ShapeDtypeStruct((8, BLK, BLK), F32), jax.ShapeDtypeStruct((BLK, BLK), F32),
                 jax.ShapeDtypeStruct((1, GM_W), F32), jax.ShapeDtypeStruct((1, GM_W), F32)]
    scratch = [pltpu.VMEM((BLK, GM_W), F32)]
    operands = [dgm, zu, zv, lng, lnb, wtril, bsx]
    extra = {}
    if rider is not None:
        in_specs += [_row_spec(tm, rm), _row_spec(tm, rn), pl.BlockSpec(memory_space=pl.ANY)]
        out_specs.append(pl.BlockSpec((rm, rn), lambda i: (rseg, 0)))
        out_shape.append(jax.ShapeDtypeStruct(rbuf.shape, rbuf.dtype))
        scratch.append(pltpu.VMEM((rm, rn), F32))
        operands += [ra, rb, rbuf]
        extra = dict(input_output_aliases={9: 6})
    return pl.pallas_call(
        body, name="gm_bwd" if rider is None else "gm_bwd_rider", grid=(nb,),
        in_specs=in_specs, out_specs=out_specs, out_shape=out_shape, scratch_shapes=scratch,
        compiler_params=_params("arbitrary"), **extra,
    )(*operands)


def _attn_bwd(q, kv, do, sinks, after, rider=None):
    t = q.shape[0]
    tq = _tile(t, 512)
    blocks = tq // BLK
    steps = t // tq
    if rider is not None:
        ra, rb, rbuf, rseg = rider
        rm, rn = ra.shape[1], rb.shape[1]
        per = -(-rm // (blocks * 256)) * 256
        pieces = [slice(k * per, min((k + 1) * per, rm)) for k in range(blocks)]

    def body(sink_ref, q_ref, kvc_ref, kvp_ref, do_ref, after_ref, *rest):
        if rider is None:
            dq_ref, dkv_ref, dkf_ref, ds_ref = rest
        else:
            ra_ref, rb_ref, rbuf_ref, dq_ref, dkv_ref, dkf_ref, ds_ref, ro_ref, acc_ref = rest
        i = pl.program_id(0)
        _zero_at(i == 0, ds_ref)
        if rider is not None:
            _zero_at(i == 0, acc_ref)
            rb16 = rb_ref[...].astype(BF16)
        low, vcur, dist, valid = _attn_masks(i)
        head_row = lax.broadcasted_iota(jnp.int32, (8, 128), 0)
        dsink_tile = jnp.zeros((8, 128), F32)
        kp, vp = _kv_variants(kvp_ref, slice(None), low)
        own = None
        for b in range(blocks):
            rows = slice(b * BLK, (b + 1) * BLK)
            kc, vc = _kv_variants(kvc_ref, rows, low)
            scores = _attn_scores_by_head(q_ref, rows, kc, kp, vcur)
            dprobs = _attn_scores_by_head(do_ref, rows, vc, vp, vcur)
            if rider is not None and pieces[b].start < rm:
                acc_ref[pieces[b], :] += _dot_tn(ra_ref[:, pieces[b]], rb16)
            parts = []
            for h in range(8):
                p, ps = _attn_probs(scores[h], h, sink_ref[h], dist, valid if b == 0 else None)
                delta = jnp.sum(p * dprobs[h], axis=1, keepdims=True)
                ds = p * (dprobs[h] - delta) * ATTN_SCALE
                dsink = jnp.sum(-ps * delta, axis=0, keepdims=True)
                dsink_tile = jnp.where(head_row == h, dsink_tile + dsink, dsink_tile)
                parts.append(_split_cols(ds, vcur) + _split_cols(p, vcur))
            acc = {}

            def add(name, key, val):
                acc[(name, key)] = val if (name, key) not in acc else acc[(name, key)] + val

            for col in range(4):
                qh = q_ref[rows, col * 128:(col + 1) * 128]
                doh = do_ref[rows, col * 128:(col + 1) * 128]
                dq = None
                for half in range(2):
                    key = _head_key(2 * col + half)
                    dsp, dsc, pp, pc = parts[2 * col + half]
                    part = _dot(dsc, kc[key]) + _dot(dsp, kp[key])
                    dq = part if dq is None else dq + part
                    add("kc", key, _dot_tn(dsc, qh))
                    add("kp", key, _dot_tn(dsp, qh))
                    add("vc", key, _dot_tn(pc, doh))
                    add("vp", key, _dot_tn(pp, doh))
                dq_ref[rows, col * 128:(col + 1) * 128] = dq.astype(BF16)

            def place(name):
                head0 = acc[(name, (0, 0))] + pltpu.roll(acc[(name, (0, 1))], 64, axis=1)
                head1 = pltpu.roll(acc[(name, (1, 0))], 64, axis=1) + acc[(name, (1, 1))]
                return jnp.where(low, head0, head1)

            before = (place("kp"), place("vp"))
            if b == 0:
                dkf_ref[:, 0:128], dkf_ref[:, 128:256] = before
            else:
                last = slice((b - 1) * BLK, b * BLK)
                dkv_ref[last, 0:128] = own[0] + before[0]
                dkv_ref[last, 128:256] = own[1] + before[1]
            own = (place("kc"), place("vc"))
            kp, vp = kc, vc
        final = slice((blocks - 1) * BLK, blocks * BLK)
        dkv_ref[final, 0:128], dkv_ref[final, 128:256] = own
        ds_ref[...] += dsink_tile
        if rider is not None:
            @pl.when(i == steps - 1)
            def _():
                ro_ref[...] = acc_ref[...].astype(ro_ref.dtype)

    row_q = _row_spec(tq, ATTN_W)
    row_kv = _row_spec(tq, 2 * KV_W)
    in_specs = [pl.BlockSpec(memory_space=pltpu.SMEM), row_q, row_kv, _kv_prev_spec(blocks), row_q,
                pl.BlockSpec(memory_space=pl.ANY)]
    out_specs = [row_q, row_kv, _row_spec(BLK, 2 * KV_W), pl.BlockSpec((8, 128), lambda i: (0, 0))]
    out_shape = [jax.ShapeDtypeStruct((t, ATTN_W), BF16), jax.ShapeDtypeStruct((t, 2 * KV_W), F32),
                 jax.ShapeDtypeStruct((t // tq * BLK, 2 * KV_W), F32), jax.ShapeDtypeStruct((8, 128), F32)]
    operands = [sinks, q, kv, kv, do, after]
    extra = {}
    if rider is not None:
        in_specs += [_row_spec(tq, rm), _row_spec(tq, rn), pl.BlockSpec(memory_space=pl.ANY)]
        out_specs.append(pl.BlockSpec((rm, rn), lambda i: (rseg, 0)))
        out_shape.append(jax.ShapeDtypeStruct(rbuf.shape, rbuf.dtype))
        operands += [ra, rb, rbuf]
        extra = dict(scratch_shapes=[pltpu.VMEM((rm, rn), F32)], input_output_aliases={8: 4})
    return pl.pallas_call(
        body, name="attn_bwd" if rider is None else "attn_bwd_rider", grid=(steps,),
        in_specs=in_specs, out_specs=out_specs, out_shape=out_shape,
        compiler_params=_params("arbitrary"), **extra,
    )(*operands)


def _in_proj_bwd(dq, dkv, dkf, dzu, dzv, h, a, dres, g, wc, layer, dh_dtype):
    t = h.shape[0]
    tm = _tile(t, 512)
    steps = t // tm
    pieces = ((0, 512), (512, 768), (768, 1280), (1280, 1792))

    def body(dq_ref, dkv_ref, dkn_ref, dzu_ref, dzv_ref, h_ref, a_ref, d_ref, g_ref, w_ref,
             dh_ref, dg_ref, gw_ref, acc_ref):
        i = pl.program_id(0)
        _zero_at(i == 0, dg_ref, acc_ref)
        tail = dkv_ref[tm - BLK:tm, :] + jnp.where(i < steps - 1, dkn_ref[...], 0.0)
        dkv = tail if tm == BLK else jnp.concatenate([dkv_ref[0:tm - BLK, :], tail], axis=0)
        dz = (dq_ref[...], dkv.astype(BF16), dzu_ref[...], dzv_ref[...])
        da = None
        for part, (lo, hi) in zip(dz, pieces):
            term = _dot(part, w_ref[lo:hi, :])
            da = term if da is None else da + term
        a16 = a_ref[...]
        for part, (lo, hi) in zip(dz, pieces):
            acc_ref[lo:hi, :] += _dot_tn(part, a16)
        dx, dg = _rms_bwd(h_ref[...], g_ref[...], da)
        dh_ref[...] = (d_ref[...].astype(F32) + dx).astype(dh_dtype)
        dg_ref[...] += dg

        @pl.when(i == steps - 1)
        def _():
            gw_ref[...] = acc_ref[...].astype(BF16)

    return pl.pallas_call(
        body, name="in_proj_bwd", grid=(t // tm,),
        in_specs=[_row_spec(tm, ATTN_W), _row_spec(tm, 2 * KV_W),
                  pl.BlockSpec((BLK, 2 * KV_W), lambda i: (jnp.minimum(i + 1, steps - 1), 0)), _row_spec(tm, GM_W),
                  _row_spec(tm, GM_W), _row_spec(tm, D_MODEL), _row_spec(tm, D_MODEL), _row_spec(tm, D_MODEL),
                  _vec_spec(D_MODEL), _seg_spec(ROWS_C, D_MODEL, layer)],
        out_specs=[_row_spec(tm, D_MODEL), _vec_spec(D_MODEL), pl.BlockSpec((D_IN, D_MODEL), lambda i: (0, 0))],
        out_shape=[jax.ShapeDtypeStruct((t, D_MODEL), dh_dtype), jax.ShapeDtypeStruct((1, D_MODEL), F32),
                   jax.ShapeDtypeStruct((D_IN, D_MODEL), BF16)],
        scratch_shapes=[pltpu.VMEM((D_IN, D_MODEL), F32)],
        compiler_params=_params("arbitrary"),
    )(dq, dkv, dkf, dzu, dzv, h, a, dres, g, wc)


def _weight_grad(a, b, buf, seg, b_layer=None):
    t, m = a.shape
    n = b.shape[-1]
    assert buf.shape[0] % m == 0 and buf.shape[1] == n
    tm = _tile(t, WGRAD_TOKENS)
    steps = t // tm
    half = m // 2

    def body(a_ref, b_ref, buf_ref, o_ref, acc_ref):
        i = pl.program_id(0)
        _zero_at(i == 0, acc_ref)
        b16 = b_ref[...].astype(BF16)
        for rows in (slice(0, half), slice(half, m)):
            acc_ref[rows, :] += _dot_tn(a_ref[:, rows], b16)

        @pl.when(i == steps - 1)
        def _():
            o_ref[...] = acc_ref[...].astype(o_ref.dtype)

    return pl.pallas_call(
        body, name="weight_grad", grid=(steps,),
        in_specs=[_row_spec(tm, m), _row_spec(tm, n) if b_layer is None else _layer_row_spec(tm, n, b_layer),
                  pl.BlockSpec(memory_space=pl.ANY)],
        out_specs=pl.BlockSpec((m, n), lambda i: (seg, 0)),
        out_shape=jax.ShapeDtypeStruct(buf.shape, buf.dtype),
        scratch_shapes=[pltpu.VMEM((m, n), F32)],
        input_output_aliases={2: 0},
        compiler_params=_params("arbitrary"),
    )(a, b, buf)


def _rows8(rows):
    return [jnp.pad(r, ((0, 7), (0, 0))) for r in rows]


def _small_pack_gating(d):
    rows = [jnp.concatenate([d["gm_ln_g"], d["gm_ln_b"]], axis=1), d["gm_bs"].reshape(1, 1024)]
    return jnp.concatenate(_rows8(rows) + [d["gm_ws"].reshape(128, 1024)], axis=0)


def _small_pack_rest(d):
    rows = [d["ln_mix_pre"], d["ln_mix_post"], d["ln_ffn_pre"], d["ln_ffn_post"], d["ln_ple_gate"],
            jnp.concatenate([d["g_attn_out"], d["g_gm_out"]], axis=1),
            jnp.pad(d["attn_sinks"].reshape(1, 8), ((0, 0), (0, 1016)))]
    return jnp.concatenate(_rows8(rows), axis=0)


def _unpack_gating(g):
    return {"gm_ln_g": g[:, 0, :512], "gm_ln_b": g[:, 0, 512:], "gm_bs": g[:, 8].reshape(DEPTH, 8, 128),
            "gm_ws": g[:, 16:GATING_ROWS].reshape(DEPTH, 8, 128, 128)}


def _unpack_rest(s):
    return {"ln_mix_pre": s[:, 0], "ln_mix_post": s[:, 8], "ln_ffn_pre": s[:, 16], "ln_ffn_post": s[:, 24],
            "ln_ple_gate": s[:, 32], "g_attn_out": s[:, 40, :512], "g_gm_out": s[:, 40, 512:], "attn_sinks": s[:, 48, :8]}


def _row(v):
    return v.reshape(1, -1)


def _layer_fwd(h, p, sp, l, weights, target=None):
    tril = jnp.tril(jnp.ones((BLK, BLK), bool))
    wtril = jnp.where(tril[None], sp["gm_ws"][l], 0.0).astype(BF16)
    bsx = jnp.repeat(sp["gm_bs"][l].T, HEAD_DIM, axis=1)
    a, q, kv, zu, zv = _in_proj(h, _row(sp["ln_mix_pre"][l]), weights("c", h), 0)
    wb = weights("b", zu)
    attn, gm, heads, mix, h1 = _mix_fwd(
        q, kv, sp["attn_sinks"][l], zu, zv, _row(sp["gm_ln_g"][l]), _row(sp["gm_ln_b"][l]), wtril, bsx, h,
        _row(sp["g_attn_out"][l]), _row(sp["g_gm_out"][l]), _row(sp["ln_mix_post"][l]), wb, 0)
    wa = weights("a", h1)
    f, gpre, up, act, fo, h2, hn, gate, *out = _ffn_fwd(
        h1, p, l, _row(sp["ln_ffn_pre"][l]), _row(sp["ln_ffn_post"][l]), _row(sp["ln_ple_gate"][l]), wa, wb,
        weights("p", zu), 0, target)
    saved = dict(h=h, a=a, q=q, kv=kv, zu=zu, zv=zv, attn=attn, gm=gm, heads=heads, mix=mix, h1=h1, f=f,
                 gpre=gpre, up=up, act=act, fo=fo, h2=h2, hn=hn, gate=gate, wtril=wtril, bsx=bsx)
    return (out[0] if target is None else tuple(out)), saved


def _layer_bwd_upper(dh, s, p, sp, l, wa, wb, wp, after, ride):
    d = {}
    dgl, dpe, dfo, dgp, dup, dh1, d["ln_ple_gate"], d["ln_ffn_post"], d["ln_ffn_pre"] = _ffn_bwd(
        dh, s["h2"], s["gate"], p, l, s["fo"], s["gpre"], s["up"], s["h1"], _row(sp["ln_ple_gate"][l]),
        _row(sp["ln_ffn_post"][l]), _row(sp["ln_ffn_pre"][l]), wa, wb, wp, 0, after)
    gb = _weight_grad(s["hn"], dgl, lax.empty((2 * D_MODEL, D_MODEL), BF16), 1)
    gp = _weight_grad(dpe, p, lax.empty((D_MODEL, PLE_DIM), BF16), 0, b_layer=l)
    ga = _weight_grad(s["act"], dfo, lax.empty((3 * D_FF, D_MODEL), BF16), 2)
    if not ride:
        ga = _weight_grad(dgp, s["f"], ga, 0)
        ga = _weight_grad(dup, s["f"], ga, 1)
    return (dh1, d, dgp, dup), ga, gp, gb


def _layer_bwd_middle(carry, s, sp, l, wb, gb, ga, after, ride):
    dh1, d, dgp, dup = carry
    dmix, dattn, dgm, d["ln_mix_post"], d["g_attn_out"], d["g_gm_out"] = _out_proj_bwd(
        dh1, s["mix"], s["attn"], s["gm"], _row(sp["ln_mix_post"][l]), _row(sp["g_attn_out"][l]),
        _row(sp["g_gm_out"][l]), wb, 0, after)
    gb = _weight_grad(s["heads"], dmix, gb, 0)
    dzu, dzv, d["gm_ws"], dbs, d["gm_ln_g"], d["gm_ln_b"], *rode = _gm_bwd(
        dgm, s["zu"], s["zv"], _row(sp["gm_ln_g"][l]), _row(sp["gm_ln_b"][l]), s["wtril"], s["bsx"],
        (dup, s["f"], ga, 1) if ride else None)
    d["gm_bs"] = dbs[:, :8].T
    return (dh1, dattn, dzu, dzv, d, dgp), gb, (rode[0] if ride else ga), _small_pack_gating(d)


def _layer_bwd_lower(carry, s, sp, l, wc, ga, after, ride):
    dh1, dattn, dzu, dzv, d, dgp = carry
    dq, dkv, dkf, dsink, *rode = _attn_bwd(s["q"], s["kv"], dattn, sp["attn_sinks"][l], after,
                                            (dgp, s["f"], ga, 0) if ride else None)
    ga = rode[0] if ride else ga
    d["attn_sinks"] = dsink[:, 0]
    dh, d["ln_mix_pre"], gc = _in_proj_bwd(dq, dkv, dkf, dzu, dzv, s["h"], s["a"], dh1, _row(sp["ln_mix_pre"][l]), wc,
                                           0, F32 if l == 0 else BF16)
    return dh, gc, ga, _small_pack_rest(d)


ANY = pl.BlockSpec(memory_space=pl.ANY)


HBM = pl.BlockSpec(memory_space=pltpu.HBM)
SEM = pl.BlockSpec(memory_space=pltpu.SEMAPHORE)
N_PEERS = N_DEV - 1


def _peers():
    x, y, c = lax.axis_index("x"), lax.axis_index("y"), lax.axis_index("c")
    peers = []
    for r in range(1, N_DEV):
        px = 1 - x if r & 4 else x
        py = 1 - y if r & 2 else y
        pc = 1 - c if r & 1 else c
        peers.append(((px, py, pc), 4 * px + 2 * py + pc))
    return 4 * x + 2 * y + c, peers


GATHER, SCATTER, SPREAD = "gather", "scatter", "spread"


def _peer_copy(src, land, send_sems, recv_sems, r, me, peer, peer_slot, mode):
    return pltpu.make_async_remote_copy(
        src_ref=src.at[:, pl.ds(peer_slot, 1)] if mode == SCATTER else src,
        dst_ref=land.at[:, pl.ds(me, 1)] if mode == GATHER else land.at[:, pl.ds(r - 1, 1)],
        send_sem=send_sems.at[r - 1], recv_sem=recv_sems.at[r - 1], device_id=peer, device_id_type=MESH)


def _peer_arrival(src, land, send_sems, recv_sems, r, me, peer, peer_slot, mode):
    return pltpu.make_async_remote_copy(
        src_ref=src.at[:, pl.ds(me, 1)] if mode == SCATTER else src,
        dst_ref=land.at[:, pl.ds(peer_slot, 1)] if mode == GATHER else land.at[:, pl.ds(r - 1, 1)],
        send_sem=send_sems.at[r - 1], recv_sem=recv_sems.at[r - 1], device_id=peer, device_id_type=MESH)


def _send_start(name, srcs, lands, modes):
    n = len(srcs)

    def body(*refs):
        src_refs, land_refs = refs[:n], refs[n:2 * n]
        outs = refs[2 * n:]
        send_sems, recv_sems, token = outs[2 * n:3 * n], outs[3 * n:4 * n], outs[4 * n]
        me, peers = _peers()
        for k in range(n):
            for r, (peer, slot) in enumerate(peers, 1):
                _peer_copy(src_refs[k], land_refs[k], send_sems[k], recv_sems[k], r, me, peer, slot, modes[k]).start()
        token[...] = jnp.zeros_like(token)

    hbm = lambda a: pltpu.HBM(a.shape, a.dtype)
    sems = [pltpu.SemaphoreType.DMA((N_PEERS,))] * (2 * n)
    outs = pl.pallas_call(
        body, name=name, in_specs=[HBM] * (2 * n),
        out_specs=[HBM] * (2 * n) + [SEM] * (2 * n) + [pl.BlockSpec(memory_space=pltpu.VMEM)],
        out_shape=[hbm(a) for a in srcs] + [hbm(a) for a in lands] + sems + [jax.ShapeDtypeStruct((8, 128), F32)],
        input_output_aliases={k: k for k in range(2 * n)},
        compiler_params=pltpu.CompilerParams(has_side_effects=pltpu.SideEffectType.DATAFLOW_SIDE_EFFECTING),
    )(*[pltpu.with_memory_space_constraint(a, pltpu.HBM) for a in list(srcs) + list(lands)])
    return dict(srcs=outs[:n], lands=outs[n:2 * n], send=outs[2 * n:3 * n], recv=outs[3 * n:4 * n],
                modes=list(modes)), outs[4 * n]


def _send_wait(name, sent, ks, after):
    n = len(ks)
    srcs = [sent["srcs"][k] for k in ks]
    lands = [sent["lands"][k] for k in ks]
    modes = [sent["modes"][k] for k in ks]

    def body(*refs):
        src_refs, land_refs = refs[:n], refs[n:2 * n]
        send_sems, recv_sems = refs[2 * n:3 * n], refs[3 * n:4 * n]
        me, peers = _peers()
        for k in range(n):
            for r, (peer, slot) in enumerate(peers, 1):
                args = (src_refs[k], land_refs[k], send_sems[k], recv_sems[k], r, me, peer, slot, modes[k])
                _peer_copy(*args).wait_send()
                _peer_arrival(*args).wait_recv()

    hbm = lambda a: pltpu.HBM(a.shape, a.dtype)
    outs = pl.pallas_call(
        body, name=name, in_specs=[HBM] * (2 * n) + [SEM] * (2 * n) + [ANY],
        out_specs=[HBM] * (2 * n), out_shape=[hbm(a) for a in srcs] + [hbm(a) for a in lands],
        input_output_aliases={k: k for k in range(2 * n)},
        compiler_params=pltpu.CompilerParams(has_side_effects=pltpu.SideEffectType.DATAFLOW_SIDE_EFFECTING),
    )(*srcs, *lands, *[sent["send"][k] for k in ks], *[sent["recv"][k] for k in ks], after)
    return outs[n:], outs[:n]


def _sum_blocks(own, land, ids, stacked, layer):
    nseg, _, rows, cols = land.shape
    if stacked is None:
        stacked = lax.empty((nseg, DEPTH, rows, cols), F32)

    def body(ids_ref, own_ref, land_ref, stacked_ref, o_ref):
        me = ids_ref[1]
        total = None
        for j in range(N_DEV):
            slot = jnp.maximum(jnp.bitwise_xor(me, j) - 1, 0)
            term = jnp.where(me == j, own_ref[...], land_ref[slot]).astype(F32)
            total = term if total is None else total + term
        o_ref[...] = total

    return pl.pallas_call(
        body, name="sum_blocks",
        grid_spec=pltpu.PrefetchScalarGridSpec(
            num_scalar_prefetch=1, grid=(nseg,),
            in_specs=[pl.BlockSpec((None, None, rows, cols), lambda s, ids: (s, ids[0], 0, 0)),
                      pl.BlockSpec((None, N_PEERS, rows, cols), lambda s, ids: (s, 0, 0, 0)),
                      pl.BlockSpec(memory_space=pl.ANY)],
            out_specs=pl.BlockSpec((None, None, rows, cols), lambda s, ids: (s, layer, 0, 0))),
        out_shape=jax.ShapeDtypeStruct(stacked.shape, F32),
        input_output_aliases={3: 0},
        compiler_params=_params("parallel"),
    )(ids, own, land, stacked)


def _adamw(w, g, m, v):
    shape = w.shape
    cols = shape[-1]
    rows = w.size // cols
    tr = rows
    for cand in (512, 256, 128, 64, 32, 16, 8):
        if rows % cand == 0:
            tr = cand
            break
    c1 = 1.0 / (1.0 - ADAM_B1 ** ADAM_STEP)
    c2 = 1.0 / (1.0 - ADAM_B2 ** ADAM_STEP)

    def body(w_ref, g_ref, m_ref, v_ref, d_ref, nm_ref, nv_ref):
        g = g_ref[...]
        m = ADAM_B1 * m_ref[...] + (1.0 - ADAM_B1) * g
        v = ADAM_B2 * v_ref[...] + (1.0 - ADAM_B2) * (g * g)
        nm_ref[...] = m
        nv_ref[...] = v
        d_ref[...] = -ADAM_LR * ((m * c1) / (jnp.sqrt(v * c2) + ADAM_EPS) + ADAM_WD * w_ref[...])

    spec = pl.BlockSpec((tr, cols), lambda i: (i, 0))
    outs = pl.pallas_call(
        body, name="adamw", grid=(rows // tr,),
        in_specs=[spec] * 4, out_specs=[spec] * 3,
        out_shape=[jax.ShapeDtypeStruct((rows, cols), F32)] * 3,
        compiler_params=_params("parallel"),
    )(*[a.reshape(rows, cols) for a in (w, g, m, v)])
    return [o.reshape(shape) for o in outs]


SMALL = ("ln_mix_pre", "attn_sinks", "gm_ln_g", "gm_ln_b", "gm_ws", "gm_bs", "g_attn_out", "g_gm_out",
         "ln_mix_post", "ln_ffn_pre", "ln_ffn_post", "ln_ple_gate")
WEIGHTS = ("ln_mix_pre", "w_in", "attn_sinks", "gm_ln_g", "gm_ln_b", "gm_ws", "gm_bs", "g_attn_out", "g_gm_out",
           "w_out", "ln_mix_post", "ln_ffn_pre", "w_ffn_gate", "w_ffn_up", "w_ffn_down", "ln_ffn_post", "w_ple",
           "ln_ple_gate", "w_ple_gate")


def _pack_shards(w, l):
    sa = jnp.stack([w["w_ffn_gate"][l].T, w["w_ffn_up"][l].T, w["w_ffn_down"][l]])[:, None]
    sb = jnp.stack([w["w_out"][l], w["w_ple_gate"][l]])[:, None]
    return [w["w_in"][l].T[None, None].astype(BF16), sb.astype(BF16), w["w_ple"][l].T[None, None].astype(BF16),
            sa.astype(BF16)]


def kernel(x, p, ln_mix_pre, w_in, attn_sinks, gm_ln_g, gm_ln_b, gm_ws, gm_bs, g_attn_out, g_gm_out, w_out, ln_mix_post, ln_ffn_pre, w_ffn_gate, w_ffn_up, w_ffn_down, ln_ffn_post, w_ple, ln_ple_gate, w_ple_gate, loss_target, m_ln_mix_pre, m_w_in, m_attn_sinks, m_gm_ln_g, m_gm_ln_b, m_gm_ws, m_gm_bs, m_g_attn_out, m_g_gm_out, m_w_out, m_ln_mix_post, m_ln_ffn_pre, m_w_ffn_gate, m_w_ffn_up, m_w_ffn_down, m_ln_ffn_post, m_w_ple, m_ln_ple_gate, m_w_ple_gate, v_ln_mix_pre, v_w_in, v_attn_sinks, v_gm_ln_g, v_gm_ln_b, v_gm_ws, v_gm_bs, v_g_attn_out, v_g_gm_out, v_w_out, v_ln_mix_post, v_ln_ffn_pre, v_w_ffn_gate, v_w_ffn_up, v_w_ffn_down, v_ln_ffn_post, v_w_ple, v_ln_ple_gate, v_w_ple_gate):
    given = dict(locals())
    w = {n: given[n] for n in WEIGHTS}
    sp = {n: w[n] for n in SMALL}
    kinds = ("c", "b", "p", "a")

    me, _ = _peers()
    shards = [s for l in range(DEPTH) for s in _pack_shards(w, l)]
    lands = [lax.dynamic_update_slice(lax.empty((s.shape[0], N_DEV) + s.shape[2:], BF16), s, (0, me, 0, 0))
             for s in shards]
    gather, token = _send_start("gather_start", shards, lands, [GATHER] * len(shards))
    layer_weights = [{} for _ in range(DEPTH)]

    def weights_of(l):
        def get(kind, after):
            have = layer_weights[l]
            if kind not in have:
                if l < 2:
                    group = {"c": ("c",), "b": ("b", "p"), "p": ("b", "p"), "a": ("a",)}[kind]
                    after = token if (l == 0 and kind == "c") else after
                else:
                    group = kinds
                got, _ = _send_wait(f"gather_wait_{l}{group[0]}", gather, [4 * l + kinds.index(k) for k in group], after)
                for k, g in zip(group, got):
                    have[k] = g.reshape(-1, g.shape[-1])
            return have[kind]
        return get

    h = x[0]
    p3 = p.reshape(DEPTH, -1, PLE_DIM)
    saved = []
    for l in range(DEPTH):
        h, s = _layer_fwd(h, p3, sp, l, weights_of(l), loss_target[0] if l == DEPTH - 1 else None)
        saved.append(s)
    dh, sq = h

    started = []
    after = token
    view = lambda g, rows: g.reshape(-1, N_DEV, rows, g.shape[-1])
    pack16 = lambda s: s.astype(BF16)[None, None]
    packs = ("gating", "rest")

    def send(l, tag, items):
        bufs = list(items.values())
        lands = [lax.empty((a.shape[0], N_PEERS) + a.shape[2:], BF16) for a in bufs]
        sent, tok = _send_start(f"reduce_start_{l}{tag}", bufs, lands, [SPREAD if k in packs else SCATTER for k in items])
        started.append((l, tag, sent, list(items)))
        return tok

    for l in reversed(range(DEPTH)):
        lw = layer_weights[l]
        ride = l > 0
        carry, ga, gp, gb = _layer_bwd_upper(dh, saved[l], p3, sp, l, lw["a"], lw["b"], lw["p"], after, ride)
        if not ride:
            after = send(l, "a", {"a": view(ga, ROWS_A)})
        carry, gb, ga, gating = _layer_bwd_middle(carry, saved[l], sp, l, lw["b"], gb, ga, after, ride)
        after = send(l, "b", {"b": view(gb, ROWS_B), "gating": pack16(gating), "p": view(gp, ROWS_B)})
        dh, gc, ga, rest = _layer_bwd_lower(carry, saved[l], sp, l, lw["c"], ga, after, ride)
        after = send(l, "c", {"c": view(gc, ROWS_C), "rest": pack16(rest), **({"a": view(ga, ROWS_A)} if ride else {})})

    mine = jnp.stack([me, me]).astype(jnp.int32)
    whole = jnp.stack([jnp.zeros_like(me), me]).astype(jnp.int32)
    sums = {}

    def collect(group, behind):
        l, tag, sent, keys = group
        lands, srcs = _send_wait(f"reduce_wait_{l}{tag}", sent, list(range(len(keys))), behind)
        for key, land, src in zip(keys, lands, srcs):
            sums[key] = _sum_blocks(src, land, whole if key in packs else mine, sums.get(key), l)

    for group in started[:-1]:
        collect(group, after)
    grad_x = dh
    loss = lax.psum(sq[0, 0] * (0.5 / D_MODEL), AXES)
    grads, delta, new_m, new_v = {}, {}, {}, {}
    row_form = {}
    flip = lambda a: jnp.swapaxes(a, 1, 2)

    def update(names):
        for n in names:
            if n in row_form:
                grads[n] = flip(row_form[n])
                outs = _adamw(flip(w[n]), row_form[n], flip(given["m_" + n]), flip(given["v_" + n]))
                delta[n], new_m[n], new_v[n] = [flip(o) for o in outs]
            else:
                delta[n], new_m[n], new_v[n] = _adamw(w[n], grads[n], given["m_" + n], given["v_" + n])

    row_form.update({"w_ffn_gate": sums["a"][0], "w_ffn_up": sums["a"][1]})
    grads.update({"w_ffn_down": sums["a"][2], "w_ple": flip(sums["p"][0]), "w_out": sums["b"][0], "w_ple_gate": sums["b"][1]})
    grads.update(_unpack_gating(sums["gating"][0]))
    early = tuple(row_form) + tuple(grads)
    update(early)
    collect(started[-1], jnp.concatenate([delta[n][(0,) * delta[n].ndim].reshape(1) for n in early]))
    row_form["w_in"] = sums["c"][0]
    late = _unpack_rest(sums["rest"][0])
    grads.update(late)
    update(["w_in", *late])
    return (loss, grad_x[None], *[grads[n] for n in WEIGHTS], *[delta[n] for n in WEIGHTS],
            *[new_m[n] for n in WEIGHTS], *[new_v[n] for n in WEIGHTS])
```

```python
import math

import jax
import jax.numpy as jnp
from jax import lax
from jax.experimental import pallas as pl
from jax.experimental.pallas import tpu as pltpu

F32 = jnp.float32
BF16 = jnp.bfloat16
MESH = pl.DeviceIdType.MESH
AXES = ("x", "y", "c")

D_MODEL = 1024
DEPTH = 4
N_DEV = 8
HEAD_DIM = 64
ATTN_W = 512
KV_W = 128
GM_W = 512
D_IN = 1792
D_FF = 2816
PLE_DIM = 256
BLK = 128
FF_CHUNK = 256
WGRAD_TOKENS = 1024
NORM_EPS = 1e-6
NEG_BIG = -1e30
ATTN_SCALE = HEAD_DIM ** -0.5

ADAM_LR = 0.001
ADAM_B1 = 0.9
ADAM_B2 = 0.999
ADAM_EPS = 1e-08
ADAM_WD = 0.01
ADAM_STEP = 10

ROWS_A = D_FF // N_DEV
ROWS_B = D_MODEL // N_DEV
ROWS_C = D_IN // N_DEV
GATING_ROWS = 144
REST_ROWS = 56

VMEM_LIMIT = 56 * 2 ** 20


def _params(*sem):
    return pltpu.CompilerParams(dimension_semantics=sem, vmem_limit_bytes=VMEM_LIMIT)


def _dot(a, b):
    return jnp.dot(a, b, preferred_element_type=F32)


def _dot_nt(a, b):
    return lax.dot_general(a, b, (((1,), (1,)), ((), ())), preferred_element_type=F32)


def _dot_tn(a, b):
    return lax.dot_general(a, b, (((0,), (0,)), ((), ())), preferred_element_type=F32)


def _rms_fwd(x, g):
    r = lax.rsqrt(jnp.mean(x * x, axis=-1, keepdims=True) + NORM_EPS)
    return x * r * g


def _rms_bwd(x, g, dy):
    r = lax.rsqrt(jnp.mean(x * x, axis=-1, keepdims=True) + NORM_EPS)
    xh = x * r
    dg = jnp.sum(dy * xh, axis=0, keepdims=True)
    dxh = dy * g
    dx = r * (dxh - xh * jnp.mean(dxh * xh, axis=-1, keepdims=True))
    return dx, dg


_GELU_C = math.sqrt(2.0 / math.pi)


def _gelu(x):
    t = jnp.tanh(_GELU_C * (x + 0.044715 * (x * x * x)))
    return 0.5 * x * (1.0 + t)


def _gelu_grad(x):
    x2 = x * x
    t = jnp.tanh(_GELU_C * (x + 0.044715 * (x2 * x)))
    return 0.5 * (1.0 + t) + 0.5 * x * (1.0 - t * t) * (_GELU_C * (1.0 + 3.0 * 0.044715 * x2))


def _sigmoid(x):
    return 1.0 / (1.0 + jnp.exp(-x))


def _row_spec(tm, n):
    return pl.BlockSpec((tm, n), lambda i: (i, 0))


def _layer_row_spec(tm, n, l):
    return pl.BlockSpec((None, tm, n), lambda i: (l, i, 0))


def _vec_spec(n):
    return pl.BlockSpec((1, n), lambda i: (0, 0))


def _seg_spec(rows, cols, seg):
    return pl.BlockSpec((N_DEV * rows, cols), lambda i: (seg, 0), pipeline_mode=pl.Buffered(1))


def _zero_at(first, *refs):
    @pl.when(first)
    def _():
        for r in refs:
            r[...] = jnp.zeros(r.shape, r.dtype)


def _tile(t, want):
    return min(t, want)


def _in_proj(h, g, wc, layer):
    t = h.shape[0]
    tm = _tile(t, 512)

    def body(h_ref, g_ref, w_ref, a_ref, q_ref, kv_ref, zu_ref, zv_ref):
        a = _rms_fwd(h_ref[...], g_ref[...]).astype(BF16)
        a_ref[...] = a
        q_ref[...] = _dot_nt(a, w_ref[0:512, :]).astype(BF16)
        kv_ref[...] = _dot_nt(a, w_ref[512:768, :]).astype(BF16)
        zu_ref[...] = _dot_nt(a, w_ref[768:1280, :])
        zv_ref[...] = _dot_nt(a, w_ref[1280:1792, :])

    return pl.pallas_call(
        body, name="in_proj", grid=(t // tm,),
        in_specs=[_row_spec(tm, D_MODEL), _vec_spec(D_MODEL), _seg_spec(ROWS_C, D_MODEL, layer)],
        out_specs=[_row_spec(tm, D_MODEL), _row_spec(tm, ATTN_W), _row_spec(tm, 2 * KV_W),
                   _row_spec(tm, GM_W), _row_spec(tm, GM_W)],
        out_shape=[jax.ShapeDtypeStruct((t, D_MODEL), BF16), jax.ShapeDtypeStruct((t, ATTN_W), BF16),
                   jax.ShapeDtypeStruct((t, 2 * KV_W), BF16), jax.ShapeDtypeStruct((t, GM_W), F32),
                   jax.ShapeDtypeStruct((t, GM_W), F32)],
        compiler_params=_params("parallel"),
    )(h, g, wc)


def _head_variants(x, low):
    xr = pltpu.roll(x, 64, axis=1)
    zero = jnp.zeros_like(x)
    return {
        (0, 0): jnp.where(low, x, zero).astype(BF16),
        (0, 1): jnp.where(low, zero, xr).astype(BF16),
        (1, 0): jnp.where(low, xr, zero).astype(BF16),
        (1, 1): jnp.where(low, zero, x).astype(BF16),
    }


def _attn_masks(i):
    row = lax.broadcasted_iota(jnp.int32, (BLK, BLK), 0)
    lane = lax.broadcasted_iota(jnp.int32, (BLK, BLK), 1)
    vcur = row >= lane
    dist = jnp.where(vcur, row - lane, row - lane + BLK).astype(F32)
    valid = jnp.logical_or(vcur, i > 0)
    return lane < 64, vcur, dist, valid


def _head_key(h):
    return (h // 4, h % 2)


def _stack_kv(prev, cur, g):
    return jnp.concatenate([prev[(g, 0)], cur[(g, 0)], prev[(g, 1)], cur[(g, 1)]], axis=0)


def _split_cols(p, vcur):
    return [jnp.where(vcur, 0.0, p).astype(BF16), jnp.where(vcur, p, 0.0).astype(BF16)]


def _attn_scores(q_ref, rows, stacked, vcur):
    out = []
    for col in range(4):
        big = _dot_nt(q_ref[rows, col * 128:(col + 1) * 128], stacked[col // 2])
        for half in range(2):
            out.append(jnp.where(vcur, big[:, half * 256 + 128:half * 256 + 256], big[:, half * 256:half * 256 + 128]))
    return out


def _attn_scores_by_head(q_ref, rows, kc, kp, vcur):
    out = []
    for h in range(8):
        qh = q_ref[rows, (h // 2) * 128:(h // 2 + 1) * 128]
        out.append(jnp.where(vcur, _dot_nt(qh, kc[_head_key(h)]), _dot_nt(qh, kp[_head_key(h)])))
    return out


def _attn_probs(s, h, sink, dist, valid):
    s = s * ATTN_SCALE - (2.0 ** -(h + 1)) * dist
    if valid is not None:
        s = jnp.where(valid, s, NEG_BIG)
    m = jnp.maximum(jnp.max(s, axis=1, keepdims=True), sink)
    e = jnp.exp(s - m)
    es = jnp.exp(sink - m)
    inv = 1.0 / (jnp.sum(e, axis=1, keepdims=True) + es)
    return e * inv, es * inv


def _kv_prev_spec(blocks):
    return pl.BlockSpec((BLK, 2 * KV_W), lambda i: (jnp.maximum(i * blocks - 1, 0), 0))


def _kv_variants(kv_ref, rows, low):
    return (_head_variants(kv_ref[rows, 0:128].astype(F32), low), _head_variants(kv_ref[rows, 128:256].astype(F32), low))


def _gm_forward_block(zu, zv, lng, lnb, w_ref, bsx, low):
    gu = _gelu(zu)
    gv = _gelu(zv)
    mu = jnp.mean(gv, axis=-1, keepdims=True)
    xc = gv - mu
    rstd = lax.rsqrt(jnp.mean(xc * xc, axis=-1, keepdims=True) + NORM_EPS)
    xn = xc * rstd
    ln = xn * lng + lnb
    mixed = []
    for col in range(4):
        lc = ln[:, col * 128:(col + 1) * 128]
        lo = jnp.where(low, lc, 0.0).astype(BF16)
        hi = jnp.where(low, 0.0, lc).astype(BF16)
        mixed.append(_dot(w_ref[2 * col], lo) + _dot(w_ref[2 * col + 1], hi) + bsx[:, col * 128:(col + 1) * 128])
    return gu, ln, xn, rstd, mixed


def _mix_fwd(q, kv, sinks, zu, zv, lng, lnb, wtril, bsx, h, ga, gg, gpost, wb, layer):
    t = q.shape[0]
    tq = _tile(t, 512)
    blocks = tq // BLK

    def body(sink_ref, q_ref, kvc_ref, kvp_ref, zu_ref, zv_ref, g_ref, b_ref, w_ref, bs_ref, h_ref, ga_ref, gg_ref,
             gp_ref, wo_ref, attn_ref, gm_ref, heads_ref, mix_ref, h1_ref):
        low, vcur, dist, valid = _attn_masks(pl.program_id(0))
        kp, vp = _kv_variants(kvp_ref, slice(None), low)

        def project(rows, mix):
            mix_ref[rows, :] = mix.astype(BF16)
            h1_ref[rows, :] = h_ref[rows, :] + _rms_fwd(mix, gp_ref[...])

        pending = None
        for b in range(blocks):
            rows = slice(b * BLK, (b + 1) * BLK)
            kc, vc = _kv_variants(kvc_ref, rows, low)
            ks = [_stack_kv(kp, kc, g) for g in range(2)]
            vs = [_stack_kv(vp, vc, g) for g in range(2)]
            scores = _attn_scores(q_ref, rows, ks, vcur)
            if pending is not None:
                project(pending[0], _dot(pending[1], wo_ref[...]))
            gu, _, _, _, mixed = _gm_forward_block(zu_ref[rows, :], zv_ref[rows, :], g_ref[...], b_ref[...], w_ref,
                                                   bs_ref[...], low)
            probs = [_attn_probs(scores[h], h, sink_ref[h], dist, valid if b == 0 else None)[0] for h in range(8)]
            attn_cols, gm_cols = [], []
            for col in range(4):
                gm_cols.append((gu[:, col * 128:(col + 1) * 128] * mixed[col]).astype(BF16))
                p_col = jnp.concatenate(_split_cols(probs[2 * col], vcur) + _split_cols(probs[2 * col + 1], vcur), axis=1)
                attn_cols.append(_dot(p_col, vs[col // 2]).astype(BF16))
            attn = jnp.concatenate(attn_cols, axis=1)
            gm = jnp.concatenate(gm_cols, axis=1)
            attn_ref[rows, :] = attn
            gm_ref[rows, :] = gm
            heads = jnp.concatenate([_rms_fwd(attn.astype(F32), ga_ref[...]).astype(BF16),
                                     _rms_fwd(gm.astype(F32), gg_ref[...]).astype(BF16)], axis=1)
            heads_ref[rows, :] = heads
            pending = (rows, heads)
            kp, vp = kc, vc
        project(pending[0], _dot(pending[1], wo_ref[...]))

    wide = _row_spec(tq, GM_W)
    row = _row_spec(tq, D_MODEL)
    return pl.pallas_call(
        body, name="mix_fwd", grid=(t // tq,),
        in_specs=[pl.BlockSpec(memory_space=pltpu.SMEM), _row_spec(tq, ATTN_W), _row_spec(tq, 2 * KV_W),
                  _kv_prev_spec(blocks), wide, wide, _vec_spec(GM_W), _vec_spec(GM_W),
                  pl.BlockSpec((8, BLK, BLK), lambda i: (0, 0, 0)), pl.BlockSpec((BLK, GM_W), lambda i: (0, 0)),
                  row, _vec_spec(ATTN_W), _vec_spec(GM_W), _vec_spec(D_MODEL), _seg_spec(ROWS_B, D_MODEL, 2 * layer)],
        out_specs=[_row_spec(tq, ATTN_W), wide, row, row, row],
        out_shape=[jax.ShapeDtypeStruct((t, ATTN_W), BF16), jax.ShapeDtypeStruct((t, GM_W), BF16),
                   jax.ShapeDtypeStruct((t, D_MODEL), BF16), jax.ShapeDtypeStruct((t, D_MODEL), BF16),
                   jax.ShapeDtypeStruct((t, D_MODEL), F32)],
        compiler_params=_params("parallel"),
    )(sinks, q, kv, kv, zu, zv, lng, lnb, wtril, bsx, h, ga, gg, gpost, wb)


def _ffn_fwd(h1, p, p_layer, gpre, gpost, gple, wa, wb, wp, layer, target=None):
    t = h1.shape[0]
    tm = _tile(t, 256)

    def body(h_ref, p_ref, gpre_ref, gpost_ref, gple_ref, wg_ref, wu_ref, wd_ref, wpg_ref, wpl_ref, *rest):
        if target is None:
            f_ref, gp_ref, up_ref, act_ref, fo_ref, h2_ref, hn_ref, gate_ref, h3_ref = rest
        else:
            t_ref, f_ref, gp_ref, up_ref, act_ref, fo_ref, h2_ref, hn_ref, gate_ref, dy_ref, l_ref = rest
            _zero_at(pl.program_id(0) == 0, l_ref)
        h = h_ref[...]
        pe = _dot_nt(p_ref[...].astype(BF16), wpl_ref[...])
        f = _rms_fwd(h, gpre_ref[...]).astype(BF16)
        f_ref[...] = f
        chunks = [slice(j * FF_CHUNK, (j + 1) * FF_CHUNK) for j in range(D_FF // FF_CHUNK)]
        fo = None
        gp, up = _dot_nt(f, wg_ref[chunks[0], :]), _dot_nt(f, wu_ref[chunks[0], :])
        for j, cols in enumerate(chunks):
            if j + 1 < len(chunks):
                gp_next, up_next = _dot_nt(f, wg_ref[chunks[j + 1], :]), _dot_nt(f, wu_ref[chunks[j + 1], :])
            act = (gp * _sigmoid(gp) * up).astype(BF16)
            gp_ref[:, cols] = gp.astype(BF16)
            up_ref[:, cols] = up.astype(BF16)
            act_ref[:, cols] = act
            part = _dot(act, wd_ref[cols, :])
            fo = part if fo is None else fo + part
            if j + 1 < len(chunks):
                gp, up = gp_next, up_next
        fo_ref[...] = fo
        h2 = h + _rms_fwd(fo, gpost_ref[...])
        h2_ref[...] = h2
        hn = _rms_fwd(h2, gple_ref[...]).astype(BF16)
        hn_ref[...] = hn
        gate = _sigmoid(_dot(hn, wpg_ref[...]))
        gate_ref[...] = gate.astype(BF16)
        h3 = h2 + pe * gate
        if target is None:
            h3_ref[...] = h3
        else:
            e = h3 - t_ref[...]
            dy_ref[...] = (e * (1.0 / D_MODEL)).astype(BF16)
            s = jnp.sum(jnp.sum(e * e, axis=1, keepdims=True), axis=0, keepdims=True)
            l_ref[...] += jnp.broadcast_to(s, (1, 128))

    wide = _row_spec(tm, D_FF)
    row = _row_spec(tm, D_MODEL)
    vec = _vec_spec(D_MODEL)
    last = target is not None
    return pl.pallas_call(
        body, name="ffn_loss" if last else "ffn_fwd", grid=(t // tm,),
        in_specs=[row, _layer_row_spec(tm, PLE_DIM, p_layer), vec, vec, vec, _seg_spec(ROWS_A, D_MODEL, 3 * layer),
                  _seg_spec(ROWS_A, D_MODEL, 3 * layer + 1), _seg_spec(ROWS_A, D_MODEL, 3 * layer + 2),
                  _seg_spec(ROWS_B, D_MODEL, 2 * layer + 1), _seg_spec(ROWS_B, PLE_DIM, layer)] + [row] * last,
        out_specs=[row, wide, wide, wide, row, row, row, row, row] + [_vec_spec(128)] * last,
        out_shape=[jax.ShapeDtypeStruct((t, D_MODEL), BF16)] + [jax.ShapeDtypeStruct((t, D_FF), BF16)] * 3
        + [jax.ShapeDtypeStruct((t, D_MODEL), F32)] * 2 + [jax.ShapeDtypeStruct((t, D_MODEL), BF16)] * 2
        + [jax.ShapeDtypeStruct((t, D_MODEL), BF16 if last else F32)] + [jax.ShapeDtypeStruct((1, 128), F32)] * last,
        compiler_params=_params("arbitrary" if last else "parallel"),
    )(h1, p, gpre, gpost, gple, wa, wa, wa, wb, wp, *([target] if last else []))


def _ffn_bwd(dh3, h2, gate, p, p_layer, fo, gp, up, h1, gple, gpost, gpre, wa, wb, wp, layer, after):
    t = dh3.shape[0]
    tm = _tile(t, 256)

    def body(d3_ref, h2_ref, gate_ref, p_ref, fo_ref, gp_ref, up_ref, h_ref, gple_ref, gpost_ref, gpre_ref,
             wg_ref, wu_ref, wd_ref, wpg_ref, wpl_ref, after_ref,
             dgl_ref, dpe_ref, dfo_ref, dgp_ref, dup_ref, dh1_ref, dgple_ref, dgpost_ref, dgpre_ref):
        _zero_at(pl.program_id(0) == 0, dgple_ref, dgpost_ref, dgpre_ref)
        d3 = d3_ref[...].astype(F32)
        gate = gate_ref[...].astype(F32)
        pe = _dot_nt(p_ref[...].astype(BF16), wpl_ref[...])
        dpe_ref[...] = (d3 * gate).astype(BF16)
        dgl = (d3 * pe * gate * (1.0 - gate)).astype(BF16)
        dgl_ref[...] = dgl
        dx2, dgple = _rms_bwd(h2_ref[...], gple_ref[...], _dot_nt(dgl, wpg_ref[...]))
        dgple_ref[...] += dgple
        d = d3 + dx2
        dfo, dgpost = _rms_bwd(fo_ref[...], gpost_ref[...], d)
        dfo = dfo.astype(BF16)
        dfo_ref[...] = dfo
        dgpost_ref[...] += dgpost
        chunks = [slice(j * FF_CHUNK, (j + 1) * FF_CHUNK) for j in range(D_FF // FF_CHUNK)]
        df = None
        dact = _dot_nt(dfo, wd_ref[chunks[0], :])
        for j, cols in enumerate(chunks):
            if j + 1 < len(chunks):
                dact_next = _dot_nt(dfo, wd_ref[chunks[j + 1], :])
            gp = gp_ref[:, cols].astype(F32)
            sg = _sigmoid(gp)
            dgp = (dact * up_ref[:, cols].astype(F32) * (sg * (1.0 + gp * (1.0 - sg)))).astype(BF16)
            dup = (dact * (gp * sg)).astype(BF16)
            dgp_ref[:, cols] = dgp
            dup_ref[:, cols] = dup
            part = _dot(dgp, wg_ref[cols, :]) + _dot(dup, wu_ref[cols, :])
            df = part if df is None else df + part
            if j + 1 < len(chunks):
                dact = dact_next
        dx, dgpre = _rms_bwd(h_ref[...], gpre_ref[...], df)
        dh1_ref[...] = (d + dx).astype(BF16)
        dgpre_ref[...] += dgpre

    wide = _row_spec(tm, D_FF)
    row = _row_spec(tm, D_MODEL)
    vec = _vec_spec(D_MODEL)
    narrow = jax.ShapeDtypeStruct((t, D_MODEL), BF16)
    return pl.pallas_call(
        body, name="ffn_bwd", grid=(t // tm,),
        in_specs=[row, row, row, _layer_row_spec(tm, PLE_DIM, p_layer), row, wide, wide, row, vec, vec, vec,
                  _seg_spec(ROWS_A, D_MODEL, 3 * layer), _seg_spec(ROWS_A, D_MODEL, 3 * layer + 1),
                  _seg_spec(ROWS_A, D_MODEL, 3 * layer + 2), _seg_spec(ROWS_B, D_MODEL, 2 * layer + 1),
                  _seg_spec(ROWS_B, PLE_DIM, layer), pl.BlockSpec(memory_space=pl.ANY)],
        out_specs=[row, row, row, wide, wide, row, vec, vec, vec],
        out_shape=[narrow, narrow, narrow, jax.ShapeDtypeStruct((t, D_FF), BF16), jax.ShapeDtypeStruct((t, D_FF), BF16),
                   narrow] + [jax.ShapeDtypeStruct((1, D_MODEL), F32)] * 3,
        compiler_params=_params("arbitrary"),
    )(dh3, h2, gate, p, fo, gp, up, h1, gple, gpost, gpre, wa, wa, wa, wb, wp, after)


def _out_proj_bwd(dh1, mix, attn, gm, gpost, ga, gg, wb, layer, after):
    t = dh1.shape[0]
    tm = _tile(t, 512)

    def body(d_ref, mix_ref, a_ref, m_ref, gp_ref, ga_ref, gg_ref, w_ref, after_ref,
             dmix_ref, da_ref, dm_ref, dgp_ref, dga_ref, dgg_ref):
        _zero_at(pl.program_id(0) == 0, dgp_ref, dga_ref, dgg_ref)
        dmix, dgp = _rms_bwd(mix_ref[...].astype(F32), gp_ref[...], d_ref[...].astype(F32))
        dmix = dmix.astype(BF16)
        dmix_ref[...] = dmix
        da, dga = _rms_bwd(a_ref[...].astype(F32), ga_ref[...], _dot_nt(dmix, w_ref[0:512, :]))
        dm, dgg = _rms_bwd(m_ref[...].astype(F32), gg_ref[...], _dot_nt(dmix, w_ref[512:1024, :]))
        da_ref[...] = da.astype(BF16)
        dm_ref[...] = dm
        dgp_ref[...] += dgp
        dga_ref[...] += dga
        dgg_ref[...] += dgg

    return pl.pallas_call(
        body, name="out_proj_bwd", grid=(t // tm,),
        in_specs=[_row_spec(tm, D_MODEL), _row_spec(tm, D_MODEL), _row_spec(tm, ATTN_W), _row_spec(tm, GM_W),
                  _vec_spec(D_MODEL), _vec_spec(ATTN_W), _vec_spec(GM_W), _seg_spec(ROWS_B, D_MODEL, 2 * layer),
                  pl.BlockSpec(memory_space=pl.ANY)],
        out_specs=[_row_spec(tm, D_MODEL), _row_spec(tm, ATTN_W), _row_spec(tm, GM_W),
                   _vec_spec(D_MODEL), _vec_spec(ATTN_W), _vec_spec(GM_W)],
        out_shape=[jax.ShapeDtypeStruct((t, D_MODEL), BF16), jax.ShapeDtypeStruct((t, ATTN_W), BF16),
                   jax.ShapeDtypeStruct((t, GM_W), F32), jax.ShapeDtypeStruct((1, D_MODEL), F32),
                   jax.ShapeDtypeStruct((1, ATTN_W), F32), jax.ShapeDtypeStruct((1, GM_W), F32)],
        compiler_params=_params("arbitrary"),
    )(dh1, mix, attn, gm, gpost, ga, gg, wb, after)


def _split3(x):
    hi = x.astype(BF16)
    r1 = x - hi.astype(F32)
    mid = r1.astype(BF16)
    lo = (r1 - mid.astype(F32)).astype(BF16)
    return hi, mid, lo


def _rider_pieces(rider, blocks):
    rm = rider[0].shape[1]
    per = -(-rm // (blocks * 256)) * 256
    return [slice(k * per, min((k + 1) * per, rm)) for k in range(blocks) if k * per < rm]


def _gm_bwd(dgm, zu, zv, lng, lnb, wtril, bsx, rider=None):
    t = zu.shape[0]
    tm = _tile(t, 512)
    nb = t // tm
    if rider is not None:
        ra, rb, rbuf, rseg = rider
        rm, rn = ra.shape[1], rb.shape[1]
        pieces = _rider_pieces(rider, tm // BLK)

    def body(d_ref, zu_ref, zv_ref, g_ref, b_ref, w_ref, bs_ref, *rest):
        if rider is None:
            dzu_ref, dzv_ref, dw_ref, dbs_ref, dlg_ref, dlb_ref, dbsx_ref = rest
        else:
            ra_ref, rb_ref, rbuf_ref, dzu_ref, dzv_ref, dw_ref, dbs_ref, dlg_ref, dlb_ref, ro_ref, dbsx_ref, acc_ref = rest
        i = pl.program_id(0)
        _zero_at(i == 0, dw_ref, dlg_ref, dlb_ref, dbsx_ref)
        if rider is not None:
            _zero_at(i == 0, acc_ref)
            rb16 = rb_ref[...].astype(BF16)
        row = lax.broadcasted_iota(jnp.int32, (BLK, BLK), 0)
        lane = lax.broadcasted_iota(jnp.int32, (BLK, BLK), 1)
        low = lane < 64
        tril = row >= lane
        lng = g_ref[...]
        for b in range(tm // BLK):
            rows = slice(b * BLK, (b + 1) * BLK)
            if rider is not None and b < len(pieces):
                acc_ref[pieces[b], :] += _dot_tn(ra_ref[:, pieces[b]], rb16)
            zu = zu_ref[rows, :]
            zv = zv_ref[rows, :]
            gu, ln, xn, rstd, mixed = _gm_forward_block(zu, zv, lng, b_ref[...], w_ref, bs_ref[...], low)
            dgm = d_ref[rows, :]
            dgu_cols, dmx_cols, dln_cols = [], [], []
            for col in range(4):
                sl = slice(col * 128, (col + 1) * 128)
                dg = dgm[:, sl]
                dgu_cols.append(dg * mixed[col])
                dmx = dg * gu[:, sl]
                dmx_cols.append(dmx)
                lc = ln[:, sl]
                halves = (jnp.where(low, lc, 0.0).astype(BF16), jnp.where(low, 0.0, lc).astype(BF16))
                dmx16 = dmx.astype(BF16)
                dmx_half = (jnp.where(low, dmx, 0.0).astype(BF16), jnp.where(low, 0.0, dmx).astype(BF16))
                dln = None
                for half in range(2):
                    hd = 2 * col + half
                    dw_ref[hd] += jnp.where(tril, _dot_nt(dmx16, halves[half]), 0.0)
                    part = _dot_tn(w_ref[hd], dmx_half[half])
                    dln = part if dln is None else dln + part
                dln_cols.append(dln)
            dgu = jnp.concatenate(dgu_cols, axis=1)
            dmx = jnp.concatenate(dmx_cols, axis=1)
            dln = jnp.concatenate(dln_cols, axis=1)
            dzu_ref[rows, :] = (dgu * _gelu_grad(zu)).astype(BF16)
            dbsx_ref[...] += dmx
            dlg_ref[...] += jnp.sum(dln * xn, axis=0, keepdims=True)
            dlb_ref[...] += jnp.sum(dln, axis=0, keepdims=True)
            dxn = dln * lng
            dgv = rstd * (dxn - jnp.mean(dxn, axis=-1, keepdims=True) - xn * jnp.mean(dxn * xn, axis=-1, keepdims=True))
            dzv_ref[rows, :] = (dgv * _gelu_grad(zv)).astype(BF16)

        @pl.when(i == nb - 1)
        def _():
            r = lax.broadcasted_iota(jnp.int32, (GM_W, BLK), 0)
            c = lax.broadcasted_iota(jnp.int32, (GM_W, BLK), 1)
            e = jnp.where(jnp.logical_and(r >= c * 64, r < c * 64 + 64), 1.0, 0.0).astype(BF16)
            hi, mid, lo = _split3(dbsx_ref[...])
            dbs_ref[...] = _dot(hi, e) + _dot(mid, e) + _dot(lo, e)
            if rider is not None:
                ro_ref[...] = acc_ref[...].astype(ro_ref.dtype)

    vec = _vec_spec(GM_W)
    in_specs = [_row_spec(tm, GM_W)] * 3 + [vec, vec, pl.BlockSpec((8, BLK, BLK), lambda i: (0, 0, 0)),
                                            pl.BlockSpec((BLK, GM_W), lambda i: (0, 0))]
    out_specs = [_row_spec(tm, GM_W), _row_spec(tm, GM_W), pl.BlockSpec((8, BLK, BLK), lambda i: (0, 0, 0)),
                 pl.BlockSpec((BLK, BLK), lambda i: (0, 0)), vec, vec]
    out_shape = [jax.ShapeDtypeStruct((t, GM_W), BF16), jax.ShapeDtypeStruct((t, GM_W), BF16),
                 jax.ShapeDtypeStruct((8, BLK, BLK), F32), jax.ShapeDtypeStruct((BLK, BLK), F32),
                 jax.ShapeDtypeStruct((1, GM_W), F32), jax.ShapeDtypeStruct((1, GM_W), F32)]
    scratch = [pltpu.VMEM((BLK, GM_W), F32)]
    operands = [dgm, zu, zv, lng, lnb, wtril, bsx]
    extra = {}
    if rider is not None:
        in_specs += [_row_spec(tm, rm), _row_spec(tm, rn), pl.BlockSpec(memory_space=pl.ANY)]
        out_specs.append(pl.BlockSpec((rm, rn), lambda i: (rseg, 0)))
        out_shape.append(jax.ShapeDtypeStruct(rbuf.shape, rbuf.dtype))
        scratch.append(pltpu.VMEM((rm, rn), F32))
        operands += [ra, rb, rbuf]
        extra = dict(input_output_aliases={9: 6})
    return pl.pallas_call(
        body, name="gm_bwd" if rider is None else "gm_bwd_rider", grid=(nb,),
        in_specs=in_specs, out_specs=out_specs, out_shape=out_shape, scratch_shapes=scratch,
        compiler_params=_params("arbitrary"), **extra,
    )(*operands)


def _attn_bwd(q, kv, do, sinks, after, rider=None):
    t = q.shape[0]
    tq = _tile(t, 512)
    blocks = tq // BLK
    steps = t // tq
    if rider is not None:
        ra, rb, rbuf, rseg = rider
        rm, rn = ra.shape[1], rb.shape[1]
        per = -(-rm // (blocks * 256)) * 256
        pieces = [slice(k * per, min((k + 1) * per, rm)) for k in range(blocks)]

    def body(sink_ref, q_ref, kvc_ref, kvp_ref, do_ref, after_ref, *rest):
        if rider is None:
            dq_ref, dkv_ref, dkf_ref, ds_ref = rest
        else:
            ra_ref, rb_ref, rbuf_ref, dq_ref, dkv_ref, dkf_ref, ds_ref, ro_ref, acc_ref = rest
        i = pl.program_id(0)
        _zero_at(i == 0, ds_ref)
        if rider is not None:
            _zero_at(i == 0, acc_ref)
            rb16 = rb_ref[...].astype(BF16)
        low, vcur, dist, valid = _attn_masks(i)
        head_row = lax.broadcasted_iota(jnp.int32, (8, 128), 0)
        dsink_tile = jnp.zeros((8, 128), F32)
        kp, vp = _kv_variants(kvp_ref, slice(None), low)
        own = None
        for b in range(blocks):
            rows = slice(b * BLK, (b + 1) * BLK)
            kc, vc = _kv_variants(kvc_ref, rows, low)
            scores = _attn_scores_by_head(q_ref, rows, kc, kp, vcur)
            dprobs = _attn_scores_by_head(do_ref, rows, vc, vp, vcur)
            if rider is not None and pieces[b].start < rm:
                acc_ref[pieces[b], :] += _dot_tn(ra_ref[:, pieces[b]], rb16)
            parts = []
            for h in range(8):
                p, ps = _attn_probs(scores[h], h, sink_ref[h], dist, valid if b == 0 else None)
                delta = jnp.sum(p * dprobs[h], axis=1, keepdims=True)
                ds = p * (dprobs[h] - delta) * ATTN_SCALE
                dsink = jnp.sum(-ps * delta, axis=0, keepdims=True)
                dsink_tile = jnp.where(head_row == h, dsink_tile + dsink, dsink_tile)
                parts.append(_split_cols(ds, vcur) + _split_cols(p, vcur))
            acc = {}

            def add(name, key, val):
                acc[(name, key)] = val if (name, key) not in acc else acc[(name, key)] + val

            for col in range(4):
                qh = q_ref[rows, col * 128:(col + 1) * 128]
                doh = do_ref[rows, col * 128:(col + 1) * 128]
                dq = None
                for half in range(2):
                    key = _head_key(2 * col + half)
                    dsp, dsc, pp, pc = parts[2 * col + half]
                    part = _dot(dsc, kc[key]) + _dot(dsp, kp[key])
                    dq = part if dq is None else dq + part
                    add("kc", key, _dot_tn(dsc, qh))
                    add("kp", key, _dot_tn(dsp, qh))
                    add("vc", key, _dot_tn(pc, doh))
                    add("vp", key, _dot_tn(pp, doh))
                dq_ref[rows, col * 128:(col + 1) * 128] = dq.astype(BF16)

            def place(name):
                head0 = acc[(name, (0, 0))] + pltpu.roll(acc[(name, (0, 1))], 64, axis=1)
                head1 = pltpu.roll(acc[(name, (1, 0))], 64, axis=1) + acc[(name, (1, 1))]
                return jnp.where(low, head0, head1)

            before = (place("kp"), place("vp"))
            if b == 0:
                dkf_ref[:, 0:128], dkf_ref[:, 128:256] = before
            else:
                last = slice((b - 1) * BLK, b * BLK)
                dkv_ref[last, 0:128] = own[0] + before[0]
                dkv_ref[last, 128:256] = own[1] + before[1]
            own = (place("kc"), place("vc"))
            kp, vp = kc, vc
        final = slice((blocks - 1) * BLK, blocks * BLK)
        dkv_ref[final, 0:128], dkv_ref[final, 128:256] = own
        ds_ref[...] += dsink_tile
        if rider is not None:
            @pl.when(i == steps - 1)
            def _():
                ro_ref[...] = acc_ref[...].astype(ro_ref.dtype)

    row_q = _row_spec(tq, ATTN_W)
    row_kv = _row_spec(tq, 2 * KV_W)
    in_specs = [pl.BlockSpec(memory_space=pltpu.SMEM), row_q, row_kv, _kv_prev_spec(blocks), row_q,
                pl.BlockSpec(memory_space=pl.ANY)]
    out_specs = [row_q, row_kv, _row_spec(BLK, 2 * KV_W), pl.BlockSpec((8, 128), lambda i: (0, 0))]
    out_shape = [jax.ShapeDtypeStruct((t, ATTN_W), BF16), jax.ShapeDtypeStruct((t, 2 * KV_W), F32),
                 jax.ShapeDtypeStruct((t // tq * BLK, 2 * KV_W), F32), jax.ShapeDtypeStruct((8, 128), F32)]
    operands = [sinks, q, kv, kv, do, after]
    extra = {}
    if rider is not None:
        in_specs += [_row_spec(tq, rm), _row_spec(tq, rn), pl.BlockSpec(memory_space=pl.ANY)]
        out_specs.append(pl.BlockSpec((rm, rn), lambda i: (rseg, 0)))
        out_shape.append(jax.ShapeDtypeStruct(rbuf.shape, rbuf.dtype))
        operands += [ra, rb, rbuf]
        extra = dict(scratch_shapes=[pltpu.VMEM((rm, rn), F32)], input_output_aliases={8: 4})
    return pl.pallas_call(
        body, name="attn_bwd" if rider is None else "attn_bwd_rider", grid=(steps,),
        in_specs=in_specs, out_specs=out_specs, out_shape=out_shape,
        compiler_params=_params("arbitrary"), **extra,
    )(*operands)


def _in_proj_bwd(dq, dkv, dkf, dzu, dzv, h, a, dres, g, wc, layer, dh_dtype):
    t = h.shape[0]
    tm = _tile(t, 512)
    steps = t // tm
    pieces = ((0, 512), (512, 768), (768, 1280), (1280, 1792))

    def body(dq_ref, dkv_ref, dkn_ref, dzu_ref, dzv_ref, h_ref, a_ref, d_ref, g_ref, w_ref,
             dh_ref, dg_ref, gw_ref, acc_ref):
        i = pl.program_id(0)
        _zero_at(i == 0, dg_ref, acc_ref)
        tail = dkv_ref[tm - BLK:tm, :] + jnp.where(i < steps - 1, dkn_ref[...], 0.0)
        dkv = tail if tm == BLK else jnp.concatenate([dkv_ref[0:tm - BLK, :], tail], axis=0)
        dz = (dq_ref[...], dkv.astype(BF16), dzu_ref[...], dzv_ref[...])
        da = None
        for part, (lo, hi) in zip(dz, pieces):
            term = _dot(part, w_ref[lo:hi, :])
            da = term if da is None else da + term
        a16 = a_ref[...]
        for part, (lo, hi) in zip(dz, pieces):
            acc_ref[lo:hi, :] += _dot_tn(part, a16)
        dx, dg = _rms_bwd(h_ref[...], g_ref[...], da)
        dh_ref[...] = (d_ref[...].astype(F32) + dx).astype(dh_dtype)
        dg_ref[...] += dg

        @pl.when(i == steps - 1)
        def _():
            gw_ref[...] = acc_ref[...].astype(BF16)

    return pl.pallas_call(
        body, name="in_proj_bwd", grid=(t // tm,),
        in_specs=[_row_spec(tm, ATTN_W), _row_spec(tm, 2 * KV_W),
                  pl.BlockSpec((BLK, 2 * KV_W), lambda i: (jnp.minimum(i + 1, steps - 1), 0)), _row_spec(tm, GM_W),
                  _row_spec(tm, GM_W), _row_spec(tm, D_MODEL), _row_spec(tm, D_MODEL), _row_spec(tm, D_MODEL),
                  _vec_spec(D_MODEL), _seg_spec(ROWS_C, D_MODEL, layer)],
        out_specs=[_row_spec(tm, D_MODEL), _vec_spec(D_MODEL), pl.BlockSpec((D_IN, D_MODEL), lambda i: (0, 0))],
        out_shape=[jax.ShapeDtypeStruct((t, D_MODEL), dh_dtype), jax.ShapeDtypeStruct((1, D_MODEL), F32),
                   jax.ShapeDtypeStruct((D_IN, D_MODEL), BF16)],
        scratch_shapes=[pltpu.VMEM((D_IN, D_MODEL), F32)],
        compiler_params=_params("arbitrary"),
    )(dq, dkv, dkf, dzu, dzv, h, a, dres, g, wc)


def _weight_grad(a, b, buf, seg, b_layer=None):
    t, m = a.shape
    n = b.shape[-1]
    assert buf.shape[0] % m == 0 and buf.shape[1] == n
    tm = _tile(t, WGRAD_TOKENS)
    steps = t // tm
    half = m // 2

    def body(a_ref, b_ref, buf_ref, o_ref, acc_ref):
        i = pl.program_id(0)
        _zero_at(i == 0, acc_ref)
        b16 = b_ref[...].astype(BF16)
        for rows in (slice(0, half), slice(half, m)):
            acc_ref[rows, :] += _dot_tn(a_ref[:, rows], b16)

        @pl.when(i == steps - 1)
        def _():
            o_ref[...] = acc_ref[...].astype(o_ref.dtype)

    return pl.pallas_call(
        body, name="weight_grad", grid=(steps,),
        in_specs=[_row_spec(tm, m), _row_spec(tm, n) if b_layer is None else _layer_row_spec(tm, n, b_layer),
                  pl.BlockSpec(memory_space=pl.ANY)],
        out_specs=pl.BlockSpec((m, n), lambda i: (seg, 0)),
        out_shape=jax.ShapeDtypeStruct(buf.shape, buf.dtype),
        scratch_shapes=[pltpu.VMEM((m, n), F32)],
        input_output_aliases={2: 0},
        compiler_params=_params("arbitrary"),
    )(a, b, buf)


def _rows8(rows):
    return [jnp.pad(r, ((0, 7), (0, 0))) for r in rows]


def _small_pack_gating(d):
    rows = [jnp.concatenate([d["gm_ln_g"], d["gm_ln_b"]], axis=1), d["gm_bs"].reshape(1, 1024)]
    return jnp.concatenate(_rows8(rows) + [d["gm_ws"].reshape(128, 1024)], axis=0)


def _small_pack_rest(d):
    rows = [d["ln_mix_pre"], d["ln_mix_post"], d["ln_ffn_pre"], d["ln_ffn_post"], d["ln_ple_gate"],
            jnp.concatenate([d["g_attn_out"], d["g_gm_out"]], axis=1),
            jnp.pad(d["attn_sinks"].reshape(1, 8), ((0, 0), (0, 1016)))]
    return jnp.concatenate(_rows8(rows), axis=0)


def _unpack_gating(g):
    return {"gm_ln_g": g[:, 0, :512], "gm_ln_b": g[:, 0, 512:], "gm_bs": g[:, 8].reshape(DEPTH, 8, 128),
            "gm_ws": g[:, 16:GATING_ROWS].reshape(DEPTH, 8, 128, 128)}


def _unpack_rest(s):
    return {"ln_mix_pre": s[:, 0], "ln_mix_post": s[:, 8], "ln_ffn_pre": s[:, 16], "ln_ffn_post": s[:, 24],
            "ln_ple_gate": s[:, 32], "g_attn_out": s[:, 40, :512], "g_gm_out": s[:, 40, 512:], "attn_sinks": s[:, 48, :8]}


def _row(v):
    return v.reshape(1, -1)


def _layer_fwd(h, p, sp, l, weights, target=None):
    tril = jnp.tril(jnp.ones((BLK, BLK), bool))
    wtril = jnp.where(tril[None], sp["gm_ws"][l], 0.0).astype(BF16)
    bsx = jnp.repeat(sp["gm_bs"][l].T, HEAD_DIM, axis=1)
    a, q, kv, zu, zv = _in_proj(h, _row(sp["ln_mix_pre"][l]), weights("c", h), 0)
    wb = weights("b", zu)
    attn, gm, heads, mix, h1 = _mix_fwd(
        q, kv, sp["attn_sinks"][l], zu, zv, _row(sp["gm_ln_g"][l]), _row(sp["gm_ln_b"][l]), wtril, bsx, h,
        _row(sp["g_attn_out"][l]), _row(sp["g_gm_out"][l]), _row(sp["ln_mix_post"][l]), wb, 0)
    wa = weights("a", h1)
    f, gpre, up, act, fo, h2, hn, gate, *out = _ffn_fwd(
        h1, p, l, _row(sp["ln_ffn_pre"][l]), _row(sp["ln_ffn_post"][l]), _row(sp["ln_ple_gate"][l]), wa, wb,
        weights("p", zu), 0, target)
    saved = dict(h=h, a=a, q=q, kv=kv, zu=zu, zv=zv, attn=attn, gm=gm, heads=heads, mix=mix, h1=h1, f=f,
                 gpre=gpre, up=up, act=act, fo=fo, h2=h2, hn=hn, gate=gate, wtril=wtril, bsx=bsx)
    return (out[0] if target is None else tuple(out)), saved


def _layer_bwd_upper(dh, s, p, sp, l, wa, wb, wp, after, ride):
    d = {}
    dgl, dpe, dfo, dgp, dup, dh1, d["ln_ple_gate"], d["ln_ffn_post"], d["ln_ffn_pre"] = _ffn_bwd(
        dh, s["h2"], s["gate"], p, l, s["fo"], s["gpre"], s["up"], s["h1"], _row(sp["ln_ple_gate"][l]),
        _row(sp["ln_ffn_post"][l]), _row(sp["ln_ffn_pre"][l]), wa, wb, wp, 0, after)
    gb = _weight_grad(s["hn"], dgl, lax.empty((2 * D_MODEL, D_MODEL), BF16), 1)
    gp = _weight_grad(dpe, p, lax.empty((D_MODEL, PLE_DIM), BF16), 0, b_layer=l)
    ga = _weight_grad(s["act"], dfo, lax.empty((3 * D_FF, D_MODEL), BF16), 2)
    if not ride:
        ga = _weight_grad(dgp, s["f"], ga, 0)
        ga = _weight_grad(dup, s["f"], ga, 1)
    return (dh1, d, dgp, dup), ga, gp, gb


def _layer_bwd_middle(carry, s, sp, l, wb, gb, ga, after, ride):
    dh1, d, dgp, dup = carry
    dmix, dattn, dgm, d["ln_mix_post"], d["g_attn_out"], d["g_gm_out"] = _out_proj_bwd(
        dh1, s["mix"], s["attn"], s["gm"], _row(sp["ln_mix_post"][l]), _row(sp["g_attn_out"][l]),
        _row(sp["g_gm_out"][l]), wb, 0, after)
    gb = _weight_grad(s["heads"], dmix, gb, 0)
    dzu, dzv, d["gm_ws"], dbs, d["gm_ln_g"], d["gm_ln_b"], *rode = _gm_bwd(
        dgm, s["zu"], s["zv"], _row(sp["gm_ln_g"][l]), _row(sp["gm_ln_b"][l]), s["wtril"], s["bsx"],
        (dup, s["f"], ga, 1) if ride else None)
    d["gm_bs"] = dbs[:, :8].T
    return (dh1, dattn, dzu, dzv, d, dgp), gb, (rode[0] if ride else ga), _small_pack_gating(d)


def _layer_bwd_lower(carry, s, sp, l, wc, ga, after, ride):
    dh1, dattn, dzu, dzv, d, dgp = carry
    dq, dkv, dkf, dsink, *rode = _attn_bwd(s["q"], s["kv"], dattn, sp["attn_sinks"][l], after,
                                            (dgp, s["f"], ga, 0) if ride else None)
    ga = rode[0] if ride else ga
    d["attn_sinks"] = dsink[:, 0]
    dh, d["ln_mix_pre"], gc = _in_proj_bwd(dq, dkv, dkf, dzu, dzv, s["h"], s["a"], dh1, _row(sp["ln_mix_pre"][l]), wc,
                                           0, F32 if l == 0 else BF16)
    return dh, gc, ga, _small_pack_rest(d)


ANY = pl.BlockSpec(memory_space=pl.ANY)


HBM = pl.BlockSpec(memory_space=pltpu.HBM)
SEM = pl.BlockSpec(memory_space=pltpu.SEMAPHORE)
N_PEERS = N_DEV - 1


def _peers():
    x, y, c = lax.axis_index("x"), lax.axis_index("y"), lax.axis_index("c")
    peers = []
    for r in range(1, N_DEV):
        px = 1 - x if r & 4 else x
        py = 1 - y if r & 2 else y
        pc = 1 - c if r & 1 else c
        peers.append(((px, py, pc), 4 * px + 2 * py + pc))
    return 4 * x + 2 * y + c, peers


GATHER, SCATTER, SPREAD = "gather", "scatter", "spread"


def _peer_copy(src, land, send_sems, recv_sems, r, me, peer, peer_slot, mode):
    return pltpu.make_async_remote_copy(
        src_ref=src.at[:, pl.ds(peer_slot, 1)] if mode == SCATTER else src,
        dst_ref=land.at[:, pl.ds(me, 1)] if mode == GATHER else land.at[:, pl.ds(r - 1, 1)],
        send_sem=send_sems.at[r - 1], recv_sem=recv_sems.at[r - 1], device_id=peer, device_id_type=MESH)


def _peer_arrival(src, land, send_sems, recv_sems, r, me, peer, peer_slot, mode):
    return pltpu.make_async_remote_copy(
        src_ref=src.at[:, pl.ds(me, 1)] if mode == SCATTER else src,
        dst_ref=land.at[:, pl.ds(peer_slot, 1)] if mode == GATHER else land.at[:, pl.ds(r - 1, 1)],
        send_sem=send_sems.at[r - 1], recv_sem=recv_sems.at[r - 1], device_id=peer, device_id_type=MESH)


def _send_start(name, srcs, lands, modes):
    n = len(srcs)

    def body(*refs):
        src_refs, land_refs = refs[:n], refs[n:2 * n]
        outs = refs[2 * n:]
        send_sems, recv_sems, token = outs[2 * n:3 * n], outs[3 * n:4 * n], outs[4 * n]
        me, peers = _peers()
        for k in range(n):
            for r, (peer, slot) in enumerate(peers, 1):
                _peer_copy(src_refs[k], land_refs[k], send_sems[k], recv_sems[k], r, me, peer, slot, modes[k]).start()
        token[...] = jnp.zeros_like(token)

    hbm = lambda a: pltpu.HBM(a.shape, a.dtype)
    sems = [pltpu.SemaphoreType.DMA((N_PEERS,))] * (2 * n)
    outs = pl.pallas_call(
        body, name=name, in_specs=[HBM] * (2 * n),
        out_specs=[HBM] * (2 * n) + [SEM] * (2 * n) + [pl.BlockSpec(memory_space=pltpu.VMEM)],
        out_shape=[hbm(a) for a in srcs] + [hbm(a) for a in lands] + sems + [jax.ShapeDtypeStruct((8, 128), F32)],
        input_output_aliases={k: k for k in range(2 * n)},
        compiler_params=pltpu.CompilerParams(has_side_effects=pltpu.SideEffectType.DATAFLOW_SIDE_EFFECTING),
    )(*[pltpu.with_memory_space_constraint(a, pltpu.HBM) for a in list(srcs) + list(lands)])
    return dict(srcs=outs[:n], lands=outs[n:2 * n], send=outs[2 * n:3 * n], recv=outs[3 * n:4 * n],
                modes=list(modes)), outs[4 * n]


def _send_wait(name, sent, ks, after):
    n = len(ks)
    srcs = [sent["srcs"][k] for k in ks]
    lands = [sent["lands"][k] for k in ks]
    modes = [sent["modes"][k] for k in ks]

    def body(*refs):
        src_refs, land_refs = refs[:n], refs[n:2 * n]
        send_sems, recv_sems = refs[2 * n:3 * n], refs[3 * n:4 * n]
        me, peers = _peers()
        for k in range(n):
            for r, (peer, slot) in enumerate(peers, 1):
                args = (src_refs[k], land_refs[k], send_sems[k], recv_sems[k], r, me, peer, slot, modes[k])
                _peer_copy(*args).wait_send()
                _peer_arrival(*args).wait_recv()

    hbm = lambda a: pltpu.HBM(a.shape, a.dtype)
    outs = pl.pallas_call(
        body, name=name, in_specs=[HBM] * (2 * n) + [SEM] * (2 * n) + [ANY],
        out_specs=[HBM] * (2 * n), out_shape=[hbm(a) for a in srcs] + [hbm(a) for a in lands],
        input_output_aliases={k: k for k in range(2 * n)},
        compiler_params=pltpu.CompilerParams(has_side_effects=pltpu.SideEffectType.DATAFLOW_SIDE_EFFECTING),
    )(*srcs, *lands, *[sent["send"][k] for k in ks], *[sent["recv"][k] for k in ks], after)
    return outs[n:], outs[:n]


def _sum_blocks(own, land, ids, stacked, layer):
    nseg, _, rows, cols = land.shape
    if stacked is None:
        stacked = lax.empty((nseg, DEPTH, rows, cols), F32)

    def body(ids_ref, own_ref, land_ref, stacked_ref, o_ref):
        me = ids_ref[1]
        total = None
        for j in range(N_DEV):
            slot = jnp.maximum(jnp.bitwise_xor(me, j) - 1, 0)
            term = jnp.where(me == j, own_ref[...], land_ref[slot]).astype(F32)
            total = term if total is None else total + term
        o_ref[...] = total

    return pl.pallas_call(
        body, name="sum_blocks",
        grid_spec=pltpu.PrefetchScalarGridSpec(
            num_scalar_prefetch=1, grid=(nseg,),
            in_specs=[pl.BlockSpec((None, None, rows, cols), lambda s, ids: (s, ids[0], 0, 0)),
                      pl.BlockSpec((None, N_PEERS, rows, cols), lambda s, ids: (s, 0, 0, 0)),
                      pl.BlockSpec(memory_space=pl.ANY)],
            out_specs=pl.BlockSpec((None, None, rows, cols), lambda s, ids: (s, layer, 0, 0))),
        out_shape=jax.ShapeDtypeStruct(stacked.shape, F32),
        input_output_aliases={3: 0},
        compiler_params=_params("parallel"),
    )(ids, own, land, stacked)


def _adamw(w, g, m, v):
    shape = w.shape
    cols = shape[-1]
    rows = w.size // cols
    tr = rows
    for cand in (512, 448, 352, 256, 128, 64, 32, 16, 8):
        if rows % cand == 0:
            tr = cand
            break
    c1 = 1.0 / (1.0 - ADAM_B1 ** ADAM_STEP)
    c2 = 1.0 / (1.0 - ADAM_B2 ** ADAM_STEP)

    def body(w_ref, g_ref, m_ref, v_ref, d_ref, nm_ref, nv_ref):
        g = g_ref[...]
        m = ADAM_B1 * m_ref[...] + (1.0 - ADAM_B1) * g
        v = ADAM_B2 * v_ref[...] + (1.0 - ADAM_B2) * (g * g)
        nm_ref[...] = m
        nv_ref[...] = v
        d_ref[...] = -ADAM_LR * ((m * c1) / (jnp.sqrt(v * c2) + ADAM_EPS) + ADAM_WD * w_ref[...])

    steps = rows // tr
    if steps >= 3:
        def ring(w_hbm, g_hbm, m_hbm, v_hbm, d_hbm, nm_hbm, nv_hbm, ibuf, obuf, isem, osem):
            ins, outs_ = (w_hbm, g_hbm, m_hbm, v_hbm), (d_hbm, nm_hbm, nv_hbm)

            def read(k, s, slot):
                return pltpu.make_async_copy(ins[k].at[pl.ds(s * tr, tr)], ibuf.at[k, slot], isem.at[k, slot])

            def write(k, s, slot):
                return pltpu.make_async_copy(obuf.at[k, slot], outs_[k].at[pl.ds(s * tr, tr)], osem.at[k, slot])

            for s in range(2):
                for k in range(4):
                    read(k, s, s).start()

            def step(s, carry):
                slot, oslot = s % 3, s % 2

                @pl.when(s + 2 < steps)
                def _():
                    for k in range(4):
                        read(k, s + 2, (s + 2) % 3).start()

                for k in range(4):
                    read(k, s, slot).wait()

                @pl.when(s >= 2)
                def _():
                    for k in range(3):
                        write(k, s - 2, oslot).wait()

                body(*[ibuf.at[k, slot] for k in range(4)], *[obuf.at[k, oslot] for k in range(3)])
                for k in range(3):
                    write(k, s, oslot).start()
                return carry

            lax.fori_loop(0, steps, step, 0)
            for s in range(steps - 2, steps):
                for k in range(3):
                    write(k, s, s % 2).wait()

        outs = pl.pallas_call(
            ring, name="adamw_ring", in_specs=[ANY] * 4, out_specs=[ANY] * 3,
            out_shape=[jax.ShapeDtypeStruct((rows, cols), F32)] * 3,
            scratch_shapes=[pltpu.VMEM((4, 3, tr, cols), F32), pltpu.VMEM((3, 2, tr, cols), F32),
                            pltpu.SemaphoreType.DMA((4, 3)), pltpu.SemaphoreType.DMA((3, 2))],
            compiler_params=_params(),
        )(*[a.reshape(rows, cols) for a in (w, g, m, v)])
        return [o.reshape(shape) for o in outs]
    spec = pl.BlockSpec((tr, cols), lambda i: (i, 0))
    outs = pl.pallas_call(
        body, name="adamw", grid=(steps,),
        in_specs=[spec] * 4, out_specs=[spec] * 3,
        out_shape=[jax.ShapeDtypeStruct((rows, cols), F32)] * 3,
        compiler_params=_params("parallel"),
    )(*[a.reshape(rows, cols) for a in (w, g, m, v)])
    return [o.reshape(shape) for o in outs]


SMALL = ("ln_mix_pre", "attn_sinks", "gm_ln_g", "gm_ln_b", "gm_ws", "gm_bs", "g_attn_out", "g_gm_out",
         "ln_mix_post", "ln_ffn_pre", "ln_ffn_post", "ln_ple_gate")
WEIGHTS = ("ln_mix_pre", "w_in", "attn_sinks", "gm_ln_g", "gm_ln_b", "gm_ws", "gm_bs", "g_attn_out", "g_gm_out",
           "w_out", "ln_mix_post", "ln_ffn_pre", "w_ffn_gate", "w_ffn_up", "w_ffn_down", "ln_ffn_post", "w_ple",
           "ln_ple_gate", "w_ple_gate")


def _pack_shards(w, l):
    sa = jnp.stack([w["w_ffn_gate"][l].T, w["w_ffn_up"][l].T, w["w_ffn_down"][l]])[:, None]
    sb = jnp.stack([w["w_out"][l], w["w_ple_gate"][l]])[:, None]
    return [w["w_in"][l].T[None, None].astype(BF16), sb.astype(BF16), w["w_ple"][l].T[None, None].astype(BF16),
            sa.astype(BF16)]


def kernel(x, p, ln_mix_pre, w_in, attn_sinks, gm_ln_g, gm_ln_b, gm_ws, gm_bs, g_attn_out, g_gm_out, w_out, ln_mix_post, ln_ffn_pre, w_ffn_gate, w_ffn_up, w_ffn_down, ln_ffn_post, w_ple, ln_ple_gate, w_ple_gate, loss_target, m_ln_mix_pre, m_w_in, m_attn_sinks, m_gm_ln_g, m_gm_ln_b, m_gm_ws, m_gm_bs, m_g_attn_out, m_g_gm_out, m_w_out, m_ln_mix_post, m_ln_ffn_pre, m_w_ffn_gate, m_w_ffn_up, m_w_ffn_down, m_ln_ffn_post, m_w_ple, m_ln_ple_gate, m_w_ple_gate, v_ln_mix_pre, v_w_in, v_attn_sinks, v_gm_ln_g, v_gm_ln_b, v_gm_ws, v_gm_bs, v_g_attn_out, v_g_gm_out, v_w_out, v_ln_mix_post, v_ln_ffn_pre, v_w_ffn_gate, v_w_ffn_up, v_w_ffn_down, v_ln_ffn_post, v_w_ple, v_ln_ple_gate, v_w_ple_gate):
    given = dict(locals())
    w = {n: given[n] for n in WEIGHTS}
    sp = {n: w[n] for n in SMALL}
    kinds = ("c", "b", "p", "a")

    me, _ = _peers()
    shards = [s for l in range(DEPTH) for s in _pack_shards(w, l)]
    lands = [lax.dynamic_update_slice(lax.empty((s.shape[0], N_DEV) + s.shape[2:], BF16), s, (0, me, 0, 0))
             for s in shards]
    gather, token = _send_start("gather_start", shards, lands, [GATHER] * len(shards))
    layer_weights = [{} for _ in range(DEPTH)]

    def weights_of(l):
        def get(kind, after):
            have = layer_weights[l]
            if kind not in have:
                if l < 2:
                    group = {"c": ("c",), "b": ("b", "p"), "p": ("b", "p"), "a": ("a",)}[kind]
                    after = token if (l == 0 and kind == "c") else after
                else:
                    group = kinds
                got, _ = _send_wait(f"gather_wait_{l}{group[0]}", gather, [4 * l + kinds.index(k) for k in group], after)
                for k, g in zip(group, got):
                    have[k] = g.reshape(-1, g.shape[-1])
            return have[kind]
        return get

    h = x[0]
    p3 = p.reshape(DEPTH, -1, PLE_DIM)
    saved = []
    for l in range(DEPTH):
        h, s = _layer_fwd(h, p3, sp, l, weights_of(l), loss_target[0] if l == DEPTH - 1 else None)
        saved.append(s)
    dh, sq = h

    started = []
    after = token
    view = lambda g, rows: g.reshape(-1, N_DEV, rows, g.shape[-1])
    pack16 = lambda s: s.astype(BF16)[None, None]
    packs = ("gating", "rest")

    def send(l, tag, items):
        bufs = list(items.values())
        lands = [lax.empty((a.shape[0], N_PEERS) + a.shape[2:], BF16) for a in bufs]
        sent, tok = _send_start(f"reduce_start_{l}{tag}", bufs, lands, [SPREAD if k in packs else SCATTER for k in items])
        started.append((l, tag, sent, list(items)))
        return tok

    for l in reversed(range(DEPTH)):
        lw = layer_weights[l]
        ride = l > 0
        carry, ga, gp, gb = _layer_bwd_upper(dh, saved[l], p3, sp, l, lw["a"], lw["b"], lw["p"], after, ride)
        if not ride:
            after = send(l, "a", {"a": view(ga, ROWS_A)})
        carry, gb, ga, gating = _layer_bwd_middle(carry, saved[l], sp, l, lw["b"], gb, ga, after, ride)
        after = send(l, "b", {"b": view(gb, ROWS_B), "gating": pack16(gating), "p": view(gp, ROWS_B)})
        dh, gc, ga, rest = _layer_bwd_lower(carry, saved[l], sp, l, lw["c"], ga, after, ride)
        after = send(l, "c", {"c": view(gc, ROWS_C), "rest": pack16(rest), **({"a": view(ga, ROWS_A)} if ride else {})})

    mine = jnp.stack([me, me]).astype(jnp.int32)
    whole = jnp.stack([jnp.zeros_like(me), me]).astype(jnp.int32)
    sums = {}

    def collect(group, behind):
        l, tag, sent, keys = group
        lands, srcs = _send_wait(f"reduce_wait_{l}{tag}", sent, list(range(len(keys))), behind)
        for key, land, src in zip(keys, lands, srcs):
            sums[key] = _sum_blocks(src, land, whole if key in packs else mine, sums.get(key), l)

    for group in started[:-1]:
        collect(group, after)
    grad_x = dh
    loss = lax.psum(sq[0, 0] * (0.5 / D_MODEL), AXES)
    grads, delta, new_m, new_v = {}, {}, {}, {}
    row_form = {}
    flip = lambda a: jnp.swapaxes(a, 1, 2)

    def update(names):
        for n in names:
            if n in row_form:
                grads[n] = flip(row_form[n])
                outs = _adamw(flip(w[n]), row_form[n], flip(given["m_" + n]), flip(given["v_" + n]))
                delta[n], new_m[n], new_v[n] = [flip(o) for o in outs]
            else:
                delta[n], new_m[n], new_v[n] = _adamw(w[n], grads[n], given["m_" + n], given["v_" + n])

    row_form.update({"w_ffn_gate": sums["a"][0], "w_ffn_up": sums["a"][1]})
    grads.update({"w_ffn_down": sums["a"][2], "w_ple": flip(sums["p"][0]), "w_out": sums["b"][0], "w_ple_gate": sums["b"][1]})
    grads.update(_unpack_gating(sums["gating"][0]))
    early = tuple(row_form) + tuple(grads)
    update(early)
    collect(started[-1], jnp.concatenate([delta[n][(0,) * delta[n].ndim].reshape(1) for n in early]))
    row_form["w_in"] = sums["c"][0]
    late = _unpack_rest(sums["rest"][0])
    grads.update(late)
    update(["w_in", *late])
    return (loss, grad_x[None], *[grads[n] for n in WEIGHTS], *[delta[n] for n in WEIGHTS],
            *[new_m[n] for n in WEIGHTS], *[new_v[n] for n in WEIGHTS])
```
